```python
import jax
import jax.numpy as jnp
from jax import lax
import numpy as np

D_MODEL = 1024
BATCH = 16
SEQ = 256
DEPTH = 2
DEC_BATCH = 4
DEC_SEQ = 2048
PAST_LEN = 512

GRID_W = 64
CHUNK = 128
ROW_BLOCK = CHUNK // GRID_W
SGU_WIDTH = D_MODEL // 2
SGU_GROUPS = 4
SGU_GROUP_DIM = SGU_WIDTH // SGU_GROUPS
LRU_WIDTH = D_MODEL // 2
LRU_BLOCKS = 8
LRU_BLOCK_DIM = LRU_WIDTH // LRU_BLOCKS
CONV_WIDTH = 4
LRU_C = 8.0
NA_HEADS = 8
NA_HEAD_DIM = 64
NA_WIDTH = NA_HEADS * NA_HEAD_DIM
WIN_ROWS = 8
WIN_COLS = 16
N_BRANCH = 3
N_GROUPS = 4
EXPERTS_PER_GROUP = 4
N_EXPERTS = N_GROUPS * EXPERTS_PER_GROUP
TOP_K = 2
D_EXPERT = D_MODEL // 2
IN_WIDTH = 2 * SGU_WIDTH + 2 * LRU_WIDTH + 3 * NA_WIDTH + N_BRANCH * D_MODEL
EPS = 1e-6
NEG_INF = -1e30
F32 = jnp.float32

kernel_name = 'hybrid_dit_ctx_prefix_step'


def rmsnorm(x, gain):
    xf = x.astype(F32)
    y = xf * lax.rsqrt(jnp.mean(xf * xf, axis=-1, keepdims=True) + EPS)
    return (y * gain.astype(F32)).astype(x.dtype)


def spatial_gating(u, v, gain, w_s, b_s):
    B, T, _ = u.shape
    u = jax.nn.gelu(u)
    v = rmsnorm(jax.nn.gelu(v), gain)
    vc = v.reshape(B, T // CHUNK, CHUNK, SGU_GROUPS, SGU_GROUP_DIM)
    mixed = jnp.einsum('gpq,bnqgc->bnpgc', w_s, vc) + b_s.T[None, None, :, :, None]
    return u * mixed.reshape(B, T, SGU_WIDTH)


def depthwise_conv(x, w, b):
    pad_l = (CONV_WIDTH - 1) // 2
    y = lax.conv_general_dilated(x, w[:, None, :], window_strides=(1,),
                                 padding=[(pad_l, CONV_WIDTH - 1 - pad_l)],
                                 dimension_numbers=('NWC', 'WIO', 'NWC'),
                                 feature_group_count=x.shape[-1])
    return y + b


def block_diag_linear(x, w, b):
    B, T, _ = x.shape
    xb = x.reshape(B, T, LRU_BLOCKS, LRU_BLOCK_DIM)
    return jnp.einsum('bthi,hij->bthj', xb, w).reshape(B, T, LRU_WIDTH) + b


def linear_scan(a, b, h0, reverse):
    def combine(e1, e2):
        a1, b1 = e1
        a2, b2 = e2
        return a1 * a2, a2 * b1 + b2
    a_cum, b_cum = lax.associative_scan(combine, (a, b), reverse=reverse, axis=1)
    return a_cum * h0[:, None, :] + b_cum


def rglru_bidirectional(x, w_r, b_r, w_i, b_i, lam, h0):
    xf = x.astype(F32)
    hs = []
    for d, rev in ((0, False), (1, True)):
        r = jax.nn.sigmoid(block_diag_linear(x, w_r[d], b_r[d]).astype(F32))
        i = jax.nn.sigmoid(block_diag_linear(x, w_i[d], b_i[d]).astype(F32))
        log_a = -LRU_C * r * jax.nn.softplus(-lam[d].astype(F32))
        a = jnp.exp(log_a)
        b = jnp.sqrt(-jnp.expm1(2.0 * log_a)) * (i * xf)
        hs.append(linear_scan(a, b, h0[:, d], rev))
    return hs[0], hs[1]


def context_attention(q, k, v):
    B, T, H, Dh = q.shape
    scale = Dh ** -0.5
    q_blocks = jnp.moveaxis(q.reshape(B, T // CHUNK, CHUNK, H, Dh), 1, 0)

    def one_block(qb):
        s = jnp.einsum('bqhd,bkhd->bhqk', qb, k).astype(F32) * scale
        pr = jax.nn.softmax(s, axis=-1).astype(v.dtype)
        return jnp.einsum('bhqk,bkhd->bqhd', pr, v)

    o = lax.map(one_block, q_blocks)
    return jnp.moveaxis(o, 0, 1).reshape(B, T, H * Dh)


def neighbourhood_attention(q, k, v, k_ctx, v_ctx, rpb):
    B, T, H, Dh = q.shape
    R = T // GRID_W
    wr = min(WIN_ROWS, R)
    n_blk = R // ROW_BLOCK
    scale = Dh ** -0.5
    rows = np.arange(R)
    row_idx = np.clip(rows - wr // 2, 0, R - wr)[:, None] + np.arange(wr)[None, :]
    cols = np.arange(GRID_W)
    col_start = np.clip(cols - WIN_COLS // 2, 0, GRID_W - WIN_COLS)
    col_ok = (cols[None, :] >= col_start[:, None]) & (cols[None, :] < col_start[:, None] + WIN_COLS)
    d_row = row_idx - rows[:, None]
    d_col = np.clip(cols[None, :] - cols[:, None], 1 - WIN_COLS, WIN_COLS - 1)
    bias = rpb[:, d_row[:, None, :, None] + WIN_ROWS - 1, d_col[None, :, None, :] + WIN_COLS - 1]
    bias = jnp.where(col_ok[None, None, :, None, :], bias.astype(F32), NEG_INF)
    bias_blocks = jnp.moveaxis(bias.reshape(H, n_blk, ROW_BLOCK, GRID_W, wr, GRID_W), 1, 0)
    q_blocks = jnp.moveaxis(q.reshape(B, n_blk, ROW_BLOCK, GRID_W, H, Dh), 1, 0)
    idx_blocks = jnp.asarray(row_idx.reshape(n_blk, ROW_BLOCK, wr), jnp.int32)
    k_grid = k.reshape(B, R, GRID_W, H, Dh)
    v_grid = v.reshape(B, R, GRID_W, H, Dh)
    n_loc = wr * GRID_W

    def one_block(args):
        qb, ridx, bb = args
        kb = k_grid[:, ridx]
        vb = v_grid[:, ridx]
        s_loc = jnp.einsum('brqhd,brjkhd->bhrqjk', qb, kb).astype(F32) * scale + bb[None]
        s_ctx = jnp.einsum('brqhd,bthd->bhrqt', qb, k_ctx).astype(F32) * scale
        s = jnp.concatenate([s_loc.reshape(s_loc.shape[:4] + (n_loc,)), s_ctx], axis=-1)
        pr = jax.nn.softmax(s, axis=-1).astype(v.dtype)
        p_loc = pr[..., :n_loc].reshape(s_loc.shape)
        return (jnp.einsum('bhrqjk,brjkhd->brqhd', p_loc, vb)
                + jnp.einsum('bhrqt,bthd->brqhd', pr[..., n_loc:], v_ctx))

    o = lax.map(one_block, (q_blocks, idx_blocks, bias_blocks))
    return jnp.moveaxis(o, 0, 1).reshape(B, T, H * Dh)


def hierarchical_moe(h, w_grp, b_grp, w_exp, b_exp, w1, w3, w2):
    B, T, D = h.shape
    t = h.reshape(B * T, D)
    g_logits = (t @ w_grp + b_grp).astype(F32)
    g_onehot = jax.nn.one_hot(jnp.argmax(g_logits, axis=-1), N_GROUPS, dtype=F32)
    p_grp = jnp.sum(jax.nn.softmax(g_logits, axis=-1) * g_onehot, axis=-1, keepdims=True)
    e_logits = (t @ w_exp + b_exp).astype(F32).reshape(-1, N_GROUPS, EXPERTS_PER_GROUP)
    e_in = jnp.einsum('ng,nge->ne', g_onehot, e_logits)
    top_v, top_i = lax.top_k(e_in, TOP_K)
    w_top = jax.nn.softmax(top_v, axis=-1) * p_grp
    e_idx = jnp.argmax(g_onehot, axis=-1)[:, None] * EXPERTS_PER_GROUP + top_i
    comb = jnp.einsum('nk,nke->ne', w_top, jax.nn.one_hot(e_idx, N_EXPERTS, dtype=F32)).astype(t.dtype)
    hid = jax.nn.silu(jnp.einsum('nd,edf->nef', t, w1)) * jnp.einsum('nd,edf->nef', t, w3)
    out = jnp.einsum('nef,efd->nd', hid * comb[:, :, None], w2)
    return out.reshape(B, T, D)


def mixer_sublayer(h, p, l, ctx):
    B, T, _ = h.shape
    z = h @ p['w_in'][l]
    widths = [SGU_WIDTH, SGU_WIDTH, LRU_WIDTH, LRU_WIDTH, NA_WIDTH, NA_WIDTH, NA_WIDTH]
    u, v_s, xr, gr, q, k, v, gates = jnp.split(z, list(np.cumsum(widths)), axis=-1)
    y_a = spatial_gating(u, v_s, p['sgu_norm'][l], p['sgu_w'][l], p['sgu_b'][l])
    xr = depthwise_conv(xr, p['lru_conv_w'][l], p['lru_conv_b'][l])
    h0 = jnp.zeros((B, 2, LRU_WIDTH), F32) if ctx is None else ctx[2].astype(F32)
    h_fwd, h_bwd = rglru_bidirectional(xr, p['lru_w_r'][l], p['lru_b_r'][l], p['lru_w_i'][l],
                                       p['lru_b_i'][l], p['lru_lambda'][l], h0)
    y_b = (h_fwd + h_bwd).astype(h.dtype) * jax.nn.gelu(gr)
    q = q.reshape(B, T, NA_HEADS, NA_HEAD_DIM)
    k = k.reshape(B, T, NA_HEADS, NA_HEAD_DIM)
    v = v.reshape(B, T, NA_HEADS, NA_HEAD_DIM)
    if ctx is None:
        y_c = context_attention(q, k, v)
        new_ctx = (k, v, jnp.stack([h_fwd[:, -1], h_bwd[:, 0]], axis=1).astype(h.dtype))
    else:
        y_c = neighbourhood_attention(q, k, v, ctx[0], ctx[1], p['na_rpb'][l])
        new_ctx = None
    g = jax.nn.sigmoid(gates.astype(F32)).astype(h.dtype).reshape(B, T, N_BRANCH, D_MODEL)
    merged = (g[:, :, 0] * (y_a @ p['w_branch_sgu'][l])
              + g[:, :, 1] * (y_b @ p['w_branch_lru'][l])
              + g[:, :, 2] * (y_c @ p['w_branch_na'][l]))
    return merged @ p['w_out'][l], new_ctx


def trunk_layer(x, cond, p, l, ctx):
    mod = (jax.nn.silu(cond) @ p['w_mod'][l] + p['b_mod'][l])[:, None, :]
    sh_a, sc_a, g_a, sh_f, sc_f, g_f = jnp.split(mod, 6, axis=-1)
    h = rmsnorm(x, p['norm_mix'][l]) * (1 + sc_a) + sh_a
    y, new_ctx = mixer_sublayer(h, p, l, ctx)
    x = x + g_a * y
    h = rmsnorm(x, p['norm_ffn'][l]) * (1 + sc_f) + sh_f
    y = hierarchical_moe(h, p['moe_w_group'][l], p['moe_b_group'][l], p['moe_w_expert'][l],
                         p['moe_b_expert'][l], p['moe_w1'][l], p['moe_w3'][l], p['moe_w2'][l])
    return x + g_f * y, new_ctx


def setup_inputs(seed: int = 0) -> dict:
    key = jax.random.key(seed)
    keys = iter(jax.random.split(key, 48))

    def nrm(shape, scale):
        return jax.random.normal(next(keys), shape, F32) * scale

    a0 = jax.random.uniform(next(keys), (DEPTH, 2, LRU_WIDTH), F32, minval=0.9, maxval=0.999)
    return {
        'x_prompt': nrm((BATCH, SEQ, D_MODEL), 1.0),
        'x_sample': nrm((DEC_BATCH, DEC_SEQ, D_MODEL), 1.0),
        'cache_k': nrm((DEC_BATCH, DEPTH, PAST_LEN, NA_HEADS, NA_HEAD_DIM), 1.0),
        'cache_v': nrm((DEC_BATCH, DEPTH, PAST_LEN, NA_HEADS, NA_HEAD_DIM), 1.0),
        'state_lru': nrm((DEC_BATCH, DEPTH, 2, LRU_WIDTH), 0.5),
        'c': nrm((DEC_BATCH, D_MODEL), 1.0),
        'c_ctx': nrm((D_MODEL,), 1.0),
        'w_mod': nrm((DEPTH, D_MODEL, 6 * D_MODEL), 0.5 * D_MODEL ** -0.5),
        'b_mod': nrm((DEPTH, 6 * D_MODEL), 0.01),
        'norm_mix': 1.0 + nrm((DEPTH, D_MODEL), 0.02),
        'norm_ffn': 1.0 + nrm((DEPTH, D_MODEL), 0.02),
        'w_in': nrm((DEPTH, D_MODEL, IN_WIDTH), D_MODEL ** -0.5),
        'sgu_norm': 1.0 + nrm((DEPTH, SGU_WIDTH), 0.02),
        'sgu_w': nrm((DEPTH, SGU_GROUPS, CHUNK, CHUNK), CHUNK ** -0.5),
        'sgu_b': 1.0 + nrm((DEPTH, SGU_GROUPS, CHUNK), 0.02),
        'lru_conv_w': nrm((DEPTH, CONV_WIDTH, LRU_WIDTH), CONV_WIDTH ** -0.5),
        'lru_conv_b': nrm((DEPTH, LRU_WIDTH), 0.01),
        'lru_w_r': nrm((DEPTH, 2, LRU_BLOCKS, LRU_BLOCK_DIM, LRU_BLOCK_DIM), LRU_BLOCK_DIM ** -0.5),
        'lru_b_r': nrm((DEPTH, 2, LRU_WIDTH), 0.01),
        'lru_w_i': nrm((DEPTH, 2, LRU_BLOCKS, LRU_BLOCK_DIM, LRU_BLOCK_DIM), LRU_BLOCK_DIM ** -0.5),
        'lru_b_i': nrm((DEPTH, 2, LRU_WIDTH), 0.01),
        'lru_lambda': jnp.log(a0) - jnp.log1p(-a0),
        'na_rpb': nrm((DEPTH, NA_HEADS, 2 * WIN_ROWS - 1, 2 * WIN_COLS - 1), 0.02),
        'w_branch_sgu': nrm((DEPTH, SGU_WIDTH, D_MODEL), SGU_WIDTH ** -0.5),
        'w_branch_lru': nrm((DEPTH, LRU_WIDTH, D_MODEL), LRU_WIDTH ** -0.5),
        'w_branch_na': nrm((DEPTH, NA_WIDTH, D_MODEL), NA_WIDTH ** -0.5),
        'w_out': nrm((DEPTH, D_MODEL, D_MODEL), D_MODEL ** -0.5),
        'moe_w_group': nrm((DEPTH, D_MODEL, N_GROUPS), D_MODEL ** -0.5),
        'moe_b_group': nrm((DEPTH, N_GROUPS), 0.01),
        'moe_w_expert': nrm((DEPTH, D_MODEL, N_EXPERTS), D_MODEL ** -0.5),
        'moe_b_expert': nrm((DEPTH, N_EXPERTS), 0.01),
        'moe_w1': nrm((DEPTH, N_EXPERTS, D_MODEL, D_EXPERT), D_MODEL ** -0.5),
        'moe_w3': nrm((DEPTH, N_EXPERTS, D_MODEL, D_EXPERT), D_MODEL ** -0.5),
        'moe_w2': nrm((DEPTH, N_EXPERTS, D_EXPERT, D_MODEL), D_EXPERT ** -0.5),
        'final_norm': 1.0 + nrm((D_MODEL,), 0.02),
    }


def reference(x_prompt, x_sample, cache_k, cache_v, state_lru, c, c_ctx, w_mod, b_mod, norm_mix,
              norm_ffn, w_in, sgu_norm, sgu_w, sgu_b, lru_conv_w, lru_conv_b, lru_w_r, lru_b_r,
              lru_w_i, lru_b_i, lru_lambda, na_rpb, w_branch_sgu, w_branch_lru, w_branch_na, w_out,
              moe_w_group, moe_b_group, moe_w_expert, moe_b_expert, moe_w1, moe_w3, moe_w2,
              final_norm):
    p = dict(w_mod=w_mod, b_mod=b_mod, norm_mix=norm_mix, norm_ffn=norm_ffn, w_in=w_in,
             sgu_norm=sgu_norm, sgu_w=sgu_w, sgu_b=sgu_b, lru_conv_w=lru_conv_w,
             lru_conv_b=lru_conv_b, lru_w_r=lru_w_r, lru_b_r=lru_b_r, lru_w_i=lru_w_i,
             lru_b_i=lru_b_i, lru_lambda=lru_lambda, na_rpb=na_rpb, w_branch_sgu=w_branch_sgu,
             w_branch_lru=w_branch_lru, w_branch_na=w_branch_na, w_out=w_out,
             moe_w_group=moe_w_group, moe_b_group=moe_b_group, moe_w_expert=moe_w_expert,
             moe_b_expert=moe_b_expert, moe_w1=moe_w1, moe_w3=moe_w3, moe_w2=moe_w2)
    xp = x_prompt
    cond_ctx = c_ctx[None, :]
    ks, vs, ss = [], [], []
    for l in range(DEPTH):
        xp, (k_l, v_l, s_l) = trunk_layer(xp, cond_ctx, p, l, None)
        ks.append(k_l)
        vs.append(v_l)
        ss.append(s_l)
    y_prompt = rmsnorm(xp, final_norm)
    xs = x_sample
    for l in range(DEPTH):
        xs, _ = trunk_layer(xs, c, p, l, (cache_k[:, l], cache_v[:, l], state_lru[:, l]))
    y_sample = rmsnorm(xs, final_norm)
    new_cache_k = jnp.stack(ks, axis=1)
    new_cache_v = jnp.stack(vs, axis=1)
    new_state_lru = jnp.stack(ss, axis=1)
    return (y_prompt, y_sample, new_cache_k, new_cache_v, new_state_lru)
```

```python
import functools

import jax
import jax.numpy as jnp
import numpy as np
from jax import lax
from jax.experimental import pallas as pl
from jax.experimental.pallas import tpu as pltpu

F32 = jnp.float32
BF16 = jnp.bfloat16

D_MODEL = 1024
BATCH = 16
SEQ = 256
DEPTH = 2
DEC_BATCH = 4
DEC_SEQ = 2048
PAST_LEN = 512
GRID_W = 64
CHUNK = 128
SGU_WIDTH = 512
SGU_GROUPS = 4
LRU_WIDTH = 512
LRU_BLOCKS = 8
CONV_WIDTH = 4
LRU_C = 8.0
NA_HEADS = 8
NA_HEAD_DIM = 64
NA_WIDTH = 512
WIN_ROWS = 8
WIN_COLS = 16
N_GROUPS = 4
EXPERTS_PER_GROUP = 4
N_EXPERTS = 16
D_EXPERT = 512
IN_WIDTH = 6656
EPS = 1e-6
NEG_INF = -1e30

N_CTX = BATCH * SEQ
N_LAT = DEC_BATCH * DEC_SEQ
N_TOK = N_CTX + N_LAT
N_COND = 8
MOD_WIDTH = 6 * D_MODEL
GRID_ROWS = DEC_SEQ // GRID_W

VMEM_LIMIT_BYTES = 56 * 1024 * 1024


def _params(n_axes):
    return pltpu.CompilerParams(dimension_semantics=("arbitrary",) * n_axes,
                                vmem_limit_bytes=VMEM_LIMIT_BYTES)


def _cond_row(tile, tile_rows):
    tok = tile * tile_rows
    return jnp.where(tok < N_CTX, 0, 1 + (tok - N_CTX) // DEC_SEQ)


def _rms(x, gain):
    return x * lax.rsqrt(jnp.mean(x * x, axis=-1, keepdims=True) + EPS) * gain


def _bdot(a, b):
    return jnp.dot(a.astype(BF16), b.astype(BF16), preferred_element_type=F32)


def _bdot_t(a, b):
    return lax.dot_general(a.astype(BF16), b.astype(BF16), (((1,), (1,)), ((), ())),
                           preferred_element_type=F32)


MOD_TN = 1536


def _mod_kernel(cond_ref, w_ref, b_ref, o_ref):
    c = cond_ref[...]
    s = c * jax.nn.sigmoid(c)
    o_ref[...] = _bdot(s, w_ref[...]) + b_ref[...]


def modulation(cond, w_mod, b_mod):
    return pl.pallas_call(
        _mod_kernel,
        grid=(DEPTH, MOD_WIDTH // MOD_TN),
        in_specs=[
            pl.BlockSpec((N_COND, D_MODEL), lambda l, j: (0, 0)),
            pl.BlockSpec((None, D_MODEL, MOD_TN), lambda l, j: (l, 0, j)),
            pl.BlockSpec((None, 1, MOD_TN), lambda l, j: (l, 0, j)),
        ],
        out_specs=pl.BlockSpec((None, N_COND, MOD_TN), lambda l, j: (l, 0, j)),
        out_shape=jax.ShapeDtypeStruct((DEPTH, N_COND, MOD_WIDTH), F32),
        compiler_params=_params(2),
        name="modulation",
    )(cond, w_mod, b_mod.reshape(DEPTH, 1, MOD_WIDTH))


IN_TM = 256
ZA_WIDTH = 4 * 512
KV_OFF = ZA_WIDTH + NA_WIDTH
GATE_OFF = KV_OFF + 2 * NA_WIDTH


def _inproj_kernel(x_ref, mod_ref, gain_ref, w_ref, za_ref, q_ref, kb_ref, vb_ref, kv_ref, g_ref):
    m = mod_ref[...]
    shift, scale = m[:, 0:D_MODEL], m[:, D_MODEL:2 * D_MODEL]
    h = (_rms(x_ref[...], gain_ref[...]) * (1.0 + scale) + shift).astype(BF16)
    za_ref[...] = jnp.dot(h, w_ref[:, 0:ZA_WIDTH], preferred_element_type=F32).astype(BF16)
    q_ref[...] = jnp.dot(h, w_ref[:, ZA_WIDTH:KV_OFF], preferred_element_type=F32).astype(BF16)
    kv = jnp.dot(h, w_ref[:, KV_OFF:GATE_OFF], preferred_element_type=F32)
    kv_ref[...] = kv
    kb_ref[...] = kv[:, 0:NA_WIDTH].astype(BF16)
    vb_ref[...] = kv[:, NA_WIDTH:].astype(BF16)
    g_ref[...] = jnp.dot(h, w_ref[:, GATE_OFF:], preferred_element_type=F32).astype(BF16)


def in_projection(x, mod3, gain, w_in):
    row = lambda i: (i, 0)
    return pl.pallas_call(
        _inproj_kernel,
        grid=(N_TOK // IN_TM,),
        in_specs=[
            pl.BlockSpec((IN_TM, D_MODEL), row),
            pl.BlockSpec((None, 1, MOD_WIDTH), lambda i: (_cond_row(i, IN_TM), 0, 0)),
            pl.BlockSpec((1, D_MODEL), lambda i: (0, 0)),
            pl.BlockSpec((D_MODEL, IN_WIDTH), lambda i: (0, 0)),
        ],
        out_specs=[
            pl.BlockSpec((IN_TM, ZA_WIDTH), row),
            pl.BlockSpec((IN_TM, NA_WIDTH), row),
            pl.BlockSpec((IN_TM, NA_WIDTH), row),
            pl.BlockSpec((IN_TM, NA_WIDTH), row),
            pl.BlockSpec((IN_TM, 2 * NA_WIDTH), row),
            pl.BlockSpec((IN_TM, 3 * D_MODEL), row),
        ],
        out_shape=[
            jax.ShapeDtypeStruct((N_TOK, ZA_WIDTH), BF16),
            jax.ShapeDtypeStruct((N_TOK, NA_WIDTH), BF16),
            jax.ShapeDtypeStruct((N_TOK, NA_WIDTH), BF16),
            jax.ShapeDtypeStruct((N_TOK, NA_WIDTH), BF16),
            jax.ShapeDtypeStruct((N_TOK, 2 * NA_WIDTH), F32),
            jax.ShapeDtypeStruct((N_TOK, 3 * D_MODEL), BF16),
        ],
        compiler_params=_params(1),
        name="in_projection",
    )(x, mod3, gain, w_in)


SGU_TM = 512
SGU_GD = SGU_WIDTH // SGU_GROUPS


def _sgu_kernel(u_ref, v_ref, gain_ref, ws_ref, bs_ref, o_ref):
    gain = gain_ref[...]
    for c in range(SGU_TM // CHUNK):
        rows = slice(c * CHUNK, (c + 1) * CHUNK)
        u = jax.nn.gelu(u_ref[rows, :].astype(F32))
        v = _rms(jax.nn.gelu(v_ref[rows, :].astype(F32)), gain).astype(BF16)
        for g in range(SGU_GROUPS):
            cols = slice(g * SGU_GD, (g + 1) * SGU_GD)
            mixed = jnp.dot(ws_ref[g], v[:, cols], preferred_element_type=F32) + bs_ref[:, g:g + 1]
            o_ref[rows, cols] = (u[:, cols] * mixed).astype(BF16)


def spatial_gating(za, gain, ws, bs_t):
    return pl.pallas_call(
        _sgu_kernel,
        grid=(N_TOK // SGU_TM,),
        in_specs=[
            pl.BlockSpec((SGU_TM, SGU_WIDTH), lambda i: (i, 0)),
            pl.BlockSpec((SGU_TM, SGU_WIDTH), lambda i: (i, 1)),
            pl.BlockSpec((1, SGU_WIDTH), lambda i: (0, 0)),
            pl.BlockSpec((SGU_GROUPS, CHUNK, CHUNK), lambda i: (0, 0, 0)),
            pl.BlockSpec((CHUNK, SGU_GROUPS), lambda i: (0, 0)),
        ],
        out_specs=pl.BlockSpec((SGU_TM, SGU_WIDTH), lambda i: (i, 0)),
        out_shape=jax.ShapeDtypeStruct((N_TOK, SGU_WIDTH), BF16),
        compiler_params=_params(1),
        name="spatial_gating",
    )(za, za, gain, ws, bs_t)


LRU_TC = 256
LRU_HALF = 256
SUB = 8
HALO = 8


def _lru_kernel(seq_len, xr_ref, gr_ref, cw_ref, cb_ref, wlo_ref, whi_ref, br_ref, bi_ref,
                lam_ref, h0_ref, y_ref, st_ref, xp_ref, a_ref, b_ref):
    n_chunks = seq_len // LRU_TC
    zeros = jnp.zeros((HALO, LRU_WIDTH), F32)
    xp_ref[0:HALO, :] = zeros
    xp_ref[seq_len + HALO:seq_len + 2 * HALO, :] = zeros

    def copy_in(c, carry):
        r0 = pl.multiple_of(c * LRU_TC, LRU_TC)
        xp_ref[pl.ds(r0 + HALO, LRU_TC), :] = xr_ref[pl.ds(r0, LRU_TC), :].astype(F32)
        return carry

    lax.fori_loop(0, n_chunks, copy_in, 0)

    cw = cw_ref[...]
    cb = cb_ref[...]
    win = LRU_TC + 2 * HALO
    sub_pos = lax.broadcasted_iota(jnp.int32, (LRU_TC, LRU_HALF), 0) % SUB
    neg_lam = -lam_ref[...]
    softplus = jnp.maximum(neg_lam, 0.0) + jnp.log1p(jnp.exp(-jnp.abs(neg_lam)))

    def local_scan(a, b, reverse):
        for s in (1, 2, 4):
            if reverse:
                a_n = pltpu.roll(a, LRU_TC - s, 0)
                b_n = pltpu.roll(b, LRU_TC - s, 0)
                keep = sub_pos < SUB - s
            else:
                a_n = pltpu.roll(a, s, 0)
                b_n = pltpu.roll(b, s, 0)
                keep = sub_pos >= s
            b = jnp.where(keep, a * b_n + b, b)
            a = jnp.where(keep, a * a_n, a)
        return a, b

    def gates(c, carry):
        r0 = pl.multiple_of(c * LRU_TC, LRU_TC)
        w = xp_ref[pl.ds(r0, win), :]
        xc = (cw[0:1, :] * pltpu.roll(w, 1, 0)[HALO:HALO + LRU_TC]
              + cw[1:2, :] * w[HALO:HALO + LRU_TC]
              + cw[2:3, :] * pltpu.roll(w, win - 1, 0)[HALO:HALO + LRU_TC]
              + cw[3:4, :] * pltpu.roll(w, win - 2, 0)[HALO:HALO + LRU_TC]) + cb
        xb = xc.astype(BF16)
        for half, w_ref in enumerate((wlo_ref, whi_ref)):
            cols = slice(half * LRU_HALF, (half + 1) * LRU_HALF)
            pre = jnp.dot(xb[:, cols], w_ref[...], preferred_element_type=F32)
            xh = xc[:, cols]
            for d in range(2):
                r = jax.nn.sigmoid(pre[:, (2 * d) * LRU_HALF:(2 * d + 1) * LRU_HALF] + br_ref[d:d + 1, cols])
                i = jax.nn.sigmoid(pre[:, (2 * d + 1) * LRU_HALF:(2 * d + 2) * LRU_HALF] + bi_ref[d:d + 1, cols])
                log_a = -LRU_C * r * softplus[d:d + 1, cols]
                a = jnp.exp(log_a)
                b = jnp.sqrt(-jnp.tanh(log_a) * (a * a + 1.0)) * (i * xh)
                a, b = local_scan(a, b, reverse=(d == 1))
                a_ref[d, pl.ds(r0, LRU_TC), cols] = a
                b_ref[d, pl.ds(r0, LRU_TC), cols] = b
        return carry

    lax.fori_loop(0, n_chunks, gates, 0)

    n_blocks = seq_len // SUB

    def fwd(j, h):
        r0 = pl.multiple_of(j * SUB, SUB)
        hb = a_ref[0, pl.ds(r0, SUB), :] * h + b_ref[0, pl.ds(r0, SUB), :]
        b_ref[0, pl.ds(r0, SUB), :] = hb
        return hb[SUB - 1:SUB, :]

    def bwd(j, h):
        r0 = pl.multiple_of((n_blocks - 1 - j) * SUB, SUB)
        hb = a_ref[1, pl.ds(r0, SUB), :] * h + b_ref[1, pl.ds(r0, SUB), :]
        b_ref[1, pl.ds(r0, SUB), :] = hb
        return hb[0:1, :]

    h_f = lax.fori_loop(0, n_blocks, fwd, h0_ref[0:1, :], unroll=4)
    h_b = lax.fori_loop(0, n_blocks, bwd, h0_ref[1:2, :], unroll=4)
    st_ref[0:1, :] = h_f
    st_ref[1:2, :] = h_b

    def merge(c, carry):
        r0 = pl.multiple_of(c * LRU_TC, LRU_TC)
        rows = pl.ds(r0, LRU_TC)
        y_ref[rows, :] = ((b_ref[0, rows, :] + b_ref[1, rows, :])
                          * jax.nn.gelu(gr_ref[rows, :].astype(F32))).astype(BF16)
        return carry

    lax.fori_loop(0, n_chunks, merge, 0)


def rglru(za, conv_w, conv_b, w_lo, w_hi, b_r, b_i, lam, h0, seq_len, n_seq, tok_off):
    blk0 = tok_off // seq_len
    full = lambda shape: pl.BlockSpec(shape, lambda i: (0,) * len(shape))
    return pl.pallas_call(
        functools.partial(_lru_kernel, seq_len),
        grid=(n_seq,),
        in_specs=[
            pl.BlockSpec((seq_len, LRU_WIDTH), lambda i: (i + blk0, 2)),
            pl.BlockSpec((seq_len, LRU_WIDTH), lambda i: (i + blk0, 3)),
            full((CONV_WIDTH, LRU_WIDTH)),
            full((1, LRU_WIDTH)),
            full((LRU_HALF, 4 * LRU_HALF)),
            full((LRU_HALF, 4 * LRU_HALF)),
            full((2, LRU_WIDTH)),
            full((2, LRU_WIDTH)),
            full((2, LRU_WIDTH)),
            pl.BlockSpec((None, 2, LRU_WIDTH), lambda i: (i, 0, 0)),
        ],
        out_specs=[
            pl.BlockSpec((seq_len, LRU_WIDTH), lambda i: (i, 0)),
            pl.BlockSpec((None, 2, LRU_WIDTH), lambda i: (i, 0, 0)),
        ],
        out_shape=[
            jax.ShapeDtypeStruct((n_seq * seq_len, LRU_WIDTH), BF16),
            jax.ShapeDtypeStruct((n_seq, 2, LRU_WIDTH), F32),
        ],
        scratch_shapes=[
            pltpu.VMEM((seq_len + 2 * HALO, LRU_WIDTH), F32),
            pltpu.VMEM((2, seq_len, LRU_WIDTH), F32),
            pltpu.VMEM((2, seq_len, LRU_WIDTH), F32),
        ],
        compiler_params=_params(1),
        name=f"rglru_{seq_len}",
    )(za, za, conv_w, conv_b, w_lo, w_hi, b_r, b_i, lam, h0)


def _block_diag_gate_weights(w_r, w_i):
    per_half = LRU_BLOCKS // 2
    eye = jnp.eye(per_half, dtype=w_r.dtype)

    def bd(w):
        return jnp.einsum("hij,hg->higj", w, eye).reshape(LRU_HALF, LRU_HALF)

    tiles = []
    for half in range(2):
        sl = slice(half * per_half, (half + 1) * per_half)
        tiles.append(jnp.concatenate([bd(w_r[0, sl]), bd(w_i[0, sl]), bd(w_r[1, sl]), bd(w_i[1, sl])], axis=1))
    return tiles[0].astype(BF16), tiles[1].astype(BF16)


ATT_SCALE = NA_HEAD_DIM ** -0.5


def _ctx_attn_kernel(q_ref, k_ref, v_ref, o_ref):
    for h in range(NA_HEADS):
        cols = slice(h * NA_HEAD_DIM, (h + 1) * NA_HEAD_DIM)
        s = _bdot_t(q_ref[:, cols], k_ref[:, cols]) * ATT_SCALE
        p = jnp.exp(s - jnp.max(s, axis=-1, keepdims=True))
        denom = jnp.sum(p, axis=-1, keepdims=True)
        o = jnp.dot(p.astype(BF16), v_ref[:, cols], preferred_element_type=F32)
        o_ref[:, cols] = (o / denom).astype(BF16)


def context_attention(q, kb, vb):
    spec = pl.BlockSpec((SEQ, NA_WIDTH), lambda i: (i, 0))
    return pl.pallas_call(
        _ctx_attn_kernel,
        grid=(BATCH,),
        in_specs=[spec, spec, spec],
        out_specs=spec,
        out_shape=jax.ShapeDtypeStruct((N_CTX, NA_WIDTH), BF16),
        compiler_params=_params(1),
        name="context_attention",
    )(q, kb, vb)


NA_QROWS = 4
NA_TQ = NA_QROWS * GRID_W
NA_KROWS = NA_QROWS + WIN_ROWS
NA_TK = NA_KROWS * GRID_W
NA_QBLOCKS = GRID_ROWS // NA_QROWS


def _lat_attn_kernel(q_ref, k_ref, v_ref, ck_ref, cv_ref, tab_ref, o_ref, bias_ref):
    qb = pl.program_id(1)
    first_krow = jnp.clip(qb * NA_QROWS - WIN_ROWS // 2, 0, GRID_ROWS - NA_KROWS)
    neg = jnp.full((GRID_W, GRID_W), NEG_INF, F32)
    for i in range(NA_QROWS):
        qrow = qb * NA_QROWS + i
        win0 = jnp.clip(qrow - WIN_ROWS // 2, 0, GRID_ROWS - WIN_ROWS)
        for kr in range(NA_KROWS):
            krow = first_krow + kr
            inside = jnp.logical_and(krow >= win0, krow < win0 + WIN_ROWS)
            d_row = jnp.clip(krow - qrow + WIN_ROWS - 1, 0, 2 * WIN_ROWS - 2)
            for h in range(NA_HEADS):
                bias_ref[h, i * GRID_W:(i + 1) * GRID_W, kr * GRID_W:(kr + 1) * GRID_W] = (
                    jnp.where(inside, tab_ref[h, d_row], neg))

    k0 = pl.multiple_of(first_krow * GRID_W, GRID_W)
    for h in range(NA_HEADS):
        cols = slice(h * NA_HEAD_DIM, (h + 1) * NA_HEAD_DIM)
        qh = q_ref[:, cols]
        s_loc = _bdot_t(qh, k_ref[pl.ds(k0, NA_TK), cols]) * ATT_SCALE + bias_ref[h]
        s_ctx = _bdot_t(qh, ck_ref[:, cols]) * ATT_SCALE
        m = jnp.maximum(jnp.max(s_loc, axis=-1, keepdims=True), jnp.max(s_ctx, axis=-1, keepdims=True))
        p_loc = jnp.exp(s_loc - m)
        p_ctx = jnp.exp(s_ctx - m)
        denom = jnp.sum(p_loc, axis=-1, keepdims=True) + jnp.sum(p_ctx, axis=-1, keepdims=True)
        o = (jnp.dot(p_loc.astype(BF16), v_ref[pl.ds(k0, NA_TK), cols], preferred_element_type=F32)
             + _bdot(p_ctx, cv_ref[:, cols]))
        o_ref[:, cols] = (o / denom).astype(BF16)


def latent_attention(q, kb, vb, cache_k, cache_v, table):
    q_blk0 = N_CTX // NA_TQ
    s_blk0 = N_CTX // DEC_SEQ
    return pl.pallas_call(
        _lat_attn_kernel,
        grid=(DEC_BATCH, NA_QBLOCKS),
        in_specs=[
            pl.BlockSpec((NA_TQ, NA_WIDTH), lambda b, m: (q_blk0 + b * NA_QBLOCKS + m, 0)),
            pl.BlockSpec((DEC_SEQ, NA_WIDTH), lambda b, m: (s_blk0 + b, 0)),
            pl.BlockSpec((DEC_SEQ, NA_WIDTH), lambda b, m: (s_blk0 + b, 0)),
            pl.BlockSpec((None, PAST_LEN, NA_WIDTH), lambda b, m: (b, 0, 0)),
            pl.BlockSpec((None, PAST_LEN, NA_WIDTH), lambda b, m: (b, 0, 0)),
            pl.BlockSpec((NA_HEADS, 2 * WIN_ROWS - 1, GRID_W, GRID_W), lambda b, m: (0, 0, 0, 0)),
        ],
        out_specs=pl.BlockSpec((NA_TQ, NA_WIDTH), lambda b, m: (b * NA_QBLOCKS + m, 0)),
        out_shape=jax.ShapeDtypeStruct((N_LAT, NA_WIDTH), BF16),
        scratch_shapes=[pltpu.VMEM((NA_HEADS, NA_TQ, NA_TK), F32)],
        compiler_params=_params(2),
        name="latent_attention",
    )(q, kb, vb, cache_k, cache_v, table)


def _rel_bias_table(rpb):
    cols = np.arange(GRID_W)
    col_start = np.clip(cols - WIN_COLS // 2, 0, GRID_W - WIN_COLS)
    col_ok = (cols[None, :] >= col_start[:, None]) & (cols[None, :] < col_start[:, None] + WIN_COLS)
    d_col = np.clip(cols[None, :] - cols[:, None], 1 - WIN_COLS, WIN_COLS - 1) + WIN_COLS - 1
    return jnp.where(col_ok[None, None], rpb[:, :, d_col].astype(F32), NEG_INF)


MERGE_TM = 256


def _merge_kernel(x_ref, mod_ref, ya_ref, yb_ref, yc_ref, g_ref, wa_ref, wb_ref, wc_ref, wo_ref, o_ref):
    g = jax.nn.sigmoid(g_ref[...].astype(F32))
    merged = (g[:, 0:D_MODEL] * jnp.dot(ya_ref[...], wa_ref[...], preferred_element_type=F32)
              + g[:, D_MODEL:2 * D_MODEL] * jnp.dot(yb_ref[...], wb_ref[...], preferred_element_type=F32)
              + g[:, 2 * D_MODEL:] * jnp.dot(yc_ref[...], wc_ref[...], preferred_element_type=F32))
    y = jnp.dot(merged.astype(BF16), wo_ref[...], preferred_element_type=F32)
    gate = mod_ref[:, 2 * D_MODEL:3 * D_MODEL]
    o_ref[...] = x_ref[...] + gate * y


def merge_branches(x, mod3, ya, yb, yc, gates, wa, wb, wc, wo):
    row = lambda i: (i, 0)
    const = lambda i: (0, 0)
    return pl.pallas_call(
        _merge_kernel,
        grid=(N_TOK // MERGE_TM,),
        in_specs=[
            pl.BlockSpec((MERGE_TM, D_MODEL), row),
            pl.BlockSpec((None, 1, MOD_WIDTH), lambda i: (_cond_row(i, MERGE_TM), 0, 0)),
            pl.BlockSpec((MERGE_TM, SGU_WIDTH), row),
            pl.BlockSpec((MERGE_TM, LRU_WIDTH), row),
            pl.BlockSpec((MERGE_TM, NA_WIDTH), row),
            pl.BlockSpec((MERGE_TM, 3 * D_MODEL), row),
            pl.BlockSpec((SGU_WIDTH, D_MODEL), const),
            pl.BlockSpec((LRU_WIDTH, D_MODEL), const),
            pl.BlockSpec((NA_WIDTH, D_MODEL), const),
            pl.BlockSpec((D_MODEL, D_MODEL), const),
        ],
        out_specs=pl.BlockSpec((MERGE_TM, D_MODEL), row),
        out_shape=jax.ShapeDtypeStruct((N_TOK, D_MODEL), F32),
        compiler_params=_params(1),
        name="merge_branches",
    )(x, mod3, ya, yb, yc, gates, wa, wb, wc, wo)


RT_TM = 512
RT_ROWS = 32


def _split_bf16(x):
    hi = x.astype(BF16)
    return hi, (x - hi.astype(F32)).astype(BF16)


def _router_kernel(x_ref, mod_ref, gain_ref, w_ref, b_ref, h_ref, comb_ref):
    m = mod_ref[...]
    shift, scale = m[:, 3 * D_MODEL:4 * D_MODEL], m[:, 4 * D_MODEL:5 * D_MODEL]
    h = _rms(x_ref[...], gain_ref[...]) * (1.0 + scale) + shift
    h_ref[...] = h.astype(BF16)
    h_hi, h_lo = _split_bf16(h)
    w_hi, w_lo = _split_bf16(w_ref[...])
    dims = (((1,), (1,)), ((), ()))
    logits = (lax.dot_general(w_hi, h_hi, dims, preferred_element_type=F32)
              + lax.dot_general(w_hi, h_lo, dims, preferred_element_type=F32)
              + lax.dot_general(w_lo, h_hi, dims, preferred_element_type=F32)) + b_ref[...]
    gl = [logits[g:g + 1, :] for g in range(N_GROUPS)]
    gmax = functools.reduce(jnp.maximum, gl)
    gid = jnp.full(gmax.shape, N_GROUPS - 1, jnp.int32)
    for g in reversed(range(N_GROUPS - 1)):
        gid = jnp.where(gl[g] == gmax, g, gid)
    p_grp = 1.0 / functools.reduce(jnp.add, [jnp.exp(v - gmax) for v in gl])
    el = []
    for e in range(EXPERTS_PER_GROUP):
        v = logits[8 + e:9 + e, :]
        for g in range(1, N_GROUPS):
            row = 8 + g * EXPERTS_PER_GROUP + e
            v = jnp.where(gid == g, logits[row:row + 1, :], v)
        el.append(v)
    top1 = functools.reduce(jnp.maximum, el)
    idx1 = jnp.full(top1.shape, EXPERTS_PER_GROUP - 1, jnp.int32)
    for e in reversed(range(EXPERTS_PER_GROUP - 1)):
        idx1 = jnp.where(el[e] == top1, e, idx1)
    rest = [jnp.where(idx1 == e, -jnp.inf, el[e]) for e in range(EXPERTS_PER_GROUP)]
    top2 = functools.reduce(jnp.maximum, rest)
    idx2 = jnp.full(top1.shape, EXPERTS_PER_GROUP - 1, jnp.int32)
    for e in reversed(range(EXPERTS_PER_GROUP - 1)):
        idx2 = jnp.where(rest[e] == top2, e, idx2)
    e2 = jnp.exp(top2 - top1)
    w1 = p_grp / (1.0 + e2)
    w2 = p_grp * e2 / (1.0 + e2)
    for g in range(N_GROUPS):
        for e in range(EXPERTS_PER_GROUP):
            w = jnp.where(idx1 == e, w1, 0.0) + jnp.where(idx2 == e, w2, 0.0)
            row = g * EXPERTS_PER_GROUP + e
            comb_ref[row:row + 1, :] = jnp.where(gid == g, w, 0.0)


def router(x, mod3, gain, w_rt, b_rt):
    return pl.pallas_call(
        _router_kernel,
        grid=(N_TOK // RT_TM,),
        in_specs=[
            pl.BlockSpec((RT_TM, D_MODEL), lambda i: (i, 0)),
            pl.BlockSpec((None, 1, MOD_WIDTH), lambda i: (_cond_row(i, RT_TM), 0, 0)),
            pl.BlockSpec((1, D_MODEL), lambda i: (0, 0)),
            pl.BlockSpec((RT_ROWS, D_MODEL), lambda i: (0, 0)),
            pl.BlockSpec((RT_ROWS, 1), lambda i: (0, 0)),
        ],
        out_specs=[
            pl.BlockSpec((RT_TM, D_MODEL), lambda i: (i, 0)),
            pl.BlockSpec((N_EXPERTS, RT_TM), lambda i: (0, i)),
        ],
        out_shape=[
            jax.ShapeDtypeStruct((N_TOK, D_MODEL), BF16),
            jax.ShapeDtypeStruct((N_EXPERTS, N_TOK), F32),
        ],
        compiler_params=_params(1),
        name="router",
    )(x, mod3, gain, w_rt, b_rt)


def _router_weights(w_grp, b_grp, w_exp, b_exp):
    w = jnp.zeros((RT_ROWS, D_MODEL), F32).at[0:N_GROUPS].set(w_grp.T).at[8:8 + N_EXPERTS].set(w_exp.T)
    b = jnp.zeros((RT_ROWS, 1), F32).at[0:N_GROUPS, 0].set(b_grp).at[8:8 + N_EXPERTS, 0].set(b_exp)
    return w, b


MOE_TM = 1024


def _moe_kernel(h_ref, comb_ref, x_ref, mod_ref, w1_ref, w3_ref, w2_ref, o_ref, acc_ref):
    e = pl.program_id(1)

    @pl.when(e == 0)
    def _():
        acc_ref[...] = jnp.zeros_like(acc_ref)

    h = h_ref[...]
    a = jnp.dot(h, w1_ref[...], preferred_element_type=F32)
    b = jnp.dot(h, w3_ref[...], preferred_element_type=F32)
    lane = lax.broadcasted_iota(jnp.int32, (MOE_TM, N_EXPERTS), 1)
    c = jnp.sum(jnp.where(lane == e, comb_ref[...], 0.0), axis=-1, keepdims=True)
    hid = (a * jax.nn.sigmoid(a)) * b * c
    acc_ref[...] += jnp.dot(hid.astype(BF16), w2_ref[...], preferred_element_type=F32)

    @pl.when(e == N_EXPERTS - 1)
    def _():
        o_ref[...] = x_ref[...] + mod_ref[:, 5 * D_MODEL:6 * D_MODEL] * acc_ref[...]


def experts(h, comb, x, mod3, w1, w3, w2):
    return pl.pallas_call(
        _moe_kernel,
        grid=(N_TOK // MOE_TM, N_EXPERTS),
        in_specs=[
            pl.BlockSpec((MOE_TM, D_MODEL), lambda i, e: (i, 0)),
            pl.BlockSpec((MOE_TM, N_EXPERTS), lambda i, e: (i, 0)),
            pl.BlockSpec((MOE_TM, D_MODEL), lambda i, e: (i, 0)),
            pl.BlockSpec((None, 1, MOD_WIDTH), lambda i, e: (_cond_row(i, MOE_TM), 0, 0)),
            pl.BlockSpec((None, D_MODEL, D_EXPERT), lambda i, e: (e, 0, 0)),
            pl.BlockSpec((None, D_MODEL, D_EXPERT), lambda i, e: (e, 0, 0)),
            pl.BlockSpec((None, D_EXPERT, D_MODEL), lambda i, e: (e, 0, 0)),
        ],
        out_specs=pl.BlockSpec((MOE_TM, D_MODEL), lambda i, e: (i, 0)),
        out_shape=jax.ShapeDtypeStruct((N_TOK, D_MODEL), F32),
        scratch_shapes=[pltpu.VMEM((MOE_TM, D_MODEL), F32)],
        compiler_params=_params(2),
        name="experts",
    )(h, comb, x, mod3, w1, w3, w2)


FN_TM = 512


def _final_norm_kernel(x_ref, gain_ref, o_ref):
    o_ref[...] = _rms(x_ref[...], gain_ref[...])


def final_norm(x, gain):
    return pl.pallas_call(
        _final_norm_kernel,
        grid=(N_TOK // FN_TM,),
        in_specs=[pl.BlockSpec((FN_TM, D_MODEL), lambda i: (i, 0)),
                  pl.BlockSpec((1, D_MODEL), lambda i: (0, 0))],
        out_specs=pl.BlockSpec((FN_TM, D_MODEL), lambda i: (i, 0)),
        out_shape=jax.ShapeDtypeStruct((N_TOK, D_MODEL), F32),
        compiler_params=_params(1),
        name="final_norm",
    )(x, gain)


def kernel(x_prompt, x_sample, cache_k, cache_v, state_lru, c, c_ctx, w_mod, b_mod, norm_mix, norm_ffn, w_in, sgu_norm, sgu_w, sgu_b, lru_conv_w, lru_conv_b, lru_w_r, lru_b_r, lru_w_i, lru_b_i, lru_lambda, na_rpb, w_branch_sgu, w_branch_lru, w_branch_na, w_out, moe_w_group, moe_b_group, moe_w_expert, moe_b_expert, moe_w1, moe_w3, moe_w2, final_norm_gain):
    x = jnp.concatenate([x_prompt.reshape(N_CTX, D_MODEL), x_sample.reshape(N_LAT, D_MODEL)], axis=0)
    cond = jnp.zeros((N_COND, D_MODEL), F32).at[0].set(c_ctx).at[1:1 + DEC_BATCH].set(c)
    mod = modulation(cond, w_mod, b_mod)
    zero_state = jnp.zeros((BATCH, 2, LRU_WIDTH), F32)
    ks, vs, ss = [], [], []
    for l in range(DEPTH):
        mod3 = mod[l].reshape(N_COND, 1, MOD_WIDTH)
        za, q, kb, vb, kv32, gates = in_projection(x, mod3, norm_mix[l].reshape(1, D_MODEL), w_in[l].astype(BF16))
        ya = spatial_gating(za, sgu_norm[l].reshape(1, SGU_WIDTH), sgu_w[l].astype(BF16), sgu_b[l].T)
        w_lo, w_hi = _block_diag_gate_weights(lru_w_r[l], lru_w_i[l])
        lru_args = (lru_conv_w[l], lru_conv_b[l].reshape(1, LRU_WIDTH), w_lo, w_hi, lru_b_r[l], lru_b_i[l],
                    lru_lambda[l])
        yb_ctx, st_ctx = rglru(za, *lru_args, zero_state, SEQ, BATCH, 0)
        yb_lat, _ = rglru(za, *lru_args, state_lru[:, l], DEC_SEQ, DEC_BATCH, N_CTX)
        yc_ctx = context_attention(q, kb, vb)
        yc_lat = latent_attention(q, kb, vb, cache_k[:, l].reshape(DEC_BATCH, PAST_LEN, NA_WIDTH),
                                  cache_v[:, l].reshape(DEC_BATCH, PAST_LEN, NA_WIDTH), _rel_bias_table(na_rpb[l]))
        x = merge_branches(x, mod3, ya, jnp.concatenate([yb_ctx, yb_lat], axis=0),
                           jnp.concatenate([yc_ctx, yc_lat], axis=0), gates,
                           w_branch_sgu[l].astype(BF16), w_branch_lru[l].astype(BF16),
                           w_branch_na[l].astype(BF16), w_out[l].astype(BF16))
        w_rt, b_rt = _router_weights(moe_w_group[l], moe_b_group[l], moe_w_expert[l], moe_b_expert[l])
        h, comb_t = router(x, mod3, norm_ffn[l].reshape(1, D_MODEL), w_rt, b_rt)
        x = experts(h, comb_t.T, x, mod3, moe_w1[l].astype(BF16), moe_w3[l].astype(BF16), moe_w2[l].astype(BF16))
        ks.append(kv32[:N_CTX, :NA_WIDTH].reshape(BATCH, SEQ, NA_HEADS, NA_HEAD_DIM))
        vs.append(kv32[:N_CTX, NA_WIDTH:].reshape(BATCH, SEQ, NA_HEADS, NA_HEAD_DIM))
        ss.append(st_ctx)
    y = final_norm(x, final_norm_gain.reshape(1, D_MODEL))
    return (y[:N_CTX].reshape(BATCH, SEQ, D_MODEL), y[N_CTX:].reshape(DEC_BATCH, DEC_SEQ, D_MODEL),
            jnp.stack(ks, axis=1), jnp.stack(vs, axis=1), jnp.stack(ss, axis=1))
```

```python
import functools

import jax
import jax.numpy as jnp
import numpy as np
from jax import lax
from jax.experimental import pallas as pl
from jax.experimental.pallas import tpu as pltpu

F32 = jnp.float32
BF16 = jnp.bfloat16

D_MODEL = 1024
BATCH = 16
SEQ = 256
DEPTH = 2
DEC_BATCH = 4
DEC_SEQ = 2048
PAST_LEN = 512
GRID_W = 64
CHUNK = 128
SGU_WIDTH = 512
SGU_GROUPS = 4
LRU_WIDTH = 512
LRU_BLOCKS = 8
CONV_WIDTH = 4
LRU_C = 8.0
NA_HEADS = 8
NA_HEAD_DIM = 64
NA_WIDTH = 512
WIN_ROWS = 8
WIN_COLS = 16
N_GROUPS = 4
EXPERTS_PER_GROUP = 4
N_EXPERTS = 16
D_EXPERT = 512
IN_WIDTH = 6656
EPS = 1e-6
NEG_INF = -1e30

N_CTX = BATCH * SEQ
N_LAT = DEC_BATCH * DEC_SEQ
N_TOK = N_CTX + N_LAT
N_COND = 8
MOD_WIDTH = 6 * D_MODEL
GRID_ROWS = DEC_SEQ // GRID_W

VMEM_LIMIT_BYTES = 56 * 1024 * 1024


def _params(n_axes):
    return pltpu.CompilerParams(dimension_semantics=("arbitrary",) * n_axes,
                                vmem_limit_bytes=VMEM_LIMIT_BYTES)


def _cond_row(tile, tile_rows):
    tok = tile * tile_rows
    return jnp.where(tok < N_CTX, 0, 1 + (tok - N_CTX) // DEC_SEQ)


def _rms(x, gain):
    return x * lax.rsqrt(jnp.mean(x * x, axis=-1, keepdims=True) + EPS) * gain


def _bdot(a, b):
    return jnp.dot(a.astype(BF16), b.astype(BF16), preferred_element_type=F32)


def _bdot_t(a, b):
    return lax.dot_general(a.astype(BF16), b.astype(BF16), (((1,), (1,)), ((), ())),
                           preferred_element_type=F32)


MOD_TN = 1536


def _mod_kernel(cond_ref, w_ref, b_ref, o_ref):
    c = cond_ref[...]
    s = c * jax.nn.sigmoid(c)
    o_ref[...] = _bdot(s, w_ref[...]) + b_ref[...]


def modulation(cond, w_mod, b_mod):
    return pl.pallas_call(
        _mod_kernel,
        grid=(DEPTH, MOD_WIDTH // MOD_TN),
        in_specs=[
            pl.BlockSpec((N_COND, D_MODEL), lambda l, j: (0, 0)),
            pl.BlockSpec((None, D_MODEL, MOD_TN), lambda l, j: (l, 0, j)),
            pl.BlockSpec((None, 1, MOD_TN), lambda l, j: (l, 0, j)),
        ],
        out_specs=pl.BlockSpec((None, N_COND, MOD_TN), lambda l, j: (l, 0, j)),
        out_shape=jax.ShapeDtypeStruct((DEPTH, N_COND, MOD_WIDTH), F32),
        compiler_params=_params(2),
        name="modulation",
    )(cond, w_mod, b_mod.reshape(DEPTH, 1, MOD_WIDTH))


IN_TM = 256
ZA_WIDTH = 4 * 512
KV_OFF = ZA_WIDTH + NA_WIDTH
GATE_OFF = KV_OFF + 2 * NA_WIDTH


def _inproj_kernel(x_ref, mod_ref, gain_ref, w_ref, za_ref, q_ref, kb_ref, vb_ref, kv_ref, g_ref):
    m = mod_ref[...]
    shift, scale = m[:, 0:D_MODEL], m[:, D_MODEL:2 * D_MODEL]
    h = (_rms(x_ref[...], gain_ref[...]) * (1.0 + scale) + shift).astype(BF16)
    za_ref[...] = jnp.dot(h, w_ref[:, 0:ZA_WIDTH], preferred_element_type=F32).astype(BF16)
    q_ref[...] = jnp.dot(h, w_ref[:, ZA_WIDTH:KV_OFF], preferred_element_type=F32).astype(BF16)
    kv = jnp.dot(h, w_ref[:, KV_OFF:GATE_OFF], preferred_element_type=F32)
    kv_ref[...] = kv
    kb_ref[...] = kv[:, 0:NA_WIDTH].astype(BF16)
    vb_ref[...] = kv[:, NA_WIDTH:].astype(BF16)
    g_ref[...] = jnp.dot(h, w_ref[:, GATE_OFF:], preferred_element_type=F32).astype(BF16)


def in_projection(x, mod3, gain, w_in):
    row = lambda i: (i, 0)
    return pl.pallas_call(
        _inproj_kernel,
        grid=(N_TOK // IN_TM,),
        in_specs=[
            pl.BlockSpec((IN_TM, D_MODEL), row),
            pl.BlockSpec((None, 1, MOD_WIDTH), lambda i: (_cond_row(i, IN_TM), 0, 0)),
            pl.BlockSpec((1, D_MODEL), lambda i: (0, 0)),
            pl.BlockSpec((D_MODEL, IN_WIDTH), lambda i: (0, 0)),
        ],
        out_specs=[
            pl.BlockSpec((IN_TM, ZA_WIDTH), row),
            pl.BlockSpec((IN_TM, NA_WIDTH), row),
            pl.BlockSpec((IN_TM, NA_WIDTH), row),
            pl.BlockSpec((IN_TM, NA_WIDTH), row),
            pl.BlockSpec((IN_TM, 2 * NA_WIDTH), row),
            pl.BlockSpec((IN_TM, 3 * D_MODEL), row),
        ],
        out_shape=[
            jax.ShapeDtypeStruct((N_TOK, ZA_WIDTH), BF16),
            jax.ShapeDtypeStruct((N_TOK, NA_WIDTH), BF16),
            jax.ShapeDtypeStruct((N_TOK, NA_WIDTH), BF16),
            jax.ShapeDtypeStruct((N_TOK, NA_WIDTH), BF16),
            jax.ShapeDtypeStruct((N_TOK, 2 * NA_WIDTH), F32),
            jax.ShapeDtypeStruct((N_TOK, 3 * D_MODEL), BF16),
        ],
        compiler_params=_params(1),
        name="in_projection",
    )(x, mod3, gain, w_in)


SGU_TM = 512
SGU_GD = SGU_WIDTH // SGU_GROUPS


def _sgu_kernel(u_ref, v_ref, gain_ref, ws_ref, bs_ref, o_ref):
    gain = gain_ref[...]
    for c in range(SGU_TM // CHUNK):
        rows = slice(c * CHUNK, (c + 1) * CHUNK)
        u = jax.nn.gelu(u_ref[rows, :].astype(F32))
        v = _rms(jax.nn.gelu(v_ref[rows, :].astype(F32)), gain).astype(BF16)
        for g in range(SGU_GROUPS):
            cols = slice(g * SGU_GD, (g + 1) * SGU_GD)
            mixed = jnp.dot(ws_ref[g], v[:, cols], preferred_element_type=F32) + bs_ref[:, g:g + 1]
            o_ref[rows, cols] = (u[:, cols] * mixed).astype(BF16)


def spatial_gating(za, gain, ws, bs_t):
    return pl.pallas_call(
        _sgu_kernel,
        grid=(N_TOK // SGU_TM,),
        in_specs=[
            pl.BlockSpec((SGU_TM, SGU_WIDTH), lambda i: (i, 0)),
            pl.BlockSpec((SGU_TM, SGU_WIDTH), lambda i: (i, 1)),
            pl.BlockSpec((1, SGU_WIDTH), lambda i: (0, 0)),
            pl.BlockSpec((SGU_GROUPS, CHUNK, CHUNK), lambda i: (0, 0, 0)),
            pl.BlockSpec((CHUNK, SGU_GROUPS), lambda i: (0, 0)),
        ],
        out_specs=pl.BlockSpec((SGU_TM, SGU_WIDTH), lambda i: (i, 0)),
        out_shape=jax.ShapeDtypeStruct((N_TOK, SGU_WIDTH), BF16),
        compiler_params=_params(1),
        name="spatial_gating",
    )(za, za, gain, ws, bs_t)


LRU_TC = 256
LRU_HALF = 256
SUB = 8
HALO = 8


def _lru_kernel(seq_len, xr_ref, gr_ref, cw_ref, cb_ref, wlo_ref, whi_ref, br_ref, bi_ref,
                lam_ref, h0_ref, y_ref, st_ref, xp_ref, a_ref, b_ref):
    n_chunks = seq_len // LRU_TC
    zeros = jnp.zeros((HALO, LRU_WIDTH), F32)
    xp_ref[0:HALO, :] = zeros
    xp_ref[seq_len + HALO:seq_len + 2 * HALO, :] = zeros

    def copy_in(c, carry):
        r0 = pl.multiple_of(c * LRU_TC, LRU_TC)
        xp_ref[pl.ds(r0 + HALO, LRU_TC), :] = xr_ref[pl.ds(r0, LRU_TC), :].astype(F32)
        return carry

    lax.fori_loop(0, n_chunks, copy_in, 0)

    cw = cw_ref[...]
    cb = cb_ref[...]
    win = LRU_TC + 2 * HALO
    sub_pos = lax.broadcasted_iota(jnp.int32, (LRU_TC, LRU_HALF), 0) % SUB
    neg_lam = -lam_ref[...]
    softplus = jnp.maximum(neg_lam, 0.0) + jnp.log1p(jnp.exp(-jnp.abs(neg_lam)))

    def local_scan(a, b, reverse):
        for s in (1, 2, 4):
            if reverse:
                a_n = pltpu.roll(a, LRU_TC - s, 0)
                b_n = pltpu.roll(b, LRU_TC - s, 0)
                keep = sub_pos < SUB - s
            else:
                a_n = pltpu.roll(a, s, 0)
                b_n = pltpu.roll(b, s, 0)
                keep = sub_pos >= s
            b = jnp.where(keep, a * b_n + b, b)
            a = jnp.where(keep, a * a_n, a)
        return a, b

    def gates(c, carry):
        r0 = pl.multiple_of(c * LRU_TC, LRU_TC)
        w = xp_ref[pl.ds(r0, win), :]
        xc = (cw[0:1, :] * pltpu.roll(w, 1, 0)[HALO:HALO + LRU_TC]
              + cw[1:2, :] * w[HALO:HALO + LRU_TC]
              + cw[2:3, :] * pltpu.roll(w, win - 1, 0)[HALO:HALO + LRU_TC]
              + cw[3:4, :] * pltpu.roll(w, win - 2, 0)[HALO:HALO + LRU_TC]) + cb
        xb = xc.astype(BF16)
        for half, w_ref in enumerate((wlo_ref, whi_ref)):
            cols = slice(half * LRU_HALF, (half + 1) * LRU_HALF)
            pre = jnp.dot(xb[:, cols], w_ref[...], preferred_element_type=F32)
            xh = xc[:, cols]
            for d in range(2):
                r = jax.nn.sigmoid(pre[:, (2 * d) * LRU_HALF:(2 * d + 1) * LRU_HALF] + br_ref[d:d + 1, cols])
                i = jax.nn.sigmoid(pre[:, (2 * d + 1) * LRU_HALF:(2 * d + 2) * LRU_HALF] + bi_ref[d:d + 1, cols])
                log_a = -LRU_C * r * softplus[d:d + 1, cols]
                a = jnp.exp(log_a)
                b = jnp.sqrt(-jnp.tanh(log_a) * (a * a + 1.0)) * (i * xh)
                a, b = local_scan(a, b, reverse=(d == 1))
                a_ref[d, pl.ds(r0, LRU_TC), cols] = a
                b_ref[d, pl.ds(r0, LRU_TC), cols] = b
        return carry

    lax.fori_loop(0, n_chunks, gates, 0)

    n_blocks = seq_len // SUB

    def fwd(j, h):
        r0 = pl.multiple_of(j * SUB, SUB)
        hb = a_ref[0, pl.ds(r0, SUB), :] * h + b_ref[0, pl.ds(r0, SUB), :]
        b_ref[0, pl.ds(r0, SUB), :] = hb
        return hb[SUB - 1:SUB, :]

    def bwd(j, h):
        r0 = pl.multiple_of((n_blocks - 1 - j) * SUB, SUB)
        hb = a_ref[1, pl.ds(r0, SUB), :] * h + b_ref[1, pl.ds(r0, SUB), :]
        b_ref[1, pl.ds(r0, SUB), :] = hb
        return hb[0:1, :]

    h_f = lax.fori_loop(0, n_blocks, fwd, h0_ref[0:1, :], unroll=4)
    h_b = lax.fori_loop(0, n_blocks, bwd, h0_ref[1:2, :], unroll=4)
    st_ref[0:1, :] = h_f
    st_ref[1:2, :] = h_b

    def merge(c, carry):
        r0 = pl.multiple_of(c * LRU_TC, LRU_TC)
        rows = pl.ds(r0, LRU_TC)
        y_ref[rows, :] = ((b_ref[0, rows, :] + b_ref[1, rows, :])
                          * jax.nn.gelu(gr_ref[rows, :].astype(F32))).astype(BF16)
        return carry

    lax.fori_loop(0, n_chunks, merge, 0)


def rglru(za, conv_w, conv_b, w_lo, w_hi, b_r, b_i, lam, h0, seq_len, n_seq, tok_off):
    blk0 = tok_off // seq_len
    full = lambda shape: pl.BlockSpec(shape, lambda i: (0,) * len(shape))
    return pl.pallas_call(
        functools.partial(_lru_kernel, seq_len),
        grid=(n_seq,),
        in_specs=[
            pl.BlockSpec((seq_len, LRU_WIDTH), lambda i: (i + blk0, 2)),
            pl.BlockSpec((seq_len, LRU_WIDTH), lambda i: (i + blk0, 3)),
            full((CONV_WIDTH, LRU_WIDTH)),
            full((1, LRU_WIDTH)),
            full((LRU_HALF, 4 * LRU_HALF)),
            full((LRU_HALF, 4 * LRU_HALF)),
            full((2, LRU_WIDTH)),
            full((2, LRU_WIDTH)),
            full((2, LRU_WIDTH)),
            pl.BlockSpec((None, 2, LRU_WIDTH), lambda i: (i, 0, 0)),
        ],
        out_specs=[
            pl.BlockSpec((seq_len, LRU_WIDTH), lambda i: (i, 0)),
            pl.BlockSpec((None, 2, LRU_WIDTH), lambda i: (i, 0, 0)),
        ],
        out_shape=[
            jax.ShapeDtypeStruct((n_seq * seq_len, LRU_WIDTH), BF16),
            jax.ShapeDtypeStruct((n_seq, 2, LRU_WIDTH), F32),
        ],
        scratch_shapes=[
            pltpu.VMEM((seq_len + 2 * HALO, LRU_WIDTH), F32),
            pltpu.VMEM((2, seq_len, LRU_WIDTH), F32),
            pltpu.VMEM((2, seq_len, LRU_WIDTH), F32),
        ],
        compiler_params=_params(1),
        name=f"rglru_{seq_len}",
    )(za, za, conv_w, conv_b, w_lo, w_hi, b_r, b_i, lam, h0)


def _block_diag_gate_weights(w_r, w_i):
    per_half = LRU_BLOCKS // 2
    eye = jnp.eye(per_half, dtype=w_r.dtype)

    def bd(w):
        return jnp.einsum("hij,hg->higj", w, eye).reshape(LRU_HALF, LRU_HALF)

    tiles = []
    for half in range(2):
        sl = slice(half * per_half, (half + 1) * per_half)
        tiles.append(jnp.concatenate([bd(w_r[0, sl]), bd(w_i[0, sl]), bd(w_r[1, sl]), bd(w_i[1, sl])], axis=1))
    return tiles[0].astype(BF16), tiles[1].astype(BF16)


ATT_SCALE = NA_HEAD_DIM ** -0.5


def _ctx_attn_kernel(q_ref, k_ref, v_ref, o_ref):
    for h in range(NA_HEADS):
        cols = slice(h * NA_HEAD_DIM, (h + 1) * NA_HEAD_DIM)
        s = _bdot_t(q_ref[:, cols], k_ref[:, cols]) * ATT_SCALE
        p = jnp.exp(s - jnp.max(s, axis=-1, keepdims=True))
        denom = jnp.sum(p, axis=-1, keepdims=True)
        o = jnp.dot(p.astype(BF16), v_ref[:, cols], preferred_element_type=F32)
        o_ref[:, cols] = (o / denom).astype(BF16)


def context_attention(q, kb, vb):
    spec = pl.BlockSpec((SEQ, NA_WIDTH), lambda i: (i, 0))
    return pl.pallas_call(
        _ctx_attn_kernel,
        grid=(BATCH,),
        in_specs=[spec, spec, spec],
        out_specs=spec,
        out_shape=jax.ShapeDtypeStruct((N_CTX, NA_WIDTH), BF16),
        compiler_params=_params(1),
        name="context_attention",
    )(q, kb, vb)


NA_QROWS = 4
NA_TQ = NA_QROWS * GRID_W
NA_KROWS = NA_QROWS + WIN_ROWS
NA_TK = NA_KROWS * GRID_W
NA_QBLOCKS = GRID_ROWS // NA_QROWS


def _lat_attn_kernel(q_ref, k_ref, v_ref, ck_ref, cv_ref, tab_ref, o_ref, bias_ref):
    qb = pl.program_id(1)
    first_krow = jnp.clip(qb * NA_QROWS - WIN_ROWS // 2, 0, GRID_ROWS - NA_KROWS)
    neg = jnp.full((GRID_W, GRID_W), NEG_INF, F32)
    for i in range(NA_QROWS):
        qrow = qb * NA_QROWS + i
        win0 = jnp.clip(qrow - WIN_ROWS // 2, 0, GRID_ROWS - WIN_ROWS)
        for kr in range(NA_KROWS):
            krow = first_krow + kr
            inside = jnp.logical_and(krow >= win0, krow < win0 + WIN_ROWS)
            d_row = jnp.clip(krow - qrow + WIN_ROWS - 1, 0, 2 * WIN_ROWS - 2)
            for h in range(NA_HEADS):
                bias_ref[h, i * GRID_W:(i + 1) * GRID_W, kr * GRID_W:(kr + 1) * GRID_W] = (
                    jnp.where(inside, tab_ref[h, d_row], neg))

    k0 = pl.multiple_of(first_krow * GRID_W, GRID_W)
    for h in range(NA_HEADS):
        cols = slice(h * NA_HEAD_DIM, (h + 1) * NA_HEAD_DIM)
        qh = q_ref[:, cols]
        s_loc = _bdot_t(qh, k_ref[pl.ds(k0, NA_TK), cols]) * ATT_SCALE + bias_ref[h]
        s_ctx = _bdot_t(qh, ck_ref[:, cols]) * ATT_SCALE
        m = jnp.maximum(jnp.max(s_loc, axis=-1, keepdims=True), jnp.max(s_ctx, axis=-1, keepdims=True))
        p_loc = jnp.exp(s_loc - m)
        p_ctx = jnp.exp(s_ctx - m)
        denom = jnp.sum(p_loc, axis=-1, keepdims=True) + jnp.sum(p_ctx, axis=-1, keepdims=True)
        o = (jnp.dot(p_loc.astype(BF16), v_ref[pl.ds(k0, NA_TK), cols], preferred_element_type=F32)
             + _bdot(p_ctx, cv_ref[:, cols]))
        o_ref[:, cols] = (o / denom).astype(BF16)


def latent_attention(q, kb, vb, cache_k, cache_v, table):
    q_blk0 = N_CTX // NA_TQ
    s_blk0 = N_CTX // DEC_SEQ
    return pl.pallas_call(
        _lat_attn_kernel,
        grid=(DEC_BATCH, NA_QBLOCKS),
        in_specs=[
            pl.BlockSpec((NA_TQ, NA_WIDTH), lambda b, m: (q_blk0 + b * NA_QBLOCKS + m, 0)),
            pl.BlockSpec((DEC_SEQ, NA_WIDTH), lambda b, m: (s_blk0 + b, 0)),
            pl.BlockSpec((DEC_SEQ, NA_WIDTH), lambda b, m: (s_blk0 + b, 0)),
            pl.BlockSpec((None, PAST_LEN, NA_WIDTH), lambda b, m: (b, 0, 0)),
            pl.BlockSpec((None, PAST_LEN, NA_WIDTH), lambda b, m: (b, 0, 0)),
            pl.BlockSpec((NA_HEADS, 2 * WIN_ROWS - 1, GRID_W, GRID_W), lambda b, m: (0, 0, 0, 0)),
        ],
        out_specs=pl.BlockSpec((NA_TQ, NA_WIDTH), lambda b, m: (b * NA_QBLOCKS + m, 0)),
        out_shape=jax.ShapeDtypeStruct((N_LAT, NA_WIDTH), BF16),
        scratch_shapes=[pltpu.VMEM((NA_HEADS, NA_TQ, NA_TK), F32)],
        compiler_params=_params(2),
        name="latent_attention",
    )(q, kb, vb, cache_k, cache_v, table)


def _rel_bias_table(rpb):
    cols = np.arange(GRID_W)
    col_start = np.clip(cols - WIN_COLS // 2, 0, GRID_W - WIN_COLS)
    col_ok = (cols[None, :] >= col_start[:, None]) & (cols[None, :] < col_start[:, None] + WIN_COLS)
    d_col = np.clip(cols[None, :] - cols[:, None], 1 - WIN_COLS, WIN_COLS - 1) + WIN_COLS - 1
    return jnp.where(col_ok[None, None], rpb[:, :, d_col].astype(F32), NEG_INF)


MERGE_TM = 256


def _merge_kernel(x_ref, mod_ref, ya_ref, yb_ref, yc_ref, g_ref, wa_ref, wb_ref, wc_ref, wo_ref, o_ref):
    g = jax.nn.sigmoid(g_ref[...].astype(F32))
    merged = (g[:, 0:D_MODEL] * jnp.dot(ya_ref[...], wa_ref[...], preferred_element_type=F32)
              + g[:, D_MODEL:2 * D_MODEL] * jnp.dot(yb_ref[...], wb_ref[...], preferred_element_type=F32)
              + g[:, 2 * D_MODEL:] * jnp.dot(yc_ref[...], wc_ref[...], preferred_element_type=F32))
    y = jnp.dot(merged.astype(BF16), wo_ref[...], preferred_element_type=F32)
    gate = mod_ref[:, 2 * D_MODEL:3 * D_MODEL]
    o_ref[...] = x_ref[...] + gate * y


def merge_branches(x, mod3, ya, yb, yc, gates, wa, wb, wc, wo):
    row = lambda i: (i, 0)
    const = lambda i: (0, 0)
    return pl.pallas_call(
        _merge_kernel,
        grid=(N_TOK // MERGE_TM,),
        in_specs=[
            pl.BlockSpec((MERGE_TM, D_MODEL), row),
            pl.BlockSpec((None, 1, MOD_WIDTH), lambda i: (_cond_row(i, MERGE_TM), 0, 0)),
            pl.BlockSpec((MERGE_TM, SGU_WIDTH), row),
            pl.BlockSpec((MERGE_TM, LRU_WIDTH), row),
            pl.BlockSpec((MERGE_TM, NA_WIDTH), row),
            pl.BlockSpec((MERGE_TM, 3 * D_MODEL), row),
            pl.BlockSpec((SGU_WIDTH, D_MODEL), const),
            pl.BlockSpec((LRU_WIDTH, D_MODEL), const),
            pl.BlockSpec((NA_WIDTH, D_MODEL), const),
            pl.BlockSpec((D_MODEL, D_MODEL), const),
        ],
        out_specs=pl.BlockSpec((MERGE_TM, D_MODEL), row),
        out_shape=jax.ShapeDtypeStruct((N_TOK, D_MODEL), F32),
        compiler_params=_params(1),
        name="merge_branches",
    )(x, mod3, ya, yb, yc, gates, wa, wb, wc, wo)


RT_TM = 512
RT_ROWS = 32
N_BUCKETS = N_GROUPS
CNT_ROWS = 8
COMB_W = 128
HX_W = D_MODEL + COMB_W
EXP_TM = 256
EXP_TILES = N_TOK // EXP_TM + N_BUCKETS
HS_ROWS = EXP_TILES * EXP_TM


def _split_bf16(x):
    hi = x.astype(BF16)
    return hi, (x - hi.astype(F32)).astype(BF16)


def _router_kernel(x_ref, mod_ref, gain_ref, w_ref, b_ref, tri_ref, hx_ref, bucket_ref, rank_ref, cnt_out_ref,
                   cnt_ref, cpad_ref):
    @pl.when(pl.program_id(0) == 0)
    def _():
        cnt_ref[...] = jnp.zeros_like(cnt_ref)
        cpad_ref[...] = jnp.zeros_like(cpad_ref)

    m = mod_ref[...]
    shift, scale = m[:, 3 * D_MODEL:4 * D_MODEL], m[:, 4 * D_MODEL:5 * D_MODEL]
    h = _rms(x_ref[...], gain_ref[...]) * (1.0 + scale) + shift
    hx_ref[:, 0:D_MODEL] = h
    h_hi, h_lo = _split_bf16(h)
    w_hi, w_lo = _split_bf16(w_ref[...])
    dims = (((1,), (1,)), ((), ()))
    logits = (lax.dot_general(w_hi, h_hi, dims, preferred_element_type=F32)
              + lax.dot_general(w_hi, h_lo, dims, preferred_element_type=F32)
              + lax.dot_general(w_lo, h_hi, dims, preferred_element_type=F32)) + b_ref[...]
    gl = [logits[g:g + 1, :] for g in range(N_GROUPS)]
    gmax = functools.reduce(jnp.maximum, gl)
    gid = jnp.full(gmax.shape, N_GROUPS - 1, jnp.int32)
    for g in reversed(range(N_GROUPS - 1)):
        gid = jnp.where(gl[g] == gmax, g, gid)
    p_grp = 1.0 / functools.reduce(jnp.add, [jnp.exp(v - gmax) for v in gl])
    el = []
    for e in range(EXPERTS_PER_GROUP):
        v = logits[8 + e:9 + e, :]
        for g in range(1, N_GROUPS):
            row = 8 + g * EXPERTS_PER_GROUP + e
            v = jnp.where(gid == g, logits[row:row + 1, :], v)
        el.append(v)
    top1 = functools.reduce(jnp.maximum, el)
    idx1 = jnp.full(top1.shape, EXPERTS_PER_GROUP - 1, jnp.int32)
    for e in reversed(range(EXPERTS_PER_GROUP - 1)):
        idx1 = jnp.where(el[e] == top1, e, idx1)
    rest = [jnp.where(idx1 == e, -jnp.inf, el[e]) for e in range(EXPERTS_PER_GROUP)]
    top2 = functools.reduce(jnp.maximum, rest)
    idx2 = jnp.full(top1.shape, EXPERTS_PER_GROUP - 1, jnp.int32)
    for e in reversed(range(EXPERTS_PER_GROUP - 1)):
        idx2 = jnp.where(rest[e] == top2, e, idx2)
    e2 = jnp.exp(top2 - top1)
    w1 = p_grp / (1.0 + e2)
    w2 = p_grp * e2 / (1.0 + e2)
    for e in range(EXPERTS_PER_GROUP):
        cpad_ref[e:e + 1, :] = jnp.where(idx1 == e, w1, 0.0) + jnp.where(idx2 == e, w2, 0.0)
    hx_ref[:, D_MODEL:] = cpad_ref[...].T
    bucket_ref[...] = gid
    sub = lax.broadcasted_iota(jnp.int32, (CNT_ROWS, RT_TM), 0)
    onehot = jnp.where(sub == gid, 1.0, 0.0)
    seen = jnp.dot(onehot.astype(BF16), tri_ref[...], preferred_element_type=F32)
    cnt = cnt_ref[...]
    rank_ref[...] = jnp.sum(onehot * (seen - 1.0 + cnt[:, 0:1]), axis=0, keepdims=True).astype(jnp.int32)
    cnt = cnt + jnp.sum(onehot, axis=1, keepdims=True)
    cnt_ref[...] = cnt
    cnt_out_ref[...] = cnt.astype(jnp.int32)


def router(x, mod3, gain, w_rt, b_rt):
    tri = jnp.asarray(np.triu(np.ones((RT_TM, RT_TM), np.float32)), BF16)
    return pl.pallas_call(
        _router_kernel,
        grid=(N_TOK // RT_TM,),
        in_specs=[
            pl.BlockSpec((RT_TM, D_MODEL), lambda i: (i, 0)),
            pl.BlockSpec((None, 1, MOD_WIDTH), lambda i: (_cond_row(i, RT_TM), 0, 0)),
            pl.BlockSpec((1, D_MODEL), lambda i: (0, 0)),
            pl.BlockSpec((RT_ROWS, D_MODEL), lambda i: (0, 0)),
            pl.BlockSpec((RT_ROWS, 1), lambda i: (0, 0)),
            pl.BlockSpec((RT_TM, RT_TM), lambda i: (0, 0)),
        ],
        out_specs=[
            pl.BlockSpec((RT_TM, HX_W), lambda i: (i, 0)),
            pl.BlockSpec((1, RT_TM), lambda i: (0, i)),
            pl.BlockSpec((1, RT_TM), lambda i: (0, i)),
            pl.BlockSpec((CNT_ROWS, COMB_W), lambda i: (0, 0)),
        ],
        out_shape=[
            jax.ShapeDtypeStruct((N_TOK, HX_W), F32),
            jax.ShapeDtypeStruct((1, N_TOK), jnp.int32),
            jax.ShapeDtypeStruct((1, N_TOK), jnp.int32),
            jax.ShapeDtypeStruct((CNT_ROWS, COMB_W), jnp.int32),
        ],
        scratch_shapes=[pltpu.VMEM((CNT_ROWS, COMB_W), F32), pltpu.VMEM((COMB_W, RT_TM), F32)],
        compiler_params=_params(1),
        name="router",
    )(x, mod3, gain, w_rt, b_rt, tri)


def _router_weights(w_grp, b_grp, w_exp, b_exp):
    w = jnp.zeros((RT_ROWS, D_MODEL), F32).at[0:N_GROUPS].set(w_grp.T).at[8:8 + N_EXPERTS].set(w_exp.T)
    b = jnp.zeros((RT_ROWS, 1), F32).at[0:N_GROUPS, 0].set(b_grp).at[8:8 + N_EXPERTS, 0].set(b_exp)
    return w, b


DISP_TM = 512


def _sorted_row(bucket_ref, rank_ref, start_ref, k):
    return start_ref[bucket_ref[k]] + rank_ref[k]


def _dispatch_kernel(bucket_ref, rank_ref, start_ref, cnt_ref, nt_ref, hx_ref, hs_ref, zero_ref, sem, zsem):
    i = pl.program_id(0)
    base = i * DISP_TM

    def issue(r, carry):
        row = _sorted_row(bucket_ref, rank_ref, start_ref, base + r)
        pltpu.make_async_copy(hx_ref.at[pl.ds(r, 1), :], hs_ref.at[pl.ds(row, 1), :], sem).start()
        return carry

    lax.fori_loop(0, DISP_TM, issue, 0, unroll=8)
    pltpu.make_async_copy(hx_ref, hs_ref.at[pl.ds(0, DISP_TM), :], sem).wait()

    @pl.when(i == pl.num_programs(0) - 1)
    def _():
        zero_ref[...] = jnp.zeros_like(zero_ref)

        def pad_copy(row):
            return pltpu.make_async_copy(zero_ref.at[pl.ds(0, 1), :], hs_ref.at[pl.ds(row, 1), :], zsem)

        def pad_issue(row, carry):
            pad_copy(row).start()
            return carry

        def pad_wait(row, carry):
            pad_copy(row).wait()
            return carry

        for g in range(N_BUCKETS):
            lo = start_ref[g] + cnt_ref[g]
            hi = start_ref[g] + ((cnt_ref[g] + EXP_TM - 1) // EXP_TM) * EXP_TM
            lax.fori_loop(lo, hi, pad_issue, 0)
            lax.fori_loop(lo, hi, pad_wait, 0)

        def tile_copy(t):
            return pltpu.make_async_copy(zero_ref, hs_ref.at[pl.ds(pl.multiple_of(t * EXP_TM, EXP_TM), EXP_TM), :],
                                         zsem)

        def tile_issue(t, carry):
            tile_copy(t).start()
            return carry

        def tile_wait(t, carry):
            tile_copy(t).wait()
            return carry

        lax.fori_loop(nt_ref[0], EXP_TILES, tile_issue, 0)
        lax.fori_loop(nt_ref[0], EXP_TILES, tile_wait, 0)


def dispatch(bucket, rank, starts, counts, n_tiles, hx):
    return pl.pallas_call(
        _dispatch_kernel,
        grid_spec=pltpu.PrefetchScalarGridSpec(
            num_scalar_prefetch=5,
            grid=(N_TOK // DISP_TM,),
            in_specs=[pl.BlockSpec((DISP_TM, HX_W), lambda i, *_: (i, 0))],
            out_specs=pl.BlockSpec(memory_space=pl.ANY),
            scratch_shapes=[pltpu.VMEM((EXP_TM, HX_W), F32), pltpu.SemaphoreType.DMA(()),
                            pltpu.SemaphoreType.DMA(())],
        ),
        out_shape=jax.ShapeDtypeStruct((HS_ROWS, HX_W), F32),
        compiler_params=_params(1),
        name="dispatch",
    )(bucket, rank, starts, counts, n_tiles, hx)


def _experts_kernel(tg_ref, nt_ref, hs_ref, w1_ref, w3_ref, w2_ref, ys_ref):
    t = pl.program_id(0)

    @pl.when(t < nt_ref[0])
    def _():
        h = hs_ref[:, 0:D_MODEL].astype(BF16)
        c = hs_ref[:, D_MODEL:]
        acc = None
        for e in range(EXPERTS_PER_GROUP):
            a = jnp.dot(h, w1_ref[e], preferred_element_type=F32)
            b = jnp.dot(h, w3_ref[e], preferred_element_type=F32)
            hid = (a * jax.nn.sigmoid(a)) * b * c[:, e:e + 1]
            y = jnp.dot(hid.astype(BF16), w2_ref[e], preferred_element_type=F32)
            acc = y if acc is None else acc + y
        ys_ref[...] = acc

    @pl.when(t >= nt_ref[0])
    def _():
        ys_ref[...] = jnp.zeros_like(ys_ref)


def experts(tile_group, n_tiles, hs, w1, w3, w2):
    wspec = lambda shape: pl.BlockSpec(shape, lambda t, tg, nt: (tg[t], 0, 0))
    return pl.pallas_call(
        _experts_kernel,
        grid_spec=pltpu.PrefetchScalarGridSpec(
            num_scalar_prefetch=2,
            grid=(EXP_TILES,),
            in_specs=[
                pl.BlockSpec((EXP_TM, HX_W), lambda t, tg, nt: (t, 0)),
                wspec((EXPERTS_PER_GROUP, D_MODEL, D_EXPERT)),
                wspec((EXPERTS_PER_GROUP, D_MODEL, D_EXPERT)),
                wspec((EXPERTS_PER_GROUP, D_EXPERT, D_MODEL)),
            ],
            out_specs=pl.BlockSpec((EXP_TM, D_MODEL), lambda t, tg, nt: (t, 0)),
        ),
        out_shape=jax.ShapeDtypeStruct((HS_ROWS, D_MODEL), F32),
        compiler_params=_params(1),
        name="experts",
    )(tile_group, n_tiles, hs, w1, w3, w2)


def _expert_tiles(counts):
    tiles = (counts + EXP_TM - 1) // EXP_TM
    ends = jnp.cumsum(tiles)
    t = jnp.arange(EXP_TILES, dtype=jnp.int32)
    group = jnp.minimum(jnp.sum(t[:, None] >= ends[None, :], axis=1), N_BUCKETS - 1).astype(jnp.int32)
    return ((ends - tiles) * EXP_TM).astype(jnp.int32), group, ends[-1:].astype(jnp.int32)


RES_TM = 512


def _moe_residual_kernel(final, bucket_ref, rank_ref, start_ref, x_ref, mod_ref, gain_ref, ys_ref, o_ref,
                         ybuf_ref, sem):
    base = pl.program_id(0) * RES_TM

    def issue(r, carry):
        row = _sorted_row(bucket_ref, rank_ref, start_ref, base + r)
        pltpu.make_async_copy(ys_ref.at[pl.ds(row, 1), :], ybuf_ref.at[pl.ds(r, 1), :], sem).start()
        return carry

    lax.fori_loop(0, RES_TM, issue, 0, unroll=8)
    pltpu.make_async_copy(ys_ref.at[pl.ds(0, RES_TM), :], ybuf_ref, sem).wait()
    x = x_ref[...] + mod_ref[:, 5 * D_MODEL:6 * D_MODEL] * ybuf_ref[...]
    o_ref[...] = _rms(x, gain_ref[...]) if final else x


def moe_residual(bucket, rank, starts, x, mod3, gain, ys, final):
    return pl.pallas_call(
        functools.partial(_moe_residual_kernel, final),
        grid_spec=pltpu.PrefetchScalarGridSpec(
            num_scalar_prefetch=3,
            grid=(N_TOK // RES_TM,),
            in_specs=[
                pl.BlockSpec((RES_TM, D_MODEL), lambda i, *_: (i, 0)),
                pl.BlockSpec((None, 1, MOD_WIDTH), lambda i, *_: (_cond_row(i, RES_TM), 0, 0)),
                pl.BlockSpec((1, D_MODEL), lambda i, *_: (0, 0)),
                pl.BlockSpec(memory_space=pl.ANY),
            ],
            out_specs=pl.BlockSpec((RES_TM, D_MODEL), lambda i, *_: (i, 0)),
            scratch_shapes=[pltpu.VMEM((RES_TM, D_MODEL), F32), pltpu.SemaphoreType.DMA(())],
        ),
        out_shape=jax.ShapeDtypeStruct((N_TOK, D_MODEL), F32),
        compiler_params=_params(1),
        name="moe_residual_final" if final else "moe_residual",
    )(bucket, rank, starts, x, mod3, gain, ys)


def kernel(x_prompt, x_sample, cache_k, cache_v, state_lru, c, c_ctx, w_mod, b_mod, norm_mix, norm_ffn, w_in, sgu_norm, sgu_w, sgu_b, lru_conv_w, lru_conv_b, lru_w_r, lru_b_r, lru_w_i, lru_b_i, lru_lambda, na_rpb, w_branch_sgu, w_branch_lru, w_branch_na, w_out, moe_w_group, moe_b_group, moe_w_expert, moe_b_expert, moe_w1, moe_w3, moe_w2, final_norm_gain):
    x = jnp.concatenate([x_prompt.reshape(N_CTX, D_MODEL), x_sample.reshape(N_LAT, D_MODEL)], axis=0)
    cond = jnp.zeros((N_COND, D_MODEL), F32).at[0].set(c_ctx).at[1:1 + DEC_BATCH].set(c)
    mod = modulation(cond, w_mod, b_mod)
    zero_state = jnp.zeros((BATCH, 2, LRU_WIDTH), F32)
    ks, vs, ss = [], [], []
    for l in range(DEPTH):
        mod3 = mod[l].reshape(N_COND, 1, MOD_WIDTH)
        za, q, kb, vb, kv32, gates = in_projection(x, mod3, norm_mix[l].reshape(1, D_MODEL), w_in[l].astype(BF16))
        ya = spatial_gating(za, sgu_norm[l].reshape(1, SGU_WIDTH), sgu_w[l].astype(BF16), sgu_b[l].T)
        w_lo, w_hi = _block_diag_gate_weights(lru_w_r[l], lru_w_i[l])
        lru_args = (lru_conv_w[l], lru_conv_b[l].reshape(1, LRU_WIDTH), w_lo, w_hi, lru_b_r[l], lru_b_i[l],
                    lru_lambda[l])
        yb_ctx, st_ctx = rglru(za, *lru_args, zero_state, SEQ, BATCH, 0)
        yb_lat, _ = rglru(za, *lru_args, state_lru[:, l], DEC_SEQ, DEC_BATCH, N_CTX)
        yc_ctx = context_attention(q, kb, vb)
        yc_lat = latent_attention(q, kb, vb, cache_k[:, l].reshape(DEC_BATCH, PAST_LEN, NA_WIDTH),
                                  cache_v[:, l].reshape(DEC_BATCH, PAST_LEN, NA_WIDTH), _rel_bias_table(na_rpb[l]))
        x = merge_branches(x, mod3, ya, jnp.concatenate([yb_ctx, yb_lat], axis=0),
                           jnp.concatenate([yc_ctx, yc_lat], axis=0), gates,
                           w_branch_sgu[l].astype(BF16), w_branch_lru[l].astype(BF16),
                           w_branch_na[l].astype(BF16), w_out[l].astype(BF16))
        w_rt, b_rt = _router_weights(moe_w_group[l], moe_b_group[l], moe_w_expert[l], moe_b_expert[l])
        hx, bucket, rank, counts = router(x, mod3, norm_ffn[l].reshape(1, D_MODEL), w_rt, b_rt)
        bucket, rank, counts = bucket.reshape(N_TOK), rank.reshape(N_TOK), counts[:N_BUCKETS, 0]
        starts, tile_group, n_tiles = _expert_tiles(counts)
        hs = dispatch(bucket, rank, starts, counts, n_tiles, hx)
        ys = experts(tile_group, n_tiles, hs, moe_w1[l].astype(BF16), moe_w3[l].astype(BF16),
                     moe_w2[l].astype(BF16))
        last = l == DEPTH - 1
        x = moe_residual(bucket, rank, starts, x, mod3, final_norm_gain.reshape(1, D_MODEL), ys, last)
        ks.append(kv32[:N_CTX, :NA_WIDTH].reshape(BATCH, SEQ, NA_HEADS, NA_HEAD_DIM))
        vs.append(kv32[:N_CTX, NA_WIDTH:].reshape(BATCH, SEQ, NA_HEADS, NA_HEAD_DIM))
        ss.append(st_ctx)
    y = x
    return (y[:N_CTX].reshape(BATCH, SEQ, D_MODEL), y[N_CTX:].reshape(DEC_BATCH, DEC_SEQ, D_MODEL),
            jnp.stack(ks, axis=1), jnp.stack(vs, axis=1), jnp.stack(ss, axis=1))
```

```python
import functools

import jax
import jax.numpy as jnp
import numpy as np
from jax import lax
from jax.experimental import pallas as pl
from jax.experimental.pallas import tpu as pltpu

F32 = jnp.float32
BF16 = jnp.bfloat16

D_MODEL = 1024
BATCH = 16
SEQ = 256
DEPTH = 2
DEC_BATCH = 4
DEC_SEQ = 2048
PAST_LEN = 512
GRID_W = 64
CHUNK = 128
SGU_WIDTH = 512
SGU_GROUPS = 4
LRU_WIDTH = 512
LRU_BLOCKS = 8
CONV_WIDTH = 4
LRU_C = 8.0
NA_HEADS = 8
NA_HEAD_DIM = 64
NA_WIDTH = 512
WIN_ROWS = 8
WIN_COLS = 16
N_GROUPS = 4
EXPERTS_PER_GROUP = 4
N_EXPERTS = 16
D_EXPERT = 512
IN_WIDTH = 6656
EPS = 1e-6
NEG_INF = -1e30

N_CTX = BATCH * SEQ
N_LAT = DEC_BATCH * DEC_SEQ
N_TOK = N_CTX + N_LAT
N_COND = 8
MOD_WIDTH = 6 * D_MODEL
GRID_ROWS = DEC_SEQ // GRID_W

VMEM_LIMIT_BYTES = 56 * 1024 * 1024


def _params(n_axes):
    return pltpu.CompilerParams(dimension_semantics=("arbitrary",) * n_axes,
                                vmem_limit_bytes=VMEM_LIMIT_BYTES)


def _cond_row(tile, tile_rows):
    tok = tile * tile_rows
    return jnp.where(tok < N_CTX, 0, 1 + (tok - N_CTX) // DEC_SEQ)


def _rms(x, gain):
    return x * lax.rsqrt(jnp.mean(x * x, axis=-1, keepdims=True) + EPS) * gain


def _bdot(a, b):
    return jnp.dot(a.astype(BF16), b.astype(BF16), preferred_element_type=F32)


def _bdot_t(a, b):
    return lax.dot_general(a.astype(BF16), b.astype(BF16), (((1,), (1,)), ((), ())),
                           preferred_element_type=F32)


MOD_TN = 1536


def _mod_kernel(cond_ref, w_ref, b_ref, o_ref):
    c = cond_ref[...]
    s = c * jax.nn.sigmoid(c)
    o_ref[...] = _bdot(s, w_ref[...]) + b_ref[...]


def modulation(cond, w_mod, b_mod):
    return pl.pallas_call(
        _mod_kernel,
        grid=(DEPTH, MOD_WIDTH // MOD_TN),
        in_specs=[
            pl.BlockSpec((N_COND, D_MODEL), lambda l, j: (0, 0)),
            pl.BlockSpec((None, D_MODEL, MOD_TN), lambda l, j: (l, 0, j)),
            pl.BlockSpec((None, 1, MOD_TN), lambda l, j: (l, 0, j)),
        ],
        out_specs=pl.BlockSpec((None, N_COND, MOD_TN), lambda l, j: (l, 0, j)),
        out_shape=jax.ShapeDtypeStruct((DEPTH, N_COND, MOD_WIDTH), F32),
        compiler_params=_params(2),
        name="modulation",
    )(cond, w_mod, b_mod.reshape(DEPTH, 1, MOD_WIDTH))


IN_TM = 256
IN_CTX_TILES = N_CTX // IN_TM
ZA_WIDTH = 4 * 512
KV_OFF = ZA_WIDTH + NA_WIDTH
GATE_OFF = KV_OFF + 2 * NA_WIDTH
W_CHUNK = 512
Q_SCALE = NA_HEAD_DIM ** -0.5 * float(np.log2(np.e))


def _load_cast(chunks, stage_ref, sem_ref):
    def copy(j):
        return pltpu.make_async_copy(chunks[j][0], stage_ref.at[j % 2], sem_ref.at[j % 2])

    copy(0).start()
    for j in range(len(chunks)):
        if j + 1 < len(chunks):
            copy(j + 1).start()
        copy(j).wait()
        chunks[j][1](stage_ref[j % 2].astype(BF16))


def _token_tile(i, n_ctx_tiles, refs):
    if len(refs) == 1:
        return refs[0][...]
    return jnp.where(i < n_ctx_tiles, refs[0][...], refs[1][...])


def _split_specs(block, n_ctx_tiles):
    return [pl.BlockSpec(block, lambda i, *_: (jnp.minimum(i, n_ctx_tiles - 1), 0)),
            pl.BlockSpec(block, lambda i, *_: (jnp.maximum(i - n_ctx_tiles, 0), 0))]


def _inproj_kernel(layer, n_x, *refs):
    x_refs, (mod_ref, gain_ref, w_hbm) = refs[:n_x], refs[n_x:n_x + 3]
    za_ref, q_ref, kb_ref, vb_ref, kc_ref, vc_ref, g_ref, w_ref, stage_ref, sem_ref = refs[n_x + 3:]
    i = pl.program_id(0)

    @pl.when(i == 0)
    def _():
        def store(c):
            def st(v):
                w_ref[:, c * W_CHUNK:(c + 1) * W_CHUNK] = v
            return st

        _load_cast([(w_hbm.at[layer, :, pl.ds(c * W_CHUNK, W_CHUNK)], store(c)) for c in range(IN_WIDTH // W_CHUNK)],
                   stage_ref, sem_ref)

    m = mod_ref[...]
    shift, scale = m[:, 0:D_MODEL], m[:, D_MODEL:2 * D_MODEL]
    h = (_rms(_token_tile(i, IN_CTX_TILES, x_refs), gain_ref[...]) * (1.0 + scale) + shift).astype(BF16)
    za_ref[...] = jnp.dot(h, w_ref[:, 0:ZA_WIDTH], preferred_element_type=F32).astype(BF16)
    q_ref[...] = (jnp.dot(h, w_ref[:, ZA_WIDTH:KV_OFF], preferred_element_type=F32) * Q_SCALE).astype(BF16)
    kv = jnp.dot(h, w_ref[:, KV_OFF:GATE_OFF], preferred_element_type=F32)
    kb_ref[...] = kv[:, 0:NA_WIDTH].astype(BF16)
    vb_ref[...] = kv[:, NA_WIDTH:].astype(BF16)
    g_ref[...] = jnp.dot(h, w_ref[:, GATE_OFF:], preferred_element_type=F32).astype(BF16)

    @pl.when(i < IN_CTX_TILES)
    def _():
        kc_ref[...] = kv[:, 0:NA_WIDTH]
        vc_ref[...] = kv[:, NA_WIDTH:]


def in_projection(xs, mod3, gain, w_in, layer):
    row = lambda i: (i, 0)
    ctx_row = lambda i: (jnp.minimum(i, IN_CTX_TILES - 1), 0)
    x_specs = ([pl.BlockSpec((IN_TM, D_MODEL), row)] if len(xs) == 1
               else _split_specs((IN_TM, D_MODEL), IN_CTX_TILES))
    return pl.pallas_call(
        functools.partial(_inproj_kernel, layer, len(xs)),
        grid=(N_TOK // IN_TM,),
        in_specs=x_specs + [
            pl.BlockSpec((None, 1, MOD_WIDTH), lambda i: (_cond_row(i, IN_TM), 0, 0)),
            pl.BlockSpec((1, D_MODEL), lambda i: (0, 0)),
            pl.BlockSpec(memory_space=pl.ANY),
        ],
        out_specs=[
            pl.BlockSpec((IN_TM, ZA_WIDTH), row),
            pl.BlockSpec((IN_TM, NA_WIDTH), row),
            pl.BlockSpec((IN_TM, NA_WIDTH), row),
            pl.BlockSpec((IN_TM, NA_WIDTH), row),
            pl.BlockSpec((IN_TM, NA_WIDTH), ctx_row),
            pl.BlockSpec((IN_TM, NA_WIDTH), ctx_row),
            pl.BlockSpec((IN_TM, 3 * D_MODEL), row),
        ],
        out_shape=[
            jax.ShapeDtypeStruct((N_TOK, ZA_WIDTH), BF16),
            jax.ShapeDtypeStruct((N_TOK, NA_WIDTH), BF16),
            jax.ShapeDtypeStruct((N_TOK, NA_WIDTH), BF16),
            jax.ShapeDtypeStruct((N_TOK, NA_WIDTH), BF16),
            jax.ShapeDtypeStruct((N_CTX, NA_WIDTH), F32),
            jax.ShapeDtypeStruct((N_CTX, NA_WIDTH), F32),
            jax.ShapeDtypeStruct((N_TOK, 3 * D_MODEL), BF16),
        ],
        scratch_shapes=[pltpu.VMEM((D_MODEL, IN_WIDTH), BF16), pltpu.VMEM((2, D_MODEL, W_CHUNK), F32),
                        pltpu.SemaphoreType.DMA((2,))],
        compiler_params=_params(1),
        name="in_projection",
    )(*xs, mod3, gain, w_in)


SGU_TM = 512
SGU_GD = SGU_WIDTH // SGU_GROUPS


def _sgu_kernel(u_ref, v_ref, gain_ref, ws_ref, bs_ref, o_ref):
    gain = gain_ref[...]
    for c in range(SGU_TM // CHUNK):
        rows = slice(c * CHUNK, (c + 1) * CHUNK)
        u = jax.nn.gelu(u_ref[rows, :].astype(F32))
        v = _rms(jax.nn.gelu(v_ref[rows, :].astype(F32)), gain).astype(BF16)
        for g in range(SGU_GROUPS):
            cols = slice(g * SGU_GD, (g + 1) * SGU_GD)
            mixed = jnp.dot(ws_ref[g].astype(BF16), v[:, cols], preferred_element_type=F32) + bs_ref[:, g:g + 1]
            o_ref[rows, cols] = (u[:, cols] * mixed).astype(BF16)


def spatial_gating(za, gain, ws, bs_t):
    return pl.pallas_call(
        _sgu_kernel,
        grid=(N_TOK // SGU_TM,),
        in_specs=[
            pl.BlockSpec((SGU_TM, SGU_WIDTH), lambda i: (i, 0)),
            pl.BlockSpec((SGU_TM, SGU_WIDTH), lambda i: (i, 1)),
            pl.BlockSpec((1, SGU_WIDTH), lambda i: (0, 0)),
            pl.BlockSpec((SGU_GROUPS, CHUNK, CHUNK), lambda i: (0, 0, 0)),
            pl.BlockSpec((CHUNK, SGU_GROUPS), lambda i: (0, 0)),
        ],
        out_specs=pl.BlockSpec((SGU_TM, SGU_WIDTH), lambda i: (i, 0)),
        out_shape=jax.ShapeDtypeStruct((N_TOK, SGU_WIDTH), BF16),
        compiler_params=_params(1),
        name="spatial_gating",
    )(za, za, gain, ws, bs_t)


LRU_TC = 256
LRU_HALF = 256
SUB = 8
HALO = 8


def _lru_kernel(seq_len, xr_ref, gr_ref, cw_ref, cb_ref, wlo_ref, whi_ref, br_ref, bi_ref,
                lam_ref, h0_ref, y_ref, st_ref, xp_ref, a_ref, b_ref):
    n_chunks = seq_len // LRU_TC
    zeros = jnp.zeros((HALO, LRU_WIDTH), F32)
    xp_ref[0:HALO, :] = zeros
    xp_ref[seq_len + HALO:seq_len + 2 * HALO, :] = zeros

    def copy_in(c, carry):
        r0 = pl.multiple_of(c * LRU_TC, LRU_TC)
        xp_ref[pl.ds(r0 + HALO, LRU_TC), :] = xr_ref[pl.ds(r0, LRU_TC), :].astype(F32)
        return carry

    lax.fori_loop(0, n_chunks, copy_in, 0)

    cw = cw_ref[...]
    cb = cb_ref[...]
    win = LRU_TC + 2 * HALO
    sub_pos = lax.broadcasted_iota(jnp.int32, (LRU_TC, LRU_HALF), 0) % SUB
    neg_lam = -lam_ref[...]
    softplus = jnp.maximum(neg_lam, 0.0) + jnp.log1p(jnp.exp(-jnp.abs(neg_lam)))

    def local_scan(a, b, reverse):
        for s in (1, 2, 4):
            if reverse:
                a_n = pltpu.roll(a, LRU_TC - s, 0)
                b_n = pltpu.roll(b, LRU_TC - s, 0)
                keep = sub_pos < SUB - s
            else:
                a_n = pltpu.roll(a, s, 0)
                b_n = pltpu.roll(b, s, 0)
                keep = sub_pos >= s
            b = jnp.where(keep, a * b_n + b, b)
            a = jnp.where(keep, a * a_n, a)
        return a, b

    def gates(c, carry):
        r0 = pl.multiple_of(c * LRU_TC, LRU_TC)
        w = xp_ref[pl.ds(r0, win), :]
        xc = (cw[0:1, :] * pltpu.roll(w, 1, 0)[HALO:HALO + LRU_TC]
              + cw[1:2, :] * w[HALO:HALO + LRU_TC]
              + cw[2:3, :] * pltpu.roll(w, win - 1, 0)[HALO:HALO + LRU_TC]
              + cw[3:4, :] * pltpu.roll(w, win - 2, 0)[HALO:HALO + LRU_TC]) + cb
        xb = xc.astype(BF16)
        for half, w_ref in enumerate((wlo_ref, whi_ref)):
            cols = slice(half * LRU_HALF, (half + 1) * LRU_HALF)
            pre = jnp.dot(xb[:, cols], w_ref[...], preferred_element_type=F32)
            xh = xc[:, cols]
            for d in range(2):
                r = jax.nn.sigmoid(pre[:, (2 * d) * LRU_HALF:(2 * d + 1) * LRU_HALF] + br_ref[d:d + 1, cols])
                i = jax.nn.sigmoid(pre[:, (2 * d + 1) * LRU_HALF:(2 * d + 2) * LRU_HALF] + bi_ref[d:d + 1, cols])
                log_a = -LRU_C * r * softplus[d:d + 1, cols]
                a = jnp.exp(log_a)
                b = jnp.sqrt(-jnp.tanh(log_a) * (a * a + 1.0)) * (i * xh)
                a, b = local_scan(a, b, reverse=(d == 1))
                a_ref[d, pl.ds(r0, LRU_TC), cols] = a
                b_ref[d, pl.ds(r0, LRU_TC), cols] = b
        return carry

    lax.fori_loop(0, n_chunks, gates, 0)

    n_blocks = seq_len // SUB

    def fwd(j, h):
        r0 = pl.multiple_of(j * SUB, SUB)
        hb = a_ref[0, pl.ds(r0, SUB), :] * h + b_ref[0, pl.ds(r0, SUB), :]
        b_ref[0, pl.ds(r0, SUB), :] = hb
        return hb[SUB - 1:SUB, :]

    def bwd(j, h):
        r0 = pl.multiple_of((n_blocks - 1 - j) * SUB, SUB)
        hb = a_ref[1, pl.ds(r0, SUB), :] * h + b_ref[1, pl.ds(r0, SUB), :]
        b_ref[1, pl.ds(r0, SUB), :] = hb
        return hb[0:1, :]

    h_f = lax.fori_loop(0, n_blocks, fwd, h0_ref[0:1, :], unroll=4)
    h_b = lax.fori_loop(0, n_blocks, bwd, h0_ref[1:2, :], unroll=4)
    st_ref[0:1, :] = h_f
    st_ref[1:2, :] = h_b

    def merge(c, carry):
        r0 = pl.multiple_of(c * LRU_TC, LRU_TC)
        rows = pl.ds(r0, LRU_TC)
        y_ref[rows, :] = ((b_ref[0, rows, :] + b_ref[1, rows, :])
                          * jax.nn.gelu(gr_ref[rows, :].astype(F32))).astype(BF16)
        return carry

    lax.fori_loop(0, n_chunks, merge, 0)


def rglru(za, conv_w, conv_b, w_lo, w_hi, b_r, b_i, lam, h0, seq_len, n_seq, tok_off):
    blk0 = tok_off // seq_len
    full = lambda shape: pl.BlockSpec(shape, lambda i: (0,) * len(shape))
    return pl.pallas_call(
        functools.partial(_lru_kernel, seq_len),
        grid=(n_seq,),
        in_specs=[
            pl.BlockSpec((seq_len, LRU_WIDTH), lambda i: (i + blk0, 2)),
            pl.BlockSpec((seq_len, LRU_WIDTH), lambda i: (i + blk0, 3)),
            full((CONV_WIDTH, LRU_WIDTH)),
            full((1, LRU_WIDTH)),
            full((LRU_HALF, 4 * LRU_HALF)),
            full((LRU_HALF, 4 * LRU_HALF)),
            full((2, LRU_WIDTH)),
            full((2, LRU_WIDTH)),
            full((2, LRU_WIDTH)),
            pl.BlockSpec((None, 2, LRU_WIDTH), lambda i: (i, 0, 0)),
        ],
        out_specs=[
            pl.BlockSpec((seq_len, LRU_WIDTH), lambda i: (i, 0)),
            pl.BlockSpec((None, 2, LRU_WIDTH), lambda i: (i, 0, 0)),
        ],
        out_shape=[
            jax.ShapeDtypeStruct((n_seq * seq_len, LRU_WIDTH), BF16),
            jax.ShapeDtypeStruct((n_seq, 2, LRU_WIDTH), F32),
        ],
        scratch_shapes=[
            pltpu.VMEM((seq_len + 2 * HALO, LRU_WIDTH), F32),
            pltpu.VMEM((2, seq_len, LRU_WIDTH), F32),
            pltpu.VMEM((2, seq_len, LRU_WIDTH), F32),
        ],
        compiler_params=_params(1),
        name=f"rglru_{seq_len}",
    )(za, za, conv_w, conv_b, w_lo, w_hi, b_r, b_i, lam, h0)


def _block_diag_gate_weights(w_r, w_i):
    per_half = LRU_BLOCKS // 2
    eye = jnp.eye(per_half, dtype=w_r.dtype)

    def bd(w):
        return jnp.einsum("hij,hg->higj", w, eye).reshape(LRU_HALF, LRU_HALF)

    tiles = []
    for half in range(2):
        sl = slice(half * per_half, (half + 1) * per_half)
        tiles.append(jnp.concatenate([bd(w_r[0, sl]), bd(w_i[0, sl]), bd(w_r[1, sl]), bd(w_i[1, sl])], axis=1))
    return tiles[0].astype(BF16), tiles[1].astype(BF16)


PAIR_W = 2 * NA_HEAD_DIM
LOG2E = float(np.log2(np.e))


def _one_head(q_pair, head):
    lane = lax.broadcasted_iota(jnp.int32, q_pair.shape, 1)
    mine = (lane < NA_HEAD_DIM) if head == 0 else (lane >= NA_HEAD_DIM)
    return jnp.where(mine, q_pair.astype(F32), 0.0).astype(BF16)


def _join_heads(o0, o1):
    lane = lax.broadcasted_iota(jnp.int32, o0.shape, 1)
    return jnp.where(lane < NA_HEAD_DIM, o0, o1)


def _ctx_attn_kernel(q_ref, k_ref, v_ref, o_ref):
    for j in range(NA_HEADS // 2):
        cols = slice(j * PAIR_W, (j + 1) * PAIR_W)
        outs = []
        for head in range(2):
            s = _bdot_t(_one_head(q_ref[:, cols], head), k_ref[:, cols])
            p = jnp.exp2(s - jnp.max(s, axis=-1, keepdims=True))
            denom = jnp.sum(p, axis=-1, keepdims=True)
            outs.append(jnp.dot(p.astype(BF16), v_ref[:, cols], preferred_element_type=F32) / denom)
        o_ref[:, cols] = _join_heads(*outs).astype(BF16)


def context_attention(q, kb, vb):
    spec = pl.BlockSpec((SEQ, NA_WIDTH), lambda i: (i, 0))
    return pl.pallas_call(
        _ctx_attn_kernel,
        grid=(BATCH,),
        in_specs=[spec, spec, spec],
        out_specs=spec,
        out_shape=jax.ShapeDtypeStruct((N_CTX, NA_WIDTH), BF16),
        compiler_params=_params(1),
        name="context_attention",
    )(q, kb, vb)


NA_QROWS = 4
NA_TQ = NA_QROWS * GRID_W
NA_KROWS = NA_QROWS + WIN_ROWS
NA_TK = NA_KROWS * GRID_W
NA_QBLOCKS = GRID_ROWS // NA_QROWS


N_DROW = 2 * WIN_ROWS - 1
N_DCOL = 2 * WIN_COLS - 1
NA_BLOCK_KINDS = (0, 1, NA_QBLOCKS - 1)


def _first_key_row(qb):
    return (np.clip if isinstance(qb, int) else jnp.clip)(qb * NA_QROWS - WIN_ROWS // 2, 0, GRID_ROWS - NA_KROWS)


def _build_bias(rpb_ref, table_ref, bias_ref):
    qc = lax.broadcasted_iota(jnp.int32, (GRID_W, GRID_W), 0)
    kc = lax.broadcasted_iota(jnp.int32, (GRID_W, GRID_W), 1)
    col0 = jnp.clip(qc - WIN_COLS // 2, 0, GRID_W - WIN_COLS)
    col_ok = jnp.logical_and(kc >= col0, kc < col0 + WIN_COLS)
    d_col = jnp.clip(kc - qc, 1 - WIN_COLS, WIN_COLS - 1) + WIN_COLS - 1
    neg = jnp.full((GRID_W, GRID_W), NEG_INF, F32)

    def table_entry(idx, carry):
        t = neg
        for j in range(N_DCOL):
            t = jnp.where(d_col == j, rpb_ref[idx * N_DCOL + j], t)
        table_ref[idx] = jnp.where(col_ok, t * LOG2E, neg)
        return carry

    lax.fori_loop(0, NA_HEADS * N_DROW, table_entry, 0)

    def head_blocks(h, carry):
        for kind, qb in enumerate(NA_BLOCK_KINDS):
            for i in range(NA_QROWS):
                qrow = qb * NA_QROWS + i
                win0 = int(np.clip(qrow - WIN_ROWS // 2, 0, GRID_ROWS - WIN_ROWS))
                for kr in range(NA_KROWS):
                    krow = int(_first_key_row(qb)) + kr
                    inside = win0 <= krow < win0 + WIN_ROWS
                    blk = table_ref[h * N_DROW + (krow - qrow + WIN_ROWS - 1)] if inside else neg
                    bias_ref[kind, h, i * GRID_W:(i + 1) * GRID_W, kr * GRID_W:(kr + 1) * GRID_W] = blk
        return carry

    lax.fori_loop(0, NA_HEADS, head_blocks, 0)


def _lat_attn_kernel(rpb_ref, q_ref, k_ref, v_ref, ck_ref, cv_ref, o_ref, table_ref, bias_ref):
    qb = pl.program_id(1)

    @pl.when(jnp.logical_and(pl.program_id(0) == 0, qb == 0))
    def _():
        _build_bias(rpb_ref, table_ref, bias_ref)

    kind = jnp.where(qb == 0, 0, jnp.where(qb == NA_QBLOCKS - 1, 2, 1))
    k0 = pl.multiple_of(_first_key_row(qb) * GRID_W, GRID_W)
    for j in range(NA_HEADS // 2):
        cols = slice(j * PAIR_W, (j + 1) * PAIR_W)
        k_loc = k_ref[pl.ds(k0, NA_TK), cols]
        v_loc = v_ref[pl.ds(k0, NA_TK), cols]
        k_ctx = ck_ref[:, cols].astype(BF16)
        v_ctx = cv_ref[:, cols].astype(BF16)
        outs = []
        for head in range(2):
            qh = _one_head(q_ref[:, cols], head)
            s_loc = _bdot_t(qh, k_loc) + bias_ref[kind, 2 * j + head]
            s_ctx = _bdot_t(qh, k_ctx)
            m = jnp.maximum(jnp.max(s_loc, axis=-1, keepdims=True), jnp.max(s_ctx, axis=-1, keepdims=True))
            p_loc = jnp.exp2(s_loc - m)
            p_ctx = jnp.exp2(s_ctx - m)
            denom = jnp.sum(p_loc, axis=-1, keepdims=True) + jnp.sum(p_ctx, axis=-1, keepdims=True)
            o = (jnp.dot(p_loc.astype(BF16), v_loc, preferred_element_type=F32)
                 + jnp.dot(p_ctx.astype(BF16), v_ctx, preferred_element_type=F32))
            outs.append(o / denom)
        o_ref[:, cols] = _join_heads(*outs).astype(BF16)


def latent_attention(q, kb, vb, cache_k, cache_v, rpb, layer):
    q_blk0 = N_CTX // NA_TQ
    s_blk0 = N_CTX // DEC_SEQ
    cache_spec = pl.BlockSpec((None, None, PAST_LEN, NA_WIDTH), lambda b, m: (b, layer, 0, 0))
    return pl.pallas_call(
        _lat_attn_kernel,
        grid=(DEC_BATCH, NA_QBLOCKS),
        in_specs=[
            pl.BlockSpec(memory_space=pltpu.SMEM),
            pl.BlockSpec((NA_TQ, NA_WIDTH), lambda b, m: (q_blk0 + b * NA_QBLOCKS + m, 0)),
            pl.BlockSpec((DEC_SEQ, NA_WIDTH), lambda b, m: (s_blk0 + b, 0)),
            pl.BlockSpec((DEC_SEQ, NA_WIDTH), lambda b, m: (s_blk0 + b, 0)),
            cache_spec,
            cache_spec,
        ],
        out_specs=pl.BlockSpec((NA_TQ, NA_WIDTH), lambda b, m: (b * NA_QBLOCKS + m, 0)),
        out_shape=jax.ShapeDtypeStruct((N_LAT, NA_WIDTH), BF16),
        scratch_shapes=[pltpu.VMEM((NA_HEADS * N_DROW, GRID_W, GRID_W), F32),
                        pltpu.VMEM((len(NA_BLOCK_KINDS), NA_HEADS, NA_TQ, NA_TK), F32)],
        compiler_params=_params(2),
        name="latent_attention",
    )(rpb, q, kb, vb, cache_k, cache_v)


MERGE_TM = 256


MERGE_CTX_TILES = N_CTX // MERGE_TM
BRANCH_W = 512


def _merge_kernel(layer, n_x, *refs):
    x_refs, rest = refs[:n_x], refs[n_x:]
    (mod_ref, ya_ref, yb_ctx_ref, yb_lat_ref, yc_ctx_ref, yc_lat_ref, g_ref, wa_hbm, wb_hbm, wc_hbm, wo_hbm,
     o_ref, wbr_ref, wo_ref, stage_ref, sem_ref) = rest
    i = pl.program_id(0)

    @pl.when(i == 0)
    def _():
        def store_branch(k):
            def st(v):
                wbr_ref[k] = v
            return st

        def store_out(k):
            def st(v):
                wo_ref[k * BRANCH_W:(k + 1) * BRANCH_W, :] = v
            return st

        chunks = [(w.at[layer], store_branch(k)) for k, w in enumerate((wa_hbm, wb_hbm, wc_hbm))]
        chunks += [(wo_hbm.at[layer, pl.ds(k * BRANCH_W, BRANCH_W), :], store_out(k))
                   for k in range(D_MODEL // BRANCH_W)]
        _load_cast(chunks, stage_ref, sem_ref)

    g = jax.nn.sigmoid(g_ref[...].astype(F32))
    yb = _token_tile(i, MERGE_CTX_TILES, (yb_ctx_ref, yb_lat_ref))
    yc = _token_tile(i, MERGE_CTX_TILES, (yc_ctx_ref, yc_lat_ref))
    merged = (g[:, 0:D_MODEL] * jnp.dot(ya_ref[...], wbr_ref[0], preferred_element_type=F32)
              + g[:, D_MODEL:2 * D_MODEL] * jnp.dot(yb, wbr_ref[1], preferred_element_type=F32)
              + g[:, 2 * D_MODEL:] * jnp.dot(yc, wbr_ref[2], preferred_element_type=F32))
    y = jnp.dot(merged.astype(BF16), wo_ref[...], preferred_element_type=F32)
    gate = mod_ref[:, 2 * D_MODEL:3 * D_MODEL]
    o_ref[...] = _token_tile(i, MERGE_CTX_TILES, x_refs) + gate * y


def merge_branches(xs, mod3, ya, yb_ctx, yb_lat, yc_ctx, yc_lat, gates, wa, wb, wc, wo, layer):
    row = lambda i: (i, 0)
    hbm = pl.BlockSpec(memory_space=pl.ANY)
    x_specs = ([pl.BlockSpec((MERGE_TM, D_MODEL), row)] if len(xs) == 1
               else _split_specs((MERGE_TM, D_MODEL), MERGE_CTX_TILES))
    return pl.pallas_call(
        functools.partial(_merge_kernel, layer, len(xs)),
        grid=(N_TOK // MERGE_TM,),
        in_specs=x_specs + [
            pl.BlockSpec((None, 1, MOD_WIDTH), lambda i: (_cond_row(i, MERGE_TM), 0, 0)),
            pl.BlockSpec((MERGE_TM, SGU_WIDTH), row),
            *_split_specs((MERGE_TM, LRU_WIDTH), MERGE_CTX_TILES),
            *_split_specs((MERGE_TM, NA_WIDTH), MERGE_CTX_TILES),
            pl.BlockSpec((MERGE_TM, 3 * D_MODEL), row),
            hbm, hbm, hbm, hbm,
        ],
        out_specs=pl.BlockSpec((MERGE_TM, D_MODEL), row),
        out_shape=jax.ShapeDtypeStruct((N_TOK, D_MODEL), F32),
        scratch_shapes=[pltpu.VMEM((3, BRANCH_W, D_MODEL), BF16), pltpu.VMEM((D_MODEL, D_MODEL), BF16),
                        pltpu.VMEM((2, BRANCH_W, D_MODEL), F32), pltpu.SemaphoreType.DMA((2,))],
        compiler_params=_params(1),
        name="merge_branches",
    )(*xs, mod3, ya, yb_ctx, yb_lat, yc_ctx, yc_lat, gates, wa, wb, wc, wo)


RT_TM = 512
RT_ROWS = 32
N_BUCKETS = N_GROUPS
CNT_ROWS = 8
COMB_W = 128
HX_W = D_MODEL + COMB_W
EXP_TM = 256
EXP_TILES = N_TOK // EXP_TM + N_BUCKETS
HS_ROWS = EXP_TILES * EXP_TM


def _split_bf16(x):
    hi = x.astype(BF16)
    return hi, (x - hi.astype(F32)).astype(BF16)


def _router_kernel(x_ref, mod_ref, gain_ref, w_ref, b_ref, tri_ref, hx_ref, bucket_ref, rank_ref, cnt_out_ref,
                   cnt_ref, cpad_ref):
    @pl.when(pl.program_id(0) == 0)
    def _():
        cnt_ref[...] = jnp.zeros_like(cnt_ref)
        cpad_ref[...] = jnp.zeros_like(cpad_ref)

    m = mod_ref[...]
    shift, scale = m[:, 3 * D_MODEL:4 * D_MODEL], m[:, 4 * D_MODEL:5 * D_MODEL]
    h = _rms(x_ref[...], gain_ref[...]) * (1.0 + scale) + shift
    hx_ref[:, 0:D_MODEL] = h
    h_hi, h_lo = _split_bf16(h)
    w_hi, w_lo = _split_bf16(w_ref[...])
    dims = (((1,), (1,)), ((), ()))
    logits = (lax.dot_general(w_hi, h_hi, dims, preferred_element_type=F32)
              + lax.dot_general(w_hi, h_lo, dims, preferred_element_type=F32)
              + lax.dot_general(w_lo, h_hi, dims, preferred_element_type=F32)) + b_ref[...]
    gl = [logits[g:g + 1, :] for g in range(N_GROUPS)]
    gmax = functools.reduce(jnp.maximum, gl)
    gid = jnp.full(gmax.shape, N_GROUPS - 1, jnp.int32)
    for g in reversed(range(N_GROUPS - 1)):
        gid = jnp.where(gl[g] == gmax, g, gid)
    p_grp = 1.0 / functools.reduce(jnp.add, [jnp.exp(v - gmax) for v in gl])
    el = []
    for e in range(EXPERTS_PER_GROUP):
        v = logits[8 + e:9 + e, :]
        for g in range(1, N_GROUPS):
            row = 8 + g * EXPERTS_PER_GROUP + e
            v = jnp.where(gid == g, logits[row:row + 1, :], v)
        el.append(v)
    top1 = functools.reduce(jnp.maximum, el)
    idx1 = jnp.full(top1.shape, EXPERTS_PER_GROUP - 1, jnp.int32)
    for e in reversed(range(EXPERTS_PER_GROUP - 1)):
        idx1 = jnp.where(el[e] == top1, e, idx1)
    rest = [jnp.where(idx1 == e, -jnp.inf, el[e]) for e in range(EXPERTS_PER_GROUP)]
    top2 = functools.reduce(jnp.maximum, rest)
    idx2 = jnp.full(top1.shape, EXPERTS_PER_GROUP - 1, jnp.int32)
    for e in reversed(range(EXPERTS_PER_GROUP - 1)):
        idx2 = jnp.where(rest[e] == top2, e, idx2)
    e2 = jnp.exp(top2 - top1)
    w1 = p_grp / (1.0 + e2)
    w2 = p_grp * e2 / (1.0 + e2)
    for e in range(EXPERTS_PER_GROUP):
        cpad_ref[e:e + 1, :] = jnp.where(idx1 == e, w1, 0.0) + jnp.where(idx2 == e, w2, 0.0)
    hx_ref[:, D_MODEL:] = cpad_ref[...].T
    bucket_ref[...] = gid
    sub = lax.broadcasted_iota(jnp.int32, (CNT_ROWS, RT_TM), 0)
    onehot = jnp.where(sub == gid, 1.0, 0.0)
    seen = jnp.dot(onehot.astype(BF16), tri_ref[...], preferred_element_type=F32)
    cnt = cnt_ref[...]
    rank_ref[...] = jnp.sum(onehot * (seen - 1.0 + cnt[:, 0:1]), axis=0, keepdims=True).astype(jnp.int32)
    cnt = cnt + jnp.sum(onehot, axis=1, keepdims=True)
    cnt_ref[...] = cnt
    cnt_out_ref[...] = cnt.astype(jnp.int32)


def router(x, mod3, gain, w_rt, b_rt):
    tri = jnp.asarray(np.triu(np.ones((RT_TM, RT_TM), np.float32)), BF16)
    return pl.pallas_call(
        _router_kernel,
        grid=(N_TOK // RT_TM,),
        in_specs=[
            pl.BlockSpec((RT_TM, D_MODEL), lambda i: (i, 0)),
            pl.BlockSpec((None, 1, MOD_WIDTH), lambda i: (_cond_row(i, RT_TM), 0, 0)),
            pl.BlockSpec((1, D_MODEL), lambda i: (0, 0)),
            pl.BlockSpec((RT_ROWS, D_MODEL), lambda i: (0, 0)),
            pl.BlockSpec((RT_ROWS, 1), lambda i: (0, 0)),
            pl.BlockSpec((RT_TM, RT_TM), lambda i: (0, 0)),
        ],
        out_specs=[
            pl.BlockSpec((RT_TM, HX_W), lambda i: (i, 0)),
            pl.BlockSpec((1, RT_TM), lambda i: (0, i)),
            pl.BlockSpec((1, RT_TM), lambda i: (0, i)),
            pl.BlockSpec((CNT_ROWS, COMB_W), lambda i: (0, 0)),
        ],
        out_shape=[
            jax.ShapeDtypeStruct((N_TOK, HX_W), F32),
            jax.ShapeDtypeStruct((1, N_TOK), jnp.int32),
            jax.ShapeDtypeStruct((1, N_TOK), jnp.int32),
            jax.ShapeDtypeStruct((CNT_ROWS, COMB_W), jnp.int32),
        ],
        scratch_shapes=[pltpu.VMEM((CNT_ROWS, COMB_W), F32), pltpu.VMEM((COMB_W, RT_TM), F32)],
        compiler_params=_params(1),
        name="router",
    )(x, mod3, gain, w_rt, b_rt, tri)


def _router_weights(w_grp, b_grp, w_exp, b_exp):
    w = jnp.zeros((RT_ROWS, D_MODEL), F32).at[0:N_GROUPS].set(w_grp.T).at[8:8 + N_EXPERTS].set(w_exp.T)
    b = jnp.zeros((RT_ROWS, 1), F32).at[0:N_GROUPS, 0].set(b_grp).at[8:8 + N_EXPERTS, 0].set(b_exp)
    return w, b


DISP_TM = 512


def _sorted_row(bucket_ref, rank_ref, start_ref, k):
    return start_ref[bucket_ref[k]] + rank_ref[k]


def _dispatch_kernel(bucket_ref, rank_ref, start_ref, cnt_ref, nt_ref, hx_ref, hs_ref, zero_ref, sem, zsem):
    i = pl.program_id(0)
    base = i * DISP_TM

    def issue(r, carry):
        row = _sorted_row(bucket_ref, rank_ref, start_ref, base + r)
        pltpu.make_async_copy(hx_ref.at[pl.ds(r, 1), :], hs_ref.at[pl.ds(row, 1), :], sem).start()
        return carry

    lax.fori_loop(0, DISP_TM, issue, 0, unroll=8)
    pltpu.make_async_copy(hx_ref, hs_ref.at[pl.ds(0, DISP_TM), :], sem).wait()

    @pl.when(i == pl.num_programs(0) - 1)
    def _():
        zero_ref[...] = jnp.zeros_like(zero_ref)

        def pad_copy(row):
            return pltpu.make_async_copy(zero_ref.at[pl.ds(0, 1), :], hs_ref.at[pl.ds(row, 1), :], zsem)

        def pad_issue(row, carry):
            pad_copy(row).start()
            return carry

        def pad_wait(row, carry):
            pad_copy(row).wait()
            return carry

        for g in range(N_BUCKETS):
            lo = start_ref[g] + cnt_ref[g]
            hi = start_ref[g] + ((cnt_ref[g] + EXP_TM - 1) // EXP_TM) * EXP_TM
            lax.fori_loop(lo, hi, pad_issue, 0)
            lax.fori_loop(lo, hi, pad_wait, 0)

        def tile_copy(t):
            return pltpu.make_async_copy(zero_ref, hs_ref.at[pl.ds(pl.multiple_of(t * EXP_TM, EXP_TM), EXP_TM), :],
                                         zsem)

        def tile_issue(t, carry):
            tile_copy(t).start()
            return carry

        def tile_wait(t, carry):
            tile_copy(t).wait()
            return carry

        lax.fori_loop(nt_ref[0], EXP_TILES, tile_issue, 0)
        lax.fori_loop(nt_ref[0], EXP_TILES, tile_wait, 0)


def dispatch(bucket, rank, starts, counts, n_tiles, hx):
    return pl.pallas_call(
        _dispatch_kernel,
        grid_spec=pltpu.PrefetchScalarGridSpec(
            num_scalar_prefetch=5,
            grid=(N_TOK // DISP_TM,),
            in_specs=[pl.BlockSpec((DISP_TM, HX_W), lambda i, *_: (i, 0))],
            out_specs=pl.BlockSpec(memory_space=pl.ANY),
            scratch_shapes=[pltpu.VMEM((EXP_TM, HX_W), F32), pltpu.SemaphoreType.DMA(()),
                            pltpu.SemaphoreType.DMA(())],
        ),
        out_shape=jax.ShapeDtypeStruct((HS_ROWS, HX_W), F32),
        compiler_params=_params(1),
        name="dispatch",
    )(bucket, rank, starts, counts, n_tiles, hx)


def _experts_kernel(layer, tg_ref, nt_ref, hs_ref, w1_hbm, w3_hbm, w2_hbm, ys_ref, w1_ref, w3_ref, w2_ref,
                    stage_in_ref, stage_out_ref, sem_ref):
    t = pl.program_id(0)
    group = tg_ref[t]
    new_group = jnp.logical_or(t == 0, group != tg_ref[jnp.maximum(t - 1, 0)])

    @pl.when(jnp.logical_and(t < nt_ref[0], new_group))
    def _():
        def store(dst, e):
            def st(v):
                dst[e] = v
            return st

        def chunks(src, dst):
            return [(src.at[layer, group * EXPERTS_PER_GROUP + e], store(dst, e))
                    for e in range(EXPERTS_PER_GROUP)]

        _load_cast(chunks(w1_hbm, w1_ref) + chunks(w3_hbm, w3_ref), stage_in_ref, sem_ref)
        _load_cast(chunks(w2_hbm, w2_ref), stage_out_ref, sem_ref)

    @pl.when(t < nt_ref[0])
    def _():
        h = hs_ref[:, 0:D_MODEL].astype(BF16)
        c = hs_ref[:, D_MODEL:]
        acc = None
        for e in range(EXPERTS_PER_GROUP):
            a = jnp.dot(h, w1_ref[e], preferred_element_type=F32)
            b = jnp.dot(h, w3_ref[e], preferred_element_type=F32)
            hid = (a * jax.nn.sigmoid(a)) * b * c[:, e:e + 1]
            y = jnp.dot(hid.astype(BF16), w2_ref[e], preferred_element_type=F32)
            acc = y if acc is None else acc + y
        ys_ref[...] = acc

    @pl.when(t >= nt_ref[0])
    def _():
        ys_ref[...] = jnp.zeros_like(ys_ref)


def experts(tile_group, n_tiles, hs, w1, w3, w2, layer):
    hbm = pl.BlockSpec(memory_space=pl.ANY)
    return pl.pallas_call(
        functools.partial(_experts_kernel, layer),
        grid_spec=pltpu.PrefetchScalarGridSpec(
            num_scalar_prefetch=2,
            grid=(EXP_TILES,),
            in_specs=[pl.BlockSpec((EXP_TM, HX_W), lambda t, tg, nt: (t, 0)), hbm, hbm, hbm],
            out_specs=pl.BlockSpec((EXP_TM, D_MODEL), lambda t, tg, nt: (t, 0)),
            scratch_shapes=[
                pltpu.VMEM((EXPERTS_PER_GROUP, D_MODEL, D_EXPERT), BF16),
                pltpu.VMEM((EXPERTS_PER_GROUP, D_MODEL, D_EXPERT), BF16),
                pltpu.VMEM((EXPERTS_PER_GROUP, D_EXPERT, D_MODEL), BF16),
                pltpu.VMEM((2, D_MODEL, D_EXPERT), F32),
                pltpu.VMEM((2, D_EXPERT, D_MODEL), F32),
                pltpu.SemaphoreType.DMA((2,)),
            ],
        ),
        out_shape=jax.ShapeDtypeStruct((HS_ROWS, D_MODEL), F32),
        compiler_params=_params(1),
        name="experts",
    )(tile_group, n_tiles, hs, w1, w3, w2)


def _expert_tiles(counts):
    tiles = (counts + EXP_TM - 1) // EXP_TM
    ends = jnp.cumsum(tiles)
    t = jnp.arange(EXP_TILES, dtype=jnp.int32)
    group = jnp.minimum(jnp.sum(t[:, None] >= ends[None, :], axis=1), N_BUCKETS - 1).astype(jnp.int32)
    return ((ends - tiles) * EXP_TM).astype(jnp.int32), group, ends[-1:].astype(jnp.int32)


RES_TM = 512
RES_CTX_TILES = N_CTX // RES_TM


def _moe_residual_kernel(final, bucket_ref, rank_ref, start_ref, x_ref, mod_ref, gain_ref, ys_ref, *refs):
    out_refs, (ybuf_ref, sem) = refs[:-2], refs[-2:]
    i = pl.program_id(0)
    base = i * RES_TM

    def issue(r, carry):
        row = _sorted_row(bucket_ref, rank_ref, start_ref, base + r)
        pltpu.make_async_copy(ys_ref.at[pl.ds(row, 1), :], ybuf_ref.at[pl.ds(r, 1), :], sem).start()
        return carry

    lax.fori_loop(0, RES_TM, issue, 0, unroll=8)
    pltpu.make_async_copy(ys_ref.at[pl.ds(0, RES_TM), :], ybuf_ref, sem).wait()
    x = x_ref[...] + mod_ref[:, 5 * D_MODEL:6 * D_MODEL] * ybuf_ref[...]
    if not final:
        out_refs[0][...] = x
        return
    y = _rms(x, gain_ref[...])

    @pl.when(i < RES_CTX_TILES)
    def _():
        out_refs[0][...] = y

    @pl.when(i >= RES_CTX_TILES)
    def _():
        out_refs[1][...] = y


def moe_residual(bucket, rank, starts, x, mod3, gain, ys, final):
    if final:
        out_specs = _split_specs((RES_TM, D_MODEL), RES_CTX_TILES)
        out_shape = [jax.ShapeDtypeStruct((N_CTX, D_MODEL), F32), jax.ShapeDtypeStruct((N_LAT, D_MODEL), F32)]
    else:
        out_specs = pl.BlockSpec((RES_TM, D_MODEL), lambda i, *_: (i, 0))
        out_shape = jax.ShapeDtypeStruct((N_TOK, D_MODEL), F32)
    return pl.pallas_call(
        functools.partial(_moe_residual_kernel, final),
        grid_spec=pltpu.PrefetchScalarGridSpec(
            num_scalar_prefetch=3,
            grid=(N_TOK // RES_TM,),
            in_specs=[
                pl.BlockSpec((RES_TM, D_MODEL), lambda i, *_: (i, 0)),
                pl.BlockSpec((None, 1, MOD_WIDTH), lambda i, *_: (_cond_row(i, RES_TM), 0, 0)),
                pl.BlockSpec((1, D_MODEL), lambda i, *_: (0, 0)),
                pl.BlockSpec(memory_space=pl.ANY),
            ],
            out_specs=out_specs,
            scratch_shapes=[pltpu.VMEM((RES_TM, D_MODEL), F32), pltpu.SemaphoreType.DMA(())],
        ),
        out_shape=out_shape,
        compiler_params=_params(1),
        name="moe_residual_final" if final else "moe_residual",
    )(bucket, rank, starts, x, mod3, gain, ys)


def kernel(x_prompt, x_sample, cache_k, cache_v, state_lru, c, c_ctx, w_mod, b_mod, norm_mix, norm_ffn, w_in, sgu_norm, sgu_w, sgu_b, lru_conv_w, lru_conv_b, lru_w_r, lru_b_r, lru_w_i, lru_b_i, lru_lambda, na_rpb, w_branch_sgu, w_branch_lru, w_branch_na, w_out, moe_w_group, moe_b_group, moe_w_expert, moe_b_expert, moe_w1, moe_w3, moe_w2, final_norm_gain):
    xs = (x_prompt.reshape(N_CTX, D_MODEL), x_sample.reshape(N_LAT, D_MODEL))
    cond = jnp.zeros((N_COND, D_MODEL), F32).at[0].set(c_ctx).at[1:1 + DEC_BATCH].set(c)
    mod = modulation(cond, w_mod, b_mod)
    zero_state = jnp.zeros((BATCH, 2, LRU_WIDTH), F32)
    cache_k = cache_k.reshape(DEC_BATCH, DEPTH, PAST_LEN, NA_WIDTH)
    cache_v = cache_v.reshape(DEC_BATCH, DEPTH, PAST_LEN, NA_WIDTH)
    final_gain = final_norm_gain.reshape(1, D_MODEL)
    ks, vs, ss = [], [], []
    for l in range(DEPTH):
        mod3 = mod[l].reshape(N_COND, 1, MOD_WIDTH)
        za, q, kb, vb, k_ctx, v_ctx, gates = in_projection(xs, mod3, norm_mix[l].reshape(1, D_MODEL), w_in, l)
        ya = spatial_gating(za, sgu_norm[l].reshape(1, SGU_WIDTH), sgu_w[l], sgu_b[l].T)
        w_lo, w_hi = _block_diag_gate_weights(lru_w_r[l], lru_w_i[l])
        lru_args = (lru_conv_w[l], lru_conv_b[l].reshape(1, LRU_WIDTH), w_lo, w_hi, lru_b_r[l], lru_b_i[l],
                    lru_lambda[l])
        yb_ctx, st_ctx = rglru(za, *lru_args, zero_state, SEQ, BATCH, 0)
        yb_lat, _ = rglru(za, *lru_args, state_lru[:, l], DEC_SEQ, DEC_BATCH, N_CTX)
        yc_ctx = context_attention(q, kb, vb)
        yc_lat = latent_attention(q, kb, vb, cache_k, cache_v, na_rpb[l].reshape(-1), l)
        x = merge_branches(xs, mod3, ya, yb_ctx, yb_lat, yc_ctx, yc_lat, gates,
                           w_branch_sgu, w_branch_lru, w_branch_na, w_out, l)
        w_rt, b_rt = _router_weights(moe_w_group[l], moe_b_group[l], moe_w_expert[l], moe_b_expert[l])
        hx, bucket, rank, counts = router(x, mod3, norm_ffn[l].reshape(1, D_MODEL), w_rt, b_rt)
        bucket, rank, counts = bucket.reshape(N_TOK), rank.reshape(N_TOK), counts[:N_BUCKETS, 0]
        starts, tile_group, n_tiles = _expert_tiles(counts)
        hs = dispatch(bucket, rank, starts, counts, n_tiles, hx)
        ys = experts(tile_group, n_tiles, hs, moe_w1, moe_w3, moe_w2, l)
        if l < DEPTH - 1:
            xs = (moe_residual(bucket, rank, starts, x, mod3, final_gain, ys, False),)
        else:
            y_ctx, y_lat = moe_residual(bucket, rank, starts, x, mod3, final_gain, ys, True)
        ks.append(k_ctx.reshape(BATCH, SEQ, NA_HEADS, NA_HEAD_DIM))
        vs.append(v_ctx.reshape(BATCH, SEQ, NA_HEADS, NA_HEAD_DIM))
        ss.append(st_ctx)
    return (y_ctx.reshape(BATCH, SEQ, D_MODEL), y_lat.reshape(DEC_BATCH, DEC_SEQ, D_MODEL),
            jnp.stack(ks, axis=1), jnp.stack(vs, axis=1), jnp.stack(ss, axis=1))
```

```python
import functools

import jax
import jax.numpy as jnp
import numpy as np
from jax import lax
from jax.experimental import pallas as pl
from jax.experimental.pallas import tpu as pltpu

F32 = jnp.float32
BF16 = jnp.bfloat16

D_MODEL = 1024
BATCH = 16
SEQ = 256
DEPTH = 2
DEC_BATCH = 4
DEC_SEQ = 2048
PAST_LEN = 512
GRID_W = 64
CHUNK = 128
SGU_WIDTH = 512
SGU_GROUPS = 4
LRU_WIDTH = 512
LRU_BLOCKS = 8
CONV_WIDTH = 4
LRU_C = 8.0
NA_HEADS = 8
NA_HEAD_DIM = 64
NA_WIDTH = 512
WIN_ROWS = 8
WIN_COLS = 16
N_GROUPS = 4
EXPERTS_PER_GROUP = 4
N_EXPERTS = 16
D_EXPERT = 512
IN_WIDTH = 6656
EPS = 1e-6
NEG_INF = -1e30

N_CTX = BATCH * SEQ
N_LAT = DEC_BATCH * DEC_SEQ
N_TOK = N_CTX + N_LAT
N_COND = 8
MOD_WIDTH = 6 * D_MODEL
GRID_ROWS = DEC_SEQ // GRID_W

VMEM_LIMIT_BYTES = 56 * 1024 * 1024


def _params(n_axes):
    return pltpu.CompilerParams(dimension_semantics=("arbitrary",) * n_axes,
                                vmem_limit_bytes=VMEM_LIMIT_BYTES)


def _cond_row(tile, tile_rows):
    tok = tile * tile_rows
    return jnp.where(tok < N_CTX, 0, 1 + (tok - N_CTX) // DEC_SEQ)


def _rms(x, gain):
    return x * lax.rsqrt(jnp.mean(x * x, axis=-1, keepdims=True) + EPS) * gain


def _bdot(a, b):
    return jnp.dot(a.astype(BF16), b.astype(BF16), preferred_element_type=F32)


def _bdot_t(a, b):
    return lax.dot_general(a.astype(BF16), b.astype(BF16), (((1,), (1,)), ((), ())),
                           preferred_element_type=F32)


MOD_TN = 1536


def _mod_kernel(cond_ref, w_ref, b_ref, o_ref):
    c = cond_ref[...]
    s = c * jax.nn.sigmoid(c)
    o_ref[...] = _bdot(s, w_ref[...]) + b_ref[...]


def modulation(cond, w_mod, b_mod):
    return pl.pallas_call(
        _mod_kernel,
        grid=(DEPTH, MOD_WIDTH // MOD_TN),
        in_specs=[
            pl.BlockSpec((N_COND, D_MODEL), lambda l, j: (0, 0)),
            pl.BlockSpec((None, D_MODEL, MOD_TN), lambda l, j: (l, 0, j)),
            pl.BlockSpec((None, 1, MOD_TN), lambda l, j: (l, 0, j)),
        ],
        out_specs=pl.BlockSpec((None, N_COND, MOD_TN), lambda l, j: (l, 0, j)),
        out_shape=jax.ShapeDtypeStruct((DEPTH, N_COND, MOD_WIDTH), F32),
        compiler_params=_params(2),
        name="modulation",
    )(cond, w_mod, b_mod.reshape(DEPTH, 1, MOD_WIDTH))


IN_TM = 256
IN_CTX_TILES = N_CTX // IN_TM
ZA_WIDTH = 4 * 512
KV_OFF = ZA_WIDTH + NA_WIDTH
GATE_OFF = KV_OFF + 2 * NA_WIDTH
W_CHUNK = 512
Q_SCALE = NA_HEAD_DIM ** -0.5 * float(np.log2(np.e))


def _load_cast(chunks, stage_ref, sem_ref):
    def copy(j):
        return pltpu.make_async_copy(chunks[j][0], stage_ref.at[j % 2], sem_ref.at[j % 2])

    copy(0).start()
    for j in range(len(chunks)):
        if j + 1 < len(chunks):
            copy(j + 1).start()
        copy(j).wait()
        chunks[j][1](stage_ref[j % 2].astype(BF16))


def _token_tile(i, n_ctx_tiles, refs):
    if len(refs) == 1:
        return refs[0][...]
    return jnp.where(i < n_ctx_tiles, refs[0][...], refs[1][...])


def _split_specs(block, n_ctx_tiles):
    return [pl.BlockSpec(block, lambda i, *_: (jnp.minimum(i, n_ctx_tiles - 1), 0)),
            pl.BlockSpec(block, lambda i, *_: (jnp.maximum(i - n_ctx_tiles, 0), 0))]


def _inproj_kernel(layer, n_x, *refs):
    x_refs, (mod_ref, gain_ref, w_hbm) = refs[:n_x], refs[n_x:n_x + 3]
    za_ref, q_ref, kb_ref, vb_ref, kc_ref, vc_ref, g_ref, w_ref, stage_ref, sem_ref = refs[n_x + 3:]
    i = pl.program_id(0)

    @pl.when(i == 0)
    def _():
        def store(c):
            def st(v):
                w_ref[:, c * W_CHUNK:(c + 1) * W_CHUNK] = v
            return st

        _load_cast([(w_hbm.at[layer, :, pl.ds(c * W_CHUNK, W_CHUNK)], store(c)) for c in range(IN_WIDTH // W_CHUNK)],
                   stage_ref, sem_ref)

    m = mod_ref[...]
    shift, scale = m[:, 0:D_MODEL], m[:, D_MODEL:2 * D_MODEL]
    h = (_rms(_token_tile(i, IN_CTX_TILES, x_refs), gain_ref[...]) * (1.0 + scale) + shift).astype(BF16)
    za_ref[...] = jnp.dot(h, w_ref[:, 0:ZA_WIDTH], preferred_element_type=F32).astype(BF16)
    q_ref[...] = (jnp.dot(h, w_ref[:, ZA_WIDTH:KV_OFF], preferred_element_type=F32) * Q_SCALE).astype(BF16)
    kv = jnp.dot(h, w_ref[:, KV_OFF:GATE_OFF], preferred_element_type=F32)
    kb_ref[...] = kv[:, 0:NA_WIDTH].astype(BF16)
    vb_ref[...] = kv[:, NA_WIDTH:].astype(BF16)
    g_ref[...] = jnp.dot(h, w_ref[:, GATE_OFF:], preferred_element_type=F32).astype(BF16)

    @pl.when(i < IN_CTX_TILES)
    def _():
        kc_ref[...] = kv[:, 0:NA_WIDTH]
        vc_ref[...] = kv[:, NA_WIDTH:]


def in_projection(xs, mod3, gain, w_in, layer):
    row = lambda i: (i, 0)
    ctx_row = lambda i: (jnp.minimum(i, IN_CTX_TILES - 1), 0)
    x_specs = ([pl.BlockSpec((IN_TM, D_MODEL), row)] if len(xs) == 1
               else _split_specs((IN_TM, D_MODEL), IN_CTX_TILES))
    return pl.pallas_call(
        functools.partial(_inproj_kernel, layer, len(xs)),
        grid=(N_TOK // IN_TM,),
        in_specs=x_specs + [
            pl.BlockSpec((None, 1, MOD_WIDTH), lambda i: (_cond_row(i, IN_TM), 0, 0)),
            pl.BlockSpec((1, D_MODEL), lambda i: (0, 0)),
            pl.BlockSpec(memory_space=pl.ANY),
        ],
        out_specs=[
            pl.BlockSpec((IN_TM, ZA_WIDTH), row),
            pl.BlockSpec((IN_TM, NA_WIDTH), row),
            pl.BlockSpec((IN_TM, NA_WIDTH), row),
            pl.BlockSpec((IN_TM, NA_WIDTH), row),
            pl.BlockSpec((IN_TM, NA_WIDTH), ctx_row),
            pl.BlockSpec((IN_TM, NA_WIDTH), ctx_row),
            pl.BlockSpec((IN_TM, 3 * D_MODEL), row),
        ],
        out_shape=[
            jax.ShapeDtypeStruct((N_TOK, ZA_WIDTH), BF16),
            jax.ShapeDtypeStruct((N_TOK, NA_WIDTH), BF16),
            jax.ShapeDtypeStruct((N_TOK, NA_WIDTH), BF16),
            jax.ShapeDtypeStruct((N_TOK, NA_WIDTH), BF16),
            jax.ShapeDtypeStruct((N_CTX, NA_WIDTH), F32),
            jax.ShapeDtypeStruct((N_CTX, NA_WIDTH), F32),
            jax.ShapeDtypeStruct((N_TOK, 3 * D_MODEL), BF16),
        ],
        scratch_shapes=[pltpu.VMEM((D_MODEL, IN_WIDTH), BF16), pltpu.VMEM((2, D_MODEL, W_CHUNK), F32),
                        pltpu.SemaphoreType.DMA((2,))],
        compiler_params=_params(1),
        name="in_projection",
    )(*xs, mod3, gain, w_in)


SGU_TM = 512
SGU_GD = SGU_WIDTH // SGU_GROUPS


def _sgu_kernel(u_ref, v_ref, gain_ref, ws_ref, bs_ref, o_ref):
    gain = gain_ref[...]
    for c in range(SGU_TM // CHUNK):
        rows = slice(c * CHUNK, (c + 1) * CHUNK)
        u = jax.nn.gelu(u_ref[rows, :].astype(F32))
        v = _rms(jax.nn.gelu(v_ref[rows, :].astype(F32)), gain).astype(BF16)
        for g in range(SGU_GROUPS):
            cols = slice(g * SGU_GD, (g + 1) * SGU_GD)
            mixed = jnp.dot(ws_ref[g].astype(BF16), v[:, cols], preferred_element_type=F32) + bs_ref[:, g:g + 1]
            o_ref[rows, cols] = (u[:, cols] * mixed).astype(BF16)


def spatial_gating(za, gain, ws, bs_t):
    return pl.pallas_call(
        _sgu_kernel,
        grid=(N_TOK // SGU_TM,),
        in_specs=[
            pl.BlockSpec((SGU_TM, SGU_WIDTH), lambda i: (i, 0)),
            pl.BlockSpec((SGU_TM, SGU_WIDTH), lambda i: (i, 1)),
            pl.BlockSpec((1, SGU_WIDTH), lambda i: (0, 0)),
            pl.BlockSpec((SGU_GROUPS, CHUNK, CHUNK), lambda i: (0, 0, 0)),
            pl.BlockSpec((CHUNK, SGU_GROUPS), lambda i: (0, 0)),
        ],
        out_specs=pl.BlockSpec((SGU_TM, SGU_WIDTH), lambda i: (i, 0)),
        out_shape=jax.ShapeDtypeStruct((N_TOK, SGU_WIDTH), BF16),
        compiler_params=_params(1),
        name="spatial_gating",
    )(za, za, gain, ws, bs_t)


LRU_TC = 256
LRU_HALF = 256
SUB = 8
HALO = 8
LANES = 128
LRU_SLABS = LRU_WIDTH // LANES


def _sigmoid(x):
    return 0.5 * jnp.tanh(0.5 * x) + 0.5


def _lru_kernel(seq_len, xr_ref, gr_ref, cw_ref, cb_ref, wlo_ref, whi_ref, br_ref, bi_ref,
                lam_ref, h0_ref, y_ref, st_ref, xp_ref, a_ref, b_ref):
    seg_len = seq_len // SUB
    pitch = seg_len + SUB
    segs_per_chunk = max(LRU_TC // seg_len, 1)
    n_chunks = seq_len // LRU_TC
    zeros = jnp.zeros((HALO, LRU_WIDTH), F32)
    xp_ref[0:HALO, :] = zeros
    xp_ref[seq_len + HALO:seq_len + 2 * HALO, :] = zeros

    def copy_in(c, carry):
        r0 = pl.multiple_of(c * LRU_TC, LRU_TC)
        xp_ref[pl.ds(r0 + HALO, LRU_TC), :] = xr_ref[pl.ds(r0, LRU_TC), :].astype(F32)
        return carry

    lax.fori_loop(0, n_chunks, copy_in, 0)

    def chunk_rows(c, seg):
        n = min(seg_len, LRU_TC)
        start = pl.multiple_of((c * segs_per_chunk + seg) * pitch, SUB)
        return pl.ds(start, n), slice(seg * n, (seg + 1) * n)

    cw = cw_ref[...]
    cb = cb_ref[...]
    win = LRU_TC + 2 * HALO
    neg_lam = -lam_ref[...]
    softplus = jnp.maximum(neg_lam, 0.0) + jnp.log1p(jnp.exp(-jnp.abs(neg_lam)))
    decay = (0.5 * LRU_C) * softplus
    half_br = 0.5 * br_ref[...]
    half_bi = 0.5 * bi_ref[...]

    def gates(c, carry):
        r0 = pl.multiple_of(c * LRU_TC, LRU_TC)
        w = xp_ref[pl.ds(r0, win), :]
        xc = (cw[0:1, :] * pltpu.roll(w, 1, 0)[HALO:HALO + LRU_TC]
              + cw[1:2, :] * w[HALO:HALO + LRU_TC]
              + cw[2:3, :] * pltpu.roll(w, win - 1, 0)[HALO:HALO + LRU_TC]
              + cw[3:4, :] * pltpu.roll(w, win - 2, 0)[HALO:HALO + LRU_TC]) + cb
        xb = xc.astype(BF16)
        for half, w_ref in enumerate((wlo_ref, whi_ref)):
            cols = slice(half * LRU_HALF, (half + 1) * LRU_HALF)
            pre = jnp.dot(xb[:, cols], w_ref[...], preferred_element_type=F32)
            half_x = 0.5 * xc[:, cols]
            for d in range(2):
                tr = jnp.tanh(0.5 * pre[:, (2 * d) * LRU_HALF:(2 * d + 1) * LRU_HALF] + half_br[d:d + 1, cols])
                ti = jnp.tanh(0.5 * pre[:, (2 * d + 1) * LRU_HALF:(2 * d + 2) * LRU_HALF] + half_bi[d:d + 1, cols])
                neg_log_a = decay[d:d + 1, cols] * tr + decay[d:d + 1, cols]
                a = jnp.exp2(neg_log_a * -LOG2E)
                z = jnp.tanh(neg_log_a) * (a * a + 1.0)
                b = jnp.where(z > 0.0, z * lax.rsqrt(z), 0.0) * (half_x * ti + half_x)
                for k in range(LRU_HALF // LANES):
                    slab = half * (LRU_HALF // LANES) + k
                    lanes = slice(k * LANES, (k + 1) * LANES)
                    for seg in range(segs_per_chunk):
                        dst, src = chunk_rows(c, seg)
                        a_ref[d, slab, dst, :] = a[src, lanes]
                        b_ref[d, slab, dst, :] = b[src, lanes]
        return carry

    lax.fori_loop(0, n_chunks, gates, 0)

    def step_rows(d, j):
        return pl.ds(j if d == 0 else seg_len - 1 - j, SUB, stride=pitch)

    def reduce_step(j, carry):
        out = []
        for d in range(2):
            for slab in range(LRU_SLABS):
                big_a, big_b = carry[d * LRU_SLABS + slab]
                a = a_ref[d, slab, step_rows(d, j), :]
                b = b_ref[d, slab, step_rows(d, j), :]
                out.append((a * big_a, a * big_b + b))
        return tuple(out)

    ident = (jnp.ones((SUB, LANES), F32), jnp.zeros((SUB, LANES), F32))
    totals = lax.fori_loop(0, seg_len, reduce_step, (ident,) * (2 * LRU_SLABS), unroll=2)

    sub = lax.broadcasted_iota(jnp.int32, (SUB, LANES), 0)
    entering = []
    for d in range(2):
        for slab in range(LRU_SLABS):
            big_a, big_b = totals[d * LRU_SLABS + slab]
            h = h0_ref[d:d + 1, slab * LANES:(slab + 1) * LANES]
            rows = jnp.zeros((SUB, LANES), F32)
            for s in (range(SUB) if d == 0 else reversed(range(SUB))):
                rows = jnp.where(sub == s, h, rows)
                h = big_a[s:s + 1, :] * h + big_b[s:s + 1, :]
            entering.append(rows)
            st_ref[d:d + 1, slab * LANES:(slab + 1) * LANES] = h

    def apply_step(j, carry):
        out = []
        for d in range(2):
            for slab in range(LRU_SLABS):
                h = (a_ref[d, slab, step_rows(d, j), :] * carry[d * LRU_SLABS + slab]
                     + b_ref[d, slab, step_rows(d, j), :])
                b_ref[d, slab, step_rows(d, j), :] = h
                out.append(h)
        return tuple(out)

    lax.fori_loop(0, seg_len, apply_step, tuple(entering), unroll=2)

    def merge(c, carry):
        r0 = pl.multiple_of(c * LRU_TC, LRU_TC)
        gate = jax.nn.gelu(gr_ref[pl.ds(r0, LRU_TC), :].astype(F32))
        for slab in range(LRU_SLABS):
            lanes = slice(slab * LANES, (slab + 1) * LANES)
            for seg in range(segs_per_chunk):
                src, dst = chunk_rows(c, seg)
                h = b_ref[0, slab, src, :] + b_ref[1, slab, src, :]
                y_ref[pl.ds(r0 + dst.start, dst.stop - dst.start), lanes] = (h * gate[dst, lanes]).astype(BF16)
        return carry

    lax.fori_loop(0, n_chunks, merge, 0)


def rglru(za, conv_w, conv_b, w_lo, w_hi, b_r, b_i, lam, h0, seq_len, n_seq, tok_off):
    blk0 = tok_off // seq_len
    full = lambda shape: pl.BlockSpec(shape, lambda i: (0,) * len(shape))
    return pl.pallas_call(
        functools.partial(_lru_kernel, seq_len),
        grid=(n_seq,),
        in_specs=[
            pl.BlockSpec((seq_len, LRU_WIDTH), lambda i: (i + blk0, 2)),
            pl.BlockSpec((seq_len, LRU_WIDTH), lambda i: (i + blk0, 3)),
            full((CONV_WIDTH, LRU_WIDTH)),
            full((1, LRU_WIDTH)),
            full((LRU_HALF, 4 * LRU_HALF)),
            full((LRU_HALF, 4 * LRU_HALF)),
            full((2, LRU_WIDTH)),
            full((2, LRU_WIDTH)),
            full((2, LRU_WIDTH)),
            pl.BlockSpec((None, 2, LRU_WIDTH), lambda i: (i, 0, 0)),
        ],
        out_specs=[
            pl.BlockSpec((seq_len, LRU_WIDTH), lambda i: (i, 0)),
            pl.BlockSpec((None, 2, LRU_WIDTH), lambda i: (i, 0, 0)),
        ],
        out_shape=[
            jax.ShapeDtypeStruct((n_seq * seq_len, LRU_WIDTH), BF16),
            jax.ShapeDtypeStruct((n_seq, 2, LRU_WIDTH), F32),
        ],
        scratch_shapes=[
            pltpu.VMEM((seq_len + 2 * HALO, LRU_WIDTH), F32),
            pltpu.VMEM((2, LRU_SLABS, seq_len + SUB * SUB, LANES), F32),
            pltpu.VMEM((2, LRU_SLABS, seq_len + SUB * SUB, LANES), F32),
        ],
        compiler_params=_params(1),
        name=f"rglru_{seq_len}",
    )(za, za, conv_w, conv_b, w_lo, w_hi, b_r, b_i, lam, h0)


def _block_diag_gate_weights(w_r, w_i):
    per_half = LRU_BLOCKS // 2
    eye = jnp.eye(per_half, dtype=w_r.dtype)

    def bd(w):
        return jnp.einsum("hij,hg->higj", w, eye).reshape(LRU_HALF, LRU_HALF)

    tiles = []
    for half in range(2):
        sl = slice(half * per_half, (half + 1) * per_half)
        tiles.append(jnp.concatenate([bd(w_r[0, sl]), bd(w_i[0, sl]), bd(w_r[1, sl]), bd(w_i[1, sl])], axis=1))
    return tiles[0].astype(BF16), tiles[1].astype(BF16)


PAIR_W = 2 * NA_HEAD_DIM
LOG2E = float(np.log2(np.e))


def _one_head(q_pair, head):
    lane = lax.broadcasted_iota(jnp.int32, q_pair.shape, 1)
    mine = (lane < NA_HEAD_DIM) if head == 0 else (lane >= NA_HEAD_DIM)
    return jnp.where(mine, q_pair.astype(F32), 0.0).astype(BF16)


def _join_heads(o0, o1):
    lane = lax.broadcasted_iota(jnp.int32, o0.shape, 1)
    return jnp.where(lane < NA_HEAD_DIM, o0, o1)


def _ctx_attn_kernel(q_ref, k_ref, v_ref, o_ref):
    for j in range(NA_HEADS // 2):
        cols = slice(j * PAIR_W, (j + 1) * PAIR_W)
        outs = []
        for head in range(2):
            s = _bdot_t(_one_head(q_ref[:, cols], head), k_ref[:, cols])
            p = jnp.exp2(s - jnp.max(s, axis=-1, keepdims=True))
            denom = jnp.sum(p, axis=-1, keepdims=True)
            outs.append(jnp.dot(p.astype(BF16), v_ref[:, cols], preferred_element_type=F32) / denom)
        o_ref[:, cols] = _join_heads(*outs).astype(BF16)


def context_attention(q, kb, vb):
    spec = pl.BlockSpec((SEQ, NA_WIDTH), lambda i: (i, 0))
    return pl.pallas_call(
        _ctx_attn_kernel,
        grid=(BATCH,),
        in_specs=[spec, spec, spec],
        out_specs=spec,
        out_shape=jax.ShapeDtypeStruct((N_CTX, NA_WIDTH), BF16),
        compiler_params=_params(1),
        name="context_attention",
    )(q, kb, vb)


NA_QROWS = 4
NA_TQ = NA_QROWS * GRID_W
NA_KROWS = NA_QROWS + WIN_ROWS
NA_TK = NA_KROWS * GRID_W
NA_QBLOCKS = GRID_ROWS // NA_QROWS


N_DROW = 2 * WIN_ROWS - 1
N_DCOL = 2 * WIN_COLS - 1
NA_BLOCK_KINDS = (0, 1, NA_QBLOCKS - 1)


def _first_key_row(qb):
    return (np.clip if isinstance(qb, int) else jnp.clip)(qb * NA_QROWS - WIN_ROWS // 2, 0, GRID_ROWS - NA_KROWS)


def _build_bias(rpb_ref, table_ref, bias_ref):
    qc = lax.broadcasted_iota(jnp.int32, (GRID_W, GRID_W), 0)
    kc = lax.broadcasted_iota(jnp.int32, (GRID_W, GRID_W), 1)
    col0 = jnp.clip(qc - WIN_COLS // 2, 0, GRID_W - WIN_COLS)
    col_ok = jnp.logical_and(kc >= col0, kc < col0 + WIN_COLS)
    d_col = jnp.clip(kc - qc, 1 - WIN_COLS, WIN_COLS - 1) + WIN_COLS - 1
    neg = jnp.full((GRID_W, GRID_W), NEG_INF, F32)

    def table_entry(idx, carry):
        t = neg
        for j in range(N_DCOL):
            t = jnp.where(d_col == j, rpb_ref[idx * N_DCOL + j], t)
        table_ref[idx] = jnp.where(col_ok, t * LOG2E, neg)
        return carry

    lax.fori_loop(0, NA_HEADS * N_DROW, table_entry, 0)

    def head_blocks(h, carry):
        for kind, qb in enumerate(NA_BLOCK_KINDS):
            for i in range(NA_QROWS):
                qrow = qb * NA_QROWS + i
                win0 = int(np.clip(qrow - WIN_ROWS // 2, 0, GRID_ROWS - WIN_ROWS))
                for kr in range(NA_KROWS):
                    krow = int(_first_key_row(qb)) + kr
                    inside = win0 <= krow < win0 + WIN_ROWS
                    blk = table_ref[h * N_DROW + (krow - qrow + WIN_ROWS - 1)] if inside else neg
                    bias_ref[kind, h, i * GRID_W:(i + 1) * GRID_W, kr * GRID_W:(kr + 1) * GRID_W] = blk
        return carry

    lax.fori_loop(0, NA_HEADS, head_blocks, 0)


def _lat_attn_kernel(rpb_ref, q_ref, k_ref, v_ref, ck_ref, cv_ref, o_ref, table_ref, bias_ref):
    qb = pl.program_id(1)

    @pl.when(jnp.logical_and(pl.program_id(0) == 0, qb == 0))
    def _():
        _build_bias(rpb_ref, table_ref, bias_ref)

    kind = jnp.where(qb == 0, 0, jnp.where(qb == NA_QBLOCKS - 1, 2, 1))
    k0 = pl.multiple_of(_first_key_row(qb) * GRID_W, GRID_W)
    for j in range(NA_HEADS // 2):
        cols = slice(j * PAIR_W, (j + 1) * PAIR_W)
        k_loc = k_ref[pl.ds(k0, NA_TK), cols]
        v_loc = v_ref[pl.ds(k0, NA_TK), cols]
        k_ctx = ck_ref[:, cols].astype(BF16)
        v_ctx = cv_ref[:, cols].astype(BF16)
        outs = []
        for head in range(2):
            qh = _one_head(q_ref[:, cols], head)
            s_loc = _bdot_t(qh, k_loc) + bias_ref[kind, 2 * j + head]
            s_ctx = _bdot_t(qh, k_ctx)
            m = jnp.maximum(jnp.max(s_loc, axis=-1, keepdims=True), jnp.max(s_ctx, axis=-1, keepdims=True))
            p_loc = jnp.exp2(s_loc - m)
            p_ctx = jnp.exp2(s_ctx - m)
            denom = jnp.sum(p_loc, axis=-1, keepdims=True) + jnp.sum(p_ctx, axis=-1, keepdims=True)
            o = (jnp.dot(p_loc.astype(BF16), v_loc, preferred_element_type=F32)
                 + jnp.dot(p_ctx.astype(BF16), v_ctx, preferred_element_type=F32))
            outs.append(o / denom)
        o_ref[:, cols] = _join_heads(*outs).astype(BF16)


def latent_attention(q, kb, vb, cache_k, cache_v, rpb, layer):
    q_blk0 = N_CTX // NA_TQ
    s_blk0 = N_CTX // DEC_SEQ
    cache_spec = pl.BlockSpec((None, None, PAST_LEN, NA_WIDTH), lambda b, m: (b, layer, 0, 0))
    return pl.pallas_call(
        _lat_attn_kernel,
        grid=(DEC_BATCH, NA_QBLOCKS),
        in_specs=[
            pl.BlockSpec(memory_space=pltpu.SMEM),
            pl.BlockSpec((NA_TQ, NA_WIDTH), lambda b, m: (q_blk0 + b * NA_QBLOCKS + m, 0)),
            pl.BlockSpec((DEC_SEQ, NA_WIDTH), lambda b, m: (s_blk0 + b, 0)),
            pl.BlockSpec((DEC_SEQ, NA_WIDTH), lambda b, m: (s_blk0 + b, 0)),
            cache_spec,
            cache_spec,
        ],
        out_specs=pl.BlockSpec((NA_TQ, NA_WIDTH), lambda b, m: (b * NA_QBLOCKS + m, 0)),
        out_shape=jax.ShapeDtypeStruct((N_LAT, NA_WIDTH), BF16),
        scratch_shapes=[pltpu.VMEM((NA_HEADS * N_DROW, GRID_W, GRID_W), F32),
                        pltpu.VMEM((len(NA_BLOCK_KINDS), NA_HEADS, NA_TQ, NA_TK), F32)],
        compiler_params=_params(2),
        name="latent_attention",
    )(rpb, q, kb, vb, cache_k, cache_v)


MERGE_TM = 512


MERGE_CTX_TILES = N_CTX // MERGE_TM
BRANCH_W = 512


def _merge_kernel(layer, n_x, *refs):
    x_refs, rest = refs[:n_x], refs[n_x:]
    (mod_ref, ya_ref, yb_ctx_ref, yb_lat_ref, yc_ctx_ref, yc_lat_ref, g_ref, wa_hbm, wb_hbm, wc_hbm, wo_hbm,
     o_ref, wbr_ref, wo_ref, stage_ref, sem_ref) = rest
    i = pl.program_id(0)

    @pl.when(i == 0)
    def _():
        def store_branch(k):
            def st(v):
                wbr_ref[k] = v
            return st

        def store_out(k):
            def st(v):
                wo_ref[k * BRANCH_W:(k + 1) * BRANCH_W, :] = v
            return st

        chunks = [(w.at[layer], store_branch(k)) for k, w in enumerate((wa_hbm, wb_hbm, wc_hbm))]
        chunks += [(wo_hbm.at[layer, pl.ds(k * BRANCH_W, BRANCH_W), :], store_out(k))
                   for k in range(D_MODEL // BRANCH_W)]
        _load_cast(chunks, stage_ref, sem_ref)

    g = _sigmoid(g_ref[...].astype(F32))
    yb = _token_tile(i, MERGE_CTX_TILES, (yb_ctx_ref, yb_lat_ref))
    yc = _token_tile(i, MERGE_CTX_TILES, (yc_ctx_ref, yc_lat_ref))
    merged = (g[:, 0:D_MODEL] * jnp.dot(ya_ref[...], wbr_ref[0], preferred_element_type=F32)
              + g[:, D_MODEL:2 * D_MODEL] * jnp.dot(yb, wbr_ref[1], preferred_element_type=F32)
              + g[:, 2 * D_MODEL:] * jnp.dot(yc, wbr_ref[2], preferred_element_type=F32))
    y = jnp.dot(merged.astype(BF16), wo_ref[...], preferred_element_type=F32)
    gate = mod_ref[:, 2 * D_MODEL:3 * D_MODEL]
    o_ref[...] = _token_tile(i, MERGE_CTX_TILES, x_refs) + gate * y


def merge_branches(xs, mod3, ya, yb_ctx, yb_lat, yc_ctx, yc_lat, gates, wa, wb, wc, wo, layer):
    row = lambda i: (i, 0)
    hbm = pl.BlockSpec(memory_space=pl.ANY)
    x_specs = ([pl.BlockSpec((MERGE_TM, D_MODEL), row)] if len(xs) == 1
               else _split_specs((MERGE_TM, D_MODEL), MERGE_CTX_TILES))
    return pl.pallas_call(
        functools.partial(_merge_kernel, layer, len(xs)),
        grid=(N_TOK // MERGE_TM,),
        in_specs=x_specs + [
            pl.BlockSpec((None, 1, MOD_WIDTH), lambda i: (_cond_row(i, MERGE_TM), 0, 0)),
            pl.BlockSpec((MERGE_TM, SGU_WIDTH), row),
            *_split_specs((MERGE_TM, LRU_WIDTH), MERGE_CTX_TILES),
            *_split_specs((MERGE_TM, NA_WIDTH), MERGE_CTX_TILES),
            pl.BlockSpec((MERGE_TM, 3 * D_MODEL), row),
            hbm, hbm, hbm, hbm,
        ],
        out_specs=pl.BlockSpec((MERGE_TM, D_MODEL), row),
        out_shape=jax.ShapeDtypeStruct((N_TOK, D_MODEL), F32),
        scratch_shapes=[pltpu.VMEM((3, BRANCH_W, D_MODEL), BF16), pltpu.VMEM((D_MODEL, D_MODEL), BF16),
                        pltpu.VMEM((2, BRANCH_W, D_MODEL), F32), pltpu.SemaphoreType.DMA((2,))],
        compiler_params=_params(1),
        name="merge_branches",
    )(*xs, mod3, ya, yb_ctx, yb_lat, yc_ctx, yc_lat, gates, wa, wb, wc, wo)


RT_TM = 512
RT_ROWS = 32
N_BUCKETS = N_GROUPS
CNT_ROWS = 8
COMB_W = 128
HX_W = D_MODEL + COMB_W
EXP_TM = 512
EXP_TILES = N_TOK // EXP_TM + N_BUCKETS
HS_ROWS = EXP_TILES * EXP_TM


def _split_bf16(x):
    hi = x.astype(BF16)
    return hi, (x - hi.astype(F32)).astype(BF16)


def _router_kernel(x_ref, mod_ref, gain_ref, w_ref, b_ref, tri_ref, hx_ref, bucket_ref, rank_ref, cnt_out_ref,
                   cnt_ref, cpad_ref):
    @pl.when(pl.program_id(0) == 0)
    def _():
        cnt_ref[...] = jnp.zeros_like(cnt_ref)
        cpad_ref[...] = jnp.zeros_like(cpad_ref)

    m = mod_ref[...]
    shift, scale = m[:, 3 * D_MODEL:4 * D_MODEL], m[:, 4 * D_MODEL:5 * D_MODEL]
    h = _rms(x_ref[...], gain_ref[...]) * (1.0 + scale) + shift
    hx_ref[:, 0:D_MODEL] = h
    h_hi, h_lo = _split_bf16(h)
    w_hi, w_lo = _split_bf16(w_ref[...])
    dims = (((1,), (1,)), ((), ()))
    logits = (lax.dot_general(w_hi, h_hi, dims, preferred_element_type=F32)
              + lax.dot_general(w_hi, h_lo, dims, preferred_element_type=F32)
              + lax.dot_general(w_lo, h_hi, dims, preferred_element_type=F32)) + b_ref[...]
    gl = [logits[g:g + 1, :] for g in range(N_GROUPS)]
    gmax = functools.reduce(jnp.maximum, gl)
    gid = jnp.full(gmax.shape, N_GROUPS - 1, jnp.int32)
    for g in reversed(range(N_GROUPS - 1)):
        gid = jnp.where(gl[g] == gmax, g, gid)
    p_grp = 1.0 / functools.reduce(jnp.add, [jnp.exp(v - gmax) for v in gl])
    el = []
    for e in range(EXPERTS_PER_GROUP):
        v = logits[8 + e:9 + e, :]
        for g in range(1, N_GROUPS):
            row = 8 + g * EXPERTS_PER_GROUP + e
            v = jnp.where(gid == g, logits[row:row + 1, :], v)
        el.append(v)
    top1 = functools.reduce(jnp.maximum, el)
    idx1 = jnp.full(top1.shape, EXPERTS_PER_GROUP - 1, jnp.int32)
    for e in reversed(range(EXPERTS_PER_GROUP - 1)):
        idx1 = jnp.where(el[e] == top1, e, idx1)
    rest = [jnp.where(idx1 == e, -jnp.inf, el[e]) for e in range(EXPERTS_PER_GROUP)]
    top2 = functools.reduce(jnp.maximum, rest)
    idx2 = jnp.full(top1.shape, EXPERTS_PER_GROUP - 1, jnp.int32)
    for e in reversed(range(EXPERTS_PER_GROUP - 1)):
        idx2 = jnp.where(rest[e] == top2, e, idx2)
    e2 = jnp.exp(top2 - top1)
    w1 = p_grp / (1.0 + e2)
    w2 = p_grp * e2 / (1.0 + e2)
    for e in range(EXPERTS_PER_GROUP):
        cpad_ref[e:e + 1, :] = jnp.where(idx1 == e, w1, 0.0) + jnp.where(idx2 == e, w2, 0.0)
    hx_ref[:, D_MODEL:] = cpad_ref[...].T
    bucket_ref[...] = gid
    sub = lax.broadcasted_iota(jnp.int32, (CNT_ROWS, RT_TM), 0)
    onehot = jnp.where(sub == gid, 1.0, 0.0)
    seen = jnp.dot(onehot.astype(BF16), tri_ref[...], preferred_element_type=F32)
    cnt = cnt_ref[...]
    rank_ref[...] = jnp.sum(onehot * (seen - 1.0 + cnt[:, 0:1]), axis=0, keepdims=True).astype(jnp.int32)
    cnt = cnt + jnp.sum(onehot, axis=1, keepdims=True)
    cnt_ref[...] = cnt
    cnt_out_ref[...] = cnt.astype(jnp.int32)


def router(x, mod3, gain, w_rt, b_rt):
    tri = jnp.asarray(np.triu(np.ones((RT_TM, RT_TM), np.float32)), BF16)
    return pl.pallas_call(
        _router_kernel,
        grid=(N_TOK // RT_TM,),
        in_specs=[
            pl.BlockSpec((RT_TM, D_MODEL), lambda i: (i, 0)),
            pl.BlockSpec((None, 1, MOD_WIDTH), lambda i: (_cond_row(i, RT_TM), 0, 0)),
            pl.BlockSpec((1, D_MODEL), lambda i: (0, 0)),
            pl.BlockSpec((RT_ROWS, D_MODEL), lambda i: (0, 0)),
            pl.BlockSpec((RT_ROWS, 1), lambda i: (0, 0)),
            pl.BlockSpec((RT_TM, RT_TM), lambda i: (0, 0)),
        ],
        out_specs=[
            pl.BlockSpec((RT_TM, HX_W), lambda i: (i, 0)),
            pl.BlockSpec((1, RT_TM), lambda i: (0, i)),
            pl.BlockSpec((1, RT_TM), lambda i: (0, i)),
            pl.BlockSpec((CNT_ROWS, COMB_W), lambda i: (0, 0)),
        ],
        out_shape=[
            jax.ShapeDtypeStruct((N_TOK, HX_W), F32),
            jax.ShapeDtypeStruct((1, N_TOK), jnp.int32),
            jax.ShapeDtypeStruct((1, N_TOK), jnp.int32),
            jax.ShapeDtypeStruct((CNT_ROWS, COMB_W), jnp.int32),
        ],
        scratch_shapes=[pltpu.VMEM((CNT_ROWS, COMB_W), F32), pltpu.VMEM((COMB_W, RT_TM), F32)],
        compiler_params=_params(1),
        name="router",
    )(x, mod3, gain, w_rt, b_rt, tri)


def _router_weights(w_grp, b_grp, w_exp, b_exp):
    w = jnp.zeros((RT_ROWS, D_MODEL), F32).at[0:N_GROUPS].set(w_grp.T).at[8:8 + N_EXPERTS].set(w_exp.T)
    b = jnp.zeros((RT_ROWS, 1), F32).at[0:N_GROUPS, 0].set(b_grp).at[8:8 + N_EXPERTS, 0].set(b_exp)
    return w, b


DISP_TM = 512


def _sorted_row(bucket_ref, rank_ref, start_ref, k):
    return start_ref[bucket_ref[k]] + rank_ref[k]


def _dispatch_kernel(bucket_ref, rank_ref, start_ref, cnt_ref, nt_ref, hx_ref, hs_ref, zero_ref, sem, zsem):
    i = pl.program_id(0)
    base = i * DISP_TM

    def issue(r, carry):
        row = _sorted_row(bucket_ref, rank_ref, start_ref, base + r)
        pltpu.make_async_copy(hx_ref.at[pl.ds(r, 1), :], hs_ref.at[pl.ds(row, 1), :], sem).start()
        return carry

    lax.fori_loop(0, DISP_TM, issue, 0, unroll=8)
    pltpu.make_async_copy(hx_ref, hs_ref.at[pl.ds(0, DISP_TM), :], sem).wait()

    @pl.when(i == pl.num_programs(0) - 1)
    def _():
        zero_ref[...] = jnp.zeros_like(zero_ref)

        def pad_copy(row):
            return pltpu.make_async_copy(zero_ref.at[pl.ds(0, 1), :], hs_ref.at[pl.ds(row, 1), :], zsem)

        def pad_issue(row, carry):
            pad_copy(row).start()
            return carry

        def pad_wait(row, carry):
            pad_copy(row).wait()
            return carry

        for g in range(N_BUCKETS):
            lo = start_ref[g] + cnt_ref[g]
            hi = start_ref[g] + ((cnt_ref[g] + EXP_TM - 1) // EXP_TM) * EXP_TM
            lax.fori_loop(lo, hi, pad_issue, 0)
            lax.fori_loop(lo, hi, pad_wait, 0)

        def tile_copy(t):
            return pltpu.make_async_copy(zero_ref, hs_ref.at[pl.ds(pl.multiple_of(t * EXP_TM, EXP_TM), EXP_TM), :],
                                         zsem)

        def tile_issue(t, carry):
            tile_copy(t).start()
            return carry

        def tile_wait(t, carry):
            tile_copy(t).wait()
            return carry

        lax.fori_loop(nt_ref[0], EXP_TILES, tile_issue, 0)
        lax.fori_loop(nt_ref[0], EXP_TILES, tile_wait, 0)


def dispatch(bucket, rank, starts, counts, n_tiles, hx):
    return pl.pallas_call(
        _dispatch_kernel,
        grid_spec=pltpu.PrefetchScalarGridSpec(
            num_scalar_prefetch=5,
            grid=(N_TOK // DISP_TM,),
            in_specs=[pl.BlockSpec((DISP_TM, HX_W), lambda i, *_: (i, 0))],
            out_specs=pl.BlockSpec(memory_space=pl.ANY),
            scratch_shapes=[pltpu.VMEM((EXP_TM, HX_W), F32), pltpu.SemaphoreType.DMA(()),
                            pltpu.SemaphoreType.DMA(())],
        ),
        out_shape=jax.ShapeDtypeStruct((HS_ROWS, HX_W), F32),
        compiler_params=_params(1),
        name="dispatch",
    )(bucket, rank, starts, counts, n_tiles, hx)


def _experts_kernel(layer, tg_ref, nt_ref, hs_ref, w1_hbm, w3_hbm, w2_hbm, ys_ref, w1_ref, w3_ref, w2_ref,
                    stage_in_ref, stage_out_ref, sem_ref):
    t = pl.program_id(0)
    group = tg_ref[t]
    new_group = jnp.logical_or(t == 0, group != tg_ref[jnp.maximum(t - 1, 0)])

    @pl.when(jnp.logical_and(t < nt_ref[0], new_group))
    def _():
        def store(dst, e):
            def st(v):
                dst[e] = v
            return st

        def chunks(src, dst):
            return [(src.at[layer, group * EXPERTS_PER_GROUP + e], store(dst, e))
                    for e in range(EXPERTS_PER_GROUP)]

        _load_cast(chunks(w1_hbm, w1_ref) + chunks(w3_hbm, w3_ref), stage_in_ref, sem_ref)
        _load_cast(chunks(w2_hbm, w2_ref), stage_out_ref, sem_ref)

    @pl.when(t < nt_ref[0])
    def _():
        h = hs_ref[:, 0:D_MODEL].astype(BF16)
        c = hs_ref[:, D_MODEL:]
        acc = None
        for e in range(EXPERTS_PER_GROUP):
            a = jnp.dot(h, w1_ref[e], preferred_element_type=F32)
            b = jnp.dot(h, w3_ref[e], preferred_element_type=F32)
            hid = (a * _sigmoid(a)) * b * c[:, e:e + 1]
            y = jnp.dot(hid.astype(BF16), w2_ref[e], preferred_element_type=F32)
            acc = y if acc is None else acc + y
        ys_ref[...] = acc

    @pl.when(t >= nt_ref[0])
    def _():
        ys_ref[...] = jnp.zeros_like(ys_ref)


def experts(tile_group, n_tiles, hs, w1, w3, w2, layer):
    hbm = pl.BlockSpec(memory_space=pl.ANY)
    return pl.pallas_call(
        functools.partial(_experts_kernel, layer),
        grid_spec=pltpu.PrefetchScalarGridSpec(
            num_scalar_prefetch=2,
            grid=(EXP_TILES,),
            in_specs=[pl.BlockSpec((EXP_TM, HX_W), lambda t, tg, nt: (t, 0)), hbm, hbm, hbm],
            out_specs=pl.BlockSpec((EXP_TM, D_MODEL), lambda t, tg, nt: (t, 0)),
            scratch_shapes=[
                pltpu.VMEM((EXPERTS_PER_GROUP, D_MODEL, D_EXPERT), BF16),
                pltpu.VMEM((EXPERTS_PER_GROUP, D_MODEL, D_EXPERT), BF16),
                pltpu.VMEM((EXPERTS_PER_GROUP, D_EXPERT, D_MODEL), BF16),
                pltpu.VMEM((2, D_MODEL, D_EXPERT), F32),
                pltpu.VMEM((2, D_EXPERT, D_MODEL), F32),
                pltpu.SemaphoreType.DMA((2,)),
            ],
        ),
        out_shape=jax.ShapeDtypeStruct((HS_ROWS, D_MODEL), F32),
        compiler_params=_params(1),
        name="experts",
    )(tile_group, n_tiles, hs, w1, w3, w2)


def _expert_tiles(counts):
    tiles = (counts + EXP_TM - 1) // EXP_TM
    ends = jnp.cumsum(tiles)
    t = jnp.arange(EXP_TILES, dtype=jnp.int32)
    group = jnp.minimum(jnp.sum(t[:, None] >= ends[None, :], axis=1), N_BUCKETS - 1).astype(jnp.int32)
    return ((ends - tiles) * EXP_TM).astype(jnp.int32), group, ends[-1:].astype(jnp.int32)


RES_TM = 512
RES_CTX_TILES = N_CTX // RES_TM


def _moe_residual_kernel(final, bucket_ref, rank_ref, start_ref, x_ref, mod_ref, gain_ref, ys_ref, *refs):
    out_refs, (ybuf_ref, sem) = refs[:-2], refs[-2:]
    i = pl.program_id(0)
    base = i * RES_TM

    def issue(r, carry):
        row = _sorted_row(bucket_ref, rank_ref, start_ref, base + r)
        pltpu.make_async_copy(ys_ref.at[pl.ds(row, 1), :], ybuf_ref.at[pl.ds(r, 1), :], sem).start()
        return carry

    lax.fori_loop(0, RES_TM, issue, 0, unroll=8)
    pltpu.make_async_copy(ys_ref.at[pl.ds(0, RES_TM), :], ybuf_ref, sem).wait()
    x = x_ref[...] + mod_ref[:, 5 * D_MODEL:6 * D_MODEL] * ybuf_ref[...]
    if not final:
        out_refs[0][...] = x
        return
    y = _rms(x, gain_ref[...])

    @pl.when(i < RES_CTX_TILES)
    def _():
        out_refs[0][...] = y

    @pl.when(i >= RES_CTX_TILES)
    def _():
        out_refs[1][...] = y


def moe_residual(bucket, rank, starts, x, mod3, gain, ys, final):
    if final:
        out_specs = _split_specs((RES_TM, D_MODEL), RES_CTX_TILES)
        out_shape = [jax.ShapeDtypeStruct((N_CTX, D_MODEL), F32), jax.ShapeDtypeStruct((N_LAT, D_MODEL), F32)]
    else:
        out_specs = pl.BlockSpec((RES_TM, D_MODEL), lambda i, *_: (i, 0))
        out_shape = jax.ShapeDtypeStruct((N_TOK, D_MODEL), F32)
    return pl.pallas_call(
        functools.partial(_moe_residual_kernel, final),
        grid_spec=pltpu.PrefetchScalarGridSpec(
            num_scalar_prefetch=3,
            grid=(N_TOK // RES_TM,),
            in_specs=[
                pl.BlockSpec((RES_TM, D_MODEL), lambda i, *_: (i, 0)),
                pl.BlockSpec((None, 1, MOD_WIDTH), lambda i, *_: (_cond_row(i, RES_TM), 0, 0)),
                pl.BlockSpec((1, D_MODEL), lambda i, *_: (0, 0)),
                pl.BlockSpec(memory_space=pl.ANY),
            ],
            out_specs=out_specs,
            scratch_shapes=[pltpu.VMEM((RES_TM, D_MODEL), F32), pltpu.SemaphoreType.DMA(())],
        ),
        out_shape=out_shape,
        compiler_params=_params(1),
        name="moe_residual_final" if final else "moe_residual",
    )(bucket, rank, starts, x, mod3, gain, ys)


def kernel(x_prompt, x_sample, cache_k, cache_v, state_lru, c, c_ctx, w_mod, b_mod, norm_mix, norm_ffn, w_in, sgu_norm, sgu_w, sgu_b, lru_conv_w, lru_conv_b, lru_w_r, lru_b_r, lru_w_i, lru_b_i, lru_lambda, na_rpb, w_branch_sgu, w_branch_lru, w_branch_na, w_out, moe_w_group, moe_b_group, moe_w_expert, moe_b_expert, moe_w1, moe_w3, moe_w2, final_norm_gain):
    xs = (x_prompt.reshape(N_CTX, D_MODEL), x_sample.reshape(N_LAT, D_MODEL))
    cond = jnp.zeros((N_COND, D_MODEL), F32).at[0].set(c_ctx).at[1:1 + DEC_BATCH].set(c)
    mod = modulation(cond, w_mod, b_mod)
    zero_state = jnp.zeros((BATCH, 2, LRU_WIDTH), F32)
    cache_k = cache_k.reshape(DEC_BATCH, DEPTH, PAST_LEN, NA_WIDTH)
    cache_v = cache_v.reshape(DEC_BATCH, DEPTH, PAST_LEN, NA_WIDTH)
    final_gain = final_norm_gain.reshape(1, D_MODEL)
    ks, vs, ss = [], [], []
    for l in range(DEPTH):
        mod3 = mod[l].reshape(N_COND, 1, MOD_WIDTH)
        za, q, kb, vb, k_ctx, v_ctx, gates = in_projection(xs, mod3, norm_mix[l].reshape(1, D_MODEL), w_in, l)
        ya = spatial_gating(za, sgu_norm[l].reshape(1, SGU_WIDTH), sgu_w[l], sgu_b[l].T)
        w_lo, w_hi = _block_diag_gate_weights(lru_w_r[l], lru_w_i[l])
        lru_args = (lru_conv_w[l], lru_conv_b[l].reshape(1, LRU_WIDTH), w_lo, w_hi, lru_b_r[l], lru_b_i[l],
                    lru_lambda[l])
        yb_ctx, st_ctx = rglru(za, *lru_args, zero_state, SEQ, BATCH, 0)
        yb_lat, _ = rglru(za, *lru_args, state_lru[:, l], DEC_SEQ, DEC_BATCH, N_CTX)
        yc_ctx = context_attention(q, kb, vb)
        yc_lat = latent_attention(q, kb, vb, cache_k, cache_v, na_rpb[l].reshape(-1), l)
        x = merge_branches(xs, mod3, ya, yb_ctx, yb_lat, yc_ctx, yc_lat, gates,
                           w_branch_sgu, w_branch_lru, w_branch_na, w_out, l)
        w_rt, b_rt = _router_weights(moe_w_group[l], moe_b_group[l], moe_w_expert[l], moe_b_expert[l])
        hx, bucket, rank, counts = router(x, mod3, norm_ffn[l].reshape(1, D_MODEL), w_rt, b_rt)
        bucket, rank, counts = bucket.reshape(N_TOK), rank.reshape(N_TOK), counts[:N_BUCKETS, 0]
        starts, tile_group, n_tiles = _expert_tiles(counts)
        hs = dispatch(bucket, rank, starts, counts, n_tiles, hx)
        ys = experts(tile_group, n_tiles, hs, moe_w1, moe_w3, moe_w2, l)
        if l < DEPTH - 1:
            xs = (moe_residual(bucket, rank, starts, x, mod3, final_gain, ys, False),)
        else:
            y_ctx, y_lat = moe_residual(bucket, rank, starts, x, mod3, final_gain, ys, True)
        ks.append(k_ctx.reshape(BATCH, SEQ, NA_HEADS, NA_HEAD_DIM))
        vs.append(v_ctx.reshape(BATCH, SEQ, NA_HEADS, NA_HEAD_DIM))
        ss.append(st_ctx)
    return (y_ctx.reshape(BATCH, SEQ, D_MODEL), y_lat.reshape(DEC_BATCH, DEC_SEQ, D_MODEL),
            jnp.stack(ks, axis=1), jnp.stack(vs, axis=1), jnp.stack(ss, axis=1))
```

```python
import functools

import jax
import jax.numpy as jnp
import numpy as np
from jax import lax
from jax.experimental import pallas as pl
from jax.experimental.pallas import tpu as pltpu

F32 = jnp.float32
BF16 = jnp.bfloat16

D_MODEL = 1024
BATCH = 16
SEQ = 256
DEPTH = 2
DEC_BATCH = 4
DEC_SEQ = 2048
PAST_LEN = 512
GRID_W = 64
CHUNK = 128
SGU_WIDTH = 512
SGU_GROUPS = 4
LRU_WIDTH = 512
LRU_BLOCKS = 8
CONV_WIDTH = 4
LRU_C = 8.0
NA_HEADS = 8
NA_HEAD_DIM = 64
NA_WIDTH = 512
WIN_ROWS = 8
WIN_COLS = 16
N_GROUPS = 4
EXPERTS_PER_GROUP = 4
N_EXPERTS = 16
D_EXPERT = 512
IN_WIDTH = 6656
EPS = 1e-6
NEG_INF = -1e30

N_CTX = BATCH * SEQ
N_LAT = DEC_BATCH * DEC_SEQ
N_TOK = N_CTX + N_LAT
N_COND = 8
MOD_WIDTH = 6 * D_MODEL
GRID_ROWS = DEC_SEQ // GRID_W

VMEM_LIMIT_BYTES = 56 * 1024 * 1024


def _params(n_axes):
    return pltpu.CompilerParams(dimension_semantics=("arbitrary",) * n_axes,
                                vmem_limit_bytes=VMEM_LIMIT_BYTES)


def _cond_row(tile, tile_rows):
    tok = tile * tile_rows
    return jnp.where(tok < N_CTX, 0, 1 + (tok - N_CTX) // DEC_SEQ)


def _rms(x, gain):
    return x * lax.rsqrt(jnp.mean(x * x, axis=-1, keepdims=True) + EPS) * gain


def _bdot(a, b):
    return jnp.dot(a.astype(BF16), b.astype(BF16), preferred_element_type=F32)


def _bdot_t(a, b):
    return lax.dot_general(a.astype(BF16), b.astype(BF16), (((1,), (1,)), ((), ())),
                           preferred_element_type=F32)


MOD_TN = 1536


def _mod_kernel(cond_ref, w_ref, b_ref, o_ref):
    c = cond_ref[...]
    s = c * jax.nn.sigmoid(c)
    o_ref[...] = _bdot(s, w_ref[...]) + b_ref[...]


def modulation(cond, w_mod, b_mod):
    return pl.pallas_call(
        _mod_kernel,
        grid=(DEPTH, MOD_WIDTH // MOD_TN),
        in_specs=[
            pl.BlockSpec((N_COND, D_MODEL), lambda l, j: (0, 0)),
            pl.BlockSpec((None, D_MODEL, MOD_TN), lambda l, j: (l, 0, j)),
            pl.BlockSpec((None, 1, MOD_TN), lambda l, j: (l, 0, j)),
        ],
        out_specs=pl.BlockSpec((None, N_COND, MOD_TN), lambda l, j: (l, 0, j)),
        out_shape=jax.ShapeDtypeStruct((DEPTH, N_COND, MOD_WIDTH), F32),
        compiler_params=_params(2),
        name="modulation",
    )(cond, w_mod, b_mod.reshape(DEPTH, 1, MOD_WIDTH))


IN_TM = 256
IN_CTX_TILES = N_CTX // IN_TM
ZA_WIDTH = 4 * 512
KV_OFF = ZA_WIDTH + NA_WIDTH
GATE_OFF = KV_OFF + 2 * NA_WIDTH
W_CHUNK = 512
Q_SCALE = NA_HEAD_DIM ** -0.5 * float(np.log2(np.e))


def _load_cast(chunks, stage_ref, sem_ref):
    def copy(j):
        return pltpu.make_async_copy(chunks[j][0], stage_ref.at[j % 2], sem_ref.at[j % 2])

    copy(0).start()
    for j in range(len(chunks)):
        if j + 1 < len(chunks):
            copy(j + 1).start()
        copy(j).wait()
        chunks[j][1](stage_ref[j % 2].astype(BF16))


def _token_tile(i, n_ctx_tiles, refs):
    if len(refs) == 1:
        return refs[0][...]
    return jnp.where(i < n_ctx_tiles, refs[0][...], refs[1][...])


def _split_specs(block, n_ctx_tiles):
    return [pl.BlockSpec(block, lambda i, *_: (jnp.minimum(i, n_ctx_tiles - 1), 0)),
            pl.BlockSpec(block, lambda i, *_: (jnp.maximum(i - n_ctx_tiles, 0), 0))]


def _inproj_kernel(layer, n_x, *refs):
    x_refs, (mod_ref, gain_ref, w_hbm) = refs[:n_x], refs[n_x:n_x + 3]
    za_ref, q_ref, kb_ref, vb_ref, kc_ref, vc_ref, g_ref, w_ref, stage_ref, sem_ref = refs[n_x + 3:]
    i = pl.program_id(0)

    @pl.when(i == 0)
    def _():
        def store(c):
            def st(v):
                w_ref[:, c * W_CHUNK:(c + 1) * W_CHUNK] = v
            return st

        _load_cast([(w_hbm.at[layer, :, pl.ds(c * W_CHUNK, W_CHUNK)], store(c)) for c in range(IN_WIDTH // W_CHUNK)],
                   stage_ref, sem_ref)

    m = mod_ref[...]
    shift, scale = m[:, 0:D_MODEL], m[:, D_MODEL:2 * D_MODEL]
    h = (_rms(_token_tile(i, IN_CTX_TILES, x_refs), gain_ref[...]) * (1.0 + scale) + shift).astype(BF16)
    za_ref[...] = jnp.dot(h, w_ref[:, 0:ZA_WIDTH], preferred_element_type=F32).astype(BF16)
    q_ref[...] = (jnp.dot(h, w_ref[:, ZA_WIDTH:KV_OFF], preferred_element_type=F32) * Q_SCALE).astype(BF16)
    kv = jnp.dot(h, w_ref[:, KV_OFF:GATE_OFF], preferred_element_type=F32)
    kb_ref[...] = kv[:, 0:NA_WIDTH].astype(BF16)
    vb_ref[...] = kv[:, NA_WIDTH:].astype(BF16)
    g_ref[...] = jnp.dot(h, w_ref[:, GATE_OFF:], preferred_element_type=F32).astype(BF16)

    @pl.when(i < IN_CTX_TILES)
    def _():
        kc_ref[...] = kv[:, 0:NA_WIDTH]
        vc_ref[...] = kv[:, NA_WIDTH:]


def in_projection(xs, mod3, gain, w_in, layer):
    row = lambda i: (i, 0)
    ctx_row = lambda i: (jnp.minimum(i, IN_CTX_TILES - 1), 0)
    x_specs = ([pl.BlockSpec((IN_TM, D_MODEL), row)] if len(xs) == 1
               else _split_specs((IN_TM, D_MODEL), IN_CTX_TILES))
    return pl.pallas_call(
        functools.partial(_inproj_kernel, layer, len(xs)),
        grid=(N_TOK // IN_TM,),
        in_specs=x_specs + [
            pl.BlockSpec((None, 1, MOD_WIDTH), lambda i: (_cond_row(i, IN_TM), 0, 0)),
            pl.BlockSpec((1, D_MODEL), lambda i: (0, 0)),
            pl.BlockSpec(memory_space=pl.ANY),
        ],
        out_specs=[
            pl.BlockSpec((IN_TM, ZA_WIDTH), row),
            pl.BlockSpec((IN_TM, NA_WIDTH), row),
            pl.BlockSpec((IN_TM, NA_WIDTH), row),
            pl.BlockSpec((IN_TM, NA_WIDTH), row),
            pl.BlockSpec((IN_TM, NA_WIDTH), ctx_row),
            pl.BlockSpec((IN_TM, NA_WIDTH), ctx_row),
            pl.BlockSpec((IN_TM, 3 * D_MODEL), row),
        ],
        out_shape=[
            jax.ShapeDtypeStruct((N_TOK, ZA_WIDTH), BF16),
            jax.ShapeDtypeStruct((N_TOK, NA_WIDTH), BF16),
            jax.ShapeDtypeStruct((N_TOK, NA_WIDTH), BF16),
            jax.ShapeDtypeStruct((N_TOK, NA_WIDTH), BF16),
            jax.ShapeDtypeStruct((N_CTX, NA_WIDTH), F32),
            jax.ShapeDtypeStruct((N_CTX, NA_WIDTH), F32),
            jax.ShapeDtypeStruct((N_TOK, 3 * D_MODEL), BF16),
        ],
        scratch_shapes=[pltpu.VMEM((D_MODEL, IN_WIDTH), BF16), pltpu.VMEM((2, D_MODEL, W_CHUNK), F32),
                        pltpu.SemaphoreType.DMA((2,))],
        compiler_params=_params(1),
        name="in_projection",
    )(*xs, mod3, gain, w_in)


SGU_TM = 512
SGU_GD = SGU_WIDTH // SGU_GROUPS


def _sgu_kernel(u_ref, v_ref, gain_ref, ws_ref, bs_ref, o_ref):
    gain = gain_ref[...]
    for c in range(SGU_TM // CHUNK):
        rows = slice(c * CHUNK, (c + 1) * CHUNK)
        u = jax.nn.gelu(u_ref[rows, :].astype(F32))
        v = _rms(jax.nn.gelu(v_ref[rows, :].astype(F32)), gain).astype(BF16)
        for g in range(SGU_GROUPS):
            cols = slice(g * SGU_GD, (g + 1) * SGU_GD)
            mixed = jnp.dot(ws_ref[g].astype(BF16), v[:, cols], preferred_element_type=F32) + bs_ref[:, g:g + 1]
            o_ref[rows, cols] = (u[:, cols] * mixed).astype(BF16)


def spatial_gating(za, gain, ws, bs_t):
    return pl.pallas_call(
        _sgu_kernel,
        grid=(N_TOK // SGU_TM,),
        in_specs=[
            pl.BlockSpec((SGU_TM, SGU_WIDTH), lambda i: (i, 0)),
            pl.BlockSpec((SGU_TM, SGU_WIDTH), lambda i: (i, 1)),
            pl.BlockSpec((1, SGU_WIDTH), lambda i: (0, 0)),
            pl.BlockSpec((SGU_GROUPS, CHUNK, CHUNK), lambda i: (0, 0, 0)),
            pl.BlockSpec((CHUNK, SGU_GROUPS), lambda i: (0, 0)),
        ],
        out_specs=pl.BlockSpec((SGU_TM, SGU_WIDTH), lambda i: (i, 0)),
        out_shape=jax.ShapeDtypeStruct((N_TOK, SGU_WIDTH), BF16),
        compiler_params=_params(1),
        name="spatial_gating",
    )(za, za, gain, ws, bs_t)


LRU_TC = 256
LRU_HALF = 256
SUB = 8
HALO = 8
LANES = 128
LRU_SLABS = LRU_WIDTH // LANES


def _sigmoid(x):
    return 0.5 * jnp.tanh(0.5 * x) + 0.5


def _lru_kernel(seq_len, xr_ref, gr_ref, cw_ref, cb_ref, wlo_ref, whi_ref, br_ref, bi_ref,
                lam_ref, h0_ref, y_ref, st_ref, xp_ref, a_ref, b_ref):
    seg_len = seq_len // SUB
    pitch = seg_len + SUB
    segs_per_chunk = max(LRU_TC // seg_len, 1)
    n_chunks = seq_len // LRU_TC
    zeros = jnp.zeros((HALO, LRU_WIDTH), F32)
    xp_ref[0:HALO, :] = zeros
    xp_ref[seq_len + HALO:seq_len + 2 * HALO, :] = zeros

    def copy_in(c, carry):
        r0 = pl.multiple_of(c * LRU_TC, LRU_TC)
        xp_ref[pl.ds(r0 + HALO, LRU_TC), :] = xr_ref[pl.ds(r0, LRU_TC), :].astype(F32)
        return carry

    lax.fori_loop(0, n_chunks, copy_in, 0)

    def chunk_rows(c, seg):
        n = min(seg_len, LRU_TC)
        start = pl.multiple_of((c * segs_per_chunk + seg) * pitch, SUB)
        return pl.ds(start, n), slice(seg * n, (seg + 1) * n)

    cw = cw_ref[...]
    cb = cb_ref[...]
    win = LRU_TC + 2 * HALO
    neg_lam = -lam_ref[...]
    softplus = jnp.maximum(neg_lam, 0.0) + jnp.log1p(jnp.exp(-jnp.abs(neg_lam)))
    decay = (0.5 * LRU_C) * softplus
    half_br = 0.5 * br_ref[...]
    half_bi = 0.5 * bi_ref[...]

    def gates(c, carry):
        r0 = pl.multiple_of(c * LRU_TC, LRU_TC)
        w = xp_ref[pl.ds(r0, win), :]
        xc = (cw[0:1, :] * pltpu.roll(w, 1, 0)[HALO:HALO + LRU_TC]
              + cw[1:2, :] * w[HALO:HALO + LRU_TC]
              + cw[2:3, :] * pltpu.roll(w, win - 1, 0)[HALO:HALO + LRU_TC]
              + cw[3:4, :] * pltpu.roll(w, win - 2, 0)[HALO:HALO + LRU_TC]) + cb
        xb = xc.astype(BF16)
        for half, w_ref in enumerate((wlo_ref, whi_ref)):
            cols = slice(half * LRU_HALF, (half + 1) * LRU_HALF)
            pre = jnp.dot(xb[:, cols], w_ref[...], preferred_element_type=F32)
            half_x = 0.5 * xc[:, cols]
            for d in range(2):
                tr = jnp.tanh(0.5 * pre[:, (2 * d) * LRU_HALF:(2 * d + 1) * LRU_HALF] + half_br[d:d + 1, cols])
                ti = jnp.tanh(0.5 * pre[:, (2 * d + 1) * LRU_HALF:(2 * d + 2) * LRU_HALF] + half_bi[d:d + 1, cols])
                neg_log_a = decay[d:d + 1, cols] * tr + decay[d:d + 1, cols]
                a = jnp.exp2(neg_log_a * -LOG2E)
                z = jnp.tanh(neg_log_a) * (a * a + 1.0)
                b = jnp.where(z > 0.0, z * lax.rsqrt(z), 0.0) * (half_x * ti + half_x)
                for k in range(LRU_HALF // LANES):
                    slab = half * (LRU_HALF // LANES) + k
                    lanes = slice(k * LANES, (k + 1) * LANES)
                    for seg in range(segs_per_chunk):
                        dst, src = chunk_rows(c, seg)
                        a_ref[d, slab, dst, :] = a[src, lanes]
                        b_ref[d, slab, dst, :] = b[src, lanes]
        return carry

    lax.fori_loop(0, n_chunks, gates, 0)

    def step_rows(d, j):
        return pl.ds(j if d == 0 else seg_len - 1 - j, SUB, stride=pitch)

    def reduce_step(j, carry):
        out = []
        for d in range(2):
            for slab in range(LRU_SLABS):
                big_a, big_b = carry[d * LRU_SLABS + slab]
                a = a_ref[d, slab, step_rows(d, j), :]
                b = b_ref[d, slab, step_rows(d, j), :]
                out.append((a * big_a, a * big_b + b))
        return tuple(out)

    ident = (jnp.ones((SUB, LANES), F32), jnp.zeros((SUB, LANES), F32))
    totals = lax.fori_loop(0, seg_len, reduce_step, (ident,) * (2 * LRU_SLABS), unroll=2)

    sub = lax.broadcasted_iota(jnp.int32, (SUB, LANES), 0)
    entering = []
    for d in range(2):
        for slab in range(LRU_SLABS):
            big_a, big_b = totals[d * LRU_SLABS + slab]
            h = h0_ref[d:d + 1, slab * LANES:(slab + 1) * LANES]
            rows = jnp.zeros((SUB, LANES), F32)
            for s in (range(SUB) if d == 0 else reversed(range(SUB))):
                rows = jnp.where(sub == s, h, rows)
                h = big_a[s:s + 1, :] * h + big_b[s:s + 1, :]
            entering.append(rows)
            st_ref[d:d + 1, slab * LANES:(slab + 1) * LANES] = h

    def apply_step(j, carry):
        out = []
        for d in range(2):
            for slab in range(LRU_SLABS):
                h = (a_ref[d, slab, step_rows(d, j), :] * carry[d * LRU_SLABS + slab]
                     + b_ref[d, slab, step_rows(d, j), :])
                b_ref[d, slab, step_rows(d, j), :] = h
                out.append(h)
        return tuple(out)

    lax.fori_loop(0, seg_len, apply_step, tuple(entering), unroll=2)

    def merge(c, carry):
        r0 = pl.multiple_of(c * LRU_TC, LRU_TC)
        gate = jax.nn.gelu(gr_ref[pl.ds(r0, LRU_TC), :].astype(F32))
        for slab in range(LRU_SLABS):
            lanes = slice(slab * LANES, (slab + 1) * LANES)
            for seg in range(segs_per_chunk):
                src, dst = chunk_rows(c, seg)
                h = b_ref[0, slab, src, :] + b_ref[1, slab, src, :]
                y_ref[pl.ds(r0 + dst.start, dst.stop - dst.start), lanes] = (h * gate[dst, lanes]).astype(BF16)
        return carry

    lax.fori_loop(0, n_chunks, merge, 0)


def rglru(za, conv_w, conv_b, w_lo, w_hi, b_r, b_i, lam, h0, seq_len, n_seq, tok_off):
    blk0 = tok_off // seq_len
    full = lambda shape: pl.BlockSpec(shape, lambda i: (0,) * len(shape))
    return pl.pallas_call(
        functools.partial(_lru_kernel, seq_len),
        grid=(n_seq,),
        in_specs=[
            pl.BlockSpec((seq_len, LRU_WIDTH), lambda i: (i + blk0, 2)),
            pl.BlockSpec((seq_len, LRU_WIDTH), lambda i: (i + blk0, 3)),
            full((CONV_WIDTH, LRU_WIDTH)),
            full((1, LRU_WIDTH)),
            full((LRU_HALF, 4 * LRU_HALF)),
            full((LRU_HALF, 4 * LRU_HALF)),
            full((2, LRU_WIDTH)),
            full((2, LRU_WIDTH)),
            full((2, LRU_WIDTH)),
            pl.BlockSpec((None, 2, LRU_WIDTH), lambda i: (i, 0, 0)),
        ],
        out_specs=[
            pl.BlockSpec((seq_len, LRU_WIDTH), lambda i: (i, 0)),
            pl.BlockSpec((None, 2, LRU_WIDTH), lambda i: (i, 0, 0)),
        ],
        out_shape=[
            jax.ShapeDtypeStruct((n_seq * seq_len, LRU_WIDTH), BF16),
            jax.ShapeDtypeStruct((n_seq, 2, LRU_WIDTH), F32),
        ],
        scratch_shapes=[
            pltpu.VMEM((seq_len + 2 * HALO, LRU_WIDTH), F32),
            pltpu.VMEM((2, LRU_SLABS, seq_len + SUB * SUB, LANES), F32),
            pltpu.VMEM((2, LRU_SLABS, seq_len + SUB * SUB, LANES), F32),
        ],
        compiler_params=_params(1),
        name=f"rglru_{seq_len}",
    )(za, za, conv_w, conv_b, w_lo, w_hi, b_r, b_i, lam, h0)


def _block_diag_gate_weights(w_r, w_i):
    per_half = LRU_BLOCKS // 2
    eye = jnp.eye(per_half, dtype=w_r.dtype)

    def bd(w):
        return jnp.einsum("hij,hg->higj", w, eye).reshape(LRU_HALF, LRU_HALF)

    tiles = []
    for half in range(2):
        sl = slice(half * per_half, (half + 1) * per_half)
        tiles.append(jnp.concatenate([bd(w_r[0, sl]), bd(w_i[0, sl]), bd(w_r[1, sl]), bd(w_i[1, sl])], axis=1))
    return tiles[0].astype(BF16), tiles[1].astype(BF16)


PAIR_W = 2 * NA_HEAD_DIM
LOG2E = float(np.log2(np.e))


def _head_lanes(shape, head):
    lane = lax.broadcasted_iota(jnp.int32, shape, 1)
    return (lane < NA_HEAD_DIM) if head == 0 else (lane >= NA_HEAD_DIM)


def _one_head(q_pair, head):
    return jnp.where(_head_lanes(q_pair.shape, head), q_pair.astype(F32), 0.0).astype(BF16)


def _values_and_ones(v_pair, head):
    return jnp.where(_head_lanes(v_pair.shape, head), v_pair.astype(F32), 1.0).astype(BF16)


def _normalise_pair(o0, o1):
    outs = [o / pltpu.roll(o, NA_HEAD_DIM, 1) for o in (o0, o1)]
    return jnp.where(_head_lanes(o0.shape, 0), outs[0], outs[1])


def _ctx_attn_kernel(q_ref, k_ref, v_ref, o_ref):
    for j in range(NA_HEADS // 2):
        cols = slice(j * PAIR_W, (j + 1) * PAIR_W)
        outs = []
        for head in range(2):
            s = _bdot_t(_one_head(q_ref[:, cols], head), k_ref[:, cols])
            p = jnp.exp2(s - jnp.max(s, axis=-1, keepdims=True))
            outs.append(jnp.dot(p.astype(BF16), _values_and_ones(v_ref[:, cols], head),
                                preferred_element_type=F32))
        o_ref[:, cols] = _normalise_pair(*outs).astype(BF16)


def context_attention(q, kb, vb):
    spec = pl.BlockSpec((SEQ, NA_WIDTH), lambda i: (i, 0))
    return pl.pallas_call(
        _ctx_attn_kernel,
        grid=(BATCH,),
        in_specs=[spec, spec, spec],
        out_specs=spec,
        out_shape=jax.ShapeDtypeStruct((N_CTX, NA_WIDTH), BF16),
        compiler_params=_params(1),
        name="context_attention",
    )(q, kb, vb)


NA_QROWS = 4
NA_TQ = NA_QROWS * GRID_W
NA_KROWS = NA_QROWS + WIN_ROWS
NA_TK = NA_KROWS * GRID_W
NA_QBLOCKS = GRID_ROWS // NA_QROWS


N_DROW = 2 * WIN_ROWS - 1
N_DCOL = 2 * WIN_COLS - 1
NA_BLOCK_KINDS = (0, 1, NA_QBLOCKS - 1)


def _first_key_row(qb):
    return (np.clip if isinstance(qb, int) else jnp.clip)(qb * NA_QROWS - WIN_ROWS // 2, 0, GRID_ROWS - NA_KROWS)


def _build_bias(rpb_ref, table_ref, bias_ref):
    qc = lax.broadcasted_iota(jnp.int32, (GRID_W, GRID_W), 0)
    kc = lax.broadcasted_iota(jnp.int32, (GRID_W, GRID_W), 1)
    col0 = jnp.clip(qc - WIN_COLS // 2, 0, GRID_W - WIN_COLS)
    col_ok = jnp.logical_and(kc >= col0, kc < col0 + WIN_COLS)
    d_col = jnp.clip(kc - qc, 1 - WIN_COLS, WIN_COLS - 1) + WIN_COLS - 1
    neg = jnp.full((GRID_W, GRID_W), NEG_INF, F32)

    def table_entry(idx, carry):
        t = neg
        for j in range(N_DCOL):
            t = jnp.where(d_col == j, rpb_ref[idx * N_DCOL + j], t)
        table_ref[idx] = jnp.where(col_ok, t * LOG2E, neg)
        return carry

    lax.fori_loop(0, NA_HEADS * N_DROW, table_entry, 0)

    def head_blocks(h, carry):
        for kind, qb in enumerate(NA_BLOCK_KINDS):
            for i in range(NA_QROWS):
                qrow = qb * NA_QROWS + i
                win0 = int(np.clip(qrow - WIN_ROWS // 2, 0, GRID_ROWS - WIN_ROWS))
                for kr in range(NA_KROWS):
                    krow = int(_first_key_row(qb)) + kr
                    inside = win0 <= krow < win0 + WIN_ROWS
                    blk = table_ref[h * N_DROW + (krow - qrow + WIN_ROWS - 1)] if inside else neg
                    bias_ref[kind, h, i * GRID_W:(i + 1) * GRID_W, kr * GRID_W:(kr + 1) * GRID_W] = blk
        return carry

    lax.fori_loop(0, NA_HEADS, head_blocks, 0)


def _lat_attn_kernel(rpb_ref, q_ref, k_ref, v_ref, ck_ref, cv_ref, o_ref, table_ref, bias_ref):
    qb = pl.program_id(1)

    @pl.when(jnp.logical_and(pl.program_id(0) == 0, qb == 0))
    def _():
        _build_bias(rpb_ref, table_ref, bias_ref)

    kind = jnp.where(qb == 0, 0, jnp.where(qb == NA_QBLOCKS - 1, 2, 1))
    k0 = pl.multiple_of(_first_key_row(qb) * GRID_W, GRID_W)
    for j in range(NA_HEADS // 2):
        cols = slice(j * PAIR_W, (j + 1) * PAIR_W)
        k_loc = k_ref[pl.ds(k0, NA_TK), cols]
        v_loc = v_ref[pl.ds(k0, NA_TK), cols]
        k_ctx = ck_ref[:, cols].astype(BF16)
        v_ctx = cv_ref[:, cols]
        outs = []
        for head in range(2):
            qh = _one_head(q_ref[:, cols], head)
            s_loc = _bdot_t(qh, k_loc) + bias_ref[kind, 2 * j + head]
            s_ctx = _bdot_t(qh, k_ctx)
            m = jnp.maximum(jnp.max(s_loc, axis=-1, keepdims=True), jnp.max(s_ctx, axis=-1, keepdims=True))
            p_loc = jnp.exp2(s_loc - m)
            p_ctx = jnp.exp2(s_ctx - m)
            outs.append(jnp.dot(p_loc.astype(BF16), _values_and_ones(v_loc, head), preferred_element_type=F32)
                        + jnp.dot(p_ctx.astype(BF16), _values_and_ones(v_ctx, head), preferred_element_type=F32))
        o_ref[:, cols] = _normalise_pair(*outs).astype(BF16)


def latent_attention(q, kb, vb, cache_k, cache_v, rpb, layer):
    q_blk0 = N_CTX // NA_TQ
    s_blk0 = N_CTX // DEC_SEQ
    cache_spec = pl.BlockSpec((None, None, PAST_LEN, NA_WIDTH), lambda b, m: (b, layer, 0, 0))
    return pl.pallas_call(
        _lat_attn_kernel,
        grid=(DEC_BATCH, NA_QBLOCKS),
        in_specs=[
            pl.BlockSpec(memory_space=pltpu.SMEM),
            pl.BlockSpec((NA_TQ, NA_WIDTH), lambda b, m: (q_blk0 + b * NA_QBLOCKS + m, 0)),
            pl.BlockSpec((DEC_SEQ, NA_WIDTH), lambda b, m: (s_blk0 + b, 0)),
            pl.BlockSpec((DEC_SEQ, NA_WIDTH), lambda b, m: (s_blk0 + b, 0)),
            cache_spec,
            cache_spec,
        ],
        out_specs=pl.BlockSpec((NA_TQ, NA_WIDTH), lambda b, m: (b * NA_QBLOCKS + m, 0)),
        out_shape=jax.ShapeDtypeStruct((N_LAT, NA_WIDTH), BF16),
        scratch_shapes=[pltpu.VMEM((NA_HEADS * N_DROW, GRID_W, GRID_W), F32),
                        pltpu.VMEM((len(NA_BLOCK_KINDS), NA_HEADS, NA_TQ, NA_TK), F32)],
        compiler_params=_params(2),
        name="latent_attention",
    )(rpb, q, kb, vb, cache_k, cache_v)


MERGE_TM = 512


MERGE_CTX_TILES = N_CTX // MERGE_TM
BRANCH_W = 512


def _merge_kernel(layer, n_x, *refs):
    x_refs, rest = refs[:n_x], refs[n_x:]
    (mod_ref, ya_ref, yb_ctx_ref, yb_lat_ref, yc_ctx_ref, yc_lat_ref, g_ref, wa_hbm, wb_hbm, wc_hbm, wo_hbm,
     o_ref, wbr_ref, wo_ref, stage_ref, sem_ref) = rest
    i = pl.program_id(0)

    @pl.when(i == 0)
    def _():
        def store_branch(k):
            def st(v):
                wbr_ref[k] = v
            return st

        def store_out(k):
            def st(v):
                wo_ref[k * BRANCH_W:(k + 1) * BRANCH_W, :] = v
            return st

        chunks = [(w.at[layer], store_branch(k)) for k, w in enumerate((wa_hbm, wb_hbm, wc_hbm))]
        chunks += [(wo_hbm.at[layer, pl.ds(k * BRANCH_W, BRANCH_W), :], store_out(k))
                   for k in range(D_MODEL // BRANCH_W)]
        _load_cast(chunks, stage_ref, sem_ref)

    g = _sigmoid(g_ref[...].astype(F32))
    yb = _token_tile(i, MERGE_CTX_TILES, (yb_ctx_ref, yb_lat_ref))
    yc = _token_tile(i, MERGE_CTX_TILES, (yc_ctx_ref, yc_lat_ref))
    merged = (g[:, 0:D_MODEL] * jnp.dot(ya_ref[...], wbr_ref[0], preferred_element_type=F32)
              + g[:, D_MODEL:2 * D_MODEL] * jnp.dot(yb, wbr_ref[1], preferred_element_type=F32)
              + g[:, 2 * D_MODEL:] * jnp.dot(yc, wbr_ref[2], preferred_element_type=F32))
    y = jnp.dot(merged.astype(BF16), wo_ref[...], preferred_element_type=F32)
    gate = mod_ref[:, 2 * D_MODEL:3 * D_MODEL]
    o_ref[...] = _token_tile(i, MERGE_CTX_TILES, x_refs) + gate * y


def merge_branches(xs, mod3, ya, yb_ctx, yb_lat, yc_ctx, yc_lat, gates, wa, wb, wc, wo, layer):
    row = lambda i: (i, 0)
    hbm = pl.BlockSpec(memory_space=pl.ANY)
    x_specs = ([pl.BlockSpec((MERGE_TM, D_MODEL), row)] if len(xs) == 1
               else _split_specs((MERGE_TM, D_MODEL), MERGE_CTX_TILES))
    return pl.pallas_call(
        functools.partial(_merge_kernel, layer, len(xs)),
        grid=(N_TOK // MERGE_TM,),
        in_specs=x_specs + [
            pl.BlockSpec((None, 1, MOD_WIDTH), lambda i: (_cond_row(i, MERGE_TM), 0, 0)),
            pl.BlockSpec((MERGE_TM, SGU_WIDTH), row),
            *_split_specs((MERGE_TM, LRU_WIDTH), MERGE_CTX_TILES),
            *_split_specs((MERGE_TM, NA_WIDTH), MERGE_CTX_TILES),
            pl.BlockSpec((MERGE_TM, 3 * D_MODEL), row),
            hbm, hbm, hbm, hbm,
        ],
        out_specs=pl.BlockSpec((MERGE_TM, D_MODEL), row),
        out_shape=jax.ShapeDtypeStruct((N_TOK, D_MODEL), F32),
        scratch_shapes=[pltpu.VMEM((3, BRANCH_W, D_MODEL), BF16), pltpu.VMEM((D_MODEL, D_MODEL), BF16),
                        pltpu.VMEM((2, BRANCH_W, D_MODEL), F32), pltpu.SemaphoreType.DMA((2,))],
        compiler_params=_params(1),
        name="merge_branches",
    )(*xs, mod3, ya, yb_ctx, yb_lat, yc_ctx, yc_lat, gates, wa, wb, wc, wo)


RT_TM = 512
RT_ROWS = 32
N_BUCKETS = N_GROUPS
CNT_ROWS = 8
COMB_W = 128
HX_W = D_MODEL + COMB_W
EXP_TM = 512
EXP_TILES = N_TOK // EXP_TM + N_BUCKETS
HS_ROWS = EXP_TILES * EXP_TM


def _split_bf16(x):
    hi = x.astype(BF16)
    return hi, (x - hi.astype(F32)).astype(BF16)


def _router_kernel(x_ref, mod_ref, gain_ref, w_ref, b_ref, tri_ref, hx_ref, bucket_ref, rank_ref, cnt_out_ref,
                   cnt_ref, cpad_ref):
    @pl.when(pl.program_id(0) == 0)
    def _():
        cnt_ref[...] = jnp.zeros_like(cnt_ref)
        cpad_ref[...] = jnp.zeros_like(cpad_ref)

    m = mod_ref[...]
    shift, scale = m[:, 3 * D_MODEL:4 * D_MODEL], m[:, 4 * D_MODEL:5 * D_MODEL]
    h = _rms(x_ref[...], gain_ref[...]) * (1.0 + scale) + shift
    hx_ref[:, 0:D_MODEL] = h
    h_hi, h_lo = _split_bf16(h)
    w_hi, w_lo = _split_bf16(w_ref[...])
    dims = (((1,), (1,)), ((), ()))
    logits = (lax.dot_general(w_hi, h_hi, dims, preferred_element_type=F32)
              + lax.dot_general(w_hi, h_lo, dims, preferred_element_type=F32)
              + lax.dot_general(w_lo, h_hi, dims, preferred_element_type=F32)) + b_ref[...]
    gl = [logits[g:g + 1, :] for g in range(N_GROUPS)]
    gmax = functools.reduce(jnp.maximum, gl)
    gid = jnp.full(gmax.shape, N_GROUPS - 1, jnp.int32)
    for g in reversed(range(N_GROUPS - 1)):
        gid = jnp.where(gl[g] == gmax, g, gid)
    p_grp = 1.0 / functools.reduce(jnp.add, [jnp.exp(v - gmax) for v in gl])
    el = []
    for e in range(EXPERTS_PER_GROUP):
        v = logits[8 + e:9 + e, :]
        for g in range(1, N_GROUPS):
            row = 8 + g * EXPERTS_PER_GROUP + e
            v = jnp.where(gid == g, logits[row:row + 1, :], v)
        el.append(v)
    top1 = functools.reduce(jnp.maximum, el)
    idx1 = jnp.full(top1.shape, EXPERTS_PER_GROUP - 1, jnp.int32)
    for e in reversed(range(EXPERTS_PER_GROUP - 1)):
        idx1 = jnp.where(el[e] == top1, e, idx1)
    rest = [jnp.where(idx1 == e, -jnp.inf, el[e]) for e in range(EXPERTS_PER_GROUP)]
    top2 = functools.reduce(jnp.maximum, rest)
    idx2 = jnp.full(top1.shape, EXPERTS_PER_GROUP - 1, jnp.int32)
    for e in reversed(range(EXPERTS_PER_GROUP - 1)):
        idx2 = jnp.where(rest[e] == top2, e, idx2)
    e2 = jnp.exp(top2 - top1)
    w1 = p_grp / (1.0 + e2)
    w2 = p_grp * e2 / (1.0 + e2)
    for e in range(EXPERTS_PER_GROUP):
        cpad_ref[e:e + 1, :] = jnp.where(idx1 == e, w1, 0.0) + jnp.where(idx2 == e, w2, 0.0)
    hx_ref[:, D_MODEL:] = cpad_ref[...].T
    bucket_ref[...] = gid
    sub = lax.broadcasted_iota(jnp.int32, (CNT_ROWS, RT_TM), 0)
    onehot = jnp.where(sub == gid, 1.0, 0.0)
    seen = jnp.dot(onehot.astype(BF16), tri_ref[...], preferred_element_type=F32)
    cnt = cnt_ref[...]
    rank_ref[...] = jnp.sum(onehot * (seen - 1.0 + cnt[:, 0:1]), axis=0, keepdims=True).astype(jnp.int32)
    cnt = cnt + jnp.sum(onehot, axis=1, keepdims=True)
    cnt_ref[...] = cnt
    cnt_out_ref[...] = cnt.astype(jnp.int32)


def router(x, mod3, gain, w_rt, b_rt):
    tri = jnp.asarray(np.triu(np.ones((RT_TM, RT_TM), np.float32)), BF16)
    return pl.pallas_call(
        _router_kernel,
        grid=(N_TOK // RT_TM,),
        in_specs=[
            pl.BlockSpec((RT_TM, D_MODEL), lambda i: (i, 0)),
            pl.BlockSpec((None, 1, MOD_WIDTH), lambda i: (_cond_row(i, RT_TM), 0, 0)),
            pl.BlockSpec((1, D_MODEL), lambda i: (0, 0)),
            pl.BlockSpec((RT_ROWS, D_MODEL), lambda i: (0, 0)),
            pl.BlockSpec((RT_ROWS, 1), lambda i: (0, 0)),
            pl.BlockSpec((RT_TM, RT_TM), lambda i: (0, 0)),
        ],
        out_specs=[
            pl.BlockSpec((RT_TM, HX_W), lambda i: (i, 0)),
            pl.BlockSpec((1, RT_TM), lambda i: (0, i)),
            pl.BlockSpec((1, RT_TM), lambda i: (0, i)),
            pl.BlockSpec((CNT_ROWS, COMB_W), lambda i: (0, 0)),
        ],
        out_shape=[
            jax.ShapeDtypeStruct((N_TOK, HX_W), F32),
            jax.ShapeDtypeStruct((1, N_TOK), jnp.int32),
            jax.ShapeDtypeStruct((1, N_TOK), jnp.int32),
            jax.ShapeDtypeStruct((CNT_ROWS, COMB_W), jnp.int32),
        ],
        scratch_shapes=[pltpu.VMEM((CNT_ROWS, COMB_W), F32), pltpu.VMEM((COMB_W, RT_TM), F32)],
        compiler_params=_params(1),
        name="router",
    )(x, mod3, gain, w_rt, b_rt, tri)


def _router_weights(w_grp, b_grp, w_exp, b_exp):
    w = jnp.zeros((RT_ROWS, D_MODEL), F32).at[0:N_GROUPS].set(w_grp.T).at[8:8 + N_EXPERTS].set(w_exp.T)
    b = jnp.zeros((RT_ROWS, 1), F32).at[0:N_GROUPS, 0].set(b_grp).at[8:8 + N_EXPERTS, 0].set(b_exp)
    return w, b


DISP_TM = 512


ROW_GROUP = 64


def _for_each_row(n_rows, fn):
    def group(k, carry):
        g0 = pl.multiple_of(k * ROW_GROUP, ROW_GROUP)
        for u in range(ROW_GROUP):
            fn(g0, u)
        return carry

    lax.fori_loop(0, n_rows // ROW_GROUP, group, 0)


def _dispatch_kernel(pos_ref, start_ref, cnt_ref, nt_ref, hx_ref, hs_ref, zero_ref, sem, zsem):
    i = pl.program_id(0)
    base = i * DISP_TM

    def issue(g0, u):
        src = hx_ref.at[pl.ds(g0, ROW_GROUP), :]
        pltpu.make_async_copy(src.at[pl.ds(u, 1), :], hs_ref.at[pl.ds(pos_ref[base + g0 + u], 1), :], sem).start()

    _for_each_row(DISP_TM, issue)
    pltpu.make_async_copy(hx_ref, hs_ref.at[pl.ds(0, DISP_TM), :], sem).wait()

    @pl.when(i == pl.num_programs(0) - 1)
    def _():
        zero_ref[...] = jnp.zeros_like(zero_ref)

        def pad_copy(row):
            return pltpu.make_async_copy(zero_ref.at[pl.ds(0, 1), :], hs_ref.at[pl.ds(row, 1), :], zsem)

        def pad_issue(row, carry):
            pad_copy(row).start()
            return carry

        def pad_wait(row, carry):
            pad_copy(row).wait()
            return carry

        for g in range(N_BUCKETS):
            lo = start_ref[g] + cnt_ref[g]
            hi = start_ref[g] + ((cnt_ref[g] + EXP_TM - 1) // EXP_TM) * EXP_TM
            lax.fori_loop(lo, hi, pad_issue, 0)
            lax.fori_loop(lo, hi, pad_wait, 0)

        def tile_copy(t):
            return pltpu.make_async_copy(zero_ref, hs_ref.at[pl.ds(pl.multiple_of(t * EXP_TM, EXP_TM), EXP_TM), :],
                                         zsem)

        def tile_issue(t, carry):
            tile_copy(t).start()
            return carry

        def tile_wait(t, carry):
            tile_copy(t).wait()
            return carry

        lax.fori_loop(nt_ref[0], EXP_TILES, tile_issue, 0)
        lax.fori_loop(nt_ref[0], EXP_TILES, tile_wait, 0)


def dispatch(pos, starts, counts, n_tiles, hx):
    return pl.pallas_call(
        _dispatch_kernel,
        grid_spec=pltpu.PrefetchScalarGridSpec(
            num_scalar_prefetch=4,
            grid=(N_TOK // DISP_TM,),
            in_specs=[pl.BlockSpec((DISP_TM, HX_W), lambda i, *_: (i, 0))],
            out_specs=pl.BlockSpec(memory_space=pl.ANY),
            scratch_shapes=[pltpu.VMEM((EXP_TM, HX_W), F32), pltpu.SemaphoreType.DMA(()),
                            pltpu.SemaphoreType.DMA(())],
        ),
        out_shape=jax.ShapeDtypeStruct((HS_ROWS, HX_W), F32),
        compiler_params=_params(1),
        name="dispatch",
    )(pos, starts, counts, n_tiles, hx)


def _experts_kernel(layer, tg_ref, nt_ref, hs_ref, w1_hbm, w3_hbm, w2_hbm, ys_ref, w1_ref, w3_ref, w2_ref,
                    stage_in_ref, stage_out_ref, sem_ref):
    t = pl.program_id(0)
    group = tg_ref[t]
    new_group = jnp.logical_or(t == 0, group != tg_ref[jnp.maximum(t - 1, 0)])

    @pl.when(jnp.logical_and(t < nt_ref[0], new_group))
    def _():
        def store(dst, e):
            def st(v):
                dst[e] = v
            return st

        def chunks(src, dst):
            return [(src.at[layer, group * EXPERTS_PER_GROUP + e], store(dst, e))
                    for e in range(EXPERTS_PER_GROUP)]

        _load_cast(chunks(w1_hbm, w1_ref) + chunks(w3_hbm, w3_ref), stage_in_ref, sem_ref)
        _load_cast(chunks(w2_hbm, w2_ref), stage_out_ref, sem_ref)

    @pl.when(t < nt_ref[0])
    def _():
        h = hs_ref[:, 0:D_MODEL].astype(BF16)
        c = hs_ref[:, D_MODEL:]
        acc = None
        for e in range(EXPERTS_PER_GROUP):
            a = jnp.dot(h, w1_ref[e], preferred_element_type=F32)
            b = jnp.dot(h, w3_ref[e], preferred_element_type=F32)
            hid = (a * _sigmoid(a)) * b * c[:, e:e + 1]
            y = jnp.dot(hid.astype(BF16), w2_ref[e], preferred_element_type=F32)
            acc = y if acc is None else acc + y
        ys_ref[...] = acc

    @pl.when(t >= nt_ref[0])
    def _():
        ys_ref[...] = jnp.zeros_like(ys_ref)


def experts(tile_group, n_tiles, hs, w1, w3, w2, layer):
    hbm = pl.BlockSpec(memory_space=pl.ANY)
    return pl.pallas_call(
        functools.partial(_experts_kernel, layer),
        grid_spec=pltpu.PrefetchScalarGridSpec(
            num_scalar_prefetch=2,
            grid=(EXP_TILES,),
            in_specs=[pl.BlockSpec((EXP_TM, HX_W), lambda t, tg, nt: (t, 0)), hbm, hbm, hbm],
            out_specs=pl.BlockSpec((EXP_TM, D_MODEL), lambda t, tg, nt: (t, 0)),
            scratch_shapes=[
                pltpu.VMEM((EXPERTS_PER_GROUP, D_MODEL, D_EXPERT), BF16),
                pltpu.VMEM((EXPERTS_PER_GROUP, D_MODEL, D_EXPERT), BF16),
                pltpu.VMEM((EXPERTS_PER_GROUP, D_EXPERT, D_MODEL), BF16),
                pltpu.VMEM((2, D_MODEL, D_EXPERT), F32),
                pltpu.VMEM((2, D_EXPERT, D_MODEL), F32),
                pltpu.SemaphoreType.DMA((2,)),
            ],
        ),
        out_shape=jax.ShapeDtypeStruct((HS_ROWS, D_MODEL), F32),
        compiler_params=_params(1),
        name="experts",
    )(tile_group, n_tiles, hs, w1, w3, w2)


def _expert_tiles(counts):
    tiles = (counts + EXP_TM - 1) // EXP_TM
    ends = jnp.cumsum(tiles)
    t = jnp.arange(EXP_TILES, dtype=jnp.int32)
    group = jnp.minimum(jnp.sum(t[:, None] >= ends[None, :], axis=1), N_BUCKETS - 1).astype(jnp.int32)
    return ((ends - tiles) * EXP_TM).astype(jnp.int32), group, ends[-1:].astype(jnp.int32)


RES_TM = 512
RES_CTX_TILES = N_CTX // RES_TM


def _moe_residual_kernel(final, pos_ref, x_ref, mod_ref, gain_ref, ys_ref, *refs):
    out_refs, (ybuf_ref, sem) = refs[:-2], refs[-2:]
    i = pl.program_id(0)
    base = i * RES_TM

    def issue(g0, u):
        dst = ybuf_ref.at[pl.ds(g0, ROW_GROUP), :]
        pltpu.make_async_copy(ys_ref.at[pl.ds(pos_ref[base + g0 + u], 1), :], dst.at[pl.ds(u, 1), :], sem).start()

    _for_each_row(RES_TM, issue)
    pltpu.make_async_copy(ys_ref.at[pl.ds(0, RES_TM), :], ybuf_ref, sem).wait()
    x = x_ref[...] + mod_ref[:, 5 * D_MODEL:6 * D_MODEL] * ybuf_ref[...]
    if not final:
        out_refs[0][...] = x
        return
    y = _rms(x, gain_ref[...])

    @pl.when(i < RES_CTX_TILES)
    def _():
        out_refs[0][...] = y

    @pl.when(i >= RES_CTX_TILES)
    def _():
        out_refs[1][...] = y


def moe_residual(pos, x, mod3, gain, ys, final):
    if final:
        out_specs = _split_specs((RES_TM, D_MODEL), RES_CTX_TILES)
        out_shape = [jax.ShapeDtypeStruct((N_CTX, D_MODEL), F32), jax.ShapeDtypeStruct((N_LAT, D_MODEL), F32)]
    else:
        out_specs = pl.BlockSpec((RES_TM, D_MODEL), lambda i, *_: (i, 0))
        out_shape = jax.ShapeDtypeStruct((N_TOK, D_MODEL), F32)
    return pl.pallas_call(
        functools.partial(_moe_residual_kernel, final),
        grid_spec=pltpu.PrefetchScalarGridSpec(
            num_scalar_prefetch=1,
            grid=(N_TOK // RES_TM,),
            in_specs=[
                pl.BlockSpec((RES_TM, D_MODEL), lambda i, *_: (i, 0)),
                pl.BlockSpec((None, 1, MOD_WIDTH), lambda i, *_: (_cond_row(i, RES_TM), 0, 0)),
                pl.BlockSpec((1, D_MODEL), lambda i, *_: (0, 0)),
                pl.BlockSpec(memory_space=pl.ANY),
            ],
            out_specs=out_specs,
            scratch_shapes=[pltpu.VMEM((RES_TM, D_MODEL), F32), pltpu.SemaphoreType.DMA(())],
        ),
        out_shape=out_shape,
        compiler_params=_params(1),
        name="moe_residual_final" if final else "moe_residual",
    )(pos, x, mod3, gain, ys)


def kernel(x_prompt, x_sample, cache_k, cache_v, state_lru, c, c_ctx, w_mod, b_mod, norm_mix, norm_ffn, w_in, sgu_norm, sgu_w, sgu_b, lru_conv_w, lru_conv_b, lru_w_r, lru_b_r, lru_w_i, lru_b_i, lru_lambda, na_rpb, w_branch_sgu, w_branch_lru, w_branch_na, w_out, moe_w_group, moe_b_group, moe_w_expert, moe_b_expert, moe_w1, moe_w3, moe_w2, final_norm_gain):
    xs = (x_prompt.reshape(N_CTX, D_MODEL), x_sample.reshape(N_LAT, D_MODEL))
    cond = jnp.zeros((N_COND, D_MODEL), F32).at[0].set(c_ctx).at[1:1 + DEC_BATCH].set(c)
    mod = modulation(cond, w_mod, b_mod)
    zero_state = jnp.zeros((BATCH, 2, LRU_WIDTH), F32)
    cache_k = cache_k.reshape(DEC_BATCH, DEPTH, PAST_LEN, NA_WIDTH)
    cache_v = cache_v.reshape(DEC_BATCH, DEPTH, PAST_LEN, NA_WIDTH)
    final_gain = final_norm_gain.reshape(1, D_MODEL)
    ks, vs, ss = [], [], []
    for l in range(DEPTH):
        mod3 = mod[l].reshape(N_COND, 1, MOD_WIDTH)
        za, q, kb, vb, k_ctx, v_ctx, gates = in_projection(xs, mod3, norm_mix[l].reshape(1, D_MODEL), w_in, l)
        ya = spatial_gating(za, sgu_norm[l].reshape(1, SGU_WIDTH), sgu_w[l], sgu_b[l].T)
        w_lo, w_hi = _block_diag_gate_weights(lru_w_r[l], lru_w_i[l])
        lru_args = (lru_conv_w[l], lru_conv_b[l].reshape(1, LRU_WIDTH), w_lo, w_hi, lru_b_r[l], lru_b_i[l],
                    lru_lambda[l])
        yb_ctx, st_ctx = rglru(za, *lru_args, zero_state, SEQ, BATCH, 0)
        yb_lat, _ = rglru(za, *lru_args, state_lru[:, l], DEC_SEQ, DEC_BATCH, N_CTX)
        yc_ctx = context_attention(q, kb, vb)
        yc_lat = latent_attention(q, kb, vb, cache_k, cache_v, na_rpb[l].reshape(-1), l)
        x = merge_branches(xs, mod3, ya, yb_ctx, yb_lat, yc_ctx, yc_lat, gates,
                           w_branch_sgu, w_branch_lru, w_branch_na, w_out, l)
        w_rt, b_rt = _router_weights(moe_w_group[l], moe_b_group[l], moe_w_expert[l], moe_b_expert[l])
        hx, bucket, rank, counts = router(x, mod3, norm_ffn[l].reshape(1, D_MODEL), w_rt, b_rt)
        bucket, rank, counts = bucket.reshape(N_TOK), rank.reshape(N_TOK), counts[:N_BUCKETS, 0]
        starts, tile_group, n_tiles = _expert_tiles(counts)
        pos = rank + jnp.sum(jnp.where(bucket[:, None] == jnp.arange(N_BUCKETS), starts[None, :], 0), axis=1)
        hs = dispatch(pos, starts, counts, n_tiles, hx)
        ys = experts(tile_group, n_tiles, hs, moe_w1, moe_w3, moe_w2, l)
        if l < DEPTH - 1:
            xs = (moe_residual(pos, x, mod3, final_gain, ys, False),)
        else:
            y_ctx, y_lat = moe_residual(pos, x, mod3, final_gain, ys, True)
        ks.append(k_ctx.reshape(BATCH, SEQ, NA_HEADS, NA_HEAD_DIM))
        vs.append(v_ctx.reshape(BATCH, SEQ, NA_HEADS, NA_HEAD_DIM))
        ss.append(st_ctx)
    return (y_ctx.reshape(BATCH, SEQ, D_MODEL), y_lat.reshape(DEC_BATCH, DEC_SEQ, D_MODEL),
            jnp.stack(ks, axis=1), jnp.stack(vs, axis=1), jnp.stack(ss, axis=1))
```

```python
import functools

import jax
import jax.numpy as jnp
import numpy as np
from jax import lax
from jax.experimental import pallas as pl
from jax.experimental.pallas import tpu as pltpu

F32 = jnp.float32
BF16 = jnp.bfloat16

D_MODEL = 1024
BATCH = 16
SEQ = 256
DEPTH = 2
DEC_BATCH = 4
DEC_SEQ = 2048
PAST_LEN = 512
GRID_W = 64
CHUNK = 128
SGU_WIDTH = 512
SGU_GROUPS = 4
LRU_WIDTH = 512
LRU_BLOCKS = 8
CONV_WIDTH = 4
LRU_C = 8.0
NA_HEADS = 8
NA_HEAD_DIM = 64
NA_WIDTH = 512
WIN_ROWS = 8
WIN_COLS = 16
N_GROUPS = 4
EXPERTS_PER_GROUP = 4
N_EXPERTS = 16
D_EXPERT = 512
IN_WIDTH = 6656
EPS = 1e-6
NEG_INF = -1e30

N_CTX = BATCH * SEQ
N_LAT = DEC_BATCH * DEC_SEQ
N_TOK = N_CTX + N_LAT
N_COND = 8
MOD_WIDTH = 6 * D_MODEL
GRID_ROWS = DEC_SEQ // GRID_W

VMEM_LIMIT_BYTES = 56 * 1024 * 1024


def _params(n_axes):
    return pltpu.CompilerParams(dimension_semantics=("arbitrary",) * n_axes,
                                vmem_limit_bytes=VMEM_LIMIT_BYTES)


def _cond_row(tile, tile_rows):
    tok = tile * tile_rows
    return jnp.where(tok < N_CTX, 0, 1 + (tok - N_CTX) // DEC_SEQ)


def _rms(x, gain):
    return x * lax.rsqrt(jnp.mean(x * x, axis=-1, keepdims=True) + EPS) * gain


def _bdot(a, b):
    return jnp.dot(a.astype(BF16), b.astype(BF16), preferred_element_type=F32)


def _bdot_t(a, b):
    return lax.dot_general(a.astype(BF16), b.astype(BF16), (((1,), (1,)), ((), ())),
                           preferred_element_type=F32)


MOD_TN = 1536


def _mod_kernel(cond_ref, w_ref, b_ref, o_ref):
    c = cond_ref[...]
    s = c * jax.nn.sigmoid(c)
    o_ref[...] = _bdot(s, w_ref[...]) + b_ref[...]


def modulation(cond, w_mod, b_mod):
    return pl.pallas_call(
        _mod_kernel,
        grid=(DEPTH, MOD_WIDTH // MOD_TN),
        in_specs=[
            pl.BlockSpec((N_COND, D_MODEL), lambda l, j: (0, 0)),
            pl.BlockSpec((None, D_MODEL, MOD_TN), lambda l, j: (l, 0, j)),
            pl.BlockSpec((None, 1, MOD_TN), lambda l, j: (l, 0, j)),
        ],
        out_specs=pl.BlockSpec((None, N_COND, MOD_TN), lambda l, j: (l, 0, j)),
        out_shape=jax.ShapeDtypeStruct((DEPTH, N_COND, MOD_WIDTH), F32),
        compiler_params=_params(2),
        name="modulation",
    )(cond, w_mod, b_mod.reshape(DEPTH, 1, MOD_WIDTH))


IN_TM = 256
IN_CTX_TILES = N_CTX // IN_TM
ZA_WIDTH = 4 * 512
KV_OFF = ZA_WIDTH + NA_WIDTH
GATE_OFF = KV_OFF + 2 * NA_WIDTH
W_CHUNK = 512
Q_SCALE = NA_HEAD_DIM ** -0.5 * float(np.log2(np.e))


def _load_cast(chunks, stage_ref, sem_ref):
    def copy(j):
        return pltpu.make_async_copy(chunks[j][0], stage_ref.at[j % 2], sem_ref.at[j % 2])

    copy(0).start()
    for j in range(len(chunks)):
        if j + 1 < len(chunks):
            copy(j + 1).start()
        copy(j).wait()
        chunks[j][1](stage_ref[j % 2].astype(BF16))


def _token_tile(i, n_ctx_tiles, refs):
    if len(refs) == 1:
        return refs[0][...]
    return jnp.where(i < n_ctx_tiles, refs[0][...], refs[1][...])


def _split_specs(block, n_ctx_tiles):
    return [pl.BlockSpec(block, lambda i, *_: (jnp.minimum(i, n_ctx_tiles - 1), 0)),
            pl.BlockSpec(block, lambda i, *_: (jnp.maximum(i - n_ctx_tiles, 0), 0))]


def _inproj_kernel(layer, n_x, *refs):
    x_refs, (mod_ref, gain_ref, w_hbm) = refs[:n_x], refs[n_x:n_x + 3]
    za_ref, q_ref, kb_ref, vb_ref, kc_ref, vc_ref, g_ref, w_ref, stage_ref, sem_ref = refs[n_x + 3:]
    i = pl.program_id(0)

    @pl.when(i == 0)
    def _():
        def store(c):
            def st(v):
                w_ref[:, c * W_CHUNK:(c + 1) * W_CHUNK] = v
            return st

        _load_cast([(w_hbm.at[layer, :, pl.ds(c * W_CHUNK, W_CHUNK)], store(c)) for c in range(IN_WIDTH // W_CHUNK)],
                   stage_ref, sem_ref)

    m = mod_ref[...]
    shift, scale = m[:, 0:D_MODEL], m[:, D_MODEL:2 * D_MODEL]
    h = (_rms(_token_tile(i, IN_CTX_TILES, x_refs), gain_ref[...]) * (1.0 + scale) + shift).astype(BF16)
    za_ref[...] = jnp.dot(h, w_ref[:, 0:ZA_WIDTH], preferred_element_type=F32).astype(BF16)
    q_ref[...] = (jnp.dot(h, w_ref[:, ZA_WIDTH:KV_OFF], preferred_element_type=F32) * Q_SCALE).astype(BF16)
    kv = jnp.dot(h, w_ref[:, KV_OFF:GATE_OFF], preferred_element_type=F32)
    kb_ref[...] = kv[:, 0:NA_WIDTH].astype(BF16)
    vb_ref[...] = kv[:, NA_WIDTH:].astype(BF16)
    g_ref[...] = jnp.dot(h, w_ref[:, GATE_OFF:], preferred_element_type=F32).astype(BF16)

    @pl.when(i < IN_CTX_TILES)
    def _():
        kc_ref[...] = kv[:, 0:NA_WIDTH]
        vc_ref[...] = kv[:, NA_WIDTH:]


def in_projection(xs, mod3, gain, w_in, layer):
    row = lambda i: (i, 0)
    ctx_row = lambda i: (jnp.minimum(i, IN_CTX_TILES - 1), 0)
    x_specs = ([pl.BlockSpec((IN_TM, D_MODEL), row)] if len(xs) == 1
               else _split_specs((IN_TM, D_MODEL), IN_CTX_TILES))
    return pl.pallas_call(
        functools.partial(_inproj_kernel, layer, len(xs)),
        grid=(N_TOK // IN_TM,),
        in_specs=x_specs + [
            pl.BlockSpec((None, 1, MOD_WIDTH), lambda i: (_cond_row(i, IN_TM), 0, 0)),
            pl.BlockSpec((1, D_MODEL), lambda i: (0, 0)),
            pl.BlockSpec(memory_space=pl.ANY),
        ],
        out_specs=[
            pl.BlockSpec((IN_TM, ZA_WIDTH), row),
            pl.BlockSpec((IN_TM, NA_WIDTH), row),
            pl.BlockSpec((IN_TM, NA_WIDTH), row),
            pl.BlockSpec((IN_TM, NA_WIDTH), row),
            pl.BlockSpec((IN_TM, NA_WIDTH), ctx_row),
            pl.BlockSpec((IN_TM, NA_WIDTH), ctx_row),
            pl.BlockSpec((IN_TM, 3 * D_MODEL), row),
        ],
        out_shape=[
            jax.ShapeDtypeStruct((N_TOK, ZA_WIDTH), BF16),
            jax.ShapeDtypeStruct((N_TOK, NA_WIDTH), BF16),
            jax.ShapeDtypeStruct((N_TOK, NA_WIDTH), BF16),
            jax.ShapeDtypeStruct((N_TOK, NA_WIDTH), BF16),
            jax.ShapeDtypeStruct((N_CTX, NA_WIDTH), F32),
            jax.ShapeDtypeStruct((N_CTX, NA_WIDTH), F32),
            jax.ShapeDtypeStruct((N_TOK, 3 * D_MODEL), BF16),
        ],
        scratch_shapes=[pltpu.VMEM((D_MODEL, IN_WIDTH), BF16), pltpu.VMEM((2, D_MODEL, W_CHUNK), F32),
                        pltpu.SemaphoreType.DMA((2,))],
        compiler_params=_params(1),
        name="in_projection",
    )(*xs, mod3, gain, w_in)


SGU_TM = 512
SGU_GD = SGU_WIDTH // SGU_GROUPS


def _sgu_kernel(u_ref, v_ref, gain_ref, ws_ref, bs_ref, o_ref):
    gain = gain_ref[...]
    for c in range(SGU_TM // CHUNK):
        rows = slice(c * CHUNK, (c + 1) * CHUNK)
        u = jax.nn.gelu(u_ref[rows, :].astype(F32))
        v = _rms(jax.nn.gelu(v_ref[rows, :].astype(F32)), gain).astype(BF16)
        for g in range(SGU_GROUPS):
            cols = slice(g * SGU_GD, (g + 1) * SGU_GD)
            mixed = jnp.dot(ws_ref[g].astype(BF16), v[:, cols], preferred_element_type=F32) + bs_ref[:, g:g + 1]
            o_ref[rows, cols] = (u[:, cols] * mixed).astype(BF16)


def spatial_gating(za, gain, ws, bs_t):
    return pl.pallas_call(
        _sgu_kernel,
        grid=(N_TOK // SGU_TM,),
        in_specs=[
            pl.BlockSpec((SGU_TM, SGU_WIDTH), lambda i: (i, 0)),
            pl.BlockSpec((SGU_TM, SGU_WIDTH), lambda i: (i, 1)),
            pl.BlockSpec((1, SGU_WIDTH), lambda i: (0, 0)),
            pl.BlockSpec((SGU_GROUPS, CHUNK, CHUNK), lambda i: (0, 0, 0)),
            pl.BlockSpec((CHUNK, SGU_GROUPS), lambda i: (0, 0)),
        ],
        out_specs=pl.BlockSpec((SGU_TM, SGU_WIDTH), lambda i: (i, 0)),
        out_shape=jax.ShapeDtypeStruct((N_TOK, SGU_WIDTH), BF16),
        compiler_params=_params(1),
        name="spatial_gating",
    )(za, za, gain, ws, bs_t)


LRU_TC = 256
LRU_HALF = 256
SUB = 8
HALO = 8
LANES = 128
LRU_SLABS = LRU_WIDTH // LANES


def _sigmoid(x):
    return 0.5 * jnp.tanh(0.5 * x) + 0.5


def _lru_kernel(seq_len, xr_ref, gr_ref, cw_ref, cb_ref, wlo_ref, whi_ref, br_ref, bi_ref,
                lam_ref, h0_ref, y_ref, st_ref, xp_ref, a_ref, b_ref):
    seg_len = seq_len // SUB
    pitch = seg_len + SUB
    segs_per_chunk = max(LRU_TC // seg_len, 1)
    n_chunks = seq_len // LRU_TC
    zeros = jnp.zeros((HALO, LRU_WIDTH), F32)
    xp_ref[0:HALO, :] = zeros
    xp_ref[seq_len + HALO:seq_len + 2 * HALO, :] = zeros

    def copy_in(c, carry):
        r0 = pl.multiple_of(c * LRU_TC, LRU_TC)
        xp_ref[pl.ds(r0 + HALO, LRU_TC), :] = xr_ref[pl.ds(r0, LRU_TC), :].astype(F32)
        return carry

    lax.fori_loop(0, n_chunks, copy_in, 0)

    def chunk_rows(c, seg):
        n = min(seg_len, LRU_TC)
        start = pl.multiple_of((c * segs_per_chunk + seg) * pitch, SUB)
        return pl.ds(start, n), slice(seg * n, (seg + 1) * n)

    cw = cw_ref[...]
    cb = cb_ref[...]
    win = LRU_TC + 2 * HALO
    neg_lam = -lam_ref[...]
    softplus = jnp.maximum(neg_lam, 0.0) + jnp.log1p(jnp.exp(-jnp.abs(neg_lam)))
    decay = (0.5 * LRU_C) * softplus
    half_br = 0.5 * br_ref[...]
    half_bi = 0.5 * bi_ref[...]

    def gates(c, carry):
        r0 = pl.multiple_of(c * LRU_TC, LRU_TC)
        w = xp_ref[pl.ds(r0, win), :]
        xc = (cw[0:1, :] * pltpu.roll(w, 1, 0)[HALO:HALO + LRU_TC]
              + cw[1:2, :] * w[HALO:HALO + LRU_TC]
              + cw[2:3, :] * pltpu.roll(w, win - 1, 0)[HALO:HALO + LRU_TC]
              + cw[3:4, :] * pltpu.roll(w, win - 2, 0)[HALO:HALO + LRU_TC]) + cb
        xb = xc.astype(BF16)
        for half, w_ref in enumerate((wlo_ref, whi_ref)):
            cols = slice(half * LRU_HALF, (half + 1) * LRU_HALF)
            pre = jnp.dot(xb[:, cols], w_ref[...], preferred_element_type=F32)
            half_x = 0.5 * xc[:, cols]
            for d in range(2):
                tr = jnp.tanh(0.5 * pre[:, (2 * d) * LRU_HALF:(2 * d + 1) * LRU_HALF] + half_br[d:d + 1, cols])
                ti = jnp.tanh(0.5 * pre[:, (2 * d + 1) * LRU_HALF:(2 * d + 2) * LRU_HALF] + half_bi[d:d + 1, cols])
                neg_log_a = decay[d:d + 1, cols] * tr + decay[d:d + 1, cols]
                a = jnp.exp2(neg_log_a * -LOG2E)
                z = jnp.tanh(neg_log_a) * (a * a + 1.0)
                b = jnp.where(z > 0.0, z * lax.rsqrt(z), 0.0) * (half_x * ti + half_x)
                for k in range(LRU_HALF // LANES):
                    slab = half * (LRU_HALF // LANES) + k
                    lanes = slice(k * LANES, (k + 1) * LANES)
                    for seg in range(segs_per_chunk):
                        dst, src = chunk_rows(c, seg)
                        a_ref[d, slab, dst, :] = a[src, lanes]
                        b_ref[d, slab, dst, :] = b[src, lanes]
        return carry

    lax.fori_loop(0, n_chunks, gates, 0)

    def step_rows(d, j):
        return pl.ds(j if d == 0 else seg_len - 1 - j, SUB, stride=pitch)

    def reduce_step(j, carry):
        out = []
        for d in range(2):
            for slab in range(LRU_SLABS):
                big_a, big_b = carry[d * LRU_SLABS + slab]
                a = a_ref[d, slab, step_rows(d, j), :]
                b = b_ref[d, slab, step_rows(d, j), :]
                out.append((a * big_a, a * big_b + b))
        return tuple(out)

    ident = (jnp.ones((SUB, LANES), F32), jnp.zeros((SUB, LANES), F32))
    totals = lax.fori_loop(0, seg_len, reduce_step, (ident,) * (2 * LRU_SLABS), unroll=2)

    sub = lax.broadcasted_iota(jnp.int32, (SUB, LANES), 0)
    entering = []
    for d in range(2):
        for slab in range(LRU_SLABS):
            big_a, big_b = totals[d * LRU_SLABS + slab]
            h = h0_ref[d:d + 1, slab * LANES:(slab + 1) * LANES]
            rows = jnp.zeros((SUB, LANES), F32)
            for s in (range(SUB) if d == 0 else reversed(range(SUB))):
                rows = jnp.where(sub == s, h, rows)
                h = big_a[s:s + 1, :] * h + big_b[s:s + 1, :]
            entering.append(rows)
            st_ref[d:d + 1, slab * LANES:(slab + 1) * LANES] = h

    def apply_step(j, carry):
        out = []
        for d in range(2):
            for slab in range(LRU_SLABS):
                h = (a_ref[d, slab, step_rows(d, j), :] * carry[d * LRU_SLABS + slab]
                     + b_ref[d, slab, step_rows(d, j), :])
                b_ref[d, slab, step_rows(d, j), :] = h
                out.append(h)
        return tuple(out)

    lax.fori_loop(0, seg_len, apply_step, tuple(entering), unroll=2)

    def merge(c, carry):
        r0 = pl.multiple_of(c * LRU_TC, LRU_TC)
        gate = jax.nn.gelu(gr_ref[pl.ds(r0, LRU_TC), :].astype(F32))
        for slab in range(LRU_SLABS):
            lanes = slice(slab * LANES, (slab + 1) * LANES)
            for seg in range(segs_per_chunk):
                src, dst = chunk_rows(c, seg)
                h = b_ref[0, slab, src, :] + b_ref[1, slab, src, :]
                y_ref[pl.ds(r0 + dst.start, dst.stop - dst.start), lanes] = (h * gate[dst, lanes]).astype(BF16)
        return carry

    lax.fori_loop(0, n_chunks, merge, 0)


def rglru(za, conv_w, conv_b, w_lo, w_hi, b_r, b_i, lam, h0, seq_len, n_seq, tok_off):
    blk0 = tok_off // seq_len
    full = lambda shape: pl.BlockSpec(shape, lambda i: (0,) * len(shape))
    return pl.pallas_call(
        functools.partial(_lru_kernel, seq_len),
        grid=(n_seq,),
        in_specs=[
            pl.BlockSpec((seq_len, LRU_WIDTH), lambda i: (i + blk0, 2)),
            pl.BlockSpec((seq_len, LRU_WIDTH), lambda i: (i + blk0, 3)),
            full((CONV_WIDTH, LRU_WIDTH)),
            full((1, LRU_WIDTH)),
            full((LRU_HALF, 4 * LRU_HALF)),
            full((LRU_HALF, 4 * LRU_HALF)),
            full((2, LRU_WIDTH)),
            full((2, LRU_WIDTH)),
            full((2, LRU_WIDTH)),
            pl.BlockSpec((None, 2, LRU_WIDTH), lambda i: (i, 0, 0)),
        ],
        out_specs=[
            pl.BlockSpec((seq_len, LRU_WIDTH), lambda i: (i, 0)),
            pl.BlockSpec((None, 2, LRU_WIDTH), lambda i: (i, 0, 0)),
        ],
        out_shape=[
            jax.ShapeDtypeStruct((n_seq * seq_len, LRU_WIDTH), BF16),
            jax.ShapeDtypeStruct((n_seq, 2, LRU_WIDTH), F32),
        ],
        scratch_shapes=[
            pltpu.VMEM((seq_len + 2 * HALO, LRU_WIDTH), F32),
            pltpu.VMEM((2, LRU_SLABS, seq_len + SUB * SUB, LANES), F32),
            pltpu.VMEM((2, LRU_SLABS, seq_len + SUB * SUB, LANES), F32),
        ],
        compiler_params=_params(1),
        name=f"rglru_{seq_len}",
    )(za, za, conv_w, conv_b, w_lo, w_hi, b_r, b_i, lam, h0)


def _block_diag_gate_weights(w_r, w_i):
    per_half = LRU_BLOCKS // 2
    eye = jnp.eye(per_half, dtype=w_r.dtype)

    def bd(w):
        return jnp.einsum("hij,hg->higj", w, eye).reshape(LRU_HALF, LRU_HALF)

    tiles = []
    for half in range(2):
        sl = slice(half * per_half, (half + 1) * per_half)
        tiles.append(jnp.concatenate([bd(w_r[0, sl]), bd(w_i[0, sl]), bd(w_r[1, sl]), bd(w_i[1, sl])], axis=1))
    return tiles[0].astype(BF16), tiles[1].astype(BF16)


PAIR_W = 2 * NA_HEAD_DIM
LOG2E = float(np.log2(np.e))


def _head_lanes(shape, head):
    lane = lax.broadcasted_iota(jnp.int32, shape, 1)
    return (lane < NA_HEAD_DIM) if head == 0 else (lane >= NA_HEAD_DIM)


def _one_head(q_pair, head):
    return jnp.where(_head_lanes(q_pair.shape, head), q_pair.astype(F32), 0.0).astype(BF16)


def _join_heads(o0, o1):
    return jnp.where(_head_lanes(o0.shape, 0), o0, o1)


def _ctx_attn_kernel(q_ref, k_ref, v_ref, o_ref):
    for j in range(NA_HEADS // 2):
        cols = slice(j * PAIR_W, (j + 1) * PAIR_W)
        outs = []
        for head in range(2):
            s = _bdot_t(_one_head(q_ref[:, cols], head), k_ref[:, cols])
            p = jnp.exp2(s - jnp.max(s, axis=-1, keepdims=True))
            denom = jnp.sum(p, axis=-1, keepdims=True)
            outs.append(jnp.dot(p.astype(BF16), v_ref[:, cols], preferred_element_type=F32) / denom)
        o_ref[:, cols] = _join_heads(*outs).astype(BF16)


def context_attention(q, kb, vb):
    spec = pl.BlockSpec((SEQ, NA_WIDTH), lambda i: (i, 0))
    return pl.pallas_call(
        _ctx_attn_kernel,
        grid=(BATCH,),
        in_specs=[spec, spec, spec],
        out_specs=spec,
        out_shape=jax.ShapeDtypeStruct((N_CTX, NA_WIDTH), BF16),
        compiler_params=_params(1),
        name="context_attention",
    )(q, kb, vb)


NA_QROWS = 4
NA_TQ = NA_QROWS * GRID_W
NA_KROWS = NA_QROWS + WIN_ROWS
NA_TK = NA_KROWS * GRID_W
NA_QBLOCKS = GRID_ROWS // NA_QROWS


N_DROW = 2 * WIN_ROWS - 1
N_DCOL = 2 * WIN_COLS - 1
NA_BLOCK_KINDS = (0, 1, NA_QBLOCKS - 1)


def _first_key_row(qb):
    return (np.clip if isinstance(qb, int) else jnp.clip)(qb * NA_QROWS - WIN_ROWS // 2, 0, GRID_ROWS - NA_KROWS)


def _build_bias(rpb_ref, table_ref, bias_ref):
    qc = lax.broadcasted_iota(jnp.int32, (GRID_W, GRID_W), 0)
    kc = lax.broadcasted_iota(jnp.int32, (GRID_W, GRID_W), 1)
    col0 = jnp.clip(qc - WIN_COLS // 2, 0, GRID_W - WIN_COLS)
    col_ok = jnp.logical_and(kc >= col0, kc < col0 + WIN_COLS)
    d_col = jnp.clip(kc - qc, 1 - WIN_COLS, WIN_COLS - 1) + WIN_COLS - 1
    neg = jnp.full((GRID_W, GRID_W), NEG_INF, F32)

    def table_entry(idx, carry):
        t = neg
        for j in range(N_DCOL):
            t = jnp.where(d_col == j, rpb_ref[idx * N_DCOL + j], t)
        table_ref[idx] = jnp.where(col_ok, t * LOG2E, neg)
        return carry

    lax.fori_loop(0, NA_HEADS * N_DROW, table_entry, 0)

    def head_blocks(h, carry):
        for kind, qb in enumerate(NA_BLOCK_KINDS):
            for i in range(NA_QROWS):
                qrow = qb * NA_QROWS + i
                win0 = int(np.clip(qrow - WIN_ROWS // 2, 0, GRID_ROWS - WIN_ROWS))
                for kr in range(NA_KROWS):
                    krow = int(_first_key_row(qb)) + kr
                    inside = win0 <= krow < win0 + WIN_ROWS
                    blk = table_ref[h * N_DROW + (krow - qrow + WIN_ROWS - 1)] if inside else neg
                    bias_ref[kind, h, i * GRID_W:(i + 1) * GRID_W, kr * GRID_W:(kr + 1) * GRID_W] = blk
        return carry

    lax.fori_loop(0, NA_HEADS, head_blocks, 0)


def _lat_attn_kernel(rpb_ref, q_ref, k_ref, v_ref, ck_ref, cv_ref, o_ref, table_ref, bias_ref):
    qb = pl.program_id(1)

    @pl.when(jnp.logical_and(pl.program_id(0) == 0, qb == 0))
    def _():
        _build_bias(rpb_ref, table_ref, bias_ref)

    kind = jnp.where(qb == 0, 0, jnp.where(qb == NA_QBLOCKS - 1, 2, 1))
    k0 = pl.multiple_of(_first_key_row(qb) * GRID_W, GRID_W)
    for j in range(NA_HEADS // 2):
        cols = slice(j * PAIR_W, (j + 1) * PAIR_W)
        k_loc = k_ref[pl.ds(k0, NA_TK), cols]
        v_loc = v_ref[pl.ds(k0, NA_TK), cols]
        k_ctx = ck_ref[:, cols].astype(BF16)
        v_ctx = cv_ref[:, cols].astype(BF16)
        outs = []
        for head in range(2):
            qh = _one_head(q_ref[:, cols], head)
            s_loc = _bdot_t(qh, k_loc) + bias_ref[kind, 2 * j + head]
            s_ctx = _bdot_t(qh, k_ctx)
            m = jnp.maximum(jnp.max(s_loc, axis=-1, keepdims=True), jnp.max(s_ctx, axis=-1, keepdims=True))
            p_loc = jnp.exp2(s_loc - m)
            p_ctx = jnp.exp2(s_ctx - m)
            denom = jnp.sum(p_loc, axis=-1, keepdims=True) + jnp.sum(p_ctx, axis=-1, keepdims=True)
            o = (jnp.dot(p_loc.astype(BF16), v_loc, preferred_element_type=F32)
                 + jnp.dot(p_ctx.astype(BF16), v_ctx, preferred_element_type=F32))
            outs.append(o / denom)
        o_ref[:, cols] = _join_heads(*outs).astype(BF16)


def latent_attention(q, kb, vb, cache_k, cache_v, rpb, layer):
    q_blk0 = N_CTX // NA_TQ
    s_blk0 = N_CTX // DEC_SEQ
    cache_spec = pl.BlockSpec((None, None, PAST_LEN, NA_WIDTH), lambda b, m: (b, layer, 0, 0))
    return pl.pallas_call(
        _lat_attn_kernel,
        grid=(DEC_BATCH, NA_QBLOCKS),
        in_specs=[
            pl.BlockSpec(memory_space=pltpu.SMEM),
            pl.BlockSpec((NA_TQ, NA_WIDTH), lambda b, m: (q_blk0 + b * NA_QBLOCKS + m, 0)),
            pl.BlockSpec((DEC_SEQ, NA_WIDTH), lambda b, m: (s_blk0 + b, 0)),
            pl.BlockSpec((DEC_SEQ, NA_WIDTH), lambda b, m: (s_blk0 + b, 0)),
            cache_spec,
            cache_spec,
        ],
        out_specs=pl.BlockSpec((NA_TQ, NA_WIDTH), lambda b, m: (b * NA_QBLOCKS + m, 0)),
        out_shape=jax.ShapeDtypeStruct((N_LAT, NA_WIDTH), BF16),
        scratch_shapes=[pltpu.VMEM((NA_HEADS * N_DROW, GRID_W, GRID_W), F32),
                        pltpu.VMEM((len(NA_BLOCK_KINDS), NA_HEADS, NA_TQ, NA_TK), F32)],
        compiler_params=_params(2),
        name="latent_attention",
    )(rpb, q, kb, vb, cache_k, cache_v)


MERGE_TM = 512


MERGE_CTX_TILES = N_CTX // MERGE_TM
BRANCH_W = 512


def _merge_kernel(layer, n_x, *refs):
    x_refs, rest = refs[:n_x], refs[n_x:]
    (mod_ref, ya_ref, yb_ctx_ref, yb_lat_ref, yc_ctx_ref, yc_lat_ref, g_ref, wa_hbm, wb_hbm, wc_hbm, wo_hbm,
     o_ref, wbr_ref, wo_ref, stage_ref, sem_ref) = rest
    i = pl.program_id(0)

    @pl.when(i == 0)
    def _():
        def store_branch(k):
            def st(v):
                wbr_ref[k] = v
            return st

        def store_out(k):
            def st(v):
                wo_ref[k * BRANCH_W:(k + 1) * BRANCH_W, :] = v
            return st

        chunks = [(w.at[layer], store_branch(k)) for k, w in enumerate((wa_hbm, wb_hbm, wc_hbm))]
        chunks += [(wo_hbm.at[layer, pl.ds(k * BRANCH_W, BRANCH_W), :], store_out(k))
                   for k in range(D_MODEL // BRANCH_W)]
        _load_cast(chunks, stage_ref, sem_ref)

    g = _sigmoid(g_ref[...].astype(F32))
    yb = _token_tile(i, MERGE_CTX_TILES, (yb_ctx_ref, yb_lat_ref))
    yc = _token_tile(i, MERGE_CTX_TILES, (yc_ctx_ref, yc_lat_ref))
    merged = (g[:, 0:D_MODEL] * jnp.dot(ya_ref[...], wbr_ref[0], preferred_element_type=F32)
              + g[:, D_MODEL:2 * D_MODEL] * jnp.dot(yb, wbr_ref[1], preferred_element_type=F32)
              + g[:, 2 * D_MODEL:] * jnp.dot(yc, wbr_ref[2], preferred_element_type=F32))
    y = jnp.dot(merged.astype(BF16), wo_ref[...], preferred_element_type=F32)
    gate = mod_ref[:, 2 * D_MODEL:3 * D_MODEL]
    o_ref[...] = _token_tile(i, MERGE_CTX_TILES, x_refs) + gate * y


def merge_branches(xs, mod3, ya, yb_ctx, yb_lat, yc_ctx, yc_lat, gates, wa, wb, wc, wo, layer):
    row = lambda i: (i, 0)
    hbm = pl.BlockSpec(memory_space=pl.ANY)
    x_specs = ([pl.BlockSpec((MERGE_TM, D_MODEL), row)] if len(xs) == 1
               else _split_specs((MERGE_TM, D_MODEL), MERGE_CTX_TILES))
    return pl.pallas_call(
        functools.partial(_merge_kernel, layer, len(xs)),
        grid=(N_TOK // MERGE_TM,),
        in_specs=x_specs + [
            pl.BlockSpec((None, 1, MOD_WIDTH), lambda i: (_cond_row(i, MERGE_TM), 0, 0)),
            pl.BlockSpec((MERGE_TM, SGU_WIDTH), row),
            *_split_specs((MERGE_TM, LRU_WIDTH), MERGE_CTX_TILES),
            *_split_specs((MERGE_TM, NA_WIDTH), MERGE_CTX_TILES),
            pl.BlockSpec((MERGE_TM, 3 * D_MODEL), row),
            hbm, hbm, hbm, hbm,
        ],
        out_specs=pl.BlockSpec((MERGE_TM, D_MODEL), row),
        out_shape=jax.ShapeDtypeStruct((N_TOK, D_MODEL), F32),
        scratch_shapes=[pltpu.VMEM((3, BRANCH_W, D_MODEL), BF16), pltpu.VMEM((D_MODEL, D_MODEL), BF16),
                        pltpu.VMEM((2, BRANCH_W, D_MODEL), F32), pltpu.SemaphoreType.DMA((2,))],
        compiler_params=_params(1),
        name="merge_branches",
    )(*xs, mod3, ya, yb_ctx, yb_lat, yc_ctx, yc_lat, gates, wa, wb, wc, wo)


RT_TM = 512
RT_ROWS = 32
EXPERT_PAIRS = ((0, 1), (0, 2), (0, 3), (1, 2), (1, 3), (2, 3))
N_PAIRS = len(EXPERT_PAIRS)
N_BUCKETS = N_GROUPS * N_PAIRS
CNT_ROWS = 32
COMB_W = 128
HX_W = D_MODEL + COMB_W
EXP_TM = 256
EXP_TILES = N_TOK // EXP_TM + N_BUCKETS
HS_ROWS = EXP_TILES * EXP_TM


def _split_bf16(x):
    hi = x.astype(BF16)
    return hi, (x - hi.astype(F32)).astype(BF16)


def _router_kernel(x_ref, mod_ref, gain_ref, w_ref, b_ref, tri_ref, hx_ref, bucket_ref, rank_ref, cnt_out_ref,
                   cnt_ref, cpad_ref):
    @pl.when(pl.program_id(0) == 0)
    def _():
        cnt_ref[...] = jnp.zeros_like(cnt_ref)
        cpad_ref[...] = jnp.zeros_like(cpad_ref)

    m = mod_ref[...]
    shift, scale = m[:, 3 * D_MODEL:4 * D_MODEL], m[:, 4 * D_MODEL:5 * D_MODEL]
    h = _rms(x_ref[...], gain_ref[...]) * (1.0 + scale) + shift
    hx_ref[:, 0:D_MODEL] = h
    h_hi, h_lo = _split_bf16(h)
    w_hi, w_lo = _split_bf16(w_ref[...])
    dims = (((1,), (1,)), ((), ()))
    logits = (lax.dot_general(w_hi, h_hi, dims, preferred_element_type=F32)
              + lax.dot_general(w_hi, h_lo, dims, preferred_element_type=F32)
              + lax.dot_general(w_lo, h_hi, dims, preferred_element_type=F32)) + b_ref[...]
    gl = [logits[g:g + 1, :] for g in range(N_GROUPS)]
    gmax = functools.reduce(jnp.maximum, gl)
    gid = jnp.full(gmax.shape, N_GROUPS - 1, jnp.int32)
    for g in reversed(range(N_GROUPS - 1)):
        gid = jnp.where(gl[g] == gmax, g, gid)
    p_grp = 1.0 / functools.reduce(jnp.add, [jnp.exp(v - gmax) for v in gl])
    el = []
    for e in range(EXPERTS_PER_GROUP):
        v = logits[8 + e:9 + e, :]
        for g in range(1, N_GROUPS):
            row = 8 + g * EXPERTS_PER_GROUP + e
            v = jnp.where(gid == g, logits[row:row + 1, :], v)
        el.append(v)
    top1 = functools.reduce(jnp.maximum, el)
    idx1 = jnp.full(top1.shape, EXPERTS_PER_GROUP - 1, jnp.int32)
    for e in reversed(range(EXPERTS_PER_GROUP - 1)):
        idx1 = jnp.where(el[e] == top1, e, idx1)
    rest = [jnp.where(idx1 == e, -jnp.inf, el[e]) for e in range(EXPERTS_PER_GROUP)]
    top2 = functools.reduce(jnp.maximum, rest)
    idx2 = jnp.full(top1.shape, EXPERTS_PER_GROUP - 1, jnp.int32)
    for e in reversed(range(EXPERTS_PER_GROUP - 1)):
        idx2 = jnp.where(rest[e] == top2, e, idx2)
    e2 = jnp.exp(top2 - top1)
    w1 = p_grp / (1.0 + e2)
    w2 = p_grp * e2 / (1.0 + e2)
    for e in range(EXPERTS_PER_GROUP):
        cpad_ref[e:e + 1, :] = jnp.where(idx1 == e, w1, 0.0) + jnp.where(idx2 == e, w2, 0.0)
    hx_ref[:, D_MODEL:] = cpad_ref[...].T
    lo = jnp.minimum(idx1, idx2)
    hi = jnp.maximum(idx1, idx2)
    pair = jnp.where(lo == 0, 0, jnp.where(lo == 1, 3, 5)) + (hi - lo - 1)
    bucket = gid * N_PAIRS + pair
    bucket_ref[...] = bucket
    sub = lax.broadcasted_iota(jnp.int32, (CNT_ROWS, RT_TM), 0)
    onehot = jnp.where(sub == bucket, 1.0, 0.0)
    seen = jnp.dot(onehot.astype(BF16), tri_ref[...], preferred_element_type=F32)
    cnt = cnt_ref[...]
    rank_ref[...] = jnp.sum(onehot * (seen - 1.0 + cnt[:, 0:1]), axis=0, keepdims=True).astype(jnp.int32)
    cnt = cnt + jnp.sum(onehot, axis=1, keepdims=True)
    cnt_ref[...] = cnt
    cnt_out_ref[...] = cnt.astype(jnp.int32)


def router(x, mod3, gain, w_rt, b_rt):
    tri = jnp.asarray(np.triu(np.ones((RT_TM, RT_TM), np.float32)), BF16)
    return pl.pallas_call(
        _router_kernel,
        grid=(N_TOK // RT_TM,),
        in_specs=[
            pl.BlockSpec((RT_TM, D_MODEL), lambda i: (i, 0)),
            pl.BlockSpec((None, 1, MOD_WIDTH), lambda i: (_cond_row(i, RT_TM), 0, 0)),
            pl.BlockSpec((1, D_MODEL), lambda i: (0, 0)),
            pl.BlockSpec((RT_ROWS, D_MODEL), lambda i: (0, 0)),
            pl.BlockSpec((RT_ROWS, 1), lambda i: (0, 0)),
            pl.BlockSpec((RT_TM, RT_TM), lambda i: (0, 0)),
        ],
        out_specs=[
            pl.BlockSpec((RT_TM, HX_W), lambda i: (i, 0)),
            pl.BlockSpec((1, RT_TM), lambda i: (0, i)),
            pl.BlockSpec((1, RT_TM), lambda i: (0, i)),
            pl.BlockSpec((CNT_ROWS, COMB_W), lambda i: (0, 0)),
        ],
        out_shape=[
            jax.ShapeDtypeStruct((N_TOK, HX_W), F32),
            jax.ShapeDtypeStruct((1, N_TOK), jnp.int32),
            jax.ShapeDtypeStruct((1, N_TOK), jnp.int32),
            jax.ShapeDtypeStruct((CNT_ROWS, COMB_W), jnp.int32),
        ],
        scratch_shapes=[pltpu.VMEM((CNT_ROWS, COMB_W), F32), pltpu.VMEM((COMB_W, RT_TM), F32)],
        compiler_params=_params(1),
        name="router",
    )(x, mod3, gain, w_rt, b_rt, tri)


def _router_weights(w_grp, b_grp, w_exp, b_exp):
    w = jnp.zeros((RT_ROWS, D_MODEL), F32).at[0:N_GROUPS].set(w_grp.T).at[8:8 + N_EXPERTS].set(w_exp.T)
    b = jnp.zeros((RT_ROWS, 1), F32).at[0:N_GROUPS, 0].set(b_grp).at[8:8 + N_EXPERTS, 0].set(b_exp)
    return w, b


DISP_TM = 512


ROW_GROUP = 64


def _for_each_row(n_rows, fn):
    def group(k, carry):
        g0 = pl.multiple_of(k * ROW_GROUP, ROW_GROUP)
        for u in range(ROW_GROUP):
            fn(g0, u)
        return carry

    lax.fori_loop(0, n_rows // ROW_GROUP, group, 0)


def _dispatch_kernel(pos_ref, start_ref, cnt_ref, nt_ref, hx_ref, hs_ref, zero_ref, sem, zsem):
    i = pl.program_id(0)
    base = i * DISP_TM

    def issue(g0, u):
        src = hx_ref.at[pl.ds(g0, ROW_GROUP), :]
        pltpu.make_async_copy(src.at[pl.ds(u, 1), :], hs_ref.at[pl.ds(pos_ref[base + g0 + u], 1), :], sem).start()

    _for_each_row(DISP_TM, issue)
    pltpu.make_async_copy(hx_ref, hs_ref.at[pl.ds(0, DISP_TM), :], sem).wait()

    @pl.when(i == pl.num_programs(0) - 1)
    def _():
        zero_ref[...] = jnp.zeros_like(zero_ref)

        def pad_copy(row):
            return pltpu.make_async_copy(zero_ref.at[pl.ds(0, 1), :], hs_ref.at[pl.ds(row, 1), :], zsem)

        def pad_issue(row, carry):
            pad_copy(row).start()
            return carry

        def pad_wait(row, carry):
            pad_copy(row).wait()
            return carry

        for g in range(N_BUCKETS):
            lo = start_ref[g] + cnt_ref[g]
            hi = start_ref[g] + ((cnt_ref[g] + EXP_TM - 1) // EXP_TM) * EXP_TM
            lax.fori_loop(lo, hi, pad_issue, 0)
            lax.fori_loop(lo, hi, pad_wait, 0)

        def tile_copy(t):
            return pltpu.make_async_copy(zero_ref, hs_ref.at[pl.ds(pl.multiple_of(t * EXP_TM, EXP_TM), EXP_TM), :],
                                         zsem)

        def tile_issue(t, carry):
            tile_copy(t).start()
            return carry

        def tile_wait(t, carry):
            tile_copy(t).wait()
            return carry

        lax.fori_loop(nt_ref[0], EXP_TILES, tile_issue, 0)
        lax.fori_loop(nt_ref[0], EXP_TILES, tile_wait, 0)


def dispatch(pos, starts, counts, n_tiles, hx):
    return pl.pallas_call(
        _dispatch_kernel,
        grid_spec=pltpu.PrefetchScalarGridSpec(
            num_scalar_prefetch=4,
            grid=(N_TOK // DISP_TM,),
            in_specs=[pl.BlockSpec((DISP_TM, HX_W), lambda i, *_: (i, 0))],
            out_specs=pl.BlockSpec(memory_space=pl.ANY),
            scratch_shapes=[pltpu.VMEM((EXP_TM, HX_W), F32), pltpu.SemaphoreType.DMA(()),
                            pltpu.SemaphoreType.DMA(())],
        ),
        out_shape=jax.ShapeDtypeStruct((HS_ROWS, HX_W), F32),
        compiler_params=_params(1),
        name="dispatch",
    )(pos, starts, counts, n_tiles, hx)


(T_GROUP, T_LO, T_HI, T_SLOT, T_RUN_POS, T_NEXT_GROUP, T_FIRST, T_PREFETCHED) = range(8)


def _experts_kernel(layer, tab_ref, nt_ref, hs_ref, w1_hbm, w3_hbm, w2_hbm, ys_ref,
                    w1_ref, w3_ref, w2_ref, st1_ref, st3_ref, st2_ref, sem_ref):
    t = pl.program_id(0)
    valid = t < nt_ref[0]
    group = tab_ref[T_GROUP, t]
    slot = tab_ref[T_SLOT, t]
    run_pos = tab_ref[T_RUN_POS, t]
    next_group = tab_ref[T_NEXT_GROUP, t]

    def expert_copies(grp, e):
        idx = grp * EXPERTS_PER_GROUP + e
        return (pltpu.make_async_copy(w1_hbm.at[layer, idx], st1_ref, sem_ref.at[0]),
                pltpu.make_async_copy(w3_hbm.at[layer, idx], st3_ref, sem_ref.at[1]),
                pltpu.make_async_copy(w2_hbm.at[layer, idx], st2_ref, sem_ref.at[2]))

    def finish(copies, dst_slot, e):
        for cp in copies:
            cp.wait()
        w1_ref[dst_slot, e] = st1_ref[...].astype(BF16)
        w3_ref[dst_slot, e] = st3_ref[...].astype(BF16)
        w2_ref[dst_slot, e] = st2_ref[...].astype(BF16)

    @pl.when(jnp.logical_and(valid, tab_ref[T_FIRST, t] == 1))
    def _():
        def load(e, carry):
            copies = expert_copies(group, e)
            for cp in copies:
                cp.start()
            finish(copies, slot, e)
            return carry

        lax.fori_loop(tab_ref[T_PREFETCHED, t], EXPERTS_PER_GROUP, load, 0)

    prefetch = jnp.logical_and(valid, jnp.logical_and(next_group >= 0, run_pos < EXPERTS_PER_GROUP))

    @pl.when(prefetch)
    def _():
        for cp in expert_copies(next_group, run_pos):
            cp.start()

    @pl.when(valid)
    def _():
        h = hs_ref[:, 0:D_MODEL].astype(BF16)
        c = hs_ref[:, D_MODEL:]
        lane = lax.broadcasted_iota(jnp.int32, c.shape, 1)
        acc = None
        for e in (tab_ref[T_LO, t], tab_ref[T_HI, t]):
            ce = jnp.sum(jnp.where(lane == e, c, 0.0), axis=1, keepdims=True)
            a = jnp.dot(h, w1_ref[slot, e], preferred_element_type=F32)
            b = jnp.dot(h, w3_ref[slot, e], preferred_element_type=F32)
            hid = (a * _sigmoid(a)) * b * ce
            y = jnp.dot(hid.astype(BF16), w2_ref[slot, e], preferred_element_type=F32)
            acc = y if acc is None else acc + y
        ys_ref[...] = acc

    @pl.when(prefetch)
    def _():
        finish(expert_copies(next_group, run_pos), 1 - slot, run_pos)

    @pl.when(jnp.logical_not(valid))
    def _():
        ys_ref[...] = jnp.zeros_like(ys_ref)


def experts(tile_table, n_tiles, hs, w1, w3, w2, layer):
    hbm = pl.BlockSpec(memory_space=pl.ANY)
    return pl.pallas_call(
        functools.partial(_experts_kernel, layer),
        grid_spec=pltpu.PrefetchScalarGridSpec(
            num_scalar_prefetch=2,
            grid=(EXP_TILES,),
            in_specs=[pl.BlockSpec((EXP_TM, HX_W), lambda t, *_: (t, 0)), hbm, hbm, hbm],
            out_specs=pl.BlockSpec((EXP_TM, D_MODEL), lambda t, *_: (t, 0)),
            scratch_shapes=[
                pltpu.VMEM((2, EXPERTS_PER_GROUP, D_MODEL, D_EXPERT), BF16),
                pltpu.VMEM((2, EXPERTS_PER_GROUP, D_MODEL, D_EXPERT), BF16),
                pltpu.VMEM((2, EXPERTS_PER_GROUP, D_EXPERT, D_MODEL), BF16),
                pltpu.VMEM((D_MODEL, D_EXPERT), F32),
                pltpu.VMEM((D_MODEL, D_EXPERT), F32),
                pltpu.VMEM((D_EXPERT, D_MODEL), F32),
                pltpu.SemaphoreType.DMA((3,)),
            ],
        ),
        out_shape=jax.ShapeDtypeStruct((HS_ROWS, D_MODEL), F32),
        compiler_params=_params(1),
        name="experts",
    )(tile_table, n_tiles, hs, w1, w3, w2)


def _expert_tiles(counts):
    tiles = (counts + EXP_TM - 1) // EXP_TM
    ends = jnp.cumsum(tiles)
    n_tiles = ends[-1]
    t = jnp.arange(EXP_TILES, dtype=jnp.int32)
    used = t < n_tiles
    bucket = jnp.minimum(jnp.sum(t[:, None] >= ends[None, :], axis=1), N_BUCKETS - 1)
    group = bucket // N_PAIRS
    pair = bucket % N_PAIRS
    lo = sum(jnp.where(pair == k, p[0], 0) for k, p in enumerate(EXPERT_PAIRS))
    hi = sum(jnp.where(pair == k, p[1], 0) for k, p in enumerate(EXPERT_PAIRS))
    first = jnp.logical_and(used, jnp.logical_or(t == 0, group != jnp.roll(group, 1)))
    run = jnp.cumsum(first) - 1
    run_start = lax.cummax(jnp.where(first, t, 0))
    same_run = jnp.logical_and(run[:, None] == run[None, :], used[None, :])
    run_len = jnp.sum(same_run, axis=1)
    next_first = jnp.logical_and(first[None, :], run[None, :] == run[:, None] + 1)
    next_group = jnp.where(jnp.any(next_first, axis=1), jnp.sum(jnp.where(next_first, group[None, :], 0), axis=1), -1)
    prev_len = jnp.sum(jnp.where(t[None, :] == run_start[:, None] - 1, run_len[None, :], 0), axis=1)
    prefetched = jnp.where(run > 0, jnp.minimum(prev_len, EXPERTS_PER_GROUP), 0)
    table = jnp.stack([group, lo, hi, run % 2, t - run_start, next_group, first, prefetched]).astype(jnp.int32)
    return ((ends - tiles) * EXP_TM).astype(jnp.int32), table, n_tiles.reshape(1).astype(jnp.int32)


RES_TM = 512
RES_CTX_TILES = N_CTX // RES_TM


def _moe_residual_kernel(final, pos_ref, x_ref, mod_ref, gain_ref, ys_ref, *refs):
    out_refs, (ybuf_ref, sem) = refs[:-2], refs[-2:]
    i = pl.program_id(0)
    base = i * RES_TM

    def issue(g0, u):
        dst = ybuf_ref.at[pl.ds(g0, ROW_GROUP), :]
        pltpu.make_async_copy(ys_ref.at[pl.ds(pos_ref[base + g0 + u], 1), :], dst.at[pl.ds(u, 1), :], sem).start()

    _for_each_row(RES_TM, issue)
    pltpu.make_async_copy(ys_ref.at[pl.ds(0, RES_TM), :], ybuf_ref, sem).wait()
    x = x_ref[...] + mod_ref[:, 5 * D_MODEL:6 * D_MODEL] * ybuf_ref[...]
    if not final:
        out_refs[0][...] = x
        return
    y = _rms(x, gain_ref[...])

    @pl.when(i < RES_CTX_TILES)
    def _():
        out_refs[0][...] = y

    @pl.when(i >= RES_CTX_TILES)
    def _():
        out_refs[1][...] = y


def moe_residual(pos, x, mod3, gain, ys, final):
    if final:
        out_specs = _split_specs((RES_TM, D_MODEL), RES_CTX_TILES)
        out_shape = [jax.ShapeDtypeStruct((N_CTX, D_MODEL), F32), jax.ShapeDtypeStruct((N_LAT, D_MODEL), F32)]
    else:
        out_specs = pl.BlockSpec((RES_TM, D_MODEL), lambda i, *_: (i, 0))
        out_shape = jax.ShapeDtypeStruct((N_TOK, D_MODEL), F32)
    return pl.pallas_call(
        functools.partial(_moe_residual_kernel, final),
        grid_spec=pltpu.PrefetchScalarGridSpec(
            num_scalar_prefetch=1,
            grid=(N_TOK // RES_TM,),
            in_specs=[
                pl.BlockSpec((RES_TM, D_MODEL), lambda i, *_: (i, 0)),
                pl.BlockSpec((None, 1, MOD_WIDTH), lambda i, *_: (_cond_row(i, RES_TM), 0, 0)),
                pl.BlockSpec((1, D_MODEL), lambda i, *_: (0, 0)),
                pl.BlockSpec(memory_space=pl.ANY),
            ],
            out_specs=out_specs,
            scratch_shapes=[pltpu.VMEM((RES_TM, D_MODEL), F32), pltpu.SemaphoreType.DMA(())],
        ),
        out_shape=out_shape,
        compiler_params=_params(1),
        name="moe_residual_final" if final else "moe_residual",
    )(pos, x, mod3, gain, ys)


def kernel(x_prompt, x_sample, cache_k, cache_v, state_lru, c, c_ctx, w_mod, b_mod, norm_mix, norm_ffn, w_in, sgu_norm, sgu_w, sgu_b, lru_conv_w, lru_conv_b, lru_w_r, lru_b_r, lru_w_i, lru_b_i, lru_lambda, na_rpb, w_branch_sgu, w_branch_lru, w_branch_na, w_out, moe_w_group, moe_b_group, moe_w_expert, moe_b_expert, moe_w1, moe_w3, moe_w2, final_norm_gain):
    xs = (x_prompt.reshape(N_CTX, D_MODEL), x_sample.reshape(N_LAT, D_MODEL))
    cond = jnp.zeros((N_COND, D_MODEL), F32).at[0].set(c_ctx).at[1:1 + DEC_BATCH].set(c)
    mod = modulation(cond, w_mod, b_mod)
    zero_state = jnp.zeros((BATCH, 2, LRU_WIDTH), F32)
    cache_k = cache_k.reshape(DEC_BATCH, DEPTH, PAST_LEN, NA_WIDTH)
    cache_v = cache_v.reshape(DEC_BATCH, DEPTH, PAST_LEN, NA_WIDTH)
    final_gain = final_norm_gain.reshape(1, D_MODEL)
    ks, vs, ss = [], [], []
    for l in range(DEPTH):
        mod3 = mod[l].reshape(N_COND, 1, MOD_WIDTH)
        za, q, kb, vb, k_ctx, v_ctx, gates = in_projection(xs, mod3, norm_mix[l].reshape(1, D_MODEL), w_in, l)
        ya = spatial_gating(za, sgu_norm[l].reshape(1, SGU_WIDTH), sgu_w[l], sgu_b[l].T)
        w_lo, w_hi = _block_diag_gate_weights(lru_w_r[l], lru_w_i[l])
        lru_args = (lru_conv_w[l], lru_conv_b[l].reshape(1, LRU_WIDTH), w_lo, w_hi, lru_b_r[l], lru_b_i[l],
                    lru_lambda[l])
        yb_ctx, st_ctx = rglru(za, *lru_args, zero_state, SEQ, BATCH, 0)
        yb_lat, _ = rglru(za, *lru_args, state_lru[:, l], DEC_SEQ, DEC_BATCH, N_CTX)
        yc_ctx = context_attention(q, kb, vb)
        yc_lat = latent_attention(q, kb, vb, cache_k, cache_v, na_rpb[l].reshape(-1), l)
        x = merge_branches(xs, mod3, ya, yb_ctx, yb_lat, yc_ctx, yc_lat, gates,
                           w_branch_sgu, w_branch_lru, w_branch_na, w_out, l)
        w_rt, b_rt = _router_weights(moe_w_group[l], moe_b_group[l], moe_w_expert[l], moe_b_expert[l])
        hx, bucket, rank, counts = router(x, mod3, norm_ffn[l].reshape(1, D_MODEL), w_rt, b_rt)
        bucket, rank, counts = bucket.reshape(N_TOK), rank.reshape(N_TOK), counts[:N_BUCKETS, 0]
        starts, tile_table, n_tiles = _expert_tiles(counts)
        pos = rank + jnp.sum(jnp.where(bucket[:, None] == jnp.arange(N_BUCKETS), starts[None, :], 0), axis=1)
        hs = dispatch(pos, starts, counts, n_tiles, hx)
        ys = experts(tile_table, n_tiles, hs, moe_w1, moe_w3, moe_w2, l)
        if l < DEPTH - 1:
            xs = (moe_residual(pos, x, mod3, final_gain, ys, False),)
        else:
            y_ctx, y_lat = moe_residual(pos, x, mod3, final_gain, ys, True)
        ks.append(k_ctx.reshape(BATCH, SEQ, NA_HEADS, NA_HEAD_DIM))
        vs.append(v_ctx.reshape(BATCH, SEQ, NA_HEADS, NA_HEAD_DIM))
        ss.append(st_ctx)
    return (y_ctx.reshape(BATCH, SEQ, D_MODEL), y_lat.reshape(DEC_BATCH, DEC_SEQ, D_MODEL),
            jnp.stack(ks, axis=1), jnp.stack(vs, axis=1), jnp.stack(ss, axis=1))
```

```python
import functools

import jax
import jax.numpy as jnp
import numpy as np
from jax import lax
from jax.experimental import pallas as pl
from jax.experimental.pallas import tpu as pltpu

F32 = jnp.float32
BF16 = jnp.bfloat16

D_MODEL = 1024
BATCH = 16
SEQ = 256
DEPTH = 2
DEC_BATCH = 4
DEC_SEQ = 2048
PAST_LEN = 512
GRID_W = 64
CHUNK = 128
SGU_WIDTH = 512
SGU_GROUPS = 4
LRU_WIDTH = 512
LRU_BLOCKS = 8
CONV_WIDTH = 4
LRU_C = 8.0
NA_HEADS = 8
NA_HEAD_DIM = 64
NA_WIDTH = 512
WIN_ROWS = 8
WIN_COLS = 16
N_GROUPS = 4
EXPERTS_PER_GROUP = 4
N_EXPERTS = 16
D_EXPERT = 512
IN_WIDTH = 6656
EPS = 1e-6
NEG_INF = -1e30

N_CTX = BATCH * SEQ
N_LAT = DEC_BATCH * DEC_SEQ
N_TOK = N_CTX + N_LAT
N_COND = 8
MOD_WIDTH = 6 * D_MODEL
GRID_ROWS = DEC_SEQ // GRID_W

VMEM_LIMIT_BYTES = 56 * 1024 * 1024


def _params(n_axes):
    return pltpu.CompilerParams(dimension_semantics=("arbitrary",) * n_axes,
                                vmem_limit_bytes=VMEM_LIMIT_BYTES)


def _cond_row(tile, tile_rows):
    tok = tile * tile_rows
    return jnp.where(tok < N_CTX, 0, 1 + (tok - N_CTX) // DEC_SEQ)


def _rms(x, gain):
    return x * lax.rsqrt(jnp.mean(x * x, axis=-1, keepdims=True) + EPS) * gain


def _bdot(a, b):
    return jnp.dot(a.astype(BF16), b.astype(BF16), preferred_element_type=F32)


def _bdot_t(a, b):
    return lax.dot_general(a.astype(BF16), b.astype(BF16), (((1,), (1,)), ((), ())),
                           preferred_element_type=F32)


MOD_TN = 1536


def _mod_kernel(cond_ref, w_ref, b_ref, o_ref):
    c = cond_ref[...]
    s = c * jax.nn.sigmoid(c)
    o_ref[...] = _bdot(s, w_ref[...]) + b_ref[...]


def modulation(cond, w_mod, b_mod):
    return pl.pallas_call(
        _mod_kernel,
        grid=(DEPTH, MOD_WIDTH // MOD_TN),
        in_specs=[
            pl.BlockSpec((N_COND, D_MODEL), lambda l, j: (0, 0)),
            pl.BlockSpec((None, D_MODEL, MOD_TN), lambda l, j: (l, 0, j)),
            pl.BlockSpec((None, 1, MOD_TN), lambda l, j: (l, 0, j)),
        ],
        out_specs=pl.BlockSpec((None, N_COND, MOD_TN), lambda l, j: (l, 0, j)),
        out_shape=jax.ShapeDtypeStruct((DEPTH, N_COND, MOD_WIDTH), F32),
        compiler_params=_params(2),
        name="modulation",
    )(cond, w_mod, b_mod.reshape(DEPTH, 1, MOD_WIDTH))


IN_TM = 256
IN_CTX_TILES = N_CTX // IN_TM
ZA_WIDTH = 4 * 512
KV_OFF = ZA_WIDTH + NA_WIDTH
GATE_OFF = KV_OFF + 2 * NA_WIDTH
W_CHUNK = 512
Q_SCALE = NA_HEAD_DIM ** -0.5 * float(np.log2(np.e))


def _load_cast(chunks, stage_ref, sem_ref):
    def copy(j):
        return pltpu.make_async_copy(chunks[j][0], stage_ref.at[j % 2], sem_ref.at[j % 2])

    copy(0).start()
    for j in range(len(chunks)):
        if j + 1 < len(chunks):
            copy(j + 1).start()
        copy(j).wait()
        chunks[j][1](stage_ref[j % 2].astype(BF16))


def _token_tile(i, n_ctx_tiles, refs):
    if len(refs) == 1:
        return refs[0][...]
    return jnp.where(i < n_ctx_tiles, refs[0][...], refs[1][...])


def _split_specs(block, n_ctx_tiles):
    return [pl.BlockSpec(block, lambda i, *_: (jnp.minimum(i, n_ctx_tiles - 1), 0)),
            pl.BlockSpec(block, lambda i, *_: (jnp.maximum(i - n_ctx_tiles, 0), 0))]


SGU_GD = SGU_WIDTH // SGU_GROUPS


def _spatial_gating(u, v, gain, ws_ref, bs_ref):
    u = jax.nn.gelu(u)
    v = _rms(jax.nn.gelu(v), gain).astype(BF16)
    out = []
    for g in range(SGU_GROUPS):
        cols = slice(g * SGU_GD, (g + 1) * SGU_GD)
        mixed = jnp.dot(ws_ref[g].astype(BF16), v[:, cols], preferred_element_type=F32) + bs_ref[:, g:g + 1]
        out.append(u[:, cols] * mixed)
    return jnp.concatenate(out, axis=1)


def _inproj_kernel(layer, n_x, *refs):
    x_refs, (mod_ref, gain_ref, w_hbm, sgu_gain_ref, ws_ref, bs_ref) = refs[:n_x], refs[n_x:n_x + 6]
    ya_ref, zb_ref, q_ref, kb_ref, vb_ref, kc_ref, vc_ref, g_ref, w_ref, stage_ref, sem_ref = refs[n_x + 6:]
    i = pl.program_id(0)

    @pl.when(i == 0)
    def _():
        def store(c):
            def st(v):
                w_ref[:, c * W_CHUNK:(c + 1) * W_CHUNK] = v
            return st

        _load_cast([(w_hbm.at[layer, :, pl.ds(c * W_CHUNK, W_CHUNK)], store(c)) for c in range(IN_WIDTH // W_CHUNK)],
                   stage_ref, sem_ref)

    m = mod_ref[...]
    shift, scale = m[:, 0:D_MODEL], m[:, D_MODEL:2 * D_MODEL]
    h = (_rms(_token_tile(i, IN_CTX_TILES, x_refs), gain_ref[...]) * (1.0 + scale) + shift).astype(BF16)
    za = jnp.dot(h, w_ref[:, 0:ZA_WIDTH], preferred_element_type=F32)
    for c in range(IN_TM // CHUNK):
        rows = slice(c * CHUNK, (c + 1) * CHUNK)
        ya_ref[rows, :] = _spatial_gating(za[rows, 0:SGU_WIDTH], za[rows, SGU_WIDTH:2 * SGU_WIDTH],
                                          sgu_gain_ref[...], ws_ref, bs_ref).astype(BF16)
    zb_ref[...] = za[:, 2 * SGU_WIDTH:].astype(BF16)
    q_ref[...] = (jnp.dot(h, w_ref[:, ZA_WIDTH:KV_OFF], preferred_element_type=F32) * Q_SCALE).astype(BF16)
    kv = jnp.dot(h, w_ref[:, KV_OFF:GATE_OFF], preferred_element_type=F32)
    kb_ref[...] = kv[:, 0:NA_WIDTH].astype(BF16)
    vb_ref[...] = kv[:, NA_WIDTH:].astype(BF16)
    g_ref[...] = jnp.dot(h, w_ref[:, GATE_OFF:], preferred_element_type=F32).astype(BF16)

    @pl.when(i < IN_CTX_TILES)
    def _():
        kc_ref[...] = kv[:, 0:NA_WIDTH]
        vc_ref[...] = kv[:, NA_WIDTH:]


def in_projection(xs, mod3, gain, w_in, sgu_gain, sgu_w, sgu_b_t, layer):
    row = lambda i: (i, 0)
    ctx_row = lambda i: (jnp.minimum(i, IN_CTX_TILES - 1), 0)
    const = lambda shape: pl.BlockSpec(shape, lambda i: (0,) * len(shape))
    x_specs = ([pl.BlockSpec((IN_TM, D_MODEL), row)] if len(xs) == 1
               else _split_specs((IN_TM, D_MODEL), IN_CTX_TILES))
    return pl.pallas_call(
        functools.partial(_inproj_kernel, layer, len(xs)),
        grid=(N_TOK // IN_TM,),
        in_specs=x_specs + [
            pl.BlockSpec((None, 1, MOD_WIDTH), lambda i: (_cond_row(i, IN_TM), 0, 0)),
            const((1, D_MODEL)),
            pl.BlockSpec(memory_space=pl.ANY),
            const((1, SGU_WIDTH)),
            const((SGU_GROUPS, CHUNK, CHUNK)),
            const((CHUNK, SGU_GROUPS)),
        ],
        out_specs=[
            pl.BlockSpec((IN_TM, SGU_WIDTH), row),
            pl.BlockSpec((IN_TM, 2 * LRU_WIDTH), row),
            pl.BlockSpec((IN_TM, NA_WIDTH), row),
            pl.BlockSpec((IN_TM, NA_WIDTH), row),
            pl.BlockSpec((IN_TM, NA_WIDTH), row),
            pl.BlockSpec((IN_TM, NA_WIDTH), ctx_row),
            pl.BlockSpec((IN_TM, NA_WIDTH), ctx_row),
            pl.BlockSpec((IN_TM, 3 * D_MODEL), row),
        ],
        out_shape=[
            jax.ShapeDtypeStruct((N_TOK, SGU_WIDTH), BF16),
            jax.ShapeDtypeStruct((N_TOK, 2 * LRU_WIDTH), BF16),
            jax.ShapeDtypeStruct((N_TOK, NA_WIDTH), BF16),
            jax.ShapeDtypeStruct((N_TOK, NA_WIDTH), BF16),
            jax.ShapeDtypeStruct((N_TOK, NA_WIDTH), BF16),
            jax.ShapeDtypeStruct((N_CTX, NA_WIDTH), F32),
            jax.ShapeDtypeStruct((N_CTX, NA_WIDTH), F32),
            jax.ShapeDtypeStruct((N_TOK, 3 * D_MODEL), BF16),
        ],
        scratch_shapes=[pltpu.VMEM((D_MODEL, IN_WIDTH), BF16), pltpu.VMEM((2, D_MODEL, W_CHUNK), F32),
                        pltpu.SemaphoreType.DMA((2,))],
        compiler_params=_params(1),
        name="in_projection",
    )(*xs, mod3, gain, w_in, sgu_gain, sgu_w, sgu_b_t)


LRU_TC = 256
LRU_HALF = 256
SUB = 8
HALO = 8
LANES = 128
LRU_SLABS = LRU_WIDTH // LANES


def _sigmoid(x):
    return 0.5 * jnp.tanh(0.5 * x) + 0.5


def _lru_kernel(seq_len, xr_ref, gr_ref, cw_ref, cb_ref, wlo_ref, whi_ref, br_ref, bi_ref,
                lam_ref, h0_ref, y_ref, st_ref, xp_ref, a_ref, b_ref):
    seg_len = seq_len // SUB
    pitch = seg_len + SUB
    segs_per_chunk = max(LRU_TC // seg_len, 1)
    n_chunks = seq_len // LRU_TC
    zeros = jnp.zeros((HALO, LRU_WIDTH), F32)
    xp_ref[0:HALO, :] = zeros
    xp_ref[seq_len + HALO:seq_len + 2 * HALO, :] = zeros

    def copy_in(c, carry):
        r0 = pl.multiple_of(c * LRU_TC, LRU_TC)
        xp_ref[pl.ds(r0 + HALO, LRU_TC), :] = xr_ref[pl.ds(r0, LRU_TC), :].astype(F32)
        return carry

    lax.fori_loop(0, n_chunks, copy_in, 0)

    def chunk_rows(c, seg):
        n = min(seg_len, LRU_TC)
        start = pl.multiple_of((c * segs_per_chunk + seg) * pitch, SUB)
        return pl.ds(start, n), slice(seg * n, (seg + 1) * n)

    cw = cw_ref[...]
    cb = cb_ref[...]
    win = LRU_TC + 2 * HALO
    neg_lam = -lam_ref[...]
    softplus = jnp.maximum(neg_lam, 0.0) + jnp.log1p(jnp.exp(-jnp.abs(neg_lam)))
    decay = (0.5 * LRU_C) * softplus
    half_br = 0.5 * br_ref[...]
    half_bi = 0.5 * bi_ref[...]

    def gates(c, carry):
        r0 = pl.multiple_of(c * LRU_TC, LRU_TC)
        w = xp_ref[pl.ds(r0, win), :]
        xc = (cw[0:1, :] * pltpu.roll(w, 1, 0)[HALO:HALO + LRU_TC]
              + cw[1:2, :] * w[HALO:HALO + LRU_TC]
              + cw[2:3, :] * pltpu.roll(w, win - 1, 0)[HALO:HALO + LRU_TC]
              + cw[3:4, :] * pltpu.roll(w, win - 2, 0)[HALO:HALO + LRU_TC]) + cb
        xb = xc.astype(BF16)
        for half, w_ref in enumerate((wlo_ref, whi_ref)):
            cols = slice(half * LRU_HALF, (half + 1) * LRU_HALF)
            pre = jnp.dot(xb[:, cols], w_ref[...], preferred_element_type=F32)
            half_x = 0.5 * xc[:, cols]
            for d in range(2):
                tr = jnp.tanh(0.5 * pre[:, (2 * d) * LRU_HALF:(2 * d + 1) * LRU_HALF] + half_br[d:d + 1, cols])
                ti = jnp.tanh(0.5 * pre[:, (2 * d + 1) * LRU_HALF:(2 * d + 2) * LRU_HALF] + half_bi[d:d + 1, cols])
                neg_log_a = decay[d:d + 1, cols] * tr + decay[d:d + 1, cols]
                a = jnp.exp2(neg_log_a * -LOG2E)
                z = jnp.tanh(neg_log_a) * (a * a + 1.0)
                b = jnp.where(z > 0.0, z * lax.rsqrt(z), 0.0) * (half_x * ti + half_x)
                for k in range(LRU_HALF // LANES):
                    slab = half * (LRU_HALF // LANES) + k
                    lanes = slice(k * LANES, (k + 1) * LANES)
                    for seg in range(segs_per_chunk):
                        dst, src = chunk_rows(c, seg)
                        a_ref[d, slab, dst, :] = a[src, lanes]
                        b_ref[d, slab, dst, :] = b[src, lanes]
        return carry

    lax.fori_loop(0, n_chunks, gates, 0)

    def step_rows(d, j):
        return pl.ds(j if d == 0 else seg_len - 1 - j, SUB, stride=pitch)

    def reduce_step(j, carry):
        out = []
        for d in range(2):
            for slab in range(LRU_SLABS):
                big_a, big_b = carry[d * LRU_SLABS + slab]
                a = a_ref[d, slab, step_rows(d, j), :]
                b = b_ref[d, slab, step_rows(d, j), :]
                out.append((a * big_a, a * big_b + b))
        return tuple(out)

    ident = (jnp.ones((SUB, LANES), F32), jnp.zeros((SUB, LANES), F32))
    totals = lax.fori_loop(0, seg_len, reduce_step, (ident,) * (2 * LRU_SLABS), unroll=2)

    sub = lax.broadcasted_iota(jnp.int32, (SUB, LANES), 0)
    entering = []
    for d in range(2):
        for slab in range(LRU_SLABS):
            big_a, big_b = totals[d * LRU_SLABS + slab]
            h = h0_ref[d:d + 1, slab * LANES:(slab + 1) * LANES]
            rows = jnp.zeros((SUB, LANES), F32)
            for s in (range(SUB) if d == 0 else reversed(range(SUB))):
                rows = jnp.where(sub == s, h, rows)
                h = big_a[s:s + 1, :] * h + big_b[s:s + 1, :]
            entering.append(rows)
            st_ref[d:d + 1, slab * LANES:(slab + 1) * LANES] = h

    def apply_step(j, carry):
        out = []
        for d in range(2):
            for slab in range(LRU_SLABS):
                h = (a_ref[d, slab, step_rows(d, j), :] * carry[d * LRU_SLABS + slab]
                     + b_ref[d, slab, step_rows(d, j), :])
                b_ref[d, slab, step_rows(d, j), :] = h
                out.append(h)
        return tuple(out)

    lax.fori_loop(0, seg_len, apply_step, tuple(entering), unroll=2)

    def merge(c, carry):
        r0 = pl.multiple_of(c * LRU_TC, LRU_TC)
        gate = jax.nn.gelu(gr_ref[pl.ds(r0, LRU_TC), :].astype(F32))
        for slab in range(LRU_SLABS):
            lanes = slice(slab * LANES, (slab + 1) * LANES)
            for seg in range(segs_per_chunk):
                src, dst = chunk_rows(c, seg)
                h = b_ref[0, slab, src, :] + b_ref[1, slab, src, :]
                y_ref[pl.ds(r0 + dst.start, dst.stop - dst.start), lanes] = (h * gate[dst, lanes]).astype(BF16)
        return carry

    lax.fori_loop(0, n_chunks, merge, 0)


def rglru(za, conv_w, conv_b, w_lo, w_hi, b_r, b_i, lam, h0, seq_len, n_seq, tok_off):
    blk0 = tok_off // seq_len
    full = lambda shape: pl.BlockSpec(shape, lambda i: (0,) * len(shape))
    return pl.pallas_call(
        functools.partial(_lru_kernel, seq_len),
        grid=(n_seq,),
        in_specs=[
            pl.BlockSpec((seq_len, LRU_WIDTH), lambda i: (i + blk0, 0)),
            pl.BlockSpec((seq_len, LRU_WIDTH), lambda i: (i + blk0, 1)),
            full((CONV_WIDTH, LRU_WIDTH)),
            full((1, LRU_WIDTH)),
            full((LRU_HALF, 4 * LRU_HALF)),
            full((LRU_HALF, 4 * LRU_HALF)),
            full((2, LRU_WIDTH)),
            full((2, LRU_WIDTH)),
            full((2, LRU_WIDTH)),
            pl.BlockSpec((None, 2, LRU_WIDTH), lambda i: (i, 0, 0)),
        ],
        out_specs=[
            pl.BlockSpec((seq_len, LRU_WIDTH), lambda i: (i, 0)),
            pl.BlockSpec((None, 2, LRU_WIDTH), lambda i: (i, 0, 0)),
        ],
        out_shape=[
            jax.ShapeDtypeStruct((n_seq * seq_len, LRU_WIDTH), BF16),
            jax.ShapeDtypeStruct((n_seq, 2, LRU_WIDTH), F32),
        ],
        scratch_shapes=[
            pltpu.VMEM((seq_len + 2 * HALO, LRU_WIDTH), F32),
            pltpu.VMEM((2, LRU_SLABS, seq_len + SUB * SUB, LANES), F32),
            pltpu.VMEM((2, LRU_SLABS, seq_len + SUB * SUB, LANES), F32),
        ],
        compiler_params=_params(1),
        name=f"rglru_{seq_len}",
    )(za, za, conv_w, conv_b, w_lo, w_hi, b_r, b_i, lam, h0)


def _block_diag_gate_weights(w_r, w_i):
    per_half = LRU_BLOCKS // 2
    eye = jnp.eye(per_half, dtype=w_r.dtype)

    def bd(w):
        return jnp.einsum("hij,hg->higj", w, eye).reshape(LRU_HALF, LRU_HALF)

    tiles = []
    for half in range(2):
        sl = slice(half * per_half, (half + 1) * per_half)
        tiles.append(jnp.concatenate([bd(w_r[0, sl]), bd(w_i[0, sl]), bd(w_r[1, sl]), bd(w_i[1, sl])], axis=1))
    return tiles[0].astype(BF16), tiles[1].astype(BF16)


PAIR_W = 2 * NA_HEAD_DIM
LOG2E = float(np.log2(np.e))


def _head_lanes(shape, head):
    lane = lax.broadcasted_iota(jnp.int32, shape, 1)
    return (lane < NA_HEAD_DIM) if head == 0 else (lane >= NA_HEAD_DIM)


def _one_head(q_pair, head):
    return jnp.where(_head_lanes(q_pair.shape, head), q_pair.astype(F32), 0.0).astype(BF16)


def _join_heads(o0, o1):
    return jnp.where(_head_lanes(o0.shape, 0), o0, o1)


def _ctx_attn_kernel(q_ref, k_ref, v_ref, o_ref):
    for j in range(NA_HEADS // 2):
        cols = slice(j * PAIR_W, (j + 1) * PAIR_W)
        outs = []
        for head in range(2):
            s = _bdot_t(_one_head(q_ref[:, cols], head), k_ref[:, cols])
            p = jnp.exp2(s - jnp.max(s, axis=-1, keepdims=True))
            denom = jnp.sum(p, axis=-1, keepdims=True)
            outs.append(jnp.dot(p.astype(BF16), v_ref[:, cols], preferred_element_type=F32) / denom)
        o_ref[:, cols] = _join_heads(*outs).astype(BF16)


def context_attention(q, kb, vb):
    spec = pl.BlockSpec((SEQ, NA_WIDTH), lambda i: (i, 0))
    return pl.pallas_call(
        _ctx_attn_kernel,
        grid=(BATCH,),
        in_specs=[spec, spec, spec],
        out_specs=spec,
        out_shape=jax.ShapeDtypeStruct((N_CTX, NA_WIDTH), BF16),
        compiler_params=_params(1),
        name="context_attention",
    )(q, kb, vb)


NA_QROWS = 4
NA_TQ = NA_QROWS * GRID_W
NA_KROWS = NA_QROWS + WIN_ROWS
NA_TK = NA_KROWS * GRID_W
NA_QBLOCKS = GRID_ROWS // NA_QROWS


N_DROW = 2 * WIN_ROWS - 1
N_DCOL = 2 * WIN_COLS - 1
NA_BLOCK_KINDS = (0, 1, NA_QBLOCKS - 1)


def _first_key_row(qb):
    return (np.clip if isinstance(qb, int) else jnp.clip)(qb * NA_QROWS - WIN_ROWS // 2, 0, GRID_ROWS - NA_KROWS)


def _build_bias(rpb_ref, table_ref, bias_ref):
    qc = lax.broadcasted_iota(jnp.int32, (GRID_W, GRID_W), 0)
    kc = lax.broadcasted_iota(jnp.int32, (GRID_W, GRID_W), 1)
    col0 = jnp.clip(qc - WIN_COLS // 2, 0, GRID_W - WIN_COLS)
    col_ok = jnp.logical_and(kc >= col0, kc < col0 + WIN_COLS)
    d_col = jnp.clip(kc - qc, 1 - WIN_COLS, WIN_COLS - 1) + WIN_COLS - 1
    neg = jnp.full((GRID_W, GRID_W), NEG_INF, F32)

    def table_entry(idx, carry):
        t = neg
        for j in range(N_DCOL):
            t = jnp.where(d_col == j, rpb_ref[idx * N_DCOL + j], t)
        table_ref[idx] = jnp.where(col_ok, t * LOG2E, neg)
        return carry

    lax.fori_loop(0, NA_HEADS * N_DROW, table_entry, 0)

    def head_blocks(h, carry):
        for kind, qb in enumerate(NA_BLOCK_KINDS):
            for i in range(NA_QROWS):
                qrow = qb * NA_QROWS + i
                win0 = int(np.clip(qrow - WIN_ROWS // 2, 0, GRID_ROWS - WIN_ROWS))
                for kr in range(NA_KROWS):
                    krow = int(_first_key_row(qb)) + kr
                    inside = win0 <= krow < win0 + WIN_ROWS
                    blk = table_ref[h * N_DROW + (krow - qrow + WIN_ROWS - 1)] if inside else neg
                    bias_ref[kind, h, i * GRID_W:(i + 1) * GRID_W, kr * GRID_W:(kr + 1) * GRID_W] = blk
        return carry

    lax.fori_loop(0, NA_HEADS, head_blocks, 0)


def _lat_attn_kernel(rpb_ref, q_ref, k_ref, v_ref, ck_ref, cv_ref, o_ref, table_ref, bias_ref):
    qb = pl.program_id(1)

    @pl.when(jnp.logical_and(pl.program_id(0) == 0, qb == 0))
    def _():
        _build_bias(rpb_ref, table_ref, bias_ref)

    kind = jnp.where(qb == 0, 0, jnp.where(qb == NA_QBLOCKS - 1, 2, 1))
    k0 = pl.multiple_of(_first_key_row(qb) * GRID_W, GRID_W)
    for j in range(NA_HEADS // 2):
        cols = slice(j * PAIR_W, (j + 1) * PAIR_W)
        k_loc = k_ref[pl.ds(k0, NA_TK), cols]
        v_loc = v_ref[pl.ds(k0, NA_TK), cols]
        k_ctx = ck_ref[:, cols].astype(BF16)
        v_ctx = cv_ref[:, cols].astype(BF16)
        outs = []
        for head in range(2):
            qh = _one_head(q_ref[:, cols], head)
            s_loc = _bdot_t(qh, k_loc) + bias_ref[kind, 2 * j + head]
            s_ctx = _bdot_t(qh, k_ctx)
            m = jnp.maximum(jnp.max(s_loc, axis=-1, keepdims=True), jnp.max(s_ctx, axis=-1, keepdims=True))
            p_loc = jnp.exp2(s_loc - m)
            p_ctx = jnp.exp2(s_ctx - m)
            denom = jnp.sum(p_loc, axis=-1, keepdims=True) + jnp.sum(p_ctx, axis=-1, keepdims=True)
            o = (jnp.dot(p_loc.astype(BF16), v_loc, preferred_element_type=F32)
                 + jnp.dot(p_ctx.astype(BF16), v_ctx, preferred_element_type=F32))
            outs.append(o / denom)
        o_ref[:, cols] = _join_heads(*outs).astype(BF16)


def latent_attention(q, kb, vb, cache_k, cache_v, rpb, layer):
    q_blk0 = N_CTX // NA_TQ
    s_blk0 = N_CTX // DEC_SEQ
    cache_spec = pl.BlockSpec((None, None, PAST_LEN, NA_WIDTH), lambda b, m: (b, layer, 0, 0))
    return pl.pallas_call(
        _lat_attn_kernel,
        grid=(DEC_BATCH, NA_QBLOCKS),
        in_specs=[
            pl.BlockSpec(memory_space=pltpu.SMEM),
            pl.BlockSpec((NA_TQ, NA_WIDTH), lambda b, m: (q_blk0 + b * NA_QBLOCKS + m, 0)),
            pl.BlockSpec((DEC_SEQ, NA_WIDTH), lambda b, m: (s_blk0 + b, 0)),
            pl.BlockSpec((DEC_SEQ, NA_WIDTH), lambda b, m: (s_blk0 + b, 0)),
            cache_spec,
            cache_spec,
        ],
        out_specs=pl.BlockSpec((NA_TQ, NA_WIDTH), lambda b, m: (b * NA_QBLOCKS + m, 0)),
        out_shape=jax.ShapeDtypeStruct((N_LAT, NA_WIDTH), BF16),
        scratch_shapes=[pltpu.VMEM((NA_HEADS * N_DROW, GRID_W, GRID_W), F32),
                        pltpu.VMEM((len(NA_BLOCK_KINDS), NA_HEADS, NA_TQ, NA_TK), F32)],
        compiler_params=_params(2),
        name="latent_attention",
    )(rpb, q, kb, vb, cache_k, cache_v)


MERGE_TM = 512


MERGE_CTX_TILES = N_CTX // MERGE_TM
BRANCH_W = 512


def _merge_kernel(layer, n_x, *refs):
    x_refs, rest = refs[:n_x], refs[n_x:]
    (mod_ref, ya_ref, yb_ctx_ref, yb_lat_ref, yc_ctx_ref, yc_lat_ref, g_ref, wa_hbm, wb_hbm, wc_hbm, wo_hbm,
     ffn_gain_ref, wrt_ref, brt_ref, tri_ref,
     o_ref, hx_ref, bucket_ref, rank_ref, cnt_out_ref,
     wbr_ref, wo_ref, stage_ref, sem_ref, cnt_ref, cpad_ref) = rest
    i = pl.program_id(0)

    @pl.when(i == 0)
    def _():
        def store_branch(k):
            def st(v):
                wbr_ref[k] = v
            return st

        def store_out(k):
            def st(v):
                wo_ref[k * BRANCH_W:(k + 1) * BRANCH_W, :] = v
            return st

        chunks = [(w.at[layer], store_branch(k)) for k, w in enumerate((wa_hbm, wb_hbm, wc_hbm))]
        chunks += [(wo_hbm.at[layer, pl.ds(k * BRANCH_W, BRANCH_W), :], store_out(k))
                   for k in range(D_MODEL // BRANCH_W)]
        _load_cast(chunks, stage_ref, sem_ref)

    g = _sigmoid(g_ref[...].astype(F32))
    yb = _token_tile(i, MERGE_CTX_TILES, (yb_ctx_ref, yb_lat_ref))
    yc = _token_tile(i, MERGE_CTX_TILES, (yc_ctx_ref, yc_lat_ref))
    merged = (g[:, 0:D_MODEL] * jnp.dot(ya_ref[...], wbr_ref[0], preferred_element_type=F32)
              + g[:, D_MODEL:2 * D_MODEL] * jnp.dot(yb, wbr_ref[1], preferred_element_type=F32)
              + g[:, 2 * D_MODEL:] * jnp.dot(yc, wbr_ref[2], preferred_element_type=F32))
    y = jnp.dot(merged.astype(BF16), wo_ref[...], preferred_element_type=F32)
    gate = mod_ref[:, 2 * D_MODEL:3 * D_MODEL]
    x = _token_tile(i, MERGE_CTX_TILES, x_refs) + gate * y
    o_ref[...] = x
    _route(x, mod_ref, ffn_gain_ref, wrt_ref, brt_ref, tri_ref, hx_ref, bucket_ref, rank_ref, cnt_out_ref,
           cnt_ref, cpad_ref)


def merge_branches(xs, mod3, ya, yb_ctx, yb_lat, yc_ctx, yc_lat, gates, wa, wb, wc, wo, ffn_gain, w_rt, b_rt, layer):
    assert MERGE_TM == RT_TM
    row = lambda i: (i, 0)
    const = lambda shape: pl.BlockSpec(shape, lambda i: (0,) * len(shape))
    tri = jnp.asarray(np.triu(np.ones((RT_TM, RT_TM), np.float32)), BF16)
    hbm = pl.BlockSpec(memory_space=pl.ANY)
    x_specs = ([pl.BlockSpec((MERGE_TM, D_MODEL), row)] if len(xs) == 1
               else _split_specs((MERGE_TM, D_MODEL), MERGE_CTX_TILES))
    return pl.pallas_call(
        functools.partial(_merge_kernel, layer, len(xs)),
        grid=(N_TOK // MERGE_TM,),
        in_specs=x_specs + [
            pl.BlockSpec((None, 1, MOD_WIDTH), lambda i: (_cond_row(i, MERGE_TM), 0, 0)),
            pl.BlockSpec((MERGE_TM, SGU_WIDTH), row),
            *_split_specs((MERGE_TM, LRU_WIDTH), MERGE_CTX_TILES),
            *_split_specs((MERGE_TM, NA_WIDTH), MERGE_CTX_TILES),
            pl.BlockSpec((MERGE_TM, 3 * D_MODEL), row),
            hbm, hbm, hbm, hbm,
            const((1, D_MODEL)),
            const((RT_ROWS, D_MODEL)),
            const((RT_ROWS, 1)),
            const((RT_TM, RT_TM)),
        ],
        out_specs=[
            pl.BlockSpec((MERGE_TM, D_MODEL), row),
            pl.BlockSpec((RT_TM, HX_W), row),
            pl.BlockSpec((1, RT_TM), lambda i: (0, i)),
            pl.BlockSpec((1, RT_TM), lambda i: (0, i)),
            const((CNT_ROWS, COMB_W)),
        ],
        out_shape=[
            jax.ShapeDtypeStruct((N_TOK, D_MODEL), F32),
            jax.ShapeDtypeStruct((N_TOK, HX_W), F32),
            jax.ShapeDtypeStruct((1, N_TOK), jnp.int32),
            jax.ShapeDtypeStruct((1, N_TOK), jnp.int32),
            jax.ShapeDtypeStruct((CNT_ROWS, COMB_W), jnp.int32),
        ],
        scratch_shapes=[pltpu.VMEM((3, BRANCH_W, D_MODEL), BF16), pltpu.VMEM((D_MODEL, D_MODEL), BF16),
                        pltpu.VMEM((2, BRANCH_W, D_MODEL), F32), pltpu.SemaphoreType.DMA((2,)),
                        pltpu.VMEM((CNT_ROWS, COMB_W), F32), pltpu.VMEM((COMB_W, RT_TM), F32)],
        compiler_params=_params(1),
        name="merge_branches",
    )(*xs, mod3, ya, yb_ctx, yb_lat, yc_ctx, yc_lat, gates, wa, wb, wc, wo, ffn_gain, w_rt, b_rt, tri)


RT_TM = 512
RT_ROWS = 32
EXPERT_PAIRS = ((0, 1), (0, 2), (0, 3), (1, 2), (1, 3), (2, 3))
N_PAIRS = len(EXPERT_PAIRS)
N_BUCKETS = N_GROUPS * N_PAIRS
CNT_ROWS = 32
COMB_W = 128
HX_W = D_MODEL + COMB_W
EXP_TM = 256
EXP_TILES = N_TOK // EXP_TM + N_BUCKETS
HS_ROWS = EXP_TILES * EXP_TM


def _split_bf16(x):
    hi = x.astype(BF16)
    return hi, (x - hi.astype(F32)).astype(BF16)


def _route(x, mod_ref, gain_ref, w_ref, b_ref, tri_ref, hx_ref, bucket_ref, rank_ref, cnt_out_ref,
           cnt_ref, cpad_ref):
    @pl.when(pl.program_id(0) == 0)
    def _():
        cnt_ref[...] = jnp.zeros_like(cnt_ref)
        cpad_ref[...] = jnp.zeros_like(cpad_ref)

    m = mod_ref[...]
    shift, scale = m[:, 3 * D_MODEL:4 * D_MODEL], m[:, 4 * D_MODEL:5 * D_MODEL]
    h = _rms(x, gain_ref[...]) * (1.0 + scale) + shift
    hx_ref[:, 0:D_MODEL] = h
    h_hi, h_lo = _split_bf16(h)
    w_hi, w_lo = _split_bf16(w_ref[...])
    dims = (((1,), (1,)), ((), ()))
    logits = (lax.dot_general(w_hi, h_hi, dims, preferred_element_type=F32)
              + lax.dot_general(w_hi, h_lo, dims, preferred_element_type=F32)
              + lax.dot_general(w_lo, h_hi, dims, preferred_element_type=F32)) + b_ref[...]
    gl = [logits[g:g + 1, :] for g in range(N_GROUPS)]
    gmax = functools.reduce(jnp.maximum, gl)
    gid = jnp.full(gmax.shape, N_GROUPS - 1, jnp.int32)
    for g in reversed(range(N_GROUPS - 1)):
        gid = jnp.where(gl[g] == gmax, g, gid)
    p_grp = 1.0 / functools.reduce(jnp.add, [jnp.exp(v - gmax) for v in gl])
    el = []
    for e in range(EXPERTS_PER_GROUP):
        v = logits[8 + e:9 + e, :]
        for g in range(1, N_GROUPS):
            row = 8 + g * EXPERTS_PER_GROUP + e
            v = jnp.where(gid == g, logits[row:row + 1, :], v)
        el.append(v)
    top1 = functools.reduce(jnp.maximum, el)
    idx1 = jnp.full(top1.shape, EXPERTS_PER_GROUP - 1, jnp.int32)
    for e in reversed(range(EXPERTS_PER_GROUP - 1)):
        idx1 = jnp.where(el[e] == top1, e, idx1)
    rest = [jnp.where(idx1 == e, -jnp.inf, el[e]) for e in range(EXPERTS_PER_GROUP)]
    top2 = functools.reduce(jnp.maximum, rest)
    idx2 = jnp.full(top1.shape, EXPERTS_PER_GROUP - 1, jnp.int32)
    for e in reversed(range(EXPERTS_PER_GROUP - 1)):
        idx2 = jnp.where(rest[e] == top2, e, idx2)
    e2 = jnp.exp(top2 - top1)
    w1 = p_grp / (1.0 + e2)
    w2 = p_grp * e2 / (1.0 + e2)
    for e in range(EXPERTS_PER_GROUP):
        cpad_ref[e:e + 1, :] = jnp.where(idx1 == e, w1, 0.0) + jnp.where(idx2 == e, w2, 0.0)
    hx_ref[:, D_MODEL:] = cpad_ref[...].T
    lo = jnp.minimum(idx1, idx2)
    hi = jnp.maximum(idx1, idx2)
    pair = jnp.where(lo == 0, 0, jnp.where(lo == 1, 3, 5)) + (hi - lo - 1)
    bucket = gid * N_PAIRS + pair
    bucket_ref[...] = bucket
    sub = lax.broadcasted_iota(jnp.int32, (CNT_ROWS, RT_TM), 0)
    onehot = jnp.where(sub == bucket, 1.0, 0.0)
    seen = jnp.dot(onehot.astype(BF16), tri_ref[...], preferred_element_type=F32)
    cnt = cnt_ref[...]
    rank_ref[...] = jnp.sum(onehot * (seen - 1.0 + cnt[:, 0:1]), axis=0, keepdims=True).astype(jnp.int32)
    cnt = cnt + jnp.sum(onehot, axis=1, keepdims=True)
    cnt_ref[...] = cnt
    cnt_out_ref[...] = cnt.astype(jnp.int32)


def _router_weights(w_grp, b_grp, w_exp, b_exp):
    w = jnp.zeros((RT_ROWS, D_MODEL), F32).at[0:N_GROUPS].set(w_grp.T).at[8:8 + N_EXPERTS].set(w_exp.T)
    b = jnp.zeros((RT_ROWS, 1), F32).at[0:N_GROUPS, 0].set(b_grp).at[8:8 + N_EXPERTS, 0].set(b_exp)
    return w, b


DISP_TM = 512


ROW_GROUP = 64


def _for_each_row(n_rows, fn):
    def group(k, carry):
        g0 = pl.multiple_of(k * ROW_GROUP, ROW_GROUP)
        for u in range(ROW_GROUP):
            fn(g0, u)
        return carry

    lax.fori_loop(0, n_rows // ROW_GROUP, group, 0)


def _dispatch_kernel(pos_ref, start_ref, cnt_ref, nt_ref, hx_ref, hs_ref, zero_ref, sem, zsem):
    i = pl.program_id(0)
    base = i * DISP_TM

    @pl.when(i == 0)
    def _():
        zero_ref[...] = jnp.zeros_like(zero_ref)

        def tile_copy(t):
            return pltpu.make_async_copy(zero_ref, hs_ref.at[pl.ds(pl.multiple_of(t * EXP_TM, EXP_TM), EXP_TM), :],
                                         zsem)

        def last_tile(g):
            return (start_ref[g] + cnt_ref[g] - 1) // EXP_TM

        def each_tile(fn):
            for g in range(N_BUCKETS):
                pl.when(cnt_ref[g] > 0)(functools.partial(fn, last_tile(g)))
            lax.fori_loop(nt_ref[0], EXP_TILES, lambda t, carry: (fn(t), carry)[1], 0)

        each_tile(lambda t: tile_copy(t).start())
        each_tile(lambda t: tile_copy(t).wait())

    def issue(g0, u):
        src = hx_ref.at[pl.ds(g0, ROW_GROUP), :]
        pltpu.make_async_copy(src.at[pl.ds(u, 1), :], hs_ref.at[pl.ds(pos_ref[base + g0 + u], 1), :], sem).start()

    _for_each_row(DISP_TM, issue)
    pltpu.make_async_copy(hx_ref, hs_ref.at[pl.ds(0, DISP_TM), :], sem).wait()


def dispatch(pos, starts, counts, n_tiles, hx):
    return pl.pallas_call(
        _dispatch_kernel,
        grid_spec=pltpu.PrefetchScalarGridSpec(
            num_scalar_prefetch=4,
            grid=(N_TOK // DISP_TM,),
            in_specs=[pl.BlockSpec((DISP_TM, HX_W), lambda i, *_: (i, 0))],
            out_specs=pl.BlockSpec(memory_space=pl.ANY),
            scratch_shapes=[pltpu.VMEM((EXP_TM, HX_W), F32), pltpu.SemaphoreType.DMA(()),
                            pltpu.SemaphoreType.DMA(())],
        ),
        out_shape=jax.ShapeDtypeStruct((HS_ROWS, HX_W), F32),
        compiler_params=_params(1),
        name="dispatch",
    )(pos, starts, counts, n_tiles, hx)


(T_GROUP, T_LO, T_HI, T_SLOT, T_RUN_POS, T_NEXT_GROUP, T_FIRST, T_PREFETCHED) = range(8)


def _experts_kernel(layer, tab_ref, nt_ref, hs_ref, w1_hbm, w3_hbm, w2_hbm, ys_ref,
                    w1_ref, w3_ref, w2_ref, st1_ref, st3_ref, st2_ref, sem_ref):
    t = pl.program_id(0)
    valid = t < nt_ref[0]
    group = tab_ref[T_GROUP, t]
    slot = tab_ref[T_SLOT, t]
    run_pos = tab_ref[T_RUN_POS, t]
    next_group = tab_ref[T_NEXT_GROUP, t]

    def expert_copies(grp, e):
        idx = grp * EXPERTS_PER_GROUP + e
        return (pltpu.make_async_copy(w1_hbm.at[layer, idx], st1_ref, sem_ref.at[0]),
                pltpu.make_async_copy(w3_hbm.at[layer, idx], st3_ref, sem_ref.at[1]),
                pltpu.make_async_copy(w2_hbm.at[layer, idx], st2_ref, sem_ref.at[2]))

    def finish(copies, dst_slot, e):
        for cp in copies:
            cp.wait()
        w1_ref[dst_slot, e] = st1_ref[...].astype(BF16)
        w3_ref[dst_slot, e] = st3_ref[...].astype(BF16)
        w2_ref[dst_slot, e] = st2_ref[...].astype(BF16)

    @pl.when(jnp.logical_and(valid, tab_ref[T_FIRST, t] == 1))
    def _():
        def load(e, carry):
            copies = expert_copies(group, e)
            for cp in copies:
                cp.start()
            finish(copies, slot, e)
            return carry

        lax.fori_loop(tab_ref[T_PREFETCHED, t], EXPERTS_PER_GROUP, load, 0)

    prefetch = jnp.logical_and(valid, jnp.logical_and(next_group >= 0, run_pos < EXPERTS_PER_GROUP))

    @pl.when(prefetch)
    def _():
        for cp in expert_copies(next_group, run_pos):
            cp.start()

    @pl.when(valid)
    def _():
        h = hs_ref[:, 0:D_MODEL].astype(BF16)
        c = hs_ref[:, D_MODEL:]
        lane = lax.broadcasted_iota(jnp.int32, c.shape, 1)
        acc = None
        for e in (tab_ref[T_LO, t], tab_ref[T_HI, t]):
            ce = jnp.sum(jnp.where(lane == e, c, 0.0), axis=1, keepdims=True)
            a = jnp.dot(h, w1_ref[slot, e], preferred_element_type=F32)
            b = jnp.dot(h, w3_ref[slot, e], preferred_element_type=F32)
            hid = (a * _sigmoid(a)) * b * ce
            y = jnp.dot(hid.astype(BF16), w2_ref[slot, e], preferred_element_type=F32)
            acc = y if acc is None else acc + y
        ys_ref[...] = acc

    @pl.when(prefetch)
    def _():
        finish(expert_copies(next_group, run_pos), 1 - slot, run_pos)

    @pl.when(jnp.logical_not(valid))
    def _():
        ys_ref[...] = jnp.zeros_like(ys_ref)


def experts(tile_table, n_tiles, hs, w1, w3, w2, layer):
    hbm = pl.BlockSpec(memory_space=pl.ANY)
    return pl.pallas_call(
        functools.partial(_experts_kernel, layer),
        grid_spec=pltpu.PrefetchScalarGridSpec(
            num_scalar_prefetch=2,
            grid=(EXP_TILES,),
            in_specs=[pl.BlockSpec((EXP_TM, HX_W), lambda t, *_: (t, 0)), hbm, hbm, hbm],
            out_specs=pl.BlockSpec((EXP_TM, D_MODEL), lambda t, *_: (t, 0)),
            scratch_shapes=[
                pltpu.VMEM((2, EXPERTS_PER_GROUP, D_MODEL, D_EXPERT), BF16),
                pltpu.VMEM((2, EXPERTS_PER_GROUP, D_MODEL, D_EXPERT), BF16),
                pltpu.VMEM((2, EXPERTS_PER_GROUP, D_EXPERT, D_MODEL), BF16),
                pltpu.VMEM((D_MODEL, D_EXPERT), F32),
                pltpu.VMEM((D_MODEL, D_EXPERT), F32),
                pltpu.VMEM((D_EXPERT, D_MODEL), F32),
                pltpu.SemaphoreType.DMA((3,)),
            ],
        ),
        out_shape=jax.ShapeDtypeStruct((HS_ROWS, D_MODEL), F32),
        compiler_params=_params(1),
        name="experts",
    )(tile_table, n_tiles, hs, w1, w3, w2)


def _expert_tiles(counts):
    tiles = (counts + EXP_TM - 1) // EXP_TM
    ends = jnp.cumsum(tiles)
    n_tiles = ends[-1]
    t = jnp.arange(EXP_TILES, dtype=jnp.int32)
    used = t < n_tiles
    bucket = jnp.minimum(jnp.sum(t[:, None] >= ends[None, :], axis=1), N_BUCKETS - 1)
    group = bucket // N_PAIRS
    pair = bucket % N_PAIRS
    lo = sum(jnp.where(pair == k, p[0], 0) for k, p in enumerate(EXPERT_PAIRS))
    hi = sum(jnp.where(pair == k, p[1], 0) for k, p in enumerate(EXPERT_PAIRS))
    first = jnp.logical_and(used, jnp.logical_or(t == 0, group != jnp.roll(group, 1)))
    run = jnp.cumsum(first) - 1
    run_start = lax.cummax(jnp.where(first, t, 0))
    same_run = jnp.logical_and(run[:, None] == run[None, :], used[None, :])
    run_len = jnp.sum(same_run, axis=1)
    next_first = jnp.logical_and(first[None, :], run[None, :] == run[:, None] + 1)
    next_group = jnp.where(jnp.any(next_first, axis=1), jnp.sum(jnp.where(next_first, group[None, :], 0), axis=1), -1)
    prev_len = jnp.sum(jnp.where(t[None, :] == run_start[:, None] - 1, run_len[None, :], 0), axis=1)
    prefetched = jnp.where(run > 0, jnp.minimum(prev_len, EXPERTS_PER_GROUP), 0)
    table = jnp.stack([group, lo, hi, run % 2, t - run_start, next_group, first, prefetched]).astype(jnp.int32)
    return ((ends - tiles) * EXP_TM).astype(jnp.int32), table, n_tiles.reshape(1).astype(jnp.int32)


RES_TM = 512
RES_CTX_TILES = N_CTX // RES_TM


def _moe_residual_kernel(final, pos_ref, x_ref, mod_ref, gain_ref, ys_ref, *refs):
    out_refs, (ybuf_ref, sem) = refs[:-2], refs[-2:]
    i = pl.program_id(0)
    base = i * RES_TM

    def issue(g0, u):
        dst = ybuf_ref.at[pl.ds(g0, ROW_GROUP), :]
        pltpu.make_async_copy(ys_ref.at[pl.ds(pos_ref[base + g0 + u], 1), :], dst.at[pl.ds(u, 1), :], sem).start()

    _for_each_row(RES_TM, issue)
    pltpu.make_async_copy(ys_ref.at[pl.ds(0, RES_TM), :], ybuf_ref, sem).wait()
    x = x_ref[...] + mod_ref[:, 5 * D_MODEL:6 * D_MODEL] * ybuf_ref[...]
    if not final:
        out_refs[0][...] = x
        return
    y = _rms(x, gain_ref[...])

    @pl.when(i < RES_CTX_TILES)
    def _():
        out_refs[0][...] = y

    @pl.when(i >= RES_CTX_TILES)
    def _():
        out_refs[1][...] = y


def moe_residual(pos, x, mod3, gain, ys, final):
    if final:
        out_specs = _split_specs((RES_TM, D_MODEL), RES_CTX_TILES)
        out_shape = [jax.ShapeDtypeStruct((N_CTX, D_MODEL), F32), jax.ShapeDtypeStruct((N_LAT, D_MODEL), F32)]
    else:
        out_specs = pl.BlockSpec((RES_TM, D_MODEL), lambda i, *_: (i, 0))
        out_shape = jax.ShapeDtypeStruct((N_TOK, D_MODEL), F32)
    return pl.pallas_call(
        functools.partial(_moe_residual_kernel, final),
        grid_spec=pltpu.PrefetchScalarGridSpec(
            num_scalar_prefetch=1,
            grid=(N_TOK // RES_TM,),
            in_specs=[
                pl.BlockSpec((RES_TM, D_MODEL), lambda i, *_: (i, 0)),
                pl.BlockSpec((None, 1, MOD_WIDTH), lambda i, *_: (_cond_row(i, RES_TM), 0, 0)),
                pl.BlockSpec((1, D_MODEL), lambda i, *_: (0, 0)),
                pl.BlockSpec(memory_space=pl.ANY),
            ],
            out_specs=out_specs,
            scratch_shapes=[pltpu.VMEM((RES_TM, D_MODEL), F32), pltpu.SemaphoreType.DMA(())],
        ),
        out_shape=out_shape,
        compiler_params=_params(1),
        name="moe_residual_final" if final else "moe_residual",
    )(pos, x, mod3, gain, ys)


def kernel(x_prompt, x_sample, cache_k, cache_v, state_lru, c, c_ctx, w_mod, b_mod, norm_mix, norm_ffn, w_in, sgu_norm, sgu_w, sgu_b, lru_conv_w, lru_conv_b, lru_w_r, lru_b_r, lru_w_i, lru_b_i, lru_lambda, na_rpb, w_branch_sgu, w_branch_lru, w_branch_na, w_out, moe_w_group, moe_b_group, moe_w_expert, moe_b_expert, moe_w1, moe_w3, moe_w2, final_norm_gain):
    xs = (x_prompt.reshape(N_CTX, D_MODEL), x_sample.reshape(N_LAT, D_MODEL))
    cond = jnp.zeros((N_COND, D_MODEL), F32).at[0].set(c_ctx).at[1:1 + DEC_BATCH].set(c)
    mod = modulation(cond, w_mod, b_mod)
    zero_state = jnp.zeros((BATCH, 2, LRU_WIDTH), F32)
    cache_k = cache_k.reshape(DEC_BATCH, DEPTH, PAST_LEN, NA_WIDTH)
    cache_v = cache_v.reshape(DEC_BATCH, DEPTH, PAST_LEN, NA_WIDTH)
    final_gain = final_norm_gain.reshape(1, D_MODEL)
    ks, vs, ss = [], [], []
    for l in range(DEPTH):
        mod3 = mod[l].reshape(N_COND, 1, MOD_WIDTH)
        ya, za, q, kb, vb, k_ctx, v_ctx, gates = in_projection(
            xs, mod3, norm_mix[l].reshape(1, D_MODEL), w_in, sgu_norm[l].reshape(1, SGU_WIDTH), sgu_w[l], sgu_b[l].T, l)
        w_lo, w_hi = _block_diag_gate_weights(lru_w_r[l], lru_w_i[l])
        lru_args = (lru_conv_w[l], lru_conv_b[l].reshape(1, LRU_WIDTH), w_lo, w_hi, lru_b_r[l], lru_b_i[l],
                    lru_lambda[l])
        yb_ctx, st_ctx = rglru(za, *lru_args, zero_state, SEQ, BATCH, 0)
        yb_lat, _ = rglru(za, *lru_args, state_lru[:, l], DEC_SEQ, DEC_BATCH, N_CTX)
        yc_ctx = context_attention(q, kb, vb)
        yc_lat = latent_attention(q, kb, vb, cache_k, cache_v, na_rpb[l].reshape(-1), l)
        w_rt, b_rt = _router_weights(moe_w_group[l], moe_b_group[l], moe_w_expert[l], moe_b_expert[l])
        x, hx, bucket, rank, counts = merge_branches(
            xs, mod3, ya, yb_ctx, yb_lat, yc_ctx, yc_lat, gates, w_branch_sgu, w_branch_lru, w_branch_na, w_out,
            norm_ffn[l].reshape(1, D_MODEL), w_rt, b_rt, l)
        bucket, rank, counts = bucket.reshape(N_TOK), rank.reshape(N_TOK), counts[:N_BUCKETS, 0]
        starts, tile_table, n_tiles = _expert_tiles(counts)
        pos = rank + jnp.sum(jnp.where(bucket[:, None] == jnp.arange(N_BUCKETS), starts[None, :], 0), axis=1)
        hs = dispatch(pos, starts, counts, n_tiles, hx)
        ys = experts(tile_table, n_tiles, hs, moe_w1, moe_w3, moe_w2, l)
        if l < DEPTH - 1:
            xs = (moe_residual(pos, x, mod3, final_gain, ys, False),)
        else:
            y_ctx, y_lat = moe_residual(pos, x, mod3, final_gain, ys, True)
        ks.append(k_ctx.reshape(BATCH, SEQ, NA_HEADS, NA_HEAD_DIM))
        vs.append(v_ctx.reshape(BATCH, SEQ, NA_HEADS, NA_HEAD_DIM))
        ss.append(st_ctx)
    return (y_ctx.reshape(BATCH, SEQ, D_MODEL), y_lat.reshape(DEC_BATCH, DEC_SEQ, D_MODEL),
            jnp.stack(ks, axis=1), jnp.stack(vs, axis=1), jnp.stack(ss, axis=1))
```

```python
import functools

import jax
import jax.numpy as jnp
import numpy as np
from jax import lax
from jax.experimental import pallas as pl
from jax.experimental.pallas import tpu as pltpu

F32 = jnp.float32
BF16 = jnp.bfloat16

D_MODEL = 1024
BATCH = 16
SEQ = 256
DEPTH = 2
DEC_BATCH = 4
DEC_SEQ = 2048
PAST_LEN = 512
GRID_W = 64
CHUNK = 128
SGU_WIDTH = 512
SGU_GROUPS = 4
LRU_WIDTH = 512
LRU_BLOCKS = 8
CONV_WIDTH = 4
LRU_C = 8.0
NA_HEADS = 8
NA_HEAD_DIM = 64
NA_WIDTH = 512
WIN_ROWS = 8
WIN_COLS = 16
N_GROUPS = 4
EXPERTS_PER_GROUP = 4
N_EXPERTS = 16
D_EXPERT = 512
IN_WIDTH = 6656
EPS = 1e-6
NEG_INF = -1e30

N_CTX = BATCH * SEQ
N_LAT = DEC_BATCH * DEC_SEQ
N_TOK = N_CTX + N_LAT
N_COND = 8
MOD_WIDTH = 6 * D_MODEL
GRID_ROWS = DEC_SEQ // GRID_W

VMEM_LIMIT_BYTES = 56 * 1024 * 1024


def _params(n_axes):
    return pltpu.CompilerParams(dimension_semantics=("arbitrary",) * n_axes,
                                vmem_limit_bytes=VMEM_LIMIT_BYTES)


def _cond_row(tile, tile_rows):
    tok = tile * tile_rows
    return jnp.where(tok < N_CTX, 0, 1 + (tok - N_CTX) // DEC_SEQ)


def _rms(x, gain):
    return x * lax.rsqrt(jnp.mean(x * x, axis=-1, keepdims=True) + EPS) * gain


def _bdot(a, b):
    return jnp.dot(a.astype(BF16), b.astype(BF16), preferred_element_type=F32)


def _bdot_t(a, b):
    return lax.dot_general(a.astype(BF16), b.astype(BF16), (((1,), (1,)), ((), ())),
                           preferred_element_type=F32)


MOD_TN = 1536


def _mod_kernel(cond_ref, w_ref, b_ref, o_ref):
    c = cond_ref[...]
    s = c * jax.nn.sigmoid(c)
    o_ref[...] = _bdot(s, w_ref[...]) + b_ref[...]


def modulation(cond, w_mod, b_mod):
    return pl.pallas_call(
        _mod_kernel,
        grid=(DEPTH, MOD_WIDTH // MOD_TN),
        in_specs=[
            pl.BlockSpec((N_COND, D_MODEL), lambda l, j: (0, 0)),
            pl.BlockSpec((None, D_MODEL, MOD_TN), lambda l, j: (l, 0, j)),
            pl.BlockSpec((None, 1, MOD_TN), lambda l, j: (l, 0, j)),
        ],
        out_specs=pl.BlockSpec((None, N_COND, MOD_TN), lambda l, j: (l, 0, j)),
        out_shape=jax.ShapeDtypeStruct((DEPTH, N_COND, MOD_WIDTH), F32),
        compiler_params=_params(2),
        name="modulation",
    )(cond, w_mod, b_mod.reshape(DEPTH, 1, MOD_WIDTH))


IN_TM = 256
IN_CTX_TILES = N_CTX // IN_TM
ZA_WIDTH = 4 * 512
KV_OFF = ZA_WIDTH + NA_WIDTH
GATE_OFF = KV_OFF + 2 * NA_WIDTH
W_CHUNK = 512
Q_SCALE = NA_HEAD_DIM ** -0.5 * float(np.log2(np.e))


def _load_cast(chunks, stage_ref, sem_ref):
    def copy(j):
        return pltpu.make_async_copy(chunks[j][0], stage_ref.at[j % 2], sem_ref.at[j % 2])

    copy(0).start()
    for j in range(len(chunks)):
        if j + 1 < len(chunks):
            copy(j + 1).start()
        copy(j).wait()
        chunks[j][1](stage_ref[j % 2].astype(BF16))


def _token_tile(i, n_ctx_tiles, refs):
    if len(refs) == 1:
        return refs[0][...]
    return jnp.where(i < n_ctx_tiles, refs[0][...], refs[1][...])


def _split_specs(block, n_ctx_tiles):
    return [pl.BlockSpec(block, lambda i, *_: (jnp.minimum(i, n_ctx_tiles - 1), 0)),
            pl.BlockSpec(block, lambda i, *_: (jnp.maximum(i - n_ctx_tiles, 0), 0))]


SGU_GD = SGU_WIDTH // SGU_GROUPS


def _spatial_gating(u, v, gain, ws_ref, bs_ref):
    u = jax.nn.gelu(u)
    v = _rms(jax.nn.gelu(v), gain).astype(BF16)
    out = []
    for g in range(SGU_GROUPS):
        cols = slice(g * SGU_GD, (g + 1) * SGU_GD)
        mixed = jnp.dot(ws_ref[g].astype(BF16), v[:, cols], preferred_element_type=F32) + bs_ref[:, g:g + 1]
        out.append(u[:, cols] * mixed)
    return jnp.concatenate(out, axis=1)


def _inproj_kernel(layer, n_x, *refs):
    x_refs, (mod_ref, gain_ref, w_hbm, sgu_gain_ref, ws_ref, bs_ref) = refs[:n_x], refs[n_x:n_x + 6]
    ya_ref, zb_ref, q_ref, kb_ref, vb_ref, kc_ref, vc_ref, g_ref, w_ref, stage_ref, sem_ref = refs[n_x + 6:]
    i = pl.program_id(0)

    @pl.when(i == 0)
    def _():
        def store(c):
            def st(v):
                w_ref[:, c * W_CHUNK:(c + 1) * W_CHUNK] = v
            return st

        _load_cast([(w_hbm.at[layer, :, pl.ds(c * W_CHUNK, W_CHUNK)], store(c)) for c in range(IN_WIDTH // W_CHUNK)],
                   stage_ref, sem_ref)

    m = mod_ref[...]
    shift, scale = m[:, 0:D_MODEL], m[:, D_MODEL:2 * D_MODEL]
    h = (_rms(_token_tile(i, IN_CTX_TILES, x_refs), gain_ref[...]) * (1.0 + scale) + shift).astype(BF16)
    za = jnp.dot(h, w_ref[:, 0:ZA_WIDTH], preferred_element_type=F32)
    for c in range(IN_TM // CHUNK):
        rows = slice(c * CHUNK, (c + 1) * CHUNK)
        ya_ref[rows, :] = _spatial_gating(za[rows, 0:SGU_WIDTH], za[rows, SGU_WIDTH:2 * SGU_WIDTH],
                                          sgu_gain_ref[...], ws_ref, bs_ref).astype(BF16)
    zb_ref[...] = za[:, 2 * SGU_WIDTH:].astype(BF16)
    q_ref[...] = (jnp.dot(h, w_ref[:, ZA_WIDTH:KV_OFF], preferred_element_type=F32) * Q_SCALE).astype(BF16)
    kv = jnp.dot(h, w_ref[:, KV_OFF:GATE_OFF], preferred_element_type=F32)
    kb_ref[...] = kv[:, 0:NA_WIDTH].astype(BF16)
    vb_ref[...] = kv[:, NA_WIDTH:].astype(BF16)
    g_ref[...] = jnp.dot(h, w_ref[:, GATE_OFF:], preferred_element_type=F32).astype(BF16)

    @pl.when(i < IN_CTX_TILES)
    def _():
        kc_ref[...] = kv[:, 0:NA_WIDTH]
        vc_ref[...] = kv[:, NA_WIDTH:]


def in_projection(xs, mod3, gain, w_in, sgu_gain, sgu_w, sgu_b_t, layer):
    row = lambda i: (i, 0)
    ctx_row = lambda i: (jnp.minimum(i, IN_CTX_TILES - 1), 0)
    const = lambda shape: pl.BlockSpec(shape, lambda i: (0,) * len(shape))
    x_specs = ([pl.BlockSpec((IN_TM, D_MODEL), row)] if len(xs) == 1
               else _split_specs((IN_TM, D_MODEL), IN_CTX_TILES))
    return pl.pallas_call(
        functools.partial(_inproj_kernel, layer, len(xs)),
        grid=(N_TOK // IN_TM,),
        in_specs=x_specs + [
            pl.BlockSpec((None, 1, MOD_WIDTH), lambda i: (_cond_row(i, IN_TM), 0, 0)),
            const((1, D_MODEL)),
            pl.BlockSpec(memory_space=pl.ANY),
            const((1, SGU_WIDTH)),
            const((SGU_GROUPS, CHUNK, CHUNK)),
            const((CHUNK, SGU_GROUPS)),
        ],
        out_specs=[
            pl.BlockSpec((IN_TM, SGU_WIDTH), row),
            pl.BlockSpec((IN_TM, 2 * LRU_WIDTH), row),
            pl.BlockSpec((IN_TM, NA_WIDTH), row),
            pl.BlockSpec((IN_TM, NA_WIDTH), row),
            pl.BlockSpec((IN_TM, NA_WIDTH), row),
            pl.BlockSpec((IN_TM, NA_WIDTH), ctx_row),
            pl.BlockSpec((IN_TM, NA_WIDTH), ctx_row),
            pl.BlockSpec((IN_TM, 3 * D_MODEL), row),
        ],
        out_shape=[
            jax.ShapeDtypeStruct((N_TOK, SGU_WIDTH), BF16),
            jax.ShapeDtypeStruct((N_TOK, 2 * LRU_WIDTH), BF16),
            jax.ShapeDtypeStruct((N_TOK, NA_WIDTH), BF16),
            jax.ShapeDtypeStruct((N_TOK, NA_WIDTH), BF16),
            jax.ShapeDtypeStruct((N_TOK, NA_WIDTH), BF16),
            jax.ShapeDtypeStruct((N_CTX, NA_WIDTH), F32),
            jax.ShapeDtypeStruct((N_CTX, NA_WIDTH), F32),
            jax.ShapeDtypeStruct((N_TOK, 3 * D_MODEL), BF16),
        ],
        scratch_shapes=[pltpu.VMEM((D_MODEL, IN_WIDTH), BF16), pltpu.VMEM((2, D_MODEL, W_CHUNK), F32),
                        pltpu.SemaphoreType.DMA((2,))],
        compiler_params=_params(1),
        name="in_projection",
    )(*xs, mod3, gain, w_in, sgu_gain, sgu_w, sgu_b_t)


LRU_TC = 256
LRU_HALF = 256
SUB = 8
HALO = 8
LANES = 128
LRU_SLABS = LRU_WIDTH // LANES


def _sigmoid(x):
    return 0.5 * jnp.tanh(0.5 * x) + 0.5


def _lru_kernel(seq_len, xr_ref, gr_ref, cw_ref, cb_ref, wlo_ref, whi_ref, br_ref, bi_ref,
                lam_ref, h0_ref, y_ref, st_ref, xp_ref, a_ref, b_ref):
    seg_len = seq_len // SUB
    pitch = seg_len + SUB
    segs_per_chunk = max(LRU_TC // seg_len, 1)
    n_chunks = seq_len // LRU_TC
    zeros = jnp.zeros((HALO, LRU_WIDTH), F32)
    xp_ref[0:HALO, :] = zeros
    xp_ref[seq_len + HALO:seq_len + 2 * HALO, :] = zeros

    def copy_in(c, carry):
        r0 = pl.multiple_of(c * LRU_TC, LRU_TC)
        xp_ref[pl.ds(r0 + HALO, LRU_TC), :] = xr_ref[pl.ds(r0, LRU_TC), :].astype(F32)
        return carry

    lax.fori_loop(0, n_chunks, copy_in, 0)

    def chunk_rows(c, seg):
        n = min(seg_len, LRU_TC)
        start = pl.multiple_of((c * segs_per_chunk + seg) * pitch, SUB)
        return pl.ds(start, n), slice(seg * n, (seg + 1) * n)

    cw = cw_ref[...]
    cb = cb_ref[...]
    win = LRU_TC + 2 * HALO
    neg_lam = -lam_ref[...]
    softplus = jnp.maximum(neg_lam, 0.0) + jnp.log1p(jnp.exp(-jnp.abs(neg_lam)))
    decay = (0.5 * LRU_C) * softplus
    half_br = 0.5 * br_ref[...]
    half_bi = 0.5 * bi_ref[...]

    def gates(c, carry):
        r0 = pl.multiple_of(c * LRU_TC, LRU_TC)
        w = xp_ref[pl.ds(r0, win), :]
        xc = (cw[0:1, :] * pltpu.roll(w, 1, 0)[HALO:HALO + LRU_TC]
              + cw[1:2, :] * w[HALO:HALO + LRU_TC]
              + cw[2:3, :] * pltpu.roll(w, win - 1, 0)[HALO:HALO + LRU_TC]
              + cw[3:4, :] * pltpu.roll(w, win - 2, 0)[HALO:HALO + LRU_TC]) + cb
        xb = xc.astype(BF16)
        for half, w_ref in enumerate((wlo_ref, whi_ref)):
            cols = slice(half * LRU_HALF, (half + 1) * LRU_HALF)
            pre = jnp.dot(xb[:, cols], w_ref[...], preferred_element_type=F32)
            half_x = 0.5 * xc[:, cols]
            for d in range(2):
                tr = jnp.tanh(0.5 * pre[:, (2 * d) * LRU_HALF:(2 * d + 1) * LRU_HALF] + half_br[d:d + 1, cols])
                ti = jnp.tanh(0.5 * pre[:, (2 * d + 1) * LRU_HALF:(2 * d + 2) * LRU_HALF] + half_bi[d:d + 1, cols])
                neg_log_a = decay[d:d + 1, cols] * tr + decay[d:d + 1, cols]
                a = jnp.exp2(neg_log_a * -LOG2E)
                z = jnp.tanh(neg_log_a) * (a * a + 1.0)
                b = jnp.where(z > 0.0, z * lax.rsqrt(z), 0.0) * (half_x * ti + half_x)
                for k in range(LRU_HALF // LANES):
                    slab = half * (LRU_HALF // LANES) + k
                    lanes = slice(k * LANES, (k + 1) * LANES)
                    for seg in range(segs_per_chunk):
                        dst, src = chunk_rows(c, seg)
                        a_ref[d, slab, dst, :] = a[src, lanes]
                        b_ref[d, slab, dst, :] = b[src, lanes]
        return carry

    lax.fori_loop(0, n_chunks, gates, 0)

    def step_rows(d, j):
        return pl.ds(j if d == 0 else seg_len - 1 - j, SUB, stride=pitch)

    def reduce_step(j, carry):
        out = []
        for d in range(2):
            for slab in range(LRU_SLABS):
                big_a, big_b = carry[d * LRU_SLABS + slab]
                a = a_ref[d, slab, step_rows(d, j), :]
                b = b_ref[d, slab, step_rows(d, j), :]
                out.append((a * big_a, a * big_b + b))
        return tuple(out)

    ident = (jnp.ones((SUB, LANES), F32), jnp.zeros((SUB, LANES), F32))
    totals = lax.fori_loop(0, seg_len, reduce_step, (ident,) * (2 * LRU_SLABS), unroll=2)

    sub = lax.broadcasted_iota(jnp.int32, (SUB, LANES), 0)
    entering = []
    for d in range(2):
        for slab in range(LRU_SLABS):
            big_a, big_b = totals[d * LRU_SLABS + slab]
            h = h0_ref[d:d + 1, slab * LANES:(slab + 1) * LANES]
            rows = jnp.zeros((SUB, LANES), F32)
            for s in (range(SUB) if d == 0 else reversed(range(SUB))):
                rows = jnp.where(sub == s, h, rows)
                h = big_a[s:s + 1, :] * h + big_b[s:s + 1, :]
            entering.append(rows)
            st_ref[d:d + 1, slab * LANES:(slab + 1) * LANES] = h

    def apply_step(j, carry):
        out = []
        for d in range(2):
            for slab in range(LRU_SLABS):
                h = (a_ref[d, slab, step_rows(d, j), :] * carry[d * LRU_SLABS + slab]
                     + b_ref[d, slab, step_rows(d, j), :])
                b_ref[d, slab, step_rows(d, j), :] = h
                out.append(h)
        return tuple(out)

    lax.fori_loop(0, seg_len, apply_step, tuple(entering), unroll=2)

    def merge(c, carry):
        r0 = pl.multiple_of(c * LRU_TC, LRU_TC)
        gate = jax.nn.gelu(gr_ref[pl.ds(r0, LRU_TC), :].astype(F32))
        for slab in range(LRU_SLABS):
            lanes = slice(slab * LANES, (slab + 1) * LANES)
            for seg in range(segs_per_chunk):
                src, dst = chunk_rows(c, seg)
                h = b_ref[0, slab, src, :] + b_ref[1, slab, src, :]
                y_ref[pl.ds(r0 + dst.start, dst.stop - dst.start), lanes] = (h * gate[dst, lanes]).astype(BF16)
        return carry

    lax.fori_loop(0, n_chunks, merge, 0)


def rglru(za, conv_w, conv_b, w_lo, w_hi, b_r, b_i, lam, h0, seq_len, n_seq, tok_off):
    blk0 = tok_off // seq_len
    full = lambda shape: pl.BlockSpec(shape, lambda i: (0,) * len(shape))
    return pl.pallas_call(
        functools.partial(_lru_kernel, seq_len),
        grid=(n_seq,),
        in_specs=[
            pl.BlockSpec((seq_len, LRU_WIDTH), lambda i: (i + blk0, 0)),
            pl.BlockSpec((seq_len, LRU_WIDTH), lambda i: (i + blk0, 1)),
            full((CONV_WIDTH, LRU_WIDTH)),
            full((1, LRU_WIDTH)),
            full((LRU_HALF, 4 * LRU_HALF)),
            full((LRU_HALF, 4 * LRU_HALF)),
            full((2, LRU_WIDTH)),
            full((2, LRU_WIDTH)),
            full((2, LRU_WIDTH)),
            pl.BlockSpec((None, 2, LRU_WIDTH), lambda i: (i, 0, 0)),
        ],
        out_specs=[
            pl.BlockSpec((seq_len, LRU_WIDTH), lambda i: (i, 0)),
            pl.BlockSpec((None, 2, LRU_WIDTH), lambda i: (i, 0, 0)),
        ],
        out_shape=[
            jax.ShapeDtypeStruct((n_seq * seq_len, LRU_WIDTH), BF16),
            jax.ShapeDtypeStruct((n_seq, 2, LRU_WIDTH), F32),
        ],
        scratch_shapes=[
            pltpu.VMEM((seq_len + 2 * HALO, LRU_WIDTH), F32),
            pltpu.VMEM((2, LRU_SLABS, seq_len + SUB * SUB, LANES), F32),
            pltpu.VMEM((2, LRU_SLABS, seq_len + SUB * SUB, LANES), F32),
        ],
        compiler_params=_params(1),
        name=f"rglru_{seq_len}",
    )(za, za, conv_w, conv_b, w_lo, w_hi, b_r, b_i, lam, h0)


def _block_diag_gate_weights(w_r, w_i):
    per_half = LRU_BLOCKS // 2
    eye = jnp.eye(per_half, dtype=w_r.dtype)

    def bd(w):
        return jnp.einsum("hij,hg->higj", w, eye).reshape(LRU_HALF, LRU_HALF)

    tiles = []
    for half in range(2):
        sl = slice(half * per_half, (half + 1) * per_half)
        tiles.append(jnp.concatenate([bd(w_r[0, sl]), bd(w_i[0, sl]), bd(w_r[1, sl]), bd(w_i[1, sl])], axis=1))
    return tiles[0].astype(BF16), tiles[1].astype(BF16)


PAIR_W = 2 * NA_HEAD_DIM
LOG2E = float(np.log2(np.e))


def _head_lanes(shape, head):
    lane = lax.broadcasted_iota(jnp.int32, shape, 1)
    return (lane < NA_HEAD_DIM) if head == 0 else (lane >= NA_HEAD_DIM)


def _one_head(q_pair, head):
    return jnp.where(_head_lanes(q_pair.shape, head), q_pair.astype(F32), 0.0).astype(BF16)


def _join_heads(o0, o1):
    return jnp.where(_head_lanes(o0.shape, 0), o0, o1)


def _ctx_attn_kernel(q_ref, k_ref, v_ref, o_ref):
    for j in range(NA_HEADS // 2):
        cols = slice(j * PAIR_W, (j + 1) * PAIR_W)
        outs = []
        for head in range(2):
            s = _bdot_t(_one_head(q_ref[:, cols], head), k_ref[:, cols])
            p = jnp.exp2(s - jnp.max(s, axis=-1, keepdims=True))
            denom = jnp.sum(p, axis=-1, keepdims=True)
            outs.append(jnp.dot(p.astype(BF16), v_ref[:, cols], preferred_element_type=F32) / denom)
        o_ref[:, cols] = _join_heads(*outs).astype(BF16)


def context_attention(q, kb, vb):
    spec = pl.BlockSpec((SEQ, NA_WIDTH), lambda i: (i, 0))
    return pl.pallas_call(
        _ctx_attn_kernel,
        grid=(BATCH,),
        in_specs=[spec, spec, spec],
        out_specs=spec,
        out_shape=jax.ShapeDtypeStruct((N_CTX, NA_WIDTH), BF16),
        compiler_params=_params(1),
        name="context_attention",
    )(q, kb, vb)


NA_QROWS = 4
NA_TQ = NA_QROWS * GRID_W
NA_KROWS = NA_QROWS + WIN_ROWS
NA_TK = NA_KROWS * GRID_W
NA_QBLOCKS = GRID_ROWS // NA_QROWS


N_DROW = 2 * WIN_ROWS - 1
N_DCOL = 2 * WIN_COLS - 1
NA_BLOCK_KINDS = (0, 1, NA_QBLOCKS - 1)


def _first_key_row(qb):
    return (np.clip if isinstance(qb, int) else jnp.clip)(qb * NA_QROWS - WIN_ROWS // 2, 0, GRID_ROWS - NA_KROWS)


def _build_bias(rpb_ref, table_ref, bias_ref):
    qc = lax.broadcasted_iota(jnp.int32, (GRID_W, GRID_W), 0)
    kc = lax.broadcasted_iota(jnp.int32, (GRID_W, GRID_W), 1)
    col0 = jnp.clip(qc - WIN_COLS // 2, 0, GRID_W - WIN_COLS)
    col_ok = jnp.logical_and(kc >= col0, kc < col0 + WIN_COLS)
    d_col = jnp.clip(kc - qc, 1 - WIN_COLS, WIN_COLS - 1) + WIN_COLS - 1
    neg = jnp.full((GRID_W, GRID_W), NEG_INF, F32)

    def table_entry(idx, carry):
        t = neg
        for j in range(N_DCOL):
            t = jnp.where(d_col == j, rpb_ref[idx * N_DCOL + j], t)
        table_ref[idx] = jnp.where(col_ok, t * LOG2E, neg)
        return carry

    lax.fori_loop(0, NA_HEADS * N_DROW, table_entry, 0)

    def head_blocks(h, carry):
        for kind, qb in enumerate(NA_BLOCK_KINDS):
            for i in range(NA_QROWS):
                qrow = qb * NA_QROWS + i
                win0 = int(np.clip(qrow - WIN_ROWS // 2, 0, GRID_ROWS - WIN_ROWS))
                for kr in range(NA_KROWS):
                    krow = int(_first_key_row(qb)) + kr
                    inside = win0 <= krow < win0 + WIN_ROWS
                    blk = table_ref[h * N_DROW + (krow - qrow + WIN_ROWS - 1)] if inside else neg
                    bias_ref[kind, h, i * GRID_W:(i + 1) * GRID_W, kr * GRID_W:(kr + 1) * GRID_W] = blk
        return carry

    lax.fori_loop(0, NA_HEADS, head_blocks, 0)


def _lat_attn_kernel(rpb_ref, q_ref, k_ref, v_ref, ck_ref, cv_ref, o_ref, table_ref, bias_ref):
    qb = pl.program_id(1)

    @pl.when(jnp.logical_and(pl.program_id(0) == 0, qb == 0))
    def _():
        _build_bias(rpb_ref, table_ref, bias_ref)

    kind = jnp.where(qb == 0, 0, jnp.where(qb == NA_QBLOCKS - 1, 2, 1))
    k0 = pl.multiple_of(_first_key_row(qb) * GRID_W, GRID_W)
    for j in range(NA_HEADS // 2):
        cols = slice(j * PAIR_W, (j + 1) * PAIR_W)
        k_loc = k_ref[pl.ds(k0, NA_TK), cols]
        v_loc = v_ref[pl.ds(k0, NA_TK), cols]
        k_ctx = ck_ref[:, cols].astype(BF16)
        v_ctx = cv_ref[:, cols].astype(BF16)
        outs = []
        for head in range(2):
            qh = _one_head(q_ref[:, cols], head)
            s_loc = _bdot_t(qh, k_loc) + bias_ref[kind, 2 * j + head]
            s_ctx = _bdot_t(qh, k_ctx)
            m = jnp.maximum(jnp.max(s_loc, axis=-1, keepdims=True), jnp.max(s_ctx, axis=-1, keepdims=True))
            p_loc = jnp.exp2(s_loc - m)
            p_ctx = jnp.exp2(s_ctx - m)
            denom = jnp.sum(p_loc, axis=-1, keepdims=True) + jnp.sum(p_ctx, axis=-1, keepdims=True)
            o = (jnp.dot(p_loc.astype(BF16), v_loc, preferred_element_type=F32)
                 + jnp.dot(p_ctx.astype(BF16), v_ctx, preferred_element_type=F32))
            outs.append(o / denom)
        o_ref[:, cols] = _join_heads(*outs).astype(BF16)


def latent_attention(q, kb, vb, cache_k, cache_v, rpb, layer):
    q_blk0 = N_CTX // NA_TQ
    s_blk0 = N_CTX // DEC_SEQ
    cache_spec = pl.BlockSpec((None, None, PAST_LEN, NA_WIDTH), lambda b, m: (b, layer, 0, 0))
    return pl.pallas_call(
        _lat_attn_kernel,
        grid=(DEC_BATCH, NA_QBLOCKS),
        in_specs=[
            pl.BlockSpec(memory_space=pltpu.SMEM),
            pl.BlockSpec((NA_TQ, NA_WIDTH), lambda b, m: (q_blk0 + b * NA_QBLOCKS + m, 0)),
            pl.BlockSpec((DEC_SEQ, NA_WIDTH), lambda b, m: (s_blk0 + b, 0)),
            pl.BlockSpec((DEC_SEQ, NA_WIDTH), lambda b, m: (s_blk0 + b, 0)),
            cache_spec,
            cache_spec,
        ],
        out_specs=pl.BlockSpec((NA_TQ, NA_WIDTH), lambda b, m: (b * NA_QBLOCKS + m, 0)),
        out_shape=jax.ShapeDtypeStruct((N_LAT, NA_WIDTH), BF16),
        scratch_shapes=[pltpu.VMEM((NA_HEADS * N_DROW, GRID_W, GRID_W), F32),
                        pltpu.VMEM((len(NA_BLOCK_KINDS), NA_HEADS, NA_TQ, NA_TK), F32)],
        compiler_params=_params(2),
        name="latent_attention",
    )(rpb, q, kb, vb, cache_k, cache_v)


MERGE_TM = 512


MERGE_CTX_TILES = N_CTX // MERGE_TM
BRANCH_W = 512


def _merge_kernel(layer, n_x, *refs):
    x_refs, rest = refs[:n_x], refs[n_x:]
    (mod_ref, ya_ref, yb_ctx_ref, yb_lat_ref, yc_ctx_ref, yc_lat_ref, g_ref, wa_hbm, wb_hbm, wc_hbm, wo_hbm,
     ffn_gain_ref, wrt_ref, brt_ref, tri_ref,
     o_ref, hx_ref, bucket_ref, rank_ref, cnt_out_ref,
     wbr_ref, wo_ref, stage_ref, sem_ref, cnt_ref, cpad_ref) = rest
    i = pl.program_id(0)

    @pl.when(i == 0)
    def _():
        def store_branch(k):
            def st(v):
                wbr_ref[k] = v
            return st

        def store_out(k):
            def st(v):
                wo_ref[k * BRANCH_W:(k + 1) * BRANCH_W, :] = v
            return st

        chunks = [(w.at[layer], store_branch(k)) for k, w in enumerate((wa_hbm, wb_hbm, wc_hbm))]
        chunks += [(wo_hbm.at[layer, pl.ds(k * BRANCH_W, BRANCH_W), :], store_out(k))
                   for k in range(D_MODEL // BRANCH_W)]
        _load_cast(chunks, stage_ref, sem_ref)

    g = _sigmoid(g_ref[...].astype(F32))
    yb = _token_tile(i, MERGE_CTX_TILES, (yb_ctx_ref, yb_lat_ref))
    yc = _token_tile(i, MERGE_CTX_TILES, (yc_ctx_ref, yc_lat_ref))
    merged = (g[:, 0:D_MODEL] * jnp.dot(ya_ref[...], wbr_ref[0], preferred_element_type=F32)
              + g[:, D_MODEL:2 * D_MODEL] * jnp.dot(yb, wbr_ref[1], preferred_element_type=F32)
              + g[:, 2 * D_MODEL:] * jnp.dot(yc, wbr_ref[2], preferred_element_type=F32))
    y = jnp.dot(merged.astype(BF16), wo_ref[...], preferred_element_type=F32)
    gate = mod_ref[:, 2 * D_MODEL:3 * D_MODEL]
    x = _token_tile(i, MERGE_CTX_TILES, x_refs) + gate * y
    o_ref[...] = x
    _route(x, mod_ref, ffn_gain_ref, wrt_ref, brt_ref, tri_ref, hx_ref, bucket_ref, rank_ref, cnt_out_ref,
           cnt_ref, cpad_ref)


def merge_branches(xs, mod3, ya, yb_ctx, yb_lat, yc_ctx, yc_lat, gates, wa, wb, wc, wo, ffn_gain, w_rt, b_rt, layer):
    assert MERGE_TM == RT_TM
    row = lambda i: (i, 0)
    const = lambda shape: pl.BlockSpec(shape, lambda i: (0,) * len(shape))
    tri = jnp.asarray(np.triu(np.ones((RT_TM, RT_TM), np.float32)), BF16)
    hbm = pl.BlockSpec(memory_space=pl.ANY)
    x_specs = ([pl.BlockSpec((MERGE_TM, D_MODEL), row)] if len(xs) == 1
               else _split_specs((MERGE_TM, D_MODEL), MERGE_CTX_TILES))
    return pl.pallas_call(
        functools.partial(_merge_kernel, layer, len(xs)),
        grid=(N_TOK // MERGE_TM,),
        in_specs=x_specs + [
            pl.BlockSpec((None, 1, MOD_WIDTH), lambda i: (_cond_row(i, MERGE_TM), 0, 0)),
            pl.BlockSpec((MERGE_TM, SGU_WIDTH), row),
            *_split_specs((MERGE_TM, LRU_WIDTH), MERGE_CTX_TILES),
            *_split_specs((MERGE_TM, NA_WIDTH), MERGE_CTX_TILES),
            pl.BlockSpec((MERGE_TM, 3 * D_MODEL), row),
            hbm, hbm, hbm, hbm,
            const((1, D_MODEL)),
            const((RT_ROWS, D_MODEL)),
            const((RT_ROWS, 1)),
            const((RT_TM, RT_TM)),
        ],
        out_specs=[
            pl.BlockSpec((MERGE_TM, D_MODEL), row),
            pl.BlockSpec((RT_TM, HX_W), row),
            pl.BlockSpec((1, RT_TM), lambda i: (0, i)),
            pl.BlockSpec((1, RT_TM), lambda i: (0, i)),
            const((CNT_ROWS, COMB_W)),
        ],
        out_shape=[
            jax.ShapeDtypeStruct((N_TOK, D_MODEL), F32),
            jax.ShapeDtypeStruct((N_TOK, HX_W), F32),
            jax.ShapeDtypeStruct((1, N_TOK), jnp.int32),
            jax.ShapeDtypeStruct((1, N_TOK), jnp.int32),
            jax.ShapeDtypeStruct((CNT_ROWS, COMB_W), jnp.int32),
        ],
        scratch_shapes=[pltpu.VMEM((3, BRANCH_W, D_MODEL), BF16), pltpu.VMEM((D_MODEL, D_MODEL), BF16),
                        pltpu.VMEM((2, BRANCH_W, D_MODEL), F32), pltpu.SemaphoreType.DMA((2,)),
                        pltpu.VMEM((CNT_ROWS, COMB_W), F32), pltpu.VMEM((COMB_W, RT_TM), F32)],
        compiler_params=_params(1),
        name="merge_branches",
    )(*xs, mod3, ya, yb_ctx, yb_lat, yc_ctx, yc_lat, gates, wa, wb, wc, wo, ffn_gain, w_rt, b_rt, tri)


RT_TM = 512
RT_ROWS = 32
EXPERT_PAIRS = ((0, 1), (0, 2), (0, 3), (1, 2), (1, 3), (2, 3))
N_PAIRS = len(EXPERT_PAIRS)
N_BUCKETS = N_GROUPS * N_PAIRS
CNT_ROWS = 32
COMB_W = 128
HX_W = D_MODEL + COMB_W
EXP_TM = 256
EXP_TILES = N_TOK // EXP_TM + N_BUCKETS
HS_ROWS = EXP_TILES * EXP_TM


def _split_bf16(x):
    hi = x.astype(BF16)
    return hi, (x - hi.astype(F32)).astype(BF16)


def _route(x, mod_ref, gain_ref, w_ref, b_ref, tri_ref, hx_ref, bucket_ref, rank_ref, cnt_out_ref,
           cnt_ref, cpad_ref):
    @pl.when(pl.program_id(0) == 0)
    def _():
        cnt_ref[...] = jnp.zeros_like(cnt_ref)
        cpad_ref[...] = jnp.zeros_like(cpad_ref)

    m = mod_ref[...]
    shift, scale = m[:, 3 * D_MODEL:4 * D_MODEL], m[:, 4 * D_MODEL:5 * D_MODEL]
    h = _rms(x, gain_ref[...]) * (1.0 + scale) + shift
    hx_ref[:, 0:D_MODEL] = h
    h_hi, h_lo = _split_bf16(h)
    w_hi, w_lo = _split_bf16(w_ref[...])
    dims = (((1,), (1,)), ((), ()))
    logits = (lax.dot_general(w_hi, h_hi, dims, preferred_element_type=F32)
              + lax.dot_general(w_hi, h_lo, dims, preferred_element_type=F32)
              + lax.dot_general(w_lo, h_hi, dims, preferred_element_type=F32)) + b_ref[...]
    gl = [logits[g:g + 1, :] for g in range(N_GROUPS)]
    gmax = functools.reduce(jnp.maximum, gl)
    gid = jnp.full(gmax.shape, N_GROUPS - 1, jnp.int32)
    for g in reversed(range(N_GROUPS - 1)):
        gid = jnp.where(gl[g] == gmax, g, gid)
    p_grp = 1.0 / functools.reduce(jnp.add, [jnp.exp(v - gmax) for v in gl])
    el = []
    for e in range(EXPERTS_PER_GROUP):
        v = logits[8 + e:9 + e, :]
        for g in range(1, N_GROUPS):
            row = 8 + g * EXPERTS_PER_GROUP + e
            v = jnp.where(gid == g, logits[row:row + 1, :], v)
        el.append(v)
    top1 = functools.reduce(jnp.maximum, el)
    idx1 = jnp.full(top1.shape, EXPERTS_PER_GROUP - 1, jnp.int32)
    for e in reversed(range(EXPERTS_PER_GROUP - 1)):
        idx1 = jnp.where(el[e] == top1, e, idx1)
    rest = [jnp.where(idx1 == e, -jnp.inf, el[e]) for e in range(EXPERTS_PER_GROUP)]
    top2 = functools.reduce(jnp.maximum, rest)
    idx2 = jnp.full(top1.shape, EXPERTS_PER_GROUP - 1, jnp.int32)
    for e in reversed(range(EXPERTS_PER_GROUP - 1)):
        idx2 = jnp.where(rest[e] == top2, e, idx2)
    e2 = jnp.exp(top2 - top1)
    w1 = p_grp / (1.0 + e2)
    w2 = p_grp * e2 / (1.0 + e2)
    for e in range(EXPERTS_PER_GROUP):
        cpad_ref[e:e + 1, :] = jnp.where(idx1 == e, w1, 0.0) + jnp.where(idx2 == e, w2, 0.0)
    hx_ref[:, D_MODEL:] = cpad_ref[...].T
    lo = jnp.minimum(idx1, idx2)
    hi = jnp.maximum(idx1, idx2)
    pair = jnp.where(lo == 0, 0, jnp.where(lo == 1, 3, 5)) + (hi - lo - 1)
    bucket = gid * N_PAIRS + pair
    bucket_ref[...] = bucket
    sub = lax.broadcasted_iota(jnp.int32, (CNT_ROWS, RT_TM), 0)
    onehot = jnp.where(sub == bucket, 1.0, 0.0)
    seen = jnp.dot(onehot.astype(BF16), tri_ref[...], preferred_element_type=F32)
    cnt = cnt_ref[...]
    rank_ref[...] = jnp.sum(onehot * (seen - 1.0 + cnt[:, 0:1]), axis=0, keepdims=True).astype(jnp.int32)
    cnt = cnt + jnp.sum(onehot, axis=1, keepdims=True)
    cnt_ref[...] = cnt
    cnt_out_ref[...] = cnt.astype(jnp.int32)


def _router_weights(w_grp, b_grp, w_exp, b_exp):
    w = jnp.zeros((RT_ROWS, D_MODEL), F32).at[0:N_GROUPS].set(w_grp.T).at[8:8 + N_EXPERTS].set(w_exp.T)
    b = jnp.zeros((RT_ROWS, 1), F32).at[0:N_GROUPS, 0].set(b_grp).at[8:8 + N_EXPERTS, 0].set(b_exp)
    return w, b


DISP_TM = 512


ROW_GROUP = 64


def _for_each_row(n_rows, fn):
    def group(k, carry):
        g0 = pl.multiple_of(k * ROW_GROUP, ROW_GROUP)
        for u in range(ROW_GROUP):
            fn(g0, u)
        return carry

    lax.fori_loop(0, n_rows // ROW_GROUP, group, 0)


def _dispatch_kernel(pos_ref, start_ref, cnt_ref, nt_ref, hx_hbm, hs_ref, buf_ref, zero_ref, blk_sem, row_sem, zsem):
    i = pl.program_id(0)
    last = pl.num_programs(0) - 1
    slot = i % 2
    base = i * DISP_TM

    def tile_in(tile, s):
        return pltpu.make_async_copy(hx_hbm.at[pl.ds(pl.multiple_of(tile * DISP_TM, DISP_TM), DISP_TM), :],
                                     buf_ref.at[s], blk_sem.at[s])

    def wait_rows(s):
        pltpu.make_async_copy(buf_ref.at[s], hs_ref.at[pl.ds(0, DISP_TM), :], row_sem.at[s]).wait()

    @pl.when(i == 0)
    def _():
        zero_ref[...] = jnp.zeros_like(zero_ref)

        def tile_copy(t):
            return pltpu.make_async_copy(zero_ref, hs_ref.at[pl.ds(pl.multiple_of(t * EXP_TM, EXP_TM), EXP_TM), :],
                                         zsem)

        def last_tile(g):
            return (start_ref[g] + cnt_ref[g] - 1) // EXP_TM

        def each_tile(fn):
            for g in range(N_BUCKETS):
                pl.when(cnt_ref[g] > 0)(functools.partial(fn, last_tile(g)))
            lax.fori_loop(nt_ref[0], EXP_TILES, lambda t, carry: (fn(t), carry)[1], 0)

        each_tile(lambda t: tile_copy(t).start())
        each_tile(lambda t: tile_copy(t).wait())
        tile_in(0, 0).start()

    @pl.when(i > 0)
    def _():
        wait_rows(1 - slot)

    @pl.when(i < last)
    def _():
        tile_in(i + 1, 1 - slot).start()

    tile_in(i, slot).wait()

    def issue(g0, u):
        src = buf_ref.at[slot, pl.ds(g0, ROW_GROUP), :]
        pltpu.make_async_copy(src.at[pl.ds(u, 1), :], hs_ref.at[pl.ds(pos_ref[base + g0 + u], 1), :],
                              row_sem.at[slot]).start()

    _for_each_row(DISP_TM, issue)

    @pl.when(i == last)
    def _():
        wait_rows(slot)


def dispatch(pos, starts, counts, n_tiles, hx):
    return pl.pallas_call(
        _dispatch_kernel,
        grid_spec=pltpu.PrefetchScalarGridSpec(
            num_scalar_prefetch=4,
            grid=(N_TOK // DISP_TM,),
            in_specs=[pl.BlockSpec(memory_space=pl.ANY)],
            out_specs=pl.BlockSpec(memory_space=pl.ANY),
            scratch_shapes=[pltpu.VMEM((2, DISP_TM, HX_W), F32), pltpu.VMEM((EXP_TM, HX_W), F32),
                            pltpu.SemaphoreType.DMA((2,)), pltpu.SemaphoreType.DMA((2,)),
                            pltpu.SemaphoreType.DMA(())],
        ),
        out_shape=jax.ShapeDtypeStruct((HS_ROWS, HX_W), F32),
        compiler_params=_params(1),
        name="dispatch",
    )(pos, starts, counts, n_tiles, hx)


(T_GROUP, T_LO, T_HI, T_SLOT, T_RUN_POS, T_NEXT_GROUP, T_FIRST, T_PREFETCHED) = range(8)


def _experts_kernel(layer, tab_ref, nt_ref, hs_ref, w1_hbm, w3_hbm, w2_hbm, ys_ref,
                    w1_ref, w3_ref, w2_ref, st1_ref, st3_ref, st2_ref, sem_ref):
    t = pl.program_id(0)
    valid = t < nt_ref[0]
    group = tab_ref[T_GROUP, t]
    slot = tab_ref[T_SLOT, t]
    run_pos = tab_ref[T_RUN_POS, t]
    next_group = tab_ref[T_NEXT_GROUP, t]

    def expert_copies(grp, e):
        idx = grp * EXPERTS_PER_GROUP + e
        return (pltpu.make_async_copy(w1_hbm.at[layer, idx], st1_ref, sem_ref.at[0]),
                pltpu.make_async_copy(w3_hbm.at[layer, idx], st3_ref, sem_ref.at[1]),
                pltpu.make_async_copy(w2_hbm.at[layer, idx], st2_ref, sem_ref.at[2]))

    def finish(copies, dst_slot, e):
        for cp in copies:
            cp.wait()
        w1_ref[dst_slot, e] = st1_ref[...].astype(BF16)
        w3_ref[dst_slot, e] = st3_ref[...].astype(BF16)
        w2_ref[dst_slot, e] = st2_ref[...].astype(BF16)

    @pl.when(jnp.logical_and(valid, tab_ref[T_FIRST, t] == 1))
    def _():
        def load(e, carry):
            copies = expert_copies(group, e)
            for cp in copies:
                cp.start()
            finish(copies, slot, e)
            return carry

        lax.fori_loop(tab_ref[T_PREFETCHED, t], EXPERTS_PER_GROUP, load, 0)

    prefetch = jnp.logical_and(valid, jnp.logical_and(next_group >= 0, run_pos < EXPERTS_PER_GROUP))

    @pl.when(prefetch)
    def _():
        for cp in expert_copies(next_group, run_pos):
            cp.start()

    @pl.when(valid)
    def _():
        h = hs_ref[:, 0:D_MODEL].astype(BF16)
        c = hs_ref[:, D_MODEL:]
        lane = lax.broadcasted_iota(jnp.int32, c.shape, 1)
        acc = None
        for e in (tab_ref[T_LO, t], tab_ref[T_HI, t]):
            ce = jnp.sum(jnp.where(lane == e, c, 0.0), axis=1, keepdims=True)
            a = jnp.dot(h, w1_ref[slot, e], preferred_element_type=F32)
            b = jnp.dot(h, w3_ref[slot, e], preferred_element_type=F32)
            hid = (a * _sigmoid(a)) * b * ce
            y = jnp.dot(hid.astype(BF16), w2_ref[slot, e], preferred_element_type=F32)
            acc = y if acc is None else acc + y
        ys_ref[...] = acc

    @pl.when(prefetch)
    def _():
        finish(expert_copies(next_group, run_pos), 1 - slot, run_pos)

    @pl.when(jnp.logical_not(valid))
    def _():
        ys_ref[...] = jnp.zeros_like(ys_ref)


def experts(tile_table, n_tiles, hs, w1, w3, w2, layer):
    hbm = pl.BlockSpec(memory_space=pl.ANY)
    return pl.pallas_call(
        functools.partial(_experts_kernel, layer),
        grid_spec=pltpu.PrefetchScalarGridSpec(
            num_scalar_prefetch=2,
            grid=(EXP_TILES,),
            in_specs=[pl.BlockSpec((EXP_TM, HX_W), lambda t, *_: (t, 0)), hbm, hbm, hbm],
            out_specs=pl.BlockSpec((EXP_TM, D_MODEL), lambda t, *_: (t, 0)),
            scratch_shapes=[
                pltpu.VMEM((2, EXPERTS_PER_GROUP, D_MODEL, D_EXPERT), BF16),
                pltpu.VMEM((2, EXPERTS_PER_GROUP, D_MODEL, D_EXPERT), BF16),
                pltpu.VMEM((2, EXPERTS_PER_GROUP, D_EXPERT, D_MODEL), BF16),
                pltpu.VMEM((D_MODEL, D_EXPERT), F32),
                pltpu.VMEM((D_MODEL, D_EXPERT), F32),
                pltpu.VMEM((D_EXPERT, D_MODEL), F32),
                pltpu.SemaphoreType.DMA((3,)),
            ],
        ),
        out_shape=jax.ShapeDtypeStruct((HS_ROWS, D_MODEL), F32),
        compiler_params=_params(1),
        name="experts",
    )(tile_table, n_tiles, hs, w1, w3, w2)


def _expert_tiles(counts):
    tiles = (counts + EXP_TM - 1) // EXP_TM
    ends = jnp.cumsum(tiles)
    n_tiles = ends[-1]
    t = jnp.arange(EXP_TILES, dtype=jnp.int32)
    used = t < n_tiles
    bucket = jnp.minimum(jnp.sum(t[:, None] >= ends[None, :], axis=1), N_BUCKETS - 1)
    group = bucket // N_PAIRS
    pair = bucket % N_PAIRS
    lo = sum(jnp.where(pair == k, p[0], 0) for k, p in enumerate(EXPERT_PAIRS))
    hi = sum(jnp.where(pair == k, p[1], 0) for k, p in enumerate(EXPERT_PAIRS))
    first = jnp.logical_and(used, jnp.logical_or(t == 0, group != jnp.roll(group, 1)))
    run = jnp.cumsum(first) - 1
    run_start = lax.cummax(jnp.where(first, t, 0))
    same_run = jnp.logical_and(run[:, None] == run[None, :], used[None, :])
    run_len = jnp.sum(same_run, axis=1)
    next_first = jnp.logical_and(first[None, :], run[None, :] == run[:, None] + 1)
    next_group = jnp.where(jnp.any(next_first, axis=1), jnp.sum(jnp.where(next_first, group[None, :], 0), axis=1), -1)
    prev_len = jnp.sum(jnp.where(t[None, :] == run_start[:, None] - 1, run_len[None, :], 0), axis=1)
    prefetched = jnp.where(run > 0, jnp.minimum(prev_len, EXPERTS_PER_GROUP), 0)
    table = jnp.stack([group, lo, hi, run % 2, t - run_start, next_group, first, prefetched]).astype(jnp.int32)
    return ((ends - tiles) * EXP_TM).astype(jnp.int32), table, n_tiles.reshape(1).astype(jnp.int32)


RES_TM = 512
RES_CTX_TILES = N_CTX // RES_TM


def _moe_residual_kernel(final, pos_ref, x_ref, mod_ref, gain_ref, ys_ref, *refs):
    out_refs, (ybuf_ref, sem_ref) = refs[:-2], refs[-2:]
    i = pl.program_id(0)
    slot = i % 2

    def gather_tile(tile, tile_slot):
        def issue(g0, u):
            dst = ybuf_ref.at[tile_slot, pl.ds(g0, ROW_GROUP), :]
            pltpu.make_async_copy(ys_ref.at[pl.ds(pos_ref[tile * RES_TM + g0 + u], 1), :], dst.at[pl.ds(u, 1), :],
                                  sem_ref.at[tile_slot]).start()

        _for_each_row(RES_TM, issue)

    @pl.when(i == 0)
    def _():
        gather_tile(0, 0)

    @pl.when(i + 1 < pl.num_programs(0))
    def _():
        gather_tile(i + 1, 1 - slot)

    pltpu.make_async_copy(ys_ref.at[pl.ds(0, RES_TM), :], ybuf_ref.at[slot], sem_ref.at[slot]).wait()
    x = x_ref[...] + mod_ref[:, 5 * D_MODEL:6 * D_MODEL] * ybuf_ref[slot]
    if not final:
        out_refs[0][...] = x
        return
    y = _rms(x, gain_ref[...])

    @pl.when(i < RES_CTX_TILES)
    def _():
        out_refs[0][...] = y

    @pl.when(i >= RES_CTX_TILES)
    def _():
        out_refs[1][...] = y


def moe_residual(pos, x, mod3, gain, ys, final):
    if final:
        out_specs = _split_specs((RES_TM, D_MODEL), RES_CTX_TILES)
        out_shape = [jax.ShapeDtypeStruct((N_CTX, D_MODEL), F32), jax.ShapeDtypeStruct((N_LAT, D_MODEL), F32)]
    else:
        out_specs = pl.BlockSpec((RES_TM, D_MODEL), lambda i, *_: (i, 0))
        out_shape = jax.ShapeDtypeStruct((N_TOK, D_MODEL), F32)
    return pl.pallas_call(
        functools.partial(_moe_residual_kernel, final),
        grid_spec=pltpu.PrefetchScalarGridSpec(
            num_scalar_prefetch=1,
            grid=(N_TOK // RES_TM,),
            in_specs=[
                pl.BlockSpec((RES_TM, D_MODEL), lambda i, *_: (i, 0)),
                pl.BlockSpec((None, 1, MOD_WIDTH), lambda i, *_: (_cond_row(i, RES_TM), 0, 0)),
                pl.BlockSpec((1, D_MODEL), lambda i, *_: (0, 0)),
                pl.BlockSpec(memory_space=pl.ANY),
            ],
            out_specs=out_specs,
            scratch_shapes=[pltpu.VMEM((2, RES_TM, D_MODEL), F32), pltpu.SemaphoreType.DMA((2,))],
        ),
        out_shape=out_shape,
        compiler_params=_params(1),
        name="moe_residual_final" if final else "moe_residual",
    )(pos, x, mod3, gain, ys)


def kernel(x_prompt, x_sample, cache_k, cache_v, state_lru, c, c_ctx, w_mod, b_mod, norm_mix, norm_ffn, w_in, sgu_norm, sgu_w, sgu_b, lru_conv_w, lru_conv_b, lru_w_r, lru_b_r, lru_w_i, lru_b_i, lru_lambda, na_rpb, w_branch_sgu, w_branch_lru, w_branch_na, w_out, moe_w_group, moe_b_group, moe_w_expert, moe_b_expert, moe_w1, moe_w3, moe_w2, final_norm_gain):
    xs = (x_prompt.reshape(N_CTX, D_MODEL), x_sample.reshape(N_LAT, D_MODEL))
    cond = jnp.zeros((N_COND, D_MODEL), F32).at[0].set(c_ctx).at[1:1 + DEC_BATCH].set(c)
    mod = modulation(cond, w_mod, b_mod)
    zero_state = jnp.zeros((BATCH, 2, LRU_WIDTH), F32)
    cache_k = cache_k.reshape(DEC_BATCH, DEPTH, PAST_LEN, NA_WIDTH)
    cache_v = cache_v.reshape(DEC_BATCH, DEPTH, PAST_LEN, NA_WIDTH)
    final_gain = final_norm_gain.reshape(1, D_MODEL)
    ks, vs, ss = [], [], []
    for l in range(DEPTH):
        mod3 = mod[l].reshape(N_COND, 1, MOD_WIDTH)
        ya, za, q, kb, vb, k_ctx, v_ctx, gates = in_projection(
            xs, mod3, norm_mix[l].reshape(1, D_MODEL), w_in, sgu_norm[l].reshape(1, SGU_WIDTH), sgu_w[l], sgu_b[l].T, l)
        w_lo, w_hi = _block_diag_gate_weights(lru_w_r[l], lru_w_i[l])
        lru_args = (lru_conv_w[l], lru_conv_b[l].reshape(1, LRU_WIDTH), w_lo, w_hi, lru_b_r[l], lru_b_i[l],
                    lru_lambda[l])
        yb_ctx, st_ctx = rglru(za, *lru_args, zero_state, SEQ, BATCH, 0)
        yb_lat, _ = rglru(za, *lru_args, state_lru[:, l], DEC_SEQ, DEC_BATCH, N_CTX)
        yc_ctx = context_attention(q, kb, vb)
        yc_lat = latent_attention(q, kb, vb, cache_k, cache_v, na_rpb[l].reshape(-1), l)
        w_rt, b_rt = _router_weights(moe_w_group[l], moe_b_group[l], moe_w_expert[l], moe_b_expert[l])
        x, hx, bucket, rank, counts = merge_branches(
            xs, mod3, ya, yb_ctx, yb_lat, yc_ctx, yc_lat, gates, w_branch_sgu, w_branch_lru, w_branch_na, w_out,
            norm_ffn[l].reshape(1, D_MODEL), w_rt, b_rt, l)
        bucket, rank, counts = bucket.reshape(N_TOK), rank.reshape(N_TOK), counts[:N_BUCKETS, 0]
        starts, tile_table, n_tiles = _expert_tiles(counts)
        pos = rank + jnp.sum(jnp.where(bucket[:, None] == jnp.arange(N_BUCKETS), starts[None, :], 0), axis=1)
        hs = dispatch(pos, starts, counts, n_tiles, hx)
        ys = experts(tile_table, n_tiles, hs, moe_w1, moe_w3, moe_w2, l)
        if l < DEPTH - 1:
            xs = (moe_residual(pos, x, mod3, final_gain, ys, False),)
        else:
            y_ctx, y_lat = moe_residual(pos, x, mod3, final_gain, ys, True)
        ks.append(k_ctx.reshape(BATCH, SEQ, NA_HEADS, NA_HEAD_DIM))
        vs.append(v_ctx.reshape(BATCH, SEQ, NA_HEADS, NA_HEAD_DIM))
        ss.append(st_ctx)
    return (y_ctx.reshape(BATCH, SEQ, D_MODEL), y_lat.reshape(DEC_BATCH, DEC_SEQ, D_MODEL),
            jnp.stack(ks, axis=1), jnp.stack(vs, axis=1), jnp.stack(ss, axis=1))
```

```python
import functools

import jax
import jax.numpy as jnp
import numpy as np
from jax import lax
from jax.experimental import pallas as pl
from jax.experimental.pallas import tpu as pltpu

F32 = jnp.float32
BF16 = jnp.bfloat16

D_MODEL = 1024
BATCH = 16
SEQ = 256
DEPTH = 2
DEC_BATCH = 4
DEC_SEQ = 2048
PAST_LEN = 512
GRID_W = 64
CHUNK = 128
SGU_WIDTH = 512
SGU_GROUPS = 4
LRU_WIDTH = 512
LRU_BLOCKS = 8
CONV_WIDTH = 4
LRU_C = 8.0
NA_HEADS = 8
NA_HEAD_DIM = 64
NA_WIDTH = 512
WIN_ROWS = 8
WIN_COLS = 16
N_GROUPS = 4
EXPERTS_PER_GROUP = 4
N_EXPERTS = 16
D_EXPERT = 512
IN_WIDTH = 6656
EPS = 1e-6
NEG_INF = -1e30

N_CTX = BATCH * SEQ
N_LAT = DEC_BATCH * DEC_SEQ
N_TOK = N_CTX + N_LAT
N_COND = 8
MOD_WIDTH = 6 * D_MODEL
GRID_ROWS = DEC_SEQ // GRID_W

VMEM_LIMIT_BYTES = 56 * 1024 * 1024


def _params(n_axes):
    return pltpu.CompilerParams(dimension_semantics=("arbitrary",) * n_axes,
                                vmem_limit_bytes=VMEM_LIMIT_BYTES)


def _cond_row(tile, tile_rows):
    tok = tile * tile_rows
    return jnp.where(tok < N_CTX, 0, 1 + (tok - N_CTX) // DEC_SEQ)


def _rms(x, gain):
    return x * lax.rsqrt(jnp.mean(x * x, axis=-1, keepdims=True) + EPS) * gain


def _bdot(a, b):
    return jnp.dot(a.astype(BF16), b.astype(BF16), preferred_element_type=F32)


def _bdot_t(a, b):
    return lax.dot_general(a.astype(BF16), b.astype(BF16), (((1,), (1,)), ((), ())),
                           preferred_element_type=F32)


MOD_TN = 1536


def _mod_kernel(cond_ref, w_ref, b_ref, o_ref):
    c = cond_ref[...]
    s = c * jax.nn.sigmoid(c)
    o_ref[...] = _bdot(s, w_ref[...]) + b_ref[...]


def modulation(cond, w_mod, b_mod):
    return pl.pallas_call(
        _mod_kernel,
        grid=(DEPTH, MOD_WIDTH // MOD_TN),
        in_specs=[
            pl.BlockSpec((N_COND, D_MODEL), lambda l, j: (0, 0)),
            pl.BlockSpec((None, D_MODEL, MOD_TN), lambda l, j: (l, 0, j)),
            pl.BlockSpec((None, 1, MOD_TN), lambda l, j: (l, 0, j)),
        ],
        out_specs=pl.BlockSpec((None, N_COND, MOD_TN), lambda l, j: (l, 0, j)),
        out_shape=jax.ShapeDtypeStruct((DEPTH, N_COND, MOD_WIDTH), F32),
        compiler_params=_params(2),
        name="modulation",
    )(cond, w_mod, b_mod.reshape(DEPTH, 1, MOD_WIDTH))


IN_TM = 256
IN_CTX_TILES = N_CTX // IN_TM
ZA_WIDTH = 4 * 512
KV_OFF = ZA_WIDTH + NA_WIDTH
GATE_OFF = KV_OFF + 2 * NA_WIDTH
W_CHUNK = 512
Q_SCALE = NA_HEAD_DIM ** -0.5 * float(np.log2(np.e))


def _load_cast(chunks, stage_ref, sem_ref):
    def copy(j):
        return pltpu.make_async_copy(chunks[j][0], stage_ref.at[j % 2], sem_ref.at[j % 2])

    copy(0).start()
    for j in range(len(chunks)):
        if j + 1 < len(chunks):
            copy(j + 1).start()
        copy(j).wait()
        chunks[j][1](stage_ref[j % 2].astype(BF16))


def _token_tile(i, n_ctx_tiles, refs):
    if len(refs) == 1:
        return refs[0][...]
    return jnp.where(i < n_ctx_tiles, refs[0][...], refs[1][...])


def _split_specs(block, n_ctx_tiles):
    return [pl.BlockSpec(block, lambda i, *_: (jnp.minimum(i, n_ctx_tiles - 1), 0)),
            pl.BlockSpec(block, lambda i, *_: (jnp.maximum(i - n_ctx_tiles, 0), 0))]


SGU_GD = SGU_WIDTH // SGU_GROUPS


def _spatial_gating(u, v, gain, ws_ref, bs_ref):
    u = jax.nn.gelu(u)
    v = _rms(jax.nn.gelu(v), gain).astype(BF16)
    out = []
    for g in range(SGU_GROUPS):
        cols = slice(g * SGU_GD, (g + 1) * SGU_GD)
        mixed = jnp.dot(ws_ref[g].astype(BF16), v[:, cols], preferred_element_type=F32) + bs_ref[:, g:g + 1]
        out.append(u[:, cols] * mixed)
    return jnp.concatenate(out, axis=1)


def _inproj_kernel(layer, n_x, *refs):
    x_refs, (mod_ref, gain_ref, w_hbm, sgu_gain_ref, ws_ref, bs_ref) = refs[:n_x], refs[n_x:n_x + 6]
    ya_ref, zb_ref, q_ref, kb_ref, vb_ref, kc_ref, vc_ref, g_ref, w_ref, stage_ref, sem_ref = refs[n_x + 6:]
    i = pl.program_id(0)

    @pl.when(i == 0)
    def _():
        def store(c):
            def st(v):
                w_ref[:, c * W_CHUNK:(c + 1) * W_CHUNK] = v
            return st

        _load_cast([(w_hbm.at[layer, :, pl.ds(c * W_CHUNK, W_CHUNK)], store(c)) for c in range(IN_WIDTH // W_CHUNK)],
                   stage_ref, sem_ref)

    m = mod_ref[...]
    shift, scale = m[:, 0:D_MODEL], m[:, D_MODEL:2 * D_MODEL]
    h = (_rms(_token_tile(i, IN_CTX_TILES, x_refs), gain_ref[...]) * (1.0 + scale) + shift).astype(BF16)
    za = jnp.dot(h, w_ref[:, 0:ZA_WIDTH], preferred_element_type=F32)
    for c in range(IN_TM // CHUNK):
        rows = slice(c * CHUNK, (c + 1) * CHUNK)
        ya_ref[rows, :] = _spatial_gating(za[rows, 0:SGU_WIDTH], za[rows, SGU_WIDTH:2 * SGU_WIDTH],
                                          sgu_gain_ref[...], ws_ref, bs_ref).astype(BF16)
    zb_ref[...] = za[:, 2 * SGU_WIDTH:].astype(BF16)
    q_ref[...] = (jnp.dot(h, w_ref[:, ZA_WIDTH:KV_OFF], preferred_element_type=F32) * Q_SCALE).astype(BF16)
    kv = jnp.dot(h, w_ref[:, KV_OFF:GATE_OFF], preferred_element_type=F32)
    kb_ref[...] = kv[:, 0:NA_WIDTH].astype(BF16)
    vb_ref[...] = kv[:, NA_WIDTH:].astype(BF16)
    g_ref[...] = jnp.dot(h, w_ref[:, GATE_OFF:], preferred_element_type=F32).astype(BF16)

    @pl.when(i < IN_CTX_TILES)
    def _():
        kc_ref[...] = kv[:, 0:NA_WIDTH]
        vc_ref[...] = kv[:, NA_WIDTH:]


def in_projection(xs, mod3, gain, w_in, sgu_gain, sgu_w, sgu_b_t, layer):
    row = lambda i: (i, 0)
    ctx_row = lambda i: (jnp.minimum(i, IN_CTX_TILES - 1), 0)
    const = lambda shape: pl.BlockSpec(shape, lambda i: (0,) * len(shape))
    x_specs = ([pl.BlockSpec((IN_TM, D_MODEL), row)] if len(xs) == 1
               else _split_specs((IN_TM, D_MODEL), IN_CTX_TILES))
    return pl.pallas_call(
        functools.partial(_inproj_kernel, layer, len(xs)),
        grid=(N_TOK // IN_TM,),
        in_specs=x_specs + [
            pl.BlockSpec((None, 1, MOD_WIDTH), lambda i: (_cond_row(i, IN_TM), 0, 0)),
            const((1, D_MODEL)),
            pl.BlockSpec(memory_space=pl.ANY),
            const((1, SGU_WIDTH)),
            const((SGU_GROUPS, CHUNK, CHUNK)),
            const((CHUNK, SGU_GROUPS)),
        ],
        out_specs=[
            pl.BlockSpec((IN_TM, SGU_WIDTH), row),
            pl.BlockSpec((IN_TM, 2 * LRU_WIDTH), row),
            pl.BlockSpec((IN_TM, NA_WIDTH), row),
            pl.BlockSpec((IN_TM, NA_WIDTH), row),
            pl.BlockSpec((IN_TM, NA_WIDTH), row),
            pl.BlockSpec((IN_TM, NA_WIDTH), ctx_row),
            pl.BlockSpec((IN_TM, NA_WIDTH), ctx_row),
            pl.BlockSpec((IN_TM, 3 * D_MODEL), row),
        ],
        out_shape=[
            jax.ShapeDtypeStruct((N_TOK, SGU_WIDTH), BF16),
            jax.ShapeDtypeStruct((N_TOK, 2 * LRU_WIDTH), BF16),
            jax.ShapeDtypeStruct((N_TOK, NA_WIDTH), BF16),
            jax.ShapeDtypeStruct((N_TOK, NA_WIDTH), BF16),
            jax.ShapeDtypeStruct((N_TOK, NA_WIDTH), BF16),
            jax.ShapeDtypeStruct((N_CTX, NA_WIDTH), F32),
            jax.ShapeDtypeStruct((N_CTX, NA_WIDTH), F32),
            jax.ShapeDtypeStruct((N_TOK, 3 * D_MODEL), BF16),
        ],
        scratch_shapes=[pltpu.VMEM((D_MODEL, IN_WIDTH), BF16), pltpu.VMEM((2, D_MODEL, W_CHUNK), F32),
                        pltpu.SemaphoreType.DMA((2,))],
        compiler_params=_params(1),
        name="in_projection",
    )(*xs, mod3, gain, w_in, sgu_gain, sgu_w, sgu_b_t)


LRU_TC = 256
LRU_HALF = 256
SUB = 8
HALO = 8
LANES = 128
LRU_SLABS = LRU_WIDTH // LANES


def _sigmoid(x):
    return 0.5 * jnp.tanh(0.5 * x) + 0.5


def _lru_kernel(seq_len, xr_ref, gr_ref, cw_ref, cb_ref, wlo_ref, whi_ref, br_ref, bi_ref,
                lam_ref, h0_ref, y_ref, st_ref, xp_ref, a_ref, b_ref, h_ref):
    seg_len = seq_len // SUB
    pitch = seg_len + SUB
    segs_per_chunk = max(LRU_TC // seg_len, 1)
    n_chunks = seq_len // LRU_TC
    zeros = jnp.zeros((HALO, LRU_WIDTH), F32)
    xp_ref[0:HALO, :] = zeros
    xp_ref[seq_len + HALO:seq_len + 2 * HALO, :] = zeros

    def copy_in(c, carry):
        r0 = pl.multiple_of(c * LRU_TC, LRU_TC)
        xp_ref[pl.ds(r0 + HALO, LRU_TC), :] = xr_ref[pl.ds(r0, LRU_TC), :].astype(F32)
        return carry

    lax.fori_loop(0, n_chunks, copy_in, 0)

    def chunk_rows(c, seg):
        n = min(seg_len, LRU_TC)
        start = pl.multiple_of((c * segs_per_chunk + seg) * pitch, SUB)
        return pl.ds(start, n), slice(seg * n, (seg + 1) * n)

    cw = cw_ref[...]
    cb = cb_ref[...]
    win = LRU_TC + 2 * HALO
    neg_lam = -lam_ref[...]
    softplus = jnp.maximum(neg_lam, 0.0) + jnp.log1p(jnp.exp(-jnp.abs(neg_lam)))
    decay = (0.5 * LRU_C) * softplus
    half_br = 0.5 * br_ref[...]
    half_bi = 0.5 * bi_ref[...]

    def gates(c, carry):
        r0 = pl.multiple_of(c * LRU_TC, LRU_TC)
        w = xp_ref[pl.ds(r0, win), :]
        xc = (cw[0:1, :] * pltpu.roll(w, 1, 0)[HALO:HALO + LRU_TC]
              + cw[1:2, :] * w[HALO:HALO + LRU_TC]
              + cw[2:3, :] * pltpu.roll(w, win - 1, 0)[HALO:HALO + LRU_TC]
              + cw[3:4, :] * pltpu.roll(w, win - 2, 0)[HALO:HALO + LRU_TC]) + cb
        xb = xc.astype(BF16)
        for half, w_ref in enumerate((wlo_ref, whi_ref)):
            cols = slice(half * LRU_HALF, (half + 1) * LRU_HALF)
            pre = jnp.dot(xb[:, cols], w_ref[...], preferred_element_type=F32)
            half_x = 0.5 * xc[:, cols]
            for d in range(2):
                tr = jnp.tanh(0.5 * pre[:, (2 * d) * LRU_HALF:(2 * d + 1) * LRU_HALF] + half_br[d:d + 1, cols])
                ti = jnp.tanh(0.5 * pre[:, (2 * d + 1) * LRU_HALF:(2 * d + 2) * LRU_HALF] + half_bi[d:d + 1, cols])
                neg_log_a = decay[d:d + 1, cols] * tr + decay[d:d + 1, cols]
                a = jnp.exp2(neg_log_a * -LOG2E)
                z = jnp.tanh(neg_log_a) * (a * a + 1.0)
                b = jnp.where(z > 0.0, z * lax.rsqrt(z), 0.0) * (half_x * ti + half_x)
                for k in range(LRU_HALF // LANES):
                    slab = half * (LRU_HALF // LANES) + k
                    lanes = slice(k * LANES, (k + 1) * LANES)
                    for seg in range(segs_per_chunk):
                        dst, src = chunk_rows(c, seg)
                        a_ref[d, slab, dst, :] = a[src, lanes]
                        b_ref[d, slab, dst, :] = b[src, lanes]
        return carry

    lax.fori_loop(0, n_chunks, gates, 0)

    def step_rows(d, j):
        return pl.ds(j if d == 0 else seg_len - 1 - j, SUB, stride=pitch)

    def reduce_step(j, carry):
        out = []
        for d in range(2):
            for slab in range(LRU_SLABS):
                big_a, big_b = carry[d * LRU_SLABS + slab]
                a = a_ref[d, slab, step_rows(d, j), :]
                b = b_ref[d, slab, step_rows(d, j), :]
                out.append((a * big_a, a * big_b + b))
        return tuple(out)

    ident = (jnp.ones((SUB, LANES), F32), jnp.zeros((SUB, LANES), F32))
    totals = lax.fori_loop(0, seg_len, reduce_step, (ident,) * (2 * LRU_SLABS), unroll=8)

    sub = lax.broadcasted_iota(jnp.int32, (SUB, LANES), 0)
    entering = []
    for d in range(2):
        for slab in range(LRU_SLABS):
            big_a, big_b = totals[d * LRU_SLABS + slab]
            h = h0_ref[d:d + 1, slab * LANES:(slab + 1) * LANES]
            rows = jnp.zeros((SUB, LANES), F32)
            for s in (range(SUB) if d == 0 else reversed(range(SUB))):
                rows = jnp.where(sub == s, h, rows)
                h = big_a[s:s + 1, :] * h + big_b[s:s + 1, :]
            entering.append(rows)
            st_ref[d:d + 1, slab * LANES:(slab + 1) * LANES] = h

    def apply_step(j, carry):
        out = []
        for d in range(2):
            for slab in range(LRU_SLABS):
                h = (a_ref[d, slab, step_rows(d, j), :] * carry[d * LRU_SLABS + slab]
                     + b_ref[d, slab, step_rows(d, j), :])
                h_ref[d, slab, step_rows(d, j), :] = h
                out.append(h)
        return tuple(out)

    lax.fori_loop(0, seg_len, apply_step, tuple(entering), unroll=8)

    def merge(c, carry):
        r0 = pl.multiple_of(c * LRU_TC, LRU_TC)
        gate = jax.nn.gelu(gr_ref[pl.ds(r0, LRU_TC), :].astype(F32))
        for slab in range(LRU_SLABS):
            lanes = slice(slab * LANES, (slab + 1) * LANES)
            for seg in range(segs_per_chunk):
                src, dst = chunk_rows(c, seg)
                h = h_ref[0, slab, src, :] + h_ref[1, slab, src, :]
                y_ref[pl.ds(r0 + dst.start, dst.stop - dst.start), lanes] = (h * gate[dst, lanes]).astype(BF16)
        return carry

    lax.fori_loop(0, n_chunks, merge, 0)


def rglru(za, conv_w, conv_b, w_lo, w_hi, b_r, b_i, lam, h0, seq_len, n_seq, tok_off):
    blk0 = tok_off // seq_len
    full = lambda shape: pl.BlockSpec(shape, lambda i: (0,) * len(shape))
    return pl.pallas_call(
        functools.partial(_lru_kernel, seq_len),
        grid=(n_seq,),
        in_specs=[
            pl.BlockSpec((seq_len, LRU_WIDTH), lambda i: (i + blk0, 0)),
            pl.BlockSpec((seq_len, LRU_WIDTH), lambda i: (i + blk0, 1)),
            full((CONV_WIDTH, LRU_WIDTH)),
            full((1, LRU_WIDTH)),
            full((LRU_HALF, 4 * LRU_HALF)),
            full((LRU_HALF, 4 * LRU_HALF)),
            full((2, LRU_WIDTH)),
            full((2, LRU_WIDTH)),
            full((2, LRU_WIDTH)),
            pl.BlockSpec((None, 2, LRU_WIDTH), lambda i: (i, 0, 0)),
        ],
        out_specs=[
            pl.BlockSpec((seq_len, LRU_WIDTH), lambda i: (i, 0)),
            pl.BlockSpec((None, 2, LRU_WIDTH), lambda i: (i, 0, 0)),
        ],
        out_shape=[
            jax.ShapeDtypeStruct((n_seq * seq_len, LRU_WIDTH), BF16),
            jax.ShapeDtypeStruct((n_seq, 2, LRU_WIDTH), F32),
        ],
        scratch_shapes=[
            pltpu.VMEM((seq_len + 2 * HALO, LRU_WIDTH), F32),
            pltpu.VMEM((2, LRU_SLABS, seq_len + SUB * SUB, LANES), F32),
            pltpu.VMEM((2, LRU_SLABS, seq_len + SUB * SUB, LANES), F32),
            pltpu.VMEM((2, LRU_SLABS, seq_len + SUB * SUB, LANES), F32),
        ],
        compiler_params=_params(1),
        name=f"rglru_{seq_len}",
    )(za, za, conv_w, conv_b, w_lo, w_hi, b_r, b_i, lam, h0)


def _block_diag_gate_weights(w_r, w_i):
    per_half = LRU_BLOCKS // 2
    eye = jnp.eye(per_half, dtype=w_r.dtype)

    def bd(w):
        return jnp.einsum("hij,hg->higj", w, eye).reshape(LRU_HALF, LRU_HALF)

    tiles = []
    for half in range(2):
        sl = slice(half * per_half, (half + 1) * per_half)
        tiles.append(jnp.concatenate([bd(w_r[0, sl]), bd(w_i[0, sl]), bd(w_r[1, sl]), bd(w_i[1, sl])], axis=1))
    return tiles[0].astype(BF16), tiles[1].astype(BF16)


PAIR_W = 2 * NA_HEAD_DIM
LOG2E = float(np.log2(np.e))


def _head_lanes(shape, head):
    lane = lax.broadcasted_iota(jnp.int32, shape, 1)
    return (lane < NA_HEAD_DIM) if head == 0 else (lane >= NA_HEAD_DIM)


def _one_head(q_pair, head):
    return jnp.where(_head_lanes(q_pair.shape, head), q_pair.astype(F32), 0.0).astype(BF16)


def _join_heads(o0, o1):
    return jnp.where(_head_lanes(o0.shape, 0), o0, o1)


def _ctx_attn_kernel(q_ref, k_ref, v_ref, o_ref):
    for j in range(NA_HEADS // 2):
        cols = slice(j * PAIR_W, (j + 1) * PAIR_W)
        outs = []
        for head in range(2):
            s = _bdot_t(_one_head(q_ref[:, cols], head), k_ref[:, cols])
            p = jnp.exp2(s - jnp.max(s, axis=-1, keepdims=True))
            denom = jnp.sum(p, axis=-1, keepdims=True)
            outs.append(jnp.dot(p.astype(BF16), v_ref[:, cols], preferred_element_type=F32) / denom)
        o_ref[:, cols] = _join_heads(*outs).astype(BF16)


def context_attention(q, kb, vb):
    spec = pl.BlockSpec((SEQ, NA_WIDTH), lambda i: (i, 0))
    return pl.pallas_call(
        _ctx_attn_kernel,
        grid=(BATCH,),
        in_specs=[spec, spec, spec],
        out_specs=spec,
        out_shape=jax.ShapeDtypeStruct((N_CTX, NA_WIDTH), BF16),
        compiler_params=_params(1),
        name="context_attention",
    )(q, kb, vb)


NA_QROWS = 4
NA_TQ = NA_QROWS * GRID_W
NA_KROWS = NA_QROWS + WIN_ROWS
NA_TK = NA_KROWS * GRID_W
NA_QBLOCKS = GRID_ROWS // NA_QROWS


N_DROW = 2 * WIN_ROWS - 1
N_DCOL = 2 * WIN_COLS - 1
NA_BLOCK_KINDS = (0, 1, NA_QBLOCKS - 1)


def _first_key_row(qb):
    return (np.clip if isinstance(qb, int) else jnp.clip)(qb * NA_QROWS - WIN_ROWS // 2, 0, GRID_ROWS - NA_KROWS)


def _build_bias(rpb_ref, table_ref, bias_ref):
    qc = lax.broadcasted_iota(jnp.int32, (GRID_W, GRID_W), 0)
    kc = lax.broadcasted_iota(jnp.int32, (GRID_W, GRID_W), 1)
    col0 = jnp.clip(qc - WIN_COLS // 2, 0, GRID_W - WIN_COLS)
    col_ok = jnp.logical_and(kc >= col0, kc < col0 + WIN_COLS)
    d_col = jnp.clip(kc - qc, 1 - WIN_COLS, WIN_COLS - 1) + WIN_COLS - 1
    neg = jnp.full((GRID_W, GRID_W), NEG_INF, F32)

    def table_entry(idx, carry):
        t = neg
        for j in range(N_DCOL):
            t = jnp.where(d_col == j, rpb_ref[idx * N_DCOL + j], t)
        table_ref[idx] = jnp.where(col_ok, t * LOG2E, neg)
        return carry

    lax.fori_loop(0, NA_HEADS * N_DROW, table_entry, 0)

    def head_blocks(h, carry):
        for kind, qb in enumerate(NA_BLOCK_KINDS):
            for i in range(NA_QROWS):
                qrow = qb * NA_QROWS + i
                win0 = int(np.clip(qrow - WIN_ROWS // 2, 0, GRID_ROWS - WIN_ROWS))
                for kr in range(NA_KROWS):
                    krow = int(_first_key_row(qb)) + kr
                    inside = win0 <= krow < win0 + WIN_ROWS
                    blk = table_ref[h * N_DROW + (krow - qrow + WIN_ROWS - 1)] if inside else neg
                    bias_ref[kind, h, i * GRID_W:(i + 1) * GRID_W, kr * GRID_W:(kr + 1) * GRID_W] = blk
        return carry

    lax.fori_loop(0, NA_HEADS, head_blocks, 0)


def _lat_attn_kernel(rpb_ref, q_ref, k_ref, v_ref, ck_ref, cv_ref, o_ref, table_ref, bias_ref):
    qb = pl.program_id(1)

    @pl.when(jnp.logical_and(pl.program_id(0) == 0, qb == 0))
    def _():
        _build_bias(rpb_ref, table_ref, bias_ref)

    kind = jnp.where(qb == 0, 0, jnp.where(qb == NA_QBLOCKS - 1, 2, 1))
    k0 = pl.multiple_of(_first_key_row(qb) * GRID_W, GRID_W)
    for j in range(NA_HEADS // 2):
        cols = slice(j * PAIR_W, (j + 1) * PAIR_W)
        k_loc = k_ref[pl.ds(k0, NA_TK), cols]
        v_loc = v_ref[pl.ds(k0, NA_TK), cols]
        k_ctx = ck_ref[:, cols].astype(BF16)
        v_ctx = cv_ref[:, cols].astype(BF16)
        outs = []
        for head in range(2):
            qh = _one_head(q_ref[:, cols], head)
            s_loc = _bdot_t(qh, k_loc) + bias_ref[kind, 2 * j + head]
            s_ctx = _bdot_t(qh, k_ctx)
            m = jnp.maximum(jnp.max(s_loc, axis=-1, keepdims=True), jnp.max(s_ctx, axis=-1, keepdims=True))
            p_loc = jnp.exp2(s_loc - m)
            p_ctx = jnp.exp2(s_ctx - m)
            denom = jnp.sum(p_loc, axis=-1, keepdims=True) + jnp.sum(p_ctx, axis=-1, keepdims=True)
            o = (jnp.dot(p_loc.astype(BF16), v_loc, preferred_element_type=F32)
                 + jnp.dot(p_ctx.astype(BF16), v_ctx, preferred_element_type=F32))
            outs.append(o / denom)
        o_ref[:, cols] = _join_heads(*outs).astype(BF16)


def latent_attention(q, kb, vb, cache_k, cache_v, rpb, layer):
    q_blk0 = N_CTX // NA_TQ
    s_blk0 = N_CTX // DEC_SEQ
    cache_spec = pl.BlockSpec((None, None, PAST_LEN, NA_WIDTH), lambda b, m: (b, layer, 0, 0))
    return pl.pallas_call(
        _lat_attn_kernel,
        grid=(DEC_BATCH, NA_QBLOCKS),
        in_specs=[
            pl.BlockSpec(memory_space=pltpu.SMEM),
            pl.BlockSpec((NA_TQ, NA_WIDTH), lambda b, m: (q_blk0 + b * NA_QBLOCKS + m, 0)),
            pl.BlockSpec((DEC_SEQ, NA_WIDTH), lambda b, m: (s_blk0 + b, 0)),
            pl.BlockSpec((DEC_SEQ, NA_WIDTH), lambda b, m: (s_blk0 + b, 0)),
            cache_spec,
            cache_spec,
        ],
        out_specs=pl.BlockSpec((NA_TQ, NA_WIDTH), lambda b, m: (b * NA_QBLOCKS + m, 0)),
        out_shape=jax.ShapeDtypeStruct((N_LAT, NA_WIDTH), BF16),
        scratch_shapes=[pltpu.VMEM((NA_HEADS * N_DROW, GRID_W, GRID_W), F32),
                        pltpu.VMEM((len(NA_BLOCK_KINDS), NA_HEADS, NA_TQ, NA_TK), F32)],
        compiler_params=_params(2),
        name="latent_attention",
    )(rpb, q, kb, vb, cache_k, cache_v)


MERGE_TM = 512


MERGE_CTX_TILES = N_CTX // MERGE_TM
BRANCH_W = 512


def _merge_kernel(layer, n_x, *refs):
    x_refs, rest = refs[:n_x], refs[n_x:]
    (mod_ref, ya_ref, yb_ctx_ref, yb_lat_ref, yc_ctx_ref, yc_lat_ref, g_ref, wa_hbm, wb_hbm, wc_hbm, wo_hbm,
     ffn_gain_ref, wrt_ref, brt_ref, tri_ref,
     o_ref, hx_ref, bucket_ref, rank_ref, cnt_out_ref,
     wbr_ref, wo_ref, stage_ref, sem_ref, cnt_ref, cpad_ref) = rest
    i = pl.program_id(0)

    @pl.when(i == 0)
    def _():
        def store_branch(k):
            def st(v):
                wbr_ref[k] = v
            return st

        def store_out(k):
            def st(v):
                wo_ref[k * BRANCH_W:(k + 1) * BRANCH_W, :] = v
            return st

        chunks = [(w.at[layer], store_branch(k)) for k, w in enumerate((wa_hbm, wb_hbm, wc_hbm))]
        chunks += [(wo_hbm.at[layer, pl.ds(k * BRANCH_W, BRANCH_W), :], store_out(k))
                   for k in range(D_MODEL // BRANCH_W)]
        _load_cast(chunks, stage_ref, sem_ref)

    g = _sigmoid(g_ref[...].astype(F32))
    yb = _token_tile(i, MERGE_CTX_TILES, (yb_ctx_ref, yb_lat_ref))
    yc = _token_tile(i, MERGE_CTX_TILES, (yc_ctx_ref, yc_lat_ref))
    merged = (g[:, 0:D_MODEL] * jnp.dot(ya_ref[...], wbr_ref[0], preferred_element_type=F32)
              + g[:, D_MODEL:2 * D_MODEL] * jnp.dot(yb, wbr_ref[1], preferred_element_type=F32)
              + g[:, 2 * D_MODEL:] * jnp.dot(yc, wbr_ref[2], preferred_element_type=F32))
    y = jnp.dot(merged.astype(BF16), wo_ref[...], preferred_element_type=F32)
    gate = mod_ref[:, 2 * D_MODEL:3 * D_MODEL]
    x = _token_tile(i, MERGE_CTX_TILES, x_refs) + gate * y
    o_ref[...] = x
    _route(x, mod_ref, ffn_gain_ref, wrt_ref, brt_ref, tri_ref, hx_ref, bucket_ref, rank_ref, cnt_out_ref,
           cnt_ref, cpad_ref)


def merge_branches(xs, mod3, ya, yb_ctx, yb_lat, yc_ctx, yc_lat, gates, wa, wb, wc, wo, ffn_gain, w_rt, b_rt, layer):
    assert MERGE_TM == RT_TM
    row = lambda i: (i, 0)
    const = lambda shape: pl.BlockSpec(shape, lambda i: (0,) * len(shape))
    tri = jnp.asarray(np.triu(np.ones((RT_TM, RT_TM), np.float32)), BF16)
    hbm = pl.BlockSpec(memory_space=pl.ANY)
    x_specs = ([pl.BlockSpec((MERGE_TM, D_MODEL), row)] if len(xs) == 1
               else _split_specs((MERGE_TM, D_MODEL), MERGE_CTX_TILES))
    return pl.pallas_call(
        functools.partial(_merge_kernel, layer, len(xs)),
        grid=(N_TOK // MERGE_TM,),
        in_specs=x_specs + [
            pl.BlockSpec((None, 1, MOD_WIDTH), lambda i: (_cond_row(i, MERGE_TM), 0, 0)),
            pl.BlockSpec((MERGE_TM, SGU_WIDTH), row),
            *_split_specs((MERGE_TM, LRU_WIDTH), MERGE_CTX_TILES),
            *_split_specs((MERGE_TM, NA_WIDTH), MERGE_CTX_TILES),
            pl.BlockSpec((MERGE_TM, 3 * D_MODEL), row),
            hbm, hbm, hbm, hbm,
            const((1, D_MODEL)),
            const((RT_ROWS, D_MODEL)),
            const((RT_ROWS, 1)),
            const((RT_TM, RT_TM)),
        ],
        out_specs=[
            pl.BlockSpec((MERGE_TM, D_MODEL), row),
            pl.BlockSpec((RT_TM, HX_W), row),
            pl.BlockSpec((1, RT_TM), lambda i: (0, i)),
            pl.BlockSpec((1, RT_TM), lambda i: (0, i)),
            const((CNT_ROWS, COMB_W)),
        ],
        out_shape=[
            jax.ShapeDtypeStruct((N_TOK, D_MODEL), F32),
            jax.ShapeDtypeStruct((N_TOK, HX_W), F32),
            jax.ShapeDtypeStruct((1, N_TOK), jnp.int32),
            jax.ShapeDtypeStruct((1, N_TOK), jnp.int32),
            jax.ShapeDtypeStruct((CNT_ROWS, COMB_W), jnp.int32),
        ],
        scratch_shapes=[pltpu.VMEM((3, BRANCH_W, D_MODEL), BF16), pltpu.VMEM((D_MODEL, D_MODEL), BF16),
                        pltpu.VMEM((2, BRANCH_W, D_MODEL), F32), pltpu.SemaphoreType.DMA((2,)),
                        pltpu.VMEM((CNT_ROWS, COMB_W), F32), pltpu.VMEM((COMB_W, RT_TM), F32)],
        compiler_params=_params(1),
        name="merge_branches",
    )(*xs, mod3, ya, yb_ctx, yb_lat, yc_ctx, yc_lat, gates, wa, wb, wc, wo, ffn_gain, w_rt, b_rt, tri)


RT_TM = 512
RT_ROWS = 32
EXPERT_PAIRS = ((0, 1), (0, 2), (0, 3), (1, 2), (1, 3), (2, 3))
N_PAIRS = len(EXPERT_PAIRS)
N_BUCKETS = N_GROUPS * N_PAIRS
CNT_ROWS = 32
COMB_W = 128
HX_W = D_MODEL + COMB_W
EXP_TM = 256
EXP_TILES = N_TOK // EXP_TM + N_BUCKETS
HS_ROWS = EXP_TILES * EXP_TM


def _split_bf16(x):
    hi = x.astype(BF16)
    return hi, (x - hi.astype(F32)).astype(BF16)


def _route(x, mod_ref, gain_ref, w_ref, b_ref, tri_ref, hx_ref, bucket_ref, rank_ref, cnt_out_ref,
           cnt_ref, cpad_ref):
    @pl.when(pl.program_id(0) == 0)
    def _():
        cnt_ref[...] = jnp.zeros_like(cnt_ref)
        cpad_ref[...] = jnp.zeros_like(cpad_ref)

    m = mod_ref[...]
    shift, scale = m[:, 3 * D_MODEL:4 * D_MODEL], m[:, 4 * D_MODEL:5 * D_MODEL]
    h = _rms(x, gain_ref[...]) * (1.0 + scale) + shift
    hx_ref[:, 0:D_MODEL] = h
    h_hi, h_lo = _split_bf16(h)
    w_hi, w_lo = _split_bf16(w_ref[...])
    dims = (((1,), (1,)), ((), ()))
    logits = (lax.dot_general(w_hi, h_hi, dims, preferred_element_type=F32)
              + lax.dot_general(w_hi, h_lo, dims, preferred_element_type=F32)
              + lax.dot_general(w_lo, h_hi, dims, preferred_element_type=F32)) + b_ref[...]
    gl = [logits[g:g + 1, :] for g in range(N_GROUPS)]
    gmax = functools.reduce(jnp.maximum, gl)
    gid = jnp.full(gmax.shape, N_GROUPS - 1, jnp.int32)
    for g in reversed(range(N_GROUPS - 1)):
        gid = jnp.where(gl[g] == gmax, g, gid)
    p_grp = 1.0 / functools.reduce(jnp.add, [jnp.exp(v - gmax) for v in gl])
    el = []
    for e in range(EXPERTS_PER_GROUP):
        v = logits[8 + e:9 + e, :]
        for g in range(1, N_GROUPS):
            row = 8 + g * EXPERTS_PER_GROUP + e
            v = jnp.where(gid == g, logits[row:row + 1, :], v)
        el.append(v)
    top1 = functools.reduce(jnp.maximum, el)
    idx1 = jnp.full(top1.shape, EXPERTS_PER_GROUP - 1, jnp.int32)
    for e in reversed(range(EXPERTS_PER_GROUP - 1)):
        idx1 = jnp.where(el[e] == top1, e, idx1)
    rest = [jnp.where(idx1 == e, -jnp.inf, el[e]) for e in range(EXPERTS_PER_GROUP)]
    top2 = functools.reduce(jnp.maximum, rest)
    idx2 = jnp.full(top1.shape, EXPERTS_PER_GROUP - 1, jnp.int32)
    for e in reversed(range(EXPERTS_PER_GROUP - 1)):
        idx2 = jnp.where(rest[e] == top2, e, idx2)
    e2 = jnp.exp(top2 - top1)
    w1 = p_grp / (1.0 + e2)
    w2 = p_grp * e2 / (1.0 + e2)
    for e in range(EXPERTS_PER_GROUP):
        cpad_ref[e:e + 1, :] = jnp.where(idx1 == e, w1, 0.0) + jnp.where(idx2 == e, w2, 0.0)
    hx_ref[:, D_MODEL:] = cpad_ref[...].T
    lo = jnp.minimum(idx1, idx2)
    hi = jnp.maximum(idx1, idx2)
    pair = jnp.where(lo == 0, 0, jnp.where(lo == 1, 3, 5)) + (hi - lo - 1)
    bucket = gid * N_PAIRS + pair
    bucket_ref[...] = bucket
    sub = lax.broadcasted_iota(jnp.int32, (CNT_ROWS, RT_TM), 0)
    onehot = jnp.where(sub == bucket, 1.0, 0.0)
    seen = jnp.dot(onehot.astype(BF16), tri_ref[...], preferred_element_type=F32)
    cnt = cnt_ref[...]
    rank_ref[...] = jnp.sum(onehot * (seen - 1.0 + cnt[:, 0:1]), axis=0, keepdims=True).astype(jnp.int32)
    cnt = cnt + jnp.sum(onehot, axis=1, keepdims=True)
    cnt_ref[...] = cnt
    cnt_out_ref[...] = cnt.astype(jnp.int32)


def _router_weights(w_grp, b_grp, w_exp, b_exp):
    w = jnp.zeros((RT_ROWS, D_MODEL), F32).at[0:N_GROUPS].set(w_grp.T).at[8:8 + N_EXPERTS].set(w_exp.T)
    b = jnp.zeros((RT_ROWS, 1), F32).at[0:N_GROUPS, 0].set(b_grp).at[8:8 + N_EXPERTS, 0].set(b_exp)
    return w, b


DISP_TM = 512


ROW_GROUP = 64


def _for_each_row(n_rows, fn):
    def group(k, carry):
        g0 = pl.multiple_of(k * ROW_GROUP, ROW_GROUP)
        for u in range(ROW_GROUP):
            fn(g0, u)
        return carry

    lax.fori_loop(0, n_rows // ROW_GROUP, group, 0)


def _dispatch_kernel(pos_ref, start_ref, cnt_ref, nt_ref, hx_hbm, hs_ref, buf_ref, zero_ref, blk_sem, row_sem, zsem):
    i = pl.program_id(0)
    last = pl.num_programs(0) - 1
    slot = i % 2
    base = i * DISP_TM

    def tile_in(tile, s):
        return pltpu.make_async_copy(hx_hbm.at[pl.ds(pl.multiple_of(tile * DISP_TM, DISP_TM), DISP_TM), :],
                                     buf_ref.at[s], blk_sem.at[s])

    def wait_rows(s):
        pltpu.make_async_copy(buf_ref.at[s], hs_ref.at[pl.ds(0, DISP_TM), :], row_sem.at[s]).wait()

    @pl.when(i == 0)
    def _():
        zero_ref[...] = jnp.zeros_like(zero_ref)

        def tile_copy(t):
            return pltpu.make_async_copy(zero_ref, hs_ref.at[pl.ds(pl.multiple_of(t * EXP_TM, EXP_TM), EXP_TM), :],
                                         zsem)

        def last_tile(g):
            return (start_ref[g] + cnt_ref[g] - 1) // EXP_TM

        def each_tile(fn):
            for g in range(N_BUCKETS):
                pl.when(cnt_ref[g] > 0)(functools.partial(fn, last_tile(g)))
            lax.fori_loop(nt_ref[0], EXP_TILES, lambda t, carry: (fn(t), carry)[1], 0)

        each_tile(lambda t: tile_copy(t).start())
        each_tile(lambda t: tile_copy(t).wait())
        tile_in(0, 0).start()

    @pl.when(i > 0)
    def _():
        wait_rows(1 - slot)

    @pl.when(i < last)
    def _():
        tile_in(i + 1, 1 - slot).start()

    tile_in(i, slot).wait()

    def issue(g0, u):
        src = buf_ref.at[slot, pl.ds(g0, ROW_GROUP), :]
        pltpu.make_async_copy(src.at[pl.ds(u, 1), :], hs_ref.at[pl.ds(pos_ref[base + g0 + u], 1), :],
                              row_sem.at[slot]).start(priority=u % 2)

    _for_each_row(DISP_TM, issue)

    @pl.when(i == last)
    def _():
        wait_rows(slot)


def dispatch(pos, starts, counts, n_tiles, hx):
    return pl.pallas_call(
        _dispatch_kernel,
        grid_spec=pltpu.PrefetchScalarGridSpec(
            num_scalar_prefetch=4,
            grid=(N_TOK // DISP_TM,),
            in_specs=[pl.BlockSpec(memory_space=pl.ANY)],
            out_specs=pl.BlockSpec(memory_space=pl.ANY),
            scratch_shapes=[pltpu.VMEM((2, DISP_TM, HX_W), F32), pltpu.VMEM((EXP_TM, HX_W), F32),
                            pltpu.SemaphoreType.DMA((2,)), pltpu.SemaphoreType.DMA((2,)),
                            pltpu.SemaphoreType.DMA(())],
        ),
        out_shape=jax.ShapeDtypeStruct((HS_ROWS, HX_W), F32),
        compiler_params=_params(1),
        name="dispatch",
    )(pos, starts, counts, n_tiles, hx)


(T_GROUP, T_LO, T_HI, T_SLOT, T_RUN_POS, T_NEXT_GROUP, T_FIRST, T_PREFETCHED) = range(8)


def _experts_kernel(layer, tab_ref, nt_ref, hs_ref, w1_hbm, w3_hbm, w2_hbm, ys_ref,
                    w1_ref, w3_ref, w2_ref, st1_ref, st3_ref, st2_ref, sem_ref):
    t = pl.program_id(0)
    valid = t < nt_ref[0]
    group = tab_ref[T_GROUP, t]
    slot = tab_ref[T_SLOT, t]
    run_pos = tab_ref[T_RUN_POS, t]
    next_group = tab_ref[T_NEXT_GROUP, t]

    def expert_copies(grp, e):
        idx = grp * EXPERTS_PER_GROUP + e
        return (pltpu.make_async_copy(w1_hbm.at[layer, idx], st1_ref, sem_ref.at[0]),
                pltpu.make_async_copy(w3_hbm.at[layer, idx], st3_ref, sem_ref.at[1]),
                pltpu.make_async_copy(w2_hbm.at[layer, idx], st2_ref, sem_ref.at[2]))

    def finish(copies, dst_slot, e):
        for cp in copies:
            cp.wait()
        w1_ref[dst_slot, e] = st1_ref[...].astype(BF16)
        w3_ref[dst_slot, e] = st3_ref[...].astype(BF16)
        w2_ref[dst_slot, e] = st2_ref[...].astype(BF16)

    @pl.when(jnp.logical_and(valid, tab_ref[T_FIRST, t] == 1))
    def _():
        def load(e, carry):
            copies = expert_copies(group, e)
            for cp in copies:
                cp.start()
            finish(copies, slot, e)
            return carry

        lax.fori_loop(tab_ref[T_PREFETCHED, t], EXPERTS_PER_GROUP, load, 0)

    prefetch = jnp.logical_and(valid, jnp.logical_and(next_group >= 0, run_pos < EXPERTS_PER_GROUP))

    @pl.when(prefetch)
    def _():
        for cp in expert_copies(next_group, run_pos):
            cp.start()

    @pl.when(valid)
    def _():
        h = hs_ref[:, 0:D_MODEL].astype(BF16)
        c = hs_ref[:, D_MODEL:]
        lane = lax.broadcasted_iota(jnp.int32, c.shape, 1)
        acc = None
        for e in (tab_ref[T_LO, t], tab_ref[T_HI, t]):
            ce = jnp.sum(jnp.where(lane == e, c, 0.0), axis=1, keepdims=True)
            a = jnp.dot(h, w1_ref[slot, e], preferred_element_type=F32)
            b = jnp.dot(h, w3_ref[slot, e], preferred_element_type=F32)
            hid = (a * _sigmoid(a)) * b * ce
            y = jnp.dot(hid.astype(BF16), w2_ref[slot, e], preferred_element_type=F32)
            acc = y if acc is None else acc + y
        ys_ref[...] = acc

    @pl.when(prefetch)
    def _():
        finish(expert_copies(next_group, run_pos), 1 - slot, run_pos)

    @pl.when(jnp.logical_not(valid))
    def _():
        ys_ref[...] = jnp.zeros_like(ys_ref)


def experts(tile_table, n_tiles, hs, w1, w3, w2, layer):
    hbm = pl.BlockSpec(memory_space=pl.ANY)
    return pl.pallas_call(
        functools.partial(_experts_kernel, layer),
        grid_spec=pltpu.PrefetchScalarGridSpec(
            num_scalar_prefetch=2,
            grid=(EXP_TILES,),
            in_specs=[pl.BlockSpec((EXP_TM, HX_W), lambda t, *_: (t, 0)), hbm, hbm, hbm],
            out_specs=pl.BlockSpec((EXP_TM, D_MODEL), lambda t, *_: (t, 0)),
            scratch_shapes=[
                pltpu.VMEM((2, EXPERTS_PER_GROUP, D_MODEL, D_EXPERT), BF16),
                pltpu.VMEM((2, EXPERTS_PER_GROUP, D_MODEL, D_EXPERT), BF16),
                pltpu.VMEM((2, EXPERTS_PER_GROUP, D_EXPERT, D_MODEL), BF16),
                pltpu.VMEM((D_MODEL, D_EXPERT), F32),
                pltpu.VMEM((D_MODEL, D_EXPERT), F32),
                pltpu.VMEM((D_EXPERT, D_MODEL), F32),
                pltpu.SemaphoreType.DMA((3,)),
            ],
        ),
        out_shape=jax.ShapeDtypeStruct((HS_ROWS, D_MODEL), F32),
        compiler_params=_params(1),
        name="experts",
    )(tile_table, n_tiles, hs, w1, w3, w2)


def _expert_tiles(counts):
    tiles = (counts + EXP_TM - 1) // EXP_TM
    ends = jnp.cumsum(tiles)
    n_tiles = ends[-1]
    t = jnp.arange(EXP_TILES, dtype=jnp.int32)
    used = t < n_tiles
    bucket = jnp.minimum(jnp.sum(t[:, None] >= ends[None, :], axis=1), N_BUCKETS - 1)
    group = bucket // N_PAIRS
    pair = bucket % N_PAIRS
    lo = sum(jnp.where(pair == k, p[0], 0) for k, p in enumerate(EXPERT_PAIRS))
    hi = sum(jnp.where(pair == k, p[1], 0) for k, p in enumerate(EXPERT_PAIRS))
    first = jnp.logical_and(used, jnp.logical_or(t == 0, group != jnp.roll(group, 1)))
    run = jnp.cumsum(first) - 1
    run_start = lax.cummax(jnp.where(first, t, 0))
    same_run = jnp.logical_and(run[:, None] == run[None, :], used[None, :])
    run_len = jnp.sum(same_run, axis=1)
    next_first = jnp.logical_and(first[None, :], run[None, :] == run[:, None] + 1)
    next_group = jnp.where(jnp.any(next_first, axis=1), jnp.sum(jnp.where(next_first, group[None, :], 0), axis=1), -1)
    prev_len = jnp.sum(jnp.where(t[None, :] == run_start[:, None] - 1, run_len[None, :], 0), axis=1)
    prefetched = jnp.where(run > 0, jnp.minimum(prev_len, EXPERTS_PER_GROUP), 0)
    table = jnp.stack([group, lo, hi, run % 2, t - run_start, next_group, first, prefetched]).astype(jnp.int32)
    return ((ends - tiles) * EXP_TM).astype(jnp.int32), table, n_tiles.reshape(1).astype(jnp.int32)


RES_TM = 512
RES_CTX_TILES = N_CTX // RES_TM


def _moe_residual_kernel(final, pos_ref, x_ref, mod_ref, gain_ref, ys_ref, *refs):
    out_refs, (ybuf_ref, sem_ref) = refs[:-2], refs[-2:]
    i = pl.program_id(0)
    slot = i % 2

    def gather_tile(tile, tile_slot):
        def issue(g0, u):
            dst = ybuf_ref.at[tile_slot, pl.ds(g0, ROW_GROUP), :]
            pltpu.make_async_copy(ys_ref.at[pl.ds(pos_ref[tile * RES_TM + g0 + u], 1), :], dst.at[pl.ds(u, 1), :],
                                  sem_ref.at[tile_slot]).start(priority=u % 2)

        _for_each_row(RES_TM, issue)

    @pl.when(i == 0)
    def _():
        gather_tile(0, 0)

    @pl.when(i + 1 < pl.num_programs(0))
    def _():
        gather_tile(i + 1, 1 - slot)

    pltpu.make_async_copy(ys_ref.at[pl.ds(0, RES_TM), :], ybuf_ref.at[slot], sem_ref.at[slot]).wait()
    x = x_ref[...] + mod_ref[:, 5 * D_MODEL:6 * D_MODEL] * ybuf_ref[slot]
    if not final:
        out_refs[0][...] = x
        return
    y = _rms(x, gain_ref[...])

    @pl.when(i < RES_CTX_TILES)
    def _():
        out_refs[0][...] = y

    @pl.when(i >= RES_CTX_TILES)
    def _():
        out_refs[1][...] = y


def moe_residual(pos, x, mod3, gain, ys, final):
    if final:
        out_specs = _split_specs((RES_TM, D_MODEL), RES_CTX_TILES)
        out_shape = [jax.ShapeDtypeStruct((N_CTX, D_MODEL), F32), jax.ShapeDtypeStruct((N_LAT, D_MODEL), F32)]
    else:
        out_specs = pl.BlockSpec((RES_TM, D_MODEL), lambda i, *_: (i, 0))
        out_shape = jax.ShapeDtypeStruct((N_TOK, D_MODEL), F32)
    return pl.pallas_call(
        functools.partial(_moe_residual_kernel, final),
        grid_spec=pltpu.PrefetchScalarGridSpec(
            num_scalar_prefetch=1,
            grid=(N_TOK // RES_TM,),
            in_specs=[
                pl.BlockSpec((RES_TM, D_MODEL), lambda i, *_: (i, 0)),
                pl.BlockSpec((None, 1, MOD_WIDTH), lambda i, *_: (_cond_row(i, RES_TM), 0, 0)),
                pl.BlockSpec((1, D_MODEL), lambda i, *_: (0, 0)),
                pl.BlockSpec(memory_space=pl.ANY),
            ],
            out_specs=out_specs,
            scratch_shapes=[pltpu.VMEM((2, RES_TM, D_MODEL), F32), pltpu.SemaphoreType.DMA((2,))],
        ),
        out_shape=out_shape,
        compiler_params=_params(1),
        name="moe_residual_final" if final else "moe_residual",
    )(pos, x, mod3, gain, ys)


def kernel(x_prompt, x_sample, cache_k, cache_v, state_lru, c, c_ctx, w_mod, b_mod, norm_mix, norm_ffn, w_in, sgu_norm, sgu_w, sgu_b, lru_conv_w, lru_conv_b, lru_w_r, lru_b_r, lru_w_i, lru_b_i, lru_lambda, na_rpb, w_branch_sgu, w_branch_lru, w_branch_na, w_out, moe_w_group, moe_b_group, moe_w_expert, moe_b_expert, moe_w1, moe_w3, moe_w2, final_norm_gain):
    xs = (x_prompt.reshape(N_CTX, D_MODEL), x_sample.reshape(N_LAT, D_MODEL))
    cond = jnp.zeros((N_COND, D_MODEL), F32).at[0].set(c_ctx).at[1:1 + DEC_BATCH].set(c)
    mod = modulation(cond, w_mod, b_mod)
    zero_state = jnp.zeros((BATCH, 2, LRU_WIDTH), F32)
    cache_k = cache_k.reshape(DEC_BATCH, DEPTH, PAST_LEN, NA_WIDTH)
    cache_v = cache_v.reshape(DEC_BATCH, DEPTH, PAST_LEN, NA_WIDTH)
    final_gain = final_norm_gain.reshape(1, D_MODEL)
    ks, vs, ss = [], [], []
    for l in range(DEPTH):
        mod3 = mod[l].reshape(N_COND, 1, MOD_WIDTH)
        ya, za, q, kb, vb, k_ctx, v_ctx, gates = in_projection(
            xs, mod3, norm_mix[l].reshape(1, D_MODEL), w_in, sgu_norm[l].reshape(1, SGU_WIDTH), sgu_w[l], sgu_b[l].T, l)
        w_lo, w_hi = _block_diag_gate_weights(lru_w_r[l], lru_w_i[l])
        lru_args = (lru_conv_w[l], lru_conv_b[l].reshape(1, LRU_WIDTH), w_lo, w_hi, lru_b_r[l], lru_b_i[l],
                    lru_lambda[l])
        yb_ctx, st_ctx = rglru(za, *lru_args, zero_state, SEQ, BATCH, 0)
        yb_lat, _ = rglru(za, *lru_args, state_lru[:, l], DEC_SEQ, DEC_BATCH, N_CTX)
        yc_ctx = context_attention(q, kb, vb)
        yc_lat = latent_attention(q, kb, vb, cache_k, cache_v, na_rpb[l].reshape(-1), l)
        w_rt, b_rt = _router_weights(moe_w_group[l], moe_b_group[l], moe_w_expert[l], moe_b_expert[l])
        x, hx, bucket, rank, counts = merge_branches(
            xs, mod3, ya, yb_ctx, yb_lat, yc_ctx, yc_lat, gates, w_branch_sgu, w_branch_lru, w_branch_na, w_out,
            norm_ffn[l].reshape(1, D_MODEL), w_rt, b_rt, l)
        bucket, rank, counts = bucket.reshape(N_TOK), rank.reshape(N_TOK), counts[:N_BUCKETS, 0]
        starts, tile_table, n_tiles = _expert_tiles(counts)
        pos = rank + jnp.sum(jnp.where(bucket[:, None] == jnp.arange(N_BUCKETS), starts[None, :], 0), axis=1)
        hs = dispatch(pos, starts, counts, n_tiles, hx)
        ys = experts(tile_table, n_tiles, hs, moe_w1, moe_w3, moe_w2, l)
        if l < DEPTH - 1:
            xs = (moe_residual(pos, x, mod3, final_gain, ys, False),)
        else:
            y_ctx, y_lat = moe_residual(pos, x, mod3, final_gain, ys, True)
        ks.append(k_ctx.reshape(BATCH, SEQ, NA_HEADS, NA_HEAD_DIM))
        vs.append(v_ctx.reshape(BATCH, SEQ, NA_HEADS, NA_HEAD_DIM))
        ss.append(st_ctx)
    return (y_ctx.reshape(BATCH, SEQ, D_MODEL), y_lat.reshape(DEC_BATCH, DEC_SEQ, D_MODEL),
            jnp.stack(ks, axis=1), jnp.stack(vs, axis=1), jnp.stack(ss, axis=1))
```

```python
import functools

import jax
import jax.numpy as jnp
import numpy as np
from jax import lax
from jax.experimental import pallas as pl
from jax.experimental.pallas import tpu as pltpu

F32 = jnp.float32
BF16 = jnp.bfloat16

D_MODEL = 1024
BATCH = 16
SEQ = 256
DEPTH = 2
DEC_BATCH = 4
DEC_SEQ = 2048
PAST_LEN = 512
GRID_W = 64
CHUNK = 128
SGU_WIDTH = 512
SGU_GROUPS = 4
LRU_WIDTH = 512
LRU_BLOCKS = 8
CONV_WIDTH = 4
LRU_C = 8.0
NA_HEADS = 8
NA_HEAD_DIM = 64
NA_WIDTH = 512
WIN_ROWS = 8
WIN_COLS = 16
N_GROUPS = 4
EXPERTS_PER_GROUP = 4
N_EXPERTS = 16
D_EXPERT = 512
IN_WIDTH = 6656
EPS = 1e-6
NEG_INF = -1e30

N_CTX = BATCH * SEQ
N_LAT = DEC_BATCH * DEC_SEQ
N_TOK = N_CTX + N_LAT
N_COND = 8
MOD_WIDTH = 6 * D_MODEL
GRID_ROWS = DEC_SEQ // GRID_W

VMEM_LIMIT_BYTES = 56 * 1024 * 1024


def _params(n_axes):
    return pltpu.CompilerParams(dimension_semantics=("arbitrary",) * n_axes,
                                vmem_limit_bytes=VMEM_LIMIT_BYTES)


def _cond_row(tile, tile_rows):
    tok = tile * tile_rows
    return jnp.where(tok < N_CTX, 0, 1 + (tok - N_CTX) // DEC_SEQ)


def _rms(x, gain):
    return x * lax.rsqrt(jnp.mean(x * x, axis=-1, keepdims=True) + EPS) * gain


def _bdot(a, b):
    return jnp.dot(a.astype(BF16), b.astype(BF16), preferred_element_type=F32)


def _bdot_t(a, b):
    return lax.dot_general(a.astype(BF16), b.astype(BF16), (((1,), (1,)), ((), ())),
                           preferred_element_type=F32)


MOD_TN = 1536


def _mod_kernel(cond_ref, w_ref, b_ref, o_ref):
    c = cond_ref[...]
    s = c * jax.nn.sigmoid(c)
    o_ref[...] = _bdot(s, w_ref[...]) + b_ref[...]


def modulation(cond, w_mod, b_mod):
    return pl.pallas_call(
        _mod_kernel,
        grid=(DEPTH, MOD_WIDTH // MOD_TN),
        in_specs=[
            pl.BlockSpec((N_COND, D_MODEL), lambda l, j: (0, 0)),
            pl.BlockSpec((None, D_MODEL, MOD_TN), lambda l, j: (l, 0, j)),
            pl.BlockSpec((None, 1, MOD_TN), lambda l, j: (l, 0, j)),
        ],
        out_specs=pl.BlockSpec((None, N_COND, MOD_TN), lambda l, j: (l, 0, j)),
        out_shape=jax.ShapeDtypeStruct((DEPTH, N_COND, MOD_WIDTH), F32),
        compiler_params=_params(2),
        name="modulation",
    )(cond, w_mod, b_mod.reshape(DEPTH, 1, MOD_WIDTH))


IN_TM = 256
IN_CTX_TILES = N_CTX // IN_TM
ZA_WIDTH = 4 * 512
KV_OFF = ZA_WIDTH + NA_WIDTH
GATE_OFF = KV_OFF + 2 * NA_WIDTH
W_CHUNK = 512
Q_SCALE = NA_HEAD_DIM ** -0.5 * float(np.log2(np.e))


def _load_cast(chunks, stage_ref, sem_ref):
    def copy(j):
        return pltpu.make_async_copy(chunks[j][0], stage_ref.at[j % 2], sem_ref.at[j % 2])

    copy(0).start()
    for j in range(len(chunks)):
        if j + 1 < len(chunks):
            copy(j + 1).start()
        copy(j).wait()
        chunks[j][1](stage_ref[j % 2].astype(BF16))


def _token_tile(i, n_ctx_tiles, refs):
    if len(refs) == 1:
        return refs[0][...]
    return jnp.where(i < n_ctx_tiles, refs[0][...], refs[1][...])


def _split_specs(block, n_ctx_tiles):
    return [pl.BlockSpec(block, lambda i, *_: (jnp.minimum(i, n_ctx_tiles - 1), 0)),
            pl.BlockSpec(block, lambda i, *_: (jnp.maximum(i - n_ctx_tiles, 0), 0))]


SGU_GD = SGU_WIDTH // SGU_GROUPS


def _spatial_gating(u, v, gain, ws_ref, bs_ref):
    u = jax.nn.gelu(u)
    v = _rms(jax.nn.gelu(v), gain).astype(BF16)
    out = []
    for g in range(SGU_GROUPS):
        cols = slice(g * SGU_GD, (g + 1) * SGU_GD)
        mixed = jnp.dot(ws_ref[g].astype(BF16), v[:, cols], preferred_element_type=F32) + bs_ref[:, g:g + 1]
        out.append(u[:, cols] * mixed)
    return jnp.concatenate(out, axis=1)


def _inproj_kernel(layer, n_x, *refs):
    x_refs, (mod_ref, gain_ref, w_hbm, sgu_gain_ref, ws_ref, bs_ref) = refs[:n_x], refs[n_x:n_x + 6]
    ya_ref, zb_ref, q_ref, kb_ref, vb_ref, kc_ref, vc_ref, g_ref, w_ref, stage_ref, sem_ref = refs[n_x + 6:]
    i = pl.program_id(0)

    @pl.when(i == 0)
    def _():
        def store(c):
            def st(v):
                w_ref[:, c * W_CHUNK:(c + 1) * W_CHUNK] = v
            return st

        _load_cast([(w_hbm.at[layer, :, pl.ds(c * W_CHUNK, W_CHUNK)], store(c)) for c in range(IN_WIDTH // W_CHUNK)],
                   stage_ref, sem_ref)

    m = mod_ref[...]
    shift, scale = m[:, 0:D_MODEL], m[:, D_MODEL:2 * D_MODEL]
    h = (_rms(_token_tile(i, IN_CTX_TILES, x_refs), gain_ref[...]) * (1.0 + scale) + shift).astype(BF16)
    za = jnp.dot(h, w_ref[:, 0:ZA_WIDTH], preferred_element_type=F32)
    for c in range(IN_TM // CHUNK):
        rows = slice(c * CHUNK, (c + 1) * CHUNK)
        ya_ref[rows, :] = _spatial_gating(za[rows, 0:SGU_WIDTH], za[rows, SGU_WIDTH:2 * SGU_WIDTH],
                                          sgu_gain_ref[...], ws_ref, bs_ref).astype(BF16)
    zb_ref[...] = za[:, 2 * SGU_WIDTH:].astype(BF16)
    q_ref[...] = (jnp.dot(h, w_ref[:, ZA_WIDTH:KV_OFF], preferred_element_type=F32) * Q_SCALE).astype(BF16)
    kv = jnp.dot(h, w_ref[:, KV_OFF:GATE_OFF], preferred_element_type=F32)
    kb_ref[...] = kv[:, 0:NA_WIDTH].astype(BF16)
    vb_ref[...] = kv[:, NA_WIDTH:].astype(BF16)
    g_ref[...] = jnp.dot(h, w_ref[:, GATE_OFF:], preferred_element_type=F32).astype(BF16)

    @pl.when(i < IN_CTX_TILES)
    def _():
        kc_ref[...] = kv[:, 0:NA_WIDTH]
        vc_ref[...] = kv[:, NA_WIDTH:]


def in_projection(xs, mod3, gain, w_in, sgu_gain, sgu_w, sgu_b_t, layer):
    row = lambda i: (i, 0)
    ctx_row = lambda i: (jnp.minimum(i, IN_CTX_TILES - 1), 0)
    const = lambda shape: pl.BlockSpec(shape, lambda i: (0,) * len(shape))
    x_specs = ([pl.BlockSpec((IN_TM, D_MODEL), row)] if len(xs) == 1
               else _split_specs((IN_TM, D_MODEL), IN_CTX_TILES))
    return pl.pallas_call(
        functools.partial(_inproj_kernel, layer, len(xs)),
        grid=(N_TOK // IN_TM,),
        in_specs=x_specs + [
            pl.BlockSpec((None, 1, MOD_WIDTH), lambda i: (_cond_row(i, IN_TM), 0, 0)),
            const((1, D_MODEL)),
            pl.BlockSpec(memory_space=pl.ANY),
            const((1, SGU_WIDTH)),
            const((SGU_GROUPS, CHUNK, CHUNK)),
            const((CHUNK, SGU_GROUPS)),
        ],
        out_specs=[
            pl.BlockSpec((IN_TM, SGU_WIDTH), row),
            pl.BlockSpec((IN_TM, 2 * LRU_WIDTH), row),
            pl.BlockSpec((IN_TM, NA_WIDTH), row),
            pl.BlockSpec((IN_TM, NA_WIDTH), row),
            pl.BlockSpec((IN_TM, NA_WIDTH), row),
            pl.BlockSpec((IN_TM, NA_WIDTH), ctx_row),
            pl.BlockSpec((IN_TM, NA_WIDTH), ctx_row),
            pl.BlockSpec((IN_TM, 3 * D_MODEL), row),
        ],
        out_shape=[
            jax.ShapeDtypeStruct((N_TOK, SGU_WIDTH), BF16),
            jax.ShapeDtypeStruct((N_TOK, 2 * LRU_WIDTH), BF16),
            jax.ShapeDtypeStruct((N_TOK, NA_WIDTH), BF16),
            jax.ShapeDtypeStruct((N_TOK, NA_WIDTH), BF16),
            jax.ShapeDtypeStruct((N_TOK, NA_WIDTH), BF16),
            jax.ShapeDtypeStruct((N_CTX, NA_WIDTH), F32),
            jax.ShapeDtypeStruct((N_CTX, NA_WIDTH), F32),
            jax.ShapeDtypeStruct((N_TOK, 3 * D_MODEL), BF16),
        ],
        scratch_shapes=[pltpu.VMEM((D_MODEL, IN_WIDTH), BF16), pltpu.VMEM((2, D_MODEL, W_CHUNK), F32),
                        pltpu.SemaphoreType.DMA((2,))],
        compiler_params=_params(1),
        name="in_projection",
    )(*xs, mod3, gain, w_in, sgu_gain, sgu_w, sgu_b_t)


LRU_TC = 256
LRU_HALF = 256
SUB = 8
HALO = 8
LANES = 128
LRU_SLABS = LRU_WIDTH // LANES


def _sigmoid(x):
    return 0.5 * jnp.tanh(0.5 * x) + 0.5


def _lru_kernel(seq_len, xr_ref, gr_ref, cw_ref, cb_ref, wlo_ref, whi_ref, br_ref, bi_ref,
                lam_ref, h0_ref, y_ref, st_ref, xp_ref, a_ref, b_ref, h_ref):
    seg_len = seq_len // SUB
    pitch = seg_len + SUB
    segs_per_chunk = max(LRU_TC // seg_len, 1)
    n_chunks = seq_len // LRU_TC
    zeros = jnp.zeros((HALO, LRU_WIDTH), F32)
    xp_ref[0:HALO, :] = zeros
    xp_ref[seq_len + HALO:seq_len + 2 * HALO, :] = zeros

    def copy_in(c, carry):
        r0 = pl.multiple_of(c * LRU_TC, LRU_TC)
        xp_ref[pl.ds(r0 + HALO, LRU_TC), :] = xr_ref[pl.ds(r0, LRU_TC), :].astype(F32)
        return carry

    lax.fori_loop(0, n_chunks, copy_in, 0)

    def chunk_rows(c, seg):
        n = min(seg_len, LRU_TC)
        start = pl.multiple_of((c * segs_per_chunk + seg) * pitch, SUB)
        return pl.ds(start, n), slice(seg * n, (seg + 1) * n)

    cw = cw_ref[...]
    cb = cb_ref[...]
    win = LRU_TC + 2 * HALO
    neg_lam = -lam_ref[...]
    softplus = jnp.maximum(neg_lam, 0.0) + jnp.log1p(jnp.exp(-jnp.abs(neg_lam)))
    decay = (0.5 * LRU_C) * softplus
    half_br = 0.5 * br_ref[...]
    half_bi = 0.5 * bi_ref[...]

    def gates(c, carry):
        r0 = pl.multiple_of(c * LRU_TC, LRU_TC)
        w = xp_ref[pl.ds(r0, win), :]
        xc = (cw[0:1, :] * pltpu.roll(w, 1, 0)[HALO:HALO + LRU_TC]
              + cw[1:2, :] * w[HALO:HALO + LRU_TC]
              + cw[2:3, :] * pltpu.roll(w, win - 1, 0)[HALO:HALO + LRU_TC]
              + cw[3:4, :] * pltpu.roll(w, win - 2, 0)[HALO:HALO + LRU_TC]) + cb
        xb = xc.astype(BF16)
        for half, w_ref in enumerate((wlo_ref, whi_ref)):
            cols = slice(half * LRU_HALF, (half + 1) * LRU_HALF)
            pre = jnp.dot(xb[:, cols], w_ref[...], preferred_element_type=F32)
            half_x = 0.5 * xc[:, cols]
            for d in range(2):
                tr = jnp.tanh(pre[:, (2 * d) * LRU_HALF:(2 * d + 1) * LRU_HALF] + half_br[d:d + 1, cols])
                ti = jnp.tanh(pre[:, (2 * d + 1) * LRU_HALF:(2 * d + 2) * LRU_HALF] + half_bi[d:d + 1, cols])
                neg_log_a = decay[d:d + 1, cols] * tr + decay[d:d + 1, cols]
                a = jnp.exp2(neg_log_a * -LOG2E)
                z = jnp.tanh(neg_log_a) * (a * a + 1.0)
                b = jnp.where(z > 0.0, z * lax.rsqrt(z), 0.0) * (half_x * ti + half_x)
                for k in range(LRU_HALF // LANES):
                    slab = half * (LRU_HALF // LANES) + k
                    lanes = slice(k * LANES, (k + 1) * LANES)
                    for seg in range(segs_per_chunk):
                        dst, src = chunk_rows(c, seg)
                        a_ref[d, slab, dst, :] = a[src, lanes]
                        b_ref[d, slab, dst, :] = b[src, lanes]
        return carry

    lax.fori_loop(0, n_chunks, gates, 0)

    def step_rows(d, j):
        return pl.ds(j if d == 0 else seg_len - 1 - j, SUB, stride=pitch)

    def reduce_step(j, carry):
        out = []
        for d in range(2):
            for slab in range(LRU_SLABS):
                big_a, big_b = carry[d * LRU_SLABS + slab]
                a = a_ref[d, slab, step_rows(d, j), :]
                b = b_ref[d, slab, step_rows(d, j), :]
                out.append((a * big_a, a * big_b + b))
        return tuple(out)

    ident = (jnp.ones((SUB, LANES), F32), jnp.zeros((SUB, LANES), F32))
    totals = lax.fori_loop(0, seg_len, reduce_step, (ident,) * (2 * LRU_SLABS), unroll=8)

    sub = lax.broadcasted_iota(jnp.int32, (SUB, LANES), 0)
    entering = []
    for d in range(2):
        for slab in range(LRU_SLABS):
            big_a, big_b = totals[d * LRU_SLABS + slab]
            h = h0_ref[d:d + 1, slab * LANES:(slab + 1) * LANES]
            rows = jnp.zeros((SUB, LANES), F32)
            for s in (range(SUB) if d == 0 else reversed(range(SUB))):
                rows = jnp.where(sub == s, h, rows)
                h = big_a[s:s + 1, :] * h + big_b[s:s + 1, :]
            entering.append(rows)
            st_ref[d:d + 1, slab * LANES:(slab + 1) * LANES] = h

    def apply_step(j, carry):
        out = []
        for d in range(2):
            for slab in range(LRU_SLABS):
                h = (a_ref[d, slab, step_rows(d, j), :] * carry[d * LRU_SLABS + slab]
                     + b_ref[d, slab, step_rows(d, j), :])
                h_ref[d, slab, step_rows(d, j), :] = h
                out.append(h)
        return tuple(out)

    lax.fori_loop(0, seg_len, apply_step, tuple(entering), unroll=8)

    def merge(c, carry):
        r0 = pl.multiple_of(c * LRU_TC, LRU_TC)
        gate = jax.nn.gelu(gr_ref[pl.ds(r0, LRU_TC), :].astype(F32))
        for slab in range(LRU_SLABS):
            lanes = slice(slab * LANES, (slab + 1) * LANES)
            for seg in range(segs_per_chunk):
                src, dst = chunk_rows(c, seg)
                h = h_ref[0, slab, src, :] + h_ref[1, slab, src, :]
                y_ref[pl.ds(r0 + dst.start, dst.stop - dst.start), lanes] = (h * gate[dst, lanes]).astype(BF16)
        return carry

    lax.fori_loop(0, n_chunks, merge, 0)


def rglru(za, conv_w, conv_b, w_lo, w_hi, b_r, b_i, lam, h0, seq_len, n_seq, tok_off):
    blk0 = tok_off // seq_len
    full = lambda shape: pl.BlockSpec(shape, lambda i: (0,) * len(shape))
    return pl.pallas_call(
        functools.partial(_lru_kernel, seq_len),
        grid=(n_seq,),
        in_specs=[
            pl.BlockSpec((seq_len, LRU_WIDTH), lambda i: (i + blk0, 0)),
            pl.BlockSpec((seq_len, LRU_WIDTH), lambda i: (i + blk0, 1)),
            full((CONV_WIDTH, LRU_WIDTH)),
            full((1, LRU_WIDTH)),
            full((LRU_HALF, 4 * LRU_HALF)),
            full((LRU_HALF, 4 * LRU_HALF)),
            full((2, LRU_WIDTH)),
            full((2, LRU_WIDTH)),
            full((2, LRU_WIDTH)),
            pl.BlockSpec((None, 2, LRU_WIDTH), lambda i: (i, 0, 0)),
        ],
        out_specs=[
            pl.BlockSpec((seq_len, LRU_WIDTH), lambda i: (i, 0)),
            pl.BlockSpec((None, 2, LRU_WIDTH), lambda i: (i, 0, 0)),
        ],
        out_shape=[
            jax.ShapeDtypeStruct((n_seq * seq_len, LRU_WIDTH), BF16),
            jax.ShapeDtypeStruct((n_seq, 2, LRU_WIDTH), F32),
        ],
        scratch_shapes=[
            pltpu.VMEM((seq_len + 2 * HALO, LRU_WIDTH), F32),
            pltpu.VMEM((2, LRU_SLABS, seq_len + SUB * SUB, LANES), F32),
            pltpu.VMEM((2, LRU_SLABS, seq_len + SUB * SUB, LANES), F32),
            pltpu.VMEM((2, LRU_SLABS, seq_len + SUB * SUB, LANES), F32),
        ],
        compiler_params=_params(1),
        name=f"rglru_{seq_len}",
    )(za, za, conv_w, conv_b, w_lo, w_hi, b_r, b_i, lam, h0)


def _block_diag_gate_weights(w_r, w_i):
    per_half = LRU_BLOCKS // 2
    eye = jnp.eye(per_half, dtype=w_r.dtype)

    def bd(w):
        return jnp.einsum("hij,hg->higj", w, eye).reshape(LRU_HALF, LRU_HALF)

    tiles = []
    for half in range(2):
        sl = slice(half * per_half, (half + 1) * per_half)
        tiles.append(jnp.concatenate([bd(w_r[0, sl]), bd(w_i[0, sl]), bd(w_r[1, sl]), bd(w_i[1, sl])], axis=1))
    return (0.5 * tiles[0]).astype(BF16), (0.5 * tiles[1]).astype(BF16)


PAIR_W = 2 * NA_HEAD_DIM
LOG2E = float(np.log2(np.e))


def _head_lanes(shape, head):
    lane = lax.broadcasted_iota(jnp.int32, shape, 1)
    return (lane < NA_HEAD_DIM) if head == 0 else (lane >= NA_HEAD_DIM)


def _one_head(q_pair, head):
    return jnp.where(_head_lanes(q_pair.shape, head), q_pair.astype(F32), 0.0).astype(BF16)


def _join_heads(o0, o1):
    return jnp.where(_head_lanes(o0.shape, 0), o0, o1)


def _ctx_attn_kernel(q_ref, k_ref, v_ref, o_ref):
    for j in range(NA_HEADS // 2):
        cols = slice(j * PAIR_W, (j + 1) * PAIR_W)
        outs = []
        for head in range(2):
            s = _bdot_t(_one_head(q_ref[:, cols], head), k_ref[:, cols])
            p = jnp.exp2(s - jnp.max(s, axis=-1, keepdims=True))
            denom = jnp.sum(p, axis=-1, keepdims=True)
            outs.append(jnp.dot(p.astype(BF16), v_ref[:, cols], preferred_element_type=F32) / denom)
        o_ref[:, cols] = _join_heads(*outs).astype(BF16)


def context_attention(q, kb, vb):
    spec = pl.BlockSpec((SEQ, NA_WIDTH), lambda i: (i, 0))
    return pl.pallas_call(
        _ctx_attn_kernel,
        grid=(BATCH,),
        in_specs=[spec, spec, spec],
        out_specs=spec,
        out_shape=jax.ShapeDtypeStruct((N_CTX, NA_WIDTH), BF16),
        compiler_params=_params(1),
        name="context_attention",
    )(q, kb, vb)


NA_QROWS = 4
NA_TQ = NA_QROWS * GRID_W
NA_KROWS = NA_QROWS + WIN_ROWS
NA_TK = NA_KROWS * GRID_W
NA_QBLOCKS = GRID_ROWS // NA_QROWS


N_DROW = 2 * WIN_ROWS - 1
N_DCOL = 2 * WIN_COLS - 1
NA_BLOCK_KINDS = (0, 1, NA_QBLOCKS - 1)


def _first_key_row(qb):
    return (np.clip if isinstance(qb, int) else jnp.clip)(qb * NA_QROWS - WIN_ROWS // 2, 0, GRID_ROWS - NA_KROWS)


def _build_bias(rpb_ref, table_ref, bias_ref):
    qc = lax.broadcasted_iota(jnp.int32, (GRID_W, GRID_W), 0)
    kc = lax.broadcasted_iota(jnp.int32, (GRID_W, GRID_W), 1)
    col0 = jnp.clip(qc - WIN_COLS // 2, 0, GRID_W - WIN_COLS)
    col_ok = jnp.logical_and(kc >= col0, kc < col0 + WIN_COLS)
    neg = jnp.full((GRID_W, GRID_W), NEG_INF, F32)

    def table_entry(idx, carry):
        rows = jnp.broadcast_to(rpb_ref[pl.ds(idx, 1), :], (GRID_W, LANES))
        t = pltpu.roll(rows, LANES - (WIN_COLS - 1), 1, stride=1, stride_axis=0)
        table_ref[idx] = jnp.where(col_ok, t[:, 0:GRID_W] * LOG2E, neg)
        return carry

    lax.fori_loop(0, NA_HEADS * N_DROW, table_entry, 0)

    def head_blocks(h, carry):
        for kind, qb in enumerate(NA_BLOCK_KINDS):
            for i in range(NA_QROWS):
                qrow = qb * NA_QROWS + i
                win0 = int(np.clip(qrow - WIN_ROWS // 2, 0, GRID_ROWS - WIN_ROWS))
                for kr in range(NA_KROWS):
                    krow = int(_first_key_row(qb)) + kr
                    inside = win0 <= krow < win0 + WIN_ROWS
                    blk = table_ref[h * N_DROW + (krow - qrow + WIN_ROWS - 1)] if inside else neg
                    bias_ref[kind, h, i * GRID_W:(i + 1) * GRID_W, kr * GRID_W:(kr + 1) * GRID_W] = blk
        return carry

    lax.fori_loop(0, NA_HEADS, head_blocks, 0)


def _lat_attn_kernel(rpb_ref, q_ref, k_ref, v_ref, ck_ref, cv_ref, o_ref, table_ref, bias_ref):
    qb = pl.program_id(1)

    @pl.when(jnp.logical_and(pl.program_id(0) == 0, qb == 0))
    def _():
        _build_bias(rpb_ref, table_ref, bias_ref)

    kind = jnp.where(qb == 0, 0, jnp.where(qb == NA_QBLOCKS - 1, 2, 1))
    k0 = pl.multiple_of(_first_key_row(qb) * GRID_W, GRID_W)
    for j in range(NA_HEADS // 2):
        cols = slice(j * PAIR_W, (j + 1) * PAIR_W)
        k_loc = k_ref[pl.ds(k0, NA_TK), cols]
        v_loc = v_ref[pl.ds(k0, NA_TK), cols]
        k_ctx = ck_ref[:, cols].astype(BF16)
        v_ctx = cv_ref[:, cols].astype(BF16)
        outs = []
        for head in range(2):
            qh = _one_head(q_ref[:, cols], head)
            s_loc = _bdot_t(qh, k_loc) + bias_ref[kind, 2 * j + head]
            s_ctx = _bdot_t(qh, k_ctx)
            m = jnp.maximum(jnp.max(s_loc, axis=-1, keepdims=True), jnp.max(s_ctx, axis=-1, keepdims=True))
            p_loc = jnp.exp2(s_loc - m)
            p_ctx = jnp.exp2(s_ctx - m)
            denom = jnp.sum(p_loc, axis=-1, keepdims=True) + jnp.sum(p_ctx, axis=-1, keepdims=True)
            o = (jnp.dot(p_loc.astype(BF16), v_loc, preferred_element_type=F32)
                 + jnp.dot(p_ctx.astype(BF16), v_ctx, preferred_element_type=F32))
            outs.append(o / denom)
        o_ref[:, cols] = _join_heads(*outs).astype(BF16)


def latent_attention(q, kb, vb, cache_k, cache_v, rpb, layer):
    q_blk0 = N_CTX // NA_TQ
    s_blk0 = N_CTX // DEC_SEQ
    cache_spec = pl.BlockSpec((None, None, PAST_LEN, NA_WIDTH), lambda b, m: (b, layer, 0, 0))
    return pl.pallas_call(
        _lat_attn_kernel,
        grid=(DEC_BATCH, NA_QBLOCKS),
        in_specs=[
            pl.BlockSpec((NA_HEADS * N_DROW, LANES), lambda b, m: (0, 0)),
            pl.BlockSpec((NA_TQ, NA_WIDTH), lambda b, m: (q_blk0 + b * NA_QBLOCKS + m, 0)),
            pl.BlockSpec((DEC_SEQ, NA_WIDTH), lambda b, m: (s_blk0 + b, 0)),
            pl.BlockSpec((DEC_SEQ, NA_WIDTH), lambda b, m: (s_blk0 + b, 0)),
            cache_spec,
            cache_spec,
        ],
        out_specs=pl.BlockSpec((NA_TQ, NA_WIDTH), lambda b, m: (b * NA_QBLOCKS + m, 0)),
        out_shape=jax.ShapeDtypeStruct((N_LAT, NA_WIDTH), BF16),
        scratch_shapes=[pltpu.VMEM((NA_HEADS * N_DROW, GRID_W, GRID_W), F32),
                        pltpu.VMEM((len(NA_BLOCK_KINDS), NA_HEADS, NA_TQ, NA_TK), F32)],
        compiler_params=_params(2),
        name="latent_attention",
    )(rpb, q, kb, vb, cache_k, cache_v)


MERGE_TM = 512


MERGE_CTX_TILES = N_CTX // MERGE_TM
BRANCH_W = 512


def _merge_kernel(layer, n_x, *refs):
    x_refs, rest = refs[:n_x], refs[n_x:]
    (mod_ref, ya_ref, yb_ctx_ref, yb_lat_ref, yc_ctx_ref, yc_lat_ref, g_ref, wa_hbm, wb_hbm, wc_hbm, wo_hbm,
     ffn_gain_ref, wrt_ref, brt_ref, tri_ref,
     o_ref, hx_ref, bucket_ref, rank_ref, cnt_out_ref,
     wbr_ref, wo_ref, stage_ref, sem_ref, cnt_ref, cpad_ref) = rest
    i = pl.program_id(0)

    @pl.when(i == 0)
    def _():
        def store_branch(k):
            def st(v):
                wbr_ref[k] = v
            return st

        def store_out(k):
            def st(v):
                wo_ref[k * BRANCH_W:(k + 1) * BRANCH_W, :] = v
            return st

        chunks = [(w.at[layer], store_branch(k)) for k, w in enumerate((wa_hbm, wb_hbm, wc_hbm))]
        chunks += [(wo_hbm.at[layer, pl.ds(k * BRANCH_W, BRANCH_W), :], store_out(k))
                   for k in range(D_MODEL // BRANCH_W)]
        _load_cast(chunks, stage_ref, sem_ref)

    g = _sigmoid(g_ref[...].astype(F32))
    yb = _token_tile(i, MERGE_CTX_TILES, (yb_ctx_ref, yb_lat_ref))
    yc = _token_tile(i, MERGE_CTX_TILES, (yc_ctx_ref, yc_lat_ref))
    merged = (g[:, 0:D_MODEL] * jnp.dot(ya_ref[...], wbr_ref[0], preferred_element_type=F32)
              + g[:, D_MODEL:2 * D_MODEL] * jnp.dot(yb, wbr_ref[1], preferred_element_type=F32)
              + g[:, 2 * D_MODEL:] * jnp.dot(yc, wbr_ref[2], preferred_element_type=F32))
    y = jnp.dot(merged.astype(BF16), wo_ref[...], preferred_element_type=F32)
    gate = mod_ref[:, 2 * D_MODEL:3 * D_MODEL]
    x = _token_tile(i, MERGE_CTX_TILES, x_refs) + gate * y
    o_ref[...] = x
    _route(x, mod_ref, ffn_gain_ref, wrt_ref, brt_ref, tri_ref, hx_ref, bucket_ref, rank_ref, cnt_out_ref,
           cnt_ref, cpad_ref)


def merge_branches(xs, mod3, ya, yb_ctx, yb_lat, yc_ctx, yc_lat, gates, wa, wb, wc, wo, ffn_gain, w_rt, b_rt, layer):
    assert MERGE_TM == RT_TM
    row = lambda i: (i, 0)
    const = lambda shape: pl.BlockSpec(shape, lambda i: (0,) * len(shape))
    tri = jnp.asarray(np.triu(np.ones((RT_TM, RT_TM), np.float32)), BF16)
    hbm = pl.BlockSpec(memory_space=pl.ANY)
    x_specs = ([pl.BlockSpec((MERGE_TM, D_MODEL), row)] if len(xs) == 1
               else _split_specs((MERGE_TM, D_MODEL), MERGE_CTX_TILES))
    return pl.pallas_call(
        functools.partial(_merge_kernel, layer, len(xs)),
        grid=(N_TOK // MERGE_TM,),
        in_specs=x_specs + [
            pl.BlockSpec((None, 1, MOD_WIDTH), lambda i: (_cond_row(i, MERGE_TM), 0, 0)),
            pl.BlockSpec((MERGE_TM, SGU_WIDTH), row),
            *_split_specs((MERGE_TM, LRU_WIDTH), MERGE_CTX_TILES),
            *_split_specs((MERGE_TM, NA_WIDTH), MERGE_CTX_TILES),
            pl.BlockSpec((MERGE_TM, 3 * D_MODEL), row),
            hbm, hbm, hbm, hbm,
            const((1, D_MODEL)),
            const((RT_ROWS, D_MODEL)),
            const((RT_ROWS, 1)),
            const((RT_TM, RT_TM)),
        ],
        out_specs=[
            pl.BlockSpec((MERGE_TM, D_MODEL), row),
            pl.BlockSpec((RT_TM, HX_W), row),
            pl.BlockSpec((1, RT_TM), lambda i: (0, i)),
            pl.BlockSpec((1, RT_TM), lambda i: (0, i)),
            const((CNT_ROWS, COMB_W)),
        ],
        out_shape=[
            jax.ShapeDtypeStruct((N_TOK, D_MODEL), F32),
            jax.ShapeDtypeStruct((N_TOK, HX_W), F32),
            jax.ShapeDtypeStruct((1, N_TOK), jnp.int32),
            jax.ShapeDtypeStruct((1, N_TOK), jnp.int32),
            jax.ShapeDtypeStruct((CNT_ROWS, COMB_W), jnp.int32),
        ],
        scratch_shapes=[pltpu.VMEM((3, BRANCH_W, D_MODEL), BF16), pltpu.VMEM((D_MODEL, D_MODEL), BF16),
                        pltpu.VMEM((2, BRANCH_W, D_MODEL), F32), pltpu.SemaphoreType.DMA((2,)),
                        pltpu.VMEM((CNT_ROWS, COMB_W), F32), pltpu.VMEM((COMB_W, RT_TM), F32)],
        compiler_params=_params(1),
        name="merge_branches",
    )(*xs, mod3, ya, yb_ctx, yb_lat, yc_ctx, yc_lat, gates, wa, wb, wc, wo, ffn_gain, w_rt, b_rt, tri)


RT_TM = 512
RT_ROWS = 32
EXPERT_PAIRS = ((0, 1), (0, 2), (0, 3), (1, 2), (1, 3), (2, 3))
N_PAIRS = len(EXPERT_PAIRS)
N_BUCKETS = N_GROUPS * N_PAIRS
CNT_ROWS = 32
COMB_W = 128
HX_W = D_MODEL + COMB_W
EXP_TM = 256
EXP_TILES = N_TOK // EXP_TM + N_BUCKETS
HS_ROWS = EXP_TILES * EXP_TM


def _split_bf16(x):
    hi = x.astype(BF16)
    return hi, (x - hi.astype(F32)).astype(BF16)


def _route(x, mod_ref, gain_ref, w_ref, b_ref, tri_ref, hx_ref, bucket_ref, rank_ref, cnt_out_ref,
           cnt_ref, cpad_ref):
    @pl.when(pl.program_id(0) == 0)
    def _():
        cnt_ref[...] = jnp.zeros_like(cnt_ref)
        cpad_ref[...] = jnp.zeros_like(cpad_ref)

    m = mod_ref[...]
    shift, scale = m[:, 3 * D_MODEL:4 * D_MODEL], m[:, 4 * D_MODEL:5 * D_MODEL]
    h = _rms(x, gain_ref[...]) * (1.0 + scale) + shift
    hx_ref[:, 0:D_MODEL] = h
    h_hi, h_lo = _split_bf16(h)
    w_hi, w_lo = _split_bf16(w_ref[...])
    dims = (((1,), (1,)), ((), ()))
    logits = (lax.dot_general(w_hi, h_hi, dims, preferred_element_type=F32)
              + lax.dot_general(w_hi, h_lo, dims, preferred_element_type=F32)
              + lax.dot_general(w_lo, h_hi, dims, preferred_element_type=F32)) + b_ref[...]
    gl = [logits[g:g + 1, :] for g in range(N_GROUPS)]
    gmax = functools.reduce(jnp.maximum, gl)
    gid = jnp.full(gmax.shape, N_GROUPS - 1, jnp.int32)
    for g in reversed(range(N_GROUPS - 1)):
        gid = jnp.where(gl[g] == gmax, g, gid)
    p_grp = 1.0 / functools.reduce(jnp.add, [jnp.exp(v - gmax) for v in gl])
    el = []
    for e in range(EXPERTS_PER_GROUP):
        v = logits[8 + e:9 + e, :]
        for g in range(1, N_GROUPS):
            row = 8 + g * EXPERTS_PER_GROUP + e
            v = jnp.where(gid == g, logits[row:row + 1, :], v)
        el.append(v)
    top1 = functools.reduce(jnp.maximum, el)
    idx1 = jnp.full(top1.shape, EXPERTS_PER_GROUP - 1, jnp.int32)
    for e in reversed(range(EXPERTS_PER_GROUP - 1)):
        idx1 = jnp.where(el[e] == top1, e, idx1)
    rest = [jnp.where(idx1 == e, -jnp.inf, el[e]) for e in range(EXPERTS_PER_GROUP)]
    top2 = functools.reduce(jnp.maximum, rest)
    idx2 = jnp.full(top1.shape, EXPERTS_PER_GROUP - 1, jnp.int32)
    for e in reversed(range(EXPERTS_PER_GROUP - 1)):
        idx2 = jnp.where(rest[e] == top2, e, idx2)
    e2 = jnp.exp(top2 - top1)
    w1 = p_grp / (1.0 + e2)
    w2 = p_grp * e2 / (1.0 + e2)
    for e in range(EXPERTS_PER_GROUP):
        cpad_ref[e:e + 1, :] = jnp.where(idx1 == e, w1, 0.0) + jnp.where(idx2 == e, w2, 0.0)
    hx_ref[:, D_MODEL:] = cpad_ref[...].T
    lo = jnp.minimum(idx1, idx2)
    hi = jnp.maximum(idx1, idx2)
    pair = jnp.where(lo == 0, 0, jnp.where(lo == 1, 3, 5)) + (hi - lo - 1)
    bucket = gid * N_PAIRS + pair
    bucket_ref[...] = bucket
    sub = lax.broadcasted_iota(jnp.int32, (CNT_ROWS, RT_TM), 0)
    onehot = jnp.where(sub == bucket, 1.0, 0.0)
    seen = jnp.dot(onehot.astype(BF16), tri_ref[...], preferred_element_type=F32)
    cnt = cnt_ref[...]
    rank_ref[...] = jnp.sum(onehot * (seen - 1.0 + cnt[:, 0:1]), axis=0, keepdims=True).astype(jnp.int32)
    cnt = cnt + jnp.sum(onehot, axis=1, keepdims=True)
    cnt_ref[...] = cnt
    cnt_out_ref[...] = cnt.astype(jnp.int32)


def _router_weights(w_grp, b_grp, w_exp, b_exp):
    w = jnp.zeros((RT_ROWS, D_MODEL), F32).at[0:N_GROUPS].set(w_grp.T).at[8:8 + N_EXPERTS].set(w_exp.T)
    b = jnp.zeros((RT_ROWS, 1), F32).at[0:N_GROUPS, 0].set(b_grp).at[8:8 + N_EXPERTS, 0].set(b_exp)
    return w, b


DISP_TM = 512


ROW_GROUP = 64


def _for_each_row(n_rows, fn):
    def group(k, carry):
        g0 = pl.multiple_of(k * ROW_GROUP, ROW_GROUP)
        for u in range(ROW_GROUP):
            fn(g0, u)
        return carry

    lax.fori_loop(0, n_rows // ROW_GROUP, group, 0)


def _dispatch_kernel(pos_ref, start_ref, cnt_ref, nt_ref, hx_hbm, hs_ref, buf_ref, zero_ref, blk_sem, row_sem, zsem):
    i = pl.program_id(0)
    last = pl.num_programs(0) - 1
    slot = i % 2
    base = i * DISP_TM

    def tile_in(tile, s):
        return pltpu.make_async_copy(hx_hbm.at[pl.ds(pl.multiple_of(tile * DISP_TM, DISP_TM), DISP_TM), :],
                                     buf_ref.at[s], blk_sem.at[s])

    def wait_rows(s):
        pltpu.make_async_copy(buf_ref.at[s], hs_ref.at[pl.ds(0, DISP_TM), :], row_sem.at[s]).wait()

    @pl.when(i == 0)
    def _():
        zero_ref[...] = jnp.zeros_like(zero_ref)

        def tile_copy(t):
            return pltpu.make_async_copy(zero_ref, hs_ref.at[pl.ds(pl.multiple_of(t * EXP_TM, EXP_TM), EXP_TM), :],
                                         zsem)

        def last_tile(g):
            return (start_ref[g] + cnt_ref[g] - 1) // EXP_TM

        def each_tile(fn):
            for g in range(N_BUCKETS):
                pl.when(cnt_ref[g] > 0)(functools.partial(fn, last_tile(g)))
            lax.fori_loop(nt_ref[0], EXP_TILES, lambda t, carry: (fn(t), carry)[1], 0)

        each_tile(lambda t: tile_copy(t).start())
        each_tile(lambda t: tile_copy(t).wait())
        tile_in(0, 0).start()

    @pl.when(i > 0)
    def _():
        wait_rows(1 - slot)

    @pl.when(i < last)
    def _():
        tile_in(i + 1, 1 - slot).start()

    tile_in(i, slot).wait()

    def issue(g0, u):
        src = buf_ref.at[slot, pl.ds(g0, ROW_GROUP), :]
        pltpu.make_async_copy(src.at[pl.ds(u, 1), :], hs_ref.at[pl.ds(pos_ref[base + g0 + u], 1), :],
                              row_sem.at[slot]).start(priority=u % 2)

    _for_each_row(DISP_TM, issue)

    @pl.when(i == last)
    def _():
        wait_rows(slot)


def dispatch(pos, starts, counts, n_tiles, hx):
    return pl.pallas_call(
        _dispatch_kernel,
        grid_spec=pltpu.PrefetchScalarGridSpec(
            num_scalar_prefetch=4,
            grid=(N_TOK // DISP_TM,),
            in_specs=[pl.BlockSpec(memory_space=pl.ANY)],
            out_specs=pl.BlockSpec(memory_space=pl.ANY),
            scratch_shapes=[pltpu.VMEM((2, DISP_TM, HX_W), F32), pltpu.VMEM((EXP_TM, HX_W), F32),
                            pltpu.SemaphoreType.DMA((2,)), pltpu.SemaphoreType.DMA((2,)),
                            pltpu.SemaphoreType.DMA(())],
        ),
        out_shape=jax.ShapeDtypeStruct((HS_ROWS, HX_W), F32),
        compiler_params=_params(1),
        name="dispatch",
    )(pos, starts, counts, n_tiles, hx)


(T_GROUP, T_LO, T_HI, T_SLOT, T_RUN_POS, T_NEXT_GROUP, T_FIRST, T_PREFETCHED,
 T_PENDING_GROUP, T_PENDING_EXPERT, T_PENDING_SLOT) = range(11)


def _experts_kernel(layer, tab_ref, nt_ref, hs_ref, w1_hbm, w3_hbm, w2_hbm, ys_ref,
                    w1_ref, w3_ref, w2_ref, st1_ref, st3_ref, st2_ref, sem_ref):
    t = pl.program_id(0)
    valid = t < nt_ref[0]
    group = tab_ref[T_GROUP, t]
    slot = tab_ref[T_SLOT, t]
    run_pos = tab_ref[T_RUN_POS, t]
    next_group = tab_ref[T_NEXT_GROUP, t]
    pending_expert = tab_ref[T_PENDING_EXPERT, t]

    def expert_copies(grp, e):
        idx = grp * EXPERTS_PER_GROUP + e
        return (pltpu.make_async_copy(w1_hbm.at[layer, idx], st1_ref, sem_ref.at[0]),
                pltpu.make_async_copy(w3_hbm.at[layer, idx], st3_ref, sem_ref.at[1]),
                pltpu.make_async_copy(w2_hbm.at[layer, idx], st2_ref, sem_ref.at[2]))

    def finish(copies, dst_slot, e):
        for cp in copies:
            cp.wait()
        w1_ref[dst_slot, e] = st1_ref[...].astype(BF16)
        w3_ref[dst_slot, e] = st3_ref[...].astype(BF16)
        w2_ref[dst_slot, e] = st2_ref[...].astype(BF16)

    @pl.when(jnp.logical_and(valid, pending_expert >= 0))
    def _():
        finish(expert_copies(tab_ref[T_PENDING_GROUP, t], pending_expert), tab_ref[T_PENDING_SLOT, t], pending_expert)

    @pl.when(jnp.logical_and(valid, tab_ref[T_FIRST, t] == 1))
    def _():
        def load(e, carry):
            copies = expert_copies(group, e)
            for cp in copies:
                cp.start()
            finish(copies, slot, e)
            return carry

        lax.fori_loop(tab_ref[T_PREFETCHED, t], EXPERTS_PER_GROUP, load, 0)

    prefetch = jnp.logical_and(valid, jnp.logical_and(next_group >= 0, run_pos < EXPERTS_PER_GROUP))

    @pl.when(prefetch)
    def _():
        for cp in expert_copies(next_group, run_pos):
            cp.start()

    @pl.when(valid)
    def _():
        h = hs_ref[:, 0:D_MODEL].astype(BF16)
        c = hs_ref[:, D_MODEL:]
        lane = lax.broadcasted_iota(jnp.int32, c.shape, 1)
        acc = None
        for e in (tab_ref[T_LO, t], tab_ref[T_HI, t]):
            ce = jnp.sum(jnp.where(lane == e, c, 0.0), axis=1, keepdims=True)
            a = jnp.dot(h, w1_ref[slot, e], preferred_element_type=F32)
            b = jnp.dot(h, w3_ref[slot, e], preferred_element_type=F32)
            hid = (a * _sigmoid(a)) * b * ce
            y = jnp.dot(hid.astype(BF16), w2_ref[slot, e], preferred_element_type=F32)
            acc = y if acc is None else acc + y
        ys_ref[...] = acc

    @pl.when(jnp.logical_not(valid))
    def _():
        ys_ref[...] = jnp.zeros_like(ys_ref)


def experts(tile_table, n_tiles, hs, w1, w3, w2, layer):
    hbm = pl.BlockSpec(memory_space=pl.ANY)
    return pl.pallas_call(
        functools.partial(_experts_kernel, layer),
        grid_spec=pltpu.PrefetchScalarGridSpec(
            num_scalar_prefetch=2,
            grid=(EXP_TILES,),
            in_specs=[pl.BlockSpec((EXP_TM, HX_W), lambda t, *_: (t, 0)), hbm, hbm, hbm],
            out_specs=pl.BlockSpec((EXP_TM, D_MODEL), lambda t, *_: (t, 0)),
            scratch_shapes=[
                pltpu.VMEM((2, EXPERTS_PER_GROUP, D_MODEL, D_EXPERT), BF16),
                pltpu.VMEM((2, EXPERTS_PER_GROUP, D_MODEL, D_EXPERT), BF16),
                pltpu.VMEM((2, EXPERTS_PER_GROUP, D_EXPERT, D_MODEL), BF16),
                pltpu.VMEM((D_MODEL, D_EXPERT), F32),
                pltpu.VMEM((D_MODEL, D_EXPERT), F32),
                pltpu.VMEM((D_EXPERT, D_MODEL), F32),
                pltpu.SemaphoreType.DMA((3,)),
            ],
        ),
        out_shape=jax.ShapeDtypeStruct((HS_ROWS, D_MODEL), F32),
        compiler_params=_params(1),
        name="experts",
    )(tile_table, n_tiles, hs, w1, w3, w2)


def _expert_tiles(counts):
    tiles = (counts + EXP_TM - 1) // EXP_TM
    ends = jnp.cumsum(tiles)
    n_tiles = ends[-1]
    t = jnp.arange(EXP_TILES, dtype=jnp.int32)
    used = t < n_tiles
    bucket = jnp.minimum(jnp.sum(t[:, None] >= ends[None, :], axis=1), N_BUCKETS - 1)
    group = bucket // N_PAIRS
    pair = bucket % N_PAIRS
    lo = sum(jnp.where(pair == k, p[0], 0) for k, p in enumerate(EXPERT_PAIRS))
    hi = sum(jnp.where(pair == k, p[1], 0) for k, p in enumerate(EXPERT_PAIRS))
    first = jnp.logical_and(used, jnp.logical_or(t == 0, group != jnp.roll(group, 1)))
    run = jnp.cumsum(first) - 1
    run_start = lax.cummax(jnp.where(first, t, 0))
    same_run = jnp.logical_and(run[:, None] == run[None, :], used[None, :])
    run_len = jnp.sum(same_run, axis=1)
    next_first = jnp.logical_and(first[None, :], run[None, :] == run[:, None] + 1)
    next_group = jnp.where(jnp.any(next_first, axis=1), jnp.sum(jnp.where(next_first, group[None, :], 0), axis=1), -1)
    prev_len = jnp.sum(jnp.where(t[None, :] == run_start[:, None] - 1, run_len[None, :], 0), axis=1)
    prefetched = jnp.where(run > 0, jnp.minimum(prev_len, EXPERTS_PER_GROUP), 0)
    slot = run % 2
    run_pos = t - run_start
    starts_fetch = jnp.logical_and(used, jnp.logical_and(next_group >= 0, run_pos < EXPERTS_PER_GROUP))
    prev = lambda v, fill: jnp.concatenate([jnp.full((1,), fill, v.dtype), v[:-1]])
    pending_expert = jnp.where(prev(starts_fetch, False), prev(run_pos, 0), -1)
    table = jnp.stack([group, lo, hi, slot, run_pos, next_group, first, prefetched,
                       prev(next_group, 0), pending_expert, 1 - prev(slot, 0)]).astype(jnp.int32)
    return ((ends - tiles) * EXP_TM).astype(jnp.int32), table, n_tiles.reshape(1).astype(jnp.int32)


RES_TM = 512
RES_CTX_TILES = N_CTX // RES_TM


def _moe_residual_kernel(final, pos_ref, x_ref, mod_ref, gain_ref, ys_ref, *refs):
    out_refs, (ybuf_ref, sem_ref) = refs[:-2], refs[-2:]
    i = pl.program_id(0)
    slot = i % 2

    def gather_tile(tile, tile_slot):
        def issue(g0, u):
            dst = ybuf_ref.at[tile_slot, pl.ds(g0, ROW_GROUP), :]
            pltpu.make_async_copy(ys_ref.at[pl.ds(pos_ref[tile * RES_TM + g0 + u], 1), :], dst.at[pl.ds(u, 1), :],
                                  sem_ref.at[tile_slot]).start(priority=u % 2)

        _for_each_row(RES_TM, issue)

    @pl.when(i == 0)
    def _():
        gather_tile(0, 0)

    @pl.when(i + 1 < pl.num_programs(0))
    def _():
        gather_tile(i + 1, 1 - slot)

    pltpu.make_async_copy(ys_ref.at[pl.ds(0, RES_TM), :], ybuf_ref.at[slot], sem_ref.at[slot]).wait()
    x = x_ref[...] + mod_ref[:, 5 * D_MODEL:6 * D_MODEL] * ybuf_ref[slot]
    if not final:
        out_refs[0][...] = x
        return
    y = _rms(x, gain_ref[...])

    @pl.when(i < RES_CTX_TILES)
    def _():
        out_refs[0][...] = y

    @pl.when(i >= RES_CTX_TILES)
    def _():
        out_refs[1][...] = y


def moe_residual(pos, x, mod3, gain, ys, final):
    if final:
        out_specs = _split_specs((RES_TM, D_MODEL), RES_CTX_TILES)
        out_shape = [jax.ShapeDtypeStruct((N_CTX, D_MODEL), F32), jax.ShapeDtypeStruct((N_LAT, D_MODEL), F32)]
    else:
        out_specs = pl.BlockSpec((RES_TM, D_MODEL), lambda i, *_: (i, 0))
        out_shape = jax.ShapeDtypeStruct((N_TOK, D_MODEL), F32)
    return pl.pallas_call(
        functools.partial(_moe_residual_kernel, final),
        grid_spec=pltpu.PrefetchScalarGridSpec(
            num_scalar_prefetch=1,
            grid=(N_TOK // RES_TM,),
            in_specs=[
                pl.BlockSpec((RES_TM, D_MODEL), lambda i, *_: (i, 0)),
                pl.BlockSpec((None, 1, MOD_WIDTH), lambda i, *_: (_cond_row(i, RES_TM), 0, 0)),
                pl.BlockSpec((1, D_MODEL), lambda i, *_: (0, 0)),
                pl.BlockSpec(memory_space=pl.ANY),
            ],
            out_specs=out_specs,
            scratch_shapes=[pltpu.VMEM((2, RES_TM, D_MODEL), F32), pltpu.SemaphoreType.DMA((2,))],
        ),
        out_shape=out_shape,
        compiler_params=_params(1),
        name="moe_residual_final" if final else "moe_residual",
    )(pos, x, mod3, gain, ys)


def kernel(x_prompt, x_sample, cache_k, cache_v, state_lru, c, c_ctx, w_mod, b_mod, norm_mix, norm_ffn, w_in, sgu_norm, sgu_w, sgu_b, lru_conv_w, lru_conv_b, lru_w_r, lru_b_r, lru_w_i, lru_b_i, lru_lambda, na_rpb, w_branch_sgu, w_branch_lru, w_branch_na, w_out, moe_w_group, moe_b_group, moe_w_expert, moe_b_expert, moe_w1, moe_w3, moe_w2, final_norm_gain):
    xs = (x_prompt.reshape(N_CTX, D_MODEL), x_sample.reshape(N_LAT, D_MODEL))
    cond = jnp.zeros((N_COND, D_MODEL), F32).at[0].set(c_ctx).at[1:1 + DEC_BATCH].set(c)
    mod = modulation(cond, w_mod, b_mod)
    zero_state = jnp.zeros((BATCH, 2, LRU_WIDTH), F32)
    cache_k = cache_k.reshape(DEC_BATCH, DEPTH, PAST_LEN, NA_WIDTH)
    cache_v = cache_v.reshape(DEC_BATCH, DEPTH, PAST_LEN, NA_WIDTH)
    final_gain = final_norm_gain.reshape(1, D_MODEL)
    ks, vs, ss = [], [], []
    for l in range(DEPTH):
        mod3 = mod[l].reshape(N_COND, 1, MOD_WIDTH)
        ya, za, q, kb, vb, k_ctx, v_ctx, gates = in_projection(
            xs, mod3, norm_mix[l].reshape(1, D_MODEL), w_in, sgu_norm[l].reshape(1, SGU_WIDTH), sgu_w[l], sgu_b[l].T, l)
        w_lo, w_hi = _block_diag_gate_weights(lru_w_r[l], lru_w_i[l])
        lru_args = (lru_conv_w[l], lru_conv_b[l].reshape(1, LRU_WIDTH), w_lo, w_hi, lru_b_r[l], lru_b_i[l],
                    lru_lambda[l])
        yb_ctx, st_ctx = rglru(za, *lru_args, zero_state, SEQ, BATCH, 0)
        yb_lat, _ = rglru(za, *lru_args, state_lru[:, l], DEC_SEQ, DEC_BATCH, N_CTX)
        yc_ctx = context_attention(q, kb, vb)
        rpb = jnp.pad(na_rpb[l].reshape(NA_HEADS * N_DROW, N_DCOL), ((0, 0), (0, LANES - N_DCOL)))
        yc_lat = latent_attention(q, kb, vb, cache_k, cache_v, rpb, l)
        w_rt, b_rt = _router_weights(moe_w_group[l], moe_b_group[l], moe_w_expert[l], moe_b_expert[l])
        x, hx, bucket, rank, counts = merge_branches(
            xs, mod3, ya, yb_ctx, yb_lat, yc_ctx, yc_lat, gates, w_branch_sgu, w_branch_lru, w_branch_na, w_out,
            norm_ffn[l].reshape(1, D_MODEL), w_rt, b_rt, l)
        bucket, rank, counts = bucket.reshape(N_TOK), rank.reshape(N_TOK), counts[:N_BUCKETS, 0]
        starts, tile_table, n_tiles = _expert_tiles(counts)
        pos = rank + jnp.sum(jnp.where(bucket[:, None] == jnp.arange(N_BUCKETS), starts[None, :], 0), axis=1)
        hs = dispatch(pos, starts, counts, n_tiles, hx)
        ys = experts(tile_table, n_tiles, hs, moe_w1, moe_w3, moe_w2, l)
        if l < DEPTH - 1:
            xs = (moe_residual(pos, x, mod3, final_gain, ys, False),)
        else:
            y_ctx, y_lat = moe_residual(pos, x, mod3, final_gain, ys, True)
        ks.append(k_ctx.reshape(BATCH, SEQ, NA_HEADS, NA_HEAD_DIM))
        vs.append(v_ctx.reshape(BATCH, SEQ, NA_HEADS, NA_HEAD_DIM))
        ss.append(st_ctx)
    return (y_ctx.reshape(BATCH, SEQ, D_MODEL), y_lat.reshape(DEC_BATCH, DEC_SEQ, D_MODEL),
            jnp.stack(ks, axis=1), jnp.stack(vs, axis=1), jnp.stack(ss, axis=1))
```

```python
import functools

import jax
import jax.numpy as jnp
import numpy as np
from jax import lax
from jax.experimental import pallas as pl
from jax.experimental.pallas import tpu as pltpu

F32 = jnp.float32
BF16 = jnp.bfloat16

D_MODEL = 1024
BATCH = 16
SEQ = 256
DEPTH = 2
DEC_BATCH = 4
DEC_SEQ = 2048
PAST_LEN = 512
GRID_W = 64
CHUNK = 128
SGU_WIDTH = 512
SGU_GROUPS = 4
LRU_WIDTH = 512
LRU_BLOCKS = 8
CONV_WIDTH = 4
LRU_C = 8.0
NA_HEADS = 8
NA_HEAD_DIM = 64
NA_WIDTH = 512
WIN_ROWS = 8
WIN_COLS = 16
N_GROUPS = 4
EXPERTS_PER_GROUP = 4
N_EXPERTS = 16
D_EXPERT = 512
IN_WIDTH = 6656
EPS = 1e-6
NEG_INF = -1e30

N_CTX = BATCH * SEQ
N_LAT = DEC_BATCH * DEC_SEQ
N_TOK = N_CTX + N_LAT
N_COND = 8
MOD_WIDTH = 6 * D_MODEL
GRID_ROWS = DEC_SEQ // GRID_W

VMEM_LIMIT_BYTES = 56 * 1024 * 1024


def _params(n_axes):
    return pltpu.CompilerParams(dimension_semantics=("arbitrary",) * n_axes,
                                vmem_limit_bytes=VMEM_LIMIT_BYTES)


def _cond_row(tile, tile_rows):
    tok = tile * tile_rows
    return jnp.where(tok < N_CTX, 0, 1 + (tok - N_CTX) // DEC_SEQ)


def _rms(x, gain):
    return x * lax.rsqrt(jnp.mean(x * x, axis=-1, keepdims=True) + EPS) * gain


def _bdot(a, b):
    return jnp.dot(a.astype(BF16), b.astype(BF16), preferred_element_type=F32)


def _bdot_t(a, b):
    return lax.dot_general(a.astype(BF16), b.astype(BF16), (((1,), (1,)), ((), ())),
                           preferred_element_type=F32)


MOD_TN = 1536


def _mod_kernel(cond_ref, w_ref, b_ref, o_ref):
    c = cond_ref[...]
    s = c * jax.nn.sigmoid(c)
    o_ref[...] = _bdot(s, w_ref[...]) + b_ref[...]


def modulation(cond, w_mod, b_mod):
    return pl.pallas_call(
        _mod_kernel,
        grid=(DEPTH, MOD_WIDTH // MOD_TN),
        in_specs=[
            pl.BlockSpec((N_COND, D_MODEL), lambda l, j: (0, 0)),
            pl.BlockSpec((None, D_MODEL, MOD_TN), lambda l, j: (l, 0, j)),
            pl.BlockSpec((None, 1, MOD_TN), lambda l, j: (l, 0, j)),
        ],
        out_specs=pl.BlockSpec((None, N_COND, MOD_TN), lambda l, j: (l, 0, j)),
        out_shape=jax.ShapeDtypeStruct((DEPTH, N_COND, MOD_WIDTH), F32),
        compiler_params=_params(2),
        name="modulation",
    )(cond, w_mod, b_mod.reshape(DEPTH, 1, MOD_WIDTH))


IN_TM = 256
IN_CTX_TILES = N_CTX // IN_TM
ZA_WIDTH = 4 * 512
KV_OFF = ZA_WIDTH + NA_WIDTH
GATE_OFF = KV_OFF + 2 * NA_WIDTH
W_CHUNK = 512
Q_SCALE = NA_HEAD_DIM ** -0.5 * float(np.log2(np.e))


def _load_cast(chunks, stage_ref, sem_ref):
    def copy(j):
        return pltpu.make_async_copy(chunks[j][0], stage_ref.at[j % 2], sem_ref.at[j % 2])

    copy(0).start()
    for j in range(len(chunks)):
        if j + 1 < len(chunks):
            copy(j + 1).start()
        copy(j).wait()
        chunks[j][1](stage_ref[j % 2].astype(BF16))


def _token_tile(i, n_ctx_tiles, refs):
    if len(refs) == 1:
        return refs[0][...]
    return jnp.where(i < n_ctx_tiles, refs[0][...], refs[1][...])


def _split_specs(block, n_ctx_tiles):
    return [pl.BlockSpec(block, lambda i, *_: (jnp.minimum(i, n_ctx_tiles - 1), 0)),
            pl.BlockSpec(block, lambda i, *_: (jnp.maximum(i - n_ctx_tiles, 0), 0))]


SGU_GD = SGU_WIDTH // SGU_GROUPS


def _spatial_gating(u, v, gain, ws_ref, bs_ref):
    u = jax.nn.gelu(u)
    v = _rms(jax.nn.gelu(v), gain).astype(BF16)
    out = []
    for g in range(SGU_GROUPS):
        cols = slice(g * SGU_GD, (g + 1) * SGU_GD)
        mixed = jnp.dot(ws_ref[g].astype(BF16), v[:, cols], preferred_element_type=F32) + bs_ref[:, g:g + 1]
        out.append(u[:, cols] * mixed)
    return jnp.concatenate(out, axis=1)


def _inproj_kernel(layer, n_x, n_prev, *refs):
    x_refs, (mod_ref, gain_ref, w_hbm, sgu_gain_ref, ws_ref, bs_ref) = refs[:n_x], refs[n_x:n_x + 6]
    prev_refs, rest = refs[n_x + 6:n_x + 6 + 2 * n_prev], refs[n_x + 6 + 2 * n_prev:]
    ya_ref, zb_ref, q_ref, kb_ref, vb_ref, kc_ref, vc_ref, g_ref, w_ref, stage_ref, sem_ref = rest
    i = pl.program_id(0)

    @pl.when(i == 0)
    def _():
        def store(c):
            def st(v):
                w_ref[:, c * W_CHUNK:(c + 1) * W_CHUNK] = v
            return st

        _load_cast([(w_hbm.at[layer, :, pl.ds(c * W_CHUNK, W_CHUNK)], store(c)) for c in range(IN_WIDTH // W_CHUNK)],
                   stage_ref, sem_ref)

    m = mod_ref[...]
    shift, scale = m[:, 0:D_MODEL], m[:, D_MODEL:2 * D_MODEL]
    h = (_rms(_token_tile(i, IN_CTX_TILES, x_refs), gain_ref[...]) * (1.0 + scale) + shift).astype(BF16)
    za = jnp.dot(h, w_ref[:, 0:ZA_WIDTH], preferred_element_type=F32)
    for c in range(IN_TM // CHUNK):
        rows = slice(c * CHUNK, (c + 1) * CHUNK)
        ya_ref[rows, :] = _spatial_gating(za[rows, 0:SGU_WIDTH], za[rows, SGU_WIDTH:2 * SGU_WIDTH],
                                          sgu_gain_ref[...], ws_ref, bs_ref).astype(BF16)
    zb_ref[...] = za[:, 2 * SGU_WIDTH:].astype(BF16)
    q_ref[...] = (jnp.dot(h, w_ref[:, ZA_WIDTH:KV_OFF], preferred_element_type=F32) * Q_SCALE).astype(BF16)
    kv = jnp.dot(h, w_ref[:, KV_OFF:GATE_OFF], preferred_element_type=F32)
    kb_ref[...] = kv[:, 0:NA_WIDTH].astype(BF16)
    vb_ref[...] = kv[:, NA_WIDTH:].astype(BF16)
    g_ref[...] = jnp.dot(h, w_ref[:, GATE_OFF:], preferred_element_type=F32).astype(BF16)

    @pl.when(i < IN_CTX_TILES)
    def _():
        if n_prev == 0:
            kc_ref[...] = kv[:, 0:NA_WIDTH]
            vc_ref[...] = kv[:, NA_WIDTH:]
        else:
            for p in range(n_prev):
                kc_ref[p] = prev_refs[2 * p][...]
                vc_ref[p] = prev_refs[2 * p + 1][...]
            kc_ref[n_prev] = kv[:, 0:NA_WIDTH]
            vc_ref[n_prev] = kv[:, NA_WIDTH:]


def in_projection(xs, mod3, gain, w_in, sgu_gain, sgu_w, sgu_b_t, layer, prev_caches=()):
    assert IN_TM == SEQ
    row = lambda i: (i, 0)
    ctx_row = lambda i: (jnp.minimum(i, IN_CTX_TILES - 1), 0)
    const = lambda shape: pl.BlockSpec(shape, lambda i: (0,) * len(shape))
    x_specs = ([pl.BlockSpec((IN_TM, D_MODEL), row)] if len(xs) == 1
               else _split_specs((IN_TM, D_MODEL), IN_CTX_TILES))
    n_prev = len(prev_caches)
    if n_prev:
        cache_spec = pl.BlockSpec((None, n_prev + 1, SEQ, NA_WIDTH),
                                  lambda i: (jnp.minimum(i, IN_CTX_TILES - 1), 0, 0, 0))
        cache_shape = jax.ShapeDtypeStruct((BATCH, n_prev + 1, SEQ, NA_WIDTH), F32)
    else:
        cache_spec = pl.BlockSpec((IN_TM, NA_WIDTH), ctx_row)
        cache_shape = jax.ShapeDtypeStruct((N_CTX, NA_WIDTH), F32)
    return pl.pallas_call(
        functools.partial(_inproj_kernel, layer, len(xs), n_prev),
        grid=(N_TOK // IN_TM,),
        in_specs=x_specs + [
            pl.BlockSpec((None, 1, MOD_WIDTH), lambda i: (_cond_row(i, IN_TM), 0, 0)),
            const((1, D_MODEL)),
            pl.BlockSpec(memory_space=pl.ANY),
            const((1, SGU_WIDTH)),
            const((SGU_GROUPS, CHUNK, CHUNK)),
            const((CHUNK, SGU_GROUPS)),
        ] + [pl.BlockSpec((IN_TM, NA_WIDTH), ctx_row)] * (2 * n_prev),
        out_specs=[
            pl.BlockSpec((IN_TM, SGU_WIDTH), row),
            pl.BlockSpec((IN_TM, 2 * LRU_WIDTH), row),
            pl.BlockSpec((IN_TM, NA_WIDTH), row),
            pl.BlockSpec((IN_TM, NA_WIDTH), row),
            pl.BlockSpec((IN_TM, NA_WIDTH), row),
            cache_spec,
            cache_spec,
            pl.BlockSpec((IN_TM, 3 * D_MODEL), row),
        ],
        out_shape=[
            jax.ShapeDtypeStruct((N_TOK, SGU_WIDTH), BF16),
            jax.ShapeDtypeStruct((N_TOK, 2 * LRU_WIDTH), BF16),
            jax.ShapeDtypeStruct((N_TOK, NA_WIDTH), BF16),
            jax.ShapeDtypeStruct((N_TOK, NA_WIDTH), BF16),
            jax.ShapeDtypeStruct((N_TOK, NA_WIDTH), BF16),
            cache_shape,
            cache_shape,
            jax.ShapeDtypeStruct((N_TOK, 3 * D_MODEL), BF16),
        ],
        scratch_shapes=[pltpu.VMEM((D_MODEL, IN_WIDTH), BF16), pltpu.VMEM((2, D_MODEL, W_CHUNK), F32),
                        pltpu.SemaphoreType.DMA((2,))],
        compiler_params=_params(1),
        name="in_projection",
    )(*xs, mod3, gain, w_in, sgu_gain, sgu_w, sgu_b_t, *[c for kv in prev_caches for c in kv])


LRU_TC = 256
LRU_HALF = 256
SUB = 8
HALO = 8
LANES = 128
LRU_SLABS = LRU_WIDTH // LANES


def _sigmoid(x):
    return 0.5 * jnp.tanh(0.5 * x) + 0.5


def _lru_kernel(seq_len, xr_ref, gr_ref, cw_ref, cb_ref, wlo_ref, whi_ref, br_ref, bi_ref,
                lam_ref, h0_ref, y_ref, st_ref, xp_ref, a_ref, b_ref, h_ref):
    seg_len = seq_len // SUB
    pitch = seg_len + SUB
    segs_per_chunk = max(LRU_TC // seg_len, 1)
    n_chunks = seq_len // LRU_TC
    zeros = jnp.zeros((HALO, LRU_WIDTH), F32)
    xp_ref[0:HALO, :] = zeros
    xp_ref[seq_len + HALO:seq_len + 2 * HALO, :] = zeros

    def copy_in(c, carry):
        r0 = pl.multiple_of(c * LRU_TC, LRU_TC)
        xp_ref[pl.ds(r0 + HALO, LRU_TC), :] = xr_ref[pl.ds(r0, LRU_TC), :].astype(F32)
        return carry

    lax.fori_loop(0, n_chunks, copy_in, 0)

    def chunk_rows(c, seg):
        n = min(seg_len, LRU_TC)
        start = pl.multiple_of((c * segs_per_chunk + seg) * pitch, SUB)
        return pl.ds(start, n), slice(seg * n, (seg + 1) * n)

    cw = cw_ref[...]
    cb = cb_ref[...]
    win = LRU_TC + 2 * HALO
    neg_lam = -lam_ref[...]
    softplus = jnp.maximum(neg_lam, 0.0) + jnp.log1p(jnp.exp(-jnp.abs(neg_lam)))
    decay = (0.5 * LRU_C) * softplus
    half_br = 0.5 * br_ref[...]
    half_bi = 0.5 * bi_ref[...]

    def gates(c, carry):
        r0 = pl.multiple_of(c * LRU_TC, LRU_TC)
        w = xp_ref[pl.ds(r0, win), :]
        xc = (cw[0:1, :] * pltpu.roll(w, 1, 0)[HALO:HALO + LRU_TC]
              + cw[1:2, :] * w[HALO:HALO + LRU_TC]
              + cw[2:3, :] * pltpu.roll(w, win - 1, 0)[HALO:HALO + LRU_TC]
              + cw[3:4, :] * pltpu.roll(w, win - 2, 0)[HALO:HALO + LRU_TC]) + cb
        xb = xc.astype(BF16)
        for half, w_ref in enumerate((wlo_ref, whi_ref)):
            cols = slice(half * LRU_HALF, (half + 1) * LRU_HALF)
            pre = jnp.dot(xb[:, cols], w_ref[...], preferred_element_type=F32)
            half_x = 0.5 * xc[:, cols]
            for d in range(2):
                tr = jnp.tanh(pre[:, (2 * d) * LRU_HALF:(2 * d + 1) * LRU_HALF] + half_br[d:d + 1, cols])
                ti = jnp.tanh(pre[:, (2 * d + 1) * LRU_HALF:(2 * d + 2) * LRU_HALF] + half_bi[d:d + 1, cols])
                neg_log_a = decay[d:d + 1, cols] * tr + decay[d:d + 1, cols]
                a = jnp.exp2(neg_log_a * -LOG2E)
                z = jnp.tanh(neg_log_a) * (a * a + 1.0)
                b = jnp.where(z > 0.0, z * lax.rsqrt(z), 0.0) * (half_x * ti + half_x)
                for k in range(LRU_HALF // LANES):
                    slab = half * (LRU_HALF // LANES) + k
                    lanes = slice(k * LANES, (k + 1) * LANES)
                    for seg in range(segs_per_chunk):
                        dst, src = chunk_rows(c, seg)
                        a_ref[d, slab, dst, :] = a[src, lanes]
                        b_ref[d, slab, dst, :] = b[src, lanes]
        return carry

    lax.fori_loop(0, n_chunks, gates, 0)

    def step_rows(d, j):
        return pl.ds(j if d == 0 else seg_len - 1 - j, SUB, stride=pitch)

    def reduce_step(j, carry):
        out = []
        for d in range(2):
            for slab in range(LRU_SLABS):
                big_a, big_b = carry[d * LRU_SLABS + slab]
                a = a_ref[d, slab, step_rows(d, j), :]
                b = b_ref[d, slab, step_rows(d, j), :]
                out.append((a * big_a, a * big_b + b))
        return tuple(out)

    ident = (jnp.ones((SUB, LANES), F32), jnp.zeros((SUB, LANES), F32))
    totals = lax.fori_loop(0, seg_len, reduce_step, (ident,) * (2 * LRU_SLABS), unroll=8)

    sub = lax.broadcasted_iota(jnp.int32, (SUB, LANES), 0)
    entering = []
    for d in range(2):
        for slab in range(LRU_SLABS):
            big_a, big_b = totals[d * LRU_SLABS + slab]
            h = h0_ref[d:d + 1, slab * LANES:(slab + 1) * LANES]
            rows = jnp.zeros((SUB, LANES), F32)
            for s in (range(SUB) if d == 0 else reversed(range(SUB))):
                rows = jnp.where(sub == s, h, rows)
                h = big_a[s:s + 1, :] * h + big_b[s:s + 1, :]
            entering.append(rows)
            st_ref[d:d + 1, slab * LANES:(slab + 1) * LANES] = h

    def apply_step(j, carry):
        out = []
        for d in range(2):
            for slab in range(LRU_SLABS):
                h = (a_ref[d, slab, step_rows(d, j), :] * carry[d * LRU_SLABS + slab]
                     + b_ref[d, slab, step_rows(d, j), :])
                h_ref[d, slab, step_rows(d, j), :] = h
                out.append(h)
        return tuple(out)

    lax.fori_loop(0, seg_len, apply_step, tuple(entering), unroll=8)

    def merge(c, carry):
        r0 = pl.multiple_of(c * LRU_TC, LRU_TC)
        gate = jax.nn.gelu(gr_ref[pl.ds(r0, LRU_TC), :].astype(F32))
        for slab in range(LRU_SLABS):
            lanes = slice(slab * LANES, (slab + 1) * LANES)
            for seg in range(segs_per_chunk):
                src, dst = chunk_rows(c, seg)
                h = h_ref[0, slab, src, :] + h_ref[1, slab, src, :]
                y_ref[pl.ds(r0 + dst.start, dst.stop - dst.start), lanes] = (h * gate[dst, lanes]).astype(BF16)
        return carry

    lax.fori_loop(0, n_chunks, merge, 0)


def rglru(za, conv_w, conv_b, w_lo, w_hi, b_r, b_i, lam, h0, seq_len, n_seq, tok_off):
    blk0 = tok_off // seq_len
    full = lambda shape: pl.BlockSpec(shape, lambda i: (0,) * len(shape))
    return pl.pallas_call(
        functools.partial(_lru_kernel, seq_len),
        grid=(n_seq,),
        in_specs=[
            pl.BlockSpec((seq_len, LRU_WIDTH), lambda i: (i + blk0, 0)),
            pl.BlockSpec((seq_len, LRU_WIDTH), lambda i: (i + blk0, 1)),
            full((CONV_WIDTH, LRU_WIDTH)),
            full((1, LRU_WIDTH)),
            full((LRU_HALF, 4 * LRU_HALF)),
            full((LRU_HALF, 4 * LRU_HALF)),
            full((2, LRU_WIDTH)),
            full((2, LRU_WIDTH)),
            full((2, LRU_WIDTH)),
            pl.BlockSpec((None, 2, LRU_WIDTH), lambda i: (i, 0, 0)),
        ],
        out_specs=[
            pl.BlockSpec((seq_len, LRU_WIDTH), lambda i: (i, 0)),
            pl.BlockSpec((None, 2, LRU_WIDTH), lambda i: (i, 0, 0)),
        ],
        out_shape=[
            jax.ShapeDtypeStruct((n_seq * seq_len, LRU_WIDTH), BF16),
            jax.ShapeDtypeStruct((n_seq, 2, LRU_WIDTH), F32),
        ],
        scratch_shapes=[
            pltpu.VMEM((seq_len + 2 * HALO, LRU_WIDTH), F32),
            pltpu.VMEM((2, LRU_SLABS, seq_len + SUB * SUB, LANES), F32),
            pltpu.VMEM((2, LRU_SLABS, seq_len + SUB * SUB, LANES), F32),
            pltpu.VMEM((2, LRU_SLABS, seq_len + SUB * SUB, LANES), F32),
        ],
        compiler_params=_params(1),
        name=f"rglru_{seq_len}",
    )(za, za, conv_w, conv_b, w_lo, w_hi, b_r, b_i, lam, h0)


def _block_diag_gate_weights(w_r, w_i):
    per_half = LRU_BLOCKS // 2
    eye = jnp.eye(per_half, dtype=w_r.dtype)

    def bd(w):
        return jnp.einsum("hij,hg->higj", w, eye).reshape(LRU_HALF, LRU_HALF)

    tiles = []
    for half in range(2):
        sl = slice(half * per_half, (half + 1) * per_half)
        tiles.append(jnp.concatenate([bd(w_r[0, sl]), bd(w_i[0, sl]), bd(w_r[1, sl]), bd(w_i[1, sl])], axis=1))
    return (0.5 * tiles[0]).astype(BF16), (0.5 * tiles[1]).astype(BF16)


PAIR_W = 2 * NA_HEAD_DIM
LOG2E = float(np.log2(np.e))


def _head_lanes(shape, head):
    lane = lax.broadcasted_iota(jnp.int32, shape, 1)
    return (lane < NA_HEAD_DIM) if head == 0 else (lane >= NA_HEAD_DIM)


def _one_head(q_pair, head):
    return jnp.where(_head_lanes(q_pair.shape, head), q_pair.astype(F32), 0.0).astype(BF16)


def _join_heads(o0, o1):
    return jnp.where(_head_lanes(o0.shape, 0), o0, o1)


def _ctx_attn_kernel(q_ref, k_ref, v_ref, o_ref):
    for j in range(NA_HEADS // 2):
        cols = slice(j * PAIR_W, (j + 1) * PAIR_W)
        outs = []
        for head in range(2):
            s = _bdot_t(_one_head(q_ref[:, cols], head), k_ref[:, cols])
            p = jnp.exp2(s - jnp.max(s, axis=-1, keepdims=True))
            denom = jnp.sum(p, axis=-1, keepdims=True)
            outs.append(jnp.dot(p.astype(BF16), v_ref[:, cols], preferred_element_type=F32) / denom)
        o_ref[:, cols] = _join_heads(*outs).astype(BF16)


def context_attention(q, kb, vb):
    spec = pl.BlockSpec((SEQ, NA_WIDTH), lambda i: (i, 0))
    return pl.pallas_call(
        _ctx_attn_kernel,
        grid=(BATCH,),
        in_specs=[spec, spec, spec],
        out_specs=spec,
        out_shape=jax.ShapeDtypeStruct((N_CTX, NA_WIDTH), BF16),
        compiler_params=_params(1),
        name="context_attention",
    )(q, kb, vb)


NA_QROWS = 4
NA_TQ = NA_QROWS * GRID_W
NA_KROWS = NA_QROWS + WIN_ROWS
NA_TK = NA_KROWS * GRID_W
NA_QBLOCKS = GRID_ROWS // NA_QROWS


N_DROW = 2 * WIN_ROWS - 1
N_DCOL = 2 * WIN_COLS - 1
NA_BLOCK_KINDS = (0, 1, NA_QBLOCKS - 1)


def _first_key_row(qb):
    return (np.clip if isinstance(qb, int) else jnp.clip)(qb * NA_QROWS - WIN_ROWS // 2, 0, GRID_ROWS - NA_KROWS)


def _build_bias(rpb_ref, table_ref, bias_ref):
    qc = lax.broadcasted_iota(jnp.int32, (GRID_W, GRID_W), 0)
    kc = lax.broadcasted_iota(jnp.int32, (GRID_W, GRID_W), 1)
    col0 = jnp.clip(qc - WIN_COLS // 2, 0, GRID_W - WIN_COLS)
    col_ok = jnp.logical_and(kc >= col0, kc < col0 + WIN_COLS)
    d_col = jnp.clip(kc - qc, 1 - WIN_COLS, WIN_COLS - 1) + WIN_COLS - 1
    neg = jnp.full((GRID_W, GRID_W), NEG_INF, F32)

    def table_entry(idx, carry):
        t = neg
        for j in range(N_DCOL):
            t = jnp.where(d_col == j, rpb_ref[idx * N_DCOL + j], t)
        table_ref[idx] = jnp.where(col_ok, t * LOG2E, neg)
        return carry

    lax.fori_loop(0, NA_HEADS * N_DROW, table_entry, 0)

    def head_blocks(h, carry):
        for kind, qb in enumerate(NA_BLOCK_KINDS):
            for i in range(NA_QROWS):
                qrow = qb * NA_QROWS + i
                win0 = int(np.clip(qrow - WIN_ROWS // 2, 0, GRID_ROWS - WIN_ROWS))
                for kr in range(NA_KROWS):
                    krow = int(_first_key_row(qb)) + kr
                    inside = win0 <= krow < win0 + WIN_ROWS
                    blk = table_ref[h * N_DROW + (krow - qrow + WIN_ROWS - 1)] if inside else neg
                    bias_ref[kind, h, i * GRID_W:(i + 1) * GRID_W, kr * GRID_W:(kr + 1) * GRID_W] = blk
        return carry

    lax.fori_loop(0, NA_HEADS, head_blocks, 0)


def _lat_attn_kernel(rpb_ref, q_ref, k_ref, v_ref, ck_ref, cv_ref, o_ref, table_ref, bias_ref):
    qb = pl.program_id(1)

    @pl.when(jnp.logical_and(pl.program_id(0) == 0, qb == 0))
    def _():
        _build_bias(rpb_ref, table_ref, bias_ref)

    kind = jnp.where(qb == 0, 0, jnp.where(qb == NA_QBLOCKS - 1, 2, 1))
    k0 = pl.multiple_of(_first_key_row(qb) * GRID_W, GRID_W)
    for j in range(NA_HEADS // 2):
        cols = slice(j * PAIR_W, (j + 1) * PAIR_W)
        k_loc = k_ref[pl.ds(k0, NA_TK), cols]
        v_loc = v_ref[pl.ds(k0, NA_TK), cols]
        k_ctx = ck_ref[:, cols].astype(BF16)
        v_ctx = cv_ref[:, cols].astype(BF16)
        outs = []
        for head in range(2):
            qh = _one_head(q_ref[:, cols], head)
            s_loc = _bdot_t(qh, k_loc) + bias_ref[kind, 2 * j + head]
            s_ctx = _bdot_t(qh, k_ctx)
            m = jnp.maximum(jnp.max(s_loc, axis=-1, keepdims=True), jnp.max(s_ctx, axis=-1, keepdims=True))
            p_loc = jnp.exp2(s_loc - m)
            p_ctx = jnp.exp2(s_ctx - m)
            denom = jnp.sum(p_loc, axis=-1, keepdims=True) + jnp.sum(p_ctx, axis=-1, keepdims=True)
            o = (jnp.dot(p_loc.astype(BF16), v_loc, preferred_element_type=F32)
                 + jnp.dot(p_ctx.astype(BF16), v_ctx, preferred_element_type=F32))
            outs.append(o / denom)
        o_ref[:, cols] = _join_heads(*outs).astype(BF16)


def latent_attention(q, kb, vb, cache_k, cache_v, rpb, layer):
    q_blk0 = N_CTX // NA_TQ
    s_blk0 = N_CTX // DEC_SEQ
    cache_spec = pl.BlockSpec((None, None, PAST_LEN, NA_WIDTH), lambda b, m: (b, layer, 0, 0))
    return pl.pallas_call(
        _lat_attn_kernel,
        grid=(DEC_BATCH, NA_QBLOCKS),
        in_specs=[
            pl.BlockSpec(memory_space=pltpu.SMEM),
            pl.BlockSpec((NA_TQ, NA_WIDTH), lambda b, m: (q_blk0 + b * NA_QBLOCKS + m, 0)),
            pl.BlockSpec((DEC_SEQ, NA_WIDTH), lambda b, m: (s_blk0 + b, 0)),
            pl.BlockSpec((DEC_SEQ, NA_WIDTH), lambda b, m: (s_blk0 + b, 0)),
            cache_spec,
            cache_spec,
        ],
        out_specs=pl.BlockSpec((NA_TQ, NA_WIDTH), lambda b, m: (b * NA_QBLOCKS + m, 0)),
        out_shape=jax.ShapeDtypeStruct((N_LAT, NA_WIDTH), BF16),
        scratch_shapes=[pltpu.VMEM((NA_HEADS * N_DROW, GRID_W, GRID_W), F32),
                        pltpu.VMEM((len(NA_BLOCK_KINDS), NA_HEADS, NA_TQ, NA_TK), F32)],
        compiler_params=_params(2),
        name="latent_attention",
    )(rpb, q, kb, vb, cache_k, cache_v)


MERGE_TM = 512


MERGE_CTX_TILES = N_CTX // MERGE_TM
BRANCH_W = 512


def _merge_kernel(layer, n_x, *refs):
    x_refs, rest = refs[:n_x], refs[n_x:]
    (mod_ref, ya_ref, yb_ctx_ref, yb_lat_ref, yc_ctx_ref, yc_lat_ref, g_ref, wa_hbm, wb_hbm, wc_hbm, wo_hbm,
     ffn_gain_ref, wrt_ref, brt_ref, tri_ref,
     o_ref, hx_ref, bucket_ref, rank_ref, cnt_out_ref,
     wbr_ref, wo_ref, stage_ref, sem_ref, cnt_ref, cpad_ref) = rest
    i = pl.program_id(0)

    @pl.when(i == 0)
    def _():
        def store_branch(k):
            def st(v):
                wbr_ref[k] = v
            return st

        def store_out(k):
            def st(v):
                wo_ref[k * BRANCH_W:(k + 1) * BRANCH_W, :] = v
            return st

        chunks = [(w.at[layer], store_branch(k)) for k, w in enumerate((wa_hbm, wb_hbm, wc_hbm))]
        chunks += [(wo_hbm.at[layer, pl.ds(k * BRANCH_W, BRANCH_W), :], store_out(k))
                   for k in range(D_MODEL // BRANCH_W)]
        _load_cast(chunks, stage_ref, sem_ref)

    g = _sigmoid(g_ref[...].astype(F32))
    yb = _token_tile(i, MERGE_CTX_TILES, (yb_ctx_ref, yb_lat_ref))
    yc = _token_tile(i, MERGE_CTX_TILES, (yc_ctx_ref, yc_lat_ref))
    merged = (g[:, 0:D_MODEL] * jnp.dot(ya_ref[...], wbr_ref[0], preferred_element_type=F32)
              + g[:, D_MODEL:2 * D_MODEL] * jnp.dot(yb, wbr_ref[1], preferred_element_type=F32)
              + g[:, 2 * D_MODEL:] * jnp.dot(yc, wbr_ref[2], preferred_element_type=F32))
    y = jnp.dot(merged.astype(BF16), wo_ref[...], preferred_element_type=F32)
    gate = mod_ref[:, 2 * D_MODEL:3 * D_MODEL]
    x = _token_tile(i, MERGE_CTX_TILES, x_refs) + gate * y
    o_ref[...] = x
    _route(x, mod_ref, ffn_gain_ref, wrt_ref, brt_ref, tri_ref, hx_ref, bucket_ref, rank_ref, cnt_out_ref,
           cnt_ref, cpad_ref)


def merge_branches(xs, mod3, ya, yb_ctx, yb_lat, yc_ctx, yc_lat, gates, wa, wb, wc, wo, ffn_gain, w_rt, b_rt, layer):
    assert MERGE_TM == RT_TM
    row = lambda i: (i, 0)
    const = lambda shape: pl.BlockSpec(shape, lambda i: (0,) * len(shape))
    tri = jnp.asarray(np.triu(np.ones((RT_TM, RT_TM), np.float32)), BF16)
    hbm = pl.BlockSpec(memory_space=pl.ANY)
    x_specs = ([pl.BlockSpec((MERGE_TM, D_MODEL), row)] if len(xs) == 1
               else _split_specs((MERGE_TM, D_MODEL), MERGE_CTX_TILES))
    return pl.pallas_call(
        functools.partial(_merge_kernel, layer, len(xs)),
        grid=(N_TOK // MERGE_TM,),
        in_specs=x_specs + [
            pl.BlockSpec((None, 1, MOD_WIDTH), lambda i: (_cond_row(i, MERGE_TM), 0, 0)),
            pl.BlockSpec((MERGE_TM, SGU_WIDTH), row),
            *_split_specs((MERGE_TM, LRU_WIDTH), MERGE_CTX_TILES),
            *_split_specs((MERGE_TM, NA_WIDTH), MERGE_CTX_TILES),
            pl.BlockSpec((MERGE_TM, 3 * D_MODEL), row),
            hbm, hbm, hbm, hbm,
            const((1, D_MODEL)),
            const((RT_ROWS, D_MODEL)),
            const((RT_ROWS, 1)),
            const((RT_TM, RT_TM)),
        ],
        out_specs=[
            pl.BlockSpec((MERGE_TM, D_MODEL), row),
            pl.BlockSpec((RT_TM, HX_W), row),
            pl.BlockSpec((1, RT_TM), lambda i: (0, i)),
            pl.BlockSpec((1, RT_TM), lambda i: (0, i)),
            const((CNT_ROWS, COMB_W)),
        ],
        out_shape=[
            jax.ShapeDtypeStruct((N_TOK, D_MODEL), F32),
            jax.ShapeDtypeStruct((N_TOK, HX_W), F32),
            jax.ShapeDtypeStruct((1, N_TOK), jnp.int32),
            jax.ShapeDtypeStruct((1, N_TOK), jnp.int32),
            jax.ShapeDtypeStruct((CNT_ROWS, COMB_W), jnp.int32),
        ],
        scratch_shapes=[pltpu.VMEM((3, BRANCH_W, D_MODEL), BF16), pltpu.VMEM((D_MODEL, D_MODEL), BF16),
                        pltpu.VMEM((2, BRANCH_W, D_MODEL), F32), pltpu.SemaphoreType.DMA((2,)),
                        pltpu.VMEM((CNT_ROWS, COMB_W), F32), pltpu.VMEM((COMB_W, RT_TM), F32)],
        compiler_params=_params(1),
        name="merge_branches",
    )(*xs, mod3, ya, yb_ctx, yb_lat, yc_ctx, yc_lat, gates, wa, wb, wc, wo, ffn_gain, w_rt, b_rt, tri)


RT_TM = 512
RT_ROWS = 32
EXPERT_PAIRS = ((0, 1), (0, 2), (0, 3), (1, 2), (1, 3), (2, 3))
N_PAIRS = len(EXPERT_PAIRS)
N_BUCKETS = N_GROUPS * N_PAIRS
CNT_ROWS = 32
COMB_W = 128
HX_W = D_MODEL + COMB_W
EXP_TM = 256
EXP_TILES = N_TOK // EXP_TM + N_BUCKETS
HS_ROWS = EXP_TILES * EXP_TM


def _split_bf16(x):
    hi = x.astype(BF16)
    return hi, (x - hi.astype(F32)).astype(BF16)


def _route(x, mod_ref, gain_ref, w_ref, b_ref, tri_ref, hx_ref, bucket_ref, rank_ref, cnt_out_ref,
           cnt_ref, cpad_ref):
    @pl.when(pl.program_id(0) == 0)
    def _():
        cnt_ref[...] = jnp.zeros_like(cnt_ref)
        cpad_ref[...] = jnp.zeros_like(cpad_ref)

    m = mod_ref[...]
    shift, scale = m[:, 3 * D_MODEL:4 * D_MODEL], m[:, 4 * D_MODEL:5 * D_MODEL]
    h = _rms(x, gain_ref[...]) * (1.0 + scale) + shift
    hx_ref[:, 0:D_MODEL] = h
    h_hi, h_lo = _split_bf16(h)
    w_hi, w_lo = _split_bf16(w_ref[...])
    dims = (((1,), (1,)), ((), ()))
    logits = (lax.dot_general(w_hi, h_hi, dims, preferred_element_type=F32)
              + lax.dot_general(w_hi, h_lo, dims, preferred_element_type=F32)
              + lax.dot_general(w_lo, h_hi, dims, preferred_element_type=F32)) + b_ref[...]
    gl = [logits[g:g + 1, :] for g in range(N_GROUPS)]
    gmax = functools.reduce(jnp.maximum, gl)
    gid = jnp.full(gmax.shape, N_GROUPS - 1, jnp.int32)
    for g in reversed(range(N_GROUPS - 1)):
        gid = jnp.where(gl[g] == gmax, g, gid)
    p_grp = 1.0 / functools.reduce(jnp.add, [jnp.exp(v - gmax) for v in gl])
    el = []
    for e in range(EXPERTS_PER_GROUP):
        v = logits[8 + e:9 + e, :]
        for g in range(1, N_GROUPS):
            row = 8 + g * EXPERTS_PER_GROUP + e
            v = jnp.where(gid == g, logits[row:row + 1, :], v)
        el.append(v)
    top1 = functools.reduce(jnp.maximum, el)
    idx1 = jnp.full(top1.shape, EXPERTS_PER_GROUP - 1, jnp.int32)
    for e in reversed(range(EXPERTS_PER_GROUP - 1)):
        idx1 = jnp.where(el[e] == top1, e, idx1)
    rest = [jnp.where(idx1 == e, -jnp.inf, el[e]) for e in range(EXPERTS_PER_GROUP)]
    top2 = functools.reduce(jnp.maximum, rest)
    idx2 = jnp.full(top1.shape, EXPERTS_PER_GROUP - 1, jnp.int32)
    for e in reversed(range(EXPERTS_PER_GROUP - 1)):
        idx2 = jnp.where(rest[e] == top2, e, idx2)
    e2 = jnp.exp(top2 - top1)
    w1 = p_grp / (1.0 + e2)
    w2 = p_grp * e2 / (1.0 + e2)
    for e in range(EXPERTS_PER_GROUP):
        cpad_ref[e:e + 1, :] = jnp.where(idx1 == e, w1, 0.0) + jnp.where(idx2 == e, w2, 0.0)
    hx_ref[:, D_MODEL:] = cpad_ref[...].T
    lo = jnp.minimum(idx1, idx2)
    hi = jnp.maximum(idx1, idx2)
    pair = jnp.where(lo == 0, 0, jnp.where(lo == 1, 3, 5)) + (hi - lo - 1)
    bucket = gid * N_PAIRS + pair
    bucket_ref[...] = bucket
    sub = lax.broadcasted_iota(jnp.int32, (CNT_ROWS, RT_TM), 0)
    onehot = jnp.where(sub == bucket, 1.0, 0.0)
    seen = jnp.dot(onehot.astype(BF16), tri_ref[...], preferred_element_type=F32)
    cnt = cnt_ref[...]
    rank_ref[...] = jnp.sum(onehot * (seen - 1.0 + cnt[:, 0:1]), axis=0, keepdims=True).astype(jnp.int32)
    cnt = cnt + jnp.sum(onehot, axis=1, keepdims=True)
    cnt_ref[...] = cnt
    cnt_out_ref[...] = cnt.astype(jnp.int32)


def _router_weights(w_grp, b_grp, w_exp, b_exp):
    w = jnp.zeros((RT_ROWS, D_MODEL), F32).at[0:N_GROUPS].set(w_grp.T).at[8:8 + N_EXPERTS].set(w_exp.T)
    b = jnp.zeros((RT_ROWS, 1), F32).at[0:N_GROUPS, 0].set(b_grp).at[8:8 + N_EXPERTS, 0].set(b_exp)
    return w, b


DISP_TM = 512


ROW_GROUP = 64


def _for_each_row(n_rows, fn):
    def group(k, carry):
        g0 = pl.multiple_of(k * ROW_GROUP, ROW_GROUP)
        for u in range(ROW_GROUP):
            fn(g0, u)
        return carry

    lax.fori_loop(0, n_rows // ROW_GROUP, group, 0)


def _dispatch_kernel(pos_ref, start_ref, cnt_ref, nt_ref, hx_hbm, hs_ref, buf_ref, zero_ref, blk_sem, row_sem, zsem):
    i = pl.program_id(0)
    last = pl.num_programs(0) - 1
    slot = i % 2
    base = i * DISP_TM

    def tile_in(tile, s):
        return pltpu.make_async_copy(hx_hbm.at[pl.ds(pl.multiple_of(tile * DISP_TM, DISP_TM), DISP_TM), :],
                                     buf_ref.at[s], blk_sem.at[s])

    def wait_rows(s):
        pltpu.make_async_copy(buf_ref.at[s], hs_ref.at[pl.ds(0, DISP_TM), :], row_sem.at[s]).wait()

    @pl.when(i == 0)
    def _():
        zero_ref[...] = jnp.zeros_like(zero_ref)

        def tile_copy(t):
            return pltpu.make_async_copy(zero_ref, hs_ref.at[pl.ds(pl.multiple_of(t * EXP_TM, EXP_TM), EXP_TM), :],
                                         zsem)

        def last_tile(g):
            return (start_ref[g] + cnt_ref[g] - 1) // EXP_TM

        def each_tile(fn):
            for g in range(N_BUCKETS):
                pl.when(cnt_ref[g] > 0)(functools.partial(fn, last_tile(g)))
            lax.fori_loop(nt_ref[0], EXP_TILES, lambda t, carry: (fn(t), carry)[1], 0)

        each_tile(lambda t: tile_copy(t).start())
        each_tile(lambda t: tile_copy(t).wait())
        tile_in(0, 0).start()

    @pl.when(i > 0)
    def _():
        wait_rows(1 - slot)

    @pl.when(i < last)
    def _():
        tile_in(i + 1, 1 - slot).start()

    tile_in(i, slot).wait()

    def issue(g0, u):
        src = buf_ref.at[slot, pl.ds(g0, ROW_GROUP), :]
        pltpu.make_async_copy(src.at[pl.ds(u, 1), :], hs_ref.at[pl.ds(pos_ref[base + g0 + u], 1), :],
                              row_sem.at[slot]).start(priority=u % 2)

    _for_each_row(DISP_TM, issue)

    @pl.when(i == last)
    def _():
        wait_rows(slot)


def dispatch(pos, starts, counts, n_tiles, hx):
    return pl.pallas_call(
        _dispatch_kernel,
        grid_spec=pltpu.PrefetchScalarGridSpec(
            num_scalar_prefetch=4,
            grid=(N_TOK // DISP_TM,),
            in_specs=[pl.BlockSpec(memory_space=pl.ANY)],
            out_specs=pl.BlockSpec(memory_space=pl.ANY),
            scratch_shapes=[pltpu.VMEM((2, DISP_TM, HX_W), F32), pltpu.VMEM((EXP_TM, HX_W), F32),
                            pltpu.SemaphoreType.DMA((2,)), pltpu.SemaphoreType.DMA((2,)),
                            pltpu.SemaphoreType.DMA(())],
        ),
        out_shape=jax.ShapeDtypeStruct((HS_ROWS, HX_W), F32),
        compiler_params=_params(1),
        name="dispatch",
    )(pos, starts, counts, n_tiles, hx)


(T_GROUP, T_LO, T_HI, T_SLOT, T_RUN_POS, T_NEXT_GROUP, T_FIRST, T_PREFETCHED,
 T_PENDING_GROUP, T_PENDING_EXPERT, T_PENDING_SLOT) = range(11)


def _experts_kernel(layer, tab_ref, nt_ref, hs_ref, w1_hbm, w3_hbm, w2_hbm, ys_ref,
                    w1_ref, w3_ref, w2_ref, st1_ref, st3_ref, st2_ref, sem_ref):
    t = pl.program_id(0)
    valid = t < nt_ref[0]
    group = tab_ref[T_GROUP, t]
    slot = tab_ref[T_SLOT, t]
    run_pos = tab_ref[T_RUN_POS, t]
    next_group = tab_ref[T_NEXT_GROUP, t]
    pending_expert = tab_ref[T_PENDING_EXPERT, t]

    def expert_copies(grp, e):
        idx = grp * EXPERTS_PER_GROUP + e
        return (pltpu.make_async_copy(w1_hbm.at[layer, idx], st1_ref, sem_ref.at[0]),
                pltpu.make_async_copy(w3_hbm.at[layer, idx], st3_ref, sem_ref.at[1]),
                pltpu.make_async_copy(w2_hbm.at[layer, idx], st2_ref, sem_ref.at[2]))

    def finish(copies, dst_slot, e):
        for cp in copies:
            cp.wait()
        w1_ref[dst_slot, e] = st1_ref[...].astype(BF16)
        w3_ref[dst_slot, e] = st3_ref[...].astype(BF16)
        w2_ref[dst_slot, e] = st2_ref[...].astype(BF16)

    @pl.when(jnp.logical_and(valid, pending_expert >= 0))
    def _():
        finish(expert_copies(tab_ref[T_PENDING_GROUP, t], pending_expert), tab_ref[T_PENDING_SLOT, t], pending_expert)

    @pl.when(jnp.logical_and(valid, tab_ref[T_FIRST, t] == 1))
    def _():
        def load(e, carry):
            copies = expert_copies(group, e)
            for cp in copies:
                cp.start()
            finish(copies, slot, e)
            return carry

        lax.fori_loop(tab_ref[T_PREFETCHED, t], EXPERTS_PER_GROUP, load, 0)

    prefetch = jnp.logical_and(valid, jnp.logical_and(next_group >= 0, run_pos < EXPERTS_PER_GROUP))

    @pl.when(prefetch)
    def _():
        for cp in expert_copies(next_group, run_pos):
            cp.start()

    @pl.when(valid)
    def _():
        h = hs_ref[:, 0:D_MODEL].astype(BF16)
        c = hs_ref[:, D_MODEL:]
        lane = lax.broadcasted_iota(jnp.int32, c.shape, 1)
        acc = None
        for e in (tab_ref[T_LO, t], tab_ref[T_HI, t]):
            ce = jnp.sum(jnp.where(lane == e, c, 0.0), axis=1, keepdims=True)
            a = jnp.dot(h, w1_ref[slot, e], preferred_element_type=F32)
            b = jnp.dot(h, w3_ref[slot, e], preferred_element_type=F32)
            hid = (a * _sigmoid(a)) * b * ce
            y = jnp.dot(hid.astype(BF16), w2_ref[slot, e], preferred_element_type=F32)
            acc = y if acc is None else acc + y
        ys_ref[...] = acc

    @pl.when(jnp.logical_not(valid))
    def _():
        ys_ref[...] = jnp.zeros_like(ys_ref)


def experts(tile_table, n_tiles, hs, w1, w3, w2, layer):
    hbm = pl.BlockSpec(memory_space=pl.ANY)
    return pl.pallas_call(
        functools.partial(_experts_kernel, layer),
        grid_spec=pltpu.PrefetchScalarGridSpec(
            num_scalar_prefetch=2,
            grid=(EXP_TILES,),
            in_specs=[pl.BlockSpec((EXP_TM, HX_W), lambda t, *_: (t, 0)), hbm, hbm, hbm],
            out_specs=pl.BlockSpec((EXP_TM, D_MODEL), lambda t, *_: (t, 0)),
            scratch_shapes=[
                pltpu.VMEM((2, EXPERTS_PER_GROUP, D_MODEL, D_EXPERT), BF16),
                pltpu.VMEM((2, EXPERTS_PER_GROUP, D_MODEL, D_EXPERT), BF16),
                pltpu.VMEM((2, EXPERTS_PER_GROUP, D_EXPERT, D_MODEL), BF16),
                pltpu.VMEM((D_MODEL, D_EXPERT), F32),
                pltpu.VMEM((D_MODEL, D_EXPERT), F32),
                pltpu.VMEM((D_EXPERT, D_MODEL), F32),
                pltpu.SemaphoreType.DMA((3,)),
            ],
        ),
        out_shape=jax.ShapeDtypeStruct((HS_ROWS, D_MODEL), F32),
        compiler_params=_params(1),
        name="experts",
    )(tile_table, n_tiles, hs, w1, w3, w2)


def _expert_tiles(counts):
    tiles = (counts + EXP_TM - 1) // EXP_TM
    ends = jnp.cumsum(tiles)
    n_tiles = ends[-1]
    t = jnp.arange(EXP_TILES, dtype=jnp.int32)
    used = t < n_tiles
    bucket = jnp.minimum(jnp.sum(t[:, None] >= ends[None, :], axis=1), N_BUCKETS - 1)
    group = bucket // N_PAIRS
    pair = bucket % N_PAIRS
    lo = sum(jnp.where(pair == k, p[0], 0) for k, p in enumerate(EXPERT_PAIRS))
    hi = sum(jnp.where(pair == k, p[1], 0) for k, p in enumerate(EXPERT_PAIRS))
    first = jnp.logical_and(used, jnp.logical_or(t == 0, group != jnp.roll(group, 1)))
    run = jnp.cumsum(first) - 1
    run_start = lax.cummax(jnp.where(first, t, 0))
    same_run = jnp.logical_and(run[:, None] == run[None, :], used[None, :])
    run_len = jnp.sum(same_run, axis=1)
    next_first = jnp.logical_and(first[None, :], run[None, :] == run[:, None] + 1)
    next_group = jnp.where(jnp.any(next_first, axis=1), jnp.sum(jnp.where(next_first, group[None, :], 0), axis=1), -1)
    prev_len = jnp.sum(jnp.where(t[None, :] == run_start[:, None] - 1, run_len[None, :], 0), axis=1)
    prefetched = jnp.where(run > 0, jnp.minimum(prev_len, EXPERTS_PER_GROUP), 0)
    slot = run % 2
    run_pos = t - run_start
    starts_fetch = jnp.logical_and(used, jnp.logical_and(next_group >= 0, run_pos < EXPERTS_PER_GROUP))
    prev = lambda v, fill: jnp.concatenate([jnp.full((1,), fill, v.dtype), v[:-1]])
    pending_expert = jnp.where(prev(starts_fetch, False), prev(run_pos, 0), -1)
    table = jnp.stack([group, lo, hi, slot, run_pos, next_group, first, prefetched,
                       prev(next_group, 0), pending_expert, 1 - prev(slot, 0)]).astype(jnp.int32)
    return ((ends - tiles) * EXP_TM).astype(jnp.int32), table, n_tiles.reshape(1).astype(jnp.int32)


RES_TM = 512
RES_CTX_TILES = N_CTX // RES_TM


def _moe_residual_kernel(final, pos_ref, x_ref, mod_ref, gain_ref, ys_ref, *refs):
    out_refs, (ybuf_ref, sem_ref) = refs[:-2], refs[-2:]
    i = pl.program_id(0)
    slot = i % 2

    def gather_tile(tile, tile_slot):
        def issue(g0, u):
            dst = ybuf_ref.at[tile_slot, pl.ds(g0, ROW_GROUP), :]
            pltpu.make_async_copy(ys_ref.at[pl.ds(pos_ref[tile * RES_TM + g0 + u], 1), :], dst.at[pl.ds(u, 1), :],
                                  sem_ref.at[tile_slot]).start(priority=u % 2)

        _for_each_row(RES_TM, issue)

    @pl.when(i == 0)
    def _():
        gather_tile(0, 0)

    @pl.when(i + 1 < pl.num_programs(0))
    def _():
        gather_tile(i + 1, 1 - slot)

    pltpu.make_async_copy(ys_ref.at[pl.ds(0, RES_TM), :], ybuf_ref.at[slot], sem_ref.at[slot]).wait()
    x = x_ref[...] + mod_ref[:, 5 * D_MODEL:6 * D_MODEL] * ybuf_ref[slot]
    if not final:
        out_refs[0][...] = x
        return
    y = _rms(x, gain_ref[...])

    @pl.when(i < RES_CTX_TILES)
    def _():
        out_refs[0][...] = y

    @pl.when(i >= RES_CTX_TILES)
    def _():
        out_refs[1][...] = y


def moe_residual(pos, x, mod3, gain, ys, final):
    if final:
        out_specs = _split_specs((RES_TM, D_MODEL), RES_CTX_TILES)
        out_shape = [jax.ShapeDtypeStruct((N_CTX, D_MODEL), F32), jax.ShapeDtypeStruct((N_LAT, D_MODEL), F32)]
    else:
        out_specs = pl.BlockSpec((RES_TM, D_MODEL), lambda i, *_: (i, 0))
        out_shape = jax.ShapeDtypeStruct((N_TOK, D_MODEL), F32)
    return pl.pallas_call(
        functools.partial(_moe_residual_kernel, final),
        grid_spec=pltpu.PrefetchScalarGridSpec(
            num_scalar_prefetch=1,
            grid=(N_TOK // RES_TM,),
            in_specs=[
                pl.BlockSpec((RES_TM, D_MODEL), lambda i, *_: (i, 0)),
                pl.BlockSpec((None, 1, MOD_WIDTH), lambda i, *_: (_cond_row(i, RES_TM), 0, 0)),
                pl.BlockSpec((1, D_MODEL), lambda i, *_: (0, 0)),
                pl.BlockSpec(memory_space=pl.ANY),
            ],
            out_specs=out_specs,
            scratch_shapes=[pltpu.VMEM((2, RES_TM, D_MODEL), F32), pltpu.SemaphoreType.DMA((2,))],
        ),
        out_shape=out_shape,
        compiler_params=_params(1),
        name="moe_residual_final" if final else "moe_residual",
    )(pos, x, mod3, gain, ys)


def kernel(x_prompt, x_sample, cache_k, cache_v, state_lru, c, c_ctx, w_mod, b_mod, norm_mix, norm_ffn, w_in, sgu_norm, sgu_w, sgu_b, lru_conv_w, lru_conv_b, lru_w_r, lru_b_r, lru_w_i, lru_b_i, lru_lambda, na_rpb, w_branch_sgu, w_branch_lru, w_branch_na, w_out, moe_w_group, moe_b_group, moe_w_expert, moe_b_expert, moe_w1, moe_w3, moe_w2, final_norm_gain):
    xs = (x_prompt.reshape(N_CTX, D_MODEL), x_sample.reshape(N_LAT, D_MODEL))
    cond = jnp.zeros((N_COND, D_MODEL), F32).at[0].set(c_ctx).at[1:1 + DEC_BATCH].set(c)
    mod = modulation(cond, w_mod, b_mod)
    zero_state = jnp.zeros((BATCH, 2, LRU_WIDTH), F32)
    cache_k = cache_k.reshape(DEC_BATCH, DEPTH, PAST_LEN, NA_WIDTH)
    cache_v = cache_v.reshape(DEC_BATCH, DEPTH, PAST_LEN, NA_WIDTH)
    final_gain = final_norm_gain.reshape(1, D_MODEL)
    caches, ss = [], []
    for l in range(DEPTH):
        mod3 = mod[l].reshape(N_COND, 1, MOD_WIDTH)
        ya, za, q, kb, vb, k_ctx, v_ctx, gates = in_projection(
            xs, mod3, norm_mix[l].reshape(1, D_MODEL), w_in, sgu_norm[l].reshape(1, SGU_WIDTH), sgu_w[l], sgu_b[l].T, l,
            prev_caches=caches if l == DEPTH - 1 else ())
        caches.append((k_ctx, v_ctx))
        w_lo, w_hi = _block_diag_gate_weights(lru_w_r[l], lru_w_i[l])
        lru_args = (lru_conv_w[l], lru_conv_b[l].reshape(1, LRU_WIDTH), w_lo, w_hi, lru_b_r[l], lru_b_i[l],
                    lru_lambda[l])
        yb_ctx, st_ctx = rglru(za, *lru_args, zero_state, SEQ, BATCH, 0)
        yb_lat, _ = rglru(za, *lru_args, state_lru[:, l], DEC_SEQ, DEC_BATCH, N_CTX)
        yc_ctx = context_attention(q, kb, vb)
        yc_lat = latent_attention(q, kb, vb, cache_k, cache_v, na_rpb[l].reshape(-1), l)
        w_rt, b_rt = _router_weights(moe_w_group[l], moe_b_group[l], moe_w_expert[l], moe_b_expert[l])
        x, hx, bucket, rank, counts = merge_branches(
            xs, mod3, ya, yb_ctx, yb_lat, yc_ctx, yc_lat, gates, w_branch_sgu, w_branch_lru, w_branch_na, w_out,
            norm_ffn[l].reshape(1, D_MODEL), w_rt, b_rt, l)
        bucket, rank, counts = bucket.reshape(N_TOK), rank.reshape(N_TOK), counts[:N_BUCKETS, 0]
        starts, tile_table, n_tiles = _expert_tiles(counts)
        pos = rank + jnp.sum(jnp.where(bucket[:, None] == jnp.arange(N_BUCKETS), starts[None, :], 0), axis=1)
        hs = dispatch(pos, starts, counts, n_tiles, hx)
        ys = experts(tile_table, n_tiles, hs, moe_w1, moe_w3, moe_w2, l)
        if l < DEPTH - 1:
            xs = (moe_residual(pos, x, mod3, final_gain, ys, False),)
        else:
            y_ctx, y_lat = moe_residual(pos, x, mod3, final_gain, ys, True)
        ss.append(st_ctx)
    cache_shape = (BATCH, DEPTH, SEQ, NA_HEADS, NA_HEAD_DIM)
    return (y_ctx.reshape(BATCH, SEQ, D_MODEL), y_lat.reshape(DEC_BATCH, DEC_SEQ, D_MODEL),
            k_ctx.reshape(cache_shape), v_ctx.reshape(cache_shape), jnp.stack(ss, axis=1))
```

```python
import functools

import jax
import jax.numpy as jnp
import numpy as np
from jax import lax
from jax.experimental import pallas as pl
from jax.experimental.pallas import tpu as pltpu

F32 = jnp.float32
BF16 = jnp.bfloat16

D_MODEL = 1024
BATCH = 16
SEQ = 256
DEPTH = 2
DEC_BATCH = 4
DEC_SEQ = 2048
PAST_LEN = 512
GRID_W = 64
CHUNK = 128
SGU_WIDTH = 512
SGU_GROUPS = 4
LRU_WIDTH = 512
LRU_BLOCKS = 8
CONV_WIDTH = 4
LRU_C = 8.0
NA_HEADS = 8
NA_HEAD_DIM = 64
NA_WIDTH = 512
WIN_ROWS = 8
WIN_COLS = 16
N_GROUPS = 4
EXPERTS_PER_GROUP = 4
N_EXPERTS = 16
D_EXPERT = 512
IN_WIDTH = 6656
EPS = 1e-6
NEG_INF = -1e30

N_CTX = BATCH * SEQ
N_LAT = DEC_BATCH * DEC_SEQ
N_TOK = N_CTX + N_LAT
N_COND = 8
MOD_WIDTH = 6 * D_MODEL
GRID_ROWS = DEC_SEQ // GRID_W

VMEM_LIMIT_BYTES = 56 * 1024 * 1024


def _params(n_axes):
    return pltpu.CompilerParams(dimension_semantics=("arbitrary",) * n_axes,
                                vmem_limit_bytes=VMEM_LIMIT_BYTES)


def _cond_row(tile, tile_rows):
    tok = tile * tile_rows
    return jnp.where(tok < N_CTX, 0, 1 + (tok - N_CTX) // DEC_SEQ)


def _layer_spec(shape, layer):
    return pl.BlockSpec((None,) + tuple(shape), lambda i, *_: (layer,) + (0,) * len(shape))


def _mod_spec(layer, tile_rows):
    return pl.BlockSpec((None, None, 1, MOD_WIDTH), lambda i, *_: (layer, _cond_row(i, tile_rows), 0, 0))


def _rms(x, gain):
    return x * lax.rsqrt(jnp.mean(x * x, axis=-1, keepdims=True) + EPS) * gain


def _bdot(a, b):
    return jnp.dot(a.astype(BF16), b.astype(BF16), preferred_element_type=F32)


def _bdot_t(a, b):
    return lax.dot_general(a.astype(BF16), b.astype(BF16), (((1,), (1,)), ((), ())),
                           preferred_element_type=F32)


MOD_TN = 1536


def _mod_kernel(cond_ref, w_ref, b_ref, o_ref):
    c = cond_ref[...]
    s = c * jax.nn.sigmoid(c)
    o_ref[...] = _bdot(s, w_ref[...]) + b_ref[...]


def modulation(cond, w_mod, b_mod):
    return pl.pallas_call(
        _mod_kernel,
        grid=(DEPTH, MOD_WIDTH // MOD_TN),
        in_specs=[
            pl.BlockSpec((N_COND, D_MODEL), lambda l, j: (0, 0)),
            pl.BlockSpec((None, D_MODEL, MOD_TN), lambda l, j: (l, 0, j)),
            pl.BlockSpec((None, 1, MOD_TN), lambda l, j: (l, 0, j)),
        ],
        out_specs=pl.BlockSpec((None, N_COND, MOD_TN), lambda l, j: (l, 0, j)),
        out_shape=jax.ShapeDtypeStruct((DEPTH, N_COND, MOD_WIDTH), F32),
        compiler_params=_params(2),
        name="modulation",
    )(cond, w_mod, b_mod.reshape(DEPTH, 1, MOD_WIDTH))


IN_TM = 256
IN_CTX_TILES = N_CTX // IN_TM
ZA_WIDTH = 4 * 512
KV_OFF = ZA_WIDTH + NA_WIDTH
GATE_OFF = KV_OFF + 2 * NA_WIDTH
W_CHUNK = 512
Q_SCALE = NA_HEAD_DIM ** -0.5 * float(np.log2(np.e))


def _load_cast(chunks, stage_ref, sem_ref):
    def copy(j):
        return pltpu.make_async_copy(chunks[j][0], stage_ref.at[j % 2], sem_ref.at[j % 2])

    copy(0).start()
    for j in range(len(chunks)):
        if j + 1 < len(chunks):
            copy(j + 1).start()
        copy(j).wait()
        chunks[j][1](stage_ref[j % 2].astype(BF16))


def _token_tile(i, n_ctx_tiles, refs):
    if len(refs) == 1:
        return refs[0][...]
    return jnp.where(i < n_ctx_tiles, refs[0][...], refs[1][...])


def _split_specs(block, n_ctx_tiles):
    return [pl.BlockSpec(block, lambda i, *_: (jnp.minimum(i, n_ctx_tiles - 1), 0)),
            pl.BlockSpec(block, lambda i, *_: (jnp.maximum(i - n_ctx_tiles, 0), 0))]


SGU_GD = SGU_WIDTH // SGU_GROUPS


def _spatial_gating(u, v, gain, ws_ref, bs_ref):
    u = jax.nn.gelu(u)
    v = _rms(jax.nn.gelu(v), gain).astype(BF16)
    out = []
    for g in range(SGU_GROUPS):
        cols = slice(g * SGU_GD, (g + 1) * SGU_GD)
        mixed = jnp.dot(ws_ref[g].astype(BF16), v[:, cols], preferred_element_type=F32) + bs_ref[:, g:g + 1]
        out.append(u[:, cols] * mixed)
    return jnp.concatenate(out, axis=1)


def _inproj_kernel(layer, n_x, n_prev, *refs):
    x_refs, (mod_ref, gain_ref, w_hbm, sgu_gain_ref, ws_ref, bs_ref) = refs[:n_x], refs[n_x:n_x + 6]
    prev_refs, rest = refs[n_x + 6:n_x + 6 + 2 * n_prev], refs[n_x + 6 + 2 * n_prev:]
    ya_ref, zb_ref, q_ref, kb_ref, vb_ref, kc_ref, vc_ref, g_ref, w_ref, stage_ref, sem_ref = rest
    i = pl.program_id(0)

    @pl.when(i == 0)
    def _():
        def store(c):
            def st(v):
                w_ref[:, c * W_CHUNK:(c + 1) * W_CHUNK] = v
            return st

        _load_cast([(w_hbm.at[layer, :, pl.ds(c * W_CHUNK, W_CHUNK)], store(c)) for c in range(IN_WIDTH // W_CHUNK)],
                   stage_ref, sem_ref)

    m = mod_ref[...]
    shift, scale = m[:, 0:D_MODEL], m[:, D_MODEL:2 * D_MODEL]
    h = (_rms(_token_tile(i, IN_CTX_TILES, x_refs), gain_ref[...]) * (1.0 + scale) + shift).astype(BF16)
    za = jnp.dot(h, w_ref[:, 0:ZA_WIDTH], preferred_element_type=F32)
    for c in range(IN_TM // CHUNK):
        rows = slice(c * CHUNK, (c + 1) * CHUNK)
        ya_ref[rows, :] = _spatial_gating(za[rows, 0:SGU_WIDTH], za[rows, SGU_WIDTH:2 * SGU_WIDTH],
                                          sgu_gain_ref[...], ws_ref, bs_ref).astype(BF16)
    zb_ref[...] = za[:, 2 * SGU_WIDTH:].astype(BF16)
    q_ref[...] = (jnp.dot(h, w_ref[:, ZA_WIDTH:KV_OFF], preferred_element_type=F32) * Q_SCALE).astype(BF16)
    kv = jnp.dot(h, w_ref[:, KV_OFF:GATE_OFF], preferred_element_type=F32)
    kb_ref[...] = kv[:, 0:NA_WIDTH].astype(BF16)
    vb_ref[...] = kv[:, NA_WIDTH:].astype(BF16)
    g_ref[...] = jnp.dot(h, w_ref[:, GATE_OFF:], preferred_element_type=F32).astype(BF16)

    @pl.when(i < IN_CTX_TILES)
    def _():
        if n_prev == 0:
            kc_ref[...] = kv[:, 0:NA_WIDTH]
            vc_ref[...] = kv[:, NA_WIDTH:]
        else:
            for p in range(n_prev):
                kc_ref[p] = prev_refs[2 * p][...]
                vc_ref[p] = prev_refs[2 * p + 1][...]
            kc_ref[n_prev] = kv[:, 0:NA_WIDTH]
            vc_ref[n_prev] = kv[:, NA_WIDTH:]


def in_projection(xs, mod3, gain, w_in, sgu_gain, sgu_w, sgu_b_t, layer, prev_caches=()):
    assert IN_TM == SEQ
    row = lambda i: (i, 0)
    ctx_row = lambda i: (jnp.minimum(i, IN_CTX_TILES - 1), 0)
    const = lambda shape: pl.BlockSpec(shape, lambda i: (0,) * len(shape))
    x_specs = ([pl.BlockSpec((IN_TM, D_MODEL), row)] if len(xs) == 1
               else _split_specs((IN_TM, D_MODEL), IN_CTX_TILES))
    n_prev = len(prev_caches)
    if n_prev:
        cache_spec = pl.BlockSpec((None, n_prev + 1, SEQ, NA_WIDTH),
                                  lambda i: (jnp.minimum(i, IN_CTX_TILES - 1), 0, 0, 0))
        cache_shape = jax.ShapeDtypeStruct((BATCH, n_prev + 1, SEQ, NA_WIDTH), F32)
    else:
        cache_spec = pl.BlockSpec((IN_TM, NA_WIDTH), ctx_row)
        cache_shape = jax.ShapeDtypeStruct((N_CTX, NA_WIDTH), F32)
    return pl.pallas_call(
        functools.partial(_inproj_kernel, layer, len(xs), n_prev),
        grid=(N_TOK // IN_TM,),
        in_specs=x_specs + [
            _mod_spec(layer, IN_TM),
            _layer_spec((1, D_MODEL), layer),
            pl.BlockSpec(memory_space=pl.ANY),
            _layer_spec((1, SGU_WIDTH), layer),
            _layer_spec((SGU_GROUPS, CHUNK, CHUNK), layer),
            _layer_spec((CHUNK, SGU_GROUPS), layer),
        ] + [pl.BlockSpec((IN_TM, NA_WIDTH), ctx_row)] * (2 * n_prev),
        out_specs=[
            pl.BlockSpec((IN_TM, SGU_WIDTH), row),
            pl.BlockSpec((IN_TM, 2 * LRU_WIDTH), row),
            pl.BlockSpec((IN_TM, NA_WIDTH), row),
            pl.BlockSpec((IN_TM, NA_WIDTH), row),
            pl.BlockSpec((IN_TM, NA_WIDTH), row),
            cache_spec,
            cache_spec,
            pl.BlockSpec((IN_TM, 3 * D_MODEL), row),
        ],
        out_shape=[
            jax.ShapeDtypeStruct((N_TOK, SGU_WIDTH), BF16),
            jax.ShapeDtypeStruct((N_TOK, 2 * LRU_WIDTH), BF16),
            jax.ShapeDtypeStruct((N_TOK, NA_WIDTH), BF16),
            jax.ShapeDtypeStruct((N_TOK, NA_WIDTH), BF16),
            jax.ShapeDtypeStruct((N_TOK, NA_WIDTH), BF16),
            cache_shape,
            cache_shape,
            jax.ShapeDtypeStruct((N_TOK, 3 * D_MODEL), BF16),
        ],
        scratch_shapes=[pltpu.VMEM((D_MODEL, IN_WIDTH), BF16), pltpu.VMEM((2, D_MODEL, W_CHUNK), F32),
                        pltpu.SemaphoreType.DMA((2,))],
        compiler_params=_params(1),
        name="in_projection",
    )(*xs, mod3, gain, w_in, sgu_gain, sgu_w, sgu_b_t, *[c for kv in prev_caches for c in kv])


LRU_TC = 256
LRU_HALF = 256
SUB = 8
HALO = 8
LANES = 128
LRU_SLABS = LRU_WIDTH // LANES


def _sigmoid(x):
    return 0.5 * jnp.tanh(0.5 * x) + 0.5


def _lru_kernel(seq_len, xr_ref, gr_ref, cw_ref, cb_ref, wlo_ref, whi_ref, br_ref, bi_ref,
                lam_ref, h0_ref, y_ref, st_ref, xp_ref, a_ref, b_ref, h_ref):
    seg_len = seq_len // SUB
    pitch = seg_len + SUB
    segs_per_chunk = max(LRU_TC // seg_len, 1)
    n_chunks = seq_len // LRU_TC
    zeros = jnp.zeros((HALO, LRU_WIDTH), F32)
    xp_ref[0:HALO, :] = zeros
    xp_ref[seq_len + HALO:seq_len + 2 * HALO, :] = zeros

    def copy_in(c, carry):
        r0 = pl.multiple_of(c * LRU_TC, LRU_TC)
        xp_ref[pl.ds(r0 + HALO, LRU_TC), :] = xr_ref[pl.ds(r0, LRU_TC), :].astype(F32)
        return carry

    lax.fori_loop(0, n_chunks, copy_in, 0)

    def chunk_rows(c, seg):
        n = min(seg_len, LRU_TC)
        start = pl.multiple_of((c * segs_per_chunk + seg) * pitch, SUB)
        return pl.ds(start, n), slice(seg * n, (seg + 1) * n)

    cw = cw_ref[...]
    cb = cb_ref[...]
    win = LRU_TC + 2 * HALO
    neg_lam = -lam_ref[...]
    softplus = jnp.maximum(neg_lam, 0.0) + jnp.log1p(jnp.exp(-jnp.abs(neg_lam)))
    decay = (0.5 * LRU_C) * softplus
    half_br = 0.5 * br_ref[...]
    half_bi = 0.5 * bi_ref[...]

    def gates(c, carry):
        r0 = pl.multiple_of(c * LRU_TC, LRU_TC)
        w = xp_ref[pl.ds(r0, win), :]
        xc = (cw[0:1, :] * pltpu.roll(w, 1, 0)[HALO:HALO + LRU_TC]
              + cw[1:2, :] * w[HALO:HALO + LRU_TC]
              + cw[2:3, :] * pltpu.roll(w, win - 1, 0)[HALO:HALO + LRU_TC]
              + cw[3:4, :] * pltpu.roll(w, win - 2, 0)[HALO:HALO + LRU_TC]) + cb
        xb = xc.astype(BF16)
        for half, w_ref in enumerate((wlo_ref, whi_ref)):
            cols = slice(half * LRU_HALF, (half + 1) * LRU_HALF)
            pre = jnp.dot(xb[:, cols], w_ref[...], preferred_element_type=F32)
            half_x = 0.5 * xc[:, cols]
            for d in range(2):
                tr = jnp.tanh(pre[:, (2 * d) * LRU_HALF:(2 * d + 1) * LRU_HALF] + half_br[d:d + 1, cols])
                ti = jnp.tanh(pre[:, (2 * d + 1) * LRU_HALF:(2 * d + 2) * LRU_HALF] + half_bi[d:d + 1, cols])
                neg_log_a = decay[d:d + 1, cols] * tr + decay[d:d + 1, cols]
                a = jnp.exp2(neg_log_a * -LOG2E)
                z = jnp.tanh(neg_log_a) * (a * a + 1.0)
                b = jnp.where(z > 0.0, z * lax.rsqrt(z), 0.0) * (half_x * ti + half_x)
                for k in range(LRU_HALF // LANES):
                    slab = half * (LRU_HALF // LANES) + k
                    lanes = slice(k * LANES, (k + 1) * LANES)
                    for seg in range(segs_per_chunk):
                        dst, src = chunk_rows(c, seg)
                        a_ref[d, slab, dst, :] = a[src, lanes]
                        b_ref[d, slab, dst, :] = b[src, lanes]
        return carry

    lax.fori_loop(0, n_chunks, gates, 0)

    def step_rows(d, j):
        return pl.ds(j if d == 0 else seg_len - 1 - j, SUB, stride=pitch)

    def reduce_step(j, carry):
        out = []
        for d in range(2):
            for slab in range(LRU_SLABS):
                big_a, big_b = carry[d * LRU_SLABS + slab]
                a = a_ref[d, slab, step_rows(d, j), :]
                b = b_ref[d, slab, step_rows(d, j), :]
                out.append((a * big_a, a * big_b + b))
        return tuple(out)

    ident = (jnp.ones((SUB, LANES), F32), jnp.zeros((SUB, LANES), F32))
    totals = lax.fori_loop(0, seg_len, reduce_step, (ident,) * (2 * LRU_SLABS), unroll=8)

    sub = lax.broadcasted_iota(jnp.int32, (SUB, LANES), 0)
    entering = []
    for d in range(2):
        for slab in range(LRU_SLABS):
            big_a, big_b = totals[d * LRU_SLABS + slab]
            h = h0_ref[d:d + 1, slab * LANES:(slab + 1) * LANES]
            rows = jnp.zeros((SUB, LANES), F32)
            for s in (range(SUB) if d == 0 else reversed(range(SUB))):
                rows = jnp.where(sub == s, h, rows)
                h = big_a[s:s + 1, :] * h + big_b[s:s + 1, :]
            entering.append(rows)
            st_ref[d:d + 1, slab * LANES:(slab + 1) * LANES] = h

    def apply_step(j, carry):
        out = []
        for d in range(2):
            for slab in range(LRU_SLABS):
                h = (a_ref[d, slab, step_rows(d, j), :] * carry[d * LRU_SLABS + slab]
                     + b_ref[d, slab, step_rows(d, j), :])
                h_ref[d, slab, step_rows(d, j), :] = h
                out.append(h)
        return tuple(out)

    lax.fori_loop(0, seg_len, apply_step, tuple(entering), unroll=8)

    def merge(c, carry):
        r0 = pl.multiple_of(c * LRU_TC, LRU_TC)
        gate = jax.nn.gelu(gr_ref[pl.ds(r0, LRU_TC), :].astype(F32))
        for slab in range(LRU_SLABS):
            lanes = slice(slab * LANES, (slab + 1) * LANES)
            for seg in range(segs_per_chunk):
                src, dst = chunk_rows(c, seg)
                h = h_ref[0, slab, src, :] + h_ref[1, slab, src, :]
                y_ref[pl.ds(r0 + dst.start, dst.stop - dst.start), lanes] = (h * gate[dst, lanes]).astype(BF16)
        return carry

    lax.fori_loop(0, n_chunks, merge, 0)


def rglru(za, conv_w, conv_b, w_gate, b_r, b_i, lam, h0, h0_spec, seq_len, n_seq, tok_off, layer):
    blk0 = tok_off // seq_len
    half_spec = lambda half: pl.BlockSpec((None, None, LRU_HALF, 4 * LRU_HALF), lambda i: (layer, half, 0, 0))
    return pl.pallas_call(
        functools.partial(_lru_kernel, seq_len),
        grid=(n_seq,),
        in_specs=[
            pl.BlockSpec((seq_len, LRU_WIDTH), lambda i: (i + blk0, 0)),
            pl.BlockSpec((seq_len, LRU_WIDTH), lambda i: (i + blk0, 1)),
            _layer_spec((CONV_WIDTH, LRU_WIDTH), layer),
            _layer_spec((1, LRU_WIDTH), layer),
            half_spec(0),
            half_spec(1),
            _layer_spec((2, LRU_WIDTH), layer),
            _layer_spec((2, LRU_WIDTH), layer),
            _layer_spec((2, LRU_WIDTH), layer),
            h0_spec,
        ],
        out_specs=[
            pl.BlockSpec((seq_len, LRU_WIDTH), lambda i: (i, 0)),
            pl.BlockSpec((None, 2, LRU_WIDTH), lambda i: (i, 0, 0)),
        ],
        out_shape=[
            jax.ShapeDtypeStruct((n_seq * seq_len, LRU_WIDTH), BF16),
            jax.ShapeDtypeStruct((n_seq, 2, LRU_WIDTH), F32),
        ],
        scratch_shapes=[
            pltpu.VMEM((seq_len + 2 * HALO, LRU_WIDTH), F32),
            pltpu.VMEM((2, LRU_SLABS, seq_len + SUB * SUB, LANES), F32),
            pltpu.VMEM((2, LRU_SLABS, seq_len + SUB * SUB, LANES), F32),
            pltpu.VMEM((2, LRU_SLABS, seq_len + SUB * SUB, LANES), F32),
        ],
        compiler_params=_params(1),
        name=f"rglru_{seq_len}",
    )(za, za, conv_w, conv_b, w_gate, w_gate, b_r, b_i, lam, h0)


def _block_diag_gate_weights(w_r, w_i):
    per_half = LRU_BLOCKS // 2
    blk = LRU_WIDTH // LRU_BLOCKS
    g = jnp.stack([w_r, w_i], axis=2).reshape(DEPTH, 2, 2, 2, per_half, blk, blk)
    w = jnp.einsum("ldghaij,ab->lhaidgbj", g, jnp.eye(per_half, dtype=g.dtype))
    return (0.5 * w.reshape(DEPTH, 2, LRU_HALF, 4 * LRU_HALF)).astype(BF16)


PAIR_W = 2 * NA_HEAD_DIM
LOG2E = float(np.log2(np.e))


def _head_lanes(shape, head):
    lane = lax.broadcasted_iota(jnp.int32, shape, 1)
    return (lane < NA_HEAD_DIM) if head == 0 else (lane >= NA_HEAD_DIM)


def _one_head(q_pair, head):
    return jnp.where(_head_lanes(q_pair.shape, head), q_pair.astype(F32), 0.0).astype(BF16)


def _join_heads(o0, o1):
    return jnp.where(_head_lanes(o0.shape, 0), o0, o1)


def _ctx_attn_kernel(q_ref, k_ref, v_ref, o_ref):
    for j in range(NA_HEADS // 2):
        cols = slice(j * PAIR_W, (j + 1) * PAIR_W)
        outs = []
        for head in range(2):
            s = _bdot_t(_one_head(q_ref[:, cols], head), k_ref[:, cols])
            p = jnp.exp2(s - jnp.max(s, axis=-1, keepdims=True))
            denom = jnp.sum(p, axis=-1, keepdims=True)
            outs.append(jnp.dot(p.astype(BF16), v_ref[:, cols], preferred_element_type=F32) / denom)
        o_ref[:, cols] = _join_heads(*outs).astype(BF16)


def context_attention(q, kb, vb):
    spec = pl.BlockSpec((SEQ, NA_WIDTH), lambda i: (i, 0))
    return pl.pallas_call(
        _ctx_attn_kernel,
        grid=(BATCH,),
        in_specs=[spec, spec, spec],
        out_specs=spec,
        out_shape=jax.ShapeDtypeStruct((N_CTX, NA_WIDTH), BF16),
        compiler_params=_params(1),
        name="context_attention",
    )(q, kb, vb)


NA_QROWS = 4
NA_TQ = NA_QROWS * GRID_W
NA_KROWS = NA_QROWS + WIN_ROWS
NA_TK = NA_KROWS * GRID_W
NA_QBLOCKS = GRID_ROWS // NA_QROWS


N_DROW = 2 * WIN_ROWS - 1
N_DCOL = 2 * WIN_COLS - 1
NA_BLOCK_KINDS = (0, 1, NA_QBLOCKS - 1)


def _first_key_row(qb):
    return (np.clip if isinstance(qb, int) else jnp.clip)(qb * NA_QROWS - WIN_ROWS // 2, 0, GRID_ROWS - NA_KROWS)


def _build_bias(layer, rpb_ref, table_ref, bias_ref):
    qc = lax.broadcasted_iota(jnp.int32, (GRID_W, GRID_W), 0)
    kc = lax.broadcasted_iota(jnp.int32, (GRID_W, GRID_W), 1)
    col0 = jnp.clip(qc - WIN_COLS // 2, 0, GRID_W - WIN_COLS)
    col_ok = jnp.logical_and(kc >= col0, kc < col0 + WIN_COLS)
    d_col = jnp.clip(kc - qc, 1 - WIN_COLS, WIN_COLS - 1) + WIN_COLS - 1
    neg = jnp.full((GRID_W, GRID_W), NEG_INF, F32)

    def table_entry(idx, carry):
        t = neg
        for j in range(N_DCOL):
            t = jnp.where(d_col == j, rpb_ref[layer, idx * N_DCOL + j], t)
        table_ref[idx] = jnp.where(col_ok, t * LOG2E, neg)
        return carry

    lax.fori_loop(0, NA_HEADS * N_DROW, table_entry, 0)

    def head_blocks(h, carry):
        for kind, qb in enumerate(NA_BLOCK_KINDS):
            for i in range(NA_QROWS):
                qrow = qb * NA_QROWS + i
                win0 = int(np.clip(qrow - WIN_ROWS // 2, 0, GRID_ROWS - WIN_ROWS))
                for kr in range(NA_KROWS):
                    krow = int(_first_key_row(qb)) + kr
                    inside = win0 <= krow < win0 + WIN_ROWS
                    blk = table_ref[h * N_DROW + (krow - qrow + WIN_ROWS - 1)] if inside else neg
                    bias_ref[kind, h, i * GRID_W:(i + 1) * GRID_W, kr * GRID_W:(kr + 1) * GRID_W] = blk
        return carry

    lax.fori_loop(0, NA_HEADS, head_blocks, 0)


def _lat_attn_kernel(layer, rpb_ref, q_ref, k_ref, v_ref, ck_ref, cv_ref, o_ref, table_ref, bias_ref):
    qb = pl.program_id(1)

    @pl.when(jnp.logical_and(pl.program_id(0) == 0, qb == 0))
    def _():
        _build_bias(layer, rpb_ref, table_ref, bias_ref)

    kind = jnp.where(qb == 0, 0, jnp.where(qb == NA_QBLOCKS - 1, 2, 1))
    k0 = pl.multiple_of(_first_key_row(qb) * GRID_W, GRID_W)
    for j in range(NA_HEADS // 2):
        cols = slice(j * PAIR_W, (j + 1) * PAIR_W)
        k_loc = k_ref[pl.ds(k0, NA_TK), cols]
        v_loc = v_ref[pl.ds(k0, NA_TK), cols]
        k_ctx = ck_ref[:, cols].astype(BF16)
        v_ctx = cv_ref[:, cols].astype(BF16)
        outs = []
        for head in range(2):
            qh = _one_head(q_ref[:, cols], head)
            s_loc = _bdot_t(qh, k_loc) + bias_ref[kind, 2 * j + head]
            s_ctx = _bdot_t(qh, k_ctx)
            m = jnp.maximum(jnp.max(s_loc, axis=-1, keepdims=True), jnp.max(s_ctx, axis=-1, keepdims=True))
            p_loc = jnp.exp2(s_loc - m)
            p_ctx = jnp.exp2(s_ctx - m)
            denom = jnp.sum(p_loc, axis=-1, keepdims=True) + jnp.sum(p_ctx, axis=-1, keepdims=True)
            o = (jnp.dot(p_loc.astype(BF16), v_loc, preferred_element_type=F32)
                 + jnp.dot(p_ctx.astype(BF16), v_ctx, preferred_element_type=F32))
            outs.append(o / denom)
        o_ref[:, cols] = _join_heads(*outs).astype(BF16)


def latent_attention(q, kb, vb, cache_k, cache_v, rpb, layer):
    q_blk0 = N_CTX // NA_TQ
    s_blk0 = N_CTX // DEC_SEQ
    cache_spec = pl.BlockSpec((None, None, PAST_LEN, NA_WIDTH), lambda b, m: (b, layer, 0, 0))
    return pl.pallas_call(
        functools.partial(_lat_attn_kernel, layer),
        grid=(DEC_BATCH, NA_QBLOCKS),
        in_specs=[
            pl.BlockSpec(memory_space=pltpu.SMEM),
            pl.BlockSpec((NA_TQ, NA_WIDTH), lambda b, m: (q_blk0 + b * NA_QBLOCKS + m, 0)),
            pl.BlockSpec((DEC_SEQ, NA_WIDTH), lambda b, m: (s_blk0 + b, 0)),
            pl.BlockSpec((DEC_SEQ, NA_WIDTH), lambda b, m: (s_blk0 + b, 0)),
            cache_spec,
            cache_spec,
        ],
        out_specs=pl.BlockSpec((NA_TQ, NA_WIDTH), lambda b, m: (b * NA_QBLOCKS + m, 0)),
        out_shape=jax.ShapeDtypeStruct((N_LAT, NA_WIDTH), BF16),
        scratch_shapes=[pltpu.VMEM((NA_HEADS * N_DROW, GRID_W, GRID_W), F32),
                        pltpu.VMEM((len(NA_BLOCK_KINDS), NA_HEADS, NA_TQ, NA_TK), F32)],
        compiler_params=_params(2),
        name="latent_attention",
    )(rpb, q, kb, vb, cache_k, cache_v)


MERGE_TM = 512


MERGE_CTX_TILES = N_CTX // MERGE_TM
BRANCH_W = 512


def _merge_kernel(layer, n_x, *refs):
    x_refs, rest = refs[:n_x], refs[n_x:]
    (mod_ref, ya_ref, yb_ctx_ref, yb_lat_ref, yc_ctx_ref, yc_lat_ref, g_ref, wa_hbm, wb_hbm, wc_hbm, wo_hbm,
     ffn_gain_ref, wrt_ref, brt_ref, tri_ref,
     o_ref, hx_ref, bucket_ref, rank_ref, cnt_out_ref,
     wbr_ref, wo_ref, stage_ref, sem_ref, cnt_ref, cpad_ref) = rest
    i = pl.program_id(0)

    @pl.when(i == 0)
    def _():
        def store_branch(k):
            def st(v):
                wbr_ref[k] = v
            return st

        def store_out(k):
            def st(v):
                wo_ref[k * BRANCH_W:(k + 1) * BRANCH_W, :] = v
            return st

        chunks = [(w.at[layer], store_branch(k)) for k, w in enumerate((wa_hbm, wb_hbm, wc_hbm))]
        chunks += [(wo_hbm.at[layer, pl.ds(k * BRANCH_W, BRANCH_W), :], store_out(k))
                   for k in range(D_MODEL // BRANCH_W)]
        _load_cast(chunks, stage_ref, sem_ref)

    g = _sigmoid(g_ref[...].astype(F32))
    yb = _token_tile(i, MERGE_CTX_TILES, (yb_ctx_ref, yb_lat_ref))
    yc = _token_tile(i, MERGE_CTX_TILES, (yc_ctx_ref, yc_lat_ref))
    merged = (g[:, 0:D_MODEL] * jnp.dot(ya_ref[...], wbr_ref[0], preferred_element_type=F32)
              + g[:, D_MODEL:2 * D_MODEL] * jnp.dot(yb, wbr_ref[1], preferred_element_type=F32)
              + g[:, 2 * D_MODEL:] * jnp.dot(yc, wbr_ref[2], preferred_element_type=F32))
    y = jnp.dot(merged.astype(BF16), wo_ref[...], preferred_element_type=F32)
    gate = mod_ref[:, 2 * D_MODEL:3 * D_MODEL]
    x = _token_tile(i, MERGE_CTX_TILES, x_refs) + gate * y
    o_ref[...] = x
    _route(x, mod_ref, ffn_gain_ref, wrt_ref, brt_ref, tri_ref, hx_ref, bucket_ref, rank_ref, cnt_out_ref,
           cnt_ref, cpad_ref)


def merge_branches(xs, mod3, ya, yb_ctx, yb_lat, yc_ctx, yc_lat, gates, wa, wb, wc, wo, ffn_gain, w_rt, b_rt, layer):
    assert MERGE_TM == RT_TM
    row = lambda i: (i, 0)
    const = lambda shape: pl.BlockSpec(shape, lambda i: (0,) * len(shape))
    tri = jnp.asarray(np.triu(np.ones((RT_TM, RT_TM), np.float32)), BF16)
    hbm = pl.BlockSpec(memory_space=pl.ANY)
    x_specs = ([pl.BlockSpec((MERGE_TM, D_MODEL), row)] if len(xs) == 1
               else _split_specs((MERGE_TM, D_MODEL), MERGE_CTX_TILES))
    return pl.pallas_call(
        functools.partial(_merge_kernel, layer, len(xs)),
        grid=(N_TOK // MERGE_TM,),
        in_specs=x_specs + [
            _mod_spec(layer, MERGE_TM),
            pl.BlockSpec((MERGE_TM, SGU_WIDTH), row),
            *_split_specs((MERGE_TM, LRU_WIDTH), MERGE_CTX_TILES),
            *_split_specs((MERGE_TM, NA_WIDTH), MERGE_CTX_TILES),
            pl.BlockSpec((MERGE_TM, 3 * D_MODEL), row),
            hbm, hbm, hbm, hbm,
            _layer_spec((1, D_MODEL), layer),
            _layer_spec((RT_ROWS, D_MODEL), layer),
            _layer_spec((RT_ROWS, 1), layer),
            const((RT_TM, RT_TM)),
        ],
        out_specs=[
            pl.BlockSpec((MERGE_TM, D_MODEL), row),
            pl.BlockSpec((RT_TM, HX_W), row),
            pl.BlockSpec((1, RT_TM), lambda i: (0, i)),
            pl.BlockSpec((1, RT_TM), lambda i: (0, i)),
            const((CNT_ROWS, COMB_W)),
        ],
        out_shape=[
            jax.ShapeDtypeStruct((N_TOK, D_MODEL), F32),
            jax.ShapeDtypeStruct((N_TOK, HX_W), F32),
            jax.ShapeDtypeStruct((1, N_TOK), jnp.int32),
            jax.ShapeDtypeStruct((1, N_TOK), jnp.int32),
            jax.ShapeDtypeStruct((CNT_ROWS, COMB_W), jnp.int32),
        ],
        scratch_shapes=[pltpu.VMEM((3, BRANCH_W, D_MODEL), BF16), pltpu.VMEM((D_MODEL, D_MODEL), BF16),
                        pltpu.VMEM((2, BRANCH_W, D_MODEL), F32), pltpu.SemaphoreType.DMA((2,)),
                        pltpu.VMEM((CNT_ROWS, COMB_W), F32), pltpu.VMEM((COMB_W, RT_TM), F32)],
        compiler_params=_params(1),
        name="merge_branches",
    )(*xs, mod3, ya, yb_ctx, yb_lat, yc_ctx, yc_lat, gates, wa, wb, wc, wo, ffn_gain, w_rt, b_rt, tri)


RT_TM = 512
RT_ROWS = 32
EXPERT_PAIRS = ((0, 1), (0, 2), (0, 3), (1, 2), (1, 3), (2, 3))
N_PAIRS = len(EXPERT_PAIRS)
N_BUCKETS = N_GROUPS * N_PAIRS
CNT_ROWS = 32
COMB_W = 128
HX_W = D_MODEL + COMB_W
EXP_TM = 256
EXP_TILES = N_TOK // EXP_TM + N_BUCKETS
HS_ROWS = EXP_TILES * EXP_TM


def _split_bf16(x):
    hi = x.astype(BF16)
    return hi, (x - hi.astype(F32)).astype(BF16)


def _route(x, mod_ref, gain_ref, w_ref, b_ref, tri_ref, hx_ref, bucket_ref, rank_ref, cnt_out_ref,
           cnt_ref, cpad_ref):
    @pl.when(pl.program_id(0) == 0)
    def _():
        cnt_ref[...] = jnp.zeros_like(cnt_ref)
        cpad_ref[...] = jnp.zeros_like(cpad_ref)

    m = mod_ref[...]
    shift, scale = m[:, 3 * D_MODEL:4 * D_MODEL], m[:, 4 * D_MODEL:5 * D_MODEL]
    h = _rms(x, gain_ref[...]) * (1.0 + scale) + shift
    hx_ref[:, 0:D_MODEL] = h
    h_hi, h_lo = _split_bf16(h)
    w_hi, w_lo = _split_bf16(w_ref[...])
    dims = (((1,), (1,)), ((), ()))
    logits = (lax.dot_general(w_hi, h_hi, dims, preferred_element_type=F32)
              + lax.dot_general(w_hi, h_lo, dims, preferred_element_type=F32)
              + lax.dot_general(w_lo, h_hi, dims, preferred_element_type=F32)) + b_ref[...]
    gl = [logits[g:g + 1, :] for g in range(N_GROUPS)]
    gmax = functools.reduce(jnp.maximum, gl)
    gid = jnp.full(gmax.shape, N_GROUPS - 1, jnp.int32)
    for g in reversed(range(N_GROUPS - 1)):
        gid = jnp.where(gl[g] == gmax, g, gid)
    p_grp = 1.0 / functools.reduce(jnp.add, [jnp.exp(v - gmax) for v in gl])
    el = []
    for e in range(EXPERTS_PER_GROUP):
        v = logits[8 + e:9 + e, :]
        for g in range(1, N_GROUPS):
            row = 8 + g * EXPERTS_PER_GROUP + e
            v = jnp.where(gid == g, logits[row:row + 1, :], v)
        el.append(v)
    top1 = functools.reduce(jnp.maximum, el)
    idx1 = jnp.full(top1.shape, EXPERTS_PER_GROUP - 1, jnp.int32)
    for e in reversed(range(EXPERTS_PER_GROUP - 1)):
        idx1 = jnp.where(el[e] == top1, e, idx1)
    rest = [jnp.where(idx1 == e, -jnp.inf, el[e]) for e in range(EXPERTS_PER_GROUP)]
    top2 = functools.reduce(jnp.maximum, rest)
    idx2 = jnp.full(top1.shape, EXPERTS_PER_GROUP - 1, jnp.int32)
    for e in reversed(range(EXPERTS_PER_GROUP - 1)):
        idx2 = jnp.where(rest[e] == top2, e, idx2)
    e2 = jnp.exp(top2 - top1)
    w1 = p_grp / (1.0 + e2)
    w2 = p_grp * e2 / (1.0 + e2)
    for e in range(EXPERTS_PER_GROUP):
        cpad_ref[e:e + 1, :] = jnp.where(idx1 == e, w1, 0.0) + jnp.where(idx2 == e, w2, 0.0)
    hx_ref[:, D_MODEL:] = cpad_ref[...].T
    lo = jnp.minimum(idx1, idx2)
    hi = jnp.maximum(idx1, idx2)
    pair = jnp.where(lo == 0, 0, jnp.where(lo == 1, 3, 5)) + (hi - lo - 1)
    bucket = gid * N_PAIRS + pair
    bucket_ref[...] = bucket
    sub = lax.broadcasted_iota(jnp.int32, (CNT_ROWS, RT_TM), 0)
    onehot = jnp.where(sub == bucket, 1.0, 0.0)
    seen = jnp.dot(onehot.astype(BF16), tri_ref[...], preferred_element_type=F32)
    cnt = cnt_ref[...]
    rank_ref[...] = jnp.sum(onehot * (seen - 1.0 + cnt[:, 0:1]), axis=0, keepdims=True).astype(jnp.int32)
    cnt = cnt + jnp.sum(onehot, axis=1, keepdims=True)
    cnt_ref[...] = cnt
    cnt_out_ref[...] = cnt.astype(jnp.int32)


def _router_weights(w_grp, b_grp, w_exp, b_exp):
    pad = lambda v, n: jnp.zeros(v.shape[:1] + (n,) + v.shape[2:], F32)
    w_grp, w_exp = jnp.swapaxes(w_grp, 1, 2), jnp.swapaxes(w_exp, 1, 2)
    w = jnp.concatenate([w_grp, pad(w_grp, 8 - N_GROUPS), w_exp, pad(w_exp, RT_ROWS - 8 - N_EXPERTS)], axis=1)
    b_grp, b_exp = b_grp[:, :, None], b_exp[:, :, None]
    b = jnp.concatenate([b_grp, pad(b_grp, 8 - N_GROUPS), b_exp, pad(b_exp, RT_ROWS - 8 - N_EXPERTS)], axis=1)
    return w, b


DISP_TM = 512


ROW_GROUP = 64


def _for_each_row(n_rows, fn):
    def group(k, carry):
        g0 = pl.multiple_of(k * ROW_GROUP, ROW_GROUP)
        for u in range(ROW_GROUP):
            fn(g0, u)
        return carry

    lax.fori_loop(0, n_rows // ROW_GROUP, group, 0)


def _dispatch_kernel(pos_ref, start_ref, cnt_ref, nt_ref, hx_hbm, hs_ref, buf_ref, zero_ref, blk_sem, row_sem, zsem):
    i = pl.program_id(0)
    last = pl.num_programs(0) - 1
    slot = i % 2
    base = i * DISP_TM

    def tile_in(tile, s):
        return pltpu.make_async_copy(hx_hbm.at[pl.ds(pl.multiple_of(tile * DISP_TM, DISP_TM), DISP_TM), :],
                                     buf_ref.at[s], blk_sem.at[s])

    def wait_rows(s):
        pltpu.make_async_copy(buf_ref.at[s], hs_ref.at[pl.ds(0, DISP_TM), :], row_sem.at[s]).wait()

    @pl.when(i == 0)
    def _():
        zero_ref[...] = jnp.zeros_like(zero_ref)

        def tile_copy(t):
            return pltpu.make_async_copy(zero_ref, hs_ref.at[pl.ds(pl.multiple_of(t * EXP_TM, EXP_TM), EXP_TM), :],
                                         zsem)

        def last_tile(g):
            return (start_ref[g] + cnt_ref[g, 0] - 1) // EXP_TM

        def each_tile(fn):
            for g in range(N_BUCKETS):
                pl.when(cnt_ref[g, 0] > 0)(functools.partial(fn, last_tile(g)))
            lax.fori_loop(nt_ref[0], EXP_TILES, lambda t, carry: (fn(t), carry)[1], 0)

        each_tile(lambda t: tile_copy(t).start())
        each_tile(lambda t: tile_copy(t).wait())
        tile_in(0, 0).start()

    @pl.when(i > 0)
    def _():
        wait_rows(1 - slot)

    @pl.when(i < last)
    def _():
        tile_in(i + 1, 1 - slot).start()

    tile_in(i, slot).wait()

    def issue(g0, u):
        src = buf_ref.at[slot, pl.ds(g0, ROW_GROUP), :]
        pltpu.make_async_copy(src.at[pl.ds(u, 1), :], hs_ref.at[pl.ds(pos_ref[base + g0 + u], 1), :],
                              row_sem.at[slot]).start(priority=u % 2)

    _for_each_row(DISP_TM, issue)

    @pl.when(i == last)
    def _():
        wait_rows(slot)


def dispatch(pos, starts, counts, n_tiles, hx):
    return pl.pallas_call(
        _dispatch_kernel,
        grid_spec=pltpu.PrefetchScalarGridSpec(
            num_scalar_prefetch=4,
            grid=(N_TOK // DISP_TM,),
            in_specs=[pl.BlockSpec(memory_space=pl.ANY)],
            out_specs=pl.BlockSpec(memory_space=pl.ANY),
            scratch_shapes=[pltpu.VMEM((2, DISP_TM, HX_W), F32), pltpu.VMEM((EXP_TM, HX_W), F32),
                            pltpu.SemaphoreType.DMA((2,)), pltpu.SemaphoreType.DMA((2,)),
                            pltpu.SemaphoreType.DMA(())],
        ),
        out_shape=jax.ShapeDtypeStruct((HS_ROWS, HX_W), F32),
        compiler_params=_params(1),
        name="dispatch",
    )(pos, starts, counts, n_tiles, hx)


(T_GROUP, T_LO, T_HI, T_SLOT, T_RUN_POS, T_NEXT_GROUP, T_FIRST, T_PREFETCHED,
 T_PENDING_GROUP, T_PENDING_EXPERT, T_PENDING_SLOT) = range(11)


def _experts_kernel(layer, tab_ref, nt_ref, hs_ref, w1_hbm, w3_hbm, w2_hbm, ys_ref,
                    w1_ref, w3_ref, w2_ref, st1_ref, st3_ref, st2_ref, sem_ref):
    t = pl.program_id(0)
    valid = t < nt_ref[0]
    group = tab_ref[T_GROUP, t]
    slot = tab_ref[T_SLOT, t]
    run_pos = tab_ref[T_RUN_POS, t]
    next_group = tab_ref[T_NEXT_GROUP, t]
    pending_expert = tab_ref[T_PENDING_EXPERT, t]

    def expert_copies(grp, e):
        idx = grp * EXPERTS_PER_GROUP + e
        return (pltpu.make_async_copy(w1_hbm.at[layer, idx], st1_ref, sem_ref.at[0]),
                pltpu.make_async_copy(w3_hbm.at[layer, idx], st3_ref, sem_ref.at[1]),
                pltpu.make_async_copy(w2_hbm.at[layer, idx], st2_ref, sem_ref.at[2]))

    def finish(copies, dst_slot, e):
        for cp in copies:
            cp.wait()
        w1_ref[dst_slot, e] = st1_ref[...].astype(BF16)
        w3_ref[dst_slot, e] = st3_ref[...].astype(BF16)
        w2_ref[dst_slot, e] = st2_ref[...].astype(BF16)

    @pl.when(jnp.logical_and(valid, pending_expert >= 0))
    def _():
        finish(expert_copies(tab_ref[T_PENDING_GROUP, t], pending_expert), tab_ref[T_PENDING_SLOT, t], pending_expert)

    @pl.when(jnp.logical_and(valid, tab_ref[T_FIRST, t] == 1))
    def _():
        def load(e, carry):
            copies = expert_copies(group, e)
            for cp in copies:
                cp.start()
            finish(copies, slot, e)
            return carry

        lax.fori_loop(tab_ref[T_PREFETCHED, t], EXPERTS_PER_GROUP, load, 0)

    prefetch = jnp.logical_and(valid, jnp.logical_and(next_group >= 0, run_pos < EXPERTS_PER_GROUP))

    @pl.when(prefetch)
    def _():
        for cp in expert_copies(next_group, run_pos):
            cp.start()

    @pl.when(valid)
    def _():
        h = hs_ref[:, 0:D_MODEL].astype(BF16)
        c = hs_ref[:, D_MODEL:]
        lane = lax.broadcasted_iota(jnp.int32, c.shape, 1)
        acc = None
        for e in (tab_ref[T_LO, t], tab_ref[T_HI, t]):
            ce = jnp.sum(jnp.where(lane == e, c, 0.0), axis=1, keepdims=True)
            a = jnp.dot(h, w1_ref[slot, e], preferred_element_type=F32)
            b = jnp.dot(h, w3_ref[slot, e], preferred_element_type=F32)
            hid = (a * _sigmoid(a)) * b * ce
            y = jnp.dot(hid.astype(BF16), w2_ref[slot, e], preferred_element_type=F32)
            acc = y if acc is None else acc + y
        ys_ref[...] = acc

    @pl.when(jnp.logical_not(valid))
    def _():
        ys_ref[...] = jnp.zeros_like(ys_ref)


def experts(tile_table, n_tiles, hs, w1, w3, w2, layer):
    hbm = pl.BlockSpec(memory_space=pl.ANY)
    return pl.pallas_call(
        functools.partial(_experts_kernel, layer),
        grid_spec=pltpu.PrefetchScalarGridSpec(
            num_scalar_prefetch=2,
            grid=(EXP_TILES,),
            in_specs=[pl.BlockSpec((EXP_TM, HX_W), lambda t, *_: (t, 0)), hbm, hbm, hbm],
            out_specs=pl.BlockSpec((EXP_TM, D_MODEL), lambda t, *_: (t, 0)),
            scratch_shapes=[
                pltpu.VMEM((2, EXPERTS_PER_GROUP, D_MODEL, D_EXPERT), BF16),
                pltpu.VMEM((2, EXPERTS_PER_GROUP, D_MODEL, D_EXPERT), BF16),
                pltpu.VMEM((2, EXPERTS_PER_GROUP, D_EXPERT, D_MODEL), BF16),
                pltpu.VMEM((D_MODEL, D_EXPERT), F32),
                pltpu.VMEM((D_MODEL, D_EXPERT), F32),
                pltpu.VMEM((D_EXPERT, D_MODEL), F32),
                pltpu.SemaphoreType.DMA((3,)),
            ],
        ),
        out_shape=jax.ShapeDtypeStruct((HS_ROWS, D_MODEL), F32),
        compiler_params=_params(1),
        name="experts",
    )(tile_table, n_tiles, hs, w1, w3, w2)


N_TABLE_ROWS = T_PENDING_SLOT + 1
EXP_TM_LOG2 = EXP_TM.bit_length() - 1


def _schedule_kernel(cnt_ref, starts_ref, tab_ref, nt_ref):
    i32 = jnp.int32
    t = i32(0)
    for b in range(N_BUCKETS):
        n = lax.shift_right_logical(cnt_ref[b, 0] + (EXP_TM - 1), EXP_TM_LOG2)
        starts_ref[b] = t * EXP_TM

        def fill(j, carry, t=t, group=b // N_PAIRS, pair=EXPERT_PAIRS[b % N_PAIRS]):
            tab_ref[T_GROUP, t + j] = group
            tab_ref[T_LO, t + j] = pair[0]
            tab_ref[T_HI, t + j] = pair[1]
            return carry

        lax.fori_loop(0, n, fill, 0)
        t = t + n
    n_tiles = t
    nt_ref[0] = n_tiles

    def unused(k, carry):
        for row, value in ((T_GROUP, N_GROUPS - 1), (T_LO, 0), (T_HI, 1), (T_SLOT, 0), (T_RUN_POS, 0),
                           (T_NEXT_GROUP, -1), (T_FIRST, 0), (T_PREFETCHED, 0), (T_PENDING_GROUP, 0),
                           (T_PENDING_EXPERT, -1), (T_PENDING_SLOT, 0)):
            tab_ref[row, k] = value
        return carry

    lax.fori_loop(n_tiles, EXP_TILES, unused, 0)

    def back(k, carry):
        next_group, group_after = carry
        tile = n_tiles - 1 - k
        group = tab_ref[T_GROUP, tile]
        next_group = jnp.where(k == 0, -1, jnp.where(group != group_after, group_after, next_group))
        tab_ref[T_NEXT_GROUP, tile] = next_group
        return next_group, group

    lax.fori_loop(0, n_tiles, back, (i32(-1), i32(-1)))

    def forward(tile, carry):
        prev_group, run, run_start, prev_len, prev_fetch, prev_pos, prev_next, prev_slot = carry
        group = tab_ref[T_GROUP, tile]
        first = jnp.logical_or(tile == 0, group != prev_group)
        prev_len = jnp.where(first, tile - run_start, prev_len)
        run = jnp.where(first, run + 1, run)
        run_start = jnp.where(first, tile, run_start)
        slot = lax.rem(run, 2)
        run_pos = tile - run_start
        next_group = tab_ref[T_NEXT_GROUP, tile]
        tab_ref[T_FIRST, tile] = first.astype(i32)
        tab_ref[T_SLOT, tile] = slot
        tab_ref[T_RUN_POS, tile] = run_pos
        tab_ref[T_PREFETCHED, tile] = jnp.minimum(prev_len, EXPERTS_PER_GROUP)
        tab_ref[T_PENDING_EXPERT, tile] = jnp.where(prev_fetch == 1, prev_pos, -1)
        tab_ref[T_PENDING_GROUP, tile] = prev_next
        tab_ref[T_PENDING_SLOT, tile] = 1 - prev_slot
        fetch = jnp.logical_and(next_group >= 0, run_pos < EXPERTS_PER_GROUP).astype(i32)
        return group, run, run_start, prev_len, fetch, run_pos, next_group, slot

    lax.fori_loop(0, n_tiles, forward, (i32(-1), i32(-1), i32(0), i32(0), i32(0), i32(0), i32(0), i32(0)))


def expert_schedule(counts):
    smem = pl.BlockSpec(memory_space=pltpu.SMEM)
    return pl.pallas_call(
        _schedule_kernel,
        in_specs=[smem],
        out_specs=[smem, smem, smem],
        out_shape=[jax.ShapeDtypeStruct((N_BUCKETS,), jnp.int32),
                   jax.ShapeDtypeStruct((N_TABLE_ROWS, EXP_TILES), jnp.int32),
                   jax.ShapeDtypeStruct((1,), jnp.int32)],
        name="expert_schedule",
    )(counts)


RES_TM = 512
RES_CTX_TILES = N_CTX // RES_TM


def _moe_residual_kernel(final, pos_ref, x_ref, mod_ref, gain_ref, ys_ref, *refs):
    out_refs, (ybuf_ref, sem_ref) = refs[:-2], refs[-2:]
    i = pl.program_id(0)
    slot = i % 2

    def gather_tile(tile, tile_slot):
        def issue(g0, u):
            dst = ybuf_ref.at[tile_slot, pl.ds(g0, ROW_GROUP), :]
            pltpu.make_async_copy(ys_ref.at[pl.ds(pos_ref[tile * RES_TM + g0 + u], 1), :], dst.at[pl.ds(u, 1), :],
                                  sem_ref.at[tile_slot]).start(priority=u % 2)

        _for_each_row(RES_TM, issue)

    @pl.when(i == 0)
    def _():
        gather_tile(0, 0)

    @pl.when(i + 1 < pl.num_programs(0))
    def _():
        gather_tile(i + 1, 1 - slot)

    pltpu.make_async_copy(ys_ref.at[pl.ds(0, RES_TM), :], ybuf_ref.at[slot], sem_ref.at[slot]).wait()
    x = x_ref[...] + mod_ref[:, 5 * D_MODEL:6 * D_MODEL] * ybuf_ref[slot]
    if not final:
        out_refs[0][...] = x
        return
    y = _rms(x, gain_ref[...])

    @pl.when(i < RES_CTX_TILES)
    def _():
        out_refs[0][...] = y

    @pl.when(i >= RES_CTX_TILES)
    def _():
        out_refs[1][...] = y


def moe_residual(pos, x, mod4, gain, ys, layer, final):
    if final:
        out_specs = _split_specs((RES_TM, D_MODEL), RES_CTX_TILES)
        out_shape = [jax.ShapeDtypeStruct((N_CTX, D_MODEL), F32), jax.ShapeDtypeStruct((N_LAT, D_MODEL), F32)]
    else:
        out_specs = pl.BlockSpec((RES_TM, D_MODEL), lambda i, *_: (i, 0))
        out_shape = jax.ShapeDtypeStruct((N_TOK, D_MODEL), F32)
    return pl.pallas_call(
        functools.partial(_moe_residual_kernel, final),
        grid_spec=pltpu.PrefetchScalarGridSpec(
            num_scalar_prefetch=1,
            grid=(N_TOK // RES_TM,),
            in_specs=[
                pl.BlockSpec((RES_TM, D_MODEL), lambda i, *_: (i, 0)),
                _mod_spec(layer, RES_TM),
                pl.BlockSpec((1, D_MODEL), lambda i, *_: (0, 0)),
                pl.BlockSpec(memory_space=pl.ANY),
            ],
            out_specs=out_specs,
            scratch_shapes=[pltpu.VMEM((2, RES_TM, D_MODEL), F32), pltpu.SemaphoreType.DMA((2,))],
        ),
        out_shape=out_shape,
        compiler_params=_params(1),
        name="moe_residual_final" if final else "moe_residual",
    )(pos, x, mod4, gain, ys)


def kernel(x_prompt, x_sample, cache_k, cache_v, state_lru, c, c_ctx, w_mod, b_mod, norm_mix, norm_ffn, w_in, sgu_norm, sgu_w, sgu_b, lru_conv_w, lru_conv_b, lru_w_r, lru_b_r, lru_w_i, lru_b_i, lru_lambda, na_rpb, w_branch_sgu, w_branch_lru, w_branch_na, w_out, moe_w_group, moe_b_group, moe_w_expert, moe_b_expert, moe_w1, moe_w3, moe_w2, final_norm_gain):
    xs = (x_prompt.reshape(N_CTX, D_MODEL), x_sample.reshape(N_LAT, D_MODEL))
    cond = jnp.zeros((N_COND, D_MODEL), F32).at[0].set(c_ctx).at[1:1 + DEC_BATCH].set(c)
    mod = modulation(cond, w_mod, b_mod).reshape(DEPTH, N_COND, 1, MOD_WIDTH)
    zero_state = jnp.zeros((BATCH, 2, LRU_WIDTH), F32)
    cache_k = cache_k.reshape(DEC_BATCH, DEPTH, PAST_LEN, NA_WIDTH)
    cache_v = cache_v.reshape(DEC_BATCH, DEPTH, PAST_LEN, NA_WIDTH)
    final_gain = final_norm_gain.reshape(1, D_MODEL)
    norm_mix = norm_mix.reshape(DEPTH, 1, D_MODEL)
    norm_ffn = norm_ffn.reshape(DEPTH, 1, D_MODEL)
    sgu_norm = sgu_norm.reshape(DEPTH, 1, SGU_WIDTH)
    sgu_b_t = jnp.swapaxes(sgu_b, 1, 2)
    lru_conv_b = lru_conv_b.reshape(DEPTH, 1, LRU_WIDTH)
    w_gate = _block_diag_gate_weights(lru_w_r, lru_w_i)
    rpb = na_rpb.reshape(DEPTH, NA_HEADS * N_DROW * N_DCOL)
    w_rt, b_rt = _router_weights(moe_w_group, moe_b_group, moe_w_expert, moe_b_expert)
    caches, ss = [], []
    for l in range(DEPTH):
        ya, za, q, kb, vb, k_ctx, v_ctx, gates = in_projection(
            xs, mod, norm_mix, w_in, sgu_norm, sgu_w, sgu_b_t, l, prev_caches=caches if l == DEPTH - 1 else ())
        caches.append((k_ctx, v_ctx))
        lru_args = (lru_conv_w, lru_conv_b, w_gate, lru_b_r, lru_b_i, lru_lambda)
        yb_ctx, st_ctx = rglru(za, *lru_args, zero_state, pl.BlockSpec((None, 2, LRU_WIDTH), lambda i: (i, 0, 0)),
                               SEQ, BATCH, 0, l)
        yb_lat, _ = rglru(za, *lru_args, state_lru,
                          pl.BlockSpec((None, None, 2, LRU_WIDTH), lambda i, l=l: (i, l, 0, 0)),
                          DEC_SEQ, DEC_BATCH, N_CTX, l)
        yc_ctx = context_attention(q, kb, vb)
        yc_lat = latent_attention(q, kb, vb, cache_k, cache_v, rpb, l)
        x, hx, bucket, rank, counts = merge_branches(
            xs, mod, ya, yb_ctx, yb_lat, yc_ctx, yc_lat, gates, w_branch_sgu, w_branch_lru, w_branch_na, w_out,
            norm_ffn, w_rt, b_rt, l)
        bucket, rank = bucket.reshape(N_TOK), rank.reshape(N_TOK)
        starts, tile_table, n_tiles = expert_schedule(counts)
        pos = rank + jnp.sum(jnp.where(bucket[:, None] == jnp.arange(N_BUCKETS), starts[None, :], 0), axis=1)
        hs = dispatch(pos, starts, counts, n_tiles, hx)
        ys = experts(tile_table, n_tiles, hs, moe_w1, moe_w3, moe_w2, l)
        if l < DEPTH - 1:
            xs = (moe_residual(pos, x, mod, final_gain, ys, l, False),)
        else:
            y_ctx, y_lat = moe_residual(pos, x, mod, final_gain, ys, l, True)
        ss.append(st_ctx)
    cache_shape = (BATCH, DEPTH, SEQ, NA_HEADS, NA_HEAD_DIM)
    return (y_ctx.reshape(BATCH, SEQ, D_MODEL), y_lat.reshape(DEC_BATCH, DEC_SEQ, D_MODEL),
            k_ctx.reshape(cache_shape), v_ctx.reshape(cache_shape), jnp.stack(ss, axis=1))
```

```python
import functools

import jax
import jax.numpy as jnp
import numpy as np
from jax import lax
from jax.experimental import pallas as pl
from jax.experimental.pallas import tpu as pltpu

F32 = jnp.float32
BF16 = jnp.bfloat16

D_MODEL = 1024
BATCH = 16
SEQ = 256
DEPTH = 2
DEC_BATCH = 4
DEC_SEQ = 2048
PAST_LEN = 512
GRID_W = 64
CHUNK = 128
SGU_WIDTH = 512
SGU_GROUPS = 4
LRU_WIDTH = 512
LRU_BLOCKS = 8
CONV_WIDTH = 4
LRU_C = 8.0
NA_HEADS = 8
NA_HEAD_DIM = 64
NA_WIDTH = 512
WIN_ROWS = 8
WIN_COLS = 16
N_GROUPS = 4
EXPERTS_PER_GROUP = 4
N_EXPERTS = 16
D_EXPERT = 512
IN_WIDTH = 6656
EPS = 1e-6
NEG_INF = -1e30

N_CTX = BATCH * SEQ
N_LAT = DEC_BATCH * DEC_SEQ
N_TOK = N_CTX + N_LAT
N_COND = 8
MOD_WIDTH = 6 * D_MODEL
GRID_ROWS = DEC_SEQ // GRID_W

VMEM_LIMIT_BYTES = 56 * 1024 * 1024


def _params(n_axes):
    return pltpu.CompilerParams(dimension_semantics=("arbitrary",) * n_axes,
                                vmem_limit_bytes=VMEM_LIMIT_BYTES)


def _cond_row(tile, tile_rows):
    tok = tile * tile_rows
    return jnp.where(tok < N_CTX, 0, 1 + (tok - N_CTX) // DEC_SEQ)


def _layer_spec(shape, layer):
    return pl.BlockSpec((None,) + tuple(shape), lambda i, *_: (layer,) + (0,) * len(shape))


def _mod_spec(layer, tile_rows):
    return pl.BlockSpec((None, None, 1, MOD_WIDTH), lambda i, *_: (layer, _cond_row(i, tile_rows), 0, 0))


def _rms(x, gain):
    return x * lax.rsqrt(jnp.mean(x * x, axis=-1, keepdims=True) + EPS) * gain


def _bdot(a, b):
    return jnp.dot(a.astype(BF16), b.astype(BF16), preferred_element_type=F32)


def _bdot_t(a, b):
    return lax.dot_general(a.astype(BF16), b.astype(BF16), (((1,), (1,)), ((), ())),
                           preferred_element_type=F32)


MOD_TN = 1536


def _mod_kernel(cond_ref, w_ref, b_ref, o_ref):
    c = cond_ref[...]
    s = c * jax.nn.sigmoid(c)
    mod = _bdot(s, w_ref[...]) + b_ref[...]
    for r in range(N_COND):
        o_ref[r] = mod[r:r + 1, :]


def modulation(cond, w_mod, b_mod):
    return pl.pallas_call(
        _mod_kernel,
        grid=(DEPTH, MOD_WIDTH // MOD_TN),
        in_specs=[
            pl.BlockSpec((N_COND, D_MODEL), lambda l, j: (0, 0)),
            pl.BlockSpec((None, D_MODEL, MOD_TN), lambda l, j: (l, 0, j)),
            pl.BlockSpec((None, 1, MOD_TN), lambda l, j: (l, 0, j)),
        ],
        out_specs=pl.BlockSpec((None, N_COND, 1, MOD_TN), lambda l, j: (l, 0, 0, j)),
        out_shape=jax.ShapeDtypeStruct((DEPTH, N_COND, 1, MOD_WIDTH), F32),
        compiler_params=_params(2),
        name="modulation",
    )(cond, w_mod, b_mod.reshape(DEPTH, 1, MOD_WIDTH))


IN_TM = 256
IN_CTX_TILES = N_CTX // IN_TM
ZA_WIDTH = 4 * 512
KV_OFF = ZA_WIDTH + NA_WIDTH
GATE_OFF = KV_OFF + 2 * NA_WIDTH
W_CHUNK = 512
Q_SCALE = NA_HEAD_DIM ** -0.5 * float(np.log2(np.e))


def _load_cast(chunks, stage_ref, sem_ref):
    def copy(j):
        return pltpu.make_async_copy(chunks[j][0], stage_ref.at[j % 2], sem_ref.at[j % 2])

    copy(0).start()
    for j in range(len(chunks)):
        if j + 1 < len(chunks):
            copy(j + 1).start()
        copy(j).wait()
        chunks[j][1](stage_ref[j % 2].astype(BF16))


def _token_tile(i, n_ctx_tiles, refs):
    if len(refs) == 1:
        return refs[0][...]
    return jnp.where(i < n_ctx_tiles, refs[0][...], refs[1][...])


def _split_specs(block, n_ctx_tiles):
    return [pl.BlockSpec(block, lambda i, *_: (jnp.minimum(i, n_ctx_tiles - 1), 0)),
            pl.BlockSpec(block, lambda i, *_: (jnp.maximum(i - n_ctx_tiles, 0), 0))]


SGU_GD = SGU_WIDTH // SGU_GROUPS


def _spatial_gating(u, v, gain, ws_ref, bs_ref):
    u = jax.nn.gelu(u)
    v = _rms(jax.nn.gelu(v), gain).astype(BF16)
    out = []
    for g in range(SGU_GROUPS):
        cols = slice(g * SGU_GD, (g + 1) * SGU_GD)
        mixed = jnp.dot(ws_ref[g].astype(BF16), v[:, cols], preferred_element_type=F32) + bs_ref[:, g:g + 1]
        out.append(u[:, cols] * mixed)
    return jnp.concatenate(out, axis=1)


def _inproj_kernel(layer, n_x, n_prev, *refs):
    x_refs, (mod_ref, gain_ref, w_hbm, sgu_gain_ref, ws_ref, bs_ref) = refs[:n_x], refs[n_x:n_x + 6]
    prev_refs, rest = refs[n_x + 6:n_x + 6 + 2 * n_prev], refs[n_x + 6 + 2 * n_prev:]
    ya_ref, zb_ref, q_ref, kb_ref, vb_ref, kc_ref, vc_ref, g_ref, w_ref, stage_ref, sem_ref = rest
    i = pl.program_id(0)

    @pl.when(i == 0)
    def _():
        def store(c):
            def st(v):
                w_ref[:, c * W_CHUNK:(c + 1) * W_CHUNK] = v
            return st

        _load_cast([(w_hbm.at[layer, :, pl.ds(c * W_CHUNK, W_CHUNK)], store(c)) for c in range(IN_WIDTH // W_CHUNK)],
                   stage_ref, sem_ref)

    m = mod_ref[...]
    shift, scale = m[:, 0:D_MODEL], m[:, D_MODEL:2 * D_MODEL]
    h = (_rms(_token_tile(i, IN_CTX_TILES, x_refs), gain_ref[...]) * (1.0 + scale) + shift).astype(BF16)
    za = jnp.dot(h, w_ref[:, 0:ZA_WIDTH], preferred_element_type=F32)
    for c in range(IN_TM // CHUNK):
        rows = slice(c * CHUNK, (c + 1) * CHUNK)
        ya_ref[rows, :] = _spatial_gating(za[rows, 0:SGU_WIDTH], za[rows, SGU_WIDTH:2 * SGU_WIDTH],
                                          sgu_gain_ref[...], ws_ref, bs_ref).astype(BF16)
    zb_ref[...] = za[:, 2 * SGU_WIDTH:].astype(BF16)
    q_ref[...] = (jnp.dot(h, w_ref[:, ZA_WIDTH:KV_OFF], preferred_element_type=F32) * Q_SCALE).astype(BF16)
    kv = jnp.dot(h, w_ref[:, KV_OFF:GATE_OFF], preferred_element_type=F32)
    kb_ref[...] = kv[:, 0:NA_WIDTH].astype(BF16)
    vb_ref[...] = kv[:, NA_WIDTH:].astype(BF16)
    g_ref[...] = jnp.dot(h, w_ref[:, GATE_OFF:], preferred_element_type=F32).astype(BF16)

    @pl.when(i < IN_CTX_TILES)
    def _():
        if n_prev == 0:
            kc_ref[...] = kv[:, 0:NA_WIDTH]
            vc_ref[...] = kv[:, NA_WIDTH:]
        else:
            for p in range(n_prev):
                kc_ref[p] = prev_refs[2 * p][...]
                vc_ref[p] = prev_refs[2 * p + 1][...]
            kc_ref[n_prev] = kv[:, 0:NA_WIDTH]
            vc_ref[n_prev] = kv[:, NA_WIDTH:]


def in_projection(xs, mod3, gain, w_in, sgu_gain, sgu_w, sgu_b_t, layer, prev_caches=()):
    assert IN_TM == SEQ
    row = lambda i: (i, 0)
    ctx_row = lambda i: (jnp.minimum(i, IN_CTX_TILES - 1), 0)
    x_specs = ([pl.BlockSpec((IN_TM, D_MODEL), row)] if len(xs) == 1
               else _split_specs((IN_TM, D_MODEL), IN_CTX_TILES))
    n_prev = len(prev_caches)
    if n_prev:
        cache_spec = pl.BlockSpec((None, n_prev + 1, SEQ, NA_WIDTH),
                                  lambda i: (jnp.minimum(i, IN_CTX_TILES - 1), 0, 0, 0))
        cache_shape = jax.ShapeDtypeStruct((BATCH, n_prev + 1, SEQ, NA_WIDTH), F32)
    else:
        cache_spec = pl.BlockSpec((IN_TM, NA_WIDTH), ctx_row)
        cache_shape = jax.ShapeDtypeStruct((N_CTX, NA_WIDTH), F32)
    return pl.pallas_call(
        functools.partial(_inproj_kernel, layer, len(xs), n_prev),
        grid=(N_TOK // IN_TM,),
        in_specs=x_specs + [
            _mod_spec(layer, IN_TM),
            _layer_spec((1, D_MODEL), layer),
            pl.BlockSpec(memory_space=pl.ANY),
            _layer_spec((1, SGU_WIDTH), layer),
            _layer_spec((SGU_GROUPS, CHUNK, CHUNK), layer),
            _layer_spec((CHUNK, SGU_GROUPS), layer),
        ] + [pl.BlockSpec((IN_TM, NA_WIDTH), ctx_row)] * (2 * n_prev),
        out_specs=[
            pl.BlockSpec((IN_TM, SGU_WIDTH), row),
            pl.BlockSpec((IN_TM, 2 * LRU_WIDTH), row),
            pl.BlockSpec((IN_TM, NA_WIDTH), row),
            pl.BlockSpec((IN_TM, NA_WIDTH), row),
            pl.BlockSpec((IN_TM, NA_WIDTH), row),
            cache_spec,
            cache_spec,
            pl.BlockSpec((IN_TM, 3 * D_MODEL), row),
        ],
        out_shape=[
            jax.ShapeDtypeStruct((N_TOK, SGU_WIDTH), BF16),
            jax.ShapeDtypeStruct((N_TOK, 2 * LRU_WIDTH), BF16),
            jax.ShapeDtypeStruct((N_TOK, NA_WIDTH), BF16),
            jax.ShapeDtypeStruct((N_TOK, NA_WIDTH), BF16),
            jax.ShapeDtypeStruct((N_TOK, NA_WIDTH), BF16),
            cache_shape,
            cache_shape,
            jax.ShapeDtypeStruct((N_TOK, 3 * D_MODEL), BF16),
        ],
        scratch_shapes=[pltpu.VMEM((D_MODEL, IN_WIDTH), BF16), pltpu.VMEM((2, D_MODEL, W_CHUNK), F32),
                        pltpu.SemaphoreType.DMA((2,))],
        compiler_params=_params(1),
        name="in_projection",
    )(*xs, mod3, gain, w_in, sgu_gain, sgu_w, sgu_b_t, *[c for kv in prev_caches for c in kv])


LRU_TC = 256
LRU_HALF = 256
SUB = 8
HALO = 8
LANES = 128
LRU_SLABS = LRU_WIDTH // LANES


def _sigmoid(x):
    return 0.5 * jnp.tanh(0.5 * x) + 0.5


def _lru_kernel(seq_len, xr_ref, gr_ref, cw_ref, cb_ref, wlo_ref, whi_ref, br_ref, bi_ref,
                lam_ref, h0_ref, y_ref, st_ref, xp_ref, a_ref, b_ref, h_ref):
    seg_len = seq_len // SUB
    pitch = seg_len + SUB
    segs_per_chunk = max(LRU_TC // seg_len, 1)
    n_chunks = seq_len // LRU_TC
    zeros = jnp.zeros((HALO, LRU_WIDTH), F32)
    xp_ref[0:HALO, :] = zeros
    xp_ref[seq_len + HALO:seq_len + 2 * HALO, :] = zeros

    def copy_in(c, carry):
        r0 = pl.multiple_of(c * LRU_TC, LRU_TC)
        xp_ref[pl.ds(r0 + HALO, LRU_TC), :] = xr_ref[pl.ds(r0, LRU_TC), :].astype(F32)
        return carry

    lax.fori_loop(0, n_chunks, copy_in, 0)

    def chunk_rows(c, seg):
        n = min(seg_len, LRU_TC)
        start = pl.multiple_of((c * segs_per_chunk + seg) * pitch, SUB)
        return pl.ds(start, n), slice(seg * n, (seg + 1) * n)

    cw = cw_ref[...]
    cb = cb_ref[...]
    win = LRU_TC + 2 * HALO
    neg_lam = -lam_ref[...]
    softplus = jnp.maximum(neg_lam, 0.0) + jnp.log1p(jnp.exp(-jnp.abs(neg_lam)))
    decay = (0.5 * LRU_C) * softplus
    half_br = 0.5 * br_ref[...]
    half_bi = 0.5 * bi_ref[...]

    def gates(c, carry):
        r0 = pl.multiple_of(c * LRU_TC, LRU_TC)
        w = xp_ref[pl.ds(r0, win), :]
        xc = (cw[0:1, :] * pltpu.roll(w, 1, 0)[HALO:HALO + LRU_TC]
              + cw[1:2, :] * w[HALO:HALO + LRU_TC]
              + cw[2:3, :] * pltpu.roll(w, win - 1, 0)[HALO:HALO + LRU_TC]
              + cw[3:4, :] * pltpu.roll(w, win - 2, 0)[HALO:HALO + LRU_TC]) + cb
        xb = xc.astype(BF16)
        for half, w_ref in enumerate((wlo_ref, whi_ref)):
            cols = slice(half * LRU_HALF, (half + 1) * LRU_HALF)
            pre = jnp.dot(xb[:, cols], w_ref[...], preferred_element_type=F32)
            half_x = 0.5 * xc[:, cols]
            for d in range(2):
                tr = jnp.tanh(pre[:, (2 * d) * LRU_HALF:(2 * d + 1) * LRU_HALF] + half_br[d:d + 1, cols])
                ti = jnp.tanh(pre[:, (2 * d + 1) * LRU_HALF:(2 * d + 2) * LRU_HALF] + half_bi[d:d + 1, cols])
                neg_log_a = decay[d:d + 1, cols] * tr + decay[d:d + 1, cols]
                a = jnp.exp2(neg_log_a * -LOG2E)
                z = jnp.tanh(neg_log_a) * (a * a + 1.0)
                b = jnp.where(z > 0.0, z * lax.rsqrt(z), 0.0) * (half_x * ti + half_x)
                for k in range(LRU_HALF // LANES):
                    slab = half * (LRU_HALF // LANES) + k
                    lanes = slice(k * LANES, (k + 1) * LANES)
                    for seg in range(segs_per_chunk):
                        dst, src = chunk_rows(c, seg)
                        a_ref[d, slab, dst, :] = a[src, lanes]
                        b_ref[d, slab, dst, :] = b[src, lanes]
        return carry

    lax.fori_loop(0, n_chunks, gates, 0)

    def step_rows(d, j):
        return pl.ds(j if d == 0 else seg_len - 1 - j, SUB, stride=pitch)

    def reduce_step(j, carry):
        out = []
        for d in range(2):
            for slab in range(LRU_SLABS):
                big_a, big_b = carry[d * LRU_SLABS + slab]
                a = a_ref[d, slab, step_rows(d, j), :]
                b = b_ref[d, slab, step_rows(d, j), :]
                out.append((a * big_a, a * big_b + b))
        return tuple(out)

    ident = (jnp.ones((SUB, LANES), F32), jnp.zeros((SUB, LANES), F32))
    totals = lax.fori_loop(0, seg_len, reduce_step, (ident,) * (2 * LRU_SLABS), unroll=8)

    sub = lax.broadcasted_iota(jnp.int32, (SUB, LANES), 0)
    entering = []
    for d in range(2):
        for slab in range(LRU_SLABS):
            big_a, big_b = totals[d * LRU_SLABS + slab]
            h = h0_ref[d:d + 1, slab * LANES:(slab + 1) * LANES]
            rows = jnp.zeros((SUB, LANES), F32)
            for s in (range(SUB) if d == 0 else reversed(range(SUB))):
                rows = jnp.where(sub == s, h, rows)
                h = big_a[s:s + 1, :] * h + big_b[s:s + 1, :]
            entering.append(rows)
            st_ref[d:d + 1, slab * LANES:(slab + 1) * LANES] = h

    def apply_step(j, carry):
        out = []
        for d in range(2):
            for slab in range(LRU_SLABS):
                h = (a_ref[d, slab, step_rows(d, j), :] * carry[d * LRU_SLABS + slab]
                     + b_ref[d, slab, step_rows(d, j), :])
                h_ref[d, slab, step_rows(d, j), :] = h
                out.append(h)
        return tuple(out)

    lax.fori_loop(0, seg_len, apply_step, tuple(entering), unroll=8)

    def merge(c, carry):
        r0 = pl.multiple_of(c * LRU_TC, LRU_TC)
        gate = jax.nn.gelu(gr_ref[pl.ds(r0, LRU_TC), :].astype(F32))
        for slab in range(LRU_SLABS):
            lanes = slice(slab * LANES, (slab + 1) * LANES)
            for seg in range(segs_per_chunk):
                src, dst = chunk_rows(c, seg)
                h = h_ref[0, slab, src, :] + h_ref[1, slab, src, :]
                y_ref[pl.ds(r0 + dst.start, dst.stop - dst.start), lanes] = (h * gate[dst, lanes]).astype(BF16)
        return carry

    lax.fori_loop(0, n_chunks, merge, 0)


def rglru(za, conv_w, conv_b, w_gate, b_r, b_i, lam, h0, h0_spec, seq_len, n_seq, tok_off, layer):
    blk0 = tok_off // seq_len
    half_spec = lambda half: pl.BlockSpec((None, None, LRU_HALF, 4 * LRU_HALF), lambda i: (layer, half, 0, 0))
    return pl.pallas_call(
        functools.partial(_lru_kernel, seq_len),
        grid=(n_seq,),
        in_specs=[
            pl.BlockSpec((seq_len, LRU_WIDTH), lambda i: (i + blk0, 0)),
            pl.BlockSpec((seq_len, LRU_WIDTH), lambda i: (i + blk0, 1)),
            _layer_spec((CONV_WIDTH, LRU_WIDTH), layer),
            _layer_spec((1, LRU_WIDTH), layer),
            half_spec(0),
            half_spec(1),
            _layer_spec((2, LRU_WIDTH), layer),
            _layer_spec((2, LRU_WIDTH), layer),
            _layer_spec((2, LRU_WIDTH), layer),
            h0_spec,
        ],
        out_specs=[
            pl.BlockSpec((seq_len, LRU_WIDTH), lambda i: (i, 0)),
            pl.BlockSpec((None, 2, LRU_WIDTH), lambda i: (i, 0, 0)),
        ],
        out_shape=[
            jax.ShapeDtypeStruct((n_seq * seq_len, LRU_WIDTH), BF16),
            jax.ShapeDtypeStruct((n_seq, 2, LRU_WIDTH), F32),
        ],
        scratch_shapes=[
            pltpu.VMEM((seq_len + 2 * HALO, LRU_WIDTH), F32),
            pltpu.VMEM((2, LRU_SLABS, seq_len + SUB * SUB, LANES), F32),
            pltpu.VMEM((2, LRU_SLABS, seq_len + SUB * SUB, LANES), F32),
            pltpu.VMEM((2, LRU_SLABS, seq_len + SUB * SUB, LANES), F32),
        ],
        compiler_params=_params(1),
        name=f"rglru_{seq_len}",
    )(za, za, conv_w, conv_b, w_gate, w_gate, b_r, b_i, lam, h0)


def _block_diag_gate_weights(w_r, w_i):
    per_half = LRU_BLOCKS // 2
    blk = LRU_WIDTH // LRU_BLOCKS
    g = jnp.stack([w_r, w_i], axis=2).reshape(DEPTH, 2, 2, 2, per_half, blk, blk)
    w = jnp.einsum("ldghaij,ab->lhaidgbj", g, jnp.eye(per_half, dtype=g.dtype))
    return (0.5 * w.reshape(DEPTH, 2, LRU_HALF, 4 * LRU_HALF)).astype(BF16)


PAIR_W = 2 * NA_HEAD_DIM
LOG2E = float(np.log2(np.e))


def _head_lanes(shape, head):
    lane = lax.broadcasted_iota(jnp.int32, shape, 1)
    return (lane < NA_HEAD_DIM) if head == 0 else (lane >= NA_HEAD_DIM)


def _one_head(q_pair, head):
    return jnp.where(_head_lanes(q_pair.shape, head), q_pair.astype(F32), 0.0).astype(BF16)


def _join_heads(o0, o1):
    return jnp.where(_head_lanes(o0.shape, 0), o0, o1)


def _ctx_attn_kernel(q_ref, k_ref, v_ref, o_ref):
    for j in range(NA_HEADS // 2):
        cols = slice(j * PAIR_W, (j + 1) * PAIR_W)
        outs = []
        for head in range(2):
            s = _bdot_t(_one_head(q_ref[:, cols], head), k_ref[:, cols])
            p = jnp.exp2(s - jnp.max(s, axis=-1, keepdims=True))
            denom = jnp.sum(p, axis=-1, keepdims=True)
            outs.append(jnp.dot(p.astype(BF16), v_ref[:, cols], preferred_element_type=F32) / denom)
        o_ref[:, cols] = _join_heads(*outs).astype(BF16)


def context_attention(q, kb, vb):
    spec = pl.BlockSpec((SEQ, NA_WIDTH), lambda i: (i, 0))
    return pl.pallas_call(
        _ctx_attn_kernel,
        grid=(BATCH,),
        in_specs=[spec, spec, spec],
        out_specs=spec,
        out_shape=jax.ShapeDtypeStruct((N_CTX, NA_WIDTH), BF16),
        compiler_params=_params(1),
        name="context_attention",
    )(q, kb, vb)


NA_QROWS = 4
NA_TQ = NA_QROWS * GRID_W
NA_KROWS = NA_QROWS + WIN_ROWS
NA_TK = NA_KROWS * GRID_W
NA_QBLOCKS = GRID_ROWS // NA_QROWS


N_DROW = 2 * WIN_ROWS - 1
N_DCOL = 2 * WIN_COLS - 1
NA_BLOCK_KINDS = (0, 1, NA_QBLOCKS - 1)


def _first_key_row(qb):
    return (np.clip if isinstance(qb, int) else jnp.clip)(qb * NA_QROWS - WIN_ROWS // 2, 0, GRID_ROWS - NA_KROWS)


def _build_bias(layer, rpb_ref, table_ref, bias_ref):
    qc = lax.broadcasted_iota(jnp.int32, (GRID_W, GRID_W), 0)
    kc = lax.broadcasted_iota(jnp.int32, (GRID_W, GRID_W), 1)
    col0 = jnp.clip(qc - WIN_COLS // 2, 0, GRID_W - WIN_COLS)
    col_ok = jnp.logical_and(kc >= col0, kc < col0 + WIN_COLS)
    d_col = jnp.clip(kc - qc, 1 - WIN_COLS, WIN_COLS - 1) + WIN_COLS - 1
    neg = jnp.full((GRID_W, GRID_W), NEG_INF, F32)

    def table_entry(idx, carry):
        t = neg
        for j in range(N_DCOL):
            t = jnp.where(d_col == j, rpb_ref[layer, idx * N_DCOL + j], t)
        table_ref[idx] = jnp.where(col_ok, t * LOG2E, neg)
        return carry

    lax.fori_loop(0, NA_HEADS * N_DROW, table_entry, 0)

    def head_blocks(h, carry):
        for kind, qb in enumerate(NA_BLOCK_KINDS):
            for i in range(NA_QROWS):
                qrow = qb * NA_QROWS + i
                win0 = int(np.clip(qrow - WIN_ROWS // 2, 0, GRID_ROWS - WIN_ROWS))
                for kr in range(NA_KROWS):
                    krow = int(_first_key_row(qb)) + kr
                    inside = win0 <= krow < win0 + WIN_ROWS
                    blk = table_ref[h * N_DROW + (krow - qrow + WIN_ROWS - 1)] if inside else neg
                    bias_ref[kind, h, i * GRID_W:(i + 1) * GRID_W, kr * GRID_W:(kr + 1) * GRID_W] = blk
        return carry

    lax.fori_loop(0, NA_HEADS, head_blocks, 0)


def _lat_attn_kernel(layer, rpb_ref, q_ref, k_ref, v_ref, ck_ref, cv_ref, o_ref, table_ref, bias_ref):
    qb = pl.program_id(1)

    @pl.when(jnp.logical_and(pl.program_id(0) == 0, qb == 0))
    def _():
        _build_bias(layer, rpb_ref, table_ref, bias_ref)

    kind = jnp.where(qb == 0, 0, jnp.where(qb == NA_QBLOCKS - 1, 2, 1))
    k0 = pl.multiple_of(_first_key_row(qb) * GRID_W, GRID_W)
    for j in range(NA_HEADS // 2):
        cols = slice(j * PAIR_W, (j + 1) * PAIR_W)
        k_loc = k_ref[pl.ds(k0, NA_TK), cols]
        v_loc = v_ref[pl.ds(k0, NA_TK), cols]
        k_ctx = ck_ref[:, cols].astype(BF16)
        v_ctx = cv_ref[:, cols].astype(BF16)
        outs = []
        for head in range(2):
            qh = _one_head(q_ref[:, cols], head)
            s_loc = _bdot_t(qh, k_loc) + bias_ref[kind, 2 * j + head]
            s_ctx = _bdot_t(qh, k_ctx)
            m = jnp.maximum(jnp.max(s_loc, axis=-1, keepdims=True), jnp.max(s_ctx, axis=-1, keepdims=True))
            p_loc = jnp.exp2(s_loc - m)
            p_ctx = jnp.exp2(s_ctx - m)
            denom = jnp.sum(p_loc, axis=-1, keepdims=True) + jnp.sum(p_ctx, axis=-1, keepdims=True)
            o = (jnp.dot(p_loc.astype(BF16), v_loc, preferred_element_type=F32)
                 + jnp.dot(p_ctx.astype(BF16), v_ctx, preferred_element_type=F32))
            outs.append(o / denom)
        o_ref[:, cols] = _join_heads(*outs).astype(BF16)


def latent_attention(q, kb, vb, cache_k, cache_v, rpb, layer):
    q_blk0 = N_CTX // NA_TQ
    s_blk0 = N_CTX // DEC_SEQ
    cache_spec = pl.BlockSpec((None, None, PAST_LEN, NA_WIDTH), lambda b, m: (b, layer, 0, 0))
    return pl.pallas_call(
        functools.partial(_lat_attn_kernel, layer),
        grid=(DEC_BATCH, NA_QBLOCKS),
        in_specs=[
            pl.BlockSpec(memory_space=pltpu.SMEM),
            pl.BlockSpec((NA_TQ, NA_WIDTH), lambda b, m: (q_blk0 + b * NA_QBLOCKS + m, 0)),
            pl.BlockSpec((DEC_SEQ, NA_WIDTH), lambda b, m: (s_blk0 + b, 0)),
            pl.BlockSpec((DEC_SEQ, NA_WIDTH), lambda b, m: (s_blk0 + b, 0)),
            cache_spec,
            cache_spec,
        ],
        out_specs=pl.BlockSpec((NA_TQ, NA_WIDTH), lambda b, m: (b * NA_QBLOCKS + m, 0)),
        out_shape=jax.ShapeDtypeStruct((N_LAT, NA_WIDTH), BF16),
        scratch_shapes=[pltpu.VMEM((NA_HEADS * N_DROW, GRID_W, GRID_W), F32),
                        pltpu.VMEM((len(NA_BLOCK_KINDS), NA_HEADS, NA_TQ, NA_TK), F32)],
        compiler_params=_params(2),
        name="latent_attention",
    )(rpb, q, kb, vb, cache_k, cache_v)


MERGE_TM = 512


MERGE_CTX_TILES = N_CTX // MERGE_TM
BRANCH_W = 512


def _merge_kernel(layer, n_x, *refs):
    x_refs, rest = refs[:n_x], refs[n_x:]
    (mod_ref, ya_ref, yb_ctx_ref, yb_lat_ref, yc_ctx_ref, yc_lat_ref, g_ref, wa_hbm, wb_hbm, wc_hbm, wo_hbm,
     ffn_gain_ref, wrt_ref, brt_ref, tri_ref,
     o_ref, hx_ref, bucket_ref, rank_ref, cnt_out_ref,
     wbr_ref, wo_ref, stage_ref, sem_ref, cnt_ref, cpad_ref) = rest
    i = pl.program_id(0)

    @pl.when(i == 0)
    def _():
        def store_branch(k):
            def st(v):
                wbr_ref[k] = v
            return st

        def store_out(k):
            def st(v):
                wo_ref[k * BRANCH_W:(k + 1) * BRANCH_W, :] = v
            return st

        chunks = [(w.at[layer], store_branch(k)) for k, w in enumerate((wa_hbm, wb_hbm, wc_hbm))]
        chunks += [(wo_hbm.at[layer, pl.ds(k * BRANCH_W, BRANCH_W), :], store_out(k))
                   for k in range(D_MODEL // BRANCH_W)]
        _load_cast(chunks, stage_ref, sem_ref)

    g = _sigmoid(g_ref[...].astype(F32))
    yb = _token_tile(i, MERGE_CTX_TILES, (yb_ctx_ref, yb_lat_ref))
    yc = _token_tile(i, MERGE_CTX_TILES, (yc_ctx_ref, yc_lat_ref))
    merged = (g[:, 0:D_MODEL] * jnp.dot(ya_ref[...], wbr_ref[0], preferred_element_type=F32)
              + g[:, D_MODEL:2 * D_MODEL] * jnp.dot(yb, wbr_ref[1], preferred_element_type=F32)
              + g[:, 2 * D_MODEL:] * jnp.dot(yc, wbr_ref[2], preferred_element_type=F32))
    y = jnp.dot(merged.astype(BF16), wo_ref[...], preferred_element_type=F32)
    gate = mod_ref[:, 2 * D_MODEL:3 * D_MODEL]
    x = _token_tile(i, MERGE_CTX_TILES, x_refs) + gate * y
    o_ref[...] = x
    _route(x, mod_ref, ffn_gain_ref, wrt_ref, brt_ref, tri_ref, hx_ref, bucket_ref, rank_ref, cnt_out_ref,
           cnt_ref, cpad_ref)


def merge_branches(xs, mod3, ya, yb_ctx, yb_lat, yc_ctx, yc_lat, gates, wa, wb, wc, wo, ffn_gain, w_rt, b_rt, layer):
    assert MERGE_TM == RT_TM
    row = lambda i: (i, 0)
    const = lambda shape: pl.BlockSpec(shape, lambda i: (0,) * len(shape))
    tri = jnp.asarray(np.triu(np.ones((RT_TM, RT_TM), np.float32)), BF16)
    hbm = pl.BlockSpec(memory_space=pl.ANY)
    x_specs = ([pl.BlockSpec((MERGE_TM, D_MODEL), row)] if len(xs) == 1
               else _split_specs((MERGE_TM, D_MODEL), MERGE_CTX_TILES))
    return pl.pallas_call(
        functools.partial(_merge_kernel, layer, len(xs)),
        grid=(N_TOK // MERGE_TM,),
        in_specs=x_specs + [
            _mod_spec(layer, MERGE_TM),
            pl.BlockSpec((MERGE_TM, SGU_WIDTH), row),
            *_split_specs((MERGE_TM, LRU_WIDTH), MERGE_CTX_TILES),
            *_split_specs((MERGE_TM, NA_WIDTH), MERGE_CTX_TILES),
            pl.BlockSpec((MERGE_TM, 3 * D_MODEL), row),
            hbm, hbm, hbm, hbm,
            _layer_spec((1, D_MODEL), layer),
            _layer_spec((RT_ROWS, D_MODEL), layer),
            _layer_spec((RT_ROWS, 1), layer),
            const((RT_TM, RT_TM)),
        ],
        out_specs=[
            pl.BlockSpec((MERGE_TM, D_MODEL), row),
            pl.BlockSpec((RT_TM, HX_W), row),
            pl.BlockSpec((1, RT_TM), lambda i: (0, i)),
            pl.BlockSpec((1, RT_TM), lambda i: (0, i)),
            const((CNT_ROWS, COMB_W)),
        ],
        out_shape=[
            jax.ShapeDtypeStruct((N_TOK, D_MODEL), F32),
            jax.ShapeDtypeStruct((N_TOK, HX_W), F32),
            jax.ShapeDtypeStruct((1, N_TOK), jnp.int32),
            jax.ShapeDtypeStruct((1, N_TOK), jnp.int32),
            jax.ShapeDtypeStruct((CNT_ROWS, COMB_W), jnp.int32),
        ],
        scratch_shapes=[pltpu.VMEM((3, BRANCH_W, D_MODEL), BF16), pltpu.VMEM((D_MODEL, D_MODEL), BF16),
                        pltpu.VMEM((2, BRANCH_W, D_MODEL), F32), pltpu.SemaphoreType.DMA((2,)),
                        pltpu.VMEM((CNT_ROWS, COMB_W), F32), pltpu.VMEM((COMB_W, RT_TM), F32)],
        compiler_params=_params(1),
        name="merge_branches",
    )(*xs, mod3, ya, yb_ctx, yb_lat, yc_ctx, yc_lat, gates, wa, wb, wc, wo, ffn_gain, w_rt, b_rt, tri)


RT_TM = 512
RT_ROWS = 32
RT_EXPERT_ROW = 8
EXPERT_PAIRS = ((0, 1), (0, 2), (0, 3), (1, 2), (1, 3), (2, 3))
N_PAIRS = len(EXPERT_PAIRS)
N_BUCKETS = N_GROUPS * N_PAIRS
CNT_ROWS = 32
COMB_W = 128
HX_W = D_MODEL + COMB_W
EXP_TM = 256
EXP_TILES = N_TOK // EXP_TM + N_BUCKETS
HS_ROWS = EXP_TILES * EXP_TM


def _split_bf16(x):
    hi = x.astype(BF16)
    return hi, (x - hi.astype(F32)).astype(BF16)


def _route(x, mod_ref, gain_ref, w_ref, b_ref, tri_ref, hx_ref, bucket_ref, rank_ref, cnt_out_ref,
           cnt_ref, cpad_ref):
    @pl.when(pl.program_id(0) == 0)
    def _():
        cnt_ref[...] = jnp.zeros_like(cnt_ref)
        cpad_ref[...] = jnp.zeros_like(cpad_ref)

    m = mod_ref[...]
    shift, scale = m[:, 3 * D_MODEL:4 * D_MODEL], m[:, 4 * D_MODEL:5 * D_MODEL]
    h = _rms(x, gain_ref[...]) * (1.0 + scale) + shift
    hx_ref[:, 0:D_MODEL] = h
    h_hi, h_lo = _split_bf16(h)
    w_hi, w_lo = _split_bf16(w_ref[...])
    dims = (((1,), (1,)), ((), ()))
    logits = (lax.dot_general(w_hi, h_hi, dims, preferred_element_type=F32)
              + lax.dot_general(w_hi, h_lo, dims, preferred_element_type=F32)
              + lax.dot_general(w_lo, h_hi, dims, preferred_element_type=F32)) + b_ref[...]
    gl = [logits[g:g + 1, :] for g in range(N_GROUPS)]
    gmax = functools.reduce(jnp.maximum, gl)
    gid = jnp.full(gmax.shape, N_GROUPS - 1, jnp.int32)
    for g in reversed(range(N_GROUPS - 1)):
        gid = jnp.where(gl[g] == gmax, g, gid)
    p_grp = 1.0 / functools.reduce(jnp.add, [jnp.exp(v - gmax) for v in gl])
    el = []
    for e in range(EXPERTS_PER_GROUP):
        v = logits[RT_EXPERT_ROW + e:RT_EXPERT_ROW + e + 1, :]
        for g in range(1, N_GROUPS):
            row = RT_EXPERT_ROW + g * EXPERTS_PER_GROUP + e
            v = jnp.where(gid == g, logits[row:row + 1, :], v)
        el.append(v)
    top1 = functools.reduce(jnp.maximum, el)
    idx1 = jnp.full(top1.shape, EXPERTS_PER_GROUP - 1, jnp.int32)
    for e in reversed(range(EXPERTS_PER_GROUP - 1)):
        idx1 = jnp.where(el[e] == top1, e, idx1)
    rest = [jnp.where(idx1 == e, -jnp.inf, el[e]) for e in range(EXPERTS_PER_GROUP)]
    top2 = functools.reduce(jnp.maximum, rest)
    idx2 = jnp.full(top1.shape, EXPERTS_PER_GROUP - 1, jnp.int32)
    for e in reversed(range(EXPERTS_PER_GROUP - 1)):
        idx2 = jnp.where(rest[e] == top2, e, idx2)
    e2 = jnp.exp(top2 - top1)
    w1 = p_grp / (1.0 + e2)
    w2 = p_grp * e2 / (1.0 + e2)
    for e in range(EXPERTS_PER_GROUP):
        cpad_ref[e:e + 1, :] = jnp.where(idx1 == e, w1, 0.0) + jnp.where(idx2 == e, w2, 0.0)
    hx_ref[:, D_MODEL:] = cpad_ref[...].T
    lo = jnp.minimum(idx1, idx2)
    hi = jnp.maximum(idx1, idx2)
    pair = jnp.where(lo == 0, 0, jnp.where(lo == 1, 3, 5)) + (hi - lo - 1)
    bucket = gid * N_PAIRS + pair
    bucket_ref[...] = bucket
    sub = lax.broadcasted_iota(jnp.int32, (CNT_ROWS, RT_TM), 0)
    onehot = jnp.where(sub == bucket, 1.0, 0.0)
    seen = jnp.dot(onehot.astype(BF16), tri_ref[...], preferred_element_type=F32)
    cnt = cnt_ref[...]
    rank_ref[...] = jnp.sum(onehot * (seen - 1.0 + cnt[:, 0:1]), axis=0, keepdims=True).astype(jnp.int32)
    cnt = cnt + jnp.sum(onehot, axis=1, keepdims=True)
    cnt_ref[...] = cnt
    cnt_out_ref[...] = cnt.astype(jnp.int32)


def _router_weights(w_grp, b_grp, w_exp, b_exp):
    pad = lambda v, n: jnp.zeros(v.shape[:1] + (n,) + v.shape[2:], F32)
    gap, tail = RT_EXPERT_ROW - N_GROUPS, RT_ROWS - RT_EXPERT_ROW - N_EXPERTS
    w_grp, w_exp = jnp.swapaxes(w_grp, 1, 2), jnp.swapaxes(w_exp, 1, 2)
    w = jnp.concatenate([w_grp, pad(w_grp, gap), w_exp, pad(w_exp, tail)], axis=1)
    b_grp, b_exp = b_grp[:, :, None], b_exp[:, :, None]
    b = jnp.concatenate([b_grp, pad(b_grp, gap), b_exp, pad(b_exp, tail)], axis=1)
    return w, b


DISP_TM = 512


ROW_GROUP = 64


def _for_each_row(n_rows, fn):
    def group(k, carry):
        g0 = pl.multiple_of(k * ROW_GROUP, ROW_GROUP)
        for u in range(ROW_GROUP):
            fn(g0, u)
        return carry

    lax.fori_loop(0, n_rows // ROW_GROUP, group, 0)


def _dispatch_kernel(pos_ref, start_ref, cnt_ref, nt_ref, hx_hbm, hs_ref, buf_ref, zero_ref, blk_sem, row_sem, zsem):
    i = pl.program_id(0)
    last = pl.num_programs(0) - 1
    slot = i % 2
    base = i * DISP_TM

    def tile_in(tile, s):
        return pltpu.make_async_copy(hx_hbm.at[pl.ds(pl.multiple_of(tile * DISP_TM, DISP_TM), DISP_TM), :],
                                     buf_ref.at[s], blk_sem.at[s])

    def wait_rows(s):
        pltpu.make_async_copy(buf_ref.at[s], hs_ref.at[pl.ds(0, DISP_TM), :], row_sem.at[s]).wait()

    @pl.when(i == 0)
    def _():
        zero_ref[...] = jnp.zeros_like(zero_ref)

        def tile_copy(t):
            return pltpu.make_async_copy(zero_ref, hs_ref.at[pl.ds(pl.multiple_of(t * EXP_TM, EXP_TM), EXP_TM), :],
                                         zsem)

        def last_tile(g):
            return (start_ref[g] + cnt_ref[g, 0] - 1) // EXP_TM

        def each_tile(fn):
            for g in range(N_BUCKETS):
                pl.when(cnt_ref[g, 0] > 0)(functools.partial(fn, last_tile(g)))
            lax.fori_loop(nt_ref[0], EXP_TILES, lambda t, carry: (fn(t), carry)[1], 0)

        each_tile(lambda t: tile_copy(t).start())
        each_tile(lambda t: tile_copy(t).wait())
        tile_in(0, 0).start()

    @pl.when(i > 0)
    def _():
        wait_rows(1 - slot)

    @pl.when(i < last)
    def _():
        tile_in(i + 1, 1 - slot).start()

    tile_in(i, slot).wait()

    def issue(g0, u):
        src = buf_ref.at[slot, pl.ds(g0, ROW_GROUP), :]
        pltpu.make_async_copy(src.at[pl.ds(u, 1), :], hs_ref.at[pl.ds(pos_ref[base + g0 + u], 1), :],
                              row_sem.at[slot]).start(priority=u % 2)

    _for_each_row(DISP_TM, issue)

    @pl.when(i == last)
    def _():
        wait_rows(slot)


def dispatch(pos, starts, counts, n_tiles, hx):
    return pl.pallas_call(
        _dispatch_kernel,
        grid_spec=pltpu.PrefetchScalarGridSpec(
            num_scalar_prefetch=4,
            grid=(N_TOK // DISP_TM,),
            in_specs=[pl.BlockSpec(memory_space=pl.ANY)],
            out_specs=pl.BlockSpec(memory_space=pl.ANY),
            scratch_shapes=[pltpu.VMEM((2, DISP_TM, HX_W), F32), pltpu.VMEM((EXP_TM, HX_W), F32),
                            pltpu.SemaphoreType.DMA((2,)), pltpu.SemaphoreType.DMA((2,)),
                            pltpu.SemaphoreType.DMA(())],
        ),
        out_shape=jax.ShapeDtypeStruct((HS_ROWS, HX_W), F32),
        compiler_params=_params(1),
        name="dispatch",
    )(pos, starts, counts, n_tiles, hx)


(T_GROUP, T_LO, T_HI, T_SLOT, T_RUN_POS, T_NEXT_GROUP, T_FIRST, T_PREFETCHED,
 T_PENDING_GROUP, T_PENDING_EXPERT, T_PENDING_SLOT) = range(11)


def _experts_kernel(layer, tab_ref, nt_ref, hs_ref, w1_hbm, w3_hbm, w2_hbm, ys_ref,
                    w1_ref, w3_ref, w2_ref, st1_ref, st3_ref, st2_ref, sem_ref):
    t = pl.program_id(0)
    valid = t < nt_ref[0]
    group = tab_ref[T_GROUP, t]
    slot = tab_ref[T_SLOT, t]
    run_pos = tab_ref[T_RUN_POS, t]
    next_group = tab_ref[T_NEXT_GROUP, t]
    pending_expert = tab_ref[T_PENDING_EXPERT, t]

    def expert_copies(grp, e):
        idx = grp * EXPERTS_PER_GROUP + e
        return (pltpu.make_async_copy(w1_hbm.at[layer, idx], st1_ref, sem_ref.at[0]),
                pltpu.make_async_copy(w3_hbm.at[layer, idx], st3_ref, sem_ref.at[1]),
                pltpu.make_async_copy(w2_hbm.at[layer, idx], st2_ref, sem_ref.at[2]))

    def finish(copies, dst_slot, e):
        for cp in copies:
            cp.wait()
        w1_ref[dst_slot, e] = st1_ref[...].astype(BF16)
        w3_ref[dst_slot, e] = st3_ref[...].astype(BF16)
        w2_ref[dst_slot, e] = st2_ref[...].astype(BF16)

    @pl.when(jnp.logical_and(valid, pending_expert >= 0))
    def _():
        finish(expert_copies(tab_ref[T_PENDING_GROUP, t], pending_expert), tab_ref[T_PENDING_SLOT, t], pending_expert)

    @pl.when(jnp.logical_and(valid, tab_ref[T_FIRST, t] == 1))
    def _():
        def load(e, carry):
            copies = expert_copies(group, e)
            for cp in copies:
                cp.start()
            finish(copies, slot, e)
            return carry

        lax.fori_loop(tab_ref[T_PREFETCHED, t], EXPERTS_PER_GROUP, load, 0)

    prefetch = jnp.logical_and(valid, jnp.logical_and(next_group >= 0, run_pos < EXPERTS_PER_GROUP))

    @pl.when(prefetch)
    def _():
        for cp in expert_copies(next_group, run_pos):
            cp.start()

    @pl.when(valid)
    def _():
        h = hs_ref[:, 0:D_MODEL].astype(BF16)
        c = hs_ref[:, D_MODEL:]
        lane = lax.broadcasted_iota(jnp.int32, c.shape, 1)
        acc = None
        for e in (tab_ref[T_LO, t], tab_ref[T_HI, t]):
            ce = jnp.sum(jnp.where(lane == e, c, 0.0), axis=1, keepdims=True)
            a = jnp.dot(h, w1_ref[slot, e], preferred_element_type=F32)
            b = jnp.dot(h, w3_ref[slot, e], preferred_element_type=F32)
            hid = (a * _sigmoid(a)) * b * ce
            y = jnp.dot(hid.astype(BF16), w2_ref[slot, e], preferred_element_type=F32)
            acc = y if acc is None else acc + y
        ys_ref[...] = acc

    @pl.when(jnp.logical_not(valid))
    def _():
        ys_ref[...] = jnp.zeros_like(ys_ref)


def experts(tile_table, n_tiles, hs, w1, w3, w2, layer):
    hbm = pl.BlockSpec(memory_space=pl.ANY)
    return pl.pallas_call(
        functools.partial(_experts_kernel, layer),
        grid_spec=pltpu.PrefetchScalarGridSpec(
            num_scalar_prefetch=2,
            grid=(EXP_TILES,),
            in_specs=[pl.BlockSpec((EXP_TM, HX_W), lambda t, *_: (t, 0)), hbm, hbm, hbm],
            out_specs=pl.BlockSpec((EXP_TM, D_MODEL), lambda t, *_: (t, 0)),
            scratch_shapes=[
                pltpu.VMEM((2, EXPERTS_PER_GROUP, D_MODEL, D_EXPERT), BF16),
                pltpu.VMEM((2, EXPERTS_PER_GROUP, D_MODEL, D_EXPERT), BF16),
                pltpu.VMEM((2, EXPERTS_PER_GROUP, D_EXPERT, D_MODEL), BF16),
                pltpu.VMEM((D_MODEL, D_EXPERT), F32),
                pltpu.VMEM((D_MODEL, D_EXPERT), F32),
                pltpu.VMEM((D_EXPERT, D_MODEL), F32),
                pltpu.SemaphoreType.DMA((3,)),
            ],
        ),
        out_shape=jax.ShapeDtypeStruct((HS_ROWS, D_MODEL), F32),
        compiler_params=_params(1),
        name="experts",
    )(tile_table, n_tiles, hs, w1, w3, w2)


N_TABLE_ROWS = T_PENDING_SLOT + 1
EXP_TM_LOG2 = EXP_TM.bit_length() - 1


def _schedule_kernel(cnt_ref, starts_ref, tab_ref, nt_ref):
    i32 = jnp.int32
    t = i32(0)
    for b in range(N_BUCKETS):
        n = lax.shift_right_logical(cnt_ref[b, 0] + (EXP_TM - 1), EXP_TM_LOG2)
        starts_ref[b] = t * EXP_TM

        def fill(j, carry, t=t, group=b // N_PAIRS, pair=EXPERT_PAIRS[b % N_PAIRS]):
            tab_ref[T_GROUP, t + j] = group
            tab_ref[T_LO, t + j] = pair[0]
            tab_ref[T_HI, t + j] = pair[1]
            return carry

        lax.fori_loop(0, n, fill, 0)
        t = t + n
    n_tiles = t
    nt_ref[0] = n_tiles

    def unused(k, carry):
        for row, value in ((T_GROUP, N_GROUPS - 1), (T_LO, 0), (T_HI, 1), (T_SLOT, 0), (T_RUN_POS, 0),
                           (T_NEXT_GROUP, -1), (T_FIRST, 0), (T_PREFETCHED, 0), (T_PENDING_GROUP, 0),
                           (T_PENDING_EXPERT, -1), (T_PENDING_SLOT, 0)):
            tab_ref[row, k] = value
        return carry

    lax.fori_loop(n_tiles, EXP_TILES, unused, 0)

    def back(k, carry):
        next_group, group_after = carry
        tile = n_tiles - 1 - k
        group = tab_ref[T_GROUP, tile]
        next_group = jnp.where(k == 0, -1, jnp.where(group != group_after, group_after, next_group))
        tab_ref[T_NEXT_GROUP, tile] = next_group
        return next_group, group

    lax.fori_loop(0, n_tiles, back, (i32(-1), i32(-1)))

    def forward(tile, carry):
        prev_group, run, run_start, prev_len, prev_fetch, prev_pos, prev_next, prev_slot = carry
        group = tab_ref[T_GROUP, tile]
        first = jnp.logical_or(tile == 0, group != prev_group)
        prev_len = jnp.where(first, tile - run_start, prev_len)
        run = jnp.where(first, run + 1, run)
        run_start = jnp.where(first, tile, run_start)
        slot = lax.rem(run, 2)
        run_pos = tile - run_start
        next_group = tab_ref[T_NEXT_GROUP, tile]
        tab_ref[T_FIRST, tile] = first.astype(i32)
        tab_ref[T_SLOT, tile] = slot
        tab_ref[T_RUN_POS, tile] = run_pos
        tab_ref[T_PREFETCHED, tile] = jnp.minimum(prev_len, EXPERTS_PER_GROUP)
        tab_ref[T_PENDING_EXPERT, tile] = jnp.where(prev_fetch == 1, prev_pos, -1)
        tab_ref[T_PENDING_GROUP, tile] = prev_next
        tab_ref[T_PENDING_SLOT, tile] = 1 - prev_slot
        fetch = jnp.logical_and(next_group >= 0, run_pos < EXPERTS_PER_GROUP).astype(i32)
        return group, run, run_start, prev_len, fetch, run_pos, next_group, slot

    lax.fori_loop(0, n_tiles, forward, (i32(-1), i32(-1), i32(0), i32(0), i32(0), i32(0), i32(0), i32(0)))


def expert_schedule(counts):
    smem = pl.BlockSpec(memory_space=pltpu.SMEM)
    return pl.pallas_call(
        _schedule_kernel,
        in_specs=[smem],
        out_specs=[smem, smem, smem],
        out_shape=[jax.ShapeDtypeStruct((N_BUCKETS,), jnp.int32),
                   jax.ShapeDtypeStruct((N_TABLE_ROWS, EXP_TILES), jnp.int32),
                   jax.ShapeDtypeStruct((1,), jnp.int32)],
        name="expert_schedule",
    )(counts)


RES_TM = 512
RES_CTX_TILES = N_CTX // RES_TM


def _moe_residual_kernel(final, pos_ref, x_ref, mod_ref, gain_ref, ys_ref, *refs):
    out_refs, (ybuf_ref, sem_ref) = refs[:-2], refs[-2:]
    i = pl.program_id(0)
    slot = i % 2

    def gather_tile(tile, tile_slot):
        def issue(g0, u):
            dst = ybuf_ref.at[tile_slot, pl.ds(g0, ROW_GROUP), :]
            pltpu.make_async_copy(ys_ref.at[pl.ds(pos_ref[tile * RES_TM + g0 + u], 1), :], dst.at[pl.ds(u, 1), :],
                                  sem_ref.at[tile_slot]).start(priority=u % 2)

        _for_each_row(RES_TM, issue)

    @pl.when(i == 0)
    def _():
        gather_tile(0, 0)

    @pl.when(i + 1 < pl.num_programs(0))
    def _():
        gather_tile(i + 1, 1 - slot)

    pltpu.make_async_copy(ys_ref.at[pl.ds(0, RES_TM), :], ybuf_ref.at[slot], sem_ref.at[slot]).wait()
    x = x_ref[...] + mod_ref[:, 5 * D_MODEL:6 * D_MODEL] * ybuf_ref[slot]
    if not final:
        out_refs[0][...] = x
        return
    y = _rms(x, gain_ref[...])

    @pl.when(i < RES_CTX_TILES)
    def _():
        out_refs[0][...] = y

    @pl.when(i >= RES_CTX_TILES)
    def _():
        out_refs[1][...] = y


def moe_residual(pos, x, mod4, gain, ys, layer, final):
    if final:
        out_specs = _split_specs((RES_TM, D_MODEL), RES_CTX_TILES)
        out_shape = [jax.ShapeDtypeStruct((N_CTX, D_MODEL), F32), jax.ShapeDtypeStruct((N_LAT, D_MODEL), F32)]
    else:
        out_specs = pl.BlockSpec((RES_TM, D_MODEL), lambda i, *_: (i, 0))
        out_shape = jax.ShapeDtypeStruct((N_TOK, D_MODEL), F32)
    return pl.pallas_call(
        functools.partial(_moe_residual_kernel, final),
        grid_spec=pltpu.PrefetchScalarGridSpec(
            num_scalar_prefetch=1,
            grid=(N_TOK // RES_TM,),
            in_specs=[
                pl.BlockSpec((RES_TM, D_MODEL), lambda i, *_: (i, 0)),
                _mod_spec(layer, RES_TM),
                pl.BlockSpec((1, D_MODEL), lambda i, *_: (0, 0)),
                pl.BlockSpec(memory_space=pl.ANY),
            ],
            out_specs=out_specs,
            scratch_shapes=[pltpu.VMEM((2, RES_TM, D_MODEL), F32), pltpu.SemaphoreType.DMA((2,))],
        ),
        out_shape=out_shape,
        compiler_params=_params(1),
        name="moe_residual_final" if final else "moe_residual",
    )(pos, x, mod4, gain, ys)


def kernel(x_prompt, x_sample, cache_k, cache_v, state_lru, c, c_ctx, w_mod, b_mod, norm_mix, norm_ffn, w_in, sgu_norm, sgu_w, sgu_b, lru_conv_w, lru_conv_b, lru_w_r, lru_b_r, lru_w_i, lru_b_i, lru_lambda, na_rpb, w_branch_sgu, w_branch_lru, w_branch_na, w_out, moe_w_group, moe_b_group, moe_w_expert, moe_b_expert, moe_w1, moe_w3, moe_w2, final_norm_gain):
    xs = (x_prompt.reshape(N_CTX, D_MODEL), x_sample.reshape(N_LAT, D_MODEL))
    cond = jnp.zeros((N_COND, D_MODEL), F32).at[0].set(c_ctx).at[1:1 + DEC_BATCH].set(c)
    mod = modulation(cond, w_mod, b_mod)
    zero_state = jnp.zeros((BATCH, 2, LRU_WIDTH), F32)
    cache_k = cache_k.reshape(DEC_BATCH, DEPTH, PAST_LEN, NA_WIDTH)
    cache_v = cache_v.reshape(DEC_BATCH, DEPTH, PAST_LEN, NA_WIDTH)
    final_gain = final_norm_gain.reshape(1, D_MODEL)
    norm_mix = norm_mix.reshape(DEPTH, 1, D_MODEL)
    norm_ffn = norm_ffn.reshape(DEPTH, 1, D_MODEL)
    sgu_norm = sgu_norm.reshape(DEPTH, 1, SGU_WIDTH)
    sgu_b_t = jnp.swapaxes(sgu_b, 1, 2)
    lru_conv_b = lru_conv_b.reshape(DEPTH, 1, LRU_WIDTH)
    w_gate = _block_diag_gate_weights(lru_w_r, lru_w_i)
    rpb = na_rpb.reshape(DEPTH, NA_HEADS * N_DROW * N_DCOL)
    w_rt, b_rt = _router_weights(moe_w_group, moe_b_group, moe_w_expert, moe_b_expert)
    caches, ss = [], []
    for l in range(DEPTH):
        ya, za, q, kb, vb, k_ctx, v_ctx, gates = in_projection(
            xs, mod, norm_mix, w_in, sgu_norm, sgu_w, sgu_b_t, l, prev_caches=caches if l == DEPTH - 1 else ())
        caches.append((k_ctx, v_ctx))
        lru_args = (lru_conv_w, lru_conv_b, w_gate, lru_b_r, lru_b_i, lru_lambda)
        yb_ctx, st_ctx = rglru(za, *lru_args, zero_state, pl.BlockSpec((None, 2, LRU_WIDTH), lambda i: (i, 0, 0)),
                               SEQ, BATCH, 0, l)
        yb_lat, _ = rglru(za, *lru_args, state_lru,
                          pl.BlockSpec((None, None, 2, LRU_WIDTH), lambda i, l=l: (i, l, 0, 0)),
                          DEC_SEQ, DEC_BATCH, N_CTX, l)
        yc_ctx = context_attention(q, kb, vb)
        yc_lat = latent_attention(q, kb, vb, cache_k, cache_v, rpb, l)
        x, hx, bucket, rank, counts = merge_branches(
            xs, mod, ya, yb_ctx, yb_lat, yc_ctx, yc_lat, gates, w_branch_sgu, w_branch_lru, w_branch_na, w_out,
            norm_ffn, w_rt, b_rt, l)
        bucket, rank = bucket.reshape(N_TOK), rank.reshape(N_TOK)
        starts, tile_table, n_tiles = expert_schedule(counts)
        pos = rank + jnp.sum(jnp.where(bucket[:, None] == jnp.arange(N_BUCKETS), starts[None, :], 0), axis=1)
        hs = dispatch(pos, starts, counts, n_tiles, hx)
        ys = experts(tile_table, n_tiles, hs, moe_w1, moe_w3, moe_w2, l)
        if l < DEPTH - 1:
            xs = (moe_residual(pos, x, mod, final_gain, ys, l, False),)
        else:
            y_ctx, y_lat = moe_residual(pos, x, mod, final_gain, ys, l, True)
        ss.append(st_ctx)
    cache_shape = (BATCH, DEPTH, SEQ, NA_HEADS, NA_HEAD_DIM)
    return (y_ctx.reshape(BATCH, SEQ, D_MODEL), y_lat.reshape(DEC_BATCH, DEC_SEQ, D_MODEL),
            k_ctx.reshape(cache_shape), v_ctx.reshape(cache_shape), jnp.stack(ss, axis=1))
```

```python
import functools

import jax
import jax.numpy as jnp
import numpy as np
from jax import lax
from jax.experimental import pallas as pl
from jax.experimental.pallas import tpu as pltpu

F32 = jnp.float32
BF16 = jnp.bfloat16

D_MODEL = 1024
BATCH = 16
SEQ = 256
DEPTH = 2
DEC_BATCH = 4
DEC_SEQ = 2048
PAST_LEN = 512
GRID_W = 64
CHUNK = 128
SGU_WIDTH = 512
SGU_GROUPS = 4
LRU_WIDTH = 512
LRU_BLOCKS = 8
CONV_WIDTH = 4
LRU_C = 8.0
NA_HEADS = 8
NA_HEAD_DIM = 64
NA_WIDTH = 512
WIN_ROWS = 8
WIN_COLS = 16
N_GROUPS = 4
EXPERTS_PER_GROUP = 4
N_EXPERTS = 16
D_EXPERT = 512
IN_WIDTH = 6656
EPS = 1e-6
NEG_INF = -1e30

N_CTX = BATCH * SEQ
N_LAT = DEC_BATCH * DEC_SEQ
N_TOK = N_CTX + N_LAT
N_COND = 8
MOD_WIDTH = 6 * D_MODEL
GRID_ROWS = DEC_SEQ // GRID_W

VMEM_LIMIT_BYTES = 56 * 1024 * 1024


def _params(n_axes):
    return pltpu.CompilerParams(dimension_semantics=("arbitrary",) * n_axes,
                                vmem_limit_bytes=VMEM_LIMIT_BYTES)


def _cond_row(tile, tile_rows):
    tok = tile * tile_rows
    return jnp.where(tok < N_CTX, 0, 1 + (tok - N_CTX) // DEC_SEQ)


def _layer_spec(shape, layer):
    return pl.BlockSpec((None,) + tuple(shape), lambda i, *_: (layer,) + (0,) * len(shape))


def _mod_spec(layer, tile_rows):
    return pl.BlockSpec((None, None, 1, MOD_WIDTH), lambda i, *_: (layer, _cond_row(i, tile_rows), 0, 0))


def _rms(x, gain):
    return x * lax.rsqrt(jnp.mean(x * x, axis=-1, keepdims=True) + EPS) * gain


def _bdot(a, b):
    return jnp.dot(a.astype(BF16), b.astype(BF16), preferred_element_type=F32)


def _bdot_t(a, b):
    return lax.dot_general(a.astype(BF16), b.astype(BF16), (((1,), (1,)), ((), ())),
                           preferred_element_type=F32)


MOD_TN = 1536


def _mod_kernel(cond_ref, w_ref, b_ref, o_ref):
    c = cond_ref[...]
    s = c * jax.nn.sigmoid(c)
    mod = _bdot(s, w_ref[...]) + b_ref[...]
    for r in range(N_COND):
        o_ref[r] = mod[r:r + 1, :]


def modulation(cond, w_mod, b_mod):
    return pl.pallas_call(
        _mod_kernel,
        grid=(DEPTH, MOD_WIDTH // MOD_TN),
        in_specs=[
            pl.BlockSpec((N_COND, D_MODEL), lambda l, j: (0, 0)),
            pl.BlockSpec((None, D_MODEL, MOD_TN), lambda l, j: (l, 0, j)),
            pl.BlockSpec((None, 1, MOD_TN), lambda l, j: (l, 0, j)),
        ],
        out_specs=pl.BlockSpec((None, N_COND, 1, MOD_TN), lambda l, j: (l, 0, 0, j)),
        out_shape=jax.ShapeDtypeStruct((DEPTH, N_COND, 1, MOD_WIDTH), F32),
        compiler_params=_params(2),
        name="modulation",
    )(cond, w_mod, b_mod.reshape(DEPTH, 1, MOD_WIDTH))


IN_TM = 256
IN_CTX_TILES = N_CTX // IN_TM
ZA_WIDTH = 4 * 512
KV_OFF = ZA_WIDTH + NA_WIDTH
GATE_OFF = KV_OFF + 2 * NA_WIDTH
W_CHUNK = 512
Q_SCALE = NA_HEAD_DIM ** -0.5 * float(np.log2(np.e))


def _load_cast(chunks, stage_ref, sem_ref):
    def copy(j):
        return pltpu.make_async_copy(chunks[j][0], stage_ref.at[j % 2], sem_ref.at[j % 2])

    copy(0).start()
    for j in range(len(chunks)):
        if j + 1 < len(chunks):
            copy(j + 1).start()
        copy(j).wait()
        chunks[j][1](stage_ref[j % 2].astype(BF16))


def _token_tile(i, n_ctx_tiles, refs):
    if len(refs) == 1:
        return refs[0][...]
    return jnp.where(i < n_ctx_tiles, refs[0][...], refs[1][...])


def _split_specs(block, n_ctx_tiles):
    return [pl.BlockSpec(block, lambda i, *_: (jnp.minimum(i, n_ctx_tiles - 1), 0)),
            pl.BlockSpec(block, lambda i, *_: (jnp.maximum(i - n_ctx_tiles, 0), 0))]


SGU_GD = SGU_WIDTH // SGU_GROUPS


def _spatial_gating(u, v, gain, ws_ref, bs_ref):
    u = jax.nn.gelu(u)
    v = _rms(jax.nn.gelu(v), gain).astype(BF16)
    out = []
    for g in range(SGU_GROUPS):
        cols = slice(g * SGU_GD, (g + 1) * SGU_GD)
        mixed = jnp.dot(ws_ref[g].astype(BF16), v[:, cols], preferred_element_type=F32) + bs_ref[:, g:g + 1]
        out.append(u[:, cols] * mixed)
    return jnp.concatenate(out, axis=1)


def _inproj_kernel(layer, n_x, n_prev, *refs):
    x_refs, (mod_ref, gain_ref, w_hbm, sgu_gain_ref, ws_ref, bs_ref) = refs[:n_x], refs[n_x:n_x + 6]
    prev_refs, rest = refs[n_x + 6:n_x + 6 + 2 * n_prev], refs[n_x + 6 + 2 * n_prev:]
    ya_ref, zb_ref, q_ref, kb_ref, vb_ref, kc_ref, vc_ref, g_ref, w_ref, stage_ref, sem_ref = rest
    i = pl.program_id(0)

    @pl.when(i == 0)
    def _():
        def store(c):
            def st(v):
                w_ref[:, c * W_CHUNK:(c + 1) * W_CHUNK] = v
            return st

        _load_cast([(w_hbm.at[layer, :, pl.ds(c * W_CHUNK, W_CHUNK)], store(c)) for c in range(IN_WIDTH // W_CHUNK)],
                   stage_ref, sem_ref)

    m = mod_ref[...]
    shift, scale = m[:, 0:D_MODEL], m[:, D_MODEL:2 * D_MODEL]
    h = (_rms(_token_tile(i, IN_CTX_TILES, x_refs), gain_ref[...]) * (1.0 + scale) + shift).astype(BF16)
    za = jnp.dot(h, w_ref[:, 0:ZA_WIDTH], preferred_element_type=F32)
    for c in range(IN_TM // CHUNK):
        rows = slice(c * CHUNK, (c + 1) * CHUNK)
        ya_ref[rows, :] = _spatial_gating(za[rows, 0:SGU_WIDTH], za[rows, SGU_WIDTH:2 * SGU_WIDTH],
                                          sgu_gain_ref[...], ws_ref, bs_ref).astype(BF16)
    zb_ref[:, 0:LRU_WIDTH] = za[:, 2 * SGU_WIDTH:2 * SGU_WIDTH + LRU_WIDTH].astype(BF16)
    zb_ref[:, LRU_WIDTH:] = jax.nn.gelu(za[:, 2 * SGU_WIDTH + LRU_WIDTH:]).astype(BF16)
    q_ref[...] = (jnp.dot(h, w_ref[:, ZA_WIDTH:KV_OFF], preferred_element_type=F32) * Q_SCALE).astype(BF16)
    kv = jnp.dot(h, w_ref[:, KV_OFF:GATE_OFF], preferred_element_type=F32)
    kb_ref[...] = kv[:, 0:NA_WIDTH].astype(BF16)
    vb_ref[...] = kv[:, NA_WIDTH:].astype(BF16)
    g_ref[...] = _sigmoid(jnp.dot(h, w_ref[:, GATE_OFF:], preferred_element_type=F32)).astype(BF16)

    @pl.when(i < IN_CTX_TILES)
    def _():
        if n_prev == 0:
            kc_ref[...] = kv[:, 0:NA_WIDTH]
            vc_ref[...] = kv[:, NA_WIDTH:]
        else:
            for p in range(n_prev):
                kc_ref[p] = prev_refs[2 * p][...]
                vc_ref[p] = prev_refs[2 * p + 1][...]
            kc_ref[n_prev] = kv[:, 0:NA_WIDTH]
            vc_ref[n_prev] = kv[:, NA_WIDTH:]


def in_projection(xs, mod3, gain, w_in, sgu_gain, sgu_w, sgu_b_t, layer, prev_caches=()):
    assert IN_TM == SEQ
    row = lambda i: (i, 0)
    ctx_row = lambda i: (jnp.minimum(i, IN_CTX_TILES - 1), 0)
    x_specs = ([pl.BlockSpec((IN_TM, D_MODEL), row)] if len(xs) == 1
               else _split_specs((IN_TM, D_MODEL), IN_CTX_TILES))
    n_prev = len(prev_caches)
    if n_prev:
        cache_spec = pl.BlockSpec((None, n_prev + 1, SEQ, NA_WIDTH),
                                  lambda i: (jnp.minimum(i, IN_CTX_TILES - 1), 0, 0, 0))
        cache_shape = jax.ShapeDtypeStruct((BATCH, n_prev + 1, SEQ, NA_WIDTH), F32)
    else:
        cache_spec = pl.BlockSpec((IN_TM, NA_WIDTH), ctx_row)
        cache_shape = jax.ShapeDtypeStruct((N_CTX, NA_WIDTH), F32)
    return pl.pallas_call(
        functools.partial(_inproj_kernel, layer, len(xs), n_prev),
        grid=(N_TOK // IN_TM,),
        in_specs=x_specs + [
            _mod_spec(layer, IN_TM),
            _layer_spec((1, D_MODEL), layer),
            pl.BlockSpec(memory_space=pl.ANY),
            _layer_spec((1, SGU_WIDTH), layer),
            _layer_spec((SGU_GROUPS, CHUNK, CHUNK), layer),
            _layer_spec((CHUNK, SGU_GROUPS), layer),
        ] + [pl.BlockSpec((IN_TM, NA_WIDTH), ctx_row)] * (2 * n_prev),
        out_specs=[
            pl.BlockSpec((IN_TM, SGU_WIDTH), row),
            pl.BlockSpec((IN_TM, 2 * LRU_WIDTH), row),
            pl.BlockSpec((IN_TM, NA_WIDTH), row),
            pl.BlockSpec((IN_TM, NA_WIDTH), row),
            pl.BlockSpec((IN_TM, NA_WIDTH), row),
            cache_spec,
            cache_spec,
            pl.BlockSpec((IN_TM, 3 * D_MODEL), row),
        ],
        out_shape=[
            jax.ShapeDtypeStruct((N_TOK, SGU_WIDTH), BF16),
            jax.ShapeDtypeStruct((N_TOK, 2 * LRU_WIDTH), BF16),
            jax.ShapeDtypeStruct((N_TOK, NA_WIDTH), BF16),
            jax.ShapeDtypeStruct((N_TOK, NA_WIDTH), BF16),
            jax.ShapeDtypeStruct((N_TOK, NA_WIDTH), BF16),
            cache_shape,
            cache_shape,
            jax.ShapeDtypeStruct((N_TOK, 3 * D_MODEL), BF16),
        ],
        scratch_shapes=[pltpu.VMEM((D_MODEL, IN_WIDTH), BF16), pltpu.VMEM((2, D_MODEL, W_CHUNK), F32),
                        pltpu.SemaphoreType.DMA((2,))],
        compiler_params=_params(1),
        name="in_projection",
    )(*xs, mod3, gain, w_in, sgu_gain, sgu_w, sgu_b_t, *[c for kv in prev_caches for c in kv])


LRU_TC = 256
LRU_HALF = 256
SUB = 8
HALO = 8
LANES = 128
LRU_SLABS = LRU_WIDTH // LANES


def _sigmoid(x):
    return 0.5 * jnp.tanh(0.5 * x) + 0.5


def _lru_kernel(seq_len, xr_ref, gr_ref, cw_ref, cb_ref, wlo_ref, whi_ref, br_ref, bi_ref,
                lam_ref, h0_ref, y_ref, st_ref, xp_ref, a_ref, b_ref, h_ref):
    seg_len = seq_len // SUB
    pitch = seg_len + SUB
    segs_per_chunk = max(LRU_TC // seg_len, 1)
    n_chunks = seq_len // LRU_TC
    zeros = jnp.zeros((HALO, LRU_WIDTH), F32)
    xp_ref[0:HALO, :] = zeros
    xp_ref[seq_len + HALO:seq_len + 2 * HALO, :] = zeros

    def copy_in(c, carry):
        r0 = pl.multiple_of(c * LRU_TC, LRU_TC)
        xp_ref[pl.ds(r0 + HALO, LRU_TC), :] = xr_ref[pl.ds(r0, LRU_TC), :].astype(F32)
        return carry

    lax.fori_loop(0, n_chunks, copy_in, 0)

    def chunk_rows(c, seg):
        n = min(seg_len, LRU_TC)
        start = pl.multiple_of((c * segs_per_chunk + seg) * pitch, SUB)
        return pl.ds(start, n), slice(seg * n, (seg + 1) * n)

    cw = cw_ref[...]
    cb = cb_ref[...]
    win = LRU_TC + 2 * HALO
    neg_lam = -lam_ref[...]
    softplus = jnp.maximum(neg_lam, 0.0) + jnp.log1p(jnp.exp(-jnp.abs(neg_lam)))
    decay = (0.5 * LRU_C) * softplus
    half_br = 0.5 * br_ref[...]
    half_bi = 0.5 * bi_ref[...]

    def gates(c, carry):
        r0 = pl.multiple_of(c * LRU_TC, LRU_TC)
        w = xp_ref[pl.ds(r0, win), :]
        xc = (cw[0:1, :] * pltpu.roll(w, 1, 0)[HALO:HALO + LRU_TC]
              + cw[1:2, :] * w[HALO:HALO + LRU_TC]
              + cw[2:3, :] * pltpu.roll(w, win - 1, 0)[HALO:HALO + LRU_TC]
              + cw[3:4, :] * pltpu.roll(w, win - 2, 0)[HALO:HALO + LRU_TC]) + cb
        xb = xc.astype(BF16)
        for half, w_ref in enumerate((wlo_ref, whi_ref)):
            cols = slice(half * LRU_HALF, (half + 1) * LRU_HALF)
            pre = jnp.dot(xb[:, cols], w_ref[...], preferred_element_type=F32)
            half_x = 0.5 * xc[:, cols]
            for d in range(2):
                tr = jnp.tanh(pre[:, (2 * d) * LRU_HALF:(2 * d + 1) * LRU_HALF] + half_br[d:d + 1, cols])
                ti = jnp.tanh(pre[:, (2 * d + 1) * LRU_HALF:(2 * d + 2) * LRU_HALF] + half_bi[d:d + 1, cols])
                neg_log_a = decay[d:d + 1, cols] * tr + decay[d:d + 1, cols]
                a = jnp.exp2(neg_log_a * -LOG2E)
                z = jnp.tanh(neg_log_a) * (a * a + 1.0)
                b = jnp.where(z > 0.0, z * lax.rsqrt(z), 0.0) * (half_x * ti + half_x)
                for k in range(LRU_HALF // LANES):
                    slab = half * (LRU_HALF // LANES) + k
                    lanes = slice(k * LANES, (k + 1) * LANES)
                    for seg in range(segs_per_chunk):
                        dst, src = chunk_rows(c, seg)
                        a_ref[d, slab, dst, :] = a[src, lanes]
                        b_ref[d, slab, dst, :] = b[src, lanes]
        return carry

    lax.fori_loop(0, n_chunks, gates, 0)

    def step_rows(d, j):
        return pl.ds(j if d == 0 else seg_len - 1 - j, SUB, stride=pitch)

    def reduce_step(j, carry):
        out = []
        for d in range(2):
            for slab in range(LRU_SLABS):
                big_a, big_b = carry[d * LRU_SLABS + slab]
                a = a_ref[d, slab, step_rows(d, j), :]
                b = b_ref[d, slab, step_rows(d, j), :]
                out.append((a * big_a, a * big_b + b))
        return tuple(out)

    ident = (jnp.ones((SUB, LANES), F32), jnp.zeros((SUB, LANES), F32))
    totals = lax.fori_loop(0, seg_len, reduce_step, (ident,) * (2 * LRU_SLABS), unroll=8)

    sub = lax.broadcasted_iota(jnp.int32, (SUB, LANES), 0)
    entering = []
    for d in range(2):
        for slab in range(LRU_SLABS):
            big_a, big_b = totals[d * LRU_SLABS + slab]
            h = h0_ref[d:d + 1, slab * LANES:(slab + 1) * LANES]
            rows = jnp.zeros((SUB, LANES), F32)
            for s in (range(SUB) if d == 0 else reversed(range(SUB))):
                rows = jnp.where(sub == s, h, rows)
                h = big_a[s:s + 1, :] * h + big_b[s:s + 1, :]
            entering.append(rows)
            st_ref[d:d + 1, slab * LANES:(slab + 1) * LANES] = h

    def apply_step(j, carry):
        out = []
        for d in range(2):
            for slab in range(LRU_SLABS):
                h = (a_ref[d, slab, step_rows(d, j), :] * carry[d * LRU_SLABS + slab]
                     + b_ref[d, slab, step_rows(d, j), :])
                h_ref[d, slab, step_rows(d, j), :] = h
                out.append(h)
        return tuple(out)

    lax.fori_loop(0, seg_len, apply_step, tuple(entering), unroll=8)

    def merge(c, carry):
        r0 = pl.multiple_of(c * LRU_TC, LRU_TC)
        gate = gr_ref[pl.ds(r0, LRU_TC), :].astype(F32)
        for slab in range(LRU_SLABS):
            lanes = slice(slab * LANES, (slab + 1) * LANES)
            for seg in range(segs_per_chunk):
                src, dst = chunk_rows(c, seg)
                h = h_ref[0, slab, src, :] + h_ref[1, slab, src, :]
                y_ref[pl.ds(r0 + dst.start, dst.stop - dst.start), lanes] = (h * gate[dst, lanes]).astype(BF16)
        return carry

    lax.fori_loop(0, n_chunks, merge, 0)


def rglru(za, conv_w, conv_b, w_gate, b_r, b_i, lam, h0, h0_spec, seq_len, n_seq, tok_off, layer):
    blk0 = tok_off // seq_len
    half_spec = lambda half: pl.BlockSpec((None, None, LRU_HALF, 4 * LRU_HALF), lambda i: (layer, half, 0, 0))
    return pl.pallas_call(
        functools.partial(_lru_kernel, seq_len),
        grid=(n_seq,),
        in_specs=[
            pl.BlockSpec((seq_len, LRU_WIDTH), lambda i: (i + blk0, 0)),
            pl.BlockSpec((seq_len, LRU_WIDTH), lambda i: (i + blk0, 1)),
            _layer_spec((CONV_WIDTH, LRU_WIDTH), layer),
            _layer_spec((1, LRU_WIDTH), layer),
            half_spec(0),
            half_spec(1),
            _layer_spec((2, LRU_WIDTH), layer),
            _layer_spec((2, LRU_WIDTH), layer),
            _layer_spec((2, LRU_WIDTH), layer),
            h0_spec,
        ],
        out_specs=[
            pl.BlockSpec((seq_len, LRU_WIDTH), lambda i: (i, 0)),
            pl.BlockSpec((None, 2, LRU_WIDTH), lambda i: (i, 0, 0)),
        ],
        out_shape=[
            jax.ShapeDtypeStruct((n_seq * seq_len, LRU_WIDTH), BF16),
            jax.ShapeDtypeStruct((n_seq, 2, LRU_WIDTH), F32),
        ],
        scratch_shapes=[
            pltpu.VMEM((seq_len + 2 * HALO, LRU_WIDTH), F32),
            pltpu.VMEM((2, LRU_SLABS, seq_len + SUB * SUB, LANES), F32),
            pltpu.VMEM((2, LRU_SLABS, seq_len + SUB * SUB, LANES), F32),
            pltpu.VMEM((2, LRU_SLABS, seq_len + SUB * SUB, LANES), F32),
        ],
        compiler_params=_params(1),
        name=f"rglru_{seq_len}",
    )(za, za, conv_w, conv_b, w_gate, w_gate, b_r, b_i, lam, h0)


def _block_diag_gate_weights(w_r, w_i):
    per_half = LRU_BLOCKS // 2
    blk = LRU_WIDTH // LRU_BLOCKS
    g = jnp.stack([w_r, w_i], axis=2).reshape(DEPTH, 2, 2, 2, per_half, blk, blk)
    w = jnp.einsum("ldghaij,ab->lhaidgbj", g, jnp.eye(per_half, dtype=g.dtype))
    return (0.5 * w.reshape(DEPTH, 2, LRU_HALF, 4 * LRU_HALF)).astype(BF16)


PAIR_W = 2 * NA_HEAD_DIM
LOG2E = float(np.log2(np.e))


def _head_lanes(shape, head):
    lane = lax.broadcasted_iota(jnp.int32, shape, 1)
    return (lane < NA_HEAD_DIM) if head == 0 else (lane >= NA_HEAD_DIM)


def _one_head(q_pair, head):
    return jnp.where(_head_lanes(q_pair.shape, head), q_pair.astype(F32), 0.0).astype(BF16)


def _join_heads(o0, o1):
    return jnp.where(_head_lanes(o0.shape, 0), o0, o1)


def _ctx_attn_kernel(q_ref, k_ref, v_ref, o_ref):
    for j in range(NA_HEADS // 2):
        cols = slice(j * PAIR_W, (j + 1) * PAIR_W)
        outs = []
        for head in range(2):
            s = _bdot_t(_one_head(q_ref[:, cols], head), k_ref[:, cols])
            p = jnp.exp2(s - jnp.max(s, axis=-1, keepdims=True))
            denom = jnp.sum(p, axis=-1, keepdims=True)
            outs.append(jnp.dot(p.astype(BF16), v_ref[:, cols], preferred_element_type=F32) / denom)
        o_ref[:, cols] = _join_heads(*outs).astype(BF16)


def context_attention(q, kb, vb):
    spec = pl.BlockSpec((SEQ, NA_WIDTH), lambda i: (i, 0))
    return pl.pallas_call(
        _ctx_attn_kernel,
        grid=(BATCH,),
        in_specs=[spec, spec, spec],
        out_specs=spec,
        out_shape=jax.ShapeDtypeStruct((N_CTX, NA_WIDTH), BF16),
        compiler_params=_params(1),
        name="context_attention",
    )(q, kb, vb)


NA_QROWS = 4
NA_TQ = NA_QROWS * GRID_W
NA_KROWS = NA_QROWS + WIN_ROWS
NA_TK = NA_KROWS * GRID_W
NA_QBLOCKS = GRID_ROWS // NA_QROWS


N_DROW = 2 * WIN_ROWS - 1
N_DCOL = 2 * WIN_COLS - 1
NA_BLOCK_KINDS = (0, 1, NA_QBLOCKS - 1)


def _first_key_row(qb):
    return (np.clip if isinstance(qb, int) else jnp.clip)(qb * NA_QROWS - WIN_ROWS // 2, 0, GRID_ROWS - NA_KROWS)


def _build_bias(layer, rpb_ref, table_ref, bias_ref):
    qc = lax.broadcasted_iota(jnp.int32, (GRID_W, GRID_W), 0)
    kc = lax.broadcasted_iota(jnp.int32, (GRID_W, GRID_W), 1)
    col0 = jnp.clip(qc - WIN_COLS // 2, 0, GRID_W - WIN_COLS)
    col_ok = jnp.logical_and(kc >= col0, kc < col0 + WIN_COLS)
    d_col = jnp.clip(kc - qc, 1 - WIN_COLS, WIN_COLS - 1) + WIN_COLS - 1
    neg = jnp.full((GRID_W, GRID_W), NEG_INF, F32)

    def table_entry(idx, carry):
        t = neg
        for j in range(N_DCOL):
            t = jnp.where(d_col == j, rpb_ref[layer, idx * N_DCOL + j], t)
        table_ref[idx] = jnp.where(col_ok, t * LOG2E, neg)
        return carry

    lax.fori_loop(0, NA_HEADS * N_DROW, table_entry, 0)

    def head_blocks(h, carry):
        for kind, qb in enumerate(NA_BLOCK_KINDS):
            for i in range(NA_QROWS):
                qrow = qb * NA_QROWS + i
                win0 = int(np.clip(qrow - WIN_ROWS // 2, 0, GRID_ROWS - WIN_ROWS))
                for kr in range(NA_KROWS):
                    krow = int(_first_key_row(qb)) + kr
                    inside = win0 <= krow < win0 + WIN_ROWS
                    blk = table_ref[h * N_DROW + (krow - qrow + WIN_ROWS - 1)] if inside else neg
                    bias_ref[kind, h, i * GRID_W:(i + 1) * GRID_W, kr * GRID_W:(kr + 1) * GRID_W] = blk
        return carry

    lax.fori_loop(0, NA_HEADS, head_blocks, 0)


def _lat_attn_kernel(layer, rpb_ref, q_ref, k_ref, v_ref, ck_ref, cv_ref, o_ref, table_ref, bias_ref):
    qb = pl.program_id(1)

    @pl.when(jnp.logical_and(pl.program_id(0) == 0, qb == 0))
    def _():
        _build_bias(layer, rpb_ref, table_ref, bias_ref)

    kind = jnp.where(qb == 0, 0, jnp.where(qb == NA_QBLOCKS - 1, 2, 1))
    k0 = pl.multiple_of(_first_key_row(qb) * GRID_W, GRID_W)
    for j in range(NA_HEADS // 2):
        cols = slice(j * PAIR_W, (j + 1) * PAIR_W)
        k_loc = k_ref[pl.ds(k0, NA_TK), cols]
        v_loc = v_ref[pl.ds(k0, NA_TK), cols]
        k_ctx = ck_ref[:, cols].astype(BF16)
        v_ctx = cv_ref[:, cols].astype(BF16)
        outs = []
        for head in range(2):
            qh = _one_head(q_ref[:, cols], head)
            s_loc = _bdot_t(qh, k_loc) + bias_ref[kind, 2 * j + head]
            s_ctx = _bdot_t(qh, k_ctx)
            m = jnp.maximum(jnp.max(s_loc, axis=-1, keepdims=True), jnp.max(s_ctx, axis=-1, keepdims=True))
            p_loc = jnp.exp2(s_loc - m)
            p_ctx = jnp.exp2(s_ctx - m)
            denom = jnp.sum(p_loc, axis=-1, keepdims=True) + jnp.sum(p_ctx, axis=-1, keepdims=True)
            o = (jnp.dot(p_loc.astype(BF16), v_loc, preferred_element_type=F32)
                 + jnp.dot(p_ctx.astype(BF16), v_ctx, preferred_element_type=F32))
            outs.append(o / denom)
        o_ref[:, cols] = _join_heads(*outs).astype(BF16)


def latent_attention(q, kb, vb, cache_k, cache_v, rpb, layer):
    q_blk0 = N_CTX // NA_TQ
    s_blk0 = N_CTX // DEC_SEQ
    cache_spec = pl.BlockSpec((None, None, PAST_LEN, NA_WIDTH), lambda b, m: (b, layer, 0, 0))
    return pl.pallas_call(
        functools.partial(_lat_attn_kernel, layer),
        grid=(DEC_BATCH, NA_QBLOCKS),
        in_specs=[
            pl.BlockSpec(memory_space=pltpu.SMEM),
            pl.BlockSpec((NA_TQ, NA_WIDTH), lambda b, m: (q_blk0 + b * NA_QBLOCKS + m, 0)),
            pl.BlockSpec((DEC_SEQ, NA_WIDTH), lambda b, m: (s_blk0 + b, 0)),
            pl.BlockSpec((DEC_SEQ, NA_WIDTH), lambda b, m: (s_blk0 + b, 0)),
            cache_spec,
            cache_spec,
        ],
        out_specs=pl.BlockSpec((NA_TQ, NA_WIDTH), lambda b, m: (b * NA_QBLOCKS + m, 0)),
        out_shape=jax.ShapeDtypeStruct((N_LAT, NA_WIDTH), BF16),
        scratch_shapes=[pltpu.VMEM((NA_HEADS * N_DROW, GRID_W, GRID_W), F32),
                        pltpu.VMEM((len(NA_BLOCK_KINDS), NA_HEADS, NA_TQ, NA_TK), F32)],
        compiler_params=_params(2),
        name="latent_attention",
    )(rpb, q, kb, vb, cache_k, cache_v)


MERGE_TM = 512


MERGE_CTX_TILES = N_CTX // MERGE_TM
BRANCH_W = 512


def _merge_kernel(layer, n_x, *refs):
    x_refs, rest = refs[:n_x], refs[n_x:]
    (mod_ref, ya_ref, yb_ctx_ref, yb_lat_ref, yc_ctx_ref, yc_lat_ref, g_ref, wa_hbm, wb_hbm, wc_hbm, wo_hbm,
     ffn_gain_ref, wrt_ref, brt_ref, tri_ref,
     o_ref, hx_ref, bucket_ref, rank_ref, cnt_out_ref,
     wbr_ref, wo_ref, stage_ref, sem_ref, cnt_ref, cpad_ref) = rest
    i = pl.program_id(0)

    @pl.when(i == 0)
    def _():
        def store_branch(k):
            def st(v):
                wbr_ref[k] = v
            return st

        def store_out(k):
            def st(v):
                wo_ref[k * BRANCH_W:(k + 1) * BRANCH_W, :] = v
            return st

        chunks = [(w.at[layer], store_branch(k)) for k, w in enumerate((wa_hbm, wb_hbm, wc_hbm))]
        chunks += [(wo_hbm.at[layer, pl.ds(k * BRANCH_W, BRANCH_W), :], store_out(k))
                   for k in range(D_MODEL // BRANCH_W)]
        _load_cast(chunks, stage_ref, sem_ref)

    g = g_ref[...].astype(F32)
    yb = _token_tile(i, MERGE_CTX_TILES, (yb_ctx_ref, yb_lat_ref))
    yc = _token_tile(i, MERGE_CTX_TILES, (yc_ctx_ref, yc_lat_ref))
    merged = (g[:, 0:D_MODEL] * jnp.dot(ya_ref[...], wbr_ref[0], preferred_element_type=F32)
              + g[:, D_MODEL:2 * D_MODEL] * jnp.dot(yb, wbr_ref[1], preferred_element_type=F32)
              + g[:, 2 * D_MODEL:] * jnp.dot(yc, wbr_ref[2], preferred_element_type=F32))
    y = jnp.dot(merged.astype(BF16), wo_ref[...], preferred_element_type=F32)
    gate = mod_ref[:, 2 * D_MODEL:3 * D_MODEL]
    x = _token_tile(i, MERGE_CTX_TILES, x_refs) + gate * y
    o_ref[...] = x
    _route(x, mod_ref, ffn_gain_ref, wrt_ref, brt_ref, tri_ref, hx_ref, bucket_ref, rank_ref, cnt_out_ref,
           cnt_ref, cpad_ref)


def merge_branches(xs, mod3, ya, yb_ctx, yb_lat, yc_ctx, yc_lat, gates, wa, wb, wc, wo, ffn_gain, w_rt, b_rt, layer):
    assert MERGE_TM == RT_TM
    row = lambda i: (i, 0)
    const = lambda shape: pl.BlockSpec(shape, lambda i: (0,) * len(shape))
    tri = jnp.asarray(np.triu(np.ones((RT_TM, RT_TM), np.float32)), BF16)
    hbm = pl.BlockSpec(memory_space=pl.ANY)
    x_specs = ([pl.BlockSpec((MERGE_TM, D_MODEL), row)] if len(xs) == 1
               else _split_specs((MERGE_TM, D_MODEL), MERGE_CTX_TILES))
    return pl.pallas_call(
        functools.partial(_merge_kernel, layer, len(xs)),
        grid=(N_TOK // MERGE_TM,),
        in_specs=x_specs + [
            _mod_spec(layer, MERGE_TM),
            pl.BlockSpec((MERGE_TM, SGU_WIDTH), row),
            *_split_specs((MERGE_TM, LRU_WIDTH), MERGE_CTX_TILES),
            *_split_specs((MERGE_TM, NA_WIDTH), MERGE_CTX_TILES),
            pl.BlockSpec((MERGE_TM, 3 * D_MODEL), row),
            hbm, hbm, hbm, hbm,
            _layer_spec((1, D_MODEL), layer),
            _layer_spec((RT_ROWS, D_MODEL), layer),
            _layer_spec((RT_ROWS, 1), layer),
            const((RT_TM, RT_TM)),
        ],
        out_specs=[
            pl.BlockSpec((MERGE_TM, D_MODEL), row),
            pl.BlockSpec((RT_TM, HX_W), row),
            pl.BlockSpec((1, RT_TM), lambda i: (0, i)),
            pl.BlockSpec((1, RT_TM), lambda i: (0, i)),
            const((CNT_ROWS, COMB_W)),
        ],
        out_shape=[
            jax.ShapeDtypeStruct((N_TOK, D_MODEL), F32),
            jax.ShapeDtypeStruct((N_TOK, HX_W), F32),
            jax.ShapeDtypeStruct((1, N_TOK), jnp.int32),
            jax.ShapeDtypeStruct((1, N_TOK), jnp.int32),
            jax.ShapeDtypeStruct((CNT_ROWS, COMB_W), jnp.int32),
        ],
        scratch_shapes=[pltpu.VMEM((3, BRANCH_W, D_MODEL), BF16), pltpu.VMEM((D_MODEL, D_MODEL), BF16),
                        pltpu.VMEM((2, BRANCH_W, D_MODEL), F32), pltpu.SemaphoreType.DMA((2,)),
                        pltpu.VMEM((CNT_ROWS, COMB_W), F32), pltpu.VMEM((COMB_W, RT_TM), F32)],
        compiler_params=_params(1),
        name="merge_branches",
    )(*xs, mod3, ya, yb_ctx, yb_lat, yc_ctx, yc_lat, gates, wa, wb, wc, wo, ffn_gain, w_rt, b_rt, tri)


RT_TM = 512
RT_ROWS = 32
RT_EXPERT_ROW = 8
EXPERT_PAIRS = ((0, 1), (0, 2), (0, 3), (1, 2), (1, 3), (2, 3))
N_PAIRS = len(EXPERT_PAIRS)
N_BUCKETS = N_GROUPS * N_PAIRS
CNT_ROWS = 32
COMB_W = 128
HX_W = D_MODEL + COMB_W
EXP_TM = 256
EXP_TILES = N_TOK // EXP_TM + N_BUCKETS
HS_ROWS = EXP_TILES * EXP_TM


def _split_bf16(x):
    hi = x.astype(BF16)
    return hi, (x - hi.astype(F32)).astype(BF16)


def _route(x, mod_ref, gain_ref, w_ref, b_ref, tri_ref, hx_ref, bucket_ref, rank_ref, cnt_out_ref,
           cnt_ref, cpad_ref):
    @pl.when(pl.program_id(0) == 0)
    def _():
        cnt_ref[...] = jnp.zeros_like(cnt_ref)
        cpad_ref[...] = jnp.zeros_like(cpad_ref)

    m = mod_ref[...]
    shift, scale = m[:, 3 * D_MODEL:4 * D_MODEL], m[:, 4 * D_MODEL:5 * D_MODEL]
    h = _rms(x, gain_ref[...]) * (1.0 + scale) + shift
    hx_ref[:, 0:D_MODEL] = h
    h_hi, h_lo = _split_bf16(h)
    w_hi, w_lo = _split_bf16(w_ref[...])
    dims = (((1,), (1,)), ((), ()))
    logits = (lax.dot_general(w_hi, h_hi, dims, preferred_element_type=F32)
              + lax.dot_general(w_hi, h_lo, dims, preferred_element_type=F32)
              + lax.dot_general(w_lo, h_hi, dims, preferred_element_type=F32)) + b_ref[...]
    gl = [logits[g:g + 1, :] for g in range(N_GROUPS)]
    gmax = functools.reduce(jnp.maximum, gl)
    gid = jnp.full(gmax.shape, N_GROUPS - 1, jnp.int32)
    for g in reversed(range(N_GROUPS - 1)):
        gid = jnp.where(gl[g] == gmax, g, gid)
    p_grp = 1.0 / functools.reduce(jnp.add, [jnp.exp(v - gmax) for v in gl])
    el = []
    for e in range(EXPERTS_PER_GROUP):
        v = logits[RT_EXPERT_ROW + e:RT_EXPERT_ROW + e + 1, :]
        for g in range(1, N_GROUPS):
            row = RT_EXPERT_ROW + g * EXPERTS_PER_GROUP + e
            v = jnp.where(gid == g, logits[row:row + 1, :], v)
        el.append(v)
    top1 = functools.reduce(jnp.maximum, el)
    idx1 = jnp.full(top1.shape, EXPERTS_PER_GROUP - 1, jnp.int32)
    for e in reversed(range(EXPERTS_PER_GROUP - 1)):
        idx1 = jnp.where(el[e] == top1, e, idx1)
    rest = [jnp.where(idx1 == e, -jnp.inf, el[e]) for e in range(EXPERTS_PER_GROUP)]
    top2 = functools.reduce(jnp.maximum, rest)
    idx2 = jnp.full(top1.shape, EXPERTS_PER_GROUP - 1, jnp.int32)
    for e in reversed(range(EXPERTS_PER_GROUP - 1)):
        idx2 = jnp.where(rest[e] == top2, e, idx2)
    e2 = jnp.exp(top2 - top1)
    w1 = p_grp / (1.0 + e2)
    w2 = p_grp * e2 / (1.0 + e2)
    for e in range(EXPERTS_PER_GROUP):
        cpad_ref[e:e + 1, :] = jnp.where(idx1 == e, w1, 0.0) + jnp.where(idx2 == e, w2, 0.0)
    hx_ref[:, D_MODEL:] = cpad_ref[...].T
    lo = jnp.minimum(idx1, idx2)
    hi = jnp.maximum(idx1, idx2)
    pair = jnp.where(lo == 0, 0, jnp.where(lo == 1, 3, 5)) + (hi - lo - 1)
    bucket = gid * N_PAIRS + pair
    bucket_ref[...] = bucket
    sub = lax.broadcasted_iota(jnp.int32, (CNT_ROWS, RT_TM), 0)
    onehot = jnp.where(sub == bucket, 1.0, 0.0)
    seen = jnp.dot(onehot.astype(BF16), tri_ref[...], preferred_element_type=F32)
    cnt = cnt_ref[...]
    rank_ref[...] = jnp.sum(onehot * (seen - 1.0 + cnt[:, 0:1]), axis=0, keepdims=True).astype(jnp.int32)
    cnt = cnt + jnp.sum(onehot, axis=1, keepdims=True)
    cnt_ref[...] = cnt
    cnt_out_ref[...] = cnt.astype(jnp.int32)


def _router_weights(w_grp, b_grp, w_exp, b_exp):
    pad = lambda v, n: jnp.zeros(v.shape[:1] + (n,) + v.shape[2:], F32)
    gap, tail = RT_EXPERT_ROW - N_GROUPS, RT_ROWS - RT_EXPERT_ROW - N_EXPERTS
    w_grp, w_exp = jnp.swapaxes(w_grp, 1, 2), jnp.swapaxes(w_exp, 1, 2)
    w = jnp.concatenate([w_grp, pad(w_grp, gap), w_exp, pad(w_exp, tail)], axis=1)
    b_grp, b_exp = b_grp[:, :, None], b_exp[:, :, None]
    b = jnp.concatenate([b_grp, pad(b_grp, gap), b_exp, pad(b_exp, tail)], axis=1)
    return w, b


DISP_TM = 512


ROW_GROUP = 64


def _for_each_row(n_rows, fn):
    def group(k, carry):
        g0 = pl.multiple_of(k * ROW_GROUP, ROW_GROUP)
        for u in range(ROW_GROUP):
            fn(g0, u)
        return carry

    lax.fori_loop(0, n_rows // ROW_GROUP, group, 0)


def _dispatch_kernel(pos_ref, start_ref, cnt_ref, nt_ref, hx_hbm, hs_ref, buf_ref, zero_ref, blk_sem, row_sem, zsem):
    i = pl.program_id(0)
    last = pl.num_programs(0) - 1
    slot = i % 2
    base = i * DISP_TM

    def tile_in(tile, s):
        return pltpu.make_async_copy(hx_hbm.at[pl.ds(pl.multiple_of(tile * DISP_TM, DISP_TM), DISP_TM), :],
                                     buf_ref.at[s], blk_sem.at[s])

    def wait_rows(s):
        pltpu.make_async_copy(buf_ref.at[s], hs_ref.at[pl.ds(0, DISP_TM), :], row_sem.at[s]).wait()

    @pl.when(i == 0)
    def _():
        zero_ref[...] = jnp.zeros_like(zero_ref)

        def tile_copy(t):
            return pltpu.make_async_copy(zero_ref, hs_ref.at[pl.ds(pl.multiple_of(t * EXP_TM, EXP_TM), EXP_TM), :],
                                         zsem)

        def last_tile(g):
            return (start_ref[g] + cnt_ref[g, 0] - 1) // EXP_TM

        def each_tile(fn):
            for g in range(N_BUCKETS):
                pl.when(cnt_ref[g, 0] > 0)(functools.partial(fn, last_tile(g)))
            lax.fori_loop(nt_ref[0], EXP_TILES, lambda t, carry: (fn(t), carry)[1], 0)

        each_tile(lambda t: tile_copy(t).start())
        each_tile(lambda t: tile_copy(t).wait())
        tile_in(0, 0).start()

    @pl.when(i > 0)
    def _():
        wait_rows(1 - slot)

    @pl.when(i < last)
    def _():
        tile_in(i + 1, 1 - slot).start()

    tile_in(i, slot).wait()

    def issue(g0, u):
        src = buf_ref.at[slot, pl.ds(g0, ROW_GROUP), :]
        pltpu.make_async_copy(src.at[pl.ds(u, 1), :], hs_ref.at[pl.ds(pos_ref[base + g0 + u], 1), :],
                              row_sem.at[slot]).start(priority=u % 2)

    _for_each_row(DISP_TM, issue)

    @pl.when(i == last)
    def _():
        wait_rows(slot)


def dispatch(pos, starts, counts, n_tiles, hx):
    return pl.pallas_call(
        _dispatch_kernel,
        grid_spec=pltpu.PrefetchScalarGridSpec(
            num_scalar_prefetch=4,
            grid=(N_TOK // DISP_TM,),
            in_specs=[pl.BlockSpec(memory_space=pl.ANY)],
            out_specs=pl.BlockSpec(memory_space=pl.ANY),
            scratch_shapes=[pltpu.VMEM((2, DISP_TM, HX_W), F32), pltpu.VMEM((EXP_TM, HX_W), F32),
                            pltpu.SemaphoreType.DMA((2,)), pltpu.SemaphoreType.DMA((2,)),
                            pltpu.SemaphoreType.DMA(())],
        ),
        out_shape=jax.ShapeDtypeStruct((HS_ROWS, HX_W), F32),
        compiler_params=_params(1),
        name="dispatch",
    )(pos, starts, counts, n_tiles, hx)


(T_GROUP, T_LO, T_HI, T_SLOT, T_RUN_POS, T_NEXT_GROUP, T_FIRST, T_PREFETCHED,
 T_PENDING_GROUP, T_PENDING_EXPERT, T_PENDING_SLOT) = range(11)


def _experts_kernel(layer, tab_ref, nt_ref, hs_ref, w1_hbm, w3_hbm, w2_hbm, ys_ref,
                    w1_ref, w3_ref, w2_ref, st1_ref, st3_ref, st2_ref, sem_ref):
    t = pl.program_id(0)
    valid = t < nt_ref[0]
    group = tab_ref[T_GROUP, t]
    slot = tab_ref[T_SLOT, t]
    run_pos = tab_ref[T_RUN_POS, t]
    next_group = tab_ref[T_NEXT_GROUP, t]
    pending_expert = tab_ref[T_PENDING_EXPERT, t]

    def expert_copies(grp, e):
        idx = grp * EXPERTS_PER_GROUP + e
        return (pltpu.make_async_copy(w1_hbm.at[layer, idx], st1_ref, sem_ref.at[0]),
                pltpu.make_async_copy(w3_hbm.at[layer, idx], st3_ref, sem_ref.at[1]),
                pltpu.make_async_copy(w2_hbm.at[layer, idx], st2_ref, sem_ref.at[2]))

    def finish(copies, dst_slot, e):
        for cp in copies:
            cp.wait()
        w1_ref[dst_slot, e] = st1_ref[...].astype(BF16)
        w3_ref[dst_slot, e] = st3_ref[...].astype(BF16)
        w2_ref[dst_slot, e] = st2_ref[...].astype(BF16)

    @pl.when(jnp.logical_and(valid, pending_expert >= 0))
    def _():
        finish(expert_copies(tab_ref[T_PENDING_GROUP, t], pending_expert), tab_ref[T_PENDING_SLOT, t], pending_expert)

    @pl.when(jnp.logical_and(valid, tab_ref[T_FIRST, t] == 1))
    def _():
        def load(e, carry):
            copies = expert_copies(group, e)
            for cp in copies:
                cp.start()
            finish(copies, slot, e)
            return carry

        lax.fori_loop(tab_ref[T_PREFETCHED, t], EXPERTS_PER_GROUP, load, 0)

    prefetch = jnp.logical_and(valid, jnp.logical_and(next_group >= 0, run_pos < EXPERTS_PER_GROUP))

    @pl.when(prefetch)
    def _():
        for cp in expert_copies(next_group, run_pos):
            cp.start()

    @pl.when(valid)
    def _():
        h = hs_ref[:, 0:D_MODEL].astype(BF16)
        c = hs_ref[:, D_MODEL:]
        lane = lax.broadcasted_iota(jnp.int32, c.shape, 1)
        acc = None
        for e in (tab_ref[T_LO, t], tab_ref[T_HI, t]):
            ce = jnp.sum(jnp.where(lane == e, c, 0.0), axis=1, keepdims=True)
            a = jnp.dot(h, w1_ref[slot, e], preferred_element_type=F32)
            b = jnp.dot(h, w3_ref[slot, e], preferred_element_type=F32)
            hid = (a * _sigmoid(a)) * b * ce
            y = jnp.dot(hid.astype(BF16), w2_ref[slot, e], preferred_element_type=F32)
            acc = y if acc is None else acc + y
        ys_ref[...] = acc

    @pl.when(jnp.logical_not(valid))
    def _():
        ys_ref[...] = jnp.zeros_like(ys_ref)


def experts(tile_table, n_tiles, hs, w1, w3, w2, layer):
    hbm = pl.BlockSpec(memory_space=pl.ANY)
    return pl.pallas_call(
        functools.partial(_experts_kernel, layer),
        grid_spec=pltpu.PrefetchScalarGridSpec(
            num_scalar_prefetch=2,
            grid=(EXP_TILES,),
            in_specs=[pl.BlockSpec((EXP_TM, HX_W), lambda t, *_: (t, 0)), hbm, hbm, hbm],
            out_specs=pl.BlockSpec((EXP_TM, D_MODEL), lambda t, *_: (t, 0)),
            scratch_shapes=[
                pltpu.VMEM((2, EXPERTS_PER_GROUP, D_MODEL, D_EXPERT), BF16),
                pltpu.VMEM((2, EXPERTS_PER_GROUP, D_MODEL, D_EXPERT), BF16),
                pltpu.VMEM((2, EXPERTS_PER_GROUP, D_EXPERT, D_MODEL), BF16),
                pltpu.VMEM((D_MODEL, D_EXPERT), F32),
                pltpu.VMEM((D_MODEL, D_EXPERT), F32),
                pltpu.VMEM((D_EXPERT, D_MODEL), F32),
                pltpu.SemaphoreType.DMA((3,)),
            ],
        ),
        out_shape=jax.ShapeDtypeStruct((HS_ROWS, D_MODEL), F32),
        compiler_params=_params(1),
        name="experts",
    )(tile_table, n_tiles, hs, w1, w3, w2)


N_TABLE_ROWS = T_PENDING_SLOT + 1
EXP_TM_LOG2 = EXP_TM.bit_length() - 1


def _schedule_kernel(cnt_ref, starts_ref, tab_ref, nt_ref):
    i32 = jnp.int32
    t = i32(0)
    for b in range(N_BUCKETS):
        n = lax.shift_right_logical(cnt_ref[b, 0] + (EXP_TM - 1), EXP_TM_LOG2)
        starts_ref[b] = t * EXP_TM

        def fill(j, carry, t=t, group=b // N_PAIRS, pair=EXPERT_PAIRS[b % N_PAIRS]):
            tab_ref[T_GROUP, t + j] = group
            tab_ref[T_LO, t + j] = pair[0]
            tab_ref[T_HI, t + j] = pair[1]
            return carry

        lax.fori_loop(0, n, fill, 0)
        t = t + n
    n_tiles = t
    nt_ref[0] = n_tiles

    def unused(k, carry):
        for row, value in ((T_GROUP, N_GROUPS - 1), (T_LO, 0), (T_HI, 1), (T_SLOT, 0), (T_RUN_POS, 0),
                           (T_NEXT_GROUP, -1), (T_FIRST, 0), (T_PREFETCHED, 0), (T_PENDING_GROUP, 0),
                           (T_PENDING_EXPERT, -1), (T_PENDING_SLOT, 0)):
            tab_ref[row, k] = value
        return carry

    lax.fori_loop(n_tiles, EXP_TILES, unused, 0)

    def back(k, carry):
        next_group, group_after = carry
        tile = n_tiles - 1 - k
        group = tab_ref[T_GROUP, tile]
        next_group = jnp.where(k == 0, -1, jnp.where(group != group_after, group_after, next_group))
        tab_ref[T_NEXT_GROUP, tile] = next_group
        return next_group, group

    lax.fori_loop(0, n_tiles, back, (i32(-1), i32(-1)))

    def forward(tile, carry):
        prev_group, run, run_start, prev_len, prev_fetch, prev_pos, prev_next, prev_slot = carry
        group = tab_ref[T_GROUP, tile]
        first = jnp.logical_or(tile == 0, group != prev_group)
        prev_len = jnp.where(first, tile - run_start, prev_len)
        run = jnp.where(first, run + 1, run)
        run_start = jnp.where(first, tile, run_start)
        slot = lax.rem(run, 2)
        run_pos = tile - run_start
        next_group = tab_ref[T_NEXT_GROUP, tile]
        tab_ref[T_FIRST, tile] = first.astype(i32)
        tab_ref[T_SLOT, tile] = slot
        tab_ref[T_RUN_POS, tile] = run_pos
        tab_ref[T_PREFETCHED, tile] = jnp.minimum(prev_len, EXPERTS_PER_GROUP)
        tab_ref[T_PENDING_EXPERT, tile] = jnp.where(prev_fetch == 1, prev_pos, -1)
        tab_ref[T_PENDING_GROUP, tile] = prev_next
        tab_ref[T_PENDING_SLOT, tile] = 1 - prev_slot
        fetch = jnp.logical_and(next_group >= 0, run_pos < EXPERTS_PER_GROUP).astype(i32)
        return group, run, run_start, prev_len, fetch, run_pos, next_group, slot

    lax.fori_loop(0, n_tiles, forward, (i32(-1), i32(-1), i32(0), i32(0), i32(0), i32(0), i32(0), i32(0)))


def expert_schedule(counts):
    smem = pl.BlockSpec(memory_space=pltpu.SMEM)
    return pl.pallas_call(
        _schedule_kernel,
        in_specs=[smem],
        out_specs=[smem, smem, smem],
        out_shape=[jax.ShapeDtypeStruct((N_BUCKETS,), jnp.int32),
                   jax.ShapeDtypeStruct((N_TABLE_ROWS, EXP_TILES), jnp.int32),
                   jax.ShapeDtypeStruct((1,), jnp.int32)],
        name="expert_schedule",
    )(counts)


RES_TM = 512
RES_CTX_TILES = N_CTX // RES_TM


def _moe_residual_kernel(final, pos_ref, x_ref, mod_ref, gain_ref, ys_ref, *refs):
    out_refs, (ybuf_ref, sem_ref) = refs[:-2], refs[-2:]
    i = pl.program_id(0)
    slot = i % 2

    def gather_tile(tile, tile_slot):
        def issue(g0, u):
            dst = ybuf_ref.at[tile_slot, pl.ds(g0, ROW_GROUP), :]
            pltpu.make_async_copy(ys_ref.at[pl.ds(pos_ref[tile * RES_TM + g0 + u], 1), :], dst.at[pl.ds(u, 1), :],
                                  sem_ref.at[tile_slot]).start(priority=u % 2)

        _for_each_row(RES_TM, issue)

    @pl.when(i == 0)
    def _():
        gather_tile(0, 0)

    @pl.when(i + 1 < pl.num_programs(0))
    def _():
        gather_tile(i + 1, 1 - slot)

    pltpu.make_async_copy(ys_ref.at[pl.ds(0, RES_TM), :], ybuf_ref.at[slot], sem_ref.at[slot]).wait()
    x = x_ref[...] + mod_ref[:, 5 * D_MODEL:6 * D_MODEL] * ybuf_ref[slot]
    if not final:
        out_refs[0][...] = x
        return
    y = _rms(x, gain_ref[...])

    @pl.when(i < RES_CTX_TILES)
    def _():
        out_refs[0][...] = y

    @pl.when(i >= RES_CTX_TILES)
    def _():
        out_refs[1][...] = y


def moe_residual(pos, x, mod4, gain, ys, layer, final):
    if final:
        out_specs = _split_specs((RES_TM, D_MODEL), RES_CTX_TILES)
        out_shape = [jax.ShapeDtypeStruct((N_CTX, D_MODEL), F32), jax.ShapeDtypeStruct((N_LAT, D_MODEL), F32)]
    else:
        out_specs = pl.BlockSpec((RES_TM, D_MODEL), lambda i, *_: (i, 0))
        out_shape = jax.ShapeDtypeStruct((N_TOK, D_MODEL), F32)
    return pl.pallas_call(
        functools.partial(_moe_residual_kernel, final),
        grid_spec=pltpu.PrefetchScalarGridSpec(
            num_scalar_prefetch=1,
            grid=(N_TOK // RES_TM,),
            in_specs=[
                pl.BlockSpec((RES_TM, D_MODEL), lambda i, *_: (i, 0)),
                _mod_spec(layer, RES_TM),
                pl.BlockSpec((1, D_MODEL), lambda i, *_: (0, 0)),
                pl.BlockSpec(memory_space=pl.ANY),
            ],
            out_specs=out_specs,
            scratch_shapes=[pltpu.VMEM((2, RES_TM, D_MODEL), F32), pltpu.SemaphoreType.DMA((2,))],
        ),
        out_shape=out_shape,
        compiler_params=_params(1),
        name="moe_residual_final" if final else "moe_residual",
    )(pos, x, mod4, gain, ys)


def kernel(x_prompt, x_sample, cache_k, cache_v, state_lru, c, c_ctx, w_mod, b_mod, norm_mix, norm_ffn, w_in, sgu_norm, sgu_w, sgu_b, lru_conv_w, lru_conv_b, lru_w_r, lru_b_r, lru_w_i, lru_b_i, lru_lambda, na_rpb, w_branch_sgu, w_branch_lru, w_branch_na, w_out, moe_w_group, moe_b_group, moe_w_expert, moe_b_expert, moe_w1, moe_w3, moe_w2, final_norm_gain):
    xs = (x_prompt.reshape(N_CTX, D_MODEL), x_sample.reshape(N_LAT, D_MODEL))
    cond = jnp.zeros((N_COND, D_MODEL), F32).at[0].set(c_ctx).at[1:1 + DEC_BATCH].set(c)
    mod = modulation(cond, w_mod, b_mod)
    zero_state = jnp.zeros((BATCH, 2, LRU_WIDTH), F32)
    cache_k = cache_k.reshape(DEC_BATCH, DEPTH, PAST_LEN, NA_WIDTH)
    cache_v = cache_v.reshape(DEC_BATCH, DEPTH, PAST_LEN, NA_WIDTH)
    final_gain = final_norm_gain.reshape(1, D_MODEL)
    norm_mix = norm_mix.reshape(DEPTH, 1, D_MODEL)
    norm_ffn = norm_ffn.reshape(DEPTH, 1, D_MODEL)
    sgu_norm = sgu_norm.reshape(DEPTH, 1, SGU_WIDTH)
    sgu_b_t = jnp.swapaxes(sgu_b, 1, 2)
    lru_conv_b = lru_conv_b.reshape(DEPTH, 1, LRU_WIDTH)
    w_gate = _block_diag_gate_weights(lru_w_r, lru_w_i)
    rpb = na_rpb.reshape(DEPTH, NA_HEADS * N_DROW * N_DCOL)
    w_rt, b_rt = _router_weights(moe_w_group, moe_b_group, moe_w_expert, moe_b_expert)
    caches, ss = [], []
    for l in range(DEPTH):
        ya, za, q, kb, vb, k_ctx, v_ctx, gates = in_projection(
            xs, mod, norm_mix, w_in, sgu_norm, sgu_w, sgu_b_t, l, prev_caches=caches if l == DEPTH - 1 else ())
        caches.append((k_ctx, v_ctx))
        lru_args = (lru_conv_w, lru_conv_b, w_gate, lru_b_r, lru_b_i, lru_lambda)
        yb_ctx, st_ctx = rglru(za, *lru_args, zero_state, pl.BlockSpec((None, 2, LRU_WIDTH), lambda i: (i, 0, 0)),
                               SEQ, BATCH, 0, l)
        yb_lat, _ = rglru(za, *lru_args, state_lru,
                          pl.BlockSpec((None, None, 2, LRU_WIDTH), lambda i, l=l: (i, l, 0, 0)),
                          DEC_SEQ, DEC_BATCH, N_CTX, l)
        yc_ctx = context_attention(q, kb, vb)
        yc_lat = latent_attention(q, kb, vb, cache_k, cache_v, rpb, l)
        x, hx, bucket, rank, counts = merge_branches(
            xs, mod, ya, yb_ctx, yb_lat, yc_ctx, yc_lat, gates, w_branch_sgu, w_branch_lru, w_branch_na, w_out,
            norm_ffn, w_rt, b_rt, l)
        bucket, rank = bucket.reshape(N_TOK), rank.reshape(N_TOK)
        starts, tile_table, n_tiles = expert_schedule(counts)
        pos = rank + jnp.sum(jnp.where(bucket[:, None] == jnp.arange(N_BUCKETS), starts[None, :], 0), axis=1)
        hs = dispatch(pos, starts, counts, n_tiles, hx)
        ys = experts(tile_table, n_tiles, hs, moe_w1, moe_w3, moe_w2, l)
        if l < DEPTH - 1:
            xs = (moe_residual(pos, x, mod, final_gain, ys, l, False),)
        else:
            y_ctx, y_lat = moe_residual(pos, x, mod, final_gain, ys, l, True)
        ss.append(st_ctx)
    cache_shape = (BATCH, DEPTH, SEQ, NA_HEADS, NA_HEAD_DIM)
    return (y_ctx.reshape(BATCH, SEQ, D_MODEL), y_lat.reshape(DEC_BATCH, DEC_SEQ, D_MODEL),
            k_ctx.reshape(cache_shape), v_ctx.reshape(cache_shape), jnp.stack(ss, axis=1))
```

```python
import functools

import jax
import jax.numpy as jnp
import numpy as np
from jax import lax
from jax.experimental import pallas as pl
from jax.experimental.pallas import tpu as pltpu

F32 = jnp.float32
BF16 = jnp.bfloat16

D_MODEL = 1024
BATCH = 16
SEQ = 256
DEPTH = 2
DEC_BATCH = 4
DEC_SEQ = 2048
PAST_LEN = 512
GRID_W = 64
CHUNK = 128
SGU_WIDTH = 512
SGU_GROUPS = 4
LRU_WIDTH = 512
LRU_BLOCKS = 8
CONV_WIDTH = 4
LRU_C = 8.0
NA_HEADS = 8
NA_HEAD_DIM = 64
NA_WIDTH = 512
WIN_ROWS = 8
WIN_COLS = 16
N_GROUPS = 4
EXPERTS_PER_GROUP = 4
N_EXPERTS = 16
D_EXPERT = 512
IN_WIDTH = 6656
EPS = 1e-6
NEG_INF = -1e30

N_CTX = BATCH * SEQ
N_LAT = DEC_BATCH * DEC_SEQ
N_TOK = N_CTX + N_LAT
N_COND = 8
MOD_WIDTH = 6 * D_MODEL
GRID_ROWS = DEC_SEQ // GRID_W

VMEM_LIMIT_BYTES = 56 * 1024 * 1024


def _params(n_axes):
    return pltpu.CompilerParams(dimension_semantics=("arbitrary",) * n_axes,
                                vmem_limit_bytes=VMEM_LIMIT_BYTES)


def _cond_row(tile, tile_rows):
    tok = tile * tile_rows
    return jnp.where(tok < N_CTX, 0, 1 + (tok - N_CTX) // DEC_SEQ)


def _layer_spec(shape, layer):
    return pl.BlockSpec((None,) + tuple(shape), lambda i, *_: (layer,) + (0,) * len(shape))


def _mod_spec(layer, tile_rows):
    return pl.BlockSpec((None, None, 1, MOD_WIDTH), lambda i, *_: (layer, _cond_row(i, tile_rows), 0, 0))


def _rms(x, gain):
    return x * lax.rsqrt(jnp.mean(x * x, axis=-1, keepdims=True) + EPS) * gain


def _bdot(a, b):
    return jnp.dot(a.astype(BF16), b.astype(BF16), preferred_element_type=F32)


def _bdot_t(a, b):
    return lax.dot_general(a.astype(BF16), b.astype(BF16), (((1,), (1,)), ((), ())),
                           preferred_element_type=F32)


MOD_TN = 1536


def _mod_kernel(cond_ref, w_ref, b_ref, o_ref):
    c = cond_ref[...]
    s = c * jax.nn.sigmoid(c)
    mod = _bdot(s, w_ref[...]) + b_ref[...]
    for r in range(N_COND):
        o_ref[r] = mod[r:r + 1, :]


def modulation(cond, w_mod, b_mod):
    return pl.pallas_call(
        _mod_kernel,
        grid=(DEPTH, MOD_WIDTH // MOD_TN),
        in_specs=[
            pl.BlockSpec((N_COND, D_MODEL), lambda l, j: (0, 0)),
            pl.BlockSpec((None, D_MODEL, MOD_TN), lambda l, j: (l, 0, j)),
            pl.BlockSpec((None, 1, MOD_TN), lambda l, j: (l, 0, j)),
        ],
        out_specs=pl.BlockSpec((None, N_COND, 1, MOD_TN), lambda l, j: (l, 0, 0, j)),
        out_shape=jax.ShapeDtypeStruct((DEPTH, N_COND, 1, MOD_WIDTH), F32),
        compiler_params=_params(2),
        name="modulation",
    )(cond, w_mod, b_mod.reshape(DEPTH, 1, MOD_WIDTH))


IN_TM = 256
IN_CTX_TILES = N_CTX // IN_TM
ZA_WIDTH = 4 * 512
KV_OFF = ZA_WIDTH + NA_WIDTH
GATE_OFF = KV_OFF + 2 * NA_WIDTH
W_CHUNK = 512
Q_SCALE = NA_HEAD_DIM ** -0.5 * float(np.log2(np.e))


def _load_cast(chunks, stage_ref, sem_ref):
    def copy(j):
        return pltpu.make_async_copy(chunks[j][0], stage_ref.at[j % 2], sem_ref.at[j % 2])

    copy(0).start()
    for j in range(len(chunks)):
        if j + 1 < len(chunks):
            copy(j + 1).start()
        copy(j).wait()
        chunks[j][1](stage_ref[j % 2].astype(BF16))


def _token_tile(i, n_ctx_tiles, refs):
    if len(refs) == 1:
        return refs[0][...]
    return jnp.where(i < n_ctx_tiles, refs[0][...], refs[1][...])


def _split_specs(block, n_ctx_tiles):
    return [pl.BlockSpec(block, lambda i, *_: (jnp.minimum(i, n_ctx_tiles - 1), 0)),
            pl.BlockSpec(block, lambda i, *_: (jnp.maximum(i - n_ctx_tiles, 0), 0))]


SGU_GD = SGU_WIDTH // SGU_GROUPS


def _spatial_gating(u, v, gain, ws_ref, bs_ref):
    u = jax.nn.gelu(u)
    v = _rms(jax.nn.gelu(v), gain).astype(BF16)
    out = []
    for g in range(SGU_GROUPS):
        cols = slice(g * SGU_GD, (g + 1) * SGU_GD)
        mixed = jnp.dot(ws_ref[g].astype(BF16), v[:, cols], preferred_element_type=F32) + bs_ref[:, g:g + 1]
        out.append(u[:, cols] * mixed)
    return jnp.concatenate(out, axis=1)


def _inproj_kernel(layer, n_x, n_prev, *refs):
    x_refs, (mod_ref, gain_ref, w_hbm, sgu_gain_ref, ws_ref, bs_ref) = refs[:n_x], refs[n_x:n_x + 6]
    prev_refs, rest = refs[n_x + 6:n_x + 6 + 2 * n_prev], refs[n_x + 6 + 2 * n_prev:]
    ya_ref, zb_ref, q_ref, kb_ref, vb_ref, kc_ref, vc_ref, g_ref, w_ref, stage_ref, sem_ref = rest
    i = pl.program_id(0)

    @pl.when(i == 0)
    def _():
        def store(c):
            def st(v):
                w_ref[:, c * W_CHUNK:(c + 1) * W_CHUNK] = v
            return st

        _load_cast([(w_hbm.at[layer, :, pl.ds(c * W_CHUNK, W_CHUNK)], store(c)) for c in range(IN_WIDTH // W_CHUNK)],
                   stage_ref, sem_ref)

    m = mod_ref[...]
    shift, scale = m[:, 0:D_MODEL], m[:, D_MODEL:2 * D_MODEL]
    h = (_rms(_token_tile(i, IN_CTX_TILES, x_refs), gain_ref[...]) * (1.0 + scale) + shift).astype(BF16)
    za = jnp.dot(h, w_ref[:, 0:ZA_WIDTH], preferred_element_type=F32)
    for c in range(IN_TM // CHUNK):
        rows = slice(c * CHUNK, (c + 1) * CHUNK)
        ya_ref[rows, :] = _spatial_gating(za[rows, 0:SGU_WIDTH], za[rows, SGU_WIDTH:2 * SGU_WIDTH],
                                          sgu_gain_ref[...], ws_ref, bs_ref).astype(BF16)
    zb_ref[:, 0:LRU_WIDTH] = za[:, 2 * SGU_WIDTH:2 * SGU_WIDTH + LRU_WIDTH].astype(BF16)
    zb_ref[:, LRU_WIDTH:] = jax.nn.gelu(za[:, 2 * SGU_WIDTH + LRU_WIDTH:]).astype(BF16)
    q_ref[...] = (jnp.dot(h, w_ref[:, ZA_WIDTH:KV_OFF], preferred_element_type=F32) * Q_SCALE).astype(BF16)
    kv = jnp.dot(h, w_ref[:, KV_OFF:GATE_OFF], preferred_element_type=F32)
    kb_ref[...] = kv[:, 0:NA_WIDTH].astype(BF16)
    vb_ref[...] = kv[:, NA_WIDTH:].astype(BF16)
    g_ref[...] = _sigmoid(jnp.dot(h, w_ref[:, GATE_OFF:], preferred_element_type=F32)).astype(BF16)

    @pl.when(i < IN_CTX_TILES)
    def _():
        if n_prev == 0:
            kc_ref[...] = kv[:, 0:NA_WIDTH]
            vc_ref[...] = kv[:, NA_WIDTH:]
        else:
            for p in range(n_prev):
                kc_ref[p] = prev_refs[2 * p][...]
                vc_ref[p] = prev_refs[2 * p + 1][...]
            kc_ref[n_prev] = kv[:, 0:NA_WIDTH]
            vc_ref[n_prev] = kv[:, NA_WIDTH:]


def in_projection(xs, mod3, gain, w_in, sgu_gain, sgu_w, sgu_b_t, layer, prev_caches=()):
    assert IN_TM == SEQ
    row = lambda i: (i, 0)
    ctx_row = lambda i: (jnp.minimum(i, IN_CTX_TILES - 1), 0)
    x_specs = ([pl.BlockSpec((IN_TM, D_MODEL), row)] if len(xs) == 1
               else _split_specs((IN_TM, D_MODEL), IN_CTX_TILES))
    n_prev = len(prev_caches)
    if n_prev:
        cache_spec = pl.BlockSpec((None, n_prev + 1, SEQ, NA_WIDTH),
                                  lambda i: (jnp.minimum(i, IN_CTX_TILES - 1), 0, 0, 0))
        cache_shape = jax.ShapeDtypeStruct((BATCH, n_prev + 1, SEQ, NA_WIDTH), F32)
    else:
        cache_spec = pl.BlockSpec((IN_TM, NA_WIDTH), ctx_row)
        cache_shape = jax.ShapeDtypeStruct((N_CTX, NA_WIDTH), F32)
    return pl.pallas_call(
        functools.partial(_inproj_kernel, layer, len(xs), n_prev),
        grid=(N_TOK // IN_TM,),
        in_specs=x_specs + [
            _mod_spec(layer, IN_TM),
            _layer_spec((1, D_MODEL), layer),
            pl.BlockSpec(memory_space=pl.ANY),
            _layer_spec((1, SGU_WIDTH), layer),
            _layer_spec((SGU_GROUPS, CHUNK, CHUNK), layer),
            _layer_spec((CHUNK, SGU_GROUPS), layer),
        ] + [pl.BlockSpec((IN_TM, NA_WIDTH), ctx_row)] * (2 * n_prev),
        out_specs=[
            pl.BlockSpec((IN_TM, SGU_WIDTH), row),
            pl.BlockSpec((IN_TM, 2 * LRU_WIDTH), row),
            pl.BlockSpec((IN_TM, NA_WIDTH), row),
            pl.BlockSpec((IN_TM, NA_WIDTH), row),
            pl.BlockSpec((IN_TM, NA_WIDTH), row),
            cache_spec,
            cache_spec,
            pl.BlockSpec((IN_TM, 3 * D_MODEL), row),
        ],
        out_shape=[
            jax.ShapeDtypeStruct((N_TOK, SGU_WIDTH), BF16),
            jax.ShapeDtypeStruct((N_TOK, 2 * LRU_WIDTH), BF16),
            jax.ShapeDtypeStruct((N_TOK, NA_WIDTH), BF16),
            jax.ShapeDtypeStruct((N_TOK, NA_WIDTH), BF16),
            jax.ShapeDtypeStruct((N_TOK, NA_WIDTH), BF16),
            cache_shape,
            cache_shape,
            jax.ShapeDtypeStruct((N_TOK, 3 * D_MODEL), BF16),
        ],
        scratch_shapes=[pltpu.VMEM((D_MODEL, IN_WIDTH), BF16), pltpu.VMEM((2, D_MODEL, W_CHUNK), F32),
                        pltpu.SemaphoreType.DMA((2,))],
        compiler_params=_params(1),
        name="in_projection",
    )(*xs, mod3, gain, w_in, sgu_gain, sgu_w, sgu_b_t, *[c for kv in prev_caches for c in kv])


LRU_TC = 256
LRU_HALF = 256
SUB = 8
HALO = 8
LANES = 128
LRU_SLABS = LRU_WIDTH // LANES


def _sigmoid(x):
    return 0.5 * jnp.tanh(0.5 * x) + 0.5


def _lru_kernel(seq_len, xr_ref, gr_ref, cw_ref, cb_ref, wlo_ref, whi_ref, br_ref, bi_ref,
                lam_ref, h0_ref, y_ref, st_ref, xp_ref, a_ref, b_ref, h_ref):
    seg_len = seq_len // SUB
    pitch = seg_len + SUB
    segs_per_chunk = max(LRU_TC // seg_len, 1)
    n_chunks = seq_len // LRU_TC
    zeros = jnp.zeros((HALO, LRU_WIDTH), F32)
    xp_ref[0:HALO, :] = zeros
    xp_ref[seq_len + HALO:seq_len + 2 * HALO, :] = zeros

    def copy_in(c, carry):
        r0 = pl.multiple_of(c * LRU_TC, LRU_TC)
        xp_ref[pl.ds(r0 + HALO, LRU_TC), :] = xr_ref[pl.ds(r0, LRU_TC), :].astype(F32)
        return carry

    lax.fori_loop(0, n_chunks, copy_in, 0)

    def chunk_rows(c, seg):
        n = min(seg_len, LRU_TC)
        start = pl.multiple_of((c * segs_per_chunk + seg) * pitch, SUB)
        return pl.ds(start, n), slice(seg * n, (seg + 1) * n)

    cw = cw_ref[...]
    cb = cb_ref[...]
    win = LRU_TC + 2 * HALO
    neg_lam = -lam_ref[...]
    softplus = jnp.maximum(neg_lam, 0.0) + jnp.log1p(jnp.exp(-jnp.abs(neg_lam)))
    decay = (0.5 * LRU_C) * softplus
    half_br = 0.5 * br_ref[...]
    half_bi = 0.5 * bi_ref[...]

    def gates(c, carry):
        r0 = pl.multiple_of(c * LRU_TC, LRU_TC)
        w = xp_ref[pl.ds(r0, win), :]
        xc = (cw[0:1, :] * pltpu.roll(w, 1, 0)[HALO:HALO + LRU_TC]
              + cw[1:2, :] * w[HALO:HALO + LRU_TC]
              + cw[2:3, :] * pltpu.roll(w, win - 1, 0)[HALO:HALO + LRU_TC]
              + cw[3:4, :] * pltpu.roll(w, win - 2, 0)[HALO:HALO + LRU_TC]) + cb
        xb = xc.astype(BF16)
        for half, w_ref in enumerate((wlo_ref, whi_ref)):
            cols = slice(half * LRU_HALF, (half + 1) * LRU_HALF)
            pre = jnp.dot(xb[:, cols], w_ref[...], preferred_element_type=F32)
            half_x = 0.5 * xc[:, cols]
            for d in range(2):
                tr = jnp.tanh(pre[:, (2 * d) * LRU_HALF:(2 * d + 1) * LRU_HALF] + half_br[d:d + 1, cols])
                ti = jnp.tanh(pre[:, (2 * d + 1) * LRU_HALF:(2 * d + 2) * LRU_HALF] + half_bi[d:d + 1, cols])
                neg_log_a = decay[d:d + 1, cols] * tr + decay[d:d + 1, cols]
                a = jnp.exp2(neg_log_a * -LOG2E)
                z = jnp.tanh(neg_log_a) * (a * a + 1.0)
                b = jnp.where(z > 0.0, z * lax.rsqrt(z), 0.0) * (half_x * ti + half_x)
                for k in range(LRU_HALF // LANES):
                    slab = half * (LRU_HALF // LANES) + k
                    lanes = slice(k * LANES, (k + 1) * LANES)
                    for seg in range(segs_per_chunk):
                        dst, src = chunk_rows(c, seg)
                        a_ref[d, slab, dst, :] = a[src, lanes]
                        b_ref[d, slab, dst, :] = b[src, lanes]
        return carry

    lax.fori_loop(0, n_chunks, gates, 0)

    def step_rows(d, j):
        return pl.ds(j if d == 0 else seg_len - 1 - j, SUB, stride=pitch)

    def reduce_step(j, carry):
        out = []
        for d in range(2):
            for slab in range(LRU_SLABS):
                big_a, big_b = carry[d * LRU_SLABS + slab]
                a = a_ref[d, slab, step_rows(d, j), :]
                b = b_ref[d, slab, step_rows(d, j), :]
                out.append((a * big_a, a * big_b + b))
        return tuple(out)

    ident = (jnp.ones((SUB, LANES), F32), jnp.zeros((SUB, LANES), F32))
    totals = lax.fori_loop(0, seg_len, reduce_step, (ident,) * (2 * LRU_SLABS), unroll=8)

    sub = lax.broadcasted_iota(jnp.int32, (SUB, LANES), 0)
    entering = []
    for d in range(2):
        for slab in range(LRU_SLABS):
            big_a, big_b = totals[d * LRU_SLABS + slab]
            h = h0_ref[d:d + 1, slab * LANES:(slab + 1) * LANES]
            rows = jnp.zeros((SUB, LANES), F32)
            for s in (range(SUB) if d == 0 else reversed(range(SUB))):
                rows = jnp.where(sub == s, h, rows)
                h = big_a[s:s + 1, :] * h + big_b[s:s + 1, :]
            entering.append(rows)
            st_ref[d:d + 1, slab * LANES:(slab + 1) * LANES] = h

    def apply_step(j, carry):
        out = []
        for d in range(2):
            for slab in range(LRU_SLABS):
                h = (a_ref[d, slab, step_rows(d, j), :] * carry[d * LRU_SLABS + slab]
                     + b_ref[d, slab, step_rows(d, j), :])
                h_ref[d, slab, step_rows(d, j), :] = h
                out.append(h)
        return tuple(out)

    lax.fori_loop(0, seg_len, apply_step, tuple(entering), unroll=8)

    def merge(c, carry):
        r0 = pl.multiple_of(c * LRU_TC, LRU_TC)
        gate = gr_ref[pl.ds(r0, LRU_TC), :].astype(F32)
        for slab in range(LRU_SLABS):
            lanes = slice(slab * LANES, (slab + 1) * LANES)
            for seg in range(segs_per_chunk):
                src, dst = chunk_rows(c, seg)
                h = h_ref[0, slab, src, :] + h_ref[1, slab, src, :]
                y_ref[pl.ds(r0 + dst.start, dst.stop - dst.start), lanes] = (h * gate[dst, lanes]).astype(BF16)
        return carry

    lax.fori_loop(0, n_chunks, merge, 0)


def rglru(za, conv_w, conv_b, w_gate, b_r, b_i, lam, h0, h0_spec, seq_len, n_seq, tok_off, layer):
    blk0 = tok_off // seq_len
    half_spec = lambda half: pl.BlockSpec((None, None, LRU_HALF, 4 * LRU_HALF), lambda i: (layer, half, 0, 0))
    return pl.pallas_call(
        functools.partial(_lru_kernel, seq_len),
        grid=(n_seq,),
        in_specs=[
            pl.BlockSpec((seq_len, LRU_WIDTH), lambda i: (i + blk0, 0)),
            pl.BlockSpec((seq_len, LRU_WIDTH), lambda i: (i + blk0, 1)),
            _layer_spec((CONV_WIDTH, LRU_WIDTH), layer),
            _layer_spec((1, LRU_WIDTH), layer),
            half_spec(0),
            half_spec(1),
            _layer_spec((2, LRU_WIDTH), layer),
            _layer_spec((2, LRU_WIDTH), layer),
            _layer_spec((2, LRU_WIDTH), layer),
            h0_spec,
        ],
        out_specs=[
            pl.BlockSpec((seq_len, LRU_WIDTH), lambda i: (i, 0)),
            pl.BlockSpec((None, 2, LRU_WIDTH), lambda i: (i, 0, 0)),
        ],
        out_shape=[
            jax.ShapeDtypeStruct((n_seq * seq_len, LRU_WIDTH), BF16),
            jax.ShapeDtypeStruct((n_seq, 2, LRU_WIDTH), F32),
        ],
        scratch_shapes=[
            pltpu.VMEM((seq_len + 2 * HALO, LRU_WIDTH), F32),
            pltpu.VMEM((2, LRU_SLABS, seq_len + SUB * SUB, LANES), F32),
            pltpu.VMEM((2, LRU_SLABS, seq_len + SUB * SUB, LANES), F32),
            pltpu.VMEM((2, LRU_SLABS, seq_len + SUB * SUB, LANES), F32),
        ],
        compiler_params=_params(1),
        name=f"rglru_{seq_len}",
    )(za, za, conv_w, conv_b, w_gate, w_gate, b_r, b_i, lam, h0)


def _block_diag_gate_weights(w_r, w_i):
    per_half = LRU_BLOCKS // 2
    blk = LRU_WIDTH // LRU_BLOCKS
    g = jnp.stack([w_r, w_i], axis=2).reshape(DEPTH, 2, 2, 2, per_half, blk, blk)
    w = jnp.einsum("ldghaij,ab->lhaidgbj", g, jnp.eye(per_half, dtype=g.dtype))
    return (0.5 * w.reshape(DEPTH, 2, LRU_HALF, 4 * LRU_HALF)).astype(BF16)


PAIR_W = 2 * NA_HEAD_DIM
LOG2E = float(np.log2(np.e))


def _head_lanes(shape, head):
    lane = lax.broadcasted_iota(jnp.int32, shape, 1)
    return (lane < NA_HEAD_DIM) if head == 0 else (lane >= NA_HEAD_DIM)


def _one_head(q_pair, head):
    return jnp.where(_head_lanes(q_pair.shape, head), q_pair.astype(F32), 0.0).astype(BF16)


def _join_heads(o0, o1):
    return jnp.where(_head_lanes(o0.shape, 0), o0, o1)


def _ctx_attn_kernel(q_ref, k_ref, v_ref, o_ref):
    for j in range(NA_HEADS // 2):
        cols = slice(j * PAIR_W, (j + 1) * PAIR_W)
        outs = []
        for head in range(2):
            s = _bdot_t(_one_head(q_ref[:, cols], head), k_ref[:, cols])
            p = jnp.exp2(s - jnp.max(s, axis=-1, keepdims=True))
            denom = jnp.sum(p, axis=-1, keepdims=True)
            outs.append(jnp.dot(p.astype(BF16), v_ref[:, cols], preferred_element_type=F32) / denom)
        o_ref[:, cols] = _join_heads(*outs).astype(BF16)


def context_attention(q, kb, vb):
    spec = pl.BlockSpec((SEQ, NA_WIDTH), lambda i: (i, 0))
    return pl.pallas_call(
        _ctx_attn_kernel,
        grid=(BATCH,),
        in_specs=[spec, spec, spec],
        out_specs=spec,
        out_shape=jax.ShapeDtypeStruct((N_CTX, NA_WIDTH), BF16),
        compiler_params=_params(1),
        name="context_attention",
    )(q, kb, vb)


NA_QROWS = 4
NA_TQ = NA_QROWS * GRID_W
NA_KROWS = NA_QROWS + WIN_ROWS
NA_TK = NA_KROWS * GRID_W
NA_QBLOCKS = GRID_ROWS // NA_QROWS


N_DROW = 2 * WIN_ROWS - 1
N_DCOL = 2 * WIN_COLS - 1
NA_BLOCK_KINDS = (0, 1, NA_QBLOCKS - 1)


def _first_key_row(qb):
    return (np.clip if isinstance(qb, int) else jnp.clip)(qb * NA_QROWS - WIN_ROWS // 2, 0, GRID_ROWS - NA_KROWS)


def _build_bias(layer, rpb_ref, table_ref, bias_ref):
    qc = lax.broadcasted_iota(jnp.int32, (GRID_W, GRID_W), 0)
    kc = lax.broadcasted_iota(jnp.int32, (GRID_W, GRID_W), 1)
    col0 = jnp.clip(qc - WIN_COLS // 2, 0, GRID_W - WIN_COLS)
    col_ok = jnp.logical_and(kc >= col0, kc < col0 + WIN_COLS)
    d_col = jnp.clip(kc - qc, 1 - WIN_COLS, WIN_COLS - 1) + WIN_COLS - 1
    neg = jnp.full((GRID_W, GRID_W), NEG_INF, F32)

    def table_entry(idx, carry):
        t = neg
        for j in range(N_DCOL):
            t = jnp.where(d_col == j, rpb_ref[layer, idx * N_DCOL + j], t)
        table_ref[idx] = jnp.where(col_ok, t * LOG2E, neg)
        return carry

    lax.fori_loop(0, NA_HEADS * N_DROW, table_entry, 0)

    def head_blocks(h, carry):
        for kind, qb in enumerate(NA_BLOCK_KINDS):
            for i in range(NA_QROWS):
                qrow = qb * NA_QROWS + i
                win0 = int(np.clip(qrow - WIN_ROWS // 2, 0, GRID_ROWS - WIN_ROWS))
                for kr in range(NA_KROWS):
                    krow = int(_first_key_row(qb)) + kr
                    inside = win0 <= krow < win0 + WIN_ROWS
                    blk = table_ref[h * N_DROW + (krow - qrow + WIN_ROWS - 1)] if inside else neg
                    bias_ref[kind, h, i * GRID_W:(i + 1) * GRID_W, kr * GRID_W:(kr + 1) * GRID_W] = blk
        return carry

    lax.fori_loop(0, NA_HEADS, head_blocks, 0)


def _lat_attn_kernel(layer, rpb_ref, q_ref, k_ref, v_ref, ck_ref, cv_ref, o_ref, table_ref, bias_ref):
    qb = pl.program_id(1)

    @pl.when(jnp.logical_and(pl.program_id(0) == 0, qb == 0))
    def _():
        _build_bias(layer, rpb_ref, table_ref, bias_ref)

    kind = jnp.where(qb == 0, 0, jnp.where(qb == NA_QBLOCKS - 1, 2, 1))
    k0 = pl.multiple_of(_first_key_row(qb) * GRID_W, GRID_W)
    for j in range(NA_HEADS // 2):
        cols = slice(j * PAIR_W, (j + 1) * PAIR_W)
        k_loc = k_ref[pl.ds(k0, NA_TK), cols]
        v_loc = v_ref[pl.ds(k0, NA_TK), cols]
        kt_ctx = ck_ref[cols, :].astype(BF16)
        vt_ctx = cv_ref[cols, :].astype(BF16)
        outs = []
        for head in range(2):
            qh = _one_head(q_ref[:, cols], head)
            s_loc = _bdot_t(qh, k_loc) + bias_ref[kind, 2 * j + head]
            s_ctx = jnp.dot(qh, kt_ctx, preferred_element_type=F32)
            m = jnp.maximum(jnp.max(s_loc, axis=-1, keepdims=True), jnp.max(s_ctx, axis=-1, keepdims=True))
            p_loc = jnp.exp2(s_loc - m)
            p_ctx = jnp.exp2(s_ctx - m)
            denom = jnp.sum(p_loc, axis=-1, keepdims=True) + jnp.sum(p_ctx, axis=-1, keepdims=True)
            o = (jnp.dot(p_loc.astype(BF16), v_loc, preferred_element_type=F32)
                 + _bdot_t(p_ctx.astype(BF16), vt_ctx))
            outs.append(o / denom)
        o_ref[:, cols] = _join_heads(*outs).astype(BF16)


def latent_attention(q, kb, vb, cache_k, cache_v, rpb, layer):
    q_blk0 = N_CTX // NA_TQ
    s_blk0 = N_CTX // DEC_SEQ
    cache_spec = pl.BlockSpec((None, None, NA_WIDTH, PAST_LEN), lambda b, m: (b, layer, 0, 0))
    return pl.pallas_call(
        functools.partial(_lat_attn_kernel, layer),
        grid=(DEC_BATCH, NA_QBLOCKS),
        in_specs=[
            pl.BlockSpec(memory_space=pltpu.SMEM),
            pl.BlockSpec((NA_TQ, NA_WIDTH), lambda b, m: (q_blk0 + b * NA_QBLOCKS + m, 0)),
            pl.BlockSpec((DEC_SEQ, NA_WIDTH), lambda b, m: (s_blk0 + b, 0)),
            pl.BlockSpec((DEC_SEQ, NA_WIDTH), lambda b, m: (s_blk0 + b, 0)),
            cache_spec,
            cache_spec,
        ],
        out_specs=pl.BlockSpec((NA_TQ, NA_WIDTH), lambda b, m: (b * NA_QBLOCKS + m, 0)),
        out_shape=jax.ShapeDtypeStruct((N_LAT, NA_WIDTH), BF16),
        scratch_shapes=[pltpu.VMEM((NA_HEADS * N_DROW, GRID_W, GRID_W), F32),
                        pltpu.VMEM((len(NA_BLOCK_KINDS), NA_HEADS, NA_TQ, NA_TK), F32)],
        compiler_params=_params(2),
        name="latent_attention",
    )(rpb, q, kb, vb, cache_k, cache_v)


MERGE_TM = 512


MERGE_CTX_TILES = N_CTX // MERGE_TM
BRANCH_W = 512


def _merge_kernel(layer, n_x, *refs):
    x_refs, rest = refs[:n_x], refs[n_x:]
    (mod_ref, ya_ref, yb_ctx_ref, yb_lat_ref, yc_ctx_ref, yc_lat_ref, g_ref, wa_hbm, wb_hbm, wc_hbm, wo_hbm,
     ffn_gain_ref, wrt_ref, brt_ref, tri_ref,
     o_ref, hx_ref, bucket_ref, rank_ref, cnt_out_ref,
     wbr_ref, wo_ref, stage_ref, sem_ref, cnt_ref, cpad_ref) = rest
    i = pl.program_id(0)

    @pl.when(i == 0)
    def _():
        def store_branch(k):
            def st(v):
                wbr_ref[k] = v
            return st

        def store_out(k):
            def st(v):
                wo_ref[k * BRANCH_W:(k + 1) * BRANCH_W, :] = v
            return st

        chunks = [(w.at[layer], store_branch(k)) for k, w in enumerate((wa_hbm, wb_hbm, wc_hbm))]
        chunks += [(wo_hbm.at[layer, pl.ds(k * BRANCH_W, BRANCH_W), :], store_out(k))
                   for k in range(D_MODEL // BRANCH_W)]
        _load_cast(chunks, stage_ref, sem_ref)

    g = g_ref[...].astype(F32)
    yb = _token_tile(i, MERGE_CTX_TILES, (yb_ctx_ref, yb_lat_ref))
    yc = _token_tile(i, MERGE_CTX_TILES, (yc_ctx_ref, yc_lat_ref))
    merged = (g[:, 0:D_MODEL] * jnp.dot(ya_ref[...], wbr_ref[0], preferred_element_type=F32)
              + g[:, D_MODEL:2 * D_MODEL] * jnp.dot(yb, wbr_ref[1], preferred_element_type=F32)
              + g[:, 2 * D_MODEL:] * jnp.dot(yc, wbr_ref[2], preferred_element_type=F32))
    y = jnp.dot(merged.astype(BF16), wo_ref[...], preferred_element_type=F32)
    gate = mod_ref[:, 2 * D_MODEL:3 * D_MODEL]
    x = _token_tile(i, MERGE_CTX_TILES, x_refs) + gate * y
    o_ref[...] = x
    _route(x, mod_ref, ffn_gain_ref, wrt_ref, brt_ref, tri_ref, hx_ref, bucket_ref, rank_ref, cnt_out_ref,
           cnt_ref, cpad_ref)


def merge_branches(xs, mod3, ya, yb_ctx, yb_lat, yc_ctx, yc_lat, gates, wa, wb, wc, wo, ffn_gain, w_rt, b_rt, layer):
    assert MERGE_TM == RT_TM
    row = lambda i: (i, 0)
    const = lambda shape: pl.BlockSpec(shape, lambda i: (0,) * len(shape))
    tri = jnp.asarray(np.triu(np.ones((RT_TM, RT_TM), np.float32)), BF16)
    hbm = pl.BlockSpec(memory_space=pl.ANY)
    x_specs = ([pl.BlockSpec((MERGE_TM, D_MODEL), row)] if len(xs) == 1
               else _split_specs((MERGE_TM, D_MODEL), MERGE_CTX_TILES))
    return pl.pallas_call(
        functools.partial(_merge_kernel, layer, len(xs)),
        grid=(N_TOK // MERGE_TM,),
        in_specs=x_specs + [
            _mod_spec(layer, MERGE_TM),
            pl.BlockSpec((MERGE_TM, SGU_WIDTH), row),
            *_split_specs((MERGE_TM, LRU_WIDTH), MERGE_CTX_TILES),
            *_split_specs((MERGE_TM, NA_WIDTH), MERGE_CTX_TILES),
            pl.BlockSpec((MERGE_TM, 3 * D_MODEL), row),
            hbm, hbm, hbm, hbm,
            _layer_spec((1, D_MODEL), layer),
            _layer_spec((RT_ROWS, D_MODEL), layer),
            _layer_spec((RT_ROWS, 1), layer),
            const((RT_TM, RT_TM)),
        ],
        out_specs=[
            pl.BlockSpec((MERGE_TM, D_MODEL), row),
            pl.BlockSpec((RT_TM, HX_W), row),
            pl.BlockSpec((1, RT_TM), lambda i: (0, i)),
            pl.BlockSpec((1, RT_TM), lambda i: (0, i)),
            const((CNT_ROWS, COMB_W)),
        ],
        out_shape=[
            jax.ShapeDtypeStruct((N_TOK, D_MODEL), F32),
            jax.ShapeDtypeStruct((N_TOK, HX_W), F32),
            jax.ShapeDtypeStruct((1, N_TOK), jnp.int32),
            jax.ShapeDtypeStruct((1, N_TOK), jnp.int32),
            jax.ShapeDtypeStruct((CNT_ROWS, COMB_W), jnp.int32),
        ],
        scratch_shapes=[pltpu.VMEM((3, BRANCH_W, D_MODEL), BF16), pltpu.VMEM((D_MODEL, D_MODEL), BF16),
                        pltpu.VMEM((2, BRANCH_W, D_MODEL), F32), pltpu.SemaphoreType.DMA((2,)),
                        pltpu.VMEM((CNT_ROWS, COMB_W), F32), pltpu.VMEM((COMB_W, RT_TM), F32)],
        compiler_params=_params(1),
        name="merge_branches",
    )(*xs, mod3, ya, yb_ctx, yb_lat, yc_ctx, yc_lat, gates, wa, wb, wc, wo, ffn_gain, w_rt, b_rt, tri)


RT_TM = 512
RT_ROWS = 32
RT_EXPERT_ROW = 8
EXPERT_PAIRS = ((0, 1), (0, 2), (0, 3), (1, 2), (1, 3), (2, 3))
N_PAIRS = len(EXPERT_PAIRS)
N_BUCKETS = N_GROUPS * N_PAIRS
CNT_ROWS = 32
COMB_W = 128
HX_W = D_MODEL + COMB_W
EXP_TM = 256
EXP_TILES = N_TOK // EXP_TM + N_BUCKETS
HS_ROWS = EXP_TILES * EXP_TM


def _split_bf16(x):
    hi = x.astype(BF16)
    return hi, (x - hi.astype(F32)).astype(BF16)


def _route(x, mod_ref, gain_ref, w_ref, b_ref, tri_ref, hx_ref, bucket_ref, rank_ref, cnt_out_ref,
           cnt_ref, cpad_ref):
    @pl.when(pl.program_id(0) == 0)
    def _():
        cnt_ref[...] = jnp.zeros_like(cnt_ref)
        cpad_ref[...] = jnp.zeros_like(cpad_ref)

    m = mod_ref[...]
    shift, scale = m[:, 3 * D_MODEL:4 * D_MODEL], m[:, 4 * D_MODEL:5 * D_MODEL]
    h = _rms(x, gain_ref[...]) * (1.0 + scale) + shift
    hx_ref[:, 0:D_MODEL] = h
    h_hi, h_lo = _split_bf16(h)
    w_hi, w_lo = _split_bf16(w_ref[...])
    dims = (((1,), (1,)), ((), ()))
    logits = (lax.dot_general(w_hi, h_hi, dims, preferred_element_type=F32)
              + lax.dot_general(w_hi, h_lo, dims, preferred_element_type=F32)
              + lax.dot_general(w_lo, h_hi, dims, preferred_element_type=F32)) + b_ref[...]
    gl = [logits[g:g + 1, :] for g in range(N_GROUPS)]
    gmax = functools.reduce(jnp.maximum, gl)
    gid = jnp.full(gmax.shape, N_GROUPS - 1, jnp.int32)
    for g in reversed(range(N_GROUPS - 1)):
        gid = jnp.where(gl[g] == gmax, g, gid)
    p_grp = 1.0 / functools.reduce(jnp.add, [jnp.exp(v - gmax) for v in gl])
    el = []
    for e in range(EXPERTS_PER_GROUP):
        v = logits[RT_EXPERT_ROW + e:RT_EXPERT_ROW + e + 1, :]
        for g in range(1, N_GROUPS):
            row = RT_EXPERT_ROW + g * EXPERTS_PER_GROUP + e
            v = jnp.where(gid == g, logits[row:row + 1, :], v)
        el.append(v)
    top1 = functools.reduce(jnp.maximum, el)
    idx1 = jnp.full(top1.shape, EXPERTS_PER_GROUP - 1, jnp.int32)
    for e in reversed(range(EXPERTS_PER_GROUP - 1)):
        idx1 = jnp.where(el[e] == top1, e, idx1)
    rest = [jnp.where(idx1 == e, -jnp.inf, el[e]) for e in range(EXPERTS_PER_GROUP)]
    top2 = functools.reduce(jnp.maximum, rest)
    idx2 = jnp.full(top1.shape, EXPERTS_PER_GROUP - 1, jnp.int32)
    for e in reversed(range(EXPERTS_PER_GROUP - 1)):
        idx2 = jnp.where(rest[e] == top2, e, idx2)
    e2 = jnp.exp(top2 - top1)
    w1 = p_grp / (1.0 + e2)
    w2 = p_grp * e2 / (1.0 + e2)
    for e in range(EXPERTS_PER_GROUP):
        cpad_ref[e:e + 1, :] = jnp.where(idx1 == e, w1, 0.0) + jnp.where(idx2 == e, w2, 0.0)
    hx_ref[:, D_MODEL:] = cpad_ref[...].T
    lo = jnp.minimum(idx1, idx2)
    hi = jnp.maximum(idx1, idx2)
    pair = jnp.where(lo == 0, 0, jnp.where(lo == 1, 3, 5)) + (hi - lo - 1)
    bucket = gid * N_PAIRS + pair
    bucket_ref[...] = bucket
    sub = lax.broadcasted_iota(jnp.int32, (CNT_ROWS, RT_TM), 0)
    onehot = jnp.where(sub == bucket, 1.0, 0.0)
    seen = jnp.dot(onehot.astype(BF16), tri_ref[...], preferred_element_type=F32)
    cnt = cnt_ref[...]
    rank_ref[...] = jnp.sum(onehot * (seen - 1.0 + cnt[:, 0:1]), axis=0, keepdims=True).astype(jnp.int32)
    cnt = cnt + jnp.sum(onehot, axis=1, keepdims=True)
    cnt_ref[...] = cnt
    cnt_out_ref[...] = cnt.astype(jnp.int32)


def _router_weights(w_grp, b_grp, w_exp, b_exp):
    pad = lambda v, n: jnp.zeros(v.shape[:1] + (n,) + v.shape[2:], F32)
    gap, tail = RT_EXPERT_ROW - N_GROUPS, RT_ROWS - RT_EXPERT_ROW - N_EXPERTS
    w_grp, w_exp = jnp.swapaxes(w_grp, 1, 2), jnp.swapaxes(w_exp, 1, 2)
    w = jnp.concatenate([w_grp, pad(w_grp, gap), w_exp, pad(w_exp, tail)], axis=1)
    b_grp, b_exp = b_grp[:, :, None], b_exp[:, :, None]
    b = jnp.concatenate([b_grp, pad(b_grp, gap), b_exp, pad(b_exp, tail)], axis=1)
    return w, b


DISP_TM = 512


ROW_GROUP = 64


def _for_each_row(n_rows, fn):
    def group(k, carry):
        g0 = pl.multiple_of(k * ROW_GROUP, ROW_GROUP)
        for u in range(ROW_GROUP):
            fn(g0, u)
        return carry

    lax.fori_loop(0, n_rows // ROW_GROUP, group, 0)


def _dispatch_kernel(pos_ref, start_ref, cnt_ref, nt_ref, hx_hbm, hs_ref, buf_ref, zero_ref, blk_sem, row_sem, zsem):
    i = pl.program_id(0)
    last = pl.num_programs(0) - 1
    slot = i % 2
    base = i * DISP_TM

    def tile_in(tile, s):
        return pltpu.make_async_copy(hx_hbm.at[pl.ds(pl.multiple_of(tile * DISP_TM, DISP_TM), DISP_TM), :],
                                     buf_ref.at[s], blk_sem.at[s])

    def wait_rows(s):
        pltpu.make_async_copy(buf_ref.at[s], hs_ref.at[pl.ds(0, DISP_TM), :], row_sem.at[s]).wait()

    @pl.when(i == 0)
    def _():
        zero_ref[...] = jnp.zeros_like(zero_ref)

        def tile_copy(t):
            return pltpu.make_async_copy(zero_ref, hs_ref.at[pl.ds(pl.multiple_of(t * EXP_TM, EXP_TM), EXP_TM), :],
                                         zsem)

        def last_tile(g):
            return (start_ref[g] + cnt_ref[g, 0] - 1) // EXP_TM

        def each_tile(fn):
            for g in range(N_BUCKETS):
                pl.when(cnt_ref[g, 0] > 0)(functools.partial(fn, last_tile(g)))
            lax.fori_loop(nt_ref[0], EXP_TILES, lambda t, carry: (fn(t), carry)[1], 0)

        each_tile(lambda t: tile_copy(t).start())
        each_tile(lambda t: tile_copy(t).wait())
        tile_in(0, 0).start()

    @pl.when(i > 0)
    def _():
        wait_rows(1 - slot)

    @pl.when(i < last)
    def _():
        tile_in(i + 1, 1 - slot).start()

    tile_in(i, slot).wait()

    def issue(g0, u):
        src = buf_ref.at[slot, pl.ds(g0, ROW_GROUP), :]
        pltpu.make_async_copy(src.at[pl.ds(u, 1), :], hs_ref.at[pl.ds(pos_ref[base + g0 + u], 1), :],
                              row_sem.at[slot]).start(priority=u % 2)

    _for_each_row(DISP_TM, issue)

    @pl.when(i == last)
    def _():
        wait_rows(slot)


def dispatch(pos, starts, counts, n_tiles, hx):
    return pl.pallas_call(
        _dispatch_kernel,
        grid_spec=pltpu.PrefetchScalarGridSpec(
            num_scalar_prefetch=4,
            grid=(N_TOK // DISP_TM,),
            in_specs=[pl.BlockSpec(memory_space=pl.ANY)],
            out_specs=pl.BlockSpec(memory_space=pl.ANY),
            scratch_shapes=[pltpu.VMEM((2, DISP_TM, HX_W), F32), pltpu.VMEM((EXP_TM, HX_W), F32),
                            pltpu.SemaphoreType.DMA((2,)), pltpu.SemaphoreType.DMA((2,)),
                            pltpu.SemaphoreType.DMA(())],
        ),
        out_shape=jax.ShapeDtypeStruct((HS_ROWS, HX_W), F32),
        compiler_params=_params(1),
        name="dispatch",
    )(pos, starts, counts, n_tiles, hx)


(T_GROUP, T_LO, T_HI, T_SLOT, T_RUN_POS, T_NEXT_GROUP, T_FIRST, T_PREFETCHED,
 T_PENDING_GROUP, T_PENDING_EXPERT, T_PENDING_SLOT) = range(11)


def _experts_kernel(layer, tab_ref, nt_ref, hs_ref, w1_hbm, w3_hbm, w2_hbm, ys_ref,
                    w1_ref, w3_ref, w2_ref, st1_ref, st3_ref, st2_ref, sem_ref):
    t = pl.program_id(0)
    valid = t < nt_ref[0]
    group = tab_ref[T_GROUP, t]
    slot = tab_ref[T_SLOT, t]
    run_pos = tab_ref[T_RUN_POS, t]
    next_group = tab_ref[T_NEXT_GROUP, t]
    pending_expert = tab_ref[T_PENDING_EXPERT, t]

    def expert_copies(grp, e):
        idx = grp * EXPERTS_PER_GROUP + e
        return (pltpu.make_async_copy(w1_hbm.at[layer, idx], st1_ref, sem_ref.at[0]),
                pltpu.make_async_copy(w3_hbm.at[layer, idx], st3_ref, sem_ref.at[1]),
                pltpu.make_async_copy(w2_hbm.at[layer, idx], st2_ref, sem_ref.at[2]))

    def finish(copies, dst_slot, e):
        for cp in copies:
            cp.wait()
        w1_ref[dst_slot, e] = st1_ref[...].astype(BF16)
        w3_ref[dst_slot, e] = st3_ref[...].astype(BF16)
        w2_ref[dst_slot, e] = st2_ref[...].astype(BF16)

    @pl.when(jnp.logical_and(valid, pending_expert >= 0))
    def _():
        finish(expert_copies(tab_ref[T_PENDING_GROUP, t], pending_expert), tab_ref[T_PENDING_SLOT, t], pending_expert)

    @pl.when(jnp.logical_and(valid, tab_ref[T_FIRST, t] == 1))
    def _():
        def load(e, carry):
            copies = expert_copies(group, e)
            for cp in copies:
                cp.start()
            finish(copies, slot, e)
            return carry

        lax.fori_loop(tab_ref[T_PREFETCHED, t], EXPERTS_PER_GROUP, load, 0)

    prefetch = jnp.logical_and(valid, jnp.logical_and(next_group >= 0, run_pos < EXPERTS_PER_GROUP))

    @pl.when(prefetch)
    def _():
        for cp in expert_copies(next_group, run_pos):
            cp.start()

    @pl.when(valid)
    def _():
        h = hs_ref[:, 0:D_MODEL].astype(BF16)
        c = hs_ref[:, D_MODEL:]
        lane = lax.broadcasted_iota(jnp.int32, c.shape, 1)
        acc = None
        for e in (tab_ref[T_LO, t], tab_ref[T_HI, t]):
            ce = jnp.sum(jnp.where(lane == e, c, 0.0), axis=1, keepdims=True)
            a = jnp.dot(h, w1_ref[slot, e], preferred_element_type=F32)
            b = jnp.dot(h, w3_ref[slot, e], preferred_element_type=F32)
            hid = (a * _sigmoid(a)) * b * ce
            y = jnp.dot(hid.astype(BF16), w2_ref[slot, e], preferred_element_type=F32)
            acc = y if acc is None else acc + y
        ys_ref[...] = acc

    @pl.when(jnp.logical_not(valid))
    def _():
        ys_ref[...] = jnp.zeros_like(ys_ref)


def experts(tile_table, n_tiles, hs, w1, w3, w2, layer):
    hbm = pl.BlockSpec(memory_space=pl.ANY)
    return pl.pallas_call(
        functools.partial(_experts_kernel, layer),
        grid_spec=pltpu.PrefetchScalarGridSpec(
            num_scalar_prefetch=2,
            grid=(EXP_TILES,),
            in_specs=[pl.BlockSpec((EXP_TM, HX_W), lambda t, *_: (t, 0)), hbm, hbm, hbm],
            out_specs=pl.BlockSpec((EXP_TM, D_MODEL), lambda t, *_: (t, 0)),
            scratch_shapes=[
                pltpu.VMEM((2, EXPERTS_PER_GROUP, D_MODEL, D_EXPERT), BF16),
                pltpu.VMEM((2, EXPERTS_PER_GROUP, D_MODEL, D_EXPERT), BF16),
                pltpu.VMEM((2, EXPERTS_PER_GROUP, D_EXPERT, D_MODEL), BF16),
                pltpu.VMEM((D_MODEL, D_EXPERT), F32),
                pltpu.VMEM((D_MODEL, D_EXPERT), F32),
                pltpu.VMEM((D_EXPERT, D_MODEL), F32),
                pltpu.SemaphoreType.DMA((3,)),
            ],
        ),
        out_shape=jax.ShapeDtypeStruct((HS_ROWS, D_MODEL), F32),
        compiler_params=_params(1),
        name="experts",
    )(tile_table, n_tiles, hs, w1, w3, w2)


N_TABLE_ROWS = T_PENDING_SLOT + 1
EXP_TM_LOG2 = EXP_TM.bit_length() - 1


def _schedule_kernel(cnt_ref, starts_ref, tab_ref, nt_ref):
    i32 = jnp.int32
    t = i32(0)
    for b in range(N_BUCKETS):
        n = lax.shift_right_logical(cnt_ref[b, 0] + (EXP_TM - 1), EXP_TM_LOG2)
        starts_ref[b] = t * EXP_TM

        def fill(j, carry, t=t, group=b // N_PAIRS, pair=EXPERT_PAIRS[b % N_PAIRS]):
            tab_ref[T_GROUP, t + j] = group
            tab_ref[T_LO, t + j] = pair[0]
            tab_ref[T_HI, t + j] = pair[1]
            return carry

        lax.fori_loop(0, n, fill, 0)
        t = t + n
    n_tiles = t
    nt_ref[0] = n_tiles

    def unused(k, carry):
        for row, value in ((T_GROUP, N_GROUPS - 1), (T_LO, 0), (T_HI, 1), (T_SLOT, 0), (T_RUN_POS, 0),
                           (T_NEXT_GROUP, -1), (T_FIRST, 0), (T_PREFETCHED, 0), (T_PENDING_GROUP, 0),
                           (T_PENDING_EXPERT, -1), (T_PENDING_SLOT, 0)):
            tab_ref[row, k] = value
        return carry

    lax.fori_loop(n_tiles, EXP_TILES, unused, 0)

    def back(k, carry):
        next_group, group_after = carry
        tile = n_tiles - 1 - k
        group = tab_ref[T_GROUP, tile]
        next_group = jnp.where(k == 0, -1, jnp.where(group != group_after, group_after, next_group))
        tab_ref[T_NEXT_GROUP, tile] = next_group
        return next_group, group

    lax.fori_loop(0, n_tiles, back, (i32(-1), i32(-1)))

    def forward(tile, carry):
        prev_group, run, run_start, prev_len, prev_fetch, prev_pos, prev_next, prev_slot = carry
        group = tab_ref[T_GROUP, tile]
        first = jnp.logical_or(tile == 0, group != prev_group)
        prev_len = jnp.where(first, tile - run_start, prev_len)
        run = jnp.where(first, run + 1, run)
        run_start = jnp.where(first, tile, run_start)
        slot = lax.rem(run, 2)
        run_pos = tile - run_start
        next_group = tab_ref[T_NEXT_GROUP, tile]
        tab_ref[T_FIRST, tile] = first.astype(i32)
        tab_ref[T_SLOT, tile] = slot
        tab_ref[T_RUN_POS, tile] = run_pos
        tab_ref[T_PREFETCHED, tile] = jnp.minimum(prev_len, EXPERTS_PER_GROUP)
        tab_ref[T_PENDING_EXPERT, tile] = jnp.where(prev_fetch == 1, prev_pos, -1)
        tab_ref[T_PENDING_GROUP, tile] = prev_next
        tab_ref[T_PENDING_SLOT, tile] = 1 - prev_slot
        fetch = jnp.logical_and(next_group >= 0, run_pos < EXPERTS_PER_GROUP).astype(i32)
        return group, run, run_start, prev_len, fetch, run_pos, next_group, slot

    lax.fori_loop(0, n_tiles, forward, (i32(-1), i32(-1), i32(0), i32(0), i32(0), i32(0), i32(0), i32(0)))


def expert_schedule(counts):
    smem = pl.BlockSpec(memory_space=pltpu.SMEM)
    return pl.pallas_call(
        _schedule_kernel,
        in_specs=[smem],
        out_specs=[smem, smem, smem],
        out_shape=[jax.ShapeDtypeStruct((N_BUCKETS,), jnp.int32),
                   jax.ShapeDtypeStruct((N_TABLE_ROWS, EXP_TILES), jnp.int32),
                   jax.ShapeDtypeStruct((1,), jnp.int32)],
        name="expert_schedule",
    )(counts)


RES_TM = 512
RES_CTX_TILES = N_CTX // RES_TM


def _moe_residual_kernel(final, pos_ref, x_ref, mod_ref, gain_ref, ys_ref, *refs):
    out_refs, (ybuf_ref, sem_ref) = refs[:-2], refs[-2:]
    i = pl.program_id(0)
    slot = i % 2

    def gather_tile(tile, tile_slot):
        def issue(g0, u):
            dst = ybuf_ref.at[tile_slot, pl.ds(g0, ROW_GROUP), :]
            pltpu.make_async_copy(ys_ref.at[pl.ds(pos_ref[tile * RES_TM + g0 + u], 1), :], dst.at[pl.ds(u, 1), :],
                                  sem_ref.at[tile_slot]).start(priority=u % 2)

        _for_each_row(RES_TM, issue)

    @pl.when(i == 0)
    def _():
        gather_tile(0, 0)

    @pl.when(i + 1 < pl.num_programs(0))
    def _():
        gather_tile(i + 1, 1 - slot)

    pltpu.make_async_copy(ys_ref.at[pl.ds(0, RES_TM), :], ybuf_ref.at[slot], sem_ref.at[slot]).wait()
    x = x_ref[...] + mod_ref[:, 5 * D_MODEL:6 * D_MODEL] * ybuf_ref[slot]
    if not final:
        out_refs[0][...] = x
        return
    y = _rms(x, gain_ref[...])

    @pl.when(i < RES_CTX_TILES)
    def _():
        out_refs[0][...] = y

    @pl.when(i >= RES_CTX_TILES)
    def _():
        out_refs[1][...] = y


def moe_residual(pos, x, mod4, gain, ys, layer, final):
    if final:
        out_specs = _split_specs((RES_TM, D_MODEL), RES_CTX_TILES)
        out_shape = [jax.ShapeDtypeStruct((N_CTX, D_MODEL), F32), jax.ShapeDtypeStruct((N_LAT, D_MODEL), F32)]
    else:
        out_specs = pl.BlockSpec((RES_TM, D_MODEL), lambda i, *_: (i, 0))
        out_shape = jax.ShapeDtypeStruct((N_TOK, D_MODEL), F32)
    return pl.pallas_call(
        functools.partial(_moe_residual_kernel, final),
        grid_spec=pltpu.PrefetchScalarGridSpec(
            num_scalar_prefetch=1,
            grid=(N_TOK // RES_TM,),
            in_specs=[
                pl.BlockSpec((RES_TM, D_MODEL), lambda i, *_: (i, 0)),
                _mod_spec(layer, RES_TM),
                pl.BlockSpec((1, D_MODEL), lambda i, *_: (0, 0)),
                pl.BlockSpec(memory_space=pl.ANY),
            ],
            out_specs=out_specs,
            scratch_shapes=[pltpu.VMEM((2, RES_TM, D_MODEL), F32), pltpu.SemaphoreType.DMA((2,))],
        ),
        out_shape=out_shape,
        compiler_params=_params(1),
        name="moe_residual_final" if final else "moe_residual",
    )(pos, x, mod4, gain, ys)


def kernel(x_prompt, x_sample, cache_k, cache_v, state_lru, c, c_ctx, w_mod, b_mod, norm_mix, norm_ffn, w_in, sgu_norm, sgu_w, sgu_b, lru_conv_w, lru_conv_b, lru_w_r, lru_b_r, lru_w_i, lru_b_i, lru_lambda, na_rpb, w_branch_sgu, w_branch_lru, w_branch_na, w_out, moe_w_group, moe_b_group, moe_w_expert, moe_b_expert, moe_w1, moe_w3, moe_w2, final_norm_gain):
    xs = (x_prompt.reshape(N_CTX, D_MODEL), x_sample.reshape(N_LAT, D_MODEL))
    cond = jnp.zeros((N_COND, D_MODEL), F32).at[0].set(c_ctx).at[1:1 + DEC_BATCH].set(c)
    mod = modulation(cond, w_mod, b_mod)
    zero_state = jnp.zeros((BATCH, 2, LRU_WIDTH), F32)
    feature_major = lambda t: jnp.transpose(t, (0, 1, 3, 4, 2)).reshape(DEC_BATCH, DEPTH, NA_WIDTH, PAST_LEN)
    cache_k, cache_v = feature_major(cache_k), feature_major(cache_v)
    final_gain = final_norm_gain.reshape(1, D_MODEL)
    norm_mix = norm_mix.reshape(DEPTH, 1, D_MODEL)
    norm_ffn = norm_ffn.reshape(DEPTH, 1, D_MODEL)
    sgu_norm = sgu_norm.reshape(DEPTH, 1, SGU_WIDTH)
    sgu_b_t = jnp.swapaxes(sgu_b, 1, 2)
    lru_conv_b = lru_conv_b.reshape(DEPTH, 1, LRU_WIDTH)
    w_gate = _block_diag_gate_weights(lru_w_r, lru_w_i)
    rpb = na_rpb.reshape(DEPTH, NA_HEADS * N_DROW * N_DCOL)
    w_rt, b_rt = _router_weights(moe_w_group, moe_b_group, moe_w_expert, moe_b_expert)
    caches, ss = [], []
    for l in range(DEPTH):
        ya, za, q, kb, vb, k_ctx, v_ctx, gates = in_projection(
            xs, mod, norm_mix, w_in, sgu_norm, sgu_w, sgu_b_t, l, prev_caches=caches if l == DEPTH - 1 else ())
        caches.append((k_ctx, v_ctx))
        lru_args = (lru_conv_w, lru_conv_b, w_gate, lru_b_r, lru_b_i, lru_lambda)
        yb_ctx, st_ctx = rglru(za, *lru_args, zero_state, pl.BlockSpec((None, 2, LRU_WIDTH), lambda i: (i, 0, 0)),
                               SEQ, BATCH, 0, l)
        yb_lat, _ = rglru(za, *lru_args, state_lru,
                          pl.BlockSpec((None, None, 2, LRU_WIDTH), lambda i, l=l: (i, l, 0, 0)),
                          DEC_SEQ, DEC_BATCH, N_CTX, l)
        yc_ctx = context_attention(q, kb, vb)
        yc_lat = latent_attention(q, kb, vb, cache_k, cache_v, rpb, l)
        x, hx, bucket, rank, counts = merge_branches(
            xs, mod, ya, yb_ctx, yb_lat, yc_ctx, yc_lat, gates, w_branch_sgu, w_branch_lru, w_branch_na, w_out,
            norm_ffn, w_rt, b_rt, l)
        bucket, rank = bucket.reshape(N_TOK), rank.reshape(N_TOK)
        starts, tile_table, n_tiles = expert_schedule(counts)
        pos = rank + jnp.sum(jnp.where(bucket[:, None] == jnp.arange(N_BUCKETS), starts[None, :], 0), axis=1)
        hs = dispatch(pos, starts, counts, n_tiles, hx)
        ys = experts(tile_table, n_tiles, hs, moe_w1, moe_w3, moe_w2, l)
        if l < DEPTH - 1:
            xs = (moe_residual(pos, x, mod, final_gain, ys, l, False),)
        else:
            y_ctx, y_lat = moe_residual(pos, x, mod, final_gain, ys, l, True)
        ss.append(st_ctx)
    cache_shape = (BATCH, DEPTH, SEQ, NA_HEADS, NA_HEAD_DIM)
    return (y_ctx.reshape(BATCH, SEQ, D_MODEL), y_lat.reshape(DEC_BATCH, DEC_SEQ, D_MODEL),
            k_ctx.reshape(cache_shape), v_ctx.reshape(cache_shape), jnp.stack(ss, axis=1))
```

```python
import functools

import jax
import jax.numpy as jnp
import numpy as np
from jax import lax
from jax.experimental import pallas as pl
from jax.experimental.pallas import tpu as pltpu

F32 = jnp.float32
BF16 = jnp.bfloat16

D_MODEL = 1024
BATCH = 16
SEQ = 256
DEPTH = 2
DEC_BATCH = 4
DEC_SEQ = 2048
PAST_LEN = 512
GRID_W = 64
CHUNK = 128
SGU_WIDTH = 512
SGU_GROUPS = 4
LRU_WIDTH = 512
LRU_BLOCKS = 8
CONV_WIDTH = 4
LRU_C = 8.0
NA_HEADS = 8
NA_HEAD_DIM = 64
NA_WIDTH = 512
WIN_ROWS = 8
WIN_COLS = 16
N_GROUPS = 4
EXPERTS_PER_GROUP = 4
N_EXPERTS = 16
D_EXPERT = 512
IN_WIDTH = 6656
EPS = 1e-6
NEG_INF = -1e30

N_CTX = BATCH * SEQ
N_LAT = DEC_BATCH * DEC_SEQ
N_TOK = N_CTX + N_LAT
N_COND = 8
MOD_WIDTH = 6 * D_MODEL
GRID_ROWS = DEC_SEQ // GRID_W

VMEM_LIMIT_BYTES = 56 * 1024 * 1024


def _params(n_axes):
    return pltpu.CompilerParams(dimension_semantics=("arbitrary",) * n_axes,
                                vmem_limit_bytes=VMEM_LIMIT_BYTES)


def _cond_row(tile, tile_rows):
    tok = tile * tile_rows
    return jnp.where(tok < N_CTX, 0, 1 + (tok - N_CTX) // DEC_SEQ)


def _layer_spec(shape, layer):
    return pl.BlockSpec((None,) + tuple(shape), lambda i, *_: (layer,) + (0,) * len(shape))


def _stacked_rows_spec(width):
    return pl.BlockSpec((DEPTH, width), lambda i, *_: (0, 0))


def _mod_spec(layer, tile_rows):
    return pl.BlockSpec((None, None, 1, MOD_WIDTH), lambda i, *_: (layer, _cond_row(i, tile_rows), 0, 0))


def _rms(x, gain):
    return x * lax.rsqrt(jnp.mean(x * x, axis=-1, keepdims=True) + EPS) * gain


def _bdot(a, b):
    return jnp.dot(a.astype(BF16), b.astype(BF16), preferred_element_type=F32)


def _bdot_t(a, b):
    return lax.dot_general(a.astype(BF16), b.astype(BF16), (((1,), (1,)), ((), ())),
                           preferred_element_type=F32)


MOD_TN = 1536


def _mod_kernel(cond_ref, w_ref, b_ref, o_ref):
    c = cond_ref[...]
    s = c * jax.nn.sigmoid(c)
    mod = _bdot(s, w_ref[...]) + b_ref[pl.ds(pl.program_id(0), 1), :]
    for r in range(N_COND):
        o_ref[r] = mod[r:r + 1, :]


def modulation(cond, w_mod, b_mod):
    return pl.pallas_call(
        _mod_kernel,
        grid=(DEPTH, MOD_WIDTH // MOD_TN),
        in_specs=[
            pl.BlockSpec((N_COND, D_MODEL), lambda l, j: (0, 0)),
            pl.BlockSpec((None, D_MODEL, MOD_TN), lambda l, j: (l, 0, j)),
            pl.BlockSpec((DEPTH, MOD_TN), lambda l, j: (0, j)),
        ],
        out_specs=pl.BlockSpec((None, N_COND, 1, MOD_TN), lambda l, j: (l, 0, 0, j)),
        out_shape=jax.ShapeDtypeStruct((DEPTH, N_COND, 1, MOD_WIDTH), F32),
        compiler_params=_params(2),
        name="modulation",
    )(cond, w_mod, b_mod)


IN_TM = 256
IN_CTX_TILES = N_CTX // IN_TM
ZA_WIDTH = 4 * 512
KV_OFF = ZA_WIDTH + NA_WIDTH
GATE_OFF = KV_OFF + 2 * NA_WIDTH
W_CHUNK = 512
Q_SCALE = NA_HEAD_DIM ** -0.5 * float(np.log2(np.e))


def _load_cast(chunks, stage_ref, sem_ref):
    def copy(j):
        return pltpu.make_async_copy(chunks[j][0], stage_ref.at[j % 2], sem_ref.at[j % 2])

    copy(0).start()
    for j in range(len(chunks)):
        if j + 1 < len(chunks):
            copy(j + 1).start()
        copy(j).wait()
        chunks[j][1](stage_ref[j % 2].astype(BF16))


def _token_tile(i, n_ctx_tiles, refs):
    if len(refs) == 1:
        return refs[0][...]
    return jnp.where(i < n_ctx_tiles, refs[0][...], refs[1][...])


def _split_specs(block, n_ctx_tiles):
    return [pl.BlockSpec(block, lambda i, *_: (jnp.minimum(i, n_ctx_tiles - 1), 0)),
            pl.BlockSpec(block, lambda i, *_: (jnp.maximum(i - n_ctx_tiles, 0), 0))]


SGU_GD = SGU_WIDTH // SGU_GROUPS


def _spatial_gating(u, v, gain, ws_ref, bs_ref):
    u = jax.nn.gelu(u)
    v = _rms(jax.nn.gelu(v), gain).astype(BF16)
    out = []
    for g in range(SGU_GROUPS):
        cols = slice(g * SGU_GD, (g + 1) * SGU_GD)
        mixed = jnp.dot(ws_ref[g].astype(BF16), v[:, cols], preferred_element_type=F32) + bs_ref[:, g:g + 1]
        out.append(u[:, cols] * mixed)
    return jnp.concatenate(out, axis=1)


def _inproj_kernel(layer, n_x, n_prev, *refs):
    x_refs, (mod_ref, gain_ref, w_hbm, sgu_gain_ref, ws_ref, bs_ref) = refs[:n_x], refs[n_x:n_x + 6]
    prev_refs, rest = refs[n_x + 6:n_x + 6 + 2 * n_prev], refs[n_x + 6 + 2 * n_prev:]
    ya_ref, zb_ref, q_ref, kb_ref, vb_ref, kc_ref, vc_ref, g_ref, w_ref, stage_ref, sem_ref = rest
    i = pl.program_id(0)

    @pl.when(i == 0)
    def _():
        def store(c):
            def st(v):
                w_ref[:, c * W_CHUNK:(c + 1) * W_CHUNK] = v
            return st

        _load_cast([(w_hbm.at[layer, :, pl.ds(c * W_CHUNK, W_CHUNK)], store(c)) for c in range(IN_WIDTH // W_CHUNK)],
                   stage_ref, sem_ref)

    m = mod_ref[...]
    shift, scale = m[:, 0:D_MODEL], m[:, D_MODEL:2 * D_MODEL]
    gain = gain_ref[layer:layer + 1, :]
    h = (_rms(_token_tile(i, IN_CTX_TILES, x_refs), gain) * (1.0 + scale) + shift).astype(BF16)
    za = jnp.dot(h, w_ref[:, 0:ZA_WIDTH], preferred_element_type=F32)
    for c in range(IN_TM // CHUNK):
        rows = slice(c * CHUNK, (c + 1) * CHUNK)
        ya_ref[rows, :] = _spatial_gating(za[rows, 0:SGU_WIDTH], za[rows, SGU_WIDTH:2 * SGU_WIDTH],
                                          sgu_gain_ref[layer:layer + 1, :], ws_ref, bs_ref).astype(BF16)
    zb_ref[:, 0:LRU_WIDTH] = za[:, 2 * SGU_WIDTH:2 * SGU_WIDTH + LRU_WIDTH].astype(BF16)
    zb_ref[:, LRU_WIDTH:] = jax.nn.gelu(za[:, 2 * SGU_WIDTH + LRU_WIDTH:]).astype(BF16)
    q_ref[...] = (jnp.dot(h, w_ref[:, ZA_WIDTH:KV_OFF], preferred_element_type=F32) * Q_SCALE).astype(BF16)
    kv = jnp.dot(h, w_ref[:, KV_OFF:GATE_OFF], preferred_element_type=F32)
    kb_ref[...] = kv[:, 0:NA_WIDTH].astype(BF16)
    vb_ref[...] = kv[:, NA_WIDTH:].astype(BF16)
    g_ref[...] = _sigmoid(jnp.dot(h, w_ref[:, GATE_OFF:], preferred_element_type=F32)).astype(BF16)

    @pl.when(i < IN_CTX_TILES)
    def _():
        if n_prev == 0:
            kc_ref[...] = kv[:, 0:NA_WIDTH].T
            vc_ref[...] = kv[:, NA_WIDTH:].T
        else:
            for p in range(n_prev):
                kc_ref[p] = prev_refs[2 * p][...]
                vc_ref[p] = prev_refs[2 * p + 1][...]
            kc_ref[n_prev] = kv[:, 0:NA_WIDTH].T
            vc_ref[n_prev] = kv[:, NA_WIDTH:].T


def in_projection(xs, mod3, gain, w_in, sgu_gain, sgu_w, sgu_b_t, layer, prev_caches=()):
    assert IN_TM == SEQ
    row = lambda i: (i, 0)
    x_specs = ([pl.BlockSpec((IN_TM, D_MODEL), row)] if len(xs) == 1
               else _split_specs((IN_TM, D_MODEL), IN_CTX_TILES))
    n_prev = len(prev_caches)
    layer_cache_spec = pl.BlockSpec((None, NA_WIDTH, SEQ), lambda i: (jnp.minimum(i, IN_CTX_TILES - 1), 0, 0))
    if n_prev:
        cache_spec = pl.BlockSpec((None, n_prev + 1, NA_WIDTH, SEQ),
                                  lambda i: (jnp.minimum(i, IN_CTX_TILES - 1), 0, 0, 0))
        cache_shape = jax.ShapeDtypeStruct((BATCH, n_prev + 1, NA_WIDTH, SEQ), F32)
    else:
        cache_spec = layer_cache_spec
        cache_shape = jax.ShapeDtypeStruct((BATCH, NA_WIDTH, SEQ), F32)
    return pl.pallas_call(
        functools.partial(_inproj_kernel, layer, len(xs), n_prev),
        grid=(N_TOK // IN_TM,),
        in_specs=x_specs + [
            _mod_spec(layer, IN_TM),
            _stacked_rows_spec(D_MODEL),
            pl.BlockSpec(memory_space=pl.ANY),
            _stacked_rows_spec(SGU_WIDTH),
            _layer_spec((SGU_GROUPS, CHUNK, CHUNK), layer),
            _layer_spec((CHUNK, SGU_GROUPS), layer),
        ] + [layer_cache_spec] * (2 * n_prev),
        out_specs=[
            pl.BlockSpec((IN_TM, SGU_WIDTH), row),
            pl.BlockSpec((IN_TM, 2 * LRU_WIDTH), row),
            pl.BlockSpec((IN_TM, NA_WIDTH), row),
            pl.BlockSpec((IN_TM, NA_WIDTH), row),
            pl.BlockSpec((IN_TM, NA_WIDTH), row),
            cache_spec,
            cache_spec,
            pl.BlockSpec((IN_TM, 3 * D_MODEL), row),
        ],
        out_shape=[
            jax.ShapeDtypeStruct((N_TOK, SGU_WIDTH), BF16),
            jax.ShapeDtypeStruct((N_TOK, 2 * LRU_WIDTH), BF16),
            jax.ShapeDtypeStruct((N_TOK, NA_WIDTH), BF16),
            jax.ShapeDtypeStruct((N_TOK, NA_WIDTH), BF16),
            jax.ShapeDtypeStruct((N_TOK, NA_WIDTH), BF16),
            cache_shape,
            cache_shape,
            jax.ShapeDtypeStruct((N_TOK, 3 * D_MODEL), BF16),
        ],
        scratch_shapes=[pltpu.VMEM((D_MODEL, IN_WIDTH), BF16), pltpu.VMEM((2, D_MODEL, W_CHUNK), F32),
                        pltpu.SemaphoreType.DMA((2,))],
        compiler_params=_params(1),
        name="in_projection",
    )(*xs, mod3, gain, w_in, sgu_gain, sgu_w, sgu_b_t, *[c for kv in prev_caches for c in kv])


LRU_TC = 256
LRU_HALF = 256
SUB = 8
HALO = 8
LANES = 128
LRU_SLABS = LRU_WIDTH // LANES


def _sigmoid(x):
    return 0.5 * jnp.tanh(0.5 * x) + 0.5


def _lru_kernel(seq_len, layer, xr_ref, gr_ref, cw_ref, cb_ref, wlo_ref, whi_ref, br_ref, bi_ref,
                lam_ref, h0_ref, y_ref, st_ref, xp_ref, a_ref, b_ref, h_ref):
    seg_len = seq_len // SUB
    pitch = seg_len + SUB
    segs_per_chunk = max(LRU_TC // seg_len, 1)
    n_chunks = seq_len // LRU_TC
    zeros = jnp.zeros((HALO, LRU_WIDTH), F32)
    xp_ref[0:HALO, :] = zeros
    xp_ref[seq_len + HALO:seq_len + 2 * HALO, :] = zeros

    def copy_in(c, carry):
        r0 = pl.multiple_of(c * LRU_TC, LRU_TC)
        xp_ref[pl.ds(r0 + HALO, LRU_TC), :] = xr_ref[pl.ds(r0, LRU_TC), :].astype(F32)
        return carry

    lax.fori_loop(0, n_chunks, copy_in, 0)

    def chunk_rows(c, seg):
        n = min(seg_len, LRU_TC)
        start = pl.multiple_of((c * segs_per_chunk + seg) * pitch, SUB)
        return pl.ds(start, n), slice(seg * n, (seg + 1) * n)

    cw = cw_ref[...]
    cb = cb_ref[layer:layer + 1, :]
    win = LRU_TC + 2 * HALO
    neg_lam = -lam_ref[...]
    softplus = jnp.maximum(neg_lam, 0.0) + jnp.log1p(jnp.exp(-jnp.abs(neg_lam)))
    decay = (0.5 * LRU_C) * softplus
    half_br = 0.5 * br_ref[...]
    half_bi = 0.5 * bi_ref[...]

    def gates(c, carry):
        r0 = pl.multiple_of(c * LRU_TC, LRU_TC)
        w = xp_ref[pl.ds(r0, win), :]
        xc = (cw[0:1, :] * pltpu.roll(w, 1, 0)[HALO:HALO + LRU_TC]
              + cw[1:2, :] * w[HALO:HALO + LRU_TC]
              + cw[2:3, :] * pltpu.roll(w, win - 1, 0)[HALO:HALO + LRU_TC]
              + cw[3:4, :] * pltpu.roll(w, win - 2, 0)[HALO:HALO + LRU_TC]) + cb
        xb = xc.astype(BF16)
        for half, w_ref in enumerate((wlo_ref, whi_ref)):
            cols = slice(half * LRU_HALF, (half + 1) * LRU_HALF)
            pre = jnp.dot(xb[:, cols], w_ref[...], preferred_element_type=F32)
            half_x = 0.5 * xc[:, cols]
            for d in range(2):
                tr = jnp.tanh(pre[:, (2 * d) * LRU_HALF:(2 * d + 1) * LRU_HALF] + half_br[d:d + 1, cols])
                ti = jnp.tanh(pre[:, (2 * d + 1) * LRU_HALF:(2 * d + 2) * LRU_HALF] + half_bi[d:d + 1, cols])
                neg_log_a = decay[d:d + 1, cols] * tr + decay[d:d + 1, cols]
                a = jnp.exp2(neg_log_a * -LOG2E)
                z = jnp.tanh(neg_log_a) * (a * a + 1.0)
                b = jnp.where(z > 0.0, z * lax.rsqrt(z), 0.0) * (half_x * ti + half_x)
                for k in range(LRU_HALF // LANES):
                    slab = half * (LRU_HALF // LANES) + k
                    lanes = slice(k * LANES, (k + 1) * LANES)
                    for seg in range(segs_per_chunk):
                        dst, src = chunk_rows(c, seg)
                        a_ref[d, slab, dst, :] = a[src, lanes]
                        b_ref[d, slab, dst, :] = b[src, lanes]
        return carry

    lax.fori_loop(0, n_chunks, gates, 0)

    def step_rows(d, j):
        return pl.ds(j if d == 0 else seg_len - 1 - j, SUB, stride=pitch)

    def reduce_step(j, carry):
        out = []
        for d in range(2):
            for slab in range(LRU_SLABS):
                big_a, big_b = carry[d * LRU_SLABS + slab]
                a = a_ref[d, slab, step_rows(d, j), :]
                b = b_ref[d, slab, step_rows(d, j), :]
                out.append((a * big_a, a * big_b + b))
        return tuple(out)

    ident = (jnp.ones((SUB, LANES), F32), jnp.zeros((SUB, LANES), F32))
    totals = lax.fori_loop(0, seg_len, reduce_step, (ident,) * (2 * LRU_SLABS), unroll=8)

    sub = lax.broadcasted_iota(jnp.int32, (SUB, LANES), 0)
    entering = []
    for d in range(2):
        for slab in range(LRU_SLABS):
            big_a, big_b = totals[d * LRU_SLABS + slab]
            h = h0_ref[d:d + 1, slab * LANES:(slab + 1) * LANES]
            rows = jnp.zeros((SUB, LANES), F32)
            for s in (range(SUB) if d == 0 else reversed(range(SUB))):
                rows = jnp.where(sub == s, h, rows)
                h = big_a[s:s + 1, :] * h + big_b[s:s + 1, :]
            entering.append(rows)
            st_ref[d:d + 1, slab * LANES:(slab + 1) * LANES] = h

    def apply_step(j, carry):
        out = []
        for d in range(2):
            for slab in range(LRU_SLABS):
                h = (a_ref[d, slab, step_rows(d, j), :] * carry[d * LRU_SLABS + slab]
                     + b_ref[d, slab, step_rows(d, j), :])
                h_ref[d, slab, step_rows(d, j), :] = h
                out.append(h)
        return tuple(out)

    lax.fori_loop(0, seg_len, apply_step, tuple(entering), unroll=8)

    def merge(c, carry):
        r0 = pl.multiple_of(c * LRU_TC, LRU_TC)
        gate = gr_ref[pl.ds(r0, LRU_TC), :].astype(F32)
        for slab in range(LRU_SLABS):
            lanes = slice(slab * LANES, (slab + 1) * LANES)
            for seg in range(segs_per_chunk):
                src, dst = chunk_rows(c, seg)
                h = h_ref[0, slab, src, :] + h_ref[1, slab, src, :]
                y_ref[pl.ds(r0 + dst.start, dst.stop - dst.start), lanes] = (h * gate[dst, lanes]).astype(BF16)
        return carry

    lax.fori_loop(0, n_chunks, merge, 0)


def rglru(za, conv_w, conv_b, w_gate, b_r, b_i, lam, h0, h0_spec, seq_len, n_seq, tok_off, layer):
    blk0 = tok_off // seq_len
    half_spec = lambda half: pl.BlockSpec((None, None, LRU_HALF, 4 * LRU_HALF), lambda i: (layer, half, 0, 0))
    return pl.pallas_call(
        functools.partial(_lru_kernel, seq_len, layer),
        grid=(n_seq,),
        in_specs=[
            pl.BlockSpec((seq_len, LRU_WIDTH), lambda i: (i + blk0, 0)),
            pl.BlockSpec((seq_len, LRU_WIDTH), lambda i: (i + blk0, 1)),
            _layer_spec((CONV_WIDTH, LRU_WIDTH), layer),
            _stacked_rows_spec(LRU_WIDTH),
            half_spec(0),
            half_spec(1),
            _layer_spec((2, LRU_WIDTH), layer),
            _layer_spec((2, LRU_WIDTH), layer),
            _layer_spec((2, LRU_WIDTH), layer),
            h0_spec,
        ],
        out_specs=[
            pl.BlockSpec((seq_len, LRU_WIDTH), lambda i: (i, 0)),
            pl.BlockSpec((None, 2, LRU_WIDTH), lambda i: (i, 0, 0)),
        ],
        out_shape=[
            jax.ShapeDtypeStruct((n_seq * seq_len, LRU_WIDTH), BF16),
            jax.ShapeDtypeStruct((n_seq, 2, LRU_WIDTH), F32),
        ],
        scratch_shapes=[
            pltpu.VMEM((seq_len + 2 * HALO, LRU_WIDTH), F32),
            pltpu.VMEM((2, LRU_SLABS, seq_len + SUB * SUB, LANES), F32),
            pltpu.VMEM((2, LRU_SLABS, seq_len + SUB * SUB, LANES), F32),
            pltpu.VMEM((2, LRU_SLABS, seq_len + SUB * SUB, LANES), F32),
        ],
        compiler_params=_params(1),
        name=f"rglru_{seq_len}",
    )(za, za, conv_w, conv_b, w_gate, w_gate, b_r, b_i, lam, h0)


LRU_BLOCK_W = LRU_WIDTH // LRU_BLOCKS
LRU_HALF_BLOCKS = LRU_BLOCKS // 2


def _gate_weights_kernel(wr_ref, wi_ref, o_ref):
    row = lax.broadcasted_iota(jnp.int32, (LRU_BLOCK_W, LRU_HALF), 0)
    lane = lax.broadcasted_iota(jnp.int32, (LRU_BLOCK_W, LRU_HALF), 1)
    groups = [(half, d, g) for half in range(2) for d in range(2) for g in range(2)]
    for a in range(LRU_HALF_BLOCKS):
        rows = slice(a * LRU_BLOCK_W, (a + 1) * LRU_BLOCK_W)
        place = (lane == row + a * LRU_BLOCK_W).astype(BF16)
        blocks = [(0.5 * (wr_ref, wi_ref)[g][d, half * LRU_HALF_BLOCKS + a]).astype(BF16) for half, d, g in groups]
        strips = jnp.dot(jnp.concatenate(blocks, axis=0), place, preferred_element_type=F32).astype(BF16)
        for n, (half, d, g) in enumerate(groups):
            tile = 2 * d + g
            o_ref[half, rows, tile * LRU_HALF:(tile + 1) * LRU_HALF] = strips[n * LRU_BLOCK_W:(n + 1) * LRU_BLOCK_W]


def _block_diag_gate_weights(w_r, w_i):
    spec = pl.BlockSpec((None, 2, LRU_BLOCKS, LRU_BLOCK_W, LRU_BLOCK_W), lambda l: (l, 0, 0, 0, 0))
    return pl.pallas_call(
        _gate_weights_kernel,
        grid=(DEPTH,),
        in_specs=[spec, spec],
        out_specs=pl.BlockSpec((None, 2, LRU_HALF, 4 * LRU_HALF), lambda l: (l, 0, 0, 0)),
        out_shape=jax.ShapeDtypeStruct((DEPTH, 2, LRU_HALF, 4 * LRU_HALF), BF16),
        compiler_params=_params(1),
        name="gate_weights",
    )(w_r, w_i)


PAIR_W = 2 * NA_HEAD_DIM
LOG2E = float(np.log2(np.e))


def _head_lanes(shape, head):
    lane = lax.broadcasted_iota(jnp.int32, shape, 1)
    return (lane < NA_HEAD_DIM) if head == 0 else (lane >= NA_HEAD_DIM)


def _one_head(q_pair, head):
    return jnp.where(_head_lanes(q_pair.shape, head), q_pair.astype(F32), 0.0).astype(BF16)


def _join_heads(o0, o1):
    return jnp.where(_head_lanes(o0.shape, 0), o0, o1)


CTX_SEQS = 4


def _ctx_attn_kernel(q_ref, k_ref, v_ref, o_ref):
    for seq in range(CTX_SEQS):
        rows = slice(seq * SEQ, (seq + 1) * SEQ)
        for j in range(NA_HEADS // 2):
            cols = slice(j * PAIR_W, (j + 1) * PAIR_W)
            outs = []
            for head in range(2):
                s = _bdot_t(_one_head(q_ref[rows, cols], head), k_ref[rows, cols])
                p = jnp.exp2(s - jnp.max(s, axis=-1, keepdims=True))
                denom = jnp.sum(p, axis=-1, keepdims=True)
                outs.append(jnp.dot(p.astype(BF16), v_ref[rows, cols], preferred_element_type=F32) / denom)
            o_ref[rows, cols] = _join_heads(*outs).astype(BF16)


def context_attention(q, kb, vb):
    spec = pl.BlockSpec((CTX_SEQS * SEQ, NA_WIDTH), lambda i: (i, 0))
    return pl.pallas_call(
        _ctx_attn_kernel,
        grid=(BATCH // CTX_SEQS,),
        in_specs=[spec, spec, spec],
        out_specs=spec,
        out_shape=jax.ShapeDtypeStruct((N_CTX, NA_WIDTH), BF16),
        compiler_params=_params(1),
        name="context_attention",
    )(q, kb, vb)


NA_QROWS = 4
NA_TQ = NA_QROWS * GRID_W
NA_KROWS = NA_QROWS + WIN_ROWS
NA_TK = NA_KROWS * GRID_W
NA_QBLOCKS = GRID_ROWS // NA_QROWS


N_DROW = 2 * WIN_ROWS - 1
N_DCOL = 2 * WIN_COLS - 1
NA_BLOCK_KINDS = (0, 1, NA_QBLOCKS - 1)


def _first_key_row(qb):
    return (np.clip if isinstance(qb, int) else jnp.clip)(qb * NA_QROWS - WIN_ROWS // 2, 0, GRID_ROWS - NA_KROWS)


def _build_bias(layer, rpb_ref, table_ref, bias_ref):
    qc = lax.broadcasted_iota(jnp.int32, (GRID_W, GRID_W), 0)
    kc = lax.broadcasted_iota(jnp.int32, (GRID_W, GRID_W), 1)
    col0 = jnp.clip(qc - WIN_COLS // 2, 0, GRID_W - WIN_COLS)
    col_ok = jnp.logical_and(kc >= col0, kc < col0 + WIN_COLS)
    d_col = jnp.clip(kc - qc, 1 - WIN_COLS, WIN_COLS - 1) + WIN_COLS - 1
    neg = jnp.full((GRID_W, GRID_W), NEG_INF, F32)

    def table_entry(idx, carry):
        t = neg
        for j in range(N_DCOL):
            t = jnp.where(d_col == j, rpb_ref[layer, idx * N_DCOL + j], t)
        table_ref[idx] = jnp.where(col_ok, t * LOG2E, neg)
        return carry

    lax.fori_loop(0, NA_HEADS * N_DROW, table_entry, 0)

    def head_blocks(h, carry):
        for kind, qb in enumerate(NA_BLOCK_KINDS):
            for i in range(NA_QROWS):
                qrow = qb * NA_QROWS + i
                win0 = int(np.clip(qrow - WIN_ROWS // 2, 0, GRID_ROWS - WIN_ROWS))
                for kr in range(NA_KROWS):
                    krow = int(_first_key_row(qb)) + kr
                    inside = win0 <= krow < win0 + WIN_ROWS
                    blk = table_ref[h * N_DROW + (krow - qrow + WIN_ROWS - 1)] if inside else neg
                    bias_ref[kind, h, i * GRID_W:(i + 1) * GRID_W, kr * GRID_W:(kr + 1) * GRID_W] = blk
        return carry

    lax.fori_loop(0, NA_HEADS, head_blocks, 0)


def _lat_attn_kernel(layer, rpb_ref, q_ref, k_ref, v_ref, ck_ref, cv_ref, o_ref, table_ref, bias_ref):
    qb = pl.program_id(1)

    @pl.when(jnp.logical_and(pl.program_id(0) == 0, qb == 0))
    def _():
        _build_bias(layer, rpb_ref, table_ref, bias_ref)

    kind = jnp.where(qb == 0, 0, jnp.where(qb == NA_QBLOCKS - 1, 2, 1))
    k0 = pl.multiple_of(_first_key_row(qb) * GRID_W, GRID_W)
    for j in range(NA_HEADS // 2):
        cols = slice(j * PAIR_W, (j + 1) * PAIR_W)
        k_loc = k_ref[pl.ds(k0, NA_TK), cols]
        v_loc = v_ref[pl.ds(k0, NA_TK), cols]
        kt_ctx = ck_ref[cols, :].astype(BF16)
        vt_ctx = cv_ref[cols, :].astype(BF16)
        outs = []
        for head in range(2):
            qh = _one_head(q_ref[:, cols], head)
            s_loc = _bdot_t(qh, k_loc) + bias_ref[kind, 2 * j + head]
            s_ctx = jnp.dot(qh, kt_ctx, preferred_element_type=F32)
            m = jnp.maximum(jnp.max(s_loc, axis=-1, keepdims=True), jnp.max(s_ctx, axis=-1, keepdims=True))
            p_loc = jnp.exp2(s_loc - m)
            p_ctx = jnp.exp2(s_ctx - m)
            denom = jnp.sum(p_loc, axis=-1, keepdims=True) + jnp.sum(p_ctx, axis=-1, keepdims=True)
            o = (jnp.dot(p_loc.astype(BF16), v_loc, preferred_element_type=F32)
                 + _bdot_t(p_ctx.astype(BF16), vt_ctx))
            outs.append(o / denom)
        o_ref[:, cols] = _join_heads(*outs).astype(BF16)


def latent_attention(q, kb, vb, cache_k, cache_v, rpb, layer):
    q_blk0 = N_CTX // NA_TQ
    s_blk0 = N_CTX // DEC_SEQ
    cache_spec = pl.BlockSpec((None, None, NA_WIDTH, PAST_LEN), lambda b, m: (b, layer, 0, 0))
    return pl.pallas_call(
        functools.partial(_lat_attn_kernel, layer),
        grid=(DEC_BATCH, NA_QBLOCKS),
        in_specs=[
            pl.BlockSpec(memory_space=pltpu.SMEM),
            pl.BlockSpec((NA_TQ, NA_WIDTH), lambda b, m: (q_blk0 + b * NA_QBLOCKS + m, 0)),
            pl.BlockSpec((DEC_SEQ, NA_WIDTH), lambda b, m: (s_blk0 + b, 0)),
            pl.BlockSpec((DEC_SEQ, NA_WIDTH), lambda b, m: (s_blk0 + b, 0)),
            cache_spec,
            cache_spec,
        ],
        out_specs=pl.BlockSpec((NA_TQ, NA_WIDTH), lambda b, m: (b * NA_QBLOCKS + m, 0)),
        out_shape=jax.ShapeDtypeStruct((N_LAT, NA_WIDTH), BF16),
        scratch_shapes=[pltpu.VMEM((NA_HEADS * N_DROW, GRID_W, GRID_W), F32),
                        pltpu.VMEM((len(NA_BLOCK_KINDS), NA_HEADS, NA_TQ, NA_TK), F32)],
        compiler_params=_params(2),
        name="latent_attention",
    )(rpb, q, kb, vb, cache_k, cache_v)


MERGE_TM = 512


MERGE_CTX_TILES = N_CTX // MERGE_TM
BRANCH_W = 512


def _merge_kernel(layer, n_x, *refs):
    x_refs, rest = refs[:n_x], refs[n_x:]
    (mod_ref, ya_ref, yb_ctx_ref, yb_lat_ref, yc_ctx_ref, yc_lat_ref, g_ref, wa_hbm, wb_hbm, wc_hbm, wo_hbm,
     ffn_gain_ref, wrt_ref, brt_ref, tri_ref,
     o_ref, hx_ref, bucket_ref, rank_ref, cnt_out_ref,
     wbr_ref, wo_ref, stage_ref, sem_ref, cnt_ref, cpad_ref) = rest
    i = pl.program_id(0)

    @pl.when(i == 0)
    def _():
        def store_branch(k):
            def st(v):
                wbr_ref[k] = v
            return st

        def store_out(k):
            def st(v):
                wo_ref[k * BRANCH_W:(k + 1) * BRANCH_W, :] = v
            return st

        chunks = [(w.at[layer], store_branch(k)) for k, w in enumerate((wa_hbm, wb_hbm, wc_hbm))]
        chunks += [(wo_hbm.at[layer, pl.ds(k * BRANCH_W, BRANCH_W), :], store_out(k))
                   for k in range(D_MODEL // BRANCH_W)]
        _load_cast(chunks, stage_ref, sem_ref)

    g = g_ref[...].astype(F32)
    yb = _token_tile(i, MERGE_CTX_TILES, (yb_ctx_ref, yb_lat_ref))
    yc = _token_tile(i, MERGE_CTX_TILES, (yc_ctx_ref, yc_lat_ref))
    merged = (g[:, 0:D_MODEL] * jnp.dot(ya_ref[...], wbr_ref[0], preferred_element_type=F32)
              + g[:, D_MODEL:2 * D_MODEL] * jnp.dot(yb, wbr_ref[1], preferred_element_type=F32)
              + g[:, 2 * D_MODEL:] * jnp.dot(yc, wbr_ref[2], preferred_element_type=F32))
    y = jnp.dot(merged.astype(BF16), wo_ref[...], preferred_element_type=F32)
    gate = mod_ref[:, 2 * D_MODEL:3 * D_MODEL]
    x = _token_tile(i, MERGE_CTX_TILES, x_refs) + gate * y
    o_ref[...] = x
    _route(x, mod_ref, ffn_gain_ref.at[pl.ds(layer, 1)], wrt_ref, brt_ref, tri_ref, hx_ref, bucket_ref, rank_ref,
           cnt_out_ref, cnt_ref, cpad_ref)


def merge_branches(xs, mod3, ya, yb_ctx, yb_lat, yc_ctx, yc_lat, gates, wa, wb, wc, wo, ffn_gain, w_rt, b_rt, layer):
    assert MERGE_TM == RT_TM
    row = lambda i: (i, 0)
    const = lambda shape: pl.BlockSpec(shape, lambda i: (0,) * len(shape))
    tri = jnp.asarray(np.triu(np.ones((RT_TM, RT_TM), np.float32)), BF16)
    hbm = pl.BlockSpec(memory_space=pl.ANY)
    x_specs = ([pl.BlockSpec((MERGE_TM, D_MODEL), row)] if len(xs) == 1
               else _split_specs((MERGE_TM, D_MODEL), MERGE_CTX_TILES))
    return pl.pallas_call(
        functools.partial(_merge_kernel, layer, len(xs)),
        grid=(N_TOK // MERGE_TM,),
        in_specs=x_specs + [
            _mod_spec(layer, MERGE_TM),
            pl.BlockSpec((MERGE_TM, SGU_WIDTH), row),
            *_split_specs((MERGE_TM, LRU_WIDTH), MERGE_CTX_TILES),
            *_split_specs((MERGE_TM, NA_WIDTH), MERGE_CTX_TILES),
            pl.BlockSpec((MERGE_TM, 3 * D_MODEL), row),
            hbm, hbm, hbm, hbm,
            _stacked_rows_spec(D_MODEL),
            _layer_spec((RT_ROWS, D_MODEL), layer),
            _layer_spec((RT_ROWS, 1), layer),
            const((RT_TM, RT_TM)),
        ],
        out_specs=[
            pl.BlockSpec((MERGE_TM, D_MODEL), row),
            pl.BlockSpec((RT_TM, HX_W), row),
            pl.BlockSpec((1, RT_TM), lambda i: (0, i)),
            pl.BlockSpec((1, RT_TM), lambda i: (0, i)),
            const((CNT_ROWS, COMB_W)),
        ],
        out_shape=[
            jax.ShapeDtypeStruct((N_TOK, D_MODEL), F32),
            jax.ShapeDtypeStruct((N_TOK, HX_W), F32),
            jax.ShapeDtypeStruct((1, N_TOK), jnp.int32),
            jax.ShapeDtypeStruct((1, N_TOK), jnp.int32),
            jax.ShapeDtypeStruct((CNT_ROWS, COMB_W), jnp.int32),
        ],
        scratch_shapes=[pltpu.VMEM((3, BRANCH_W, D_MODEL), BF16), pltpu.VMEM((D_MODEL, D_MODEL), BF16),
                        pltpu.VMEM((2, BRANCH_W, D_MODEL), F32), pltpu.SemaphoreType.DMA((2,)),
                        pltpu.VMEM((CNT_ROWS, COMB_W), F32), pltpu.VMEM((COMB_W, RT_TM), F32)],
        compiler_params=_params(1),
        name="merge_branches",
    )(*xs, mod3, ya, yb_ctx, yb_lat, yc_ctx, yc_lat, gates, wa, wb, wc, wo, ffn_gain, w_rt, b_rt, tri)


RT_TM = 512
RT_ROWS = 32
RT_EXPERT_ROW = 8
EXPERT_PAIRS = ((0, 1), (0, 2), (0, 3), (1, 2), (1, 3), (2, 3))
N_PAIRS = len(EXPERT_PAIRS)
N_BUCKETS = N_GROUPS * N_PAIRS
CNT_ROWS = 32
COMB_W = 128
HX_W = D_MODEL + COMB_W
EXP_TM = 256
EXP_TILES = N_TOK // EXP_TM + N_BUCKETS
HS_ROWS = EXP_TILES * EXP_TM


def _split_bf16(x):
    hi = x.astype(BF16)
    return hi, (x - hi.astype(F32)).astype(BF16)


def _route(x, mod_ref, gain_ref, w_ref, b_ref, tri_ref, hx_ref, bucket_ref, rank_ref, cnt_out_ref,
           cnt_ref, cpad_ref):
    @pl.when(pl.program_id(0) == 0)
    def _():
        cnt_ref[...] = jnp.zeros_like(cnt_ref)
        cpad_ref[...] = jnp.zeros_like(cpad_ref)

    m = mod_ref[...]
    shift, scale = m[:, 3 * D_MODEL:4 * D_MODEL], m[:, 4 * D_MODEL:5 * D_MODEL]
    h = _rms(x, gain_ref[...]) * (1.0 + scale) + shift
    hx_ref[:, 0:D_MODEL] = h
    h_hi, h_lo = _split_bf16(h)
    w_hi, w_lo = _split_bf16(w_ref[...])
    dims = (((1,), (1,)), ((), ()))
    logits = (lax.dot_general(w_hi, h_hi, dims, preferred_element_type=F32)
              + lax.dot_general(w_hi, h_lo, dims, preferred_element_type=F32)
              + lax.dot_general(w_lo, h_hi, dims, preferred_element_type=F32)) + b_ref[...]
    gl = [logits[g:g + 1, :] for g in range(N_GROUPS)]
    gmax = functools.reduce(jnp.maximum, gl)
    gid = jnp.full(gmax.shape, N_GROUPS - 1, jnp.int32)
    for g in reversed(range(N_GROUPS - 1)):
        gid = jnp.where(gl[g] == gmax, g, gid)
    p_grp = 1.0 / functools.reduce(jnp.add, [jnp.exp(v - gmax) for v in gl])
    el = []
    for e in range(EXPERTS_PER_GROUP):
        v = logits[RT_EXPERT_ROW + e:RT_EXPERT_ROW + e + 1, :]
        for g in range(1, N_GROUPS):
            row = RT_EXPERT_ROW + g * EXPERTS_PER_GROUP + e
            v = jnp.where(gid == g, logits[row:row + 1, :], v)
        el.append(v)
    top1 = functools.reduce(jnp.maximum, el)
    idx1 = jnp.full(top1.shape, EXPERTS_PER_GROUP - 1, jnp.int32)
    for e in reversed(range(EXPERTS_PER_GROUP - 1)):
        idx1 = jnp.where(el[e] == top1, e, idx1)
    rest = [jnp.where(idx1 == e, -jnp.inf, el[e]) for e in range(EXPERTS_PER_GROUP)]
    top2 = functools.reduce(jnp.maximum, rest)
    idx2 = jnp.full(top1.shape, EXPERTS_PER_GROUP - 1, jnp.int32)
    for e in reversed(range(EXPERTS_PER_GROUP - 1)):
        idx2 = jnp.where(rest[e] == top2, e, idx2)
    e2 = jnp.exp(top2 - top1)
    w1 = p_grp / (1.0 + e2)
    w2 = p_grp * e2 / (1.0 + e2)
    for e in range(EXPERTS_PER_GROUP):
        cpad_ref[e:e + 1, :] = jnp.where(idx1 == e, w1, 0.0) + jnp.where(idx2 == e, w2, 0.0)
    hx_ref[:, D_MODEL:] = cpad_ref[...].T
    lo = jnp.minimum(idx1, idx2)
    hi = jnp.maximum(idx1, idx2)
    pair = jnp.where(lo == 0, 0, jnp.where(lo == 1, 3, 5)) + (hi - lo - 1)
    bucket = gid * N_PAIRS + pair
    bucket_ref[...] = bucket
    sub = lax.broadcasted_iota(jnp.int32, (CNT_ROWS, RT_TM), 0)
    onehot = jnp.where(sub == bucket, 1.0, 0.0)
    seen = jnp.dot(onehot.astype(BF16), tri_ref[...], preferred_element_type=F32)
    cnt = cnt_ref[...]
    rank_ref[...] = jnp.sum(onehot * (seen - 1.0 + cnt[:, 0:1]), axis=0, keepdims=True).astype(jnp.int32)
    cnt = cnt + jnp.sum(onehot, axis=1, keepdims=True)
    cnt_ref[...] = cnt
    cnt_out_ref[...] = cnt.astype(jnp.int32)


def _router_weights(w_grp, b_grp, w_exp, b_exp):
    pad = lambda v, n: jnp.zeros(v.shape[:1] + (n,) + v.shape[2:], F32)
    gap, tail = RT_EXPERT_ROW - N_GROUPS, RT_ROWS - RT_EXPERT_ROW - N_EXPERTS
    w_grp, w_exp = jnp.swapaxes(w_grp, 1, 2), jnp.swapaxes(w_exp, 1, 2)
    w = jnp.concatenate([w_grp, pad(w_grp, gap), w_exp, pad(w_exp, tail)], axis=1)
    b_grp, b_exp = b_grp[:, :, None], b_exp[:, :, None]
    b = jnp.concatenate([b_grp, pad(b_grp, gap), b_exp, pad(b_exp, tail)], axis=1)
    return w, b


DISP_TM = 512


ROW_GROUP = 64


def _for_each_row(n_rows, fn):
    def group(k, carry):
        g0 = pl.multiple_of(k * ROW_GROUP, ROW_GROUP)
        for u in range(ROW_GROUP):
            fn(g0, u)
        return carry

    lax.fori_loop(0, n_rows // ROW_GROUP, group, 0)


def _dispatch_kernel(pos_ref, start_ref, cnt_ref, nt_ref, hx_hbm, hs_ref, buf_ref, zero_ref, blk_sem, row_sem, zsem):
    i = pl.program_id(0)
    last = pl.num_programs(0) - 1
    slot = i % 2
    base = i * DISP_TM

    def tile_in(tile, s):
        return pltpu.make_async_copy(hx_hbm.at[pl.ds(pl.multiple_of(tile * DISP_TM, DISP_TM), DISP_TM), :],
                                     buf_ref.at[s], blk_sem.at[s])

    def wait_rows(s):
        pltpu.make_async_copy(buf_ref.at[s], hs_ref.at[pl.ds(0, DISP_TM), :], row_sem.at[s]).wait()

    @pl.when(i == 0)
    def _():
        zero_ref[...] = jnp.zeros_like(zero_ref)

        def tile_copy(t):
            return pltpu.make_async_copy(zero_ref, hs_ref.at[pl.ds(pl.multiple_of(t * EXP_TM, EXP_TM), EXP_TM), :],
                                         zsem)

        def last_tile(g):
            return (start_ref[g] + cnt_ref[g, 0] - 1) // EXP_TM

        def each_tile(fn):
            for g in range(N_BUCKETS):
                pl.when(cnt_ref[g, 0] > 0)(functools.partial(fn, last_tile(g)))
            lax.fori_loop(nt_ref[0], EXP_TILES, lambda t, carry: (fn(t), carry)[1], 0)

        each_tile(lambda t: tile_copy(t).start())
        each_tile(lambda t: tile_copy(t).wait())
        tile_in(0, 0).start()

    @pl.when(i > 0)
    def _():
        wait_rows(1 - slot)

    @pl.when(i < last)
    def _():
        tile_in(i + 1, 1 - slot).start()

    tile_in(i, slot).wait()

    def issue(g0, u):
        src = buf_ref.at[slot, pl.ds(g0, ROW_GROUP), :]
        pltpu.make_async_copy(src.at[pl.ds(u, 1), :], hs_ref.at[pl.ds(pos_ref[base + g0 + u], 1), :],
                              row_sem.at[slot]).start(priority=u % 2)

    _for_each_row(DISP_TM, issue)

    @pl.when(i == last)
    def _():
        wait_rows(slot)


def dispatch(pos, starts, counts, n_tiles, hx):
    return pl.pallas_call(
        _dispatch_kernel,
        grid_spec=pltpu.PrefetchScalarGridSpec(
            num_scalar_prefetch=4,
            grid=(N_TOK // DISP_TM,),
            in_specs=[pl.BlockSpec(memory_space=pl.ANY)],
            out_specs=pl.BlockSpec(memory_space=pl.ANY),
            scratch_shapes=[pltpu.VMEM((2, DISP_TM, HX_W), F32), pltpu.VMEM((EXP_TM, HX_W), F32),
                            pltpu.SemaphoreType.DMA((2,)), pltpu.SemaphoreType.DMA((2,)),
                            pltpu.SemaphoreType.DMA(())],
        ),
        out_shape=jax.ShapeDtypeStruct((HS_ROWS, HX_W), F32),
        compiler_params=_params(1),
        name="dispatch",
    )(pos, starts, counts, n_tiles, hx)


(T_GROUP, T_LO, T_HI, T_SLOT, T_RUN_POS, T_NEXT_GROUP, T_FIRST, T_PREFETCHED,
 T_PENDING_GROUP, T_PENDING_EXPERT, T_PENDING_SLOT) = range(11)


def _experts_kernel(layer, tab_ref, nt_ref, hs_ref, w1_hbm, w3_hbm, w2_hbm, ys_ref,
                    w1_ref, w3_ref, w2_ref, st1_ref, st3_ref, st2_ref, sem_ref):
    t = pl.program_id(0)
    valid = t < nt_ref[0]
    group = tab_ref[T_GROUP, t]
    slot = tab_ref[T_SLOT, t]
    run_pos = tab_ref[T_RUN_POS, t]
    next_group = tab_ref[T_NEXT_GROUP, t]
    pending_expert = tab_ref[T_PENDING_EXPERT, t]

    def expert_copies(grp, e):
        idx = grp * EXPERTS_PER_GROUP + e
        return (pltpu.make_async_copy(w1_hbm.at[layer, idx], st1_ref, sem_ref.at[0]),
                pltpu.make_async_copy(w3_hbm.at[layer, idx], st3_ref, sem_ref.at[1]),
                pltpu.make_async_copy(w2_hbm.at[layer, idx], st2_ref, sem_ref.at[2]))

    def finish(copies, dst_slot, e):
        for cp in copies:
            cp.wait()
        w1_ref[dst_slot, e] = st1_ref[...].astype(BF16)
        w3_ref[dst_slot, e] = st3_ref[...].astype(BF16)
        w2_ref[dst_slot, e] = st2_ref[...].astype(BF16)

    @pl.when(jnp.logical_and(valid, pending_expert >= 0))
    def _():
        finish(expert_copies(tab_ref[T_PENDING_GROUP, t], pending_expert), tab_ref[T_PENDING_SLOT, t], pending_expert)

    @pl.when(jnp.logical_and(valid, tab_ref[T_FIRST, t] == 1))
    def _():
        def load(e, carry):
            copies = expert_copies(group, e)
            for cp in copies:
                cp.start()
            finish(copies, slot, e)
            return carry

        lax.fori_loop(tab_ref[T_PREFETCHED, t], EXPERTS_PER_GROUP, load, 0)

    prefetch = jnp.logical_and(valid, jnp.logical_and(next_group >= 0, run_pos < EXPERTS_PER_GROUP))

    @pl.when(prefetch)
    def _():
        for cp in expert_copies(next_group, run_pos):
            cp.start()

    @pl.when(valid)
    def _():
        h = hs_ref[:, 0:D_MODEL].astype(BF16)
        c = hs_ref[:, D_MODEL:]
        lane = lax.broadcasted_iota(jnp.int32, c.shape, 1)
        acc = None
        for e in (tab_ref[T_LO, t], tab_ref[T_HI, t]):
            ce = jnp.sum(jnp.where(lane == e, c, 0.0), axis=1, keepdims=True)
            a = jnp.dot(h, w1_ref[slot, e], preferred_element_type=F32)
            b = jnp.dot(h, w3_ref[slot, e], preferred_element_type=F32)
            hid = (a * _sigmoid(a)) * b * ce
            y = jnp.dot(hid.astype(BF16), w2_ref[slot, e], preferred_element_type=F32)
            acc = y if acc is None else acc + y
        ys_ref[...] = acc

    @pl.when(jnp.logical_not(valid))
    def _():
        ys_ref[...] = jnp.zeros_like(ys_ref)


def experts(tile_table, n_tiles, hs, w1, w3, w2, layer):
    hbm = pl.BlockSpec(memory_space=pl.ANY)
    return pl.pallas_call(
        functools.partial(_experts_kernel, layer),
        grid_spec=pltpu.PrefetchScalarGridSpec(
            num_scalar_prefetch=2,
            grid=(EXP_TILES,),
            in_specs=[pl.BlockSpec((EXP_TM, HX_W), lambda t, *_: (t, 0)), hbm, hbm, hbm],
            out_specs=pl.BlockSpec((EXP_TM, D_MODEL), lambda t, *_: (t, 0)),
            scratch_shapes=[
                pltpu.VMEM((2, EXPERTS_PER_GROUP, D_MODEL, D_EXPERT), BF16),
                pltpu.VMEM((2, EXPERTS_PER_GROUP, D_MODEL, D_EXPERT), BF16),
                pltpu.VMEM((2, EXPERTS_PER_GROUP, D_EXPERT, D_MODEL), BF16),
                pltpu.VMEM((D_MODEL, D_EXPERT), F32),
                pltpu.VMEM((D_MODEL, D_EXPERT), F32),
                pltpu.VMEM((D_EXPERT, D_MODEL), F32),
                pltpu.SemaphoreType.DMA((3,)),
            ],
        ),
        out_shape=jax.ShapeDtypeStruct((HS_ROWS, D_MODEL), F32),
        compiler_params=_params(1),
        name="experts",
    )(tile_table, n_tiles, hs, w1, w3, w2)


N_TABLE_ROWS = T_PENDING_SLOT + 1
EXP_TM_LOG2 = EXP_TM.bit_length() - 1


def _schedule_kernel(cnt_ref, starts_ref, tab_ref, nt_ref):
    i32 = jnp.int32
    t = i32(0)
    for b in range(N_BUCKETS):
        n = lax.shift_right_logical(cnt_ref[b, 0] + (EXP_TM - 1), EXP_TM_LOG2)
        starts_ref[b] = t * EXP_TM

        def fill(j, carry, t=t, group=b // N_PAIRS, pair=EXPERT_PAIRS[b % N_PAIRS]):
            tab_ref[T_GROUP, t + j] = group
            tab_ref[T_LO, t + j] = pair[0]
            tab_ref[T_HI, t + j] = pair[1]
            return carry

        lax.fori_loop(0, n, fill, 0)
        t = t + n
    n_tiles = t
    nt_ref[0] = n_tiles

    def unused(k, carry):
        for row, value in ((T_GROUP, N_GROUPS - 1), (T_LO, 0), (T_HI, 1), (T_SLOT, 0), (T_RUN_POS, 0),
                           (T_NEXT_GROUP, -1), (T_FIRST, 0), (T_PREFETCHED, 0), (T_PENDING_GROUP, 0),
                           (T_PENDING_EXPERT, -1), (T_PENDING_SLOT, 0)):
            tab_ref[row, k] = value
        return carry

    lax.fori_loop(n_tiles, EXP_TILES, unused, 0)

    def back(k, carry):
        next_group, group_after = carry
        tile = n_tiles - 1 - k
        group = tab_ref[T_GROUP, tile]
        next_group = jnp.where(k == 0, -1, jnp.where(group != group_after, group_after, next_group))
        tab_ref[T_NEXT_GROUP, tile] = next_group
        return next_group, group

    lax.fori_loop(0, n_tiles, back, (i32(-1), i32(-1)))

    def forward(tile, carry):
        prev_group, run, run_start, prev_len, prev_fetch, prev_pos, prev_next, prev_slot = carry
        group = tab_ref[T_GROUP, tile]
        first = jnp.logical_or(tile == 0, group != prev_group)
        prev_len = jnp.where(first, tile - run_start, prev_len)
        run = jnp.where(first, run + 1, run)
        run_start = jnp.where(first, tile, run_start)
        slot = lax.rem(run, 2)
        run_pos = tile - run_start
        next_group = tab_ref[T_NEXT_GROUP, tile]
        tab_ref[T_FIRST, tile] = first.astype(i32)
        tab_ref[T_SLOT, tile] = slot
        tab_ref[T_RUN_POS, tile] = run_pos
        tab_ref[T_PREFETCHED, tile] = jnp.minimum(prev_len, EXPERTS_PER_GROUP)
        tab_ref[T_PENDING_EXPERT, tile] = jnp.where(prev_fetch == 1, prev_pos, -1)
        tab_ref[T_PENDING_GROUP, tile] = prev_next
        tab_ref[T_PENDING_SLOT, tile] = 1 - prev_slot
        fetch = jnp.logical_and(next_group >= 0, run_pos < EXPERTS_PER_GROUP).astype(i32)
        return group, run, run_start, prev_len, fetch, run_pos, next_group, slot

    lax.fori_loop(0, n_tiles, forward, (i32(-1), i32(-1), i32(0), i32(0), i32(0), i32(0), i32(0), i32(0)))


def expert_schedule(counts):
    smem = pl.BlockSpec(memory_space=pltpu.SMEM)
    return pl.pallas_call(
        _schedule_kernel,
        in_specs=[smem],
        out_specs=[smem, smem, smem],
        out_shape=[jax.ShapeDtypeStruct((N_BUCKETS,), jnp.int32),
                   jax.ShapeDtypeStruct((N_TABLE_ROWS, EXP_TILES), jnp.int32),
                   jax.ShapeDtypeStruct((1,), jnp.int32)],
        name="expert_schedule",
    )(counts)


RES_TM = 512
RES_CTX_TILES = N_CTX // RES_TM


def _moe_residual_kernel(final, pos_ref, x_ref, mod_ref, gain_ref, ys_ref, *refs):
    out_refs, (ybuf_ref, sem_ref) = refs[:-2], refs[-2:]
    i = pl.program_id(0)
    slot = i % 2

    def gather_tile(tile, tile_slot):
        def issue(g0, u):
            dst = ybuf_ref.at[tile_slot, pl.ds(g0, ROW_GROUP), :]
            pltpu.make_async_copy(ys_ref.at[pl.ds(pos_ref[tile * RES_TM + g0 + u], 1), :], dst.at[pl.ds(u, 1), :],
                                  sem_ref.at[tile_slot]).start(priority=u % 2)

        _for_each_row(RES_TM, issue)

    @pl.when(i == 0)
    def _():
        gather_tile(0, 0)

    @pl.when(i + 1 < pl.num_programs(0))
    def _():
        gather_tile(i + 1, 1 - slot)

    pltpu.make_async_copy(ys_ref.at[pl.ds(0, RES_TM), :], ybuf_ref.at[slot], sem_ref.at[slot]).wait()
    x = x_ref[...] + mod_ref[:, 5 * D_MODEL:6 * D_MODEL] * ybuf_ref[slot]
    if not final:
        out_refs[0][...] = x
        return
    y = _rms(x, gain_ref[...])

    @pl.when(i < RES_CTX_TILES)
    def _():
        out_refs[0][...] = y

    @pl.when(i >= RES_CTX_TILES)
    def _():
        out_refs[1][...] = y


def moe_residual(pos, x, mod4, gain, ys, layer, final):
    if final:
        out_specs = _split_specs((RES_TM, D_MODEL), RES_CTX_TILES)
        out_shape = [jax.ShapeDtypeStruct((N_CTX, D_MODEL), F32), jax.ShapeDtypeStruct((N_LAT, D_MODEL), F32)]
    else:
        out_specs = pl.BlockSpec((RES_TM, D_MODEL), lambda i, *_: (i, 0))
        out_shape = jax.ShapeDtypeStruct((N_TOK, D_MODEL), F32)
    return pl.pallas_call(
        functools.partial(_moe_residual_kernel, final),
        grid_spec=pltpu.PrefetchScalarGridSpec(
            num_scalar_prefetch=1,
            grid=(N_TOK // RES_TM,),
            in_specs=[
                pl.BlockSpec((RES_TM, D_MODEL), lambda i, *_: (i, 0)),
                _mod_spec(layer, RES_TM),
                pl.BlockSpec((1, D_MODEL), lambda i, *_: (0, 0)),
                pl.BlockSpec(memory_space=pl.ANY),
            ],
            out_specs=out_specs,
            scratch_shapes=[pltpu.VMEM((2, RES_TM, D_MODEL), F32), pltpu.SemaphoreType.DMA((2,))],
        ),
        out_shape=out_shape,
        compiler_params=_params(1),
        name="moe_residual_final" if final else "moe_residual",
    )(pos, x, mod4, gain, ys)


def kernel(x_prompt, x_sample, cache_k, cache_v, state_lru, c, c_ctx, w_mod, b_mod, norm_mix, norm_ffn, w_in, sgu_norm, sgu_w, sgu_b, lru_conv_w, lru_conv_b, lru_w_r, lru_b_r, lru_w_i, lru_b_i, lru_lambda, na_rpb, w_branch_sgu, w_branch_lru, w_branch_na, w_out, moe_w_group, moe_b_group, moe_w_expert, moe_b_expert, moe_w1, moe_w3, moe_w2, final_norm_gain):
    xs = (x_prompt.reshape(N_CTX, D_MODEL), x_sample.reshape(N_LAT, D_MODEL))
    cond = jnp.zeros((N_COND, D_MODEL), F32).at[0].set(c_ctx).at[1:1 + DEC_BATCH].set(c)
    mod = modulation(cond, w_mod, b_mod)
    zero_state = jnp.zeros((BATCH, 2, LRU_WIDTH), F32)
    feature_major = lambda t: jnp.transpose(t, (0, 1, 3, 4, 2)).reshape(DEC_BATCH, DEPTH, NA_WIDTH, PAST_LEN)
    cache_k, cache_v = feature_major(cache_k), feature_major(cache_v)
    final_gain = final_norm_gain.reshape(1, D_MODEL)
    sgu_b_t = jnp.swapaxes(sgu_b, 1, 2)
    w_gate = _block_diag_gate_weights(lru_w_r, lru_w_i)
    rpb = na_rpb.reshape(DEPTH, NA_HEADS * N_DROW * N_DCOL)
    w_rt, b_rt = _router_weights(moe_w_group, moe_b_group, moe_w_expert, moe_b_expert)
    caches, ss = [], []
    for l in range(DEPTH):
        ya, za, q, kb, vb, k_ctx, v_ctx, gates = in_projection(
            xs, mod, norm_mix, w_in, sgu_norm, sgu_w, sgu_b_t, l, prev_caches=caches if l == DEPTH - 1 else ())
        caches.append((k_ctx, v_ctx))
        lru_args = (lru_conv_w, lru_conv_b, w_gate, lru_b_r, lru_b_i, lru_lambda)
        yb_ctx, st_ctx = rglru(za, *lru_args, zero_state, pl.BlockSpec((None, 2, LRU_WIDTH), lambda i: (i, 0, 0)),
                               SEQ, BATCH, 0, l)
        yb_lat, _ = rglru(za, *lru_args, state_lru,
                          pl.BlockSpec((None, None, 2, LRU_WIDTH), lambda i, l=l: (i, l, 0, 0)),
                          DEC_SEQ, DEC_BATCH, N_CTX, l)
        yc_ctx = context_attention(q, kb, vb)
        yc_lat = latent_attention(q, kb, vb, cache_k, cache_v, rpb, l)
        x, hx, bucket, rank, counts = merge_branches(
            xs, mod, ya, yb_ctx, yb_lat, yc_ctx, yc_lat, gates, w_branch_sgu, w_branch_lru, w_branch_na, w_out,
            norm_ffn, w_rt, b_rt, l)
        bucket, rank = bucket.reshape(N_TOK), rank.reshape(N_TOK)
        starts, tile_table, n_tiles = expert_schedule(counts)
        pos = jnp.sum(jnp.where(bucket[:, None] == jnp.arange(N_BUCKETS), starts[None, :] + rank[:, None], 0), axis=1)
        hs = dispatch(pos, starts, counts, n_tiles, hx)
        ys = experts(tile_table, n_tiles, hs, moe_w1, moe_w3, moe_w2, l)
        if l < DEPTH - 1:
            xs = (moe_residual(pos, x, mod, final_gain, ys, l, False),)
        else:
            y_ctx, y_lat = moe_residual(pos, x, mod, final_gain, ys, l, True)
        ss.append(st_ctx)
    token_major = lambda t: jnp.transpose(t.reshape(BATCH, DEPTH, NA_HEADS, NA_HEAD_DIM, SEQ), (0, 1, 4, 2, 3))
    return (y_ctx.reshape(BATCH, SEQ, D_MODEL), y_lat.reshape(DEC_BATCH, DEC_SEQ, D_MODEL),
            token_major(k_ctx), token_major(v_ctx), jnp.stack(ss, axis=1))
```

```python
import functools

import jax
import jax.numpy as jnp
import numpy as np
from jax import lax
from jax.experimental import pallas as pl
from jax.experimental.pallas import tpu as pltpu

F32 = jnp.float32
BF16 = jnp.bfloat16

D_MODEL = 1024
BATCH = 16
SEQ = 256
DEPTH = 2
DEC_BATCH = 4
DEC_SEQ = 2048
PAST_LEN = 512
GRID_W = 64
CHUNK = 128
SGU_WIDTH = 512
SGU_GROUPS = 4
LRU_WIDTH = 512
LRU_BLOCKS = 8
CONV_WIDTH = 4
LRU_C = 8.0
NA_HEADS = 8
NA_HEAD_DIM = 64
NA_WIDTH = 512
WIN_ROWS = 8
WIN_COLS = 16
N_GROUPS = 4
EXPERTS_PER_GROUP = 4
N_EXPERTS = 16
D_EXPERT = 512
IN_WIDTH = 6656
EPS = 1e-6
NEG_INF = -1e30

N_CTX = BATCH * SEQ
N_LAT = DEC_BATCH * DEC_SEQ
N_TOK = N_CTX + N_LAT
N_COND = 8
MOD_WIDTH = 6 * D_MODEL
GRID_ROWS = DEC_SEQ // GRID_W

VMEM_LIMIT_BYTES = 56 * 1024 * 1024


def _params(n_axes):
    return pltpu.CompilerParams(dimension_semantics=("arbitrary",) * n_axes,
                                vmem_limit_bytes=VMEM_LIMIT_BYTES)


def _cond_row(tile, tile_rows):
    tok = tile * tile_rows
    return jnp.where(tok < N_CTX, 0, 1 + (tok - N_CTX) // DEC_SEQ)


def _layer_spec(shape, layer):
    return pl.BlockSpec((None,) + tuple(shape), lambda i, *_: (layer,) + (0,) * len(shape))


def _stacked_rows_spec(width):
    return pl.BlockSpec((DEPTH, width), lambda i, *_: (0, 0))


def _mod_spec(layer, tile_rows):
    return pl.BlockSpec((None, None, 1, MOD_WIDTH), lambda i, *_: (layer, _cond_row(i, tile_rows), 0, 0))


def _rms(x, gain):
    return x * lax.rsqrt(jnp.mean(x * x, axis=-1, keepdims=True) + EPS) * gain


def _bdot(a, b):
    return jnp.dot(a.astype(BF16), b.astype(BF16), preferred_element_type=F32)


def _bdot_t(a, b):
    return lax.dot_general(a.astype(BF16), b.astype(BF16), (((1,), (1,)), ((), ())),
                           preferred_element_type=F32)


MOD_TN = 1536


def _mod_kernel(cond_ref, w_ref, b_ref, o_ref):
    c = cond_ref[...]
    s = c * jax.nn.sigmoid(c)
    mod = _bdot(s, w_ref[...]) + b_ref[pl.ds(pl.program_id(0), 1), :]
    for r in range(N_COND):
        o_ref[r] = mod[r:r + 1, :]


def modulation(cond, w_mod, b_mod):
    return pl.pallas_call(
        _mod_kernel,
        grid=(DEPTH, MOD_WIDTH // MOD_TN),
        in_specs=[
            pl.BlockSpec((N_COND, D_MODEL), lambda l, j: (0, 0)),
            pl.BlockSpec((None, D_MODEL, MOD_TN), lambda l, j: (l, 0, j)),
            pl.BlockSpec((DEPTH, MOD_TN), lambda l, j: (0, j)),
        ],
        out_specs=pl.BlockSpec((None, N_COND, 1, MOD_TN), lambda l, j: (l, 0, 0, j)),
        out_shape=jax.ShapeDtypeStruct((DEPTH, N_COND, 1, MOD_WIDTH), F32),
        compiler_params=_params(2),
        name="modulation",
    )(cond, w_mod, b_mod)


IN_TM = 256
IN_CTX_TILES = N_CTX // IN_TM
ZA_WIDTH = 4 * 512
KV_OFF = ZA_WIDTH + NA_WIDTH
GATE_OFF = KV_OFF + 2 * NA_WIDTH
W_CHUNK = 512
Q_SCALE = NA_HEAD_DIM ** -0.5 * float(np.log2(np.e))


def _load_cast(chunks, stage_ref, sem_ref):
    def copy(j):
        return pltpu.make_async_copy(chunks[j][0], stage_ref.at[j % 2], sem_ref.at[j % 2])

    copy(0).start()
    for j in range(len(chunks)):
        if j + 1 < len(chunks):
            copy(j + 1).start()
        copy(j).wait()
        chunks[j][1](stage_ref[j % 2].astype(BF16))


def _token_tile(i, n_ctx_tiles, refs):
    if len(refs) == 1:
        return refs[0][...]
    return jnp.where(i < n_ctx_tiles, refs[0][...], refs[1][...])


def _split_specs(block, n_ctx_tiles):
    return [pl.BlockSpec(block, lambda i, *_: (jnp.minimum(i, n_ctx_tiles - 1), 0)),
            pl.BlockSpec(block, lambda i, *_: (jnp.maximum(i - n_ctx_tiles, 0), 0))]


SGU_GD = SGU_WIDTH // SGU_GROUPS


def _spatial_gating(u, v, gain, ws_ref, bs_ref):
    u = jax.nn.gelu(u)
    v = _rms(jax.nn.gelu(v), gain).astype(BF16)
    out = []
    for g in range(SGU_GROUPS):
        cols = slice(g * SGU_GD, (g + 1) * SGU_GD)
        mixed = jnp.dot(ws_ref[g].astype(BF16), v[:, cols], preferred_element_type=F32) + bs_ref[:, g:g + 1]
        out.append(u[:, cols] * mixed)
    return jnp.concatenate(out, axis=1)


def _inproj_kernel(layer, n_x, n_prev, *refs):
    x_refs, (mod_ref, gain_ref, w_hbm, sgu_gain_ref, ws_ref, bs_ref) = refs[:n_x], refs[n_x:n_x + 6]
    prev_refs, rest = refs[n_x + 6:n_x + 6 + 2 * n_prev], refs[n_x + 6 + 2 * n_prev:]
    ya_ref, zb_ref, q_ref, kb_ref, vb_ref, kc_ref, vc_ref, g_ref, w_ref, stage_ref, sem_ref = rest
    i = pl.program_id(0)

    @pl.when(i == 0)
    def _():
        def store(c):
            def st(v):
                w_ref[:, c * W_CHUNK:(c + 1) * W_CHUNK] = v
            return st

        _load_cast([(w_hbm.at[layer, :, pl.ds(c * W_CHUNK, W_CHUNK)], store(c)) for c in range(IN_WIDTH // W_CHUNK)],
                   stage_ref, sem_ref)

    m = mod_ref[...]
    shift, scale = m[:, 0:D_MODEL], m[:, D_MODEL:2 * D_MODEL]
    gain = gain_ref[layer:layer + 1, :]
    h = (_rms(_token_tile(i, IN_CTX_TILES, x_refs), gain) * (1.0 + scale) + shift).astype(BF16)
    za = jnp.dot(h, w_ref[:, 0:ZA_WIDTH], preferred_element_type=F32)
    for c in range(IN_TM // CHUNK):
        rows = slice(c * CHUNK, (c + 1) * CHUNK)
        ya_ref[rows, :] = _spatial_gating(za[rows, 0:SGU_WIDTH], za[rows, SGU_WIDTH:2 * SGU_WIDTH],
                                          sgu_gain_ref[layer:layer + 1, :], ws_ref, bs_ref).astype(BF16)
    zb_ref[:, 0:LRU_WIDTH] = za[:, 2 * SGU_WIDTH:2 * SGU_WIDTH + LRU_WIDTH].astype(BF16)
    zb_ref[:, LRU_WIDTH:] = jax.nn.gelu(za[:, 2 * SGU_WIDTH + LRU_WIDTH:]).astype(BF16)
    q_ref[...] = (jnp.dot(h, w_ref[:, ZA_WIDTH:KV_OFF], preferred_element_type=F32) * Q_SCALE).astype(BF16)
    kv = jnp.dot(h, w_ref[:, KV_OFF:GATE_OFF], preferred_element_type=F32)
    kb_ref[...] = kv[:, 0:NA_WIDTH].astype(BF16)
    vb_ref[...] = kv[:, NA_WIDTH:].astype(BF16)
    g_ref[...] = _sigmoid(jnp.dot(h, w_ref[:, GATE_OFF:], preferred_element_type=F32)).astype(BF16)

    @pl.when(i < IN_CTX_TILES)
    def _():
        if n_prev == 0:
            kc_ref[...] = kv[:, 0:NA_WIDTH].T
            vc_ref[...] = kv[:, NA_WIDTH:].T
        else:
            for p in range(n_prev):
                kc_ref[p] = prev_refs[2 * p][...]
                vc_ref[p] = prev_refs[2 * p + 1][...]
            kc_ref[n_prev] = kv[:, 0:NA_WIDTH].T
            vc_ref[n_prev] = kv[:, NA_WIDTH:].T


def in_projection(xs, mod3, gain, w_in, sgu_gain, sgu_w, sgu_b_t, layer, prev_caches=()):
    assert IN_TM == SEQ
    row = lambda i: (i, 0)
    x_specs = ([pl.BlockSpec((IN_TM, D_MODEL), row)] if len(xs) == 1
               else _split_specs((IN_TM, D_MODEL), IN_CTX_TILES))
    n_prev = len(prev_caches)
    layer_cache_spec = pl.BlockSpec((None, NA_WIDTH, SEQ), lambda i: (jnp.minimum(i, IN_CTX_TILES - 1), 0, 0))
    if n_prev:
        cache_spec = pl.BlockSpec((None, n_prev + 1, NA_WIDTH, SEQ),
                                  lambda i: (jnp.minimum(i, IN_CTX_TILES - 1), 0, 0, 0))
        cache_shape = jax.ShapeDtypeStruct((BATCH, n_prev + 1, NA_WIDTH, SEQ), F32)
    else:
        cache_spec = layer_cache_spec
        cache_shape = jax.ShapeDtypeStruct((BATCH, NA_WIDTH, SEQ), F32)
    return pl.pallas_call(
        functools.partial(_inproj_kernel, layer, len(xs), n_prev),
        grid=(N_TOK // IN_TM,),
        in_specs=x_specs + [
            _mod_spec(layer, IN_TM),
            _stacked_rows_spec(D_MODEL),
            pl.BlockSpec(memory_space=pl.ANY),
            _stacked_rows_spec(SGU_WIDTH),
            _layer_spec((SGU_GROUPS, CHUNK, CHUNK), layer),
            _layer_spec((CHUNK, SGU_GROUPS), layer),
        ] + [layer_cache_spec] * (2 * n_prev),
        out_specs=[
            pl.BlockSpec((IN_TM, SGU_WIDTH), row),
            pl.BlockSpec((IN_TM, 2 * LRU_WIDTH), row),
            pl.BlockSpec((IN_TM, NA_WIDTH), row),
            pl.BlockSpec((IN_TM, NA_WIDTH), row),
            pl.BlockSpec((IN_TM, NA_WIDTH), row),
            cache_spec,
            cache_spec,
            pl.BlockSpec((IN_TM, 3 * D_MODEL), row),
        ],
        out_shape=[
            jax.ShapeDtypeStruct((N_TOK, SGU_WIDTH), BF16),
            jax.ShapeDtypeStruct((N_TOK, 2 * LRU_WIDTH), BF16),
            jax.ShapeDtypeStruct((N_TOK, NA_WIDTH), BF16),
            jax.ShapeDtypeStruct((N_TOK, NA_WIDTH), BF16),
            jax.ShapeDtypeStruct((N_TOK, NA_WIDTH), BF16),
            cache_shape,
            cache_shape,
            jax.ShapeDtypeStruct((N_TOK, 3 * D_MODEL), BF16),
        ],
        scratch_shapes=[pltpu.VMEM((D_MODEL, IN_WIDTH), BF16), pltpu.VMEM((2, D_MODEL, W_CHUNK), F32),
                        pltpu.SemaphoreType.DMA((2,))],
        compiler_params=_params(1),
        name="in_projection",
    )(*xs, mod3, gain, w_in, sgu_gain, sgu_w, sgu_b_t, *[c for kv in prev_caches for c in kv])


LRU_TC = 256
LRU_HALF = 256
SUB = 8
HALO = 8
LANES = 128
LRU_SLABS = LRU_WIDTH // LANES


def _sigmoid(x):
    return 0.5 * jnp.tanh(0.5 * x) + 0.5


def _lru_kernel(seq_len, layer, xr_ref, gr_ref, cw_ref, cb_ref, wlo_ref, whi_ref, br_ref, bi_ref,
                lam_ref, h0_ref, y_ref, st_ref, xp_ref, a_ref, b_ref, h_ref):
    seg_len = seq_len // SUB
    pitch = seg_len + SUB
    segs_per_chunk = max(LRU_TC // seg_len, 1)
    n_chunks = seq_len // LRU_TC
    zeros = jnp.zeros((HALO, LRU_WIDTH), F32)
    xp_ref[0:HALO, :] = zeros
    xp_ref[seq_len + HALO:seq_len + 2 * HALO, :] = zeros

    def copy_in(c, carry):
        r0 = pl.multiple_of(c * LRU_TC, LRU_TC)
        xp_ref[pl.ds(r0 + HALO, LRU_TC), :] = xr_ref[pl.ds(r0, LRU_TC), :].astype(F32)
        return carry

    lax.fori_loop(0, n_chunks, copy_in, 0)

    def chunk_rows(c, seg):
        n = min(seg_len, LRU_TC)
        start = pl.multiple_of((c * segs_per_chunk + seg) * pitch, SUB)
        return pl.ds(start, n), slice(seg * n, (seg + 1) * n)

    cw = cw_ref[...]
    cb = cb_ref[layer:layer + 1, :]
    win = LRU_TC + 2 * HALO
    neg_lam = -lam_ref[...]
    softplus = jnp.maximum(neg_lam, 0.0) + jnp.log1p(jnp.exp(-jnp.abs(neg_lam)))
    decay = (0.5 * LRU_C) * softplus
    half_br = 0.5 * br_ref[...]
    half_bi = 0.5 * bi_ref[...]

    def gates(c, carry):
        r0 = pl.multiple_of(c * LRU_TC, LRU_TC)
        w = xp_ref[pl.ds(r0, win), :]
        xc = (cw[0:1, :] * pltpu.roll(w, 1, 0)[HALO:HALO + LRU_TC]
              + cw[1:2, :] * w[HALO:HALO + LRU_TC]
              + cw[2:3, :] * pltpu.roll(w, win - 1, 0)[HALO:HALO + LRU_TC]
              + cw[3:4, :] * pltpu.roll(w, win - 2, 0)[HALO:HALO + LRU_TC]) + cb
        xb = xc.astype(BF16)
        for half, w_ref in enumerate((wlo_ref, whi_ref)):
            cols = slice(half * LRU_HALF, (half + 1) * LRU_HALF)
            pre = jnp.dot(xb[:, cols], w_ref[...], preferred_element_type=F32)
            half_x = 0.5 * xc[:, cols]
            for d in range(2):
                tr = jnp.tanh(pre[:, (2 * d) * LRU_HALF:(2 * d + 1) * LRU_HALF] + half_br[d:d + 1, cols])
                ti = jnp.tanh(pre[:, (2 * d + 1) * LRU_HALF:(2 * d + 2) * LRU_HALF] + half_bi[d:d + 1, cols])
                neg_log_a = decay[d:d + 1, cols] * tr + decay[d:d + 1, cols]
                a = jnp.exp2(neg_log_a * -LOG2E)
                z = jnp.tanh(neg_log_a) * (a * a + 1.0)
                b = jnp.where(z > 0.0, z * lax.rsqrt(z), 0.0) * (half_x * ti + half_x)
                for k in range(LRU_HALF // LANES):
                    slab = half * (LRU_HALF // LANES) + k
                    lanes = slice(k * LANES, (k + 1) * LANES)
                    for seg in range(segs_per_chunk):
                        dst, src = chunk_rows(c, seg)
                        a_ref[d, slab, dst, :] = a[src, lanes]
                        b_ref[d, slab, dst, :] = b[src, lanes]
        return carry

    lax.fori_loop(0, n_chunks, gates, 0)

    def step_rows(d, j):
        return pl.ds(j if d == 0 else seg_len - 1 - j, SUB, stride=pitch)

    def reduce_step(j, carry):
        out = []
        for d in range(2):
            for slab in range(LRU_SLABS):
                big_a, big_b = carry[d * LRU_SLABS + slab]
                a = a_ref[d, slab, step_rows(d, j), :]
                b = b_ref[d, slab, step_rows(d, j), :]
                out.append((a * big_a, a * big_b + b))
        return tuple(out)

    ident = (jnp.ones((SUB, LANES), F32), jnp.zeros((SUB, LANES), F32))
    totals = lax.fori_loop(0, seg_len, reduce_step, (ident,) * (2 * LRU_SLABS), unroll=8)

    sub = lax.broadcasted_iota(jnp.int32, (SUB, LANES), 0)
    entering = []
    for d in range(2):
        for slab in range(LRU_SLABS):
            big_a, big_b = totals[d * LRU_SLABS + slab]
            h = h0_ref[d:d + 1, slab * LANES:(slab + 1) * LANES]
            rows = jnp.zeros((SUB, LANES), F32)
            for s in (range(SUB) if d == 0 else reversed(range(SUB))):
                rows = jnp.where(sub == s, h, rows)
                h = big_a[s:s + 1, :] * h + big_b[s:s + 1, :]
            entering.append(rows)
            st_ref[d:d + 1, slab * LANES:(slab + 1) * LANES] = h

    def apply_step(j, carry):
        out = []
        for d in range(2):
            for slab in range(LRU_SLABS):
                h = (a_ref[d, slab, step_rows(d, j), :] * carry[d * LRU_SLABS + slab]
                     + b_ref[d, slab, step_rows(d, j), :])
                h_ref[d, slab, step_rows(d, j), :] = h
                out.append(h)
        return tuple(out)

    lax.fori_loop(0, seg_len, apply_step, tuple(entering), unroll=8)

    def merge(c, carry):
        r0 = pl.multiple_of(c * LRU_TC, LRU_TC)
        gate = gr_ref[pl.ds(r0, LRU_TC), :].astype(F32)
        for slab in range(LRU_SLABS):
            lanes = slice(slab * LANES, (slab + 1) * LANES)
            for seg in range(segs_per_chunk):
                src, dst = chunk_rows(c, seg)
                h = h_ref[0, slab, src, :] + h_ref[1, slab, src, :]
                y_ref[pl.ds(r0 + dst.start, dst.stop - dst.start), lanes] = (h * gate[dst, lanes]).astype(BF16)
        return carry

    lax.fori_loop(0, n_chunks, merge, 0)


def rglru(za, conv_w, conv_b, w_gate, b_r, b_i, lam, h0, h0_spec, seq_len, n_seq, tok_off, layer):
    blk0 = tok_off // seq_len
    half_spec = lambda half: pl.BlockSpec((None, None, LRU_HALF, 4 * LRU_HALF), lambda i: (layer, half, 0, 0))
    return pl.pallas_call(
        functools.partial(_lru_kernel, seq_len, layer),
        grid=(n_seq,),
        in_specs=[
            pl.BlockSpec((seq_len, LRU_WIDTH), lambda i: (i + blk0, 0)),
            pl.BlockSpec((seq_len, LRU_WIDTH), lambda i: (i + blk0, 1)),
            _layer_spec((CONV_WIDTH, LRU_WIDTH), layer),
            _stacked_rows_spec(LRU_WIDTH),
            half_spec(0),
            half_spec(1),
            _layer_spec((2, LRU_WIDTH), layer),
            _layer_spec((2, LRU_WIDTH), layer),
            _layer_spec((2, LRU_WIDTH), layer),
            h0_spec,
        ],
        out_specs=[
            pl.BlockSpec((seq_len, LRU_WIDTH), lambda i: (i, 0)),
            pl.BlockSpec((None, 2, LRU_WIDTH), lambda i: (i, 0, 0)),
        ],
        out_shape=[
            jax.ShapeDtypeStruct((n_seq * seq_len, LRU_WIDTH), BF16),
            jax.ShapeDtypeStruct((n_seq, 2, LRU_WIDTH), F32),
        ],
        scratch_shapes=[
            pltpu.VMEM((seq_len + 2 * HALO, LRU_WIDTH), F32),
            pltpu.VMEM((2, LRU_SLABS, seq_len + SUB * SUB, LANES), F32),
            pltpu.VMEM((2, LRU_SLABS, seq_len + SUB * SUB, LANES), F32),
            pltpu.VMEM((2, LRU_SLABS, seq_len + SUB * SUB, LANES), F32),
        ],
        compiler_params=_params(1),
        name=f"rglru_{seq_len}",
    )(za, za, conv_w, conv_b, w_gate, w_gate, b_r, b_i, lam, h0)


LRU_BLOCK_W = LRU_WIDTH // LRU_BLOCKS
LRU_HALF_BLOCKS = LRU_BLOCKS // 2


def _gate_weights_kernel(wr_ref, wi_ref, o_ref):
    row = lax.broadcasted_iota(jnp.int32, (LRU_BLOCK_W, LRU_HALF), 0)
    lane = lax.broadcasted_iota(jnp.int32, (LRU_BLOCK_W, LRU_HALF), 1)
    groups = [(half, d, g) for half in range(2) for d in range(2) for g in range(2)]
    for a in range(LRU_HALF_BLOCKS):
        rows = slice(a * LRU_BLOCK_W, (a + 1) * LRU_BLOCK_W)
        place = (lane == row + a * LRU_BLOCK_W).astype(BF16)
        blocks = [(0.5 * (wr_ref, wi_ref)[g][d, half * LRU_HALF_BLOCKS + a]).astype(BF16) for half, d, g in groups]
        strips = jnp.dot(jnp.concatenate(blocks, axis=0), place, preferred_element_type=F32).astype(BF16)
        for n, (half, d, g) in enumerate(groups):
            tile = 2 * d + g
            o_ref[half, rows, tile * LRU_HALF:(tile + 1) * LRU_HALF] = strips[n * LRU_BLOCK_W:(n + 1) * LRU_BLOCK_W]


def _block_diag_gate_weights(w_r, w_i):
    spec = pl.BlockSpec((None, 2, LRU_BLOCKS, LRU_BLOCK_W, LRU_BLOCK_W), lambda l: (l, 0, 0, 0, 0))
    return pl.pallas_call(
        _gate_weights_kernel,
        grid=(DEPTH,),
        in_specs=[spec, spec],
        out_specs=pl.BlockSpec((None, 2, LRU_HALF, 4 * LRU_HALF), lambda l: (l, 0, 0, 0)),
        out_shape=jax.ShapeDtypeStruct((DEPTH, 2, LRU_HALF, 4 * LRU_HALF), BF16),
        compiler_params=_params(1),
        name="gate_weights",
    )(w_r, w_i)


PAIR_W = 2 * NA_HEAD_DIM
LOG2E = float(np.log2(np.e))


def _head_lanes(shape, head):
    lane = lax.broadcasted_iota(jnp.int32, shape, 1)
    return (lane < NA_HEAD_DIM) if head == 0 else (lane >= NA_HEAD_DIM)


def _one_head(q_pair, head):
    return jnp.where(_head_lanes(q_pair.shape, head), q_pair.astype(F32), 0.0).astype(BF16)


def _join_heads(o0, o1):
    return jnp.where(_head_lanes(o0.shape, 0), o0, o1)


CTX_SEQS = 4


def _ctx_attn_kernel(q_ref, k_ref, v_ref, o_ref):
    for seq in range(CTX_SEQS):
        rows = slice(seq * SEQ, (seq + 1) * SEQ)
        for j in range(NA_HEADS // 2):
            cols = slice(j * PAIR_W, (j + 1) * PAIR_W)
            outs = []
            for head in range(2):
                s = _bdot_t(_one_head(q_ref[rows, cols], head), k_ref[rows, cols])
                p = jnp.exp2(s - jnp.max(s, axis=-1, keepdims=True))
                denom = jnp.sum(p, axis=-1, keepdims=True)
                outs.append(jnp.dot(p.astype(BF16), v_ref[rows, cols], preferred_element_type=F32) / denom)
            o_ref[rows, cols] = _join_heads(*outs).astype(BF16)


def context_attention(q, kb, vb):
    spec = pl.BlockSpec((CTX_SEQS * SEQ, NA_WIDTH), lambda i: (i, 0))
    return pl.pallas_call(
        _ctx_attn_kernel,
        grid=(BATCH // CTX_SEQS,),
        in_specs=[spec, spec, spec],
        out_specs=spec,
        out_shape=jax.ShapeDtypeStruct((N_CTX, NA_WIDTH), BF16),
        compiler_params=_params(1),
        name="context_attention",
    )(q, kb, vb)


NA_QROWS = 4
NA_TQ = NA_QROWS * GRID_W
NA_KROWS = NA_QROWS + WIN_ROWS
NA_TK = NA_KROWS * GRID_W
NA_QBLOCKS = GRID_ROWS // NA_QROWS


N_DROW = 2 * WIN_ROWS - 1
N_DCOL = 2 * WIN_COLS - 1
NA_BLOCK_KINDS = (0, 1, NA_QBLOCKS - 1)


def _first_key_row(qb):
    return (np.clip if isinstance(qb, int) else jnp.clip)(qb * NA_QROWS - WIN_ROWS // 2, 0, GRID_ROWS - NA_KROWS)


def _build_bias(layer, rpb_ref, table_ref, bias_ref):
    qc = lax.broadcasted_iota(jnp.int32, (GRID_W, GRID_W), 0)
    kc = lax.broadcasted_iota(jnp.int32, (GRID_W, GRID_W), 1)
    col0 = jnp.clip(qc - WIN_COLS // 2, 0, GRID_W - WIN_COLS)
    col_ok = jnp.logical_and(kc >= col0, kc < col0 + WIN_COLS)
    d_col = jnp.clip(kc - qc, 1 - WIN_COLS, WIN_COLS - 1) + WIN_COLS - 1
    neg = jnp.full((GRID_W, GRID_W), NEG_INF, F32)

    def table_entry(idx, carry):
        t = neg
        for j in range(N_DCOL):
            t = jnp.where(d_col == j, rpb_ref[layer, idx * N_DCOL + j], t)
        table_ref[idx] = jnp.where(col_ok, t * LOG2E, neg)
        return carry

    lax.fori_loop(0, NA_HEADS * N_DROW, table_entry, 0)

    def head_blocks(h, carry):
        for kind, qb in enumerate(NA_BLOCK_KINDS):
            for i in range(NA_QROWS):
                qrow = qb * NA_QROWS + i
                win0 = int(np.clip(qrow - WIN_ROWS // 2, 0, GRID_ROWS - WIN_ROWS))
                for kr in range(NA_KROWS):
                    krow = int(_first_key_row(qb)) + kr
                    inside = win0 <= krow < win0 + WIN_ROWS
                    blk = table_ref[h * N_DROW + (krow - qrow + WIN_ROWS - 1)] if inside else neg
                    bias_ref[kind, h, i * GRID_W:(i + 1) * GRID_W, kr * GRID_W:(kr + 1) * GRID_W] = blk
        return carry

    lax.fori_loop(0, NA_HEADS, head_blocks, 0)


def _lat_attn_kernel(layer, rpb_ref, q_ref, k_ref, v_ref, ck_ref, cv_ref, o_ref, table_ref, bias_ref):
    qb = pl.program_id(1)

    @pl.when(jnp.logical_and(pl.program_id(0) == 0, qb == 0))
    def _():
        _build_bias(layer, rpb_ref, table_ref, bias_ref)

    kind = jnp.where(qb == 0, 0, jnp.where(qb == NA_QBLOCKS - 1, 2, 1))
    k0 = pl.multiple_of(_first_key_row(qb) * GRID_W, GRID_W)
    for j in range(NA_HEADS // 2):
        cols = slice(j * PAIR_W, (j + 1) * PAIR_W)
        k_loc = k_ref[pl.ds(k0, NA_TK), cols]
        v_loc = v_ref[pl.ds(k0, NA_TK), cols]
        kt_ctx = ck_ref[cols, :].astype(BF16)
        vt_ctx = cv_ref[cols, :].astype(BF16)
        ps_loc, ps_ctx, vs_loc, vts_ctx, denoms = [], [], [], [], []
        for head in range(2):
            qh = _one_head(q_ref[:, cols], head)
            s_loc = _bdot_t(qh, k_loc) + bias_ref[kind, 2 * j + head]
            s_ctx = jnp.dot(qh, kt_ctx, preferred_element_type=F32)
            m = jnp.maximum(jnp.max(s_loc, axis=-1, keepdims=True), jnp.max(s_ctx, axis=-1, keepdims=True))
            p_loc = jnp.exp2(s_loc - m)
            p_ctx = jnp.exp2(s_ctx - m)
            denoms.append(jnp.sum(p_loc, axis=-1, keepdims=True) + jnp.sum(p_ctx, axis=-1, keepdims=True))
            ps_loc.append(p_loc.astype(BF16))
            ps_ctx.append(p_ctx.astype(BF16))
            vs_loc.append(jnp.where(_head_lanes(v_loc.shape, head), v_loc, jnp.zeros_like(v_loc)))
            feat = lax.broadcasted_iota(jnp.int32, vt_ctx.shape, 0)
            mine = (feat < NA_HEAD_DIM) if head == 0 else (feat >= NA_HEAD_DIM)
            vts_ctx.append(jnp.where(mine, vt_ctx, jnp.zeros_like(vt_ctx)))
        o = (jnp.dot(jnp.concatenate(ps_loc, axis=1), jnp.concatenate(vs_loc, axis=0), preferred_element_type=F32)
             + _bdot_t(jnp.concatenate(ps_ctx, axis=1), jnp.concatenate(vts_ctx, axis=1)))
        o_ref[:, cols] = (o / _join_heads(*[jnp.broadcast_to(d, o.shape) for d in denoms])).astype(BF16)


def latent_attention(q, kb, vb, cache_k, cache_v, rpb, layer):
    q_blk0 = N_CTX // NA_TQ
    s_blk0 = N_CTX // DEC_SEQ
    cache_spec = pl.BlockSpec((None, None, NA_WIDTH, PAST_LEN), lambda b, m: (b, layer, 0, 0))
    return pl.pallas_call(
        functools.partial(_lat_attn_kernel, layer),
        grid=(DEC_BATCH, NA_QBLOCKS),
        in_specs=[
            pl.BlockSpec(memory_space=pltpu.SMEM),
            pl.BlockSpec((NA_TQ, NA_WIDTH), lambda b, m: (q_blk0 + b * NA_QBLOCKS + m, 0)),
            pl.BlockSpec((DEC_SEQ, NA_WIDTH), lambda b, m: (s_blk0 + b, 0)),
            pl.BlockSpec((DEC_SEQ, NA_WIDTH), lambda b, m: (s_blk0 + b, 0)),
            cache_spec,
            cache_spec,
        ],
        out_specs=pl.BlockSpec((NA_TQ, NA_WIDTH), lambda b, m: (b * NA_QBLOCKS + m, 0)),
        out_shape=jax.ShapeDtypeStruct((N_LAT, NA_WIDTH), BF16),
        scratch_shapes=[pltpu.VMEM((NA_HEADS * N_DROW, GRID_W, GRID_W), F32),
                        pltpu.VMEM((len(NA_BLOCK_KINDS), NA_HEADS, NA_TQ, NA_TK), F32)],
        compiler_params=_params(2),
        name="latent_attention",
    )(rpb, q, kb, vb, cache_k, cache_v)


MERGE_TM = 512


MERGE_CTX_TILES = N_CTX // MERGE_TM
BRANCH_W = 512


def _merge_kernel(layer, n_x, *refs):
    x_refs, rest = refs[:n_x], refs[n_x:]
    (mod_ref, ya_ref, yb_ctx_ref, yb_lat_ref, yc_ctx_ref, yc_lat_ref, g_ref, wa_hbm, wb_hbm, wc_hbm, wo_hbm,
     ffn_gain_ref, wrt_ref, brt_ref, tri_ref,
     o_ref, hx_ref, bucket_ref, rank_ref, cnt_out_ref,
     wbr_ref, wo_ref, stage_ref, sem_ref, cnt_ref, cpad_ref) = rest
    i = pl.program_id(0)

    @pl.when(i == 0)
    def _():
        def store_branch(k):
            def st(v):
                wbr_ref[k] = v
            return st

        def store_out(k):
            def st(v):
                wo_ref[k * BRANCH_W:(k + 1) * BRANCH_W, :] = v
            return st

        chunks = [(w.at[layer], store_branch(k)) for k, w in enumerate((wa_hbm, wb_hbm, wc_hbm))]
        chunks += [(wo_hbm.at[layer, pl.ds(k * BRANCH_W, BRANCH_W), :], store_out(k))
                   for k in range(D_MODEL // BRANCH_W)]
        _load_cast(chunks, stage_ref, sem_ref)

    g = g_ref[...].astype(F32)
    yb = _token_tile(i, MERGE_CTX_TILES, (yb_ctx_ref, yb_lat_ref))
    yc = _token_tile(i, MERGE_CTX_TILES, (yc_ctx_ref, yc_lat_ref))
    merged = (g[:, 0:D_MODEL] * jnp.dot(ya_ref[...], wbr_ref[0], preferred_element_type=F32)
              + g[:, D_MODEL:2 * D_MODEL] * jnp.dot(yb, wbr_ref[1], preferred_element_type=F32)
              + g[:, 2 * D_MODEL:] * jnp.dot(yc, wbr_ref[2], preferred_element_type=F32))
    y = jnp.dot(merged.astype(BF16), wo_ref[...], preferred_element_type=F32)
    gate = mod_ref[:, 2 * D_MODEL:3 * D_MODEL]
    x = _token_tile(i, MERGE_CTX_TILES, x_refs) + gate * y
    o_ref[...] = x
    _route(x, mod_ref, ffn_gain_ref.at[pl.ds(layer, 1)], wrt_ref, brt_ref, tri_ref, hx_ref, bucket_ref, rank_ref,
           cnt_out_ref, cnt_ref, cpad_ref)


def merge_branches(xs, mod3, ya, yb_ctx, yb_lat, yc_ctx, yc_lat, gates, wa, wb, wc, wo, ffn_gain, w_rt, b_rt, layer):
    assert MERGE_TM == RT_TM
    row = lambda i: (i, 0)
    const = lambda shape: pl.BlockSpec(shape, lambda i: (0,) * len(shape))
    tri = jnp.asarray(np.triu(np.ones((RT_TM, RT_TM), np.float32)), BF16)
    hbm = pl.BlockSpec(memory_space=pl.ANY)
    x_specs = ([pl.BlockSpec((MERGE_TM, D_MODEL), row)] if len(xs) == 1
               else _split_specs((MERGE_TM, D_MODEL), MERGE_CTX_TILES))
    return pl.pallas_call(
        functools.partial(_merge_kernel, layer, len(xs)),
        grid=(N_TOK // MERGE_TM,),
        in_specs=x_specs + [
            _mod_spec(layer, MERGE_TM),
            pl.BlockSpec((MERGE_TM, SGU_WIDTH), row),
            *_split_specs((MERGE_TM, LRU_WIDTH), MERGE_CTX_TILES),
            *_split_specs((MERGE_TM, NA_WIDTH), MERGE_CTX_TILES),
            pl.BlockSpec((MERGE_TM, 3 * D_MODEL), row),
            hbm, hbm, hbm, hbm,
            _stacked_rows_spec(D_MODEL),
            _layer_spec((RT_ROWS, D_MODEL), layer),
            _layer_spec((RT_ROWS, 1), layer),
            const((RT_TM, RT_TM)),
        ],
        out_specs=[
            pl.BlockSpec((MERGE_TM, D_MODEL), row),
            pl.BlockSpec((RT_TM, HX_W), row),
            pl.BlockSpec((1, RT_TM), lambda i: (0, i)),
            pl.BlockSpec((1, RT_TM), lambda i: (0, i)),
            const((CNT_ROWS, COMB_W)),
        ],
        out_shape=[
            jax.ShapeDtypeStruct((N_TOK, D_MODEL), F32),
            jax.ShapeDtypeStruct((N_TOK, HX_W), F32),
            jax.ShapeDtypeStruct((1, N_TOK), jnp.int32),
            jax.ShapeDtypeStruct((1, N_TOK), jnp.int32),
            jax.ShapeDtypeStruct((CNT_ROWS, COMB_W), jnp.int32),
        ],
        scratch_shapes=[pltpu.VMEM((3, BRANCH_W, D_MODEL), BF16), pltpu.VMEM((D_MODEL, D_MODEL), BF16),
                        pltpu.VMEM((2, BRANCH_W, D_MODEL), F32), pltpu.SemaphoreType.DMA((2,)),
                        pltpu.VMEM((CNT_ROWS, COMB_W), F32), pltpu.VMEM((COMB_W, RT_TM), F32)],
        compiler_params=_params(1),
        name="merge_branches",
    )(*xs, mod3, ya, yb_ctx, yb_lat, yc_ctx, yc_lat, gates, wa, wb, wc, wo, ffn_gain, w_rt, b_rt, tri)


RT_TM = 512
RT_ROWS = 32
RT_EXPERT_ROW = 8
EXPERT_PAIRS = ((0, 1), (0, 2), (0, 3), (1, 2), (1, 3), (2, 3))
N_PAIRS = len(EXPERT_PAIRS)
N_BUCKETS = N_GROUPS * N_PAIRS
CNT_ROWS = 32
COMB_W = 128
HX_W = D_MODEL + COMB_W
EXP_TM = 256
EXP_TILES = N_TOK // EXP_TM + N_BUCKETS
HS_ROWS = EXP_TILES * EXP_TM


def _split_bf16(x):
    hi = x.astype(BF16)
    return hi, (x - hi.astype(F32)).astype(BF16)


def _route(x, mod_ref, gain_ref, w_ref, b_ref, tri_ref, hx_ref, bucket_ref, rank_ref, cnt_out_ref,
           cnt_ref, cpad_ref):
    @pl.when(pl.program_id(0) == 0)
    def _():
        cnt_ref[...] = jnp.zeros_like(cnt_ref)
        cpad_ref[...] = jnp.zeros_like(cpad_ref)

    m = mod_ref[...]
    shift, scale = m[:, 3 * D_MODEL:4 * D_MODEL], m[:, 4 * D_MODEL:5 * D_MODEL]
    h = _rms(x, gain_ref[...]) * (1.0 + scale) + shift
    hx_ref[:, 0:D_MODEL] = h
    h_hi, h_lo = _split_bf16(h)
    w_hi, w_lo = _split_bf16(w_ref[...])
    dims = (((1,), (1,)), ((), ()))
    logits = (lax.dot_general(w_hi, h_hi, dims, preferred_element_type=F32)
              + lax.dot_general(w_hi, h_lo, dims, preferred_element_type=F32)
              + lax.dot_general(w_lo, h_hi, dims, preferred_element_type=F32)) + b_ref[...]
    gl = [logits[g:g + 1, :] for g in range(N_GROUPS)]
    gmax = functools.reduce(jnp.maximum, gl)
    gid = jnp.full(gmax.shape, N_GROUPS - 1, jnp.int32)
    for g in reversed(range(N_GROUPS - 1)):
        gid = jnp.where(gl[g] == gmax, g, gid)
    p_grp = 1.0 / functools.reduce(jnp.add, [jnp.exp(v - gmax) for v in gl])
    el = []
    for e in range(EXPERTS_PER_GROUP):
        v = logits[RT_EXPERT_ROW + e:RT_EXPERT_ROW + e + 1, :]
        for g in range(1, N_GROUPS):
            row = RT_EXPERT_ROW + g * EXPERTS_PER_GROUP + e
            v = jnp.where(gid == g, logits[row:row + 1, :], v)
        el.append(v)
    top1 = functools.reduce(jnp.maximum, el)
    idx1 = jnp.full(top1.shape, EXPERTS_PER_GROUP - 1, jnp.int32)
    for e in reversed(range(EXPERTS_PER_GROUP - 1)):
        idx1 = jnp.where(el[e] == top1, e, idx1)
    rest = [jnp.where(idx1 == e, -jnp.inf, el[e]) for e in range(EXPERTS_PER_GROUP)]
    top2 = functools.reduce(jnp.maximum, rest)
    idx2 = jnp.full(top1.shape, EXPERTS_PER_GROUP - 1, jnp.int32)
    for e in reversed(range(EXPERTS_PER_GROUP - 1)):
        idx2 = jnp.where(rest[e] == top2, e, idx2)
    e2 = jnp.exp(top2 - top1)
    w1 = p_grp / (1.0 + e2)
    w2 = p_grp * e2 / (1.0 + e2)
    for e in range(EXPERTS_PER_GROUP):
        cpad_ref[e:e + 1, :] = jnp.where(idx1 == e, w1, 0.0) + jnp.where(idx2 == e, w2, 0.0)
    hx_ref[:, D_MODEL:] = cpad_ref[...].T
    lo = jnp.minimum(idx1, idx2)
    hi = jnp.maximum(idx1, idx2)
    pair = jnp.where(lo == 0, 0, jnp.where(lo == 1, 3, 5)) + (hi - lo - 1)
    bucket = gid * N_PAIRS + pair
    bucket_ref[...] = bucket
    sub = lax.broadcasted_iota(jnp.int32, (CNT_ROWS, RT_TM), 0)
    onehot = jnp.where(sub == bucket, 1.0, 0.0)
    seen = jnp.dot(onehot.astype(BF16), tri_ref[...], preferred_element_type=F32)
    cnt = cnt_ref[...]
    rank_ref[...] = jnp.sum(onehot * (seen - 1.0 + cnt[:, 0:1]), axis=0, keepdims=True).astype(jnp.int32)
    cnt = cnt + jnp.sum(onehot, axis=1, keepdims=True)
    cnt_ref[...] = cnt
    cnt_out_ref[...] = cnt.astype(jnp.int32)


def _router_weights(w_grp, b_grp, w_exp, b_exp):
    pad = lambda v, n: jnp.zeros(v.shape[:1] + (n,) + v.shape[2:], F32)
    gap, tail = RT_EXPERT_ROW - N_GROUPS, RT_ROWS - RT_EXPERT_ROW - N_EXPERTS
    w_grp, w_exp = jnp.swapaxes(w_grp, 1, 2), jnp.swapaxes(w_exp, 1, 2)
    w = jnp.concatenate([w_grp, pad(w_grp, gap), w_exp, pad(w_exp, tail)], axis=1)
    b_grp, b_exp = b_grp[:, :, None], b_exp[:, :, None]
    b = jnp.concatenate([b_grp, pad(b_grp, gap), b_exp, pad(b_exp, tail)], axis=1)
    return w, b


DISP_TM = 512


ROW_GROUP = 64


def _for_each_row(n_rows, fn):
    def group(k, carry):
        g0 = pl.multiple_of(k * ROW_GROUP, ROW_GROUP)
        for u in range(ROW_GROUP):
            fn(g0, u)
        return carry

    lax.fori_loop(0, n_rows // ROW_GROUP, group, 0)


def _dispatch_kernel(pos_ref, start_ref, cnt_ref, nt_ref, hx_hbm, hs_ref, buf_ref, zero_ref, blk_sem, row_sem, zsem):
    i = pl.program_id(0)
    last = pl.num_programs(0) - 1
    slot = i % 2
    base = i * DISP_TM

    def tile_in(tile, s):
        return pltpu.make_async_copy(hx_hbm.at[pl.ds(pl.multiple_of(tile * DISP_TM, DISP_TM), DISP_TM), :],
                                     buf_ref.at[s], blk_sem.at[s])

    def wait_rows(s):
        pltpu.make_async_copy(buf_ref.at[s], hs_ref.at[pl.ds(0, DISP_TM), :], row_sem.at[s]).wait()

    @pl.when(i == 0)
    def _():
        zero_ref[...] = jnp.zeros_like(zero_ref)

        def tile_copy(t):
            return pltpu.make_async_copy(zero_ref, hs_ref.at[pl.ds(pl.multiple_of(t * EXP_TM, EXP_TM), EXP_TM), :],
                                         zsem)

        def last_tile(g):
            return (start_ref[g] + cnt_ref[g, 0] - 1) // EXP_TM

        def each_tile(fn):
            for g in range(N_BUCKETS):
                pl.when(cnt_ref[g, 0] > 0)(functools.partial(fn, last_tile(g)))
            lax.fori_loop(nt_ref[0], EXP_TILES, lambda t, carry: (fn(t), carry)[1], 0)

        each_tile(lambda t: tile_copy(t).start())
        each_tile(lambda t: tile_copy(t).wait())
        tile_in(0, 0).start()

    @pl.when(i > 0)
    def _():
        wait_rows(1 - slot)

    @pl.when(i < last)
    def _():
        tile_in(i + 1, 1 - slot).start()

    tile_in(i, slot).wait()

    def issue(g0, u):
        src = buf_ref.at[slot, pl.ds(g0, ROW_GROUP), :]
        pltpu.make_async_copy(src.at[pl.ds(u, 1), :], hs_ref.at[pl.ds(pos_ref[base + g0 + u], 1), :],
                              row_sem.at[slot]).start(priority=u % 2)

    _for_each_row(DISP_TM, issue)

    @pl.when(i == last)
    def _():
        wait_rows(slot)


def dispatch(pos, starts, counts, n_tiles, hx):
    return pl.pallas_call(
        _dispatch_kernel,
        grid_spec=pltpu.PrefetchScalarGridSpec(
            num_scalar_prefetch=4,
            grid=(N_TOK // DISP_TM,),
            in_specs=[pl.BlockSpec(memory_space=pl.ANY)],
            out_specs=pl.BlockSpec(memory_space=pl.ANY),
            scratch_shapes=[pltpu.VMEM((2, DISP_TM, HX_W), F32), pltpu.VMEM((EXP_TM, HX_W), F32),
                            pltpu.SemaphoreType.DMA((2,)), pltpu.SemaphoreType.DMA((2,)),
                            pltpu.SemaphoreType.DMA(())],
        ),
        out_shape=jax.ShapeDtypeStruct((HS_ROWS, HX_W), F32),
        compiler_params=_params(1),
        name="dispatch",
    )(pos, starts, counts, n_tiles, hx)


(T_GROUP, T_LO, T_HI, T_SLOT, T_RUN_POS, T_NEXT_GROUP, T_FIRST, T_PREFETCHED,
 T_PENDING_GROUP, T_PENDING_EXPERT, T_PENDING_SLOT) = range(11)


def _experts_kernel(layer, tab_ref, nt_ref, hs_ref, w1_hbm, w3_hbm, w2_hbm, ys_ref,
                    w1_ref, w3_ref, w2_ref, st1_ref, st3_ref, st2_ref, sem_ref):
    t = pl.program_id(0)
    valid = t < nt_ref[0]
    group = tab_ref[T_GROUP, t]
    slot = tab_ref[T_SLOT, t]
    run_pos = tab_ref[T_RUN_POS, t]
    next_group = tab_ref[T_NEXT_GROUP, t]
    pending_expert = tab_ref[T_PENDING_EXPERT, t]

    def expert_copies(grp, e):
        idx = grp * EXPERTS_PER_GROUP + e
        return (pltpu.make_async_copy(w1_hbm.at[layer, idx], st1_ref, sem_ref.at[0]),
                pltpu.make_async_copy(w3_hbm.at[layer, idx], st3_ref, sem_ref.at[1]),
                pltpu.make_async_copy(w2_hbm.at[layer, idx], st2_ref, sem_ref.at[2]))

    def finish(copies, dst_slot, e):
        for cp in copies:
            cp.wait()
        w1_ref[dst_slot, e] = st1_ref[...].astype(BF16)
        w3_ref[dst_slot, e] = st3_ref[...].astype(BF16)
        w2_ref[dst_slot, e] = st2_ref[...].astype(BF16)

    @pl.when(jnp.logical_and(valid, pending_expert >= 0))
    def _():
        finish(expert_copies(tab_ref[T_PENDING_GROUP, t], pending_expert), tab_ref[T_PENDING_SLOT, t], pending_expert)

    @pl.when(jnp.logical_and(valid, tab_ref[T_FIRST, t] == 1))
    def _():
        def load(e, carry):
            copies = expert_copies(group, e)
            for cp in copies:
                cp.start()
            finish(copies, slot, e)
            return carry

        lax.fori_loop(tab_ref[T_PREFETCHED, t], EXPERTS_PER_GROUP, load, 0)

    prefetch = jnp.logical_and(valid, jnp.logical_and(next_group >= 0, run_pos < EXPERTS_PER_GROUP))

    @pl.when(prefetch)
    def _():
        for cp in expert_copies(next_group, run_pos):
            cp.start()

    @pl.when(valid)
    def _():
        h = hs_ref[:, 0:D_MODEL].astype(BF16)
        c = hs_ref[:, D_MODEL:]
        lane = lax.broadcasted_iota(jnp.int32, c.shape, 1)
        acc = None
        for e in (tab_ref[T_LO, t], tab_ref[T_HI, t]):
            ce = jnp.sum(jnp.where(lane == e, c, 0.0), axis=1, keepdims=True)
            a = jnp.dot(h, w1_ref[slot, e], preferred_element_type=F32)
            b = jnp.dot(h, w3_ref[slot, e], preferred_element_type=F32)
            hid = (a * _sigmoid(a)) * b * ce
            y = jnp.dot(hid.astype(BF16), w2_ref[slot, e], preferred_element_type=F32)
            acc = y if acc is None else acc + y
        ys_ref[...] = acc

    @pl.when(jnp.logical_not(valid))
    def _():
        ys_ref[...] = jnp.zeros_like(ys_ref)


def experts(tile_table, n_tiles, hs, w1, w3, w2, layer):
    hbm = pl.BlockSpec(memory_space=pl.ANY)
    return pl.pallas_call(
        functools.partial(_experts_kernel, layer),
        grid_spec=pltpu.PrefetchScalarGridSpec(
            num_scalar_prefetch=2,
            grid=(EXP_TILES,),
            in_specs=[pl.BlockSpec((EXP_TM, HX_W), lambda t, *_: (t, 0)), hbm, hbm, hbm],
            out_specs=pl.BlockSpec((EXP_TM, D_MODEL), lambda t, *_: (t, 0)),
            scratch_shapes=[
                pltpu.VMEM((2, EXPERTS_PER_GROUP, D_MODEL, D_EXPERT), BF16),
                pltpu.VMEM((2, EXPERTS_PER_GROUP, D_MODEL, D_EXPERT), BF16),
                pltpu.VMEM((2, EXPERTS_PER_GROUP, D_EXPERT, D_MODEL), BF16),
                pltpu.VMEM((D_MODEL, D_EXPERT), F32),
                pltpu.VMEM((D_MODEL, D_EXPERT), F32),
                pltpu.VMEM((D_EXPERT, D_MODEL), F32),
                pltpu.SemaphoreType.DMA((3,)),
            ],
        ),
        out_shape=jax.ShapeDtypeStruct((HS_ROWS, D_MODEL), F32),
        compiler_params=_params(1),
        name="experts",
    )(tile_table, n_tiles, hs, w1, w3, w2)


N_TABLE_ROWS = T_PENDING_SLOT + 1
EXP_TM_LOG2 = EXP_TM.bit_length() - 1


def _schedule_kernel(cnt_ref, starts_ref, tab_ref, nt_ref):
    i32 = jnp.int32
    t = i32(0)
    for b in range(N_BUCKETS):
        n = lax.shift_right_logical(cnt_ref[b, 0] + (EXP_TM - 1), EXP_TM_LOG2)
        starts_ref[b] = t * EXP_TM

        def fill(j, carry, t=t, group=b // N_PAIRS, pair=EXPERT_PAIRS[b % N_PAIRS]):
            tab_ref[T_GROUP, t + j] = group
            tab_ref[T_LO, t + j] = pair[0]
            tab_ref[T_HI, t + j] = pair[1]
            return carry

        lax.fori_loop(0, n, fill, 0)
        t = t + n
    n_tiles = t
    nt_ref[0] = n_tiles

    def unused(k, carry):
        for row, value in ((T_GROUP, N_GROUPS - 1), (T_LO, 0), (T_HI, 1), (T_SLOT, 0), (T_RUN_POS, 0),
                           (T_NEXT_GROUP, -1), (T_FIRST, 0), (T_PREFETCHED, 0), (T_PENDING_GROUP, 0),
                           (T_PENDING_EXPERT, -1), (T_PENDING_SLOT, 0)):
            tab_ref[row, k] = value
        return carry

    lax.fori_loop(n_tiles, EXP_TILES, unused, 0)

    def back(k, carry):
        next_group, group_after = carry
        tile = n_tiles - 1 - k
        group = tab_ref[T_GROUP, tile]
        next_group = jnp.where(k == 0, -1, jnp.where(group != group_after, group_after, next_group))
        tab_ref[T_NEXT_GROUP, tile] = next_group
        return next_group, group

    lax.fori_loop(0, n_tiles, back, (i32(-1), i32(-1)))

    def forward(tile, carry):
        prev_group, run, run_start, prev_len, prev_fetch, prev_pos, prev_next, prev_slot = carry
        group = tab_ref[T_GROUP, tile]
        first = jnp.logical_or(tile == 0, group != prev_group)
        prev_len = jnp.where(first, tile - run_start, prev_len)
        run = jnp.where(first, run + 1, run)
        run_start = jnp.where(first, tile, run_start)
        slot = lax.rem(run, 2)
        run_pos = tile - run_start
        next_group = tab_ref[T_NEXT_GROUP, tile]
        tab_ref[T_FIRST, tile] = first.astype(i32)
        tab_ref[T_SLOT, tile] = slot
        tab_ref[T_RUN_POS, tile] = run_pos
        tab_ref[T_PREFETCHED, tile] = jnp.minimum(prev_len, EXPERTS_PER_GROUP)
        tab_ref[T_PENDING_EXPERT, tile] = jnp.where(prev_fetch == 1, prev_pos, -1)
        tab_ref[T_PENDING_GROUP, tile] = prev_next
        tab_ref[T_PENDING_SLOT, tile] = 1 - prev_slot
        fetch = jnp.logical_and(next_group >= 0, run_pos < EXPERTS_PER_GROUP).astype(i32)
        return group, run, run_start, prev_len, fetch, run_pos, next_group, slot

    lax.fori_loop(0, n_tiles, forward, (i32(-1), i32(-1), i32(0), i32(0), i32(0), i32(0), i32(0), i32(0)))


def expert_schedule(counts):
    smem = pl.BlockSpec(memory_space=pltpu.SMEM)
    return pl.pallas_call(
        _schedule_kernel,
        in_specs=[smem],
        out_specs=[smem, smem, smem],
        out_shape=[jax.ShapeDtypeStruct((N_BUCKETS,), jnp.int32),
                   jax.ShapeDtypeStruct((N_TABLE_ROWS, EXP_TILES), jnp.int32),
                   jax.ShapeDtypeStruct((1,), jnp.int32)],
        name="expert_schedule",
    )(counts)


RES_TM = 512
RES_CTX_TILES = N_CTX // RES_TM


def _moe_residual_kernel(final, pos_ref, x_ref, mod_ref, gain_ref, ys_ref, *refs):
    out_refs, (ybuf_ref, sem_ref) = refs[:-2], refs[-2:]
    i = pl.program_id(0)
    slot = i % 2

    def gather_tile(tile, tile_slot):
        def issue(g0, u):
            dst = ybuf_ref.at[tile_slot, pl.ds(g0, ROW_GROUP), :]
            pltpu.make_async_copy(ys_ref.at[pl.ds(pos_ref[tile * RES_TM + g0 + u], 1), :], dst.at[pl.ds(u, 1), :],
                                  sem_ref.at[tile_slot]).start(priority=u % 2)

        _for_each_row(RES_TM, issue)

    @pl.when(i == 0)
    def _():
        gather_tile(0, 0)

    @pl.when(i + 1 < pl.num_programs(0))
    def _():
        gather_tile(i + 1, 1 - slot)

    pltpu.make_async_copy(ys_ref.at[pl.ds(0, RES_TM), :], ybuf_ref.at[slot], sem_ref.at[slot]).wait()
    x = x_ref[...] + mod_ref[:, 5 * D_MODEL:6 * D_MODEL] * ybuf_ref[slot]
    if not final:
        out_refs[0][...] = x
        return
    y = _rms(x, gain_ref[...])

    @pl.when(i < RES_CTX_TILES)
    def _():
        out_refs[0][...] = y

    @pl.when(i >= RES_CTX_TILES)
    def _():
        out_refs[1][...] = y


def moe_residual(pos, x, mod4, gain, ys, layer, final):
    if final:
        out_specs = _split_specs((RES_TM, D_MODEL), RES_CTX_TILES)
        out_shape = [jax.ShapeDtypeStruct((N_CTX, D_MODEL), F32), jax.ShapeDtypeStruct((N_LAT, D_MODEL), F32)]
    else:
        out_specs = pl.BlockSpec((RES_TM, D_MODEL), lambda i, *_: (i, 0))
        out_shape = jax.ShapeDtypeStruct((N_TOK, D_MODEL), F32)
    return pl.pallas_call(
        functools.partial(_moe_residual_kernel, final),
        grid_spec=pltpu.PrefetchScalarGridSpec(
            num_scalar_prefetch=1,
            grid=(N_TOK // RES_TM,),
            in_specs=[
                pl.BlockSpec((RES_TM, D_MODEL), lambda i, *_: (i, 0)),
                _mod_spec(layer, RES_TM),
                pl.BlockSpec((1, D_MODEL), lambda i, *_: (0, 0)),
                pl.BlockSpec(memory_space=pl.ANY),
            ],
            out_specs=out_specs,
            scratch_shapes=[pltpu.VMEM((2, RES_TM, D_MODEL), F32), pltpu.SemaphoreType.DMA((2,))],
        ),
        out_shape=out_shape,
        compiler_params=_params(1),
        name="moe_residual_final" if final else "moe_residual",
    )(pos, x, mod4, gain, ys)


def kernel(x_prompt, x_sample, cache_k, cache_v, state_lru, c, c_ctx, w_mod, b_mod, norm_mix, norm_ffn, w_in, sgu_norm, sgu_w, sgu_b, lru_conv_w, lru_conv_b, lru_w_r, lru_b_r, lru_w_i, lru_b_i, lru_lambda, na_rpb, w_branch_sgu, w_branch_lru, w_branch_na, w_out, moe_w_group, moe_b_group, moe_w_expert, moe_b_expert, moe_w1, moe_w3, moe_w2, final_norm_gain):
    xs = (x_prompt.reshape(N_CTX, D_MODEL), x_sample.reshape(N_LAT, D_MODEL))
    cond = jnp.zeros((N_COND, D_MODEL), F32).at[0].set(c_ctx).at[1:1 + DEC_BATCH].set(c)
    mod = modulation(cond, w_mod, b_mod)
    zero_state = jnp.zeros((BATCH, 2, LRU_WIDTH), F32)
    feature_major = lambda t: jnp.transpose(t, (0, 1, 3, 4, 2)).reshape(DEC_BATCH, DEPTH, NA_WIDTH, PAST_LEN)
    cache_k, cache_v = feature_major(cache_k), feature_major(cache_v)
    final_gain = final_norm_gain.reshape(1, D_MODEL)
    sgu_b_t = jnp.swapaxes(sgu_b, 1, 2)
    w_gate = _block_diag_gate_weights(lru_w_r, lru_w_i)
    rpb = na_rpb.reshape(DEPTH, NA_HEADS * N_DROW * N_DCOL)
    w_rt, b_rt = _router_weights(moe_w_group, moe_b_group, moe_w_expert, moe_b_expert)
    caches, ss = [], []
    for l in range(DEPTH):
        ya, za, q, kb, vb, k_ctx, v_ctx, gates = in_projection(
            xs, mod, norm_mix, w_in, sgu_norm, sgu_w, sgu_b_t, l, prev_caches=caches if l == DEPTH - 1 else ())
        caches.append((k_ctx, v_ctx))
        lru_args = (lru_conv_w, lru_conv_b, w_gate, lru_b_r, lru_b_i, lru_lambda)
        yb_ctx, st_ctx = rglru(za, *lru_args, zero_state, pl.BlockSpec((None, 2, LRU_WIDTH), lambda i: (i, 0, 0)),
                               SEQ, BATCH, 0, l)
        yb_lat, _ = rglru(za, *lru_args, state_lru,
                          pl.BlockSpec((None, None, 2, LRU_WIDTH), lambda i, l=l: (i, l, 0, 0)),
                          DEC_SEQ, DEC_BATCH, N_CTX, l)
        yc_ctx = context_attention(q, kb, vb)
        yc_lat = latent_attention(q, kb, vb, cache_k, cache_v, rpb, l)
        x, hx, bucket, rank, counts = merge_branches(
            xs, mod, ya, yb_ctx, yb_lat, yc_ctx, yc_lat, gates, w_branch_sgu, w_branch_lru, w_branch_na, w_out,
            norm_ffn, w_rt, b_rt, l)
        bucket, rank = bucket.reshape(N_TOK), rank.reshape(N_TOK)
        starts, tile_table, n_tiles = expert_schedule(counts)
        pos = jnp.sum(jnp.where(bucket[:, None] == jnp.arange(N_BUCKETS), starts[None, :] + rank[:, None], 0), axis=1)
        hs = dispatch(pos, starts, counts, n_tiles, hx)
        ys = experts(tile_table, n_tiles, hs, moe_w1, moe_w3, moe_w2, l)
        if l < DEPTH - 1:
            xs = (moe_residual(pos, x, mod, final_gain, ys, l, False),)
        else:
            y_ctx, y_lat = moe_residual(pos, x, mod, final_gain, ys, l, True)
        ss.append(st_ctx)
    token_major = lambda t: jnp.transpose(t.reshape(BATCH, DEPTH, NA_HEADS, NA_HEAD_DIM, SEQ), (0, 1, 4, 2, 3))
    return (y_ctx.reshape(BATCH, SEQ, D_MODEL), y_lat.reshape(DEC_BATCH, DEC_SEQ, D_MODEL),
            token_major(k_ctx), token_major(v_ctx), jnp.stack(ss, axis=1))
```

```python
import functools

import jax
import jax.numpy as jnp
import numpy as np
from jax import lax
from jax.experimental import pallas as pl
from jax.experimental.pallas import tpu as pltpu

F32 = jnp.float32
BF16 = jnp.bfloat16

D_MODEL = 1024
BATCH = 16
SEQ = 256
DEPTH = 2
DEC_BATCH = 4
DEC_SEQ = 2048
PAST_LEN = 512
GRID_W = 64
CHUNK = 128
SGU_WIDTH = 512
SGU_GROUPS = 4
LRU_WIDTH = 512
LRU_BLOCKS = 8
CONV_WIDTH = 4
LRU_C = 8.0
NA_HEADS = 8
NA_HEAD_DIM = 64
NA_WIDTH = 512
WIN_ROWS = 8
WIN_COLS = 16
N_GROUPS = 4
EXPERTS_PER_GROUP = 4
N_EXPERTS = 16
D_EXPERT = 512
IN_WIDTH = 6656
EPS = 1e-6
NEG_INF = -1e30

N_CTX = BATCH * SEQ
N_LAT = DEC_BATCH * DEC_SEQ
N_TOK = N_CTX + N_LAT
N_COND = 8
MOD_WIDTH = 6 * D_MODEL
GRID_ROWS = DEC_SEQ // GRID_W

VMEM_LIMIT_BYTES = 56 * 1024 * 1024


def _params(n_axes):
    return pltpu.CompilerParams(dimension_semantics=("arbitrary",) * n_axes,
                                vmem_limit_bytes=VMEM_LIMIT_BYTES)


def _cond_row(tile, tile_rows):
    tok = tile * tile_rows
    return jnp.where(tok < N_CTX, 0, 1 + (tok - N_CTX) // DEC_SEQ)


def _layer_spec(shape, layer):
    return pl.BlockSpec((None,) + tuple(shape), lambda i, *_: (layer,) + (0,) * len(shape))


def _stacked_rows_spec(width):
    return pl.BlockSpec((DEPTH, width), lambda i, *_: (0, 0))


def _mod_spec(layer, tile_rows):
    return pl.BlockSpec((None, None, 1, MOD_WIDTH), lambda i, *_: (layer, _cond_row(i, tile_rows), 0, 0))


def _rms(x, gain):
    return x * lax.rsqrt(jnp.mean(x * x, axis=-1, keepdims=True) + EPS) * gain


def _bdot(a, b):
    return jnp.dot(a.astype(BF16), b.astype(BF16), preferred_element_type=F32)


def _bdot_t(a, b):
    return lax.dot_general(a.astype(BF16), b.astype(BF16), (((1,), (1,)), ((), ())),
                           preferred_element_type=F32)


MOD_TN = 1536


def _mod_kernel(cond_ref, w_ref, b_ref, o_ref):
    c = cond_ref[...]
    s = c * jax.nn.sigmoid(c)
    mod = _bdot(s, w_ref[...]) + b_ref[pl.ds(pl.program_id(0), 1), :]
    for r in range(N_COND):
        o_ref[r] = mod[r:r + 1, :]


def modulation(cond, w_mod, b_mod):
    return pl.pallas_call(
        _mod_kernel,
        grid=(DEPTH, MOD_WIDTH // MOD_TN),
        in_specs=[
            pl.BlockSpec((N_COND, D_MODEL), lambda l, j: (0, 0)),
            pl.BlockSpec((None, D_MODEL, MOD_TN), lambda l, j: (l, 0, j)),
            pl.BlockSpec((DEPTH, MOD_TN), lambda l, j: (0, j)),
        ],
        out_specs=pl.BlockSpec((None, N_COND, 1, MOD_TN), lambda l, j: (l, 0, 0, j)),
        out_shape=jax.ShapeDtypeStruct((DEPTH, N_COND, 1, MOD_WIDTH), F32),
        compiler_params=_params(2),
        name="modulation",
    )(cond, w_mod, b_mod)


IN_TM = 256
IN_CTX_TILES = N_CTX // IN_TM
ZA_WIDTH = 4 * 512
KV_OFF = ZA_WIDTH + NA_WIDTH
GATE_OFF = KV_OFF + 2 * NA_WIDTH
W_CHUNK = 512
Q_SCALE = NA_HEAD_DIM ** -0.5 * float(np.log2(np.e))


def _load_cast(chunks, stage_ref, sem_ref):
    def copy(j):
        return pltpu.make_async_copy(chunks[j][0], stage_ref.at[j % 2], sem_ref.at[j % 2])

    copy(0).start()
    for j in range(len(chunks)):
        if j + 1 < len(chunks):
            copy(j + 1).start()
        copy(j).wait()
        chunks[j][1](stage_ref[j % 2].astype(BF16))


def _token_tile(i, n_ctx_tiles, refs):
    if len(refs) == 1:
        return refs[0][...]
    return jnp.where(i < n_ctx_tiles, refs[0][...], refs[1][...])


def _split_specs(block, n_ctx_tiles):
    return [pl.BlockSpec(block, lambda i, *_: (jnp.minimum(i, n_ctx_tiles - 1), 0)),
            pl.BlockSpec(block, lambda i, *_: (jnp.maximum(i - n_ctx_tiles, 0), 0))]


SGU_GD = SGU_WIDTH // SGU_GROUPS


def _spatial_gating(u, v, gain, ws_ref, bs_ref):
    u = jax.nn.gelu(u)
    v = _rms(jax.nn.gelu(v), gain).astype(BF16)
    out = []
    for g in range(SGU_GROUPS):
        cols = slice(g * SGU_GD, (g + 1) * SGU_GD)
        mixed = jnp.dot(ws_ref[g].astype(BF16), v[:, cols], preferred_element_type=F32) + bs_ref[:, g:g + 1]
        out.append(u[:, cols] * mixed)
    return jnp.concatenate(out, axis=1)


def _inproj_kernel(layer, n_x, n_prev, *refs):
    x_refs, (mod_ref, gain_ref, w_hbm, sgu_gain_ref, ws_ref, bs_ref) = refs[:n_x], refs[n_x:n_x + 6]
    prev_refs, rest = refs[n_x + 6:n_x + 6 + 2 * n_prev], refs[n_x + 6 + 2 * n_prev:]
    ya_ref, zb_ref, q_ref, kb_ref, vb_ref, kc_ref, vc_ref, g_ref, w_ref, stage_ref, sem_ref = rest
    i = pl.program_id(0)

    @pl.when(i == 0)
    def _():
        def store(c):
            def st(v):
                w_ref[:, c * W_CHUNK:(c + 1) * W_CHUNK] = v
            return st

        _load_cast([(w_hbm.at[layer, :, pl.ds(c * W_CHUNK, W_CHUNK)], store(c)) for c in range(IN_WIDTH // W_CHUNK)],
                   stage_ref, sem_ref)

    m = mod_ref[...]
    shift, scale = m[:, 0:D_MODEL], m[:, D_MODEL:2 * D_MODEL]
    gain = gain_ref[layer:layer + 1, :]
    h = (_rms(_token_tile(i, IN_CTX_TILES, x_refs), gain) * (1.0 + scale) + shift).astype(BF16)
    za = jnp.dot(h, w_ref[:, 0:ZA_WIDTH], preferred_element_type=F32)
    for c in range(IN_TM // CHUNK):
        rows = slice(c * CHUNK, (c + 1) * CHUNK)
        ya_ref[rows, :] = _spatial_gating(za[rows, 0:SGU_WIDTH], za[rows, SGU_WIDTH:2 * SGU_WIDTH],
                                          sgu_gain_ref[layer:layer + 1, :], ws_ref, bs_ref).astype(BF16)
    zb_ref[:, 0:LRU_WIDTH] = za[:, 2 * SGU_WIDTH:2 * SGU_WIDTH + LRU_WIDTH].astype(BF16)
    zb_ref[:, LRU_WIDTH:] = jax.nn.gelu(za[:, 2 * SGU_WIDTH + LRU_WIDTH:]).astype(BF16)
    q_ref[...] = (jnp.dot(h, w_ref[:, ZA_WIDTH:KV_OFF], preferred_element_type=F32) * Q_SCALE).astype(BF16)
    kv = jnp.dot(h, w_ref[:, KV_OFF:GATE_OFF], preferred_element_type=F32)
    kb_ref[...] = kv[:, 0:NA_WIDTH].astype(BF16)
    vb_ref[...] = kv[:, NA_WIDTH:].astype(BF16)
    g_ref[...] = _sigmoid(jnp.dot(h, w_ref[:, GATE_OFF:], preferred_element_type=F32)).astype(BF16)

    @pl.when(i < IN_CTX_TILES)
    def _():
        if n_prev == 0:
            kc_ref[...] = kv[:, 0:NA_WIDTH].T
            vc_ref[...] = kv[:, NA_WIDTH:].T
        else:
            for p in range(n_prev):
                kc_ref[p] = prev_refs[2 * p][...]
                vc_ref[p] = prev_refs[2 * p + 1][...]
            kc_ref[n_prev] = kv[:, 0:NA_WIDTH].T
            vc_ref[n_prev] = kv[:, NA_WIDTH:].T


def in_projection(xs, mod3, gain, w_in, sgu_gain, sgu_w, sgu_b_t, layer, prev_caches=()):
    assert IN_TM == SEQ
    row = lambda i: (i, 0)
    x_specs = ([pl.BlockSpec((IN_TM, D_MODEL), row)] if len(xs) == 1
               else _split_specs((IN_TM, D_MODEL), IN_CTX_TILES))
    n_prev = len(prev_caches)
    layer_cache_spec = pl.BlockSpec((None, NA_WIDTH, SEQ), lambda i: (jnp.minimum(i, IN_CTX_TILES - 1), 0, 0))
    if n_prev:
        cache_spec = pl.BlockSpec((None, n_prev + 1, NA_WIDTH, SEQ),
                                  lambda i: (jnp.minimum(i, IN_CTX_TILES - 1), 0, 0, 0))
        cache_shape = jax.ShapeDtypeStruct((BATCH, n_prev + 1, NA_WIDTH, SEQ), F32)
    else:
        cache_spec = layer_cache_spec
        cache_shape = jax.ShapeDtypeStruct((BATCH, NA_WIDTH, SEQ), F32)
    return pl.pallas_call(
        functools.partial(_inproj_kernel, layer, len(xs), n_prev),
        grid=(N_TOK // IN_TM,),
        in_specs=x_specs + [
            _mod_spec(layer, IN_TM),
            _stacked_rows_spec(D_MODEL),
            pl.BlockSpec(memory_space=pl.ANY),
            _stacked_rows_spec(SGU_WIDTH),
            _layer_spec((SGU_GROUPS, CHUNK, CHUNK), layer),
            _layer_spec((CHUNK, SGU_GROUPS), layer),
        ] + [layer_cache_spec] * (2 * n_prev),
        out_specs=[
            pl.BlockSpec((IN_TM, SGU_WIDTH), row),
            pl.BlockSpec((IN_TM, 2 * LRU_WIDTH), row),
            pl.BlockSpec((IN_TM, NA_WIDTH), row),
            pl.BlockSpec((IN_TM, NA_WIDTH), row),
            pl.BlockSpec((IN_TM, NA_WIDTH), row),
            cache_spec,
            cache_spec,
            pl.BlockSpec((IN_TM, 3 * D_MODEL), row),
        ],
        out_shape=[
            jax.ShapeDtypeStruct((N_TOK, SGU_WIDTH), BF16),
            jax.ShapeDtypeStruct((N_TOK, 2 * LRU_WIDTH), BF16),
            jax.ShapeDtypeStruct((N_TOK, NA_WIDTH), BF16),
            jax.ShapeDtypeStruct((N_TOK, NA_WIDTH), BF16),
            jax.ShapeDtypeStruct((N_TOK, NA_WIDTH), BF16),
            cache_shape,
            cache_shape,
            jax.ShapeDtypeStruct((N_TOK, 3 * D_MODEL), BF16),
        ],
        scratch_shapes=[pltpu.VMEM((D_MODEL, IN_WIDTH), BF16), pltpu.VMEM((2, D_MODEL, W_CHUNK), F32),
                        pltpu.SemaphoreType.DMA((2,))],
        compiler_params=_params(1),
        name="in_projection",
    )(*xs, mod3, gain, w_in, sgu_gain, sgu_w, sgu_b_t, *[c for kv in prev_caches for c in kv])


LRU_TC = 256
LRU_HALF = 256
SUB = 8
HALO = 8
LANES = 128
LRU_SLABS = LRU_WIDTH // LANES


def _sigmoid(x):
    return 0.5 * jnp.tanh(0.5 * x) + 0.5


def _lru_kernel(seq_len, layer, xr_ref, gr_ref, cw_ref, cb_ref, wlo_ref, whi_ref, br_ref, bi_ref,
                lam_ref, h0_ref, y_ref, st_ref, xp_ref, a_ref, b_ref, h_ref):
    seg_len = seq_len // SUB
    pitch = seg_len + SUB
    segs_per_chunk = max(LRU_TC // seg_len, 1)
    n_chunks = seq_len // LRU_TC
    zeros = jnp.zeros((HALO, LRU_WIDTH), F32)
    xp_ref[0:HALO, :] = zeros
    xp_ref[seq_len + HALO:seq_len + 2 * HALO, :] = zeros

    def copy_in(c, carry):
        r0 = pl.multiple_of(c * LRU_TC, LRU_TC)
        xp_ref[pl.ds(r0 + HALO, LRU_TC), :] = xr_ref[pl.ds(r0, LRU_TC), :].astype(F32)
        return carry

    lax.fori_loop(0, n_chunks, copy_in, 0)

    def chunk_rows(c, seg):
        n = min(seg_len, LRU_TC)
        start = pl.multiple_of((c * segs_per_chunk + seg) * pitch, SUB)
        return pl.ds(start, n), slice(seg * n, (seg + 1) * n)

    cw = cw_ref[...]
    cb = cb_ref[layer:layer + 1, :]
    win = LRU_TC + 2 * HALO
    neg_lam = -lam_ref[...]
    softplus = jnp.maximum(neg_lam, 0.0) + jnp.log1p(jnp.exp(-jnp.abs(neg_lam)))
    decay = (0.5 * LRU_C) * softplus
    half_br = 0.5 * br_ref[...]
    half_bi = 0.5 * bi_ref[...]

    def gates(c, carry):
        r0 = pl.multiple_of(c * LRU_TC, LRU_TC)
        w = xp_ref[pl.ds(r0, win), :]
        xc = (cw[0:1, :] * pltpu.roll(w, 1, 0)[HALO:HALO + LRU_TC]
              + cw[1:2, :] * w[HALO:HALO + LRU_TC]
              + cw[2:3, :] * pltpu.roll(w, win - 1, 0)[HALO:HALO + LRU_TC]
              + cw[3:4, :] * pltpu.roll(w, win - 2, 0)[HALO:HALO + LRU_TC]) + cb
        xb = xc.astype(BF16)
        for half, w_ref in enumerate((wlo_ref, whi_ref)):
            cols = slice(half * LRU_HALF, (half + 1) * LRU_HALF)
            pre = jnp.dot(xb[:, cols], w_ref[...], preferred_element_type=F32)
            half_x = 0.5 * xc[:, cols]
            for d in range(2):
                tr = jnp.tanh(pre[:, (2 * d) * LRU_HALF:(2 * d + 1) * LRU_HALF] + half_br[d:d + 1, cols])
                ti = jnp.tanh(pre[:, (2 * d + 1) * LRU_HALF:(2 * d + 2) * LRU_HALF] + half_bi[d:d + 1, cols])
                neg_log_a = decay[d:d + 1, cols] * tr + decay[d:d + 1, cols]
                a = jnp.exp2(neg_log_a * -LOG2E)
                z = jnp.tanh(neg_log_a) * (a * a + 1.0)
                b = jnp.where(z > 0.0, z * lax.rsqrt(z), 0.0) * (half_x * ti + half_x)
                for k in range(LRU_HALF // LANES):
                    slab = half * (LRU_HALF // LANES) + k
                    lanes = slice(k * LANES, (k + 1) * LANES)
                    for seg in range(segs_per_chunk):
                        dst, src = chunk_rows(c, seg)
                        a_ref[d, slab, dst, :] = a[src, lanes]
                        b_ref[d, slab, dst, :] = b[src, lanes]
        return carry

    lax.fori_loop(0, n_chunks, gates, 0)

    def step_rows(d, j):
        return pl.ds(j if d == 0 else seg_len - 1 - j, SUB, stride=pitch)

    def reduce_step(j, carry):
        out = []
        for d in range(2):
            for slab in range(LRU_SLABS):
                big_a, big_b = carry[d * LRU_SLABS + slab]
                a = a_ref[d, slab, step_rows(d, j), :]
                b = b_ref[d, slab, step_rows(d, j), :]
                out.append((a * big_a, a * big_b + b))
        return tuple(out)

    ident = (jnp.ones((SUB, LANES), F32), jnp.zeros((SUB, LANES), F32))
    totals = lax.fori_loop(0, seg_len, reduce_step, (ident,) * (2 * LRU_SLABS), unroll=8)

    sub = lax.broadcasted_iota(jnp.int32, (SUB, LANES), 0)
    entering = []
    for d in range(2):
        for slab in range(LRU_SLABS):
            big_a, big_b = totals[d * LRU_SLABS + slab]
            h = h0_ref[d:d + 1, slab * LANES:(slab + 1) * LANES]
            rows = jnp.zeros((SUB, LANES), F32)
            for s in (range(SUB) if d == 0 else reversed(range(SUB))):
                rows = jnp.where(sub == s, h, rows)
                h = big_a[s:s + 1, :] * h + big_b[s:s + 1, :]
            entering.append(rows)
            st_ref[d:d + 1, slab * LANES:(slab + 1) * LANES] = h

    def apply_step(j, carry):
        out = []
        for d in range(2):
            for slab in range(LRU_SLABS):
                h = (a_ref[d, slab, step_rows(d, j), :] * carry[d * LRU_SLABS + slab]
                     + b_ref[d, slab, step_rows(d, j), :])
                h_ref[d, slab, step_rows(d, j), :] = h
                out.append(h)
        return tuple(out)

    lax.fori_loop(0, seg_len, apply_step, tuple(entering), unroll=8)

    def merge(c, carry):
        r0 = pl.multiple_of(c * LRU_TC, LRU_TC)
        gate = gr_ref[pl.ds(r0, LRU_TC), :].astype(F32)
        for slab in range(LRU_SLABS):
            lanes = slice(slab * LANES, (slab + 1) * LANES)
            for seg in range(segs_per_chunk):
                src, dst = chunk_rows(c, seg)
                h = h_ref[0, slab, src, :] + h_ref[1, slab, src, :]
                y_ref[pl.ds(r0 + dst.start, dst.stop - dst.start), lanes] = (h * gate[dst, lanes]).astype(BF16)
        return carry

    lax.fori_loop(0, n_chunks, merge, 0)


def rglru(za, conv_w, conv_b, w_gate, b_r, b_i, lam, h0, h0_spec, seq_len, n_seq, tok_off, layer):
    blk0 = tok_off // seq_len
    half_spec = lambda half: pl.BlockSpec((None, None, LRU_HALF, 4 * LRU_HALF), lambda i: (layer, half, 0, 0))
    return pl.pallas_call(
        functools.partial(_lru_kernel, seq_len, layer),
        grid=(n_seq,),
        in_specs=[
            pl.BlockSpec((seq_len, LRU_WIDTH), lambda i: (i + blk0, 0)),
            pl.BlockSpec((seq_len, LRU_WIDTH), lambda i: (i + blk0, 1)),
            _layer_spec((CONV_WIDTH, LRU_WIDTH), layer),
            _stacked_rows_spec(LRU_WIDTH),
            half_spec(0),
            half_spec(1),
            _layer_spec((2, LRU_WIDTH), layer),
            _layer_spec((2, LRU_WIDTH), layer),
            _layer_spec((2, LRU_WIDTH), layer),
            h0_spec,
        ],
        out_specs=[
            pl.BlockSpec((seq_len, LRU_WIDTH), lambda i: (i, 0)),
            pl.BlockSpec((None, 2, LRU_WIDTH), lambda i: (i, 0, 0)),
        ],
        out_shape=[
            jax.ShapeDtypeStruct((n_seq * seq_len, LRU_WIDTH), BF16),
            jax.ShapeDtypeStruct((n_seq, 2, LRU_WIDTH), F32),
        ],
        scratch_shapes=[
            pltpu.VMEM((seq_len + 2 * HALO, LRU_WIDTH), F32),
            pltpu.VMEM((2, LRU_SLABS, seq_len + SUB * SUB, LANES), F32),
            pltpu.VMEM((2, LRU_SLABS, seq_len + SUB * SUB, LANES), F32),
            pltpu.VMEM((2, LRU_SLABS, seq_len + SUB * SUB, LANES), F32),
        ],
        compiler_params=_params(1),
        name=f"rglru_{seq_len}",
    )(za, za, conv_w, conv_b, w_gate, w_gate, b_r, b_i, lam, h0)


LRU_BLOCK_W = LRU_WIDTH // LRU_BLOCKS
LRU_HALF_BLOCKS = LRU_BLOCKS // 2


def _gate_weights_kernel(wr_ref, wi_ref, o_ref):
    row = lax.broadcasted_iota(jnp.int32, (LRU_BLOCK_W, LRU_HALF), 0)
    lane = lax.broadcasted_iota(jnp.int32, (LRU_BLOCK_W, LRU_HALF), 1)
    groups = [(half, d, g) for half in range(2) for d in range(2) for g in range(2)]
    for a in range(LRU_HALF_BLOCKS):
        rows = slice(a * LRU_BLOCK_W, (a + 1) * LRU_BLOCK_W)
        place = (lane == row + a * LRU_BLOCK_W).astype(BF16)
        blocks = [(0.5 * (wr_ref, wi_ref)[g][d, half * LRU_HALF_BLOCKS + a]).astype(BF16) for half, d, g in groups]
        strips = jnp.dot(jnp.concatenate(blocks, axis=0), place, preferred_element_type=F32).astype(BF16)
        for n, (half, d, g) in enumerate(groups):
            tile = 2 * d + g
            o_ref[half, rows, tile * LRU_HALF:(tile + 1) * LRU_HALF] = strips[n * LRU_BLOCK_W:(n + 1) * LRU_BLOCK_W]


def _block_diag_gate_weights(w_r, w_i):
    spec = pl.BlockSpec((None, 2, LRU_BLOCKS, LRU_BLOCK_W, LRU_BLOCK_W), lambda l: (l, 0, 0, 0, 0))
    return pl.pallas_call(
        _gate_weights_kernel,
        grid=(DEPTH,),
        in_specs=[spec, spec],
        out_specs=pl.BlockSpec((None, 2, LRU_HALF, 4 * LRU_HALF), lambda l: (l, 0, 0, 0)),
        out_shape=jax.ShapeDtypeStruct((DEPTH, 2, LRU_HALF, 4 * LRU_HALF), BF16),
        compiler_params=_params(1),
        name="gate_weights",
    )(w_r, w_i)


PAIR_W = 2 * NA_HEAD_DIM
LOG2E = float(np.log2(np.e))


def _head_lanes(shape, head):
    lane = lax.broadcasted_iota(jnp.int32, shape, 1)
    return (lane < NA_HEAD_DIM) if head == 0 else (lane >= NA_HEAD_DIM)


def _one_head(q_pair, head):
    return jnp.where(_head_lanes(q_pair.shape, head), q_pair.astype(F32), 0.0).astype(BF16)


def _join_heads(o0, o1):
    return jnp.where(_head_lanes(o0.shape, 0), o0, o1)


CTX_SEQS = 4


def _ctx_attn_kernel(q_ref, k_ref, v_ref, o_ref):
    for seq in range(CTX_SEQS):
        rows = slice(seq * SEQ, (seq + 1) * SEQ)
        for j in range(NA_HEADS // 2):
            cols = slice(j * PAIR_W, (j + 1) * PAIR_W)
            outs = []
            for head in range(2):
                s = _bdot_t(_one_head(q_ref[rows, cols], head), k_ref[rows, cols])
                p = jnp.exp2(s - jnp.max(s, axis=-1, keepdims=True))
                denom = jnp.sum(p, axis=-1, keepdims=True)
                outs.append(jnp.dot(p.astype(BF16), v_ref[rows, cols], preferred_element_type=F32) / denom)
            o_ref[rows, cols] = _join_heads(*outs).astype(BF16)


def context_attention(q, kb, vb):
    spec = pl.BlockSpec((CTX_SEQS * SEQ, NA_WIDTH), lambda i: (i, 0))
    return pl.pallas_call(
        _ctx_attn_kernel,
        grid=(BATCH // CTX_SEQS,),
        in_specs=[spec, spec, spec],
        out_specs=spec,
        out_shape=jax.ShapeDtypeStruct((N_CTX, NA_WIDTH), BF16),
        compiler_params=_params(1),
        name="context_attention",
    )(q, kb, vb)


NA_QROWS = 4
NA_TQ = NA_QROWS * GRID_W
NA_KROWS = NA_QROWS + WIN_ROWS
NA_TK = NA_KROWS * GRID_W
NA_QBLOCKS = GRID_ROWS // NA_QROWS


N_DROW = 2 * WIN_ROWS - 1
N_DCOL = 2 * WIN_COLS - 1
NA_BLOCK_KINDS = (0, 1, NA_QBLOCKS - 1)


def _first_key_row(qb):
    return (np.clip if isinstance(qb, int) else jnp.clip)(qb * NA_QROWS - WIN_ROWS // 2, 0, GRID_ROWS - NA_KROWS)


def _build_bias(layer, rpb_ref, table_ref, bias_ref):
    qc = lax.broadcasted_iota(jnp.int32, (GRID_W, GRID_W), 0)
    kc = lax.broadcasted_iota(jnp.int32, (GRID_W, GRID_W), 1)
    col0 = jnp.clip(qc - WIN_COLS // 2, 0, GRID_W - WIN_COLS)
    col_ok = jnp.logical_and(kc >= col0, kc < col0 + WIN_COLS)
    d_col = jnp.clip(kc - qc, 1 - WIN_COLS, WIN_COLS - 1) + WIN_COLS - 1
    neg = jnp.full((GRID_W, GRID_W), NEG_INF, F32)

    def table_entry(idx, carry):
        t = neg
        for j in range(N_DCOL):
            t = jnp.where(d_col == j, rpb_ref[layer, idx * N_DCOL + j], t)
        table_ref[idx] = jnp.where(col_ok, t * LOG2E, neg)
        return carry

    lax.fori_loop(0, NA_HEADS * N_DROW, table_entry, 0)

    def head_blocks(h, carry):
        for kind, qb in enumerate(NA_BLOCK_KINDS):
            for i in range(NA_QROWS):
                qrow = qb * NA_QROWS + i
                win0 = int(np.clip(qrow - WIN_ROWS // 2, 0, GRID_ROWS - WIN_ROWS))
                for kr in range(NA_KROWS):
                    krow = int(_first_key_row(qb)) + kr
                    inside = win0 <= krow < win0 + WIN_ROWS
                    blk = table_ref[h * N_DROW + (krow - qrow + WIN_ROWS - 1)] if inside else neg
                    bias_ref[kind, h, i * GRID_W:(i + 1) * GRID_W, kr * GRID_W:(kr + 1) * GRID_W] = blk
        return carry

    lax.fori_loop(0, NA_HEADS, head_blocks, 0)


def _lat_attn_kernel(layer, rpb_ref, q_ref, k_ref, v_ref, ck_ref, cv_ref, o_ref, table_ref, bias_ref,
                     kt_ref, vt2_ref, vm_ref):
    qb = pl.program_id(1)

    @pl.when(jnp.logical_and(pl.program_id(0) == 0, qb == 0))
    def _():
        _build_bias(layer, rpb_ref, table_ref, bias_ref)

    @pl.when(qb == 0)
    def _():
        kt_ref[...] = ck_ref[...].astype(BF16)
        vt = cv_ref[...].astype(BF16)
        feat = lax.broadcasted_iota(jnp.int32, vt.shape, 0)
        first = jnp.bitwise_and(feat, PAIR_W - 1) < NA_HEAD_DIM
        vt2_ref[:, 0:PAST_LEN] = jnp.where(first, vt, jnp.zeros_like(vt))
        vt2_ref[:, PAST_LEN:] = jnp.where(first, jnp.zeros_like(vt), vt)

        def mask_rows(c, carry):
            rows = pl.ds(pl.multiple_of(c * NA_TQ, NA_TQ), NA_TQ)
            v = v_ref[rows, :]
            lane = lax.broadcasted_iota(jnp.int32, v.shape, 1)
            first = jnp.bitwise_and(lane, PAIR_W - 1) < NA_HEAD_DIM
            vm_ref[0, rows, :] = jnp.where(first, v, jnp.zeros_like(v))
            vm_ref[1, rows, :] = jnp.where(first, jnp.zeros_like(v), v)
            return carry

        lax.fori_loop(0, DEC_SEQ // NA_TQ, mask_rows, 0)

    kind = jnp.where(qb == 0, 0, jnp.where(qb == NA_QBLOCKS - 1, 2, 1))
    k0 = pl.multiple_of(_first_key_row(qb) * GRID_W, GRID_W)
    for j in range(NA_HEADS // 2):
        cols = slice(j * PAIR_W, (j + 1) * PAIR_W)
        k_loc = k_ref[pl.ds(k0, NA_TK), cols]
        kt_ctx = kt_ref[cols, :]
        ps_loc, ps_ctx, denoms = [], [], []
        for head in range(2):
            qh = _one_head(q_ref[:, cols], head)
            s_loc = _bdot_t(qh, k_loc) + bias_ref[kind, 2 * j + head]
            s_ctx = jnp.dot(qh, kt_ctx, preferred_element_type=F32)
            m = jnp.maximum(jnp.max(s_loc, axis=-1, keepdims=True), jnp.max(s_ctx, axis=-1, keepdims=True))
            p_loc = jnp.exp2(s_loc - m)
            p_ctx = jnp.exp2(s_ctx - m)
            denoms.append(jnp.sum(p_loc, axis=-1, keepdims=True) + jnp.sum(p_ctx, axis=-1, keepdims=True))
            ps_loc.append(p_loc.astype(BF16))
            ps_ctx.append(p_ctx.astype(BF16))
        vs_loc = [vm_ref[head, pl.ds(k0, NA_TK), cols] for head in range(2)]
        o = (jnp.dot(jnp.concatenate(ps_loc, axis=1), jnp.concatenate(vs_loc, axis=0), preferred_element_type=F32)
             + _bdot_t(jnp.concatenate(ps_ctx, axis=1), vt2_ref[cols, :]))
        o_ref[:, cols] = (o / _join_heads(*[jnp.broadcast_to(d, o.shape) for d in denoms])).astype(BF16)


def latent_attention(q, kb, vb, cache_k, cache_v, rpb, layer):
    q_blk0 = N_CTX // NA_TQ
    s_blk0 = N_CTX // DEC_SEQ
    cache_spec = pl.BlockSpec((None, None, NA_WIDTH, PAST_LEN), lambda b, m: (b, layer, 0, 0))
    return pl.pallas_call(
        functools.partial(_lat_attn_kernel, layer),
        grid=(DEC_BATCH, NA_QBLOCKS),
        in_specs=[
            pl.BlockSpec(memory_space=pltpu.SMEM),
            pl.BlockSpec((NA_TQ, NA_WIDTH), lambda b, m: (q_blk0 + b * NA_QBLOCKS + m, 0)),
            pl.BlockSpec((DEC_SEQ, NA_WIDTH), lambda b, m: (s_blk0 + b, 0)),
            pl.BlockSpec((DEC_SEQ, NA_WIDTH), lambda b, m: (s_blk0 + b, 0)),
            cache_spec,
            cache_spec,
        ],
        out_specs=pl.BlockSpec((NA_TQ, NA_WIDTH), lambda b, m: (b * NA_QBLOCKS + m, 0)),
        out_shape=jax.ShapeDtypeStruct((N_LAT, NA_WIDTH), BF16),
        scratch_shapes=[pltpu.VMEM((NA_HEADS * N_DROW, GRID_W, GRID_W), F32),
                        pltpu.VMEM((len(NA_BLOCK_KINDS), NA_HEADS, NA_TQ, NA_TK), F32),
                        pltpu.VMEM((NA_WIDTH, PAST_LEN), BF16),
                        pltpu.VMEM((NA_WIDTH, 2 * PAST_LEN), BF16),
                        pltpu.VMEM((2, DEC_SEQ, NA_WIDTH), BF16)],
        compiler_params=_params(2),
        name="latent_attention",
    )(rpb, q, kb, vb, cache_k, cache_v)


MERGE_TM = 512


MERGE_CTX_TILES = N_CTX // MERGE_TM
BRANCH_W = 512


def _merge_kernel(layer, n_x, *refs):
    x_refs, rest = refs[:n_x], refs[n_x:]
    (mod_ref, ya_ref, yb_ctx_ref, yb_lat_ref, yc_ctx_ref, yc_lat_ref, g_ref, wa_hbm, wb_hbm, wc_hbm, wo_hbm,
     ffn_gain_ref, wrt_ref, brt_ref, tri_ref,
     o_ref, hx_ref, bucket_ref, rank_ref, cnt_out_ref,
     wbr_ref, wo_ref, stage_ref, sem_ref, cnt_ref, cpad_ref) = rest
    i = pl.program_id(0)

    @pl.when(i == 0)
    def _():
        def store_branch(k):
            def st(v):
                wbr_ref[k] = v
            return st

        def store_out(k):
            def st(v):
                wo_ref[k * BRANCH_W:(k + 1) * BRANCH_W, :] = v
            return st

        chunks = [(w.at[layer], store_branch(k)) for k, w in enumerate((wa_hbm, wb_hbm, wc_hbm))]
        chunks += [(wo_hbm.at[layer, pl.ds(k * BRANCH_W, BRANCH_W), :], store_out(k))
                   for k in range(D_MODEL // BRANCH_W)]
        _load_cast(chunks, stage_ref, sem_ref)

    g = g_ref[...].astype(F32)
    yb = _token_tile(i, MERGE_CTX_TILES, (yb_ctx_ref, yb_lat_ref))
    yc = _token_tile(i, MERGE_CTX_TILES, (yc_ctx_ref, yc_lat_ref))
    merged = (g[:, 0:D_MODEL] * jnp.dot(ya_ref[...], wbr_ref[0], preferred_element_type=F32)
              + g[:, D_MODEL:2 * D_MODEL] * jnp.dot(yb, wbr_ref[1], preferred_element_type=F32)
              + g[:, 2 * D_MODEL:] * jnp.dot(yc, wbr_ref[2], preferred_element_type=F32))
    y = jnp.dot(merged.astype(BF16), wo_ref[...], preferred_element_type=F32)
    gate = mod_ref[:, 2 * D_MODEL:3 * D_MODEL]
    x = _token_tile(i, MERGE_CTX_TILES, x_refs) + gate * y
    o_ref[...] = x
    _route(x, mod_ref, ffn_gain_ref.at[pl.ds(layer, 1)], wrt_ref, brt_ref, tri_ref, hx_ref, bucket_ref, rank_ref,
           cnt_out_ref, cnt_ref, cpad_ref)


def merge_branches(xs, mod3, ya, yb_ctx, yb_lat, yc_ctx, yc_lat, gates, wa, wb, wc, wo, ffn_gain, w_rt, b_rt, layer):
    assert MERGE_TM == RT_TM
    row = lambda i: (i, 0)
    const = lambda shape: pl.BlockSpec(shape, lambda i: (0,) * len(shape))
    tri = jnp.asarray(np.triu(np.ones((RT_TM, RT_TM), np.float32)), BF16)
    hbm = pl.BlockSpec(memory_space=pl.ANY)
    x_specs = ([pl.BlockSpec((MERGE_TM, D_MODEL), row)] if len(xs) == 1
               else _split_specs((MERGE_TM, D_MODEL), MERGE_CTX_TILES))
    return pl.pallas_call(
        functools.partial(_merge_kernel, layer, len(xs)),
        grid=(N_TOK // MERGE_TM,),
        in_specs=x_specs + [
            _mod_spec(layer, MERGE_TM),
            pl.BlockSpec((MERGE_TM, SGU_WIDTH), row),
            *_split_specs((MERGE_TM, LRU_WIDTH), MERGE_CTX_TILES),
            *_split_specs((MERGE_TM, NA_WIDTH), MERGE_CTX_TILES),
            pl.BlockSpec((MERGE_TM, 3 * D_MODEL), row),
            hbm, hbm, hbm, hbm,
            _stacked_rows_spec(D_MODEL),
            _layer_spec((RT_ROWS, D_MODEL), layer),
            _layer_spec((RT_ROWS, 1), layer),
            const((RT_TM, RT_TM)),
        ],
        out_specs=[
            pl.BlockSpec((MERGE_TM, D_MODEL), row),
            pl.BlockSpec((RT_TM, HX_W), row),
            pl.BlockSpec((1, RT_TM), lambda i: (0, i)),
            pl.BlockSpec((1, RT_TM), lambda i: (0, i)),
            const((CNT_ROWS, COMB_W)),
        ],
        out_shape=[
            jax.ShapeDtypeStruct((N_TOK, D_MODEL), F32),
            jax.ShapeDtypeStruct((N_TOK, HX_W), F32),
            jax.ShapeDtypeStruct((1, N_TOK), jnp.int32),
            jax.ShapeDtypeStruct((1, N_TOK), jnp.int32),
            jax.ShapeDtypeStruct((CNT_ROWS, COMB_W), jnp.int32),
        ],
        scratch_shapes=[pltpu.VMEM((3, BRANCH_W, D_MODEL), BF16), pltpu.VMEM((D_MODEL, D_MODEL), BF16),
                        pltpu.VMEM((2, BRANCH_W, D_MODEL), F32), pltpu.SemaphoreType.DMA((2,)),
                        pltpu.VMEM((CNT_ROWS, COMB_W), F32), pltpu.VMEM((COMB_W, RT_TM), F32)],
        compiler_params=_params(1),
        name="merge_branches",
    )(*xs, mod3, ya, yb_ctx, yb_lat, yc_ctx, yc_lat, gates, wa, wb, wc, wo, ffn_gain, w_rt, b_rt, tri)


RT_TM = 512
RT_ROWS = 32
RT_EXPERT_ROW = 8
EXPERT_PAIRS = ((0, 1), (0, 2), (0, 3), (1, 2), (1, 3), (2, 3))
N_PAIRS = len(EXPERT_PAIRS)
N_BUCKETS = N_GROUPS * N_PAIRS
CNT_ROWS = 32
COMB_W = 128
HX_W = D_MODEL + COMB_W
EXP_TM = 256
EXP_TILES = N_TOK // EXP_TM + N_BUCKETS
HS_ROWS = EXP_TILES * EXP_TM


def _split_bf16(x):
    hi = x.astype(BF16)
    return hi, (x - hi.astype(F32)).astype(BF16)


def _route(x, mod_ref, gain_ref, w_ref, b_ref, tri_ref, hx_ref, bucket_ref, rank_ref, cnt_out_ref,
           cnt_ref, cpad_ref):
    @pl.when(pl.program_id(0) == 0)
    def _():
        cnt_ref[...] = jnp.zeros_like(cnt_ref)
        cpad_ref[...] = jnp.zeros_like(cpad_ref)

    m = mod_ref[...]
    shift, scale = m[:, 3 * D_MODEL:4 * D_MODEL], m[:, 4 * D_MODEL:5 * D_MODEL]
    h = _rms(x, gain_ref[...]) * (1.0 + scale) + shift
    hx_ref[:, 0:D_MODEL] = h
    h_hi, h_lo = _split_bf16(h)
    w_hi, w_lo = _split_bf16(w_ref[...])
    dims = (((1,), (1,)), ((), ()))
    logits = (lax.dot_general(w_hi, h_hi, dims, preferred_element_type=F32)
              + lax.dot_general(w_hi, h_lo, dims, preferred_element_type=F32)
              + lax.dot_general(w_lo, h_hi, dims, preferred_element_type=F32)) + b_ref[...]
    gl = [logits[g:g + 1, :] for g in range(N_GROUPS)]
    gmax = functools.reduce(jnp.maximum, gl)
    gid = jnp.full(gmax.shape, N_GROUPS - 1, jnp.int32)
    for g in reversed(range(N_GROUPS - 1)):
        gid = jnp.where(gl[g] == gmax, g, gid)
    p_grp = 1.0 / functools.reduce(jnp.add, [jnp.exp(v - gmax) for v in gl])
    el = []
    for e in range(EXPERTS_PER_GROUP):
        v = logits[RT_EXPERT_ROW + e:RT_EXPERT_ROW + e + 1, :]
        for g in range(1, N_GROUPS):
            row = RT_EXPERT_ROW + g * EXPERTS_PER_GROUP + e
            v = jnp.where(gid == g, logits[row:row + 1, :], v)
        el.append(v)
    top1 = functools.reduce(jnp.maximum, el)
    idx1 = jnp.full(top1.shape, EXPERTS_PER_GROUP - 1, jnp.int32)
    for e in reversed(range(EXPERTS_PER_GROUP - 1)):
        idx1 = jnp.where(el[e] == top1, e, idx1)
    rest = [jnp.where(idx1 == e, -jnp.inf, el[e]) for e in range(EXPERTS_PER_GROUP)]
    top2 = functools.reduce(jnp.maximum, rest)
    idx2 = jnp.full(top1.shape, EXPERTS_PER_GROUP - 1, jnp.int32)
    for e in reversed(range(EXPERTS_PER_GROUP - 1)):
        idx2 = jnp.where(rest[e] == top2, e, idx2)
    e2 = jnp.exp(top2 - top1)
    w1 = p_grp / (1.0 + e2)
    w2 = p_grp * e2 / (1.0 + e2)
    for e in range(EXPERTS_PER_GROUP):
        cpad_ref[e:e + 1, :] = jnp.where(idx1 == e, w1, 0.0) + jnp.where(idx2 == e, w2, 0.0)
    hx_ref[:, D_MODEL:] = cpad_ref[...].T
    lo = jnp.minimum(idx1, idx2)
    hi = jnp.maximum(idx1, idx2)
    pair = jnp.where(lo == 0, 0, jnp.where(lo == 1, 3, 5)) + (hi - lo - 1)
    bucket = gid * N_PAIRS + pair
    bucket_ref[...] = bucket
    sub = lax.broadcasted_iota(jnp.int32, (CNT_ROWS, RT_TM), 0)
    onehot = jnp.where(sub == bucket, 1.0, 0.0)
    seen = jnp.dot(onehot.astype(BF16), tri_ref[...], preferred_element_type=F32)
    cnt = cnt_ref[...]
    rank_ref[...] = jnp.sum(onehot * (seen - 1.0 + cnt[:, 0:1]), axis=0, keepdims=True).astype(jnp.int32)
    cnt = cnt + jnp.sum(onehot, axis=1, keepdims=True)
    cnt_ref[...] = cnt
    cnt_out_ref[...] = cnt.astype(jnp.int32)


def _router_weights(w_grp, b_grp, w_exp, b_exp):
    pad = lambda v, n: jnp.zeros(v.shape[:1] + (n,) + v.shape[2:], F32)
    gap, tail = RT_EXPERT_ROW - N_GROUPS, RT_ROWS - RT_EXPERT_ROW - N_EXPERTS
    w_grp, w_exp = jnp.swapaxes(w_grp, 1, 2), jnp.swapaxes(w_exp, 1, 2)
    w = jnp.concatenate([w_grp, pad(w_grp, gap), w_exp, pad(w_exp, tail)], axis=1)
    b_grp, b_exp = b_grp[:, :, None], b_exp[:, :, None]
    b = jnp.concatenate([b_grp, pad(b_grp, gap), b_exp, pad(b_exp, tail)], axis=1)
    return w, b


DISP_TM = 512


ROW_GROUP = 64


def _for_each_row(n_rows, fn):
    def group(k, carry):
        g0 = pl.multiple_of(k * ROW_GROUP, ROW_GROUP)
        for u in range(ROW_GROUP):
            fn(g0, u)
        return carry

    lax.fori_loop(0, n_rows // ROW_GROUP, group, 0)


def _dispatch_kernel(pos_ref, start_ref, cnt_ref, nt_ref, hx_hbm, hs_ref, buf_ref, zero_ref, blk_sem, row_sem, zsem):
    i = pl.program_id(0)
    last = pl.num_programs(0) - 1
    slot = i % 2
    base = i * DISP_TM

    def tile_in(tile, s):
        return pltpu.make_async_copy(hx_hbm.at[pl.ds(pl.multiple_of(tile * DISP_TM, DISP_TM), DISP_TM), :],
                                     buf_ref.at[s], blk_sem.at[s])

    def wait_rows(s):
        pltpu.make_async_copy(buf_ref.at[s], hs_ref.at[pl.ds(0, DISP_TM), :], row_sem.at[s]).wait()

    @pl.when(i == 0)
    def _():
        zero_ref[...] = jnp.zeros_like(zero_ref)

        def tile_copy(t):
            return pltpu.make_async_copy(zero_ref, hs_ref.at[pl.ds(pl.multiple_of(t * EXP_TM, EXP_TM), EXP_TM), :],
                                         zsem)

        def last_tile(g):
            return (start_ref[g] + cnt_ref[g, 0] - 1) // EXP_TM

        def each_tile(fn):
            for g in range(N_BUCKETS):
                pl.when(cnt_ref[g, 0] > 0)(functools.partial(fn, last_tile(g)))
            lax.fori_loop(nt_ref[0], EXP_TILES, lambda t, carry: (fn(t), carry)[1], 0)

        each_tile(lambda t: tile_copy(t).start())
        each_tile(lambda t: tile_copy(t).wait())
        tile_in(0, 0).start()

    @pl.when(i > 0)
    def _():
        wait_rows(1 - slot)

    @pl.when(i < last)
    def _():
        tile_in(i + 1, 1 - slot).start()

    tile_in(i, slot).wait()

    def issue(g0, u):
        src = buf_ref.at[slot, pl.ds(g0, ROW_GROUP), :]
        pltpu.make_async_copy(src.at[pl.ds(u, 1), :], hs_ref.at[pl.ds(pos_ref[base + g0 + u], 1), :],
                              row_sem.at[slot]).start(priority=u % 2)

    _for_each_row(DISP_TM, issue)

    @pl.when(i == last)
    def _():
        wait_rows(slot)


def dispatch(pos, starts, counts, n_tiles, hx):
    return pl.pallas_call(
        _dispatch_kernel,
        grid_spec=pltpu.PrefetchScalarGridSpec(
            num_scalar_prefetch=4,
            grid=(N_TOK // DISP_TM,),
            in_specs=[pl.BlockSpec(memory_space=pl.ANY)],
            out_specs=pl.BlockSpec(memory_space=pl.ANY),
            scratch_shapes=[pltpu.VMEM((2, DISP_TM, HX_W), F32), pltpu.VMEM((EXP_TM, HX_W), F32),
                            pltpu.SemaphoreType.DMA((2,)), pltpu.SemaphoreType.DMA((2,)),
                            pltpu.SemaphoreType.DMA(())],
        ),
        out_shape=jax.ShapeDtypeStruct((HS_ROWS, HX_W), F32),
        compiler_params=_params(1),
        name="dispatch",
    )(pos, starts, counts, n_tiles, hx)


(T_GROUP, T_LO, T_HI, T_SLOT, T_RUN_POS, T_NEXT_GROUP, T_FIRST, T_PREFETCHED,
 T_PENDING_GROUP, T_PENDING_EXPERT, T_PENDING_SLOT) = range(11)


def _experts_kernel(layer, tab_ref, nt_ref, hs_ref, w1_hbm, w3_hbm, w2_hbm, ys_ref,
                    w1_ref, w3_ref, w2_ref, st1_ref, st3_ref, st2_ref, sem_ref):
    t = pl.program_id(0)
    valid = t < nt_ref[0]
    group = tab_ref[T_GROUP, t]
    slot = tab_ref[T_SLOT, t]
    run_pos = tab_ref[T_RUN_POS, t]
    next_group = tab_ref[T_NEXT_GROUP, t]
    pending_expert = tab_ref[T_PENDING_EXPERT, t]

    def expert_copies(grp, e):
        idx = grp * EXPERTS_PER_GROUP + e
        return (pltpu.make_async_copy(w1_hbm.at[layer, idx], st1_ref, sem_ref.at[0]),
                pltpu.make_async_copy(w3_hbm.at[layer, idx], st3_ref, sem_ref.at[1]),
                pltpu.make_async_copy(w2_hbm.at[layer, idx], st2_ref, sem_ref.at[2]))

    def finish(copies, dst_slot, e):
        for cp in copies:
            cp.wait()
        w1_ref[dst_slot, e] = st1_ref[...].astype(BF16)
        w3_ref[dst_slot, e] = st3_ref[...].astype(BF16)
        w2_ref[dst_slot, e] = st2_ref[...].astype(BF16)

    @pl.when(jnp.logical_and(valid, pending_expert >= 0))
    def _():
        finish(expert_copies(tab_ref[T_PENDING_GROUP, t], pending_expert), tab_ref[T_PENDING_SLOT, t], pending_expert)

    @pl.when(jnp.logical_and(valid, tab_ref[T_FIRST, t] == 1))
    def _():
        def load(e, carry):
            copies = expert_copies(group, e)
            for cp in copies:
                cp.start()
            finish(copies, slot, e)
            return carry

        lax.fori_loop(tab_ref[T_PREFETCHED, t], EXPERTS_PER_GROUP, load, 0)

    prefetch = jnp.logical_and(valid, jnp.logical_and(next_group >= 0, run_pos < EXPERTS_PER_GROUP))

    @pl.when(prefetch)
    def _():
        for cp in expert_copies(next_group, run_pos):
            cp.start()

    @pl.when(valid)
    def _():
        h = hs_ref[:, 0:D_MODEL].astype(BF16)
        c = hs_ref[:, D_MODEL:]
        lane = lax.broadcasted_iota(jnp.int32, c.shape, 1)
        acc = None
        for e in (tab_ref[T_LO, t], tab_ref[T_HI, t]):
            ce = jnp.sum(jnp.where(lane == e, c, 0.0), axis=1, keepdims=True)
            a = jnp.dot(h, w1_ref[slot, e], preferred_element_type=F32)
            b = jnp.dot(h, w3_ref[slot, e], preferred_element_type=F32)
            hid = (a * _sigmoid(a)) * b * ce
            y = jnp.dot(hid.astype(BF16), w2_ref[slot, e], preferred_element_type=F32)
            acc = y if acc is None else acc + y
        ys_ref[...] = acc

    @pl.when(jnp.logical_not(valid))
    def _():
        ys_ref[...] = jnp.zeros_like(ys_ref)


def experts(tile_table, n_tiles, hs, w1, w3, w2, layer):
    hbm = pl.BlockSpec(memory_space=pl.ANY)
    return pl.pallas_call(
        functools.partial(_experts_kernel, layer),
        grid_spec=pltpu.PrefetchScalarGridSpec(
            num_scalar_prefetch=2,
            grid=(EXP_TILES,),
            in_specs=[pl.BlockSpec((EXP_TM, HX_W), lambda t, *_: (t, 0)), hbm, hbm, hbm],
            out_specs=pl.BlockSpec((EXP_TM, D_MODEL), lambda t, *_: (t, 0)),
            scratch_shapes=[
                pltpu.VMEM((2, EXPERTS_PER_GROUP, D_MODEL, D_EXPERT), BF16),
                pltpu.VMEM((2, EXPERTS_PER_GROUP, D_MODEL, D_EXPERT), BF16),
                pltpu.VMEM((2, EXPERTS_PER_GROUP, D_EXPERT, D_MODEL), BF16),
                pltpu.VMEM((D_MODEL, D_EXPERT), F32),
                pltpu.VMEM((D_MODEL, D_EXPERT), F32),
                pltpu.VMEM((D_EXPERT, D_MODEL), F32),
                pltpu.SemaphoreType.DMA((3,)),
            ],
        ),
        out_shape=jax.ShapeDtypeStruct((HS_ROWS, D_MODEL), F32),
        compiler_params=_params(1),
        name="experts",
    )(tile_table, n_tiles, hs, w1, w3, w2)


N_TABLE_ROWS = T_PENDING_SLOT + 1
EXP_TM_LOG2 = EXP_TM.bit_length() - 1


def _schedule_kernel(cnt_ref, starts_ref, tab_ref, nt_ref):
    i32 = jnp.int32
    t = i32(0)
    for b in range(N_BUCKETS):
        n = lax.shift_right_logical(cnt_ref[b, 0] + (EXP_TM - 1), EXP_TM_LOG2)
        starts_ref[b] = t * EXP_TM

        def fill(j, carry, t=t, group=b // N_PAIRS, pair=EXPERT_PAIRS[b % N_PAIRS]):
            tab_ref[T_GROUP, t + j] = group
            tab_ref[T_LO, t + j] = pair[0]
            tab_ref[T_HI, t + j] = pair[1]
            return carry

        lax.fori_loop(0, n, fill, 0)
        t = t + n
    n_tiles = t
    nt_ref[0] = n_tiles

    def unused(k, carry):
        for row, value in ((T_GROUP, N_GROUPS - 1), (T_LO, 0), (T_HI, 1), (T_SLOT, 0), (T_RUN_POS, 0),
                           (T_NEXT_GROUP, -1), (T_FIRST, 0), (T_PREFETCHED, 0), (T_PENDING_GROUP, 0),
                           (T_PENDING_EXPERT, -1), (T_PENDING_SLOT, 0)):
            tab_ref[row, k] = value
        return carry

    lax.fori_loop(n_tiles, EXP_TILES, unused, 0)

    def back(k, carry):
        next_group, group_after = carry
        tile = n_tiles - 1 - k
        group = tab_ref[T_GROUP, tile]
        next_group = jnp.where(k == 0, -1, jnp.where(group != group_after, group_after, next_group))
        tab_ref[T_NEXT_GROUP, tile] = next_group
        return next_group, group

    lax.fori_loop(0, n_tiles, back, (i32(-1), i32(-1)))

    def forward(tile, carry):
        prev_group, run, run_start, prev_len, prev_fetch, prev_pos, prev_next, prev_slot = carry
        group = tab_ref[T_GROUP, tile]
        first = jnp.logical_or(tile == 0, group != prev_group)
        prev_len = jnp.where(first, tile - run_start, prev_len)
        run = jnp.where(first, run + 1, run)
        run_start = jnp.where(first, tile, run_start)
        slot = lax.rem(run, 2)
        run_pos = tile - run_start
        next_group = tab_ref[T_NEXT_GROUP, tile]
        tab_ref[T_FIRST, tile] = first.astype(i32)
        tab_ref[T_SLOT, tile] = slot
        tab_ref[T_RUN_POS, tile] = run_pos
        tab_ref[T_PREFETCHED, tile] = jnp.minimum(prev_len, EXPERTS_PER_GROUP)
        tab_ref[T_PENDING_EXPERT, tile] = jnp.where(prev_fetch == 1, prev_pos, -1)
        tab_ref[T_PENDING_GROUP, tile] = prev_next
        tab_ref[T_PENDING_SLOT, tile] = 1 - prev_slot
        fetch = jnp.logical_and(next_group >= 0, run_pos < EXPERTS_PER_GROUP).astype(i32)
        return group, run, run_start, prev_len, fetch, run_pos, next_group, slot

    lax.fori_loop(0, n_tiles, forward, (i32(-1), i32(-1), i32(0), i32(0), i32(0), i32(0), i32(0), i32(0)))


def expert_schedule(counts):
    smem = pl.BlockSpec(memory_space=pltpu.SMEM)
    return pl.pallas_call(
        _schedule_kernel,
        in_specs=[smem],
        out_specs=[smem, smem, smem],
        out_shape=[jax.ShapeDtypeStruct((N_BUCKETS,), jnp.int32),
                   jax.ShapeDtypeStruct((N_TABLE_ROWS, EXP_TILES), jnp.int32),
                   jax.ShapeDtypeStruct((1,), jnp.int32)],
        name="expert_schedule",
    )(counts)


RES_TM = 512
RES_CTX_TILES = N_CTX // RES_TM


def _moe_residual_kernel(final, pos_ref, x_ref, mod_ref, gain_ref, ys_ref, *refs):
    out_refs, (ybuf_ref, sem_ref) = refs[:-2], refs[-2:]
    i = pl.program_id(0)
    slot = i % 2

    def gather_tile(tile, tile_slot):
        def issue(g0, u):
            dst = ybuf_ref.at[tile_slot, pl.ds(g0, ROW_GROUP), :]
            pltpu.make_async_copy(ys_ref.at[pl.ds(pos_ref[tile * RES_TM + g0 + u], 1), :], dst.at[pl.ds(u, 1), :],
                                  sem_ref.at[tile_slot]).start(priority=u % 2)

        _for_each_row(RES_TM, issue)

    @pl.when(i == 0)
    def _():
        gather_tile(0, 0)

    @pl.when(i + 1 < pl.num_programs(0))
    def _():
        gather_tile(i + 1, 1 - slot)

    pltpu.make_async_copy(ys_ref.at[pl.ds(0, RES_TM), :], ybuf_ref.at[slot], sem_ref.at[slot]).wait()
    x = x_ref[...] + mod_ref[:, 5 * D_MODEL:6 * D_MODEL] * ybuf_ref[slot]
    if not final:
        out_refs[0][...] = x
        return
    y = _rms(x, gain_ref[...])

    @pl.when(i < RES_CTX_TILES)
    def _():
        out_refs[0][...] = y

    @pl.when(i >= RES_CTX_TILES)
    def _():
        out_refs[1][...] = y


def moe_residual(pos, x, mod4, gain, ys, layer, final):
    if final:
        out_specs = _split_specs((RES_TM, D_MODEL), RES_CTX_TILES)
        out_shape = [jax.ShapeDtypeStruct((N_CTX, D_MODEL), F32), jax.ShapeDtypeStruct((N_LAT, D_MODEL), F32)]
    else:
        out_specs = pl.BlockSpec((RES_TM, D_MODEL), lambda i, *_: (i, 0))
        out_shape = jax.ShapeDtypeStruct((N_TOK, D_MODEL), F32)
    return pl.pallas_call(
        functools.partial(_moe_residual_kernel, final),
        grid_spec=pltpu.PrefetchScalarGridSpec(
            num_scalar_prefetch=1,
            grid=(N_TOK // RES_TM,),
            in_specs=[
                pl.BlockSpec((RES_TM, D_MODEL), lambda i, *_: (i, 0)),
                _mod_spec(layer, RES_TM),
                pl.BlockSpec((1, D_MODEL), lambda i, *_: (0, 0)),
                pl.BlockSpec(memory_space=pl.ANY),
            ],
            out_specs=out_specs,
            scratch_shapes=[pltpu.VMEM((2, RES_TM, D_MODEL), F32), pltpu.SemaphoreType.DMA((2,))],
        ),
        out_shape=out_shape,
        compiler_params=_params(1),
        name="moe_residual_final" if final else "moe_residual",
    )(pos, x, mod4, gain, ys)


def kernel(x_prompt, x_sample, cache_k, cache_v, state_lru, c, c_ctx, w_mod, b_mod, norm_mix, norm_ffn, w_in, sgu_norm, sgu_w, sgu_b, lru_conv_w, lru_conv_b, lru_w_r, lru_b_r, lru_w_i, lru_b_i, lru_lambda, na_rpb, w_branch_sgu, w_branch_lru, w_branch_na, w_out, moe_w_group, moe_b_group, moe_w_expert, moe_b_expert, moe_w1, moe_w3, moe_w2, final_norm_gain):
    xs = (x_prompt.reshape(N_CTX, D_MODEL), x_sample.reshape(N_LAT, D_MODEL))
    cond = jnp.zeros((N_COND, D_MODEL), F32).at[0].set(c_ctx).at[1:1 + DEC_BATCH].set(c)
    mod = modulation(cond, w_mod, b_mod)
    zero_state = jnp.zeros((BATCH, 2, LRU_WIDTH), F32)
    feature_major = lambda t: jnp.transpose(t, (0, 1, 3, 4, 2)).reshape(DEC_BATCH, DEPTH, NA_WIDTH, PAST_LEN)
    cache_k, cache_v = feature_major(cache_k), feature_major(cache_v)
    final_gain = final_norm_gain.reshape(1, D_MODEL)
    sgu_b_t = jnp.swapaxes(sgu_b, 1, 2)
    w_gate = _block_diag_gate_weights(lru_w_r, lru_w_i)
    rpb = na_rpb.reshape(DEPTH, NA_HEADS * N_DROW * N_DCOL)
    w_rt, b_rt = _router_weights(moe_w_group, moe_b_group, moe_w_expert, moe_b_expert)
    caches, ss = [], []
    for l in range(DEPTH):
        ya, za, q, kb, vb, k_ctx, v_ctx, gates = in_projection(
            xs, mod, norm_mix, w_in, sgu_norm, sgu_w, sgu_b_t, l, prev_caches=caches if l == DEPTH - 1 else ())
        caches.append((k_ctx, v_ctx))
        lru_args = (lru_conv_w, lru_conv_b, w_gate, lru_b_r, lru_b_i, lru_lambda)
        yb_ctx, st_ctx = rglru(za, *lru_args, zero_state, pl.BlockSpec((None, 2, LRU_WIDTH), lambda i: (i, 0, 0)),
                               SEQ, BATCH, 0, l)
        yb_lat, _ = rglru(za, *lru_args, state_lru,
                          pl.BlockSpec((None, None, 2, LRU_WIDTH), lambda i, l=l: (i, l, 0, 0)),
                          DEC_SEQ, DEC_BATCH, N_CTX, l)
        yc_ctx = context_attention(q, kb, vb)
        yc_lat = latent_attention(q, kb, vb, cache_k, cache_v, rpb, l)
        x, hx, bucket, rank, counts = merge_branches(
            xs, mod, ya, yb_ctx, yb_lat, yc_ctx, yc_lat, gates, w_branch_sgu, w_branch_lru, w_branch_na, w_out,
            norm_ffn, w_rt, b_rt, l)
        bucket, rank = bucket.reshape(N_TOK), rank.reshape(N_TOK)
        starts, tile_table, n_tiles = expert_schedule(counts)
        pos = jnp.sum(jnp.where(bucket[:, None] == jnp.arange(N_BUCKETS), starts[None, :] + rank[:, None], 0), axis=1)
        hs = dispatch(pos, starts, counts, n_tiles, hx)
        ys = experts(tile_table, n_tiles, hs, moe_w1, moe_w3, moe_w2, l)
        if l < DEPTH - 1:
            xs = (moe_residual(pos, x, mod, final_gain, ys, l, False),)
        else:
            y_ctx, y_lat = moe_residual(pos, x, mod, final_gain, ys, l, True)
        ss.append(st_ctx)
    token_major = lambda t: jnp.transpose(t.reshape(BATCH, DEPTH, NA_HEADS, NA_HEAD_DIM, SEQ), (0, 1, 4, 2, 3))
    return (y_ctx.reshape(BATCH, SEQ, D_MODEL), y_lat.reshape(DEC_BATCH, DEC_SEQ, D_MODEL),
            token_major(k_ctx), token_major(v_ctx), jnp.stack(ss, axis=1))
```
